```python
import jax, jax.numpy as jnp
from jax import lax
import numpy as np

D_MODEL = 2048
BATCH = 8
SEQ = 2048
DEPTH = 2

N_MIXERS = 2
EXPAND = 2
BRANCH_WIDTH = EXPAND * D_MODEL
SC_WIDTH = 3
LRU_CONV_WIDTH = 4
LRU_HEAD_DIM = 256
LRU_HEADS = BRANCH_WIDTH // LRU_HEAD_DIM
RGLRU_C = 8.0
N_CONV_LAYERS = (DEPTH + 1) // N_MIXERS
N_LRU_LAYERS = DEPTH // N_MIXERS
EPS = 1e-6

kernel_name = "hybrid_shortconv_rglru_adaln"


def rmsnorm(x, g):
    xf = x.astype(jnp.float32)
    y = xf * lax.rsqrt(jnp.mean(xf * xf, axis=-1, keepdims=True) + EPS)
    return (y * g.astype(jnp.float32)).astype(x.dtype)


def adaln(h, c, w, b):
    mod = jnp.einsum('bd,df->bf', jax.nn.silu(c), w) + b
    shift, scale, gate = jnp.split(mod, 3, axis=-1)
    h = h * (1.0 + scale[:, None, :]) + shift[:, None, :]
    return h, gate


def causal_depthwise_conv(u, w):
    width, e = w.shape
    rhs = w[:, None, :].astype(u.dtype)
    return lax.conv_general_dilated(
        u, rhs, window_strides=(1,), padding=[(width - 1, 0)],
        dimension_numbers=('NWC', 'WIO', 'NWC'), feature_group_count=e)


def short_conv_mixer(h, w_in, conv_w, w_out):
    proj = jnp.einsum('bsd,de->bse', h, w_in)
    b_gate, c_gate, v, g = jnp.split(proj, 4, axis=-1)
    u = causal_depthwise_conv(c_gate * v, conv_w)
    y = b_gate * u * jax.nn.silu(g)
    return jnp.einsum('bse,ed->bsd', y, w_out)


def _linear_recurrence_combine(left, right):
    a_l, b_l = left
    a_r, b_r = right
    return a_l * a_r, a_r * b_l + b_r


def rglru_mixer(h, w_in, conv_w, conv_b, w_a, b_a, w_x, b_x, lam, w_out):
    proj = jnp.einsum('bsd,de->bse', h, w_in)
    v, g = jnp.split(proj, 2, axis=-1)
    v = causal_depthwise_conv(v, conv_w) + conv_b
    bsz, seq, e = v.shape
    n_heads, head_dim = w_a.shape[0], w_a.shape[1]
    vh = v.reshape(bsz, seq, n_heads, head_dim)
    r = jax.nn.sigmoid(jnp.einsum('bshi,hij->bshj', vh, w_a) + b_a).reshape(bsz, seq, e)
    i = jax.nn.sigmoid(jnp.einsum('bshi,hij->bshj', vh, w_x) + b_x).reshape(bsz, seq, e)
    log_a = -RGLRU_C * r.astype(jnp.float32) * jax.nn.softplus(-lam.astype(jnp.float32))
    a = jnp.exp(log_a)
    norm_mult = jnp.sqrt(-jnp.expm1(2.0 * log_a))
    b = norm_mult * (i * v).astype(jnp.float32)
    _, hs = lax.associative_scan(_linear_recurrence_combine, (a, b), axis=1)
    y = hs.astype(h.dtype) * jax.nn.silu(g)
    return jnp.einsum('bse,ed->bsd', y, w_out)


def _fwd_setup_inputs(seed: int = 0) -> dict:
    key = jax.random.key(seed)
    ks = jax.random.split(key, 24)
    D, E, H, Dh = D_MODEL, BRANCH_WIDTH, LRU_HEADS, LRU_HEAD_DIM
    nA, nB = N_CONV_LAYERS, N_LRU_LAYERS
    f32 = jnp.float32
    n = lambda k, shape, s: jax.random.normal(k, shape, f32) * s
    x = jax.random.normal(ks[0], (BATCH, SEQ, D), f32)
    c = jax.random.normal(ks[1], (BATCH, D), f32)
    norm_g = 1.0 + n(ks[2], (DEPTH, D), 0.02)
    ada_w = n(ks[3], (DEPTH, D, 3 * D), 0.5 * D ** -0.5)
    ada_b = n(ks[4], (DEPTH, 3 * D), 0.02)
    sc_w_in = n(ks[5], (nA, D, 4 * E), D ** -0.5)
    sc_conv_w = n(ks[6], (nA, SC_WIDTH, E), SC_WIDTH ** -0.5)
    sc_w_out = n(ks[7], (nA, E, D), E ** -0.5)
    lru_w_in = n(ks[8], (nB, D, 2 * E), D ** -0.5)
    lru_conv_w = n(ks[9], (nB, LRU_CONV_WIDTH, E), LRU_CONV_WIDTH ** -0.5)
    lru_conv_b = n(ks[10], (nB, E), 0.02)
    lru_w_a = n(ks[11], (nB, H, Dh, Dh), Dh ** -0.5)
    lru_b_a = n(ks[12], (nB, H, Dh), 0.02)
    lru_w_x = n(ks[13], (nB, H, Dh, Dh), Dh ** -0.5)
    lru_b_x = n(ks[14], (nB, H, Dh), 0.02)
    a_pow_c = jax.random.uniform(ks[15], (nB, E), f32, minval=0.9, maxval=0.999)
    a0 = a_pow_c ** (1.0 / RGLRU_C)
    lru_lambda = jnp.log(a0) - jnp.log1p(-a0)
    lru_w_out = n(ks[16], (nB, E, D), E ** -0.5)
    final_g = 1.0 + n(ks[17], (D,), 0.02)
    return {
        "x": x, "c": c, "norm_g": norm_g, "ada_w": ada_w, "ada_b": ada_b,
        "sc_w_in": sc_w_in, "sc_conv_w": sc_conv_w, "sc_w_out": sc_w_out,
        "lru_w_in": lru_w_in, "lru_conv_w": lru_conv_w, "lru_conv_b": lru_conv_b,
        "lru_w_a": lru_w_a, "lru_b_a": lru_b_a, "lru_w_x": lru_w_x, "lru_b_x": lru_b_x,
        "lru_lambda": lru_lambda, "lru_w_out": lru_w_out, "final_g": final_g,
    }


def _fwd_reference(x, c, norm_g, ada_w, ada_b, sc_w_in, sc_conv_w, sc_w_out,
              lru_w_in, lru_conv_w, lru_conv_b, lru_w_a, lru_b_a, lru_w_x, lru_b_x,
              lru_lambda, lru_w_out, final_g):
    for layer in range(DEPTH):
        j = layer // N_MIXERS
        h = rmsnorm(x, norm_g[layer])
        h, gate = adaln(h, c, ada_w[layer], ada_b[layer])
        if layer % N_MIXERS == 0:
            y = short_conv_mixer(h, sc_w_in[j], sc_conv_w[j], sc_w_out[j])
        else:
            y = rglru_mixer(h, lru_w_in[j], lru_conv_w[j], lru_conv_b[j],
                            lru_w_a[j], lru_b_a[j], lru_w_x[j], lru_b_x[j],
                            lru_lambda[j], lru_w_out[j])
        x = x + gate[:, None, :] * y
    return rmsnorm(x, final_g)


import jax as _jax
import jax.numpy as _jnp

TWIN_FORMAT = 'train_step'
FWD_PARAMS = ['x', 'c', 'norm_g', 'ada_w', 'ada_b', 'sc_w_in', 'sc_conv_w', 'sc_w_out', 'lru_w_in', 'lru_conv_w', 'lru_conv_b', 'lru_w_a', 'lru_b_a', 'lru_w_x', 'lru_b_x', 'lru_lambda', 'lru_w_out', 'final_g']
TWIN_WEIGHTS = ['norm_g', 'ada_w', 'ada_b', 'sc_w_in', 'sc_conv_w', 'sc_w_out', 'lru_w_in', 'lru_conv_w', 'lru_conv_b', 'lru_w_a', 'lru_b_a', 'lru_w_x', 'lru_b_x', 'lru_lambda', 'lru_w_out', 'final_g']
TWIN_DIFF_INPUT = 'x'
TWIN_INPUTS = ['x', 'c', 'norm_g', 'ada_w', 'ada_b', 'sc_w_in', 'sc_conv_w', 'sc_w_out', 'lru_w_in', 'lru_conv_w', 'lru_conv_b', 'lru_w_a', 'lru_b_a', 'lru_w_x', 'lru_b_x', 'lru_lambda', 'lru_w_out', 'final_g', 'loss_target', 'm_norm_g', 'm_ada_w', 'm_ada_b', 'm_sc_w_in', 'm_sc_conv_w', 'm_sc_w_out', 'm_lru_w_in', 'm_lru_conv_w', 'm_lru_conv_b', 'm_lru_w_a', 'm_lru_b_a', 'm_lru_w_x', 'm_lru_b_x', 'm_lru_lambda', 'm_lru_w_out', 'm_final_g', 'v_norm_g', 'v_ada_w', 'v_ada_b', 'v_sc_w_in', 'v_sc_conv_w', 'v_sc_w_out', 'v_lru_w_in', 'v_lru_conv_w', 'v_lru_conv_b', 'v_lru_w_a', 'v_lru_b_a', 'v_lru_w_x', 'v_lru_b_x', 'v_lru_lambda', 'v_lru_w_out', 'v_final_g']
TWIN_OUTPUTS = ['loss', 'grad_x', 'grad_norm_g', 'grad_ada_w', 'grad_ada_b', 'grad_sc_w_in', 'grad_sc_conv_w', 'grad_sc_w_out', 'grad_lru_w_in', 'grad_lru_conv_w', 'grad_lru_conv_b', 'grad_lru_w_a', 'grad_lru_b_a', 'grad_lru_w_x', 'grad_lru_b_x', 'grad_lru_lambda', 'grad_lru_w_out', 'grad_final_g', 'delta_norm_g', 'delta_ada_w', 'delta_ada_b', 'delta_sc_w_in', 'delta_sc_conv_w', 'delta_sc_w_out', 'delta_lru_w_in', 'delta_lru_conv_w', 'delta_lru_conv_b', 'delta_lru_w_a', 'delta_lru_b_a', 'delta_lru_w_x', 'delta_lru_b_x', 'delta_lru_lambda', 'delta_lru_w_out', 'delta_final_g', 'new_m_norm_g', 'new_m_ada_w', 'new_m_ada_b', 'new_m_sc_w_in', 'new_m_sc_conv_w', 'new_m_sc_w_out', 'new_m_lru_w_in', 'new_m_lru_conv_w', 'new_m_lru_conv_b', 'new_m_lru_w_a', 'new_m_lru_b_a', 'new_m_lru_w_x', 'new_m_lru_b_x', 'new_m_lru_lambda', 'new_m_lru_w_out', 'new_m_final_g', 'new_v_norm_g', 'new_v_ada_w', 'new_v_ada_b', 'new_v_sc_w_in', 'new_v_sc_conv_w', 'new_v_sc_w_out', 'new_v_lru_w_in', 'new_v_lru_conv_w', 'new_v_lru_conv_b', 'new_v_lru_w_a', 'new_v_lru_b_a', 'new_v_lru_w_x', 'new_v_lru_b_x', 'new_v_lru_lambda', 'new_v_lru_w_out', 'new_v_final_g']
TWIN_LEAF_KINDS = {'loss': 'loss', 'grad_x': 'grad_x', 'grad_norm_g': 'grad_w', 'grad_ada_w': 'grad_w', 'grad_ada_b': 'grad_w', 'grad_sc_w_in': 'grad_w', 'grad_sc_conv_w': 'grad_w', 'grad_sc_w_out': 'grad_w', 'grad_lru_w_in': 'grad_w', 'grad_lru_conv_w': 'grad_w', 'grad_lru_conv_b': 'grad_w', 'grad_lru_w_a': 'grad_w', 'grad_lru_b_a': 'grad_w', 'grad_lru_w_x': 'grad_w', 'grad_lru_b_x': 'grad_w', 'grad_lru_lambda': 'grad_w', 'grad_lru_w_out': 'grad_w', 'grad_final_g': 'grad_w', 'delta_norm_g': 'delta_w', 'delta_ada_w': 'delta_w', 'delta_ada_b': 'delta_w', 'delta_sc_w_in': 'delta_w', 'delta_sc_conv_w': 'delta_w', 'delta_sc_w_out': 'delta_w', 'delta_lru_w_in': 'delta_w', 'delta_lru_conv_w': 'delta_w', 'delta_lru_conv_b': 'delta_w', 'delta_lru_w_a': 'delta_w', 'delta_lru_b_a': 'delta_w', 'delta_lru_w_x': 'delta_w', 'delta_lru_b_x': 'delta_w', 'delta_lru_lambda': 'delta_w', 'delta_lru_w_out': 'delta_w', 'delta_final_g': 'delta_w', 'new_m_norm_g': 'new_m', 'new_m_ada_w': 'new_m', 'new_m_ada_b': 'new_m', 'new_m_sc_w_in': 'new_m', 'new_m_sc_conv_w': 'new_m', 'new_m_sc_w_out': 'new_m', 'new_m_lru_w_in': 'new_m', 'new_m_lru_conv_w': 'new_m', 'new_m_lru_conv_b': 'new_m', 'new_m_lru_w_a': 'new_m', 'new_m_lru_b_a': 'new_m', 'new_m_lru_w_x': 'new_m', 'new_m_lru_b_x': 'new_m', 'new_m_lru_lambda': 'new_m', 'new_m_lru_w_out': 'new_m', 'new_m_final_g': 'new_m', 'new_v_norm_g': 'new_v', 'new_v_ada_w': 'new_v', 'new_v_ada_b': 'new_v', 'new_v_sc_w_in': 'new_v', 'new_v_sc_conv_w': 'new_v', 'new_v_sc_w_out': 'new_v', 'new_v_lru_w_in': 'new_v', 'new_v_lru_conv_w': 'new_v', 'new_v_lru_conv_b': 'new_v', 'new_v_lru_w_a': 'new_v', 'new_v_lru_b_a': 'new_v', 'new_v_lru_w_x': 'new_v', 'new_v_lru_b_x': 'new_v', 'new_v_lru_lambda': 'new_v', 'new_v_lru_w_out': 'new_v', 'new_v_final_g': 'new_v'}


def _forward(args):
    return _fwd_reference(*[args[k] for k in FWD_PARAMS])


def _output_shape():
    out = _jax.eval_shape(lambda: _forward(_fwd_setup_inputs(0)))
    return out.shape, out.dtype

N_MICROBATCH = 1
ADAM_LR = 0.001
ADAM_B1 = 0.9
ADAM_B2 = 0.999
ADAM_EPS = 1e-08
ADAM_WD = 0.01
ADAM_STEP = 10
PER_EXAMPLE_BATCH_AXIS = {'x': 0, 'c': 0, 'loss_target': 0}
SHARED_INPUTS = []
_WEIGHT_DTYPES = {'norm_g': _jnp.float32, 'ada_w': _jnp.float32, 'ada_b': _jnp.float32, 'sc_w_in': _jnp.float32, 'sc_conv_w': _jnp.float32, 'sc_w_out': _jnp.float32, 'lru_w_in': _jnp.float32, 'lru_conv_w': _jnp.float32, 'lru_conv_b': _jnp.float32, 'lru_w_a': _jnp.float32, 'lru_b_a': _jnp.float32, 'lru_w_x': _jnp.float32, 'lru_b_x': _jnp.float32, 'lru_lambda': _jnp.float32, 'lru_w_out': _jnp.float32, 'final_g': _jnp.float32}
MOMENT_SCALE = {'norm_g': 2.618931e-02, 'ada_w': 3.401339e-02, 'ada_b': 5.655466e-02, 'sc_w_in': 1.106159e-02, 'sc_conv_w': 1.112944e-02, 'sc_w_out': 1.561170e-02, 'lru_w_in': 1.771683e-02, 'lru_conv_w': 1.786128e-02, 'lru_conv_b': 5.513551e-02, 'lru_w_a': 1.769668e-03, 'lru_b_a': 3.386106e-03, 'lru_w_x': 3.327073e-03, 'lru_b_x': 6.928168e-03, 'lru_lambda': 8.707617e-03, 'lru_w_out': 2.441025e-02, 'final_g': 8.024577e+00}


def _to_microbatches(a, axis):
    t = _jnp.moveaxis(a, axis, 0)
    t = t.reshape((N_MICROBATCH, t.shape[0] // N_MICROBATCH) + t.shape[1:])
    return _jnp.moveaxis(t, 1, axis + 1)


def setup_inputs(seed: int = 0) -> dict:
    inp = _fwd_setup_inputs(seed)
    key = _jax.random.fold_in(_jax.random.key(seed), 7919)
    shape, _ = _output_shape()
    out = dict(inp)
    out["loss_target"] = _jax.random.normal(_jax.random.fold_in(key, 0), shape, _jnp.float32)
    for i, name in enumerate(TWIN_WEIGHTS):
        w = inp[name].astype(_jnp.float32)
        if MOMENT_SCALE is None:
            s = _jnp.sqrt(_jnp.mean(_jnp.square(w)) + 1e-30)
        else:
            s = MOMENT_SCALE[name]
        km, kv = _jax.random.split(_jax.random.fold_in(key, i + 1))
        out[name] = w
        out["m_" + name] = s * _jax.random.normal(km, w.shape, _jnp.float32)
        out["v_" + name] = (s * s) * _jax.random.uniform(kv, w.shape, _jnp.float32, 0.5, 1.5)
    if N_MICROBATCH > 1:
        for name, axis in PER_EXAMPLE_BATCH_AXIS.items():
            out[name] = _to_microbatches(out[name], axis)
    return {'x': out['x'], 'c': out['c'], 'norm_g': out['norm_g'], 'ada_w': out['ada_w'], 'ada_b': out['ada_b'], 'sc_w_in': out['sc_w_in'], 'sc_conv_w': out['sc_conv_w'], 'sc_w_out': out['sc_w_out'], 'lru_w_in': out['lru_w_in'], 'lru_conv_w': out['lru_conv_w'], 'lru_conv_b': out['lru_conv_b'], 'lru_w_a': out['lru_w_a'], 'lru_b_a': out['lru_b_a'], 'lru_w_x': out['lru_w_x'], 'lru_b_x': out['lru_b_x'], 'lru_lambda': out['lru_lambda'], 'lru_w_out': out['lru_w_out'], 'final_g': out['final_g'], 'loss_target': out['loss_target'], 'm_norm_g': out['m_norm_g'], 'm_ada_w': out['m_ada_w'], 'm_ada_b': out['m_ada_b'], 'm_sc_w_in': out['m_sc_w_in'], 'm_sc_conv_w': out['m_sc_conv_w'], 'm_sc_w_out': out['m_sc_w_out'], 'm_lru_w_in': out['m_lru_w_in'], 'm_lru_conv_w': out['m_lru_conv_w'], 'm_lru_conv_b': out['m_lru_conv_b'], 'm_lru_w_a': out['m_lru_w_a'], 'm_lru_b_a': out['m_lru_b_a'], 'm_lru_w_x': out['m_lru_w_x'], 'm_lru_b_x': out['m_lru_b_x'], 'm_lru_lambda': out['m_lru_lambda'], 'm_lru_w_out': out['m_lru_w_out'], 'm_final_g': out['m_final_g'], 'v_norm_g': out['v_norm_g'], 'v_ada_w': out['v_ada_w'], 'v_ada_b': out['v_ada_b'], 'v_sc_w_in': out['v_sc_w_in'], 'v_sc_conv_w': out['v_sc_conv_w'], 'v_sc_w_out': out['v_sc_w_out'], 'v_lru_w_in': out['v_lru_w_in'], 'v_lru_conv_w': out['v_lru_conv_w'], 'v_lru_conv_b': out['v_lru_conv_b'], 'v_lru_w_a': out['v_lru_w_a'], 'v_lru_b_a': out['v_lru_b_a'], 'v_lru_w_x': out['v_lru_w_x'], 'v_lru_b_x': out['v_lru_b_x'], 'v_lru_lambda': out['v_lru_lambda'], 'v_lru_w_out': out['v_lru_w_out'], 'v_final_g': out['v_final_g']}


def _loss(weights, diff, rest, loss_target):
    with _jax.named_scope("forward"):
        args = {**rest, TWIN_DIFF_INPUT: diff, **{k: w.astype(_WEIGHT_DTYPES[k]) for k, w in weights.items()}}
        y = _forward(args)
    with _jax.named_scope("loss_head"):
        err = _jnp.square(y.astype(_jnp.float32) - loss_target)
        return 0.5 * _jnp.sum(_jnp.mean(err, axis=-1)) if err.ndim else 0.5 * err


def _adamw(w, g, m, v):
    m = ADAM_B1 * m + (1.0 - ADAM_B1) * g
    v = ADAM_B2 * v + (1.0 - ADAM_B2) * _jnp.square(g)
    m_hat = m / (1.0 - ADAM_B1 ** ADAM_STEP)
    v_hat = v / (1.0 - ADAM_B2 ** ADAM_STEP)
    delta = -ADAM_LR * (m_hat / (_jnp.sqrt(v_hat) + ADAM_EPS) + ADAM_WD * w)
    return delta, m, v


def reference(x, c, norm_g, ada_w, ada_b, sc_w_in, sc_conv_w, sc_w_out, lru_w_in, lru_conv_w, lru_conv_b, lru_w_a, lru_b_a, lru_w_x, lru_b_x, lru_lambda, lru_w_out, final_g, loss_target, m_norm_g, m_ada_w, m_ada_b, m_sc_w_in, m_sc_conv_w, m_sc_w_out, m_lru_w_in, m_lru_conv_w, m_lru_conv_b, m_lru_w_a, m_lru_b_a, m_lru_w_x, m_lru_b_x, m_lru_lambda, m_lru_w_out, m_final_g, v_norm_g, v_ada_w, v_ada_b, v_sc_w_in, v_sc_conv_w, v_sc_w_out, v_lru_w_in, v_lru_conv_w, v_lru_conv_b, v_lru_w_a, v_lru_b_a, v_lru_w_x, v_lru_b_x, v_lru_lambda, v_lru_w_out, v_final_g):
    given = dict(x=x, c=c, norm_g=norm_g, ada_w=ada_w, ada_b=ada_b, sc_w_in=sc_w_in, sc_conv_w=sc_conv_w, sc_w_out=sc_w_out, lru_w_in=lru_w_in, lru_conv_w=lru_conv_w, lru_conv_b=lru_conv_b, lru_w_a=lru_w_a, lru_b_a=lru_b_a, lru_w_x=lru_w_x, lru_b_x=lru_b_x, lru_lambda=lru_lambda, lru_w_out=lru_w_out, final_g=final_g, loss_target=loss_target, m_norm_g=m_norm_g, m_ada_w=m_ada_w, m_ada_b=m_ada_b, m_sc_w_in=m_sc_w_in, m_sc_conv_w=m_sc_conv_w, m_sc_w_out=m_sc_w_out, m_lru_w_in=m_lru_w_in, m_lru_conv_w=m_lru_conv_w, m_lru_conv_b=m_lru_conv_b, m_lru_w_a=m_lru_w_a, m_lru_b_a=m_lru_b_a, m_lru_w_x=m_lru_w_x, m_lru_b_x=m_lru_b_x, m_lru_lambda=m_lru_lambda, m_lru_w_out=m_lru_w_out, m_final_g=m_final_g, v_norm_g=v_norm_g, v_ada_w=v_ada_w, v_ada_b=v_ada_b, v_sc_w_in=v_sc_w_in, v_sc_conv_w=v_sc_conv_w, v_sc_w_out=v_sc_w_out, v_lru_w_in=v_lru_w_in, v_lru_conv_w=v_lru_conv_w, v_lru_conv_b=v_lru_conv_b, v_lru_w_a=v_lru_w_a, v_lru_b_a=v_lru_b_a, v_lru_w_x=v_lru_w_x, v_lru_b_x=v_lru_b_x, v_lru_lambda=v_lru_lambda, v_lru_w_out=v_lru_w_out, v_final_g=v_final_g)
    weights = {n: given[n] for n in TWIN_WEIGHTS}
    shared = {n: given[n] for n in SHARED_INPUTS}
    per_example = {n: given[n] for n in ['x', 'c']}
    grad_fn = _jax.value_and_grad(_loss, argnums=(0, 1))

    def one_microbatch(ex, loss_target):
        ex = dict(ex)
        diff = ex.pop(TWIN_DIFF_INPUT)
        return grad_fn(weights, diff, {**shared, **ex}, loss_target)

    if N_MICROBATCH == 1:
        loss, (grad_w, grad_x) = one_microbatch(per_example, given["loss_target"])
    else:
        def body(carry, xs):
            loss_sum, grad_sum = carry
            l_k, (gw_k, gx_k) = one_microbatch(xs[0], xs[1])
            with _jax.named_scope("update"):
                return (loss_sum + l_k, _jax.tree.map(_jnp.add, grad_sum, gw_k)), gx_k

        init = (_jnp.zeros((), _jnp.float32), _jax.tree.map(_jnp.zeros_like, weights))
        (loss, grad_w), grad_x = _jax.lax.scan(body, init, (per_example, given["loss_target"]))
    with _jax.named_scope("update"):
        delta_w, new_m, new_v = {}, {}, {}
        for n in TWIN_WEIGHTS:
            delta_w[n], new_m[n], new_v[n] = _adamw(weights[n], grad_w[n], given["m_" + n], given["v_" + n])
    return (loss, grad_x, *[grad_w[n] for n in TWIN_WEIGHTS], *[delta_w[n] for n in TWIN_WEIGHTS],
            *[new_m[n] for n in TWIN_WEIGHTS], *[new_v[n] for n in TWIN_WEIGHTS])
```

```python
import functools

import jax
import jax.numpy as jnp
from jax import lax
from jax.experimental import pallas as pl
from jax.experimental.pallas import tpu as pltpu

F32 = jnp.float32
BF16 = jnp.bfloat16
MESH = pl.DeviceIdType.MESH
ANY = pl.BlockSpec(memory_space=pl.ANY)

RMS_EPS = 1e-6
RGLRU_C = 8.0
HEAD_DIM = 256
ADAM_LR = 0.001
ADAM_B1 = 0.9
ADAM_B2 = 0.999
ADAM_EPS = 1e-08
ADAM_WD = 0.01
ADAM_STEP = 10
V7X_VMEM_LIMIT = 56 * 1024 * 1024
LANES = 128
PACK = 8 * LANES


def _blk(dim, pref, unit=LANES):
    if dim <= pref:
        return dim
    b = (pref // unit) * unit
    while b > unit and dim % b:
        b -= unit
    assert dim % b == 0, (dim, pref, unit)
    return b


def _params(sem=None):
    return pltpu.CompilerParams(dimension_semantics=sem, vmem_limit_bytes=V7X_VMEM_LIMIT)


def _pos():
    return lax.axis_index("x"), lax.axis_index("y"), lax.axis_index("c")


def _other_chips(x, y):
    return [(1 - x, y), (x, 1 - y), (1 - x, 1 - y)]


def _allgather8(arrs, from_half, name):
    n_t = len(arrs)
    ms = [a.shape[0] // 2 if from_half else a.shape[0] for a in arrs]

    def body(*refs):
        ins, outs = refs[:n_t], refs[n_t:2 * n_t]
        send_sems, recv_sems, local_sems = refs[2 * n_t:]
        x, y, c = _pos()
        me, sibling = (x, y, c), (x, y, 1 - c)
        chips = _other_chips(x, y)

        def rows(t, px, py, pc):
            return outs[t].at[pl.ds((4 * px + 2 * py + pc) * ms[t], ms[t])]

        def copy(t, k, block, to, src=None):
            return pltpu.make_async_remote_copy(
                src_ref=rows(t, *block) if src is None else src, dst_ref=rows(t, *block),
                send_sem=send_sems.at[7 * t + k], recv_sem=recv_sems.at[7 * t + k],
                device_id=to, device_id_type=MESH)

        mine, first, passed = [], [], []
        for t in range(n_t):
            src = ins[t].at[pl.ds(c * ms[t], ms[t])] if from_half else ins[t]
            cp = pltpu.make_async_copy(src, rows(t, *me), local_sems.at[t])
            cp.start()
            mine.append(cp)
            sends = [copy(t, 0, me, sibling, src=src)]
            sends += [copy(t, 1 + j, me, (*chip, c), src=src) for j, chip in enumerate(chips)]
            for cp in sends:
                cp.start()
            first += sends
        for t in range(n_t):
            for j, chip in enumerate(chips):
                copy(t, 1 + j, (*chip, c), me).wait_recv()
                cp = copy(t, 4 + j, (*chip, c), sibling)
                cp.start()
                passed.append(cp)
        for t in range(n_t):
            copy(t, 0, sibling, me).wait_recv()
            for j, chip in enumerate(chips):
                copy(t, 4 + j, (*chip, 1 - c), me).wait_recv()
        for cp in first + passed:
            cp.wait_send()
        for cp in mine:
            cp.wait()

    return pl.pallas_call(
        body, name=name,
        out_shape=[jax.ShapeDtypeStruct((8 * m, a.shape[1]), a.dtype) for m, a in zip(ms, arrs)],
        in_specs=[ANY] * n_t, out_specs=[ANY] * n_t,
        scratch_shapes=[pltpu.SemaphoreType.DMA((7 * n_t,)), pltpu.SemaphoreType.DMA((7 * n_t,)),
                        pltpu.SemaphoreType.DMA((n_t,))],
    )(*arrs)


def _exchange(name, ins, out_shapes, n_copies, plan):
    ni, no = len(ins), len(out_shapes)

    def body(*refs):
        in_refs, out_refs = refs[:ni], refs[ni:ni + no]
        send_sems, recv_sems, local_sems = refs[ni + no:]
        x, y, c = _pos()
        copies = []
        for k, (src, dst, dev) in enumerate(plan(in_refs, out_refs, x, y, c)):
            if dev is None:
                cp = pltpu.make_async_copy(src, dst, local_sems.at[k])
            else:
                cp = pltpu.make_async_remote_copy(
                    src_ref=src, dst_ref=dst, send_sem=send_sems.at[k], recv_sem=recv_sems.at[k],
                    device_id=dev, device_id_type=MESH)
            cp.start()
            copies.append(cp)
        assert len(copies) == n_copies
        for cp in copies:
            cp.wait()

    return pl.pallas_call(
        body, name=name, out_shape=out_shapes,
        in_specs=[ANY] * ni, out_specs=[ANY] * no,
        scratch_shapes=[pltpu.SemaphoreType.DMA((n_copies,)), pltpu.SemaphoreType.DMA((n_copies,)),
                        pltpu.SemaphoreType.DMA((n_copies,))],
    )(*ins)


def _sibling_send_other_half(grads):
    def plan(in_refs, out_refs, x, y, c):
        out = []
        for g_ref, r_ref in zip(in_refs, out_refs):
            h = g_ref.shape[1] // 2
            out.append((g_ref.at[:, pl.ds((1 - c) * h, h), :], r_ref, (x, y, 1 - c)))
        return out

    shapes = [jax.ShapeDtypeStruct((4, g.shape[1] // 2, g.shape[2]), g.dtype) for g in grads]
    return _exchange("rs_sibling", grads, shapes, len(grads), plan)


def _chips_send_partials(parts):
    def plan(in_refs, out_refs, x, y, c):
        out = []
        for p_ref, r_ref in zip(in_refs, out_refs):
            for j, (px, py) in enumerate(_other_chips(x, y)):
                out.append((p_ref.at[2 * px + py], r_ref.at[j], (px, py, c)))
        return out

    shapes = [jax.ShapeDtypeStruct((3,) + p.shape[1:], p.dtype) for p in parts]
    return _exchange("rs_chips", parts, shapes, 3 * len(parts), plan)


def _sibling_share_half(halves):
    def plan(in_refs, out_refs, x, y, c):
        out = []
        for h_ref, f_ref in zip(in_refs, out_refs):
            h = h_ref.shape[0]
            out.append((h_ref, f_ref.at[pl.ds(c * h, h)], None))
            out.append((h_ref, f_ref.at[pl.ds(c * h, h)], (x, y, 1 - c)))
        return out

    shapes = [jax.ShapeDtypeStruct((2 * h.shape[0], h.shape[1]), h.dtype) for h in halves]
    return _exchange("rs_share", halves, shapes, 2 * len(halves), plan)


def _cast_bf16(w, name):
    r, c = w.shape
    br, bc = _blk(r, 512, 8), _blk(c, 2048)

    def body(w_ref, o_ref):
        o_ref[...] = w_ref[...].astype(BF16)

    return pl.pallas_call(
        body, name=name, grid=(r // br, c // bc),
        in_specs=[pl.BlockSpec((br, bc), lambda i, j: (i, j))],
        out_specs=pl.BlockSpec((br, bc), lambda i, j: (i, j)),
        out_shape=jax.ShapeDtypeStruct((r, c), BF16),
        compiler_params=_params(("parallel", "parallel")),
    )(w)


def _sum_own_and_sibling(core, g, r1, name):
    _, r, c = g.shape
    h = r // 2
    br, bc = _blk(h, 256, 8), _blk(c, 2048)
    nb = h // br

    def body(core_ref, g_ref, r_ref, o_ref):
        o_ref[...] = (g_ref[...] + r_ref[...]).astype(BF16)

    grid_spec = pltpu.PrefetchScalarGridSpec(
        num_scalar_prefetch=1, grid=(4, nb, c // bc),
        in_specs=[pl.BlockSpec((None, br, bc), lambda k, i, j, core_ref: (k, core_ref[0] * nb + i, j)),
                  pl.BlockSpec((None, br, bc), lambda k, i, j, core_ref: (k, i, j))],
        out_specs=pl.BlockSpec((None, br, bc), lambda k, i, j, core_ref: (k, i, j)))
    return pl.pallas_call(
        body, name=name, grid_spec=grid_spec,
        out_shape=jax.ShapeDtypeStruct((4, h, c), BF16),
        compiler_params=_params(("parallel", "parallel", "parallel")),
    )(core, g, r1)


def _sum_chips(chip, p, r2, name):
    _, h, c = p.shape
    br, bc = _blk(h, 256, 8), _blk(c, 2048)

    def body(chip_ref, p_ref, r_ref, o_ref):
        acc = p_ref[...].astype(F32)
        for j in range(3):
            acc = acc + r_ref[j].astype(F32)
        o_ref[...] = acc

    grid_spec = pltpu.PrefetchScalarGridSpec(
        num_scalar_prefetch=1, grid=(h // br, c // bc),
        in_specs=[pl.BlockSpec((None, br, bc), lambda i, j, chip_ref: (chip_ref[0], i, j)),
                  pl.BlockSpec((3, br, bc), lambda i, j, chip_ref: (0, i, j))],
        out_specs=pl.BlockSpec((br, bc), lambda i, j, chip_ref: (i, j)))
    return pl.pallas_call(
        body, name=name, grid_spec=grid_spec,
        out_shape=jax.ShapeDtypeStruct((h, c), F32),
        compiler_params=_params(("parallel", "parallel")),
    )(chip, p, r2)


def _adamw_math(w, g, m, v):
    m2 = ADAM_B1 * m + (1.0 - ADAM_B1) * g
    v2 = ADAM_B2 * v + (1.0 - ADAM_B2) * (g * g)
    m_hat = m2 / (1.0 - ADAM_B1 ** ADAM_STEP)
    v_hat = v2 / (1.0 - ADAM_B2 ** ADAM_STEP)
    delta = -ADAM_LR * (m_hat / (jnp.sqrt(v_hat) + ADAM_EPS) + ADAM_WD * w)
    return delta, m2, v2


def _adamw(w, g, m, v, name):
    r, c = w.shape
    br, bc = _blk(r, 128, 8), _blk(c, 2048)

    def body(w_ref, g_ref, m_ref, v_ref, d_ref, m2_ref, v2_ref):
        d, m2, v2 = _adamw_math(w_ref[...], g_ref[...], m_ref[...], v_ref[...])
        d_ref[...] = d
        m2_ref[...] = m2
        v2_ref[...] = v2

    spec = pl.BlockSpec((br, bc), lambda i, j: (i, j))
    return pl.pallas_call(
        body, name=name, grid=(r // br, c // bc),
        in_specs=[spec] * 4, out_specs=[spec] * 3,
        out_shape=[jax.ShapeDtypeStruct((r, c), F32)] * 3,
        compiler_params=_params(("parallel", "parallel")),
    )(w, g, m, v)


def _sum_rows8(g, name):
    n = g.shape[1]

    def body(g_ref, o_ref):
        acc = g_ref[0:1, :]
        for k in range(1, 8):
            acc = acc + g_ref[k:k + 1, :]
        o_ref[...] = acc

    return pl.pallas_call(
        body, name=name, out_shape=jax.ShapeDtypeStruct((1, n), F32),
        in_specs=[pl.BlockSpec(memory_space=pltpu.VMEM)],
        out_specs=pl.BlockSpec(memory_space=pltpu.VMEM),
        compiler_params=_params(),
    )(g)


def _mm(a, b, mode, out_dtype, name, bm=512, bn=512):
    if mode == "nn":
        (_, m, k), (g, _, n) = a.shape, b.shape
    elif mode == "tn":
        (_, k, m), (g, _, n) = a.shape, b.shape
    else:
        (g, m, k), (_, n, _) = a.shape, b.shape
    bm, bn = _blk(m, bm), _blk(n, bn)

    if mode == "nt":
        def body(a_ref, b_ref, o_ref, acc_ref):
            part = lax.dot_general(a_ref[...], b_ref[...], (((1,), (1,)), ((), ())),
                                   preferred_element_type=F32)
            if g == 1:
                o_ref[...] = part.astype(out_dtype)
            else:
                gi = pl.program_id(2)

                @pl.when(gi == 0)
                def _():
                    acc_ref[...] = part

                @pl.when(gi > 0)
                def _():
                    acc_ref[...] += part

                @pl.when(gi == g - 1)
                def _():
                    o_ref[...] = acc_ref[...].astype(out_dtype)

        return pl.pallas_call(
            body, name=name, grid=(m // bm, n // bn, g),
            in_specs=[pl.BlockSpec((None, bm, k), lambda i, j, gi: (gi, i, 0)),
                      pl.BlockSpec((None, bn, k), lambda i, j, gi: (gi, j, 0))],
            out_specs=pl.BlockSpec((None, bm, bn), lambda i, j, gi: (0, i, j)),
            out_shape=jax.ShapeDtypeStruct((1, m, n), out_dtype),
            scratch_shapes=[pltpu.VMEM((bm, bn), F32)],
            compiler_params=_params(("parallel", "parallel", "arbitrary")),
        )(a, b)

    contract = (((1,), (0,)), ((), ())) if mode == "nn" else (((0,), (0,)), ((), ()))

    def body(a_ref, b_ref, o_ref):
        o_ref[...] = lax.dot_general(a_ref[...], b_ref[...], contract,
                                     preferred_element_type=F32).astype(out_dtype)

    a_spec = (pl.BlockSpec((None, bm, k), lambda i, gi, j: (0, i, 0)) if mode == "nn"
              else pl.BlockSpec((None, k, bm), lambda i, gi, j: (0, 0, i)))
    return pl.pallas_call(
        body, name=name, grid=(m // bm, g, n // bn),
        in_specs=[a_spec, pl.BlockSpec((None, k, bn), lambda i, gi, j: (gi, 0, j))],
        out_specs=pl.BlockSpec((None, bm, bn), lambda i, gi, j: (gi, i, j)),
        out_shape=jax.ShapeDtypeStruct((g, m, n), out_dtype),
        compiler_params=_params(("parallel", "parallel", "parallel")),
    )(a, b)


def _row_specs(br, d):
    return (pl.BlockSpec((br, d), lambda i: (i, 0)), pl.BlockSpec((1, d), lambda i: (0, 0)),
            pl.BlockSpec((8, d), lambda i: (0, 0)))


def _rstd(xv):
    return lax.rsqrt(jnp.mean(xv * xv, axis=-1, keepdims=True) + RMS_EPS)


def _colsum(v):
    return jnp.sum(v, axis=0, keepdims=True)


def _norm_mod_fwd(x, g, scale, shift, name, o=None, gate=None):
    s, d = x.shape
    br = _blk(s, 256, 8)
    has_res = o is not None
    row, vec, _ = _row_specs(br, d)

    def body(*refs):
        if has_res:
            x_ref, o_ref, gate_ref, g_ref, sc_ref, sh_ref, x1_ref, h_ref = refs
            xv = x_ref[...] + gate_ref[...] * o_ref[...]
            x1_ref[...] = xv
        else:
            x_ref, g_ref, sc_ref, sh_ref, h_ref = refs
            xv = x_ref[...]
        n = xv * _rstd(xv) * g_ref[...]
        h_ref[...] = (n * (1.0 + sc_ref[...]) + sh_ref[...]).astype(BF16)

    ins = [x] + ([o, gate] if has_res else []) + [g, scale, shift]
    in_specs = [row] + ([row, vec] if has_res else []) + [vec] * 3
    out_shape = ([jax.ShapeDtypeStruct((s, d), F32)] if has_res else []) + [jax.ShapeDtypeStruct((s, d), BF16)]
    out = pl.pallas_call(
        body, name=name, grid=(s // br,), in_specs=in_specs, out_specs=[row] * len(out_shape),
        out_shape=out_shape, compiler_params=_params(("parallel",)),
    )(*ins)
    return out if has_res else out[0]


def _final_loss(x1, o1, gate1, final_g, tgt, name):
    s, d = x1.shape
    br = _blk(s, 256, 8)
    row, vec, acc = _row_specs(br, d)

    def body(x1_ref, o_ref, gate_ref, g_ref, t_ref, dx_ref, do_ref, acc_ref):
        @pl.when(pl.program_id(0) == 0)
        def _():
            acc_ref[...] = jnp.zeros_like(acc_ref)

        gate, o, g = gate_ref[...], o_ref[...], g_ref[...]
        x2 = x1_ref[...] + gate * o
        r = _rstd(x2)
        xh = x2 * r
        err = xh * g - t_ref[...]
        loss = 0.5 * _colsum(jnp.mean(err * err, axis=-1, keepdims=True))
        dout = err * (1.0 / d)
        dxh = dout * g
        dx2 = r * (dxh - xh * jnp.mean(dxh * xh, axis=-1, keepdims=True))
        dx_ref[...] = dx2
        do_ref[...] = (dx2 * gate).astype(BF16)
        acc_ref[0:1, :] += _colsum(dout * xh)
        acc_ref[1:2, :] += _colsum(dx2 * o)
        acc_ref[2:3, :] += jnp.broadcast_to(loss, (1, d))

    return pl.pallas_call(
        body, name=name, grid=(s // br,),
        in_specs=[row, row, vec, vec, row], out_specs=[row, row, acc],
        out_shape=[jax.ShapeDtypeStruct((s, d), F32), jax.ShapeDtypeStruct((s, d), BF16),
                   jax.ShapeDtypeStruct((8, d), F32)],
        compiler_params=_params(("arbitrary",)),
    )(x1, o1, gate1, final_g, tgt)


def _norm_mod_bwd(dh, x, g, scale, dx_next, name, o_prev=None, gate_prev=None):
    s, d = x.shape
    br = _blk(s, 256, 8)
    has_prev = o_prev is not None
    row, vec, acc = _row_specs(br, d)

    def body(*refs):
        if has_prev:
            dh_ref, x_ref, g_ref, sc_ref, dxn_ref, o_ref, gate_ref, dx_ref, do_ref, acc_ref = refs
        else:
            dh_ref, x_ref, g_ref, sc_ref, dxn_ref, dx_ref, acc_ref = refs

        @pl.when(pl.program_id(0) == 0)
        def _():
            acc_ref[...] = jnp.zeros_like(acc_ref)

        xv, gv, dhv = x_ref[...], g_ref[...], dh_ref[...]
        r = _rstd(xv)
        xh = xv * r
        acc_ref[0:1, :] += _colsum(dhv * (xh * gv))
        acc_ref[1:2, :] += _colsum(dhv)
        dn = dhv * (1.0 + sc_ref[...])
        acc_ref[2:3, :] += _colsum(dn * xh)
        dxh = dn * gv
        dx = dxn_ref[...] + r * (dxh - xh * jnp.mean(dxh * xh, axis=-1, keepdims=True))
        dx_ref[...] = dx
        if has_prev:
            acc_ref[3:4, :] += _colsum(dx * o_ref[...])
            do_ref[...] = (dx * gate_ref[...]).astype(BF16)

    ins = [dh, x, g, scale, dx_next] + ([o_prev, gate_prev] if has_prev else [])
    in_specs = [row, row, vec, vec, row] + ([row, vec] if has_prev else [])
    out_shape = [jax.ShapeDtypeStruct((s, d), F32)]
    out_specs = [row]
    if has_prev:
        out_shape.append(jax.ShapeDtypeStruct((s, d), BF16))
        out_specs.append(row)
    out_shape.append(jax.ShapeDtypeStruct((8, d), F32))
    out_specs.append(acc)
    return pl.pallas_call(
        body, name=name, grid=(s // br,), in_specs=in_specs, out_specs=out_specs,
        out_shape=out_shape, compiler_params=_params(("arbitrary",)),
    )(*ins)


def _shift_down(p, k):
    if k == 0:
        return p
    rows = lax.broadcasted_iota(jnp.int32, p.shape, 0)
    return jnp.where(rows >= k, pltpu.roll(p, k, 0), 0.0)


def _shift_up(p, k):
    if k == 0:
        return p
    s = p.shape[0]
    rows = lax.broadcasted_iota(jnp.int32, p.shape, 0)
    return jnp.where(rows < s - k, pltpu.roll(p, s - k, 0), 0.0)


def _sigmoid(z):
    return 1.0 / (1.0 + jnp.exp(-z))


def _sc_parts(proj_ref, w_ref):
    b, cg, v, g = (proj_ref[i].astype(F32) for i in range(4))
    p = cg * v
    u = w_ref[2:3, :] * p + w_ref[1:2, :] * _shift_down(p, 1) + w_ref[0:1, :] * _shift_down(p, 2)
    return b, cg, v, g, p, u


def _sc_fwd(proj, conv_w, name):
    _, s, e = proj.shape
    bc = _blk(e, 256)

    def body(proj_ref, w_ref, y_ref):
        b, _, _, g, _, u = _sc_parts(proj_ref, w_ref)
        y_ref[...] = (b * u * (g * _sigmoid(g))).astype(BF16)

    return pl.pallas_call(
        body, name=name, grid=(e // bc,),
        in_specs=[pl.BlockSpec((4, s, bc), lambda j: (0, 0, j)), pl.BlockSpec((3, bc), lambda j: (0, j))],
        out_specs=pl.BlockSpec((s, bc), lambda j: (0, j)),
        out_shape=jax.ShapeDtypeStruct((s, e), BF16),
        compiler_params=_params(("parallel",)),
    )(proj, conv_w)


def _sc_bwd(proj, dy, conv_w, name):
    _, s, e = proj.shape
    bc = _blk(e, 256)

    def body(proj_ref, dy_ref, w_ref, dp_ref, dw_ref):
        b, cg, v, g, p, u = _sc_parts(proj_ref, w_ref)
        dyv = dy_ref[...].astype(F32)
        sig = _sigmoid(g)
        t = dyv * (g * sig)
        du = t * b
        dp_ref[0] = (t * u).astype(BF16)
        dp_ref[3] = (dyv * b * u * (sig * (1.0 + g * (1.0 - sig)))).astype(BF16)
        dpp = w_ref[2:3, :] * du + w_ref[1:2, :] * _shift_up(du, 1) + w_ref[0:1, :] * _shift_up(du, 2)
        dp_ref[1] = (dpp * v).astype(BF16)
        dp_ref[2] = (dpp * cg).astype(BF16)
        dw_ref[2:3, :] = _colsum(du * p)
        dw_ref[1:2, :] = _colsum(du * _shift_down(p, 1))
        dw_ref[0:1, :] = _colsum(du * _shift_down(p, 2))

    return pl.pallas_call(
        body, name=name, grid=(e // bc,),
        in_specs=[pl.BlockSpec((4, s, bc), lambda j: (0, 0, j)), pl.BlockSpec((s, bc), lambda j: (0, j)),
                  pl.BlockSpec((3, bc), lambda j: (0, j))],
        out_specs=[pl.BlockSpec((4, s, bc), lambda j: (0, 0, j)), pl.BlockSpec((3, bc), lambda j: (0, j))],
        out_shape=[jax.ShapeDtypeStruct((4, s, e), BF16), jax.ShapeDtypeStruct((3, e), F32)],
        compiler_params=_params(("parallel",)),
    )(proj, dy, conv_w)


def _softplus_neg(lam):
    u = jnp.exp(-jnp.abs(lam))
    w = 1.0 + u
    log1p = jnp.where(w == 1.0, u, jnp.log(w) * (u / jnp.where(w == 1.0, 1.0, w - 1.0)))
    return jnp.maximum(-lam, 0.0) + log1p


def _one_minus_exp(z):
    series = -z * (1.0 + z * (0.5 + z * (1.0 / 6.0 + z * (1.0 / 24.0))))
    return jnp.where(z > -0.02, series, 1.0 - jnp.exp(z))


def _scan_in_tiles(a, b, reverse):
    s = a.shape[0]
    rows = lax.broadcasted_iota(jnp.int32, a.shape, 0) & 7
    for step in (1, 2, 4):
        if reverse:
            ok = rows < 8 - step
            a_s, b_s = pltpu.roll(a, s - step, 0), pltpu.roll(b, s - step, 0)
        else:
            ok = rows >= step
            a_s, b_s = pltpu.roll(a, step, 0), pltpu.roll(b, step, 0)
        b = jnp.where(ok, a * b_s + b, b)
        a = jnp.where(ok, a * a_s, a)
    return a, b


def _scan(a, b, a_ref, b_ref, h_ref, reverse):
    s, c = a.shape
    n = s // 8
    a_t, b_t = _scan_in_tiles(a, b, reverse)
    a_ref[...] = a_t
    b_ref[...] = b_t

    def step(i, carry):
        gi = n - 1 - i if reverse else i
        sl = pl.ds(pl.multiple_of(gi * 8, 8), 8)
        h = b_ref[sl, :] + a_ref[sl, :] * carry
        h_ref[sl, :] = h
        return h[0:1, :] if reverse else h[7:8, :]

    lax.fori_loop(0, n, step, jnp.zeros((1, c), F32))
    return h_ref[...]


def _lru_specs(s, e_half, n_heads):
    hp = e_half // HEAD_DIM
    c = HEAD_DIM
    return dict(
        pair=pl.BlockSpec((2, None, s, c), lambda h: (0, h // hp, 0, h % hp)),
        conv_w=pl.BlockSpec((4, c), lambda h: (0, h)),
        chan=pl.BlockSpec((1, c), lambda h: (0, h)),
        w=pl.BlockSpec((4, None, c // 4, c), lambda h: (0, h, 0, 0)),
        bias=pl.BlockSpec((None, 1, c), lambda h: (h, 0, 0)),
        plane=pl.BlockSpec((s, c), lambda h: (0, h)),
    )


def _lru_forward_parts(vp, cw_ref, cb_ref, wa_ref, ba_ref, wx_ref, bx_ref, lam_ref):
    c = HEAD_DIM
    v = cb_ref[...] + sum(cw_ref[k:k + 1, :] * _shift_down(vp, 3 - k) for k in range(4))
    vb = v.astype(BF16)
    wa = wa_ref[...].reshape(c, c)
    wx = wx_ref[...].reshape(c, c)
    r = _sigmoid(jnp.dot(vb, wa, preferred_element_type=F32) + ba_ref[...])
    i = _sigmoid(jnp.dot(vb, wx, preferred_element_type=F32) + bx_ref[...])
    sp = _softplus_neg(lam_ref[...])
    la = (-RGLRU_C) * sp * r
    a = jnp.exp(la)
    nm = jnp.sqrt(_one_minus_exp(2.0 * la))
    return v, vb, wa, wx, r, i, sp, a, nm


def _lru_fwd(proj, conv_w, conv_b, w_a, b_a, w_x, b_x, lam, name):
    _, _, s, e_half = proj.shape
    n_heads = 2 * e_half // HEAD_DIM
    sp_ = _lru_specs(s, e_half, n_heads)

    def body(pg_ref, cw_ref, cb_ref, wa_ref, ba_ref, wx_ref, bx_ref, lam_ref, y_ref,
             sa_ref, sb_ref, sh_ref):
        v, _, _, _, _, i, _, a, nm = _lru_forward_parts(
            pg_ref[0].astype(F32), cw_ref, cb_ref, wa_ref, ba_ref, wx_ref, bx_ref, lam_ref)
        hs = _scan(a, nm * (i * v), sa_ref, sb_ref, sh_ref, reverse=False)
        g = pg_ref[1].astype(F32)
        y_ref[...] = (hs * (g * _sigmoid(g))).astype(BF16)

    return pl.pallas_call(
        body, name=name, grid=(n_heads,),
        in_specs=[sp_["pair"], sp_["conv_w"], sp_["chan"], sp_["w"], sp_["bias"], sp_["w"],
                  sp_["bias"], sp_["chan"]],
        out_specs=sp_["plane"],
        out_shape=jax.ShapeDtypeStruct((s, 2 * e_half), BF16),
        scratch_shapes=[pltpu.VMEM((s, HEAD_DIM), F32)] * 3,
        compiler_params=_params(("parallel",)),
    )(proj, conv_w, conv_b, w_a, b_a, w_x, b_x, lam)


def _lru_bwd(proj, dy, conv_w, conv_b, w_a, b_a, w_x, b_x, lam, name):
    _, _, s, e_half = proj.shape
    e = 2 * e_half
    c = HEAD_DIM
    n_heads = e // c
    hp = e_half // c
    sp_ = _lru_specs(s, e_half, n_heads)

    def body(pg_ref, dy_ref, cw_ref, cb_ref, wa_ref, ba_ref, wx_ref, bx_ref, lam_ref,
             dpg_ref, dwa_ref, dwx_ref, dba_ref, dbx_ref, dlam_ref, dcw_ref, dcb_ref,
             sa_ref, sb_ref, sh_ref, sd_ref):
        vp = pg_ref[0].astype(F32)
        v, vb, wa, wx, r, i, sp, a, nm = _lru_forward_parts(
            vp, cw_ref, cb_ref, wa_ref, ba_ref, wx_ref, bx_ref, lam_ref)
        iv = i * v
        hs = _scan(a, nm * iv, sa_ref, sb_ref, sh_ref, reverse=False)
        g = pg_ref[1].astype(F32)
        dyv = dy_ref[...].astype(F32)
        sig = _sigmoid(g)
        dpg_ref[1] = (dyv * hs * (sig * (1.0 + g * (1.0 - sig)))).astype(BF16)
        dh = _scan(_shift_up(a, 1), dyv * (g * sig), sa_ref, sb_ref, sd_ref, reverse=True)
        da = dh * _shift_down(hs, 1)
        div = dh * nm
        a2 = a * a
        dla = da * a - (dh * iv) * (a2 / nm)
        dzr = (dla * ((-RGLRU_C) * sp)) * (r * (1.0 - r))
        dzi = (div * v) * (i * (1.0 - i))
        lam = lam_ref[...]
        dlam_ref[...] = _colsum(dla * ((-RGLRU_C) * r)) * (-_sigmoid(-lam))
        dba_ref[...] = _colsum(dzr)
        dbx_ref[...] = _colsum(dzi)
        dzr_b, dzi_b = dzr.astype(BF16), dzi.astype(BF16)
        tn = (((0,), (0,)), ((), ()))
        nt = (((1,), (1,)), ((), ()))
        dwa_ref[...] = lax.dot_general(vb, dzr_b, tn, preferred_element_type=F32)
        dwx_ref[...] = lax.dot_general(vb, dzi_b, tn, preferred_element_type=F32)
        dv = (div * i + lax.dot_general(dzr_b, wa, nt, preferred_element_type=F32)
              + lax.dot_general(dzi_b, wx, nt, preferred_element_type=F32))
        dcb_ref[...] = _colsum(dv)
        dvp = jnp.zeros_like(dv)
        for k in range(4):
            dvp = dvp + cw_ref[k:k + 1, :] * _shift_up(dv, 3 - k)
            dcw_ref[k:k + 1, :] = _colsum(dv * _shift_down(vp, 3 - k))
        dpg_ref[0] = dvp.astype(BF16)

    head_mat = pl.BlockSpec((None, c, c), lambda h: (h, 0, 0))
    outs = pl.pallas_call(
        body, name=name, grid=(n_heads,),
        in_specs=[sp_["pair"], sp_["plane"], sp_["conv_w"], sp_["chan"], sp_["w"], sp_["bias"],
                  sp_["w"], sp_["bias"], sp_["chan"]],
        out_specs=[sp_["pair"], head_mat, head_mat, sp_["bias"], sp_["bias"],
                   sp_["chan"], sp_["conv_w"], sp_["chan"]],
        out_shape=[jax.ShapeDtypeStruct((2, 2, s, e_half), BF16),
                   jax.ShapeDtypeStruct((n_heads, c, c), F32), jax.ShapeDtypeStruct((n_heads, c, c), F32),
                   jax.ShapeDtypeStruct((n_heads, 1, c), F32), jax.ShapeDtypeStruct((n_heads, 1, c), F32),
                   jax.ShapeDtypeStruct((1, e), F32), jax.ShapeDtypeStruct((4, e), F32),
                   jax.ShapeDtypeStruct((1, e), F32)],
        scratch_shapes=[pltpu.VMEM((s, c), F32)] * 4,
        compiler_params=_params(("parallel",)),
    )(proj, dy, conv_w, conv_b, w_a, b_a, w_x, b_x, lam)
    return tuple(outs)


def _ada_fwd(c_all, ada_w, name):
    n_l, d, f = ada_w.shape
    bf = _blk(f, 512)

    def body(c_ref, w_ref, o_ref):
        cv = c_ref[...]
        sc = (cv * _sigmoid(cv)).astype(BF16)
        o_ref[...] = jnp.dot(sc, w_ref[...].astype(BF16), preferred_element_type=F32)

    return pl.pallas_call(
        body, name=name, grid=(n_l, f // bf),
        in_specs=[pl.BlockSpec((8, d), lambda l, j: (0, 0)), pl.BlockSpec((None, d, bf), lambda l, j: (l, 0, j))],
        out_specs=pl.BlockSpec((None, 8, bf), lambda l, j: (l, 0, j)),
        out_shape=jax.ShapeDtypeStruct((n_l, 8, f), F32),
        compiler_params=_params(("parallel", "parallel")),
    )(c_all, ada_w)


def _ada_bwd_adamw(c_t, dmod, w, m, v, name):
    n_l, d, f = w.shape
    bf = _blk(f, 256)

    def body(c_ref, dm_ref, w_ref, m_ref, v_ref, g_ref, d_ref, m2_ref, v2_ref):
        cv = c_ref[...]
        sc = cv * _sigmoid(cv)
        dm = dm_ref[...]
        g = sc[:, 0:1] * dm[0:1, :]
        for b in range(1, 8):
            g = g + sc[:, b:b + 1] * dm[b:b + 1, :]
        g_ref[...] = g
        dl, m2, v2 = _adamw_math(w_ref[...], g, m_ref[...], v_ref[...])
        d_ref[...] = dl
        m2_ref[...] = m2
        v2_ref[...] = v2

    big = pl.BlockSpec((None, d, bf), lambda l, j: (l, 0, j))
    return pl.pallas_call(
        body, name=name, grid=(n_l, f // bf),
        in_specs=[pl.BlockSpec((d, 8), lambda l, j: (0, 0)), pl.BlockSpec((None, 8, bf), lambda l, j: (l, 0, j)),
                  big, big, big],
        out_specs=[big] * 4, out_shape=[jax.ShapeDtypeStruct((n_l, d, f), F32)] * 4,
        compiler_params=_params(("parallel", "parallel")),
    )(c_t, dmod, w, m, v)


def _pack(parts):
    padded, offs, n = [], [], 0
    for p in parts:
        p = p.reshape(-1)
        size = -(-p.shape[0] // PACK) * PACK
        offs.append(n)
        n += size
        padded.append(jnp.pad(p, (0, size - p.shape[0])) if size != p.shape[0] else p)
    return jnp.concatenate(padded), offs, n


def kernel(x, c, norm_g, ada_w, ada_b, sc_w_in, sc_conv_w, sc_w_out, lru_w_in, lru_conv_w, lru_conv_b, lru_w_a, lru_b_a, lru_w_x, lru_b_x, lru_lambda, lru_w_out, final_g, loss_target, m_norm_g, m_ada_w, m_ada_b, m_sc_w_in, m_sc_conv_w, m_sc_w_out, m_lru_w_in, m_lru_conv_w, m_lru_conv_b, m_lru_w_a, m_lru_b_a, m_lru_w_x, m_lru_b_x, m_lru_lambda, m_lru_w_out, m_final_g, v_norm_g, v_ada_w, v_ada_b, v_sc_w_in, v_sc_conv_w, v_sc_w_out, v_lru_w_in, v_lru_conv_w, v_lru_conv_b, v_lru_w_a, v_lru_b_a, v_lru_w_x, v_lru_b_x, v_lru_lambda, v_lru_w_out, v_final_g):
    xi, yi, ci = _pos()
    chip = 2 * xi + yi
    batch = 4 * xi + 2 * yi + ci
    core_op = jnp.reshape(ci, (1,)).astype(jnp.int32)
    chip_op = jnp.reshape(chip, (1,)).astype(jnp.int32)

    x2d, tgt = x[0], loss_target[0]
    s, d = x2d.shape
    es = sc_conv_w.shape[2]
    e = 4 * es
    n_heads = lru_w_a.shape[1]
    hj = lru_b_a.shape[2]
    f = ada_w.shape[2]
    row = lambda t: t.reshape(1, -1)

    small_parts = [c, sc_conv_w, lru_conv_w, lru_conv_b, lru_b_a, lru_b_x, lru_lambda]
    small, offs, n_small = _pack(small_parts)
    got = _allgather8([small.reshape(8, n_small // 8)], False, "ag_small")[0].reshape(8, n_small)
    c_all = got[:, :d]
    per_chip = got[0::2]

    def chip_part(k, shape):
        size = 1
        for dim in shape:
            size *= dim
        return per_chip[:, offs[k]:offs[k] + size].reshape((4,) + shape)

    conv_w0 = jnp.transpose(chip_part(1, (3, es)), (1, 0, 2)).reshape(3, e)
    conv_w1 = jnp.transpose(chip_part(2, (4, es)), (1, 0, 2)).reshape(4, e)
    conv_b1 = chip_part(3, (es,)).reshape(1, e)
    b_a = jnp.transpose(chip_part(4, (n_heads, hj)), (1, 0, 2)).reshape(n_heads, 1, 4 * hj)
    b_x = jnp.transpose(chip_part(5, (n_heads, hj)), (1, 0, 2)).reshape(n_heads, 1, 4 * hj)
    lam = chip_part(6, (es,)).reshape(1, e)

    shards = [sc_w_in[0], sc_w_out[0], lru_w_in[0], lru_w_a[0].reshape(n_heads * hj, HEAD_DIM),
              lru_w_x[0].reshape(n_heads * hj, HEAD_DIM), lru_w_out[0]]
    names = ["sc_w_in", "sc_w_out", "lru_w_in", "lru_w_a", "lru_w_x", "lru_w_out"]
    shards_bf = [_cast_bf16(w, "cast_" + nm) for w, nm in zip(shards, names)]
    full = _allgather8(shards_bf, True, "ag_weights")
    w_in0 = full[0].reshape(4, d, e)
    w_out0 = full[1].reshape(1, e, d)
    w_in1 = full[2].reshape(4, d, e // 2)
    w_a = full[3].reshape(4, n_heads, hj, HEAD_DIM)
    w_x = full[4].reshape(4, n_heads, hj, HEAD_DIM)
    w_out1 = full[5].reshape(1, e, d)

    mod_nb = _ada_fwd(c_all, ada_w, "ada_fwd")
    mods = _allgather8([mod_nb.reshape(16, f)], False, "ag_mod")[0].reshape(8, 2, 8, f)[0::2]
    mine = lax.dynamic_index_in_dim(mods, batch, axis=2, keepdims=False)
    mod = jnp.transpose(mine, (1, 0, 2)).reshape(2, 4 * f) + ada_b
    shift = [row(mod[l, :d]) for l in range(2)]
    scale = [row(mod[l, d:2 * d]) for l in range(2)]
    gate = [row(mod[l, 2 * d:]) for l in range(2)]
    ng = [row(norm_g[l]) for l in range(2)]

    h0 = _norm_mod_fwd(x2d, ng[0], scale[0], shift[0], "norm0")
    proj0 = _mm(h0[None], w_in0, "nn", BF16, "sc_in")
    y0 = _sc_fwd(proj0, conv_w0, "sc_mix")
    o0 = _mm(y0[None], w_out0, "nn", F32, "sc_out")[0]
    x1, h1 = _norm_mod_fwd(x2d, ng[1], scale[1], shift[1], "norm1", o=o0, gate=gate[0])
    proj1 = _mm(h1[None], w_in1, "nn", BF16, "lru_in")
    pairs1 = proj1.reshape(2, 2, s, e // 2)
    y1 = _lru_fwd(pairs1, conv_w1, conv_b1, w_a, b_a, w_x, b_x, lam, "lru_mix")
    o1 = _mm(y1[None], w_out1, "nn", F32, "lru_out")[0]
    dx2, do1, acc_f = _final_loss(x1, o1, gate[1], row(final_g), tgt, "final_loss")

    dy1 = _mm(do1[None], w_out1, "nt", BF16, "lru_out_dx")[0]
    g_w_out1 = _mm(y1[None], do1[None], "tn", F32, "lru_out_dw", bm=512, bn=2048)
    dpairs1, g_wa, g_wx, g_ba, g_bx, g_lam, g_cw1, g_cb1 = _lru_bwd(
        pairs1, dy1, conv_w1, conv_b1, w_a, b_a, w_x, b_x, lam, "lru_mix_bwd")
    dproj1 = dpairs1.reshape(4, s, e // 2)
    dh1 = _mm(dproj1, w_in1, "nt", F32, "lru_in_dx")[0]
    g_w_in1 = _mm(h1[None], dproj1, "tn", F32, "lru_in_dw", bm=512, bn=2048)
    dx1, do0, acc1 = _norm_mod_bwd(dh1, x1, ng[1], scale[1], dx2, "norm1_bwd", o_prev=o0, gate_prev=gate[0])

    dy0 = _mm(do0[None], w_out0, "nt", BF16, "sc_out_dx")[0]
    g_w_out0 = _mm(y0[None], do0[None], "tn", F32, "sc_out_dw", bm=512, bn=2048)
    dproj0, g_cw0 = _sc_bwd(proj0, dy0, conv_w0, "sc_mix_bwd")
    dh0 = _mm(dproj0, w_in0, "nt", F32, "sc_in_dx")[0]
    g_w_in0 = _mm(h0[None], dproj0, "tn", F32, "sc_in_dw", bm=512, bn=2048)
    grad_x, acc0 = _norm_mod_bwd(dh0, x2d, ng[0], scale[0], dx1, "norm0_bwd")

    def head_major_to_chip_major(t):
        return jnp.transpose(t.reshape(n_heads, 4, hj, HEAD_DIM), (1, 0, 2, 3)).reshape(4, n_heads * hj, HEAD_DIM)

    grads = [g_w_in0, g_w_out0.reshape(4, es, d), g_w_in1, head_major_to_chip_major(g_wa),
             head_major_to_chip_major(g_wx), g_w_out1.reshape(4, es, d)]
    from_sibling = _sibling_send_other_half(grads)
    parts = [_sum_own_and_sibling(core_op, g, r1, "rs_sum1_" + nm) for g, r1, nm in zip(grads, from_sibling, names)]
    from_chips = _chips_send_partials(parts)
    halves = [_sum_chips(chip_op, p, r2, "rs_sum2_" + nm) for p, r2, nm in zip(parts, from_chips, names)]
    g_full = _sibling_share_half(halves)

    big_m = [m_sc_w_in, m_sc_w_out, m_lru_w_in, m_lru_w_a, m_lru_w_x, m_lru_w_out]
    big_v = [v_sc_w_in, v_sc_w_out, v_lru_w_in, v_lru_w_a, v_lru_w_x, v_lru_w_out]
    big_w = [sc_w_in, sc_w_out, lru_w_in, lru_w_a, lru_w_x, lru_w_out]
    big = {}
    for nm, w2, g2, m4, v4, w4 in zip(names, shards, g_full, big_m, big_v, big_w):
        dl, m2, v2 = _adamw(w2, g2, m4.reshape(w2.shape), v4.reshape(w2.shape), "adamw_" + nm)
        big[nm] = tuple(t.reshape(w4.shape) for t in (g2, dl, m2, v2))

    dmod = jnp.stack([jnp.concatenate([acc0[1], acc0[0], acc1[3]]),
                      jnp.concatenate([acc1[1], acc1[0], acc_f[1]])])
    part_list = [jnp.stack([acc0[2], acc1[2]]), acc_f[0], acc_f[2, :1], g_cw0, g_cw1, g_cb1, g_ba, g_bx,
                 g_lam, dmod]
    partials, poffs, n_part = _pack(part_list)
    every = _allgather8([partials.reshape(8, n_part // 8)], False, "ag_partials")[0].reshape(8, n_part)
    total = _sum_rows8(every, "sum_partials")[0]

    def tot(k, shape):
        size = 1
        for dim in shape:
            size *= dim
        return total[poffs[k]:poffs[k] + size].reshape(shape)

    def my_cols(t, width):
        return lax.dynamic_slice_in_dim(t, chip * width, width, axis=t.ndim - 1)

    loss = tot(2, (1,))[0]
    g_norm_g, g_final_g, g_ada_b = tot(0, (2, d)), tot(1, (d,)), tot(9, (2, 3 * d))
    g_sc_conv_w = my_cols(tot(3, (3, e)), es)[None]
    g_lru_conv_w = my_cols(tot(4, (4, e)), es)[None]
    g_lru_conv_b = my_cols(tot(5, (1, e)), es)
    g_lru_b_a = my_cols(tot(6, (n_heads, 4 * hj)), hj)[None]
    g_lru_b_x = my_cols(tot(7, (n_heads, 4 * hj)), hj)[None]
    g_lru_lambda = my_cols(tot(8, (1, e)), es)

    dmod_all = every[:, poffs[9]:poffs[9] + 6 * d].reshape(8, 2, 3 * d)
    dmod_mine = jnp.transpose(my_cols(dmod_all, f), (1, 0, 2))
    ada = _ada_bwd_adamw(jnp.transpose(c_all), dmod_mine, ada_w, m_ada_w, v_ada_w, "ada_bwd_adamw")

    small_names = ["norm_g", "ada_b", "final_g", "sc_conv_w", "lru_conv_w", "lru_conv_b", "lru_b_a",
                   "lru_b_x", "lru_lambda"]
    small_w = [norm_g, ada_b, final_g, sc_conv_w, lru_conv_w, lru_conv_b, lru_b_a, lru_b_x, lru_lambda]
    small_g = [g_norm_g, g_ada_b, g_final_g, g_sc_conv_w, g_lru_conv_w, g_lru_conv_b, g_lru_b_a,
               g_lru_b_x, g_lru_lambda]
    small_m = [m_norm_g, m_ada_b, m_final_g, m_sc_conv_w, m_lru_conv_w, m_lru_conv_b, m_lru_b_a,
               m_lru_b_x, m_lru_lambda]
    small_v = [v_norm_g, v_ada_b, v_final_g, v_sc_conv_w, v_lru_conv_w, v_lru_conv_b, v_lru_b_a,
               v_lru_b_x, v_lru_lambda]
    pw, soffs, n_s = _pack(small_w)
    pg, pm, pv = _pack(small_g)[0], _pack(small_m)[0], _pack(small_v)[0]
    shape2 = (n_s // PACK, PACK)
    pd, pm2, pv2 = _adamw(pw.reshape(shape2), pg.reshape(shape2), pm.reshape(shape2), pv.reshape(shape2),
                          "adamw_small")
    small = {}
    for k, (nm, w_) in enumerate(zip(small_names, small_w)):
        take = lambda t: t.reshape(-1)[soffs[k]:soffs[k] + w_.size].reshape(w_.shape)
        small[nm] = (small_g[k].reshape(w_.shape), take(pd), take(pm2), take(pv2))

    results = dict(small)
    results.update(big)
    results["ada_w"] = tuple(ada)
    order = ["norm_g", "ada_w", "ada_b", "sc_w_in", "sc_conv_w", "sc_w_out", "lru_w_in", "lru_conv_w",
             "lru_conv_b", "lru_w_a", "lru_b_a", "lru_w_x", "lru_b_x", "lru_lambda", "lru_w_out", "final_g"]
    out = [loss, grad_x[None]]
    for kind in range(4):
        out += [results[nm][kind] for nm in order]
    return tuple(out)
```

```python
import functools

import jax
import jax.numpy as jnp
from jax import lax
from jax.experimental import pallas as pl
from jax.experimental.pallas import tpu as pltpu

F32 = jnp.float32
BF16 = jnp.bfloat16
MESH = pl.DeviceIdType.MESH
ANY = pl.BlockSpec(memory_space=pl.ANY)

RMS_EPS = 1e-6
RGLRU_C = 8.0
HEAD_DIM = 256
ADAM_LR = 0.001
ADAM_B1 = 0.9
ADAM_B2 = 0.999
ADAM_EPS = 1e-08
ADAM_WD = 0.01
ADAM_STEP = 10
V7X_VMEM_LIMIT = 56 * 1024 * 1024
LANES = 128
PACK = 8 * LANES


def _blk(dim, pref, unit=LANES):
    if dim <= pref:
        return dim
    b = (pref // unit) * unit
    while b > unit and dim % b:
        b -= unit
    assert dim % b == 0, (dim, pref, unit)
    return b


def _params(sem=None):
    return pltpu.CompilerParams(dimension_semantics=sem, vmem_limit_bytes=V7X_VMEM_LIMIT)


def _pos():
    return lax.axis_index("x"), lax.axis_index("y"), lax.axis_index("c")


def _other_chips(x, y):
    return [(1 - x, y), (x, 1 - y), (1 - x, 1 - y)]


def _allgather8(arrs, from_half, name):
    n_t = len(arrs)
    ms = [a.shape[0] // 2 if from_half else a.shape[0] for a in arrs]

    def body(*refs):
        ins, outs = refs[:n_t], refs[n_t:2 * n_t]
        send_sems, recv_sems, local_sems = refs[2 * n_t:]
        x, y, c = _pos()
        me, sibling = (x, y, c), (x, y, 1 - c)
        chips = _other_chips(x, y)

        def rows(t, px, py, pc):
            return outs[t].at[pl.ds((4 * px + 2 * py + pc) * ms[t], ms[t])]

        def copy(t, k, block, to, src=None):
            return pltpu.make_async_remote_copy(
                src_ref=rows(t, *block) if src is None else src, dst_ref=rows(t, *block),
                send_sem=send_sems.at[7 * t + k], recv_sem=recv_sems.at[7 * t + k],
                device_id=to, device_id_type=MESH)

        mine, first, passed = [], [], []
        for t in range(n_t):
            src = ins[t].at[pl.ds(c * ms[t], ms[t])] if from_half else ins[t]
            cp = pltpu.make_async_copy(src, rows(t, *me), local_sems.at[t])
            cp.start()
            mine.append(cp)
            sends = [copy(t, 0, me, sibling, src=src)]
            sends += [copy(t, 1 + j, me, (*chip, c), src=src) for j, chip in enumerate(chips)]
            for cp in sends:
                cp.start()
            first += sends
        for t in range(n_t):
            for j, chip in enumerate(chips):
                copy(t, 1 + j, (*chip, c), me).wait_recv()
                cp = copy(t, 4 + j, (*chip, c), sibling)
                cp.start()
                passed.append(cp)
        for t in range(n_t):
            copy(t, 0, sibling, me).wait_recv()
            for j, chip in enumerate(chips):
                copy(t, 4 + j, (*chip, 1 - c), me).wait_recv()
        for cp in first + passed:
            cp.wait_send()
        for cp in mine:
            cp.wait()

    return pl.pallas_call(
        body, name=name,
        out_shape=[jax.ShapeDtypeStruct((8 * m, a.shape[1]), a.dtype) for m, a in zip(ms, arrs)],
        in_specs=[ANY] * n_t, out_specs=[ANY] * n_t,
        scratch_shapes=[pltpu.SemaphoreType.DMA((7 * n_t,)), pltpu.SemaphoreType.DMA((7 * n_t,)),
                        pltpu.SemaphoreType.DMA((n_t,))],
    )(*arrs)


def _allgather_shards(bufs, name):
    n_t = len(bufs)
    hs = [b.shape[0] // 8 for b in bufs]

    def body(*refs):
        outs = refs[n_t:2 * n_t]
        send_sems, recv_sems = refs[2 * n_t:]
        x, y, c = _pos()
        me, sibling = (x, y, c), (x, y, 1 - c)
        chips = _other_chips(x, y)

        def copy(t, k, block, to):
            px, py, pc = block
            rows = outs[t].at[pl.ds((4 * px + 2 * py + pc) * hs[t], hs[t])]
            return pltpu.make_async_remote_copy(
                src_ref=rows, dst_ref=rows, send_sem=send_sems.at[6 * t + k],
                recv_sem=recv_sems.at[6 * t + k], device_id=to, device_id_type=MESH)

        started = []
        for t in range(n_t):
            for j, chip in enumerate(chips):
                cp = copy(t, j, me, (*chip, c))
                cp.start()
                started.append(cp)
        for t in range(n_t):
            for j, chip in enumerate(chips):
                copy(t, j, (*chip, c), me).wait_recv()
                cp = copy(t, 3 + j, (*chip, c), sibling)
                cp.start()
                started.append(cp)
        for t in range(n_t):
            for j, chip in enumerate(chips):
                copy(t, 3 + j, (*chip, 1 - c), me).wait_recv()
        for cp in started:
            cp.wait_send()

    return pl.pallas_call(
        body, name=name,
        out_shape=[jax.ShapeDtypeStruct(b.shape, b.dtype) for b in bufs],
        in_specs=[ANY] * n_t, out_specs=[ANY] * n_t,
        input_output_aliases={t: t for t in range(n_t)},
        scratch_shapes=[pltpu.SemaphoreType.DMA((6 * n_t,)), pltpu.SemaphoreType.DMA((6 * n_t,))],
    )(*bufs)


def _exchange(name, ins, out_shapes, n_copies, plan, aliases=None):
    ni, no = len(ins), len(out_shapes)

    def body(*refs):
        in_refs, out_refs = refs[:ni], refs[ni:ni + no]
        send_sems, recv_sems, local_sems = refs[ni + no:]
        x, y, c = _pos()
        copies = []
        for k, (src, dst, dev) in enumerate(plan(in_refs, out_refs, x, y, c)):
            if dev is None:
                cp = pltpu.make_async_copy(src, dst, local_sems.at[k])
            else:
                cp = pltpu.make_async_remote_copy(
                    src_ref=src, dst_ref=dst, send_sem=send_sems.at[k], recv_sem=recv_sems.at[k],
                    device_id=dev, device_id_type=MESH)
            cp.start()
            copies.append(cp)
        assert len(copies) == n_copies
        for cp in copies:
            cp.wait()

    return pl.pallas_call(
        body, name=name, out_shape=out_shapes,
        in_specs=[ANY] * ni, out_specs=[ANY] * no, input_output_aliases=aliases or {},
        scratch_shapes=[pltpu.SemaphoreType.DMA((n_copies,)), pltpu.SemaphoreType.DMA((n_copies,)),
                        pltpu.SemaphoreType.DMA((n_copies,))],
    )(*ins)


def _sibling_send_other_half(grads):
    def plan(in_refs, out_refs, x, y, c):
        out = []
        for g_ref, r_ref in zip(in_refs, out_refs):
            h = g_ref.shape[1] // 2
            out.append((g_ref.at[:, pl.ds((1 - c) * h, h), :], r_ref, (x, y, 1 - c)))
        return out

    shapes = [jax.ShapeDtypeStruct((4, g.shape[1] // 2, g.shape[2]), g.dtype) for g in grads]
    return _exchange("rs_sibling", grads, shapes, len(grads), plan)


def _chips_send_partials(parts):
    def plan(in_refs, out_refs, x, y, c):
        out = []
        for p_ref, r_ref in zip(in_refs, out_refs):
            for j, (px, py) in enumerate(_other_chips(x, y)):
                out.append((p_ref.at[2 * px + py], r_ref.at[j], (px, py, c)))
        return out

    shapes = [jax.ShapeDtypeStruct((3,) + p.shape[1:], p.dtype) for p in parts]
    return _exchange("rs_chips", parts, shapes, 3 * len(parts), plan)


def _sibling_share_half(fulls):
    def plan(in_refs, out_refs, x, y, c):
        out = []
        for f_ref in out_refs:
            h = f_ref.shape[0] // 2
            rows = f_ref.at[pl.ds(c * h, h)]
            out.append((rows, rows, (x, y, 1 - c)))
        return out

    shapes = [jax.ShapeDtypeStruct(f.shape, f.dtype) for f in fulls]
    return _exchange("rs_share", fulls, shapes, len(fulls), plan, aliases={t: t for t in range(len(fulls))})


def _cast_into_slot(chip, w, name):
    r, c = w.shape
    br, bc = _blk(r, 512, 8), _blk(c, 2048)
    nb = r // br

    def body(chip_ref, w_ref, o_ref):
        o_ref[...] = w_ref[...].astype(BF16)

    grid_spec = pltpu.PrefetchScalarGridSpec(
        num_scalar_prefetch=1, grid=(nb, c // bc),
        in_specs=[pl.BlockSpec((br, bc), lambda i, j, chip_ref: (i, j))],
        out_specs=pl.BlockSpec((br, bc), lambda i, j, chip_ref: (chip_ref[0] * nb + i, j)))
    return pl.pallas_call(
        body, name=name, grid_spec=grid_spec,
        out_shape=jax.ShapeDtypeStruct((4 * r, c), BF16),
        compiler_params=_params(("parallel", "parallel")),
    )(chip, w)


def _sum_own_and_sibling(core, g, r1, name):
    _, r, c = g.shape
    h = r // 2
    br, bc = _blk(h, 256, 8), _blk(c, 2048)
    nb = h // br

    def body(core_ref, g_ref, r_ref, o_ref):
        o_ref[...] = (g_ref[...] + r_ref[...]).astype(BF16)

    grid_spec = pltpu.PrefetchScalarGridSpec(
        num_scalar_prefetch=1, grid=(4, nb, c // bc),
        in_specs=[pl.BlockSpec((None, br, bc), lambda k, i, j, core_ref: (k, core_ref[0] * nb + i, j)),
                  pl.BlockSpec((None, br, bc), lambda k, i, j, core_ref: (k, i, j))],
        out_specs=pl.BlockSpec((None, br, bc), lambda k, i, j, core_ref: (k, i, j)))
    return pl.pallas_call(
        body, name=name, grid_spec=grid_spec,
        out_shape=jax.ShapeDtypeStruct((4, h, c), BF16),
        compiler_params=_params(("parallel", "parallel", "parallel")),
    )(core, g, r1)


def _sum_chips(where, p, r2, name):
    _, h, c = p.shape
    br, bc = _blk(h, 256, 8), _blk(c, 2048)
    nb = h // br

    def body(where_ref, p_ref, r_ref, o_ref):
        acc = p_ref[...].astype(F32)
        for j in range(3):
            acc = acc + r_ref[j].astype(F32)
        o_ref[...] = acc

    grid_spec = pltpu.PrefetchScalarGridSpec(
        num_scalar_prefetch=1, grid=(nb, c // bc),
        in_specs=[pl.BlockSpec((None, br, bc), lambda i, j, where_ref: (where_ref[0], i, j)),
                  pl.BlockSpec((3, br, bc), lambda i, j, where_ref: (0, i, j))],
        out_specs=pl.BlockSpec((br, bc), lambda i, j, where_ref: (where_ref[1] * nb + i, j)))
    return pl.pallas_call(
        body, name=name, grid_spec=grid_spec,
        out_shape=jax.ShapeDtypeStruct((2 * h, c), F32),
        compiler_params=_params(("parallel", "parallel")),
    )(where, p, r2)


def _adamw_math(w, g, m, v):
    m2 = ADAM_B1 * m + (1.0 - ADAM_B1) * g
    v2 = ADAM_B2 * v + (1.0 - ADAM_B2) * (g * g)
    m_hat = m2 / (1.0 - ADAM_B1 ** ADAM_STEP)
    v_hat = v2 / (1.0 - ADAM_B2 ** ADAM_STEP)
    delta = -ADAM_LR * (m_hat / (jnp.sqrt(v_hat) + ADAM_EPS) + ADAM_WD * w)
    return delta, m2, v2


def _adamw(w, g, m, v, name):
    r, c = w.shape
    br, bc = _blk(r, 128, 8), _blk(c, 2048)

    def body(w_ref, g_ref, m_ref, v_ref, d_ref, m2_ref, v2_ref):
        d, m2, v2 = _adamw_math(w_ref[...], g_ref[...], m_ref[...], v_ref[...])
        d_ref[...] = d
        m2_ref[...] = m2
        v2_ref[...] = v2

    spec = pl.BlockSpec((br, bc), lambda i, j: (i, j))
    return pl.pallas_call(
        body, name=name, grid=(r // br, c // bc),
        in_specs=[spec] * 4, out_specs=[spec] * 3,
        out_shape=[jax.ShapeDtypeStruct((r, c), F32)] * 3,
        compiler_params=_params(("parallel", "parallel")),
    )(w, g, m, v)


def _sum_rows8(g, name):
    n = g.shape[1]

    def body(g_ref, o_ref):
        acc = g_ref[0:1, :]
        for k in range(1, 8):
            acc = acc + g_ref[k:k + 1, :]
        o_ref[...] = acc

    return pl.pallas_call(
        body, name=name, out_shape=jax.ShapeDtypeStruct((1, n), F32),
        in_specs=[pl.BlockSpec(memory_space=pltpu.VMEM)],
        out_specs=pl.BlockSpec(memory_space=pltpu.VMEM),
        compiler_params=_params(),
    )(g)


def _mm(a, b, mode, out_dtype, name, bm=1024, bn=None):
    if mode == "nn":
        (_, m, k), (g, _, n) = a.shape, b.shape
    elif mode == "tn":
        (_, k, m), (g, _, n) = a.shape, b.shape
    else:
        (g, m, k), (_, n, _) = a.shape, b.shape
    if bn is None:
        bn = 1024 if k <= 2048 else 512
    bm, bn = _blk(m, bm), _blk(n, bn)

    if mode == "nt":
        def body(a_ref, b_ref, o_ref, acc_ref):
            part = lax.dot_general(a_ref[...], b_ref[...], (((1,), (1,)), ((), ())),
                                   preferred_element_type=F32)
            if g == 1:
                o_ref[...] = part.astype(out_dtype)
            else:
                gi = pl.program_id(2)

                @pl.when(gi == 0)
                def _():
                    acc_ref[...] = part

                @pl.when(gi > 0)
                def _():
                    acc_ref[...] += part

                @pl.when(gi == g - 1)
                def _():
                    o_ref[...] = acc_ref[...].astype(out_dtype)

        return pl.pallas_call(
            body, name=name, grid=(m // bm, n // bn, g),
            in_specs=[pl.BlockSpec((None, bm, k), lambda i, j, gi: (gi, i, 0)),
                      pl.BlockSpec((None, bn, k), lambda i, j, gi: (gi, j, 0))],
            out_specs=pl.BlockSpec((None, bm, bn), lambda i, j, gi: (0, i, j)),
            out_shape=jax.ShapeDtypeStruct((1, m, n), out_dtype),
            scratch_shapes=[pltpu.VMEM((bm, bn), F32)],
            compiler_params=_params(("parallel", "parallel", "arbitrary")),
        )(a, b)

    contract = (((1,), (0,)), ((), ())) if mode == "nn" else (((0,), (0,)), ((), ()))

    def body(a_ref, b_ref, o_ref):
        o_ref[...] = lax.dot_general(a_ref[...], b_ref[...], contract,
                                     preferred_element_type=F32).astype(out_dtype)

    a_spec = (pl.BlockSpec((None, bm, k), lambda i, gi, j: (0, i, 0)) if mode == "nn"
              else pl.BlockSpec((None, k, bm), lambda i, gi, j: (0, 0, i)))
    return pl.pallas_call(
        body, name=name, grid=(m // bm, g, n // bn),
        in_specs=[a_spec, pl.BlockSpec((None, k, bn), lambda i, gi, j: (gi, 0, j))],
        out_specs=pl.BlockSpec((None, bm, bn), lambda i, gi, j: (gi, i, j)),
        out_shape=jax.ShapeDtypeStruct((g, m, n), out_dtype),
        compiler_params=_params(("parallel", "parallel", "parallel")),
    )(a, b)


def _row_specs(br, d):
    return (pl.BlockSpec((br, d), lambda i: (i, 0)), pl.BlockSpec((1, d), lambda i: (0, 0)),
            pl.BlockSpec((8, d), lambda i: (0, 0)))


def _rstd(xv):
    return lax.rsqrt(jnp.mean(xv * xv, axis=-1, keepdims=True) + RMS_EPS)


def _colsum(v):
    return jnp.sum(v, axis=0, keepdims=True)


def _norm_mod_fwd(x, g, scale, shift, name, o=None, gate=None):
    s, d = x.shape
    br = _blk(s, 256, 8)
    has_res = o is not None
    row, vec, _ = _row_specs(br, d)

    def body(*refs):
        if has_res:
            x_ref, o_ref, gate_ref, g_ref, sc_ref, sh_ref, x1_ref, h_ref = refs
            xv = x_ref[...] + gate_ref[...] * o_ref[...]
            x1_ref[...] = xv
        else:
            x_ref, g_ref, sc_ref, sh_ref, h_ref = refs
            xv = x_ref[...]
        n = xv * _rstd(xv) * g_ref[...]
        h_ref[...] = (n * (1.0 + sc_ref[...]) + sh_ref[...]).astype(BF16)

    ins = [x] + ([o, gate] if has_res else []) + [g, scale, shift]
    in_specs = [row] + ([row, vec] if has_res else []) + [vec] * 3
    out_shape = ([jax.ShapeDtypeStruct((s, d), F32)] if has_res else []) + [jax.ShapeDtypeStruct((s, d), BF16)]
    out = pl.pallas_call(
        body, name=name, grid=(s // br,), in_specs=in_specs, out_specs=[row] * len(out_shape),
        out_shape=out_shape, compiler_params=_params(("parallel",)),
    )(*ins)
    return out if has_res else out[0]


def _final_loss(x1, o1, gate1, final_g, tgt, name):
    s, d = x1.shape
    br = _blk(s, 256, 8)
    row, vec, acc = _row_specs(br, d)

    def body(x1_ref, o_ref, gate_ref, g_ref, t_ref, dx_ref, do_ref, acc_ref):
        @pl.when(pl.program_id(0) == 0)
        def _():
            acc_ref[...] = jnp.zeros_like(acc_ref)

        gate, o, g = gate_ref[...], o_ref[...], g_ref[...]
        x2 = x1_ref[...] + gate * o
        r = _rstd(x2)
        xh = x2 * r
        err = xh * g - t_ref[...]
        loss = 0.5 * _colsum(jnp.mean(err * err, axis=-1, keepdims=True))
        dout = err * (1.0 / d)
        dxh = dout * g
        dx2 = r * (dxh - xh * jnp.mean(dxh * xh, axis=-1, keepdims=True))
        dx_ref[...] = dx2
        do_ref[...] = (dx2 * gate).astype(BF16)
        acc_ref[0:1, :] += _colsum(dout * xh)
        acc_ref[1:2, :] += _colsum(dx2 * o)
        acc_ref[2:3, :] += jnp.broadcast_to(loss, (1, d))

    return pl.pallas_call(
        body, name=name, grid=(s // br,),
        in_specs=[row, row, vec, vec, row], out_specs=[row, row, acc],
        out_shape=[jax.ShapeDtypeStruct((s, d), F32), jax.ShapeDtypeStruct((s, d), BF16),
                   jax.ShapeDtypeStruct((8, d), F32)],
        compiler_params=_params(("arbitrary",)),
    )(x1, o1, gate1, final_g, tgt)


def _norm_mod_bwd(dh, x, g, scale, dx_next, name, o_prev=None, gate_prev=None):
    s, d = x.shape
    br = _blk(s, 256, 8)
    has_prev = o_prev is not None
    row, vec, acc = _row_specs(br, d)

    def body(*refs):
        if has_prev:
            dh_ref, x_ref, g_ref, sc_ref, dxn_ref, o_ref, gate_ref, dx_ref, do_ref, acc_ref = refs
        else:
            dh_ref, x_ref, g_ref, sc_ref, dxn_ref, dx_ref, acc_ref = refs

        @pl.when(pl.program_id(0) == 0)
        def _():
            acc_ref[...] = jnp.zeros_like(acc_ref)

        xv, gv, dhv = x_ref[...], g_ref[...], dh_ref[...]
        r = _rstd(xv)
        xh = xv * r
        acc_ref[0:1, :] += _colsum(dhv * (xh * gv))
        acc_ref[1:2, :] += _colsum(dhv)
        dn = dhv * (1.0 + sc_ref[...])
        acc_ref[2:3, :] += _colsum(dn * xh)
        dxh = dn * gv
        dx = dxn_ref[...] + r * (dxh - xh * jnp.mean(dxh * xh, axis=-1, keepdims=True))
        dx_ref[...] = dx
        if has_prev:
            acc_ref[3:4, :] += _colsum(dx * o_ref[...])
            do_ref[...] = (dx * gate_ref[...]).astype(BF16)

    ins = [dh, x, g, scale, dx_next] + ([o_prev, gate_prev] if has_prev else [])
    in_specs = [row, row, vec, vec, row] + ([row, vec] if has_prev else [])
    out_shape = [jax.ShapeDtypeStruct((s, d), F32)]
    out_specs = [row]
    if has_prev:
        out_shape.append(jax.ShapeDtypeStruct((s, d), BF16))
        out_specs.append(row)
    out_shape.append(jax.ShapeDtypeStruct((8, d), F32))
    out_specs.append(acc)
    return pl.pallas_call(
        body, name=name, grid=(s // br,), in_specs=in_specs, out_specs=out_specs,
        out_shape=out_shape, compiler_params=_params(("arbitrary",)),
    )(*ins)


def _shift_down(p, k):
    if k == 0:
        return p
    rows = lax.broadcasted_iota(jnp.int32, p.shape, 0)
    return jnp.where(rows >= k, pltpu.roll(p, k, 0), 0.0)


def _shift_up(p, k):
    if k == 0:
        return p
    s = p.shape[0]
    rows = lax.broadcasted_iota(jnp.int32, p.shape, 0)
    return jnp.where(rows < s - k, pltpu.roll(p, s - k, 0), 0.0)


def _sigmoid(z):
    return 1.0 / (1.0 + jnp.exp(-z))


def _sc_parts(proj_ref, w_ref):
    b, cg, v, g = (proj_ref[i].astype(F32) for i in range(4))
    p = cg * v
    u = w_ref[2:3, :] * p + w_ref[1:2, :] * _shift_down(p, 1) + w_ref[0:1, :] * _shift_down(p, 2)
    return b, cg, v, g, p, u


def _sc_fwd(proj, conv_w, name):
    _, s, e = proj.shape
    bc = _blk(e, 256)

    def body(proj_ref, w_ref, y_ref):
        b, _, _, g, _, u = _sc_parts(proj_ref, w_ref)
        y_ref[...] = (b * u * (g * _sigmoid(g))).astype(BF16)

    return pl.pallas_call(
        body, name=name, grid=(e // bc,),
        in_specs=[pl.BlockSpec((4, s, bc), lambda j: (0, 0, j)), pl.BlockSpec((3, bc), lambda j: (0, j))],
        out_specs=pl.BlockSpec((s, bc), lambda j: (0, j)),
        out_shape=jax.ShapeDtypeStruct((s, e), BF16),
        compiler_params=_params(("parallel",)),
    )(proj, conv_w)


def _sc_bwd(proj, dy, conv_w, name):
    _, s, e = proj.shape
    bc = _blk(e, 256)

    def body(proj_ref, dy_ref, w_ref, dp_ref, dw_ref):
        b, cg, v, g, p, u = _sc_parts(proj_ref, w_ref)
        dyv = dy_ref[...].astype(F32)
        sig = _sigmoid(g)
        t = dyv * (g * sig)
        du = t * b
        dp_ref[0] = (t * u).astype(BF16)
        dp_ref[3] = (dyv * b * u * (sig * (1.0 + g * (1.0 - sig)))).astype(BF16)
        dpp = w_ref[2:3, :] * du + w_ref[1:2, :] * _shift_up(du, 1) + w_ref[0:1, :] * _shift_up(du, 2)
        dp_ref[1] = (dpp * v).astype(BF16)
        dp_ref[2] = (dpp * cg).astype(BF16)
        dw_ref[2:3, :] = _colsum(du * p)
        dw_ref[1:2, :] = _colsum(du * _shift_down(p, 1))
        dw_ref[0:1, :] = _colsum(du * _shift_down(p, 2))

    return pl.pallas_call(
        body, name=name, grid=(e // bc,),
        in_specs=[pl.BlockSpec((4, s, bc), lambda j: (0, 0, j)), pl.BlockSpec((s, bc), lambda j: (0, j)),
                  pl.BlockSpec((3, bc), lambda j: (0, j))],
        out_specs=[pl.BlockSpec((4, s, bc), lambda j: (0, 0, j)), pl.BlockSpec((3, bc), lambda j: (0, j))],
        out_shape=[jax.ShapeDtypeStruct((4, s, e), BF16), jax.ShapeDtypeStruct((3, e), F32)],
        compiler_params=_params(("parallel",)),
    )(proj, dy, conv_w)


def _softplus_neg(lam):
    u = jnp.exp(-jnp.abs(lam))
    w = 1.0 + u
    log1p = jnp.where(w == 1.0, u, jnp.log(w) * (u / jnp.where(w == 1.0, 1.0, w - 1.0)))
    return jnp.maximum(-lam, 0.0) + log1p


def _one_minus_exp(z):
    series = -z * (1.0 + z * (0.5 + z * (1.0 / 6.0 + z * (1.0 / 24.0))))
    return jnp.where(z > -0.02, series, 1.0 - jnp.exp(z))


def _scan_in_tiles(a, b, reverse):
    s = a.shape[0]
    rows = lax.broadcasted_iota(jnp.int32, a.shape, 0) & 7
    for step in (1, 2, 4):
        if reverse:
            ok = rows < 8 - step
            a_s, b_s = pltpu.roll(a, s - step, 0), pltpu.roll(b, s - step, 0)
        else:
            ok = rows >= step
            a_s, b_s = pltpu.roll(a, step, 0), pltpu.roll(b, step, 0)
        b = jnp.where(ok, a * b_s + b, b)
        a = jnp.where(ok, a * a_s, a)
    return a, b


def _scan(a, b, a_ref, b_ref, h_ref, reverse):
    s, c = a.shape
    n = s // 8
    a_t, b_t = _scan_in_tiles(a, b, reverse)
    a_ref[...] = a_t
    b_ref[...] = b_t

    def step(i, carry):
        gi = n - 1 - i if reverse else i
        sl = pl.ds(pl.multiple_of(gi * 8, 8), 8)
        h = b_ref[sl, :] + a_ref[sl, :] * carry
        h_ref[sl, :] = h
        return h[0:1, :] if reverse else h[7:8, :]

    lax.fori_loop(0, n, step, jnp.zeros((1, c), F32))
    return h_ref[...]


def _lru_specs(s, e_half, n_heads):
    hp = e_half // HEAD_DIM
    c = HEAD_DIM
    return dict(
        pair=pl.BlockSpec((2, None, s, c), lambda h: (0, h // hp, 0, h % hp)),
        conv_w=pl.BlockSpec((4, c), lambda h: (0, h)),
        chan=pl.BlockSpec((1, c), lambda h: (0, h)),
        w=pl.BlockSpec((4, None, c // 4, c), lambda h: (0, h, 0, 0)),
        bias=pl.BlockSpec((None, 1, c), lambda h: (h, 0, 0)),
        plane=pl.BlockSpec((s, c), lambda h: (0, h)),
    )


def _lru_forward_parts(vp, cw_ref, cb_ref, wa_ref, ba_ref, wx_ref, bx_ref, lam_ref):
    c = HEAD_DIM
    v = cb_ref[...] + sum(cw_ref[k:k + 1, :] * _shift_down(vp, 3 - k) for k in range(4))
    vb = v.astype(BF16)
    wa = wa_ref[...].reshape(c, c)
    wx = wx_ref[...].reshape(c, c)
    r = _sigmoid(jnp.dot(vb, wa, preferred_element_type=F32) + ba_ref[...])
    i = _sigmoid(jnp.dot(vb, wx, preferred_element_type=F32) + bx_ref[...])
    sp = _softplus_neg(lam_ref[...])
    la = (-RGLRU_C) * sp * r
    a = jnp.exp(la)
    nm = jnp.sqrt(_one_minus_exp(2.0 * la))
    return v, vb, wa, wx, r, i, sp, a, nm


def _lru_fwd(proj, conv_w, conv_b, w_a, b_a, w_x, b_x, lam, name):
    _, _, s, e_half = proj.shape
    n_heads = 2 * e_half // HEAD_DIM
    sp_ = _lru_specs(s, e_half, n_heads)

    def body(pg_ref, cw_ref, cb_ref, wa_ref, ba_ref, wx_ref, bx_ref, lam_ref, y_ref,
             sa_ref, sb_ref, sh_ref):
        v, _, _, _, _, i, _, a, nm = _lru_forward_parts(
            pg_ref[0].astype(F32), cw_ref, cb_ref, wa_ref, ba_ref, wx_ref, bx_ref, lam_ref)
        hs = _scan(a, nm * (i * v), sa_ref, sb_ref, sh_ref, reverse=False)
        g = pg_ref[1].astype(F32)
        y_ref[...] = (hs * (g * _sigmoid(g))).astype(BF16)

    return pl.pallas_call(
        body, name=name, grid=(n_heads,),
        in_specs=[sp_["pair"], sp_["conv_w"], sp_["chan"], sp_["w"], sp_["bias"], sp_["w"],
                  sp_["bias"], sp_["chan"]],
        out_specs=sp_["plane"],
        out_shape=jax.ShapeDtypeStruct((s, 2 * e_half), BF16),
        scratch_shapes=[pltpu.VMEM((s, HEAD_DIM), F32)] * 3,
        compiler_params=_params(("parallel",)),
    )(proj, conv_w, conv_b, w_a, b_a, w_x, b_x, lam)


def _lru_bwd(proj, dy, conv_w, conv_b, w_a, b_a, w_x, b_x, lam, name):
    _, _, s, e_half = proj.shape
    e = 2 * e_half
    c = HEAD_DIM
    n_heads = e // c
    hp = e_half // c
    sp_ = _lru_specs(s, e_half, n_heads)

    def body(pg_ref, dy_ref, cw_ref, cb_ref, wa_ref, ba_ref, wx_ref, bx_ref, lam_ref,
             dpg_ref, dwa_ref, dwx_ref, dba_ref, dbx_ref, dlam_ref, dcw_ref, dcb_ref,
             sa_ref, sb_ref, sh_ref, sd_ref):
        vp = pg_ref[0].astype(F32)
        v, vb, wa, wx, r, i, sp, a, nm = _lru_forward_parts(
            vp, cw_ref, cb_ref, wa_ref, ba_ref, wx_ref, bx_ref, lam_ref)
        iv = i * v
        hs = _scan(a, nm * iv, sa_ref, sb_ref, sh_ref, reverse=False)
        g = pg_ref[1].astype(F32)
        dyv = dy_ref[...].astype(F32)
        sig = _sigmoid(g)
        dpg_ref[1] = (dyv * hs * (sig * (1.0 + g * (1.0 - sig)))).astype(BF16)
        dh = _scan(_shift_up(a, 1), dyv * (g * sig), sa_ref, sb_ref, sd_ref, reverse=True)
        da = dh * _shift_down(hs, 1)
        div = dh * nm
        a2 = a * a
        dla = da * a - (dh * iv) * (a2 / nm)
        dzr = (dla * ((-RGLRU_C) * sp)) * (r * (1.0 - r))
        dzi = (div * v) * (i * (1.0 - i))
        lam = lam_ref[...]
        dlam_ref[...] = _colsum(dla * ((-RGLRU_C) * r)) * (-_sigmoid(-lam))
        dba_ref[...] = _colsum(dzr)
        dbx_ref[...] = _colsum(dzi)
        dzr_b, dzi_b = dzr.astype(BF16), dzi.astype(BF16)
        tn = (((0,), (0,)), ((), ()))
        nt = (((1,), (1,)), ((), ()))
        dwa_ref[...] = lax.dot_general(vb, dzr_b, tn, preferred_element_type=F32)
        dwx_ref[...] = lax.dot_general(vb, dzi_b, tn, preferred_element_type=F32)
        dv = (div * i + lax.dot_general(dzr_b, wa, nt, preferred_element_type=F32)
              + lax.dot_general(dzi_b, wx, nt, preferred_element_type=F32))
        dcb_ref[...] = _colsum(dv)
        dvp = jnp.zeros_like(dv)
        for k in range(4):
            dvp = dvp + cw_ref[k:k + 1, :] * _shift_up(dv, 3 - k)
            dcw_ref[k:k + 1, :] = _colsum(dv * _shift_down(vp, 3 - k))
        dpg_ref[0] = dvp.astype(BF16)

    head_mat = pl.BlockSpec((None, c, c), lambda h: (h, 0, 0))
    outs = pl.pallas_call(
        body, name=name, grid=(n_heads,),
        in_specs=[sp_["pair"], sp_["plane"], sp_["conv_w"], sp_["chan"], sp_["w"], sp_["bias"],
                  sp_["w"], sp_["bias"], sp_["chan"]],
        out_specs=[sp_["pair"], head_mat, head_mat, sp_["bias"], sp_["bias"],
                   sp_["chan"], sp_["conv_w"], sp_["chan"]],
        out_shape=[jax.ShapeDtypeStruct((2, 2, s, e_half), BF16),
                   jax.ShapeDtypeStruct((n_heads, c, c), F32), jax.ShapeDtypeStruct((n_heads, c, c), F32),
                   jax.ShapeDtypeStruct((n_heads, 1, c), F32), jax.ShapeDtypeStruct((n_heads, 1, c), F32),
                   jax.ShapeDtypeStruct((1, e), F32), jax.ShapeDtypeStruct((4, e), F32),
                   jax.ShapeDtypeStruct((1, e), F32)],
        scratch_shapes=[pltpu.VMEM((s, c), F32)] * 4,
        compiler_params=_params(("parallel",)),
    )(proj, dy, conv_w, conv_b, w_a, b_a, w_x, b_x, lam)
    return tuple(outs)


def _ada_fwd(c_all, ada_w, name):
    n_l, d, f = ada_w.shape
    bf = _blk(f, 512)

    def body(c_ref, w_ref, o_ref):
        cv = c_ref[...]
        sc = (cv * _sigmoid(cv)).astype(BF16)
        o_ref[...] = jnp.dot(sc, w_ref[...].astype(BF16), preferred_element_type=F32)

    return pl.pallas_call(
        body, name=name, grid=(n_l, f // bf),
        in_specs=[pl.BlockSpec((8, d), lambda l, j: (0, 0)), pl.BlockSpec((None, d, bf), lambda l, j: (l, 0, j))],
        out_specs=pl.BlockSpec((None, 8, bf), lambda l, j: (l, 0, j)),
        out_shape=jax.ShapeDtypeStruct((n_l, 8, f), F32),
        compiler_params=_params(("parallel", "parallel")),
    )(c_all, ada_w)


def _ada_bwd_adamw(c_t, dmod, w, m, v, name):
    n_l, d, f = w.shape
    bf = _blk(f, 256)

    def body(c_ref, dm_ref, w_ref, m_ref, v_ref, g_ref, d_ref, m2_ref, v2_ref):
        cv = c_ref[...]
        sc = cv * _sigmoid(cv)
        dm = dm_ref[...]
        g = sc[:, 0:1] * dm[0:1, :]
        for b in range(1, 8):
            g = g + sc[:, b:b + 1] * dm[b:b + 1, :]
        g_ref[...] = g
        dl, m2, v2 = _adamw_math(w_ref[...], g, m_ref[...], v_ref[...])
        d_ref[...] = dl
        m2_ref[...] = m2
        v2_ref[...] = v2

    big = pl.BlockSpec((None, d, bf), lambda l, j: (l, 0, j))
    return pl.pallas_call(
        body, name=name, grid=(n_l, f // bf),
        in_specs=[pl.BlockSpec((d, 8), lambda l, j: (0, 0)), pl.BlockSpec((None, 8, bf), lambda l, j: (l, 0, j)),
                  big, big, big],
        out_specs=[big] * 4, out_shape=[jax.ShapeDtypeStruct((n_l, d, f), F32)] * 4,
        compiler_params=_params(("parallel", "parallel")),
    )(c_t, dmod, w, m, v)


def _pack(parts):
    padded, offs, n = [], [], 0
    for p in parts:
        p = p.reshape(-1)
        size = -(-p.shape[0] // PACK) * PACK
        offs.append(n)
        n += size
        padded.append(jnp.pad(p, (0, size - p.shape[0])) if size != p.shape[0] else p)
    return jnp.concatenate(padded), offs, n


def kernel(x, c, norm_g, ada_w, ada_b, sc_w_in, sc_conv_w, sc_w_out, lru_w_in, lru_conv_w, lru_conv_b, lru_w_a, lru_b_a, lru_w_x, lru_b_x, lru_lambda, lru_w_out, final_g, loss_target, m_norm_g, m_ada_w, m_ada_b, m_sc_w_in, m_sc_conv_w, m_sc_w_out, m_lru_w_in, m_lru_conv_w, m_lru_conv_b, m_lru_w_a, m_lru_b_a, m_lru_w_x, m_lru_b_x, m_lru_lambda, m_lru_w_out, m_final_g, v_norm_g, v_ada_w, v_ada_b, v_sc_w_in, v_sc_conv_w, v_sc_w_out, v_lru_w_in, v_lru_conv_w, v_lru_conv_b, v_lru_w_a, v_lru_b_a, v_lru_w_x, v_lru_b_x, v_lru_lambda, v_lru_w_out, v_final_g):
    xi, yi, ci = _pos()
    chip = 2 * xi + yi
    batch = 4 * xi + 2 * yi + ci
    core_op = jnp.reshape(ci, (1,)).astype(jnp.int32)
    chip_op = jnp.reshape(chip, (1,)).astype(jnp.int32)
    where_op = jnp.stack([chip, ci]).astype(jnp.int32)

    x2d, tgt = x[0], loss_target[0]
    s, d = x2d.shape
    es = sc_conv_w.shape[2]
    e = 4 * es
    n_heads = lru_w_a.shape[1]
    hj = lru_b_a.shape[2]
    f = ada_w.shape[2]
    row = lambda t: t.reshape(1, -1)

    small_parts = [c, sc_conv_w, lru_conv_w, lru_conv_b, lru_b_a, lru_b_x, lru_lambda]
    small, offs, n_small = _pack(small_parts)
    got = _allgather8([small.reshape(8, n_small // 8)], False, "ag_small")[0].reshape(8, n_small)
    c_all = got[:, :d]
    per_chip = got[0::2]

    def chip_part(k, shape):
        size = 1
        for dim in shape:
            size *= dim
        return per_chip[:, offs[k]:offs[k] + size].reshape((4,) + shape)

    conv_w0 = jnp.transpose(chip_part(1, (3, es)), (1, 0, 2)).reshape(3, e)
    conv_w1 = jnp.transpose(chip_part(2, (4, es)), (1, 0, 2)).reshape(4, e)
    conv_b1 = chip_part(3, (es,)).reshape(1, e)
    b_a = jnp.transpose(chip_part(4, (n_heads, hj)), (1, 0, 2)).reshape(n_heads, 1, 4 * hj)
    b_x = jnp.transpose(chip_part(5, (n_heads, hj)), (1, 0, 2)).reshape(n_heads, 1, 4 * hj)
    lam = chip_part(6, (es,)).reshape(1, e)

    shards = [sc_w_in[0], sc_w_out[0], lru_w_in[0], lru_w_a[0].reshape(n_heads * hj, HEAD_DIM),
              lru_w_x[0].reshape(n_heads * hj, HEAD_DIM), lru_w_out[0]]
    names = ["sc_w_in", "sc_w_out", "lru_w_in", "lru_w_a", "lru_w_x", "lru_w_out"]
    slots = [_cast_into_slot(chip_op, w, "cast_" + nm) for w, nm in zip(shards, names)]
    full = _allgather_shards(slots, "ag_weights")
    w_in0 = full[0].reshape(4, d, e)
    w_out0 = full[1].reshape(1, e, d)
    w_in1 = full[2].reshape(4, d, e // 2)
    w_a = full[3].reshape(4, n_heads, hj, HEAD_DIM)
    w_x = full[4].reshape(4, n_heads, hj, HEAD_DIM)
    w_out1 = full[5].reshape(1, e, d)

    mod_nb = _ada_fwd(c_all, ada_w, "ada_fwd")
    mods = _allgather8([mod_nb.reshape(16, f)], False, "ag_mod")[0].reshape(8, 2, 8, f)[0::2]
    mine = lax.dynamic_index_in_dim(mods, batch, axis=2, keepdims=False)
    mod = jnp.transpose(mine, (1, 0, 2)).reshape(2, 4 * f) + ada_b
    shift = [row(mod[l, :d]) for l in range(2)]
    scale = [row(mod[l, d:2 * d]) for l in range(2)]
    gate = [row(mod[l, 2 * d:]) for l in range(2)]
    ng = [row(norm_g[l]) for l in range(2)]

    h0 = _norm_mod_fwd(x2d, ng[0], scale[0], shift[0], "norm0")
    proj0 = _mm(h0[None], w_in0, "nn", BF16, "sc_in")
    y0 = _sc_fwd(proj0, conv_w0, "sc_mix")
    o0 = _mm(y0[None], w_out0, "nn", F32, "sc_out")[0]
    x1, h1 = _norm_mod_fwd(x2d, ng[1], scale[1], shift[1], "norm1", o=o0, gate=gate[0])
    proj1 = _mm(h1[None], w_in1, "nn", BF16, "lru_in")
    pairs1 = proj1.reshape(2, 2, s, e // 2)
    y1 = _lru_fwd(pairs1, conv_w1, conv_b1, w_a, b_a, w_x, b_x, lam, "lru_mix")
    o1 = _mm(y1[None], w_out1, "nn", F32, "lru_out")[0]
    dx2, do1, acc_f = _final_loss(x1, o1, gate[1], row(final_g), tgt, "final_loss")

    dy1 = _mm(do1[None], w_out1, "nt", BF16, "lru_out_dx")[0]
    g_w_out1 = _mm(y1[None], do1[None], "tn", F32, "lru_out_dw", bm=512, bn=2048)
    dpairs1, g_wa, g_wx, g_ba, g_bx, g_lam, g_cw1, g_cb1 = _lru_bwd(
        pairs1, dy1, conv_w1, conv_b1, w_a, b_a, w_x, b_x, lam, "lru_mix_bwd")
    dproj1 = dpairs1.reshape(4, s, e // 2)
    dh1 = _mm(dproj1, w_in1, "nt", F32, "lru_in_dx")[0]
    g_w_in1 = _mm(h1[None], dproj1, "tn", F32, "lru_in_dw", bm=512, bn=2048)
    dx1, do0, acc1 = _norm_mod_bwd(dh1, x1, ng[1], scale[1], dx2, "norm1_bwd", o_prev=o0, gate_prev=gate[0])

    dy0 = _mm(do0[None], w_out0, "nt", BF16, "sc_out_dx")[0]
    g_w_out0 = _mm(y0[None], do0[None], "tn", F32, "sc_out_dw", bm=512, bn=2048)
    dproj0, g_cw0 = _sc_bwd(proj0, dy0, conv_w0, "sc_mix_bwd")
    dh0 = _mm(dproj0, w_in0, "nt", F32, "sc_in_dx")[0]
    g_w_in0 = _mm(h0[None], dproj0, "tn", F32, "sc_in_dw", bm=512, bn=2048)
    grad_x, acc0 = _norm_mod_bwd(dh0, x2d, ng[0], scale[0], dx1, "norm0_bwd")

    def head_major_to_chip_major(t):
        return jnp.transpose(t.reshape(n_heads, 4, hj, HEAD_DIM), (1, 0, 2, 3)).reshape(4, n_heads * hj, HEAD_DIM)

    grads = [g_w_in0, g_w_out0.reshape(4, es, d), g_w_in1, head_major_to_chip_major(g_wa),
             head_major_to_chip_major(g_wx), g_w_out1.reshape(4, es, d)]
    from_sibling = _sibling_send_other_half(grads)
    parts = [_sum_own_and_sibling(core_op, g, r1, "rs_sum1_" + nm) for g, r1, nm in zip(grads, from_sibling, names)]
    from_chips = _chips_send_partials(parts)
    halves = [_sum_chips(where_op, p, r2, "rs_sum2_" + nm) for p, r2, nm in zip(parts, from_chips, names)]
    g_full = _sibling_share_half(halves)

    big_m = [m_sc_w_in, m_sc_w_out, m_lru_w_in, m_lru_w_a, m_lru_w_x, m_lru_w_out]
    big_v = [v_sc_w_in, v_sc_w_out, v_lru_w_in, v_lru_w_a, v_lru_w_x, v_lru_w_out]
    big_w = [sc_w_in, sc_w_out, lru_w_in, lru_w_a, lru_w_x, lru_w_out]
    big = {}
    for nm, w2, g2, m4, v4, w4 in zip(names, shards, g_full, big_m, big_v, big_w):
        dl, m2, v2 = _adamw(w2, g2, m4.reshape(w2.shape), v4.reshape(w2.shape), "adamw_" + nm)
        big[nm] = tuple(t.reshape(w4.shape) for t in (g2, dl, m2, v2))

    dmod = jnp.stack([jnp.concatenate([acc0[1], acc0[0], acc1[3]]),
                      jnp.concatenate([acc1[1], acc1[0], acc_f[1]])])
    part_list = [jnp.stack([acc0[2], acc1[2]]), acc_f[0], acc_f[2, :1], g_cw0, g_cw1, g_cb1, g_ba, g_bx,
                 g_lam, dmod]
    partials, poffs, n_part = _pack(part_list)
    every = _allgather8([partials.reshape(8, n_part // 8)], False, "ag_partials")[0].reshape(8, n_part)
    total = _sum_rows8(every, "sum_partials")[0]

    def tot(k, shape):
        size = 1
        for dim in shape:
            size *= dim
        return total[poffs[k]:poffs[k] + size].reshape(shape)

    def my_cols(t, width):
        return lax.dynamic_slice_in_dim(t, chip * width, width, axis=t.ndim - 1)

    loss = tot(2, (1,))[0]
    g_norm_g, g_final_g, g_ada_b = tot(0, (2, d)), tot(1, (d,)), tot(9, (2, 3 * d))
    g_sc_conv_w = my_cols(tot(3, (3, e)), es)[None]
    g_lru_conv_w = my_cols(tot(4, (4, e)), es)[None]
    g_lru_conv_b = my_cols(tot(5, (1, e)), es)
    g_lru_b_a = my_cols(tot(6, (n_heads, 4 * hj)), hj)[None]
    g_lru_b_x = my_cols(tot(7, (n_heads, 4 * hj)), hj)[None]
    g_lru_lambda = my_cols(tot(8, (1, e)), es)

    dmod_all = every[:, poffs[9]:poffs[9] + 6 * d].reshape(8, 2, 3 * d)
    dmod_mine = jnp.transpose(my_cols(dmod_all, f), (1, 0, 2))
    ada = _ada_bwd_adamw(jnp.transpose(c_all), dmod_mine, ada_w, m_ada_w, v_ada_w, "ada_bwd_adamw")

    small_names = ["norm_g", "ada_b", "final_g", "sc_conv_w", "lru_conv_w", "lru_conv_b", "lru_b_a",
                   "lru_b_x", "lru_lambda"]
    small_w = [norm_g, ada_b, final_g, sc_conv_w, lru_conv_w, lru_conv_b, lru_b_a, lru_b_x, lru_lambda]
    small_g = [g_norm_g, g_ada_b, g_final_g, g_sc_conv_w, g_lru_conv_w, g_lru_conv_b, g_lru_b_a,
               g_lru_b_x, g_lru_lambda]
    small_m = [m_norm_g, m_ada_b, m_final_g, m_sc_conv_w, m_lru_conv_w, m_lru_conv_b, m_lru_b_a,
               m_lru_b_x, m_lru_lambda]
    small_v = [v_norm_g, v_ada_b, v_final_g, v_sc_conv_w, v_lru_conv_w, v_lru_conv_b, v_lru_b_a,
               v_lru_b_x, v_lru_lambda]
    pw, soffs, n_s = _pack(small_w)
    pg, pm, pv = _pack(small_g)[0], _pack(small_m)[0], _pack(small_v)[0]
    shape2 = (n_s // PACK, PACK)
    pd, pm2, pv2 = _adamw(pw.reshape(shape2), pg.reshape(shape2), pm.reshape(shape2), pv.reshape(shape2),
                          "adamw_small")
    small = {}
    for k, (nm, w_) in enumerate(zip(small_names, small_w)):
        take = lambda t: t.reshape(-1)[soffs[k]:soffs[k] + w_.size].reshape(w_.shape)
        small[nm] = (small_g[k].reshape(w_.shape), take(pd), take(pm2), take(pv2))

    results = dict(small)
    results.update(big)
    results["ada_w"] = tuple(ada)
    order = ["norm_g", "ada_w", "ada_b", "sc_w_in", "sc_conv_w", "sc_w_out", "lru_w_in", "lru_conv_w",
             "lru_conv_b", "lru_w_a", "lru_b_a", "lru_w_x", "lru_b_x", "lru_lambda", "lru_w_out", "final_g"]
    out = [loss, grad_x[None]]
    for kind in range(4):
        out += [results[nm][kind] for nm in order]
    return tuple(out)
```

```python
import functools

import jax
import jax.numpy as jnp
from jax import lax
from jax.experimental import pallas as pl
from jax.experimental.pallas import tpu as pltpu

F32 = jnp.float32
BF16 = jnp.bfloat16
MESH = pl.DeviceIdType.MESH
ANY = pl.BlockSpec(memory_space=pl.ANY)

RMS_EPS = 1e-6
RGLRU_C = 8.0
HEAD_DIM = 256
ADAM_LR = 0.001
ADAM_B1 = 0.9
ADAM_B2 = 0.999
ADAM_EPS = 1e-08
ADAM_WD = 0.01
ADAM_STEP = 10
V7X_VMEM_LIMIT = 56 * 1024 * 1024
LANES = 128
PACK = 8 * LANES


def _blk(dim, pref, unit=LANES):
    if dim <= pref:
        return dim
    b = (pref // unit) * unit
    while b > unit and dim % b:
        b -= unit
    assert dim % b == 0, (dim, pref, unit)
    return b


def _params(sem=None):
    return pltpu.CompilerParams(dimension_semantics=sem, vmem_limit_bytes=V7X_VMEM_LIMIT)


def _pos():
    return lax.axis_index("x"), lax.axis_index("y"), lax.axis_index("c")


def _other_chips(x, y):
    return [(1 - x, y), (x, 1 - y), (1 - x, 1 - y)]


def _allgather8(arrs, from_half, name, after=None):
    n_t = len(arrs)
    ms = [a.shape[0] // 2 if from_half else a.shape[0] for a in arrs]
    extra = [] if after is None else [after]
    n_in = n_t + len(extra)

    def body(*refs):
        ins, outs = refs[:n_t], refs[n_in:n_in + n_t]
        send_sems, recv_sems, local_sems = refs[n_in + n_t:]
        x, y, c = _pos()
        me, sibling = (x, y, c), (x, y, 1 - c)
        chips = _other_chips(x, y)

        def rows(t, px, py, pc):
            return outs[t].at[pl.ds((4 * px + 2 * py + pc) * ms[t], ms[t])]

        def copy(t, k, block, to, src=None):
            return pltpu.make_async_remote_copy(
                src_ref=rows(t, *block) if src is None else src, dst_ref=rows(t, *block),
                send_sem=send_sems.at[7 * t + k], recv_sem=recv_sems.at[7 * t + k],
                device_id=to, device_id_type=MESH)

        mine, first, passed = [], [], []
        for t in range(n_t):
            src = ins[t].at[pl.ds(c * ms[t], ms[t])] if from_half else ins[t]
            cp = pltpu.make_async_copy(src, rows(t, *me), local_sems.at[t])
            cp.start()
            mine.append(cp)
            sends = [copy(t, 0, me, sibling, src=src)]
            sends += [copy(t, 1 + j, me, (*chip, c), src=src) for j, chip in enumerate(chips)]
            for cp in sends:
                cp.start()
            first += sends
        for t in range(n_t):
            for j, chip in enumerate(chips):
                copy(t, 1 + j, (*chip, c), me).wait_recv()
                cp = copy(t, 4 + j, (*chip, c), sibling)
                cp.start()
                passed.append(cp)
        for t in range(n_t):
            copy(t, 0, sibling, me).wait_recv()
            for j, chip in enumerate(chips):
                copy(t, 4 + j, (*chip, 1 - c), me).wait_recv()
        for cp in first + passed:
            cp.wait_send()
        for cp in mine:
            cp.wait()

    return pl.pallas_call(
        body, name=name,
        out_shape=[jax.ShapeDtypeStruct((8 * m, a.shape[1]), a.dtype) for m, a in zip(ms, arrs)],
        in_specs=[ANY] * n_in, out_specs=[ANY] * n_t,
        scratch_shapes=[pltpu.SemaphoreType.DMA((7 * n_t,)), pltpu.SemaphoreType.DMA((7 * n_t,)),
                        pltpu.SemaphoreType.DMA((n_t,))],
    )(*arrs, *extra)


HBM = pl.BlockSpec(memory_space=pltpu.HBM)
SEM = pl.BlockSpec(memory_space=pltpu.SEMAPHORE)
TOKEN = pl.BlockSpec(memory_space=pltpu.VMEM)
IN_FLIGHT = pltpu.CompilerParams(has_side_effects=pltpu.SideEffectType.DATAFLOW_SIDE_EFFECTING)


def _in_hbm(arrs):
    return [pltpu.with_memory_space_constraint(a, pltpu.HBM) for a in arrs]


def _shard_rows(ref, h, px, py, pc):
    return ref.at[pl.ds((4 * px + 2 * py + pc) * h, h)]


def _gather_start(groups, name):
    bufs = [b for grp in groups for b in grp]
    n_t, n_g = len(bufs), len(groups)

    def body(*refs):
        sems, thru = refs[n_t:n_t + 2 * n_g], refs[n_t + 2 * n_g:2 * n_t + 2 * n_g]
        token = refs[-1]
        x, y, c = _pos()
        t = 0
        for g, grp in enumerate(groups):
            for i in range(len(grp)):
                h = bufs[t].shape[0] // 8
                rows = _shard_rows(thru[t], h, x, y, c)
                for j, chip in enumerate(_other_chips(x, y)):
                    pltpu.make_async_remote_copy(
                        src_ref=rows, dst_ref=rows, send_sem=sems[2 * g].at[3 * i + j],
                        recv_sem=sems[2 * g + 1].at[3 * i + j], device_id=(*chip, c),
                        device_id_type=MESH).start()
                t += 1
        token[...] = jnp.zeros_like(token)

    sem_shapes = []
    for grp in groups:
        sem_shapes += [pltpu.SemaphoreType.DMA((3 * len(grp),))] * 2
    out = pl.pallas_call(
        body, name=name,
        out_shape=sem_shapes + [pltpu.HBM(b.shape, b.dtype) for b in bufs] + [jax.ShapeDtypeStruct((8, LANES), F32)],
        in_specs=[HBM] * n_t, out_specs=[SEM] * (2 * n_g) + [HBM] * n_t + [TOKEN],
        input_output_aliases={t: 2 * n_g + t for t in range(n_t)},
        compiler_params=IN_FLIGHT,
    )(*_in_hbm(bufs))
    sems, thru, token = out[:2 * n_g], out[2 * n_g:2 * n_g + n_t], out[-1]
    per_group, t = [], 0
    for g, grp in enumerate(groups):
        per_group.append((sems[2 * g], sems[2 * g + 1], thru[t:t + len(grp)]))
        t += len(grp)
    return per_group, token


def _gather_forward(send_sems, recv_sems, bufs, after, name):
    n_t = len(bufs)

    def body(*refs):
        ins = refs[:n_t]
        send1, recv1 = refs[n_t], refs[n_t + 1]
        send2, recv2 = refs[n_t + 3], refs[n_t + 4]
        token = refs[-1]
        x, y, c = _pos()
        for t in range(n_t):
            h = bufs[t].shape[0] // 8
            mine = _shard_rows(ins[t], h, x, y, c)
            for j, chip in enumerate(_other_chips(x, y)):
                landed = _shard_rows(ins[t], h, *chip, c)
                pltpu.make_async_remote_copy(
                    src_ref=mine, dst_ref=landed, send_sem=send1.at[3 * t + j], recv_sem=recv1.at[3 * t + j],
                    device_id=(*chip, c), device_id_type=MESH).wait_recv()
                pltpu.make_async_remote_copy(
                    src_ref=landed, dst_ref=landed, send_sem=send2.at[3 * t + j], recv_sem=recv2.at[3 * t + j],
                    device_id=(x, y, 1 - c), device_id_type=MESH).start()
        for t in range(n_t):
            h = bufs[t].shape[0] // 8
            mine = _shard_rows(ins[t], h, x, y, c)
            for j, chip in enumerate(_other_chips(x, y)):
                pltpu.make_async_remote_copy(
                    src_ref=mine, dst_ref=mine, send_sem=send1.at[3 * t + j], recv_sem=recv1.at[3 * t + j],
                    device_id=(*chip, c), device_id_type=MESH).wait_send()
        token[...] = jnp.zeros_like(token)

    out = pl.pallas_call(
        body, name=name,
        out_shape=[pltpu.SemaphoreType.DMA((3 * n_t,))] * 2 + [pltpu.HBM(b.shape, b.dtype) for b in bufs]
        + [jax.ShapeDtypeStruct((8, LANES), F32)],
        in_specs=[HBM] * n_t + [SEM, SEM, ANY], out_specs=[SEM, SEM] + [HBM] * n_t + [TOKEN],
        input_output_aliases={t: 2 + t for t in range(n_t)},
        compiler_params=IN_FLIGHT,
    )(*bufs, send_sems, recv_sems, after)
    return out[0], out[1], out[2:2 + n_t], out[-1]


def _gather_finish(send_sems, recv_sems, bufs, after, name):
    n_t = len(bufs)

    def body(*refs):
        ins = refs[:n_t]
        send2, recv2 = refs[n_t], refs[n_t + 1]
        x, y, c = _pos()
        for t in range(n_t):
            h = bufs[t].shape[0] // 8
            for j, chip in enumerate(_other_chips(x, y)):
                sent = _shard_rows(ins[t], h, *chip, c)
                got = _shard_rows(ins[t], h, *chip, 1 - c)
                cp = pltpu.make_async_remote_copy(
                    src_ref=sent, dst_ref=got, send_sem=send2.at[3 * t + j], recv_sem=recv2.at[3 * t + j],
                    device_id=(x, y, 1 - c), device_id_type=MESH)
                cp.wait_send()
                cp.wait_recv()

    return pl.pallas_call(
        body, name=name, out_shape=[pltpu.HBM(b.shape, b.dtype) for b in bufs],
        in_specs=[HBM] * n_t + [SEM, SEM, ANY], out_specs=[HBM] * n_t,
        input_output_aliases={t: t for t in range(n_t)},
        compiler_params=IN_FLIGHT,
    )(*bufs, send_sems, recv_sems, after)


def _exchange(name, ins, out_shapes, n_copies, plan, aliases=None):
    ni, no = len(ins), len(out_shapes)

    def body(*refs):
        in_refs, out_refs = refs[:ni], refs[ni:ni + no]
        send_sems, recv_sems, local_sems = refs[ni + no:]
        x, y, c = _pos()
        copies = []
        for k, (src, dst, dev) in enumerate(plan(in_refs, out_refs, x, y, c)):
            if dev is None:
                cp = pltpu.make_async_copy(src, dst, local_sems.at[k])
            else:
                cp = pltpu.make_async_remote_copy(
                    src_ref=src, dst_ref=dst, send_sem=send_sems.at[k], recv_sem=recv_sems.at[k],
                    device_id=dev, device_id_type=MESH)
            cp.start()
            copies.append(cp)
        assert len(copies) == n_copies
        for cp in copies:
            cp.wait()

    return pl.pallas_call(
        body, name=name, out_shape=out_shapes,
        in_specs=[ANY] * ni, out_specs=[ANY] * no, input_output_aliases=aliases or {},
        scratch_shapes=[pltpu.SemaphoreType.DMA((n_copies,)), pltpu.SemaphoreType.DMA((n_copies,)),
                        pltpu.SemaphoreType.DMA((n_copies,))],
    )(*ins)


def _sibling_send_other_half(grads):
    def plan(in_refs, out_refs, x, y, c):
        out = []
        for g_ref, r_ref in zip(in_refs, out_refs):
            h = g_ref.shape[1] // 2
            out.append((g_ref.at[:, pl.ds((1 - c) * h, h), :], r_ref, (x, y, 1 - c)))
        return out

    shapes = [jax.ShapeDtypeStruct((4, g.shape[1] // 2, g.shape[2]), g.dtype) for g in grads]
    return _exchange("rs_sibling", grads, shapes, len(grads), plan)


def _chips_send_partials(parts):
    def plan(in_refs, out_refs, x, y, c):
        out = []
        for p_ref, r_ref in zip(in_refs, out_refs):
            for j, (px, py) in enumerate(_other_chips(x, y)):
                out.append((p_ref.at[2 * px + py], r_ref.at[j], (px, py, c)))
        return out

    shapes = [jax.ShapeDtypeStruct((3,) + p.shape[1:], p.dtype) for p in parts]
    return _exchange("rs_chips", parts, shapes, 3 * len(parts), plan)


def _sibling_share_half(fulls):
    def plan(in_refs, out_refs, x, y, c):
        out = []
        for f_ref in out_refs:
            h = f_ref.shape[0] // 2
            rows = f_ref.at[pl.ds(c * h, h)]
            out.append((rows, rows, (x, y, 1 - c)))
        return out

    shapes = [jax.ShapeDtypeStruct(f.shape, f.dtype) for f in fulls]
    return _exchange("rs_share", fulls, shapes, len(fulls), plan, aliases={t: t for t in range(len(fulls))})


def _cast_into_slot(chip, w, name):
    r, c = w.shape
    br, bc = _blk(r, 512, 8), _blk(c, 2048)
    nb = r // br

    def body(chip_ref, w_ref, o_ref):
        o_ref[...] = w_ref[...].astype(BF16)

    grid_spec = pltpu.PrefetchScalarGridSpec(
        num_scalar_prefetch=1, grid=(nb, c // bc),
        in_specs=[pl.BlockSpec((br, bc), lambda i, j, chip_ref: (i, j))],
        out_specs=pl.BlockSpec((br, bc), lambda i, j, chip_ref: (chip_ref[0] * nb + i, j)))
    return pl.pallas_call(
        body, name=name, grid_spec=grid_spec,
        out_shape=jax.ShapeDtypeStruct((4 * r, c), BF16),
        compiler_params=_params(("parallel", "parallel")),
    )(chip, w)


def _sum_own_and_sibling(core, g, r1, name):
    _, r, c = g.shape
    h = r // 2
    br, bc = _blk(h, 256, 8), _blk(c, 2048)
    nb = h // br

    def body(core_ref, g_ref, r_ref, o_ref):
        o_ref[...] = (g_ref[...] + r_ref[...]).astype(BF16)

    grid_spec = pltpu.PrefetchScalarGridSpec(
        num_scalar_prefetch=1, grid=(4, nb, c // bc),
        in_specs=[pl.BlockSpec((None, br, bc), lambda k, i, j, core_ref: (k, core_ref[0] * nb + i, j)),
                  pl.BlockSpec((None, br, bc), lambda k, i, j, core_ref: (k, i, j))],
        out_specs=pl.BlockSpec((None, br, bc), lambda k, i, j, core_ref: (k, i, j)))
    return pl.pallas_call(
        body, name=name, grid_spec=grid_spec,
        out_shape=jax.ShapeDtypeStruct((4, h, c), BF16),
        compiler_params=_params(("parallel", "parallel", "parallel")),
    )(core, g, r1)


def _sum_chips(where, p, r2, name):
    _, h, c = p.shape
    br, bc = _blk(h, 256, 8), _blk(c, 2048)
    nb = h // br

    def body(where_ref, p_ref, r_ref, o_ref):
        acc = p_ref[...].astype(F32)
        for j in range(3):
            acc = acc + r_ref[j].astype(F32)
        o_ref[...] = acc

    grid_spec = pltpu.PrefetchScalarGridSpec(
        num_scalar_prefetch=1, grid=(nb, c // bc),
        in_specs=[pl.BlockSpec((None, br, bc), lambda i, j, where_ref: (where_ref[0], i, j)),
                  pl.BlockSpec((3, br, bc), lambda i, j, where_ref: (0, i, j))],
        out_specs=pl.BlockSpec((br, bc), lambda i, j, where_ref: (where_ref[1] * nb + i, j)))
    return pl.pallas_call(
        body, name=name, grid_spec=grid_spec,
        out_shape=jax.ShapeDtypeStruct((2 * h, c), F32),
        compiler_params=_params(("parallel", "parallel")),
    )(where, p, r2)


def _adamw_math(w, g, m, v):
    m2 = ADAM_B1 * m + (1.0 - ADAM_B1) * g
    v2 = ADAM_B2 * v + (1.0 - ADAM_B2) * (g * g)
    m_hat = m2 / (1.0 - ADAM_B1 ** ADAM_STEP)
    v_hat = v2 / (1.0 - ADAM_B2 ** ADAM_STEP)
    delta = -ADAM_LR * (m_hat / (jnp.sqrt(v_hat) + ADAM_EPS) + ADAM_WD * w)
    return delta, m2, v2


def _adamw(w, g, m, v, name):
    r, c = w.shape
    br, bc = _blk(r, 128, 8), _blk(c, 2048)

    def body(w_ref, g_ref, m_ref, v_ref, d_ref, m2_ref, v2_ref):
        d, m2, v2 = _adamw_math(w_ref[...], g_ref[...], m_ref[...], v_ref[...])
        d_ref[...] = d
        m2_ref[...] = m2
        v2_ref[...] = v2

    spec = pl.BlockSpec((br, bc), lambda i, j: (i, j))
    return pl.pallas_call(
        body, name=name, grid=(r // br, c // bc),
        in_specs=[spec] * 4, out_specs=[spec] * 3,
        out_shape=[jax.ShapeDtypeStruct((r, c), F32)] * 3,
        compiler_params=_params(("parallel", "parallel")),
    )(w, g, m, v)


def _sum_rows8(g, name):
    n = g.shape[1]

    def body(g_ref, o_ref):
        acc = g_ref[0:1, :]
        for k in range(1, 8):
            acc = acc + g_ref[k:k + 1, :]
        o_ref[...] = acc

    return pl.pallas_call(
        body, name=name, out_shape=jax.ShapeDtypeStruct((1, n), F32),
        in_specs=[pl.BlockSpec(memory_space=pltpu.VMEM)],
        out_specs=pl.BlockSpec(memory_space=pltpu.VMEM),
        compiler_params=_params(),
    )(g)


def _mm(a, b, mode, out_dtype, name, bm=1024, bn=None):
    if mode == "nn":
        (_, m, k), (g, _, n) = a.shape, b.shape
    elif mode == "tn":
        (_, k, m), (g, _, n) = a.shape, b.shape
    else:
        (g, m, k), (_, n, _) = a.shape, b.shape
    if bn is None:
        bn = 1024 if k <= 2048 else 512
    bm, bn = _blk(m, bm), _blk(n, bn)

    if mode == "nt":
        def body(a_ref, b_ref, o_ref, acc_ref):
            part = lax.dot_general(a_ref[...], b_ref[...], (((1,), (1,)), ((), ())),
                                   preferred_element_type=F32)
            if g == 1:
                o_ref[...] = part.astype(out_dtype)
            else:
                gi = pl.program_id(2)

                @pl.when(gi == 0)
                def _():
                    acc_ref[...] = part

                @pl.when(gi > 0)
                def _():
                    acc_ref[...] += part

                @pl.when(gi == g - 1)
                def _():
                    o_ref[...] = acc_ref[...].astype(out_dtype)

        return pl.pallas_call(
            body, name=name, grid=(m // bm, n // bn, g),
            in_specs=[pl.BlockSpec((None, bm, k), lambda i, j, gi: (gi, i, 0)),
                      pl.BlockSpec((None, bn, k), lambda i, j, gi: (gi, j, 0))],
            out_specs=pl.BlockSpec((None, bm, bn), lambda i, j, gi: (0, i, j)),
            out_shape=jax.ShapeDtypeStruct((1, m, n), out_dtype),
            scratch_shapes=[pltpu.VMEM((bm, bn), F32)],
            compiler_params=_params(("parallel", "parallel", "arbitrary")),
        )(a, b)

    contract = (((1,), (0,)), ((), ())) if mode == "nn" else (((0,), (0,)), ((), ()))

    def body(a_ref, b_ref, o_ref):
        o_ref[...] = lax.dot_general(a_ref[...], b_ref[...], contract,
                                     preferred_element_type=F32).astype(out_dtype)

    a_spec = (pl.BlockSpec((None, bm, k), lambda i, gi, j: (0, i, 0)) if mode == "nn"
              else pl.BlockSpec((None, k, bm), lambda i, gi, j: (0, 0, i)))
    return pl.pallas_call(
        body, name=name, grid=(m // bm, g, n // bn),
        in_specs=[a_spec, pl.BlockSpec((None, k, bn), lambda i, gi, j: (gi, 0, j))],
        out_specs=pl.BlockSpec((None, bm, bn), lambda i, gi, j: (gi, i, j)),
        out_shape=jax.ShapeDtypeStruct((g, m, n), out_dtype),
        compiler_params=_params(("parallel", "parallel", "parallel")),
    )(a, b)


def _row_specs(br, d):
    return (pl.BlockSpec((br, d), lambda i: (i, 0)), pl.BlockSpec((1, d), lambda i: (0, 0)),
            pl.BlockSpec((8, d), lambda i: (0, 0)))


def _rstd(xv):
    return lax.rsqrt(jnp.mean(xv * xv, axis=-1, keepdims=True) + RMS_EPS)


def _colsum(v):
    return jnp.sum(v, axis=0, keepdims=True)


def _norm_mod_fwd(x, g, scale, shift, name, o=None, gate=None):
    s, d = x.shape
    br = _blk(s, 256, 8)
    has_res = o is not None
    row, vec, _ = _row_specs(br, d)

    def body(*refs):
        if has_res:
            x_ref, o_ref, gate_ref, g_ref, sc_ref, sh_ref, x1_ref, h_ref = refs
            xv = x_ref[...] + gate_ref[...] * o_ref[...]
            x1_ref[...] = xv
        else:
            x_ref, g_ref, sc_ref, sh_ref, h_ref = refs
            xv = x_ref[...]
        n = xv * _rstd(xv) * g_ref[...]
        h_ref[...] = (n * (1.0 + sc_ref[...]) + sh_ref[...]).astype(BF16)

    ins = [x] + ([o, gate] if has_res else []) + [g, scale, shift]
    in_specs = [row] + ([row, vec] if has_res else []) + [vec] * 3
    out_shape = ([jax.ShapeDtypeStruct((s, d), F32)] if has_res else []) + [jax.ShapeDtypeStruct((s, d), BF16)]
    out = pl.pallas_call(
        body, name=name, grid=(s // br,), in_specs=in_specs, out_specs=[row] * len(out_shape),
        out_shape=out_shape, compiler_params=_params(("parallel",)),
    )(*ins)
    return out if has_res else out[0]


def _final_loss(x1, o1, gate1, final_g, tgt, name):
    s, d = x1.shape
    br = _blk(s, 256, 8)
    row, vec, acc = _row_specs(br, d)

    def body(x1_ref, o_ref, gate_ref, g_ref, t_ref, dx_ref, do_ref, acc_ref):
        @pl.when(pl.program_id(0) == 0)
        def _():
            acc_ref[...] = jnp.zeros_like(acc_ref)

        gate, o, g = gate_ref[...], o_ref[...], g_ref[...]
        x2 = x1_ref[...] + gate * o
        r = _rstd(x2)
        xh = x2 * r
        err = xh * g - t_ref[...]
        loss = 0.5 * _colsum(jnp.mean(err * err, axis=-1, keepdims=True))
        dout = err * (1.0 / d)
        dxh = dout * g
        dx2 = r * (dxh - xh * jnp.mean(dxh * xh, axis=-1, keepdims=True))
        dx_ref[...] = dx2
        do_ref[...] = (dx2 * gate).astype(BF16)
        acc_ref[0:1, :] += _colsum(dout * xh)
        acc_ref[1:2, :] += _colsum(dx2 * o)
        acc_ref[2:3, :] += jnp.broadcast_to(loss, (1, d))

    return pl.pallas_call(
        body, name=name, grid=(s // br,),
        in_specs=[row, row, vec, vec, row], out_specs=[row, row, acc],
        out_shape=[jax.ShapeDtypeStruct((s, d), F32), jax.ShapeDtypeStruct((s, d), BF16),
                   jax.ShapeDtypeStruct((8, d), F32)],
        compiler_params=_params(("arbitrary",)),
    )(x1, o1, gate1, final_g, tgt)


def _norm_mod_bwd(dh, x, g, scale, dx_next, name, o_prev=None, gate_prev=None):
    s, d = x.shape
    br = _blk(s, 256, 8)
    has_prev = o_prev is not None
    row, vec, acc = _row_specs(br, d)

    def body(*refs):
        if has_prev:
            dh_ref, x_ref, g_ref, sc_ref, dxn_ref, o_ref, gate_ref, dx_ref, do_ref, acc_ref = refs
        else:
            dh_ref, x_ref, g_ref, sc_ref, dxn_ref, dx_ref, acc_ref = refs

        @pl.when(pl.program_id(0) == 0)
        def _():
            acc_ref[...] = jnp.zeros_like(acc_ref)

        xv, gv, dhv = x_ref[...], g_ref[...], dh_ref[...]
        r = _rstd(xv)
        xh = xv * r
        acc_ref[0:1, :] += _colsum(dhv * (xh * gv))
        acc_ref[1:2, :] += _colsum(dhv)
        dn = dhv * (1.0 + sc_ref[...])
        acc_ref[2:3, :] += _colsum(dn * xh)
        dxh = dn * gv
        dx = dxn_ref[...] + r * (dxh - xh * jnp.mean(dxh * xh, axis=-1, keepdims=True))
        dx_ref[...] = dx
        if has_prev:
            acc_ref[3:4, :] += _colsum(dx * o_ref[...])
            do_ref[...] = (dx * gate_ref[...]).astype(BF16)

    ins = [dh, x, g, scale, dx_next] + ([o_prev, gate_prev] if has_prev else [])
    in_specs = [row, row, vec, vec, row] + ([row, vec] if has_prev else [])
    out_shape = [jax.ShapeDtypeStruct((s, d), F32)]
    out_specs = [row]
    if has_prev:
        out_shape.append(jax.ShapeDtypeStruct((s, d), BF16))
        out_specs.append(row)
    out_shape.append(jax.ShapeDtypeStruct((8, d), F32))
    out_specs.append(acc)
    return pl.pallas_call(
        body, name=name, grid=(s // br,), in_specs=in_specs, out_specs=out_specs,
        out_shape=out_shape, compiler_params=_params(("arbitrary",)),
    )(*ins)


def _shift_down(p, k):
    if k == 0:
        return p
    rows = lax.broadcasted_iota(jnp.int32, p.shape, 0)
    return jnp.where(rows >= k, pltpu.roll(p, k, 0), 0.0)


def _shift_up(p, k):
    if k == 0:
        return p
    s = p.shape[0]
    rows = lax.broadcasted_iota(jnp.int32, p.shape, 0)
    return jnp.where(rows < s - k, pltpu.roll(p, s - k, 0), 0.0)


def _sigmoid(z):
    return 1.0 / (1.0 + jnp.exp(-z))


def _sc_parts(proj_ref, w_ref):
    b, cg, v, g = (proj_ref[i].astype(F32) for i in range(4))
    p = cg * v
    u = w_ref[2:3, :] * p + w_ref[1:2, :] * _shift_down(p, 1) + w_ref[0:1, :] * _shift_down(p, 2)
    return b, cg, v, g, p, u


def _sc_fwd(proj, conv_w, name):
    _, s, e = proj.shape
    bc = _blk(e, 256)

    def body(proj_ref, w_ref, y_ref):
        b, _, _, g, _, u = _sc_parts(proj_ref, w_ref)
        y_ref[...] = (b * u * (g * _sigmoid(g))).astype(BF16)

    return pl.pallas_call(
        body, name=name, grid=(e // bc,),
        in_specs=[pl.BlockSpec((4, s, bc), lambda j: (0, 0, j)), pl.BlockSpec((3, bc), lambda j: (0, j))],
        out_specs=pl.BlockSpec((s, bc), lambda j: (0, j)),
        out_shape=jax.ShapeDtypeStruct((s, e), BF16),
        compiler_params=_params(("parallel",)),
    )(proj, conv_w)


def _sc_bwd(proj, dy, conv_w, name):
    _, s, e = proj.shape
    bc = _blk(e, 256)

    def body(proj_ref, dy_ref, w_ref, dp_ref, dw_ref):
        b, cg, v, g, p, u = _sc_parts(proj_ref, w_ref)
        dyv = dy_ref[...].astype(F32)
        sig = _sigmoid(g)
        t = dyv * (g * sig)
        du = t * b
        dp_ref[0] = (t * u).astype(BF16)
        dp_ref[3] = (dyv * b * u * (sig * (1.0 + g * (1.0 - sig)))).astype(BF16)
        dpp = w_ref[2:3, :] * du + w_ref[1:2, :] * _shift_up(du, 1) + w_ref[0:1, :] * _shift_up(du, 2)
        dp_ref[1] = (dpp * v).astype(BF16)
        dp_ref[2] = (dpp * cg).astype(BF16)
        dw_ref[2:3, :] = _colsum(du * p)
        dw_ref[1:2, :] = _colsum(du * _shift_down(p, 1))
        dw_ref[0:1, :] = _colsum(du * _shift_down(p, 2))

    return pl.pallas_call(
        body, name=name, grid=(e // bc,),
        in_specs=[pl.BlockSpec((4, s, bc), lambda j: (0, 0, j)), pl.BlockSpec((s, bc), lambda j: (0, j)),
                  pl.BlockSpec((3, bc), lambda j: (0, j))],
        out_specs=[pl.BlockSpec((4, s, bc), lambda j: (0, 0, j)), pl.BlockSpec((3, bc), lambda j: (0, j))],
        out_shape=[jax.ShapeDtypeStruct((4, s, e), BF16), jax.ShapeDtypeStruct((3, e), F32)],
        compiler_params=_params(("parallel",)),
    )(proj, dy, conv_w)


def _softplus_neg(lam):
    u = jnp.exp(-jnp.abs(lam))
    w = 1.0 + u
    log1p = jnp.where(w == 1.0, u, jnp.log(w) * (u / jnp.where(w == 1.0, 1.0, w - 1.0)))
    return jnp.maximum(-lam, 0.0) + log1p


def _one_minus_exp(z):
    series = -z * (1.0 + z * (0.5 + z * (1.0 / 6.0 + z * (1.0 / 24.0))))
    return jnp.where(z > -0.02, series, 1.0 - jnp.exp(z))


def _scan_in_tiles(a, b, reverse):
    s = a.shape[0]
    rows = lax.broadcasted_iota(jnp.int32, a.shape, 0) & 7
    for step in (1, 2, 4):
        if reverse:
            ok = rows < 8 - step
            a_s, b_s = pltpu.roll(a, s - step, 0), pltpu.roll(b, s - step, 0)
        else:
            ok = rows >= step
            a_s, b_s = pltpu.roll(a, step, 0), pltpu.roll(b, step, 0)
        b = jnp.where(ok, a * b_s + b, b)
        a = jnp.where(ok, a * a_s, a)
    return a, b


def _scan(a, b, a_ref, b_ref, h_ref, reverse):
    s, c = a.shape
    n = s // 8
    a_t, b_t = _scan_in_tiles(a, b, reverse)
    a_ref[...] = a_t
    b_ref[...] = b_t

    def step(i, carry):
        gi = n - 1 - i if reverse else i
        sl = pl.ds(pl.multiple_of(gi * 8, 8), 8)
        h = b_ref[sl, :] + a_ref[sl, :] * carry
        h_ref[sl, :] = h
        return h[0:1, :] if reverse else h[7:8, :]

    lax.fori_loop(0, n, step, jnp.zeros((1, c), F32))
    return h_ref[...]


def _lru_specs(s, e_half, n_heads):
    hp = e_half // HEAD_DIM
    c = HEAD_DIM
    return dict(
        pair=pl.BlockSpec((2, None, s, c), lambda h: (0, h // hp, 0, h % hp)),
        conv_w=pl.BlockSpec((4, c), lambda h: (0, h)),
        chan=pl.BlockSpec((1, c), lambda h: (0, h)),
        w=pl.BlockSpec((4, None, c // 4, c), lambda h: (0, h, 0, 0)),
        bias=pl.BlockSpec((None, 1, c), lambda h: (h, 0, 0)),
        plane=pl.BlockSpec((s, c), lambda h: (0, h)),
    )


def _lru_forward_parts(vp, cw_ref, cb_ref, wa_ref, ba_ref, wx_ref, bx_ref, lam_ref):
    c = HEAD_DIM
    v = cb_ref[...] + sum(cw_ref[k:k + 1, :] * _shift_down(vp, 3 - k) for k in range(4))
    vb = v.astype(BF16)
    wa = wa_ref[...].reshape(c, c)
    wx = wx_ref[...].reshape(c, c)
    r = _sigmoid(jnp.dot(vb, wa, preferred_element_type=F32) + ba_ref[...])
    i = _sigmoid(jnp.dot(vb, wx, preferred_element_type=F32) + bx_ref[...])
    sp = _softplus_neg(lam_ref[...])
    la = (-RGLRU_C) * sp * r
    a = jnp.exp(la)
    nm = jnp.sqrt(_one_minus_exp(2.0 * la))
    return v, vb, wa, wx, r, i, sp, a, nm


def _lru_fwd(proj, conv_w, conv_b, w_a, b_a, w_x, b_x, lam, name):
    _, _, s, e_half = proj.shape
    n_heads = 2 * e_half // HEAD_DIM
    sp_ = _lru_specs(s, e_half, n_heads)

    def body(pg_ref, cw_ref, cb_ref, wa_ref, ba_ref, wx_ref, bx_ref, lam_ref, y_ref,
             sa_ref, sb_ref, sh_ref):
        v, _, _, _, _, i, _, a, nm = _lru_forward_parts(
            pg_ref[0].astype(F32), cw_ref, cb_ref, wa_ref, ba_ref, wx_ref, bx_ref, lam_ref)
        hs = _scan(a, nm * (i * v), sa_ref, sb_ref, sh_ref, reverse=False)
        g = pg_ref[1].astype(F32)
        y_ref[...] = (hs * (g * _sigmoid(g))).astype(BF16)

    return pl.pallas_call(
        body, name=name, grid=(n_heads,),
        in_specs=[sp_["pair"], sp_["conv_w"], sp_["chan"], sp_["w"], sp_["bias"], sp_["w"],
                  sp_["bias"], sp_["chan"]],
        out_specs=sp_["plane"],
        out_shape=jax.ShapeDtypeStruct((s, 2 * e_half), BF16),
        scratch_shapes=[pltpu.VMEM((s, HEAD_DIM), F32)] * 3,
        compiler_params=_params(("parallel",)),
    )(proj, conv_w, conv_b, w_a, b_a, w_x, b_x, lam)


def _lru_bwd(proj, dy, conv_w, conv_b, w_a, b_a, w_x, b_x, lam, name):
    _, _, s, e_half = proj.shape
    e = 2 * e_half
    c = HEAD_DIM
    n_heads = e // c
    hp = e_half // c
    sp_ = _lru_specs(s, e_half, n_heads)

    def body(pg_ref, dy_ref, cw_ref, cb_ref, wa_ref, ba_ref, wx_ref, bx_ref, lam_ref,
             dpg_ref, dwa_ref, dwx_ref, dba_ref, dbx_ref, dlam_ref, dcw_ref, dcb_ref,
             sa_ref, sb_ref, sh_ref, sd_ref):
        vp = pg_ref[0].astype(F32)
        v, vb, wa, wx, r, i, sp, a, nm = _lru_forward_parts(
            vp, cw_ref, cb_ref, wa_ref, ba_ref, wx_ref, bx_ref, lam_ref)
        iv = i * v
        hs = _scan(a, nm * iv, sa_ref, sb_ref, sh_ref, reverse=False)
        g = pg_ref[1].astype(F32)
        dyv = dy_ref[...].astype(F32)
        sig = _sigmoid(g)
        dpg_ref[1] = (dyv * hs * (sig * (1.0 + g * (1.0 - sig)))).astype(BF16)
        dh = _scan(_shift_up(a, 1), dyv * (g * sig), sa_ref, sb_ref, sd_ref, reverse=True)
        da = dh * _shift_down(hs, 1)
        div = dh * nm
        a2 = a * a
        dla = da * a - (dh * iv) * (a2 / nm)
        dzr = (dla * ((-RGLRU_C) * sp)) * (r * (1.0 - r))
        dzi = (div * v) * (i * (1.0 - i))
        lam = lam_ref[...]
        dlam_ref[...] = _colsum(dla * ((-RGLRU_C) * r)) * (-_sigmoid(-lam))
        dba_ref[...] = _colsum(dzr)
        dbx_ref[...] = _colsum(dzi)
        dzr_b, dzi_b = dzr.astype(BF16), dzi.astype(BF16)
        tn = (((0,), (0,)), ((), ()))
        nt = (((1,), (1,)), ((), ()))
        dwa_ref[...] = lax.dot_general(vb, dzr_b, tn, preferred_element_type=F32)
        dwx_ref[...] = lax.dot_general(vb, dzi_b, tn, preferred_element_type=F32)
        dv = (div * i + lax.dot_general(dzr_b, wa, nt, preferred_element_type=F32)
              + lax.dot_general(dzi_b, wx, nt, preferred_element_type=F32))
        dcb_ref[...] = _colsum(dv)
        dvp = jnp.zeros_like(dv)
        for k in range(4):
            dvp = dvp + cw_ref[k:k + 1, :] * _shift_up(dv, 3 - k)
            dcw_ref[k:k + 1, :] = _colsum(dv * _shift_down(vp, 3 - k))
        dpg_ref[0] = dvp.astype(BF16)

    head_mat = pl.BlockSpec((None, c, c), lambda h: (h, 0, 0))
    outs = pl.pallas_call(
        body, name=name, grid=(n_heads,),
        in_specs=[sp_["pair"], sp_["plane"], sp_["conv_w"], sp_["chan"], sp_["w"], sp_["bias"],
                  sp_["w"], sp_["bias"], sp_["chan"]],
        out_specs=[sp_["pair"], head_mat, head_mat, sp_["bias"], sp_["bias"],
                   sp_["chan"], sp_["conv_w"], sp_["chan"]],
        out_shape=[jax.ShapeDtypeStruct((2, 2, s, e_half), BF16),
                   jax.ShapeDtypeStruct((n_heads, c, c), F32), jax.ShapeDtypeStruct((n_heads, c, c), F32),
                   jax.ShapeDtypeStruct((n_heads, 1, c), F32), jax.ShapeDtypeStruct((n_heads, 1, c), F32),
                   jax.ShapeDtypeStruct((1, e), F32), jax.ShapeDtypeStruct((4, e), F32),
                   jax.ShapeDtypeStruct((1, e), F32)],
        scratch_shapes=[pltpu.VMEM((s, c), F32)] * 4,
        compiler_params=_params(("parallel",)),
    )(proj, dy, conv_w, conv_b, w_a, b_a, w_x, b_x, lam)
    return tuple(outs)


def _ada_fwd(c_all, ada_w, name):
    n_l, d, f = ada_w.shape
    bf = _blk(f, 512)

    def body(c_ref, w_ref, o_ref):
        cv = c_ref[...]
        sc = (cv * _sigmoid(cv)).astype(BF16)
        o_ref[...] = jnp.dot(sc, w_ref[...].astype(BF16), preferred_element_type=F32)

    return pl.pallas_call(
        body, name=name, grid=(n_l, f // bf),
        in_specs=[pl.BlockSpec((8, d), lambda l, j: (0, 0)), pl.BlockSpec((None, d, bf), lambda l, j: (l, 0, j))],
        out_specs=pl.BlockSpec((None, 8, bf), lambda l, j: (l, 0, j)),
        out_shape=jax.ShapeDtypeStruct((n_l, 8, f), F32),
        compiler_params=_params(("parallel", "parallel")),
    )(c_all, ada_w)


def _ada_bwd_adamw(c_t, dmod, w, m, v, name):
    n_l, d, f = w.shape
    bf = _blk(f, 256)

    def body(c_ref, dm_ref, w_ref, m_ref, v_ref, g_ref, d_ref, m2_ref, v2_ref):
        cv = c_ref[...]
        sc = cv * _sigmoid(cv)
        dm = dm_ref[...]
        g = sc[:, 0:1] * dm[0:1, :]
        for b in range(1, 8):
            g = g + sc[:, b:b + 1] * dm[b:b + 1, :]
        g_ref[...] = g
        dl, m2, v2 = _adamw_math(w_ref[...], g, m_ref[...], v_ref[...])
        d_ref[...] = dl
        m2_ref[...] = m2
        v2_ref[...] = v2

    big = pl.BlockSpec((None, d, bf), lambda l, j: (l, 0, j))
    return pl.pallas_call(
        body, name=name, grid=(n_l, f // bf),
        in_specs=[pl.BlockSpec((d, 8), lambda l, j: (0, 0)), pl.BlockSpec((None, 8, bf), lambda l, j: (l, 0, j)),
                  big, big, big],
        out_specs=[big] * 4, out_shape=[jax.ShapeDtypeStruct((n_l, d, f), F32)] * 4,
        compiler_params=_params(("parallel", "parallel")),
    )(c_t, dmod, w, m, v)


def _pack(parts):
    padded, offs, n = [], [], 0
    for p in parts:
        p = p.reshape(-1)
        size = -(-p.shape[0] // PACK) * PACK
        offs.append(n)
        n += size
        padded.append(jnp.pad(p, (0, size - p.shape[0])) if size != p.shape[0] else p)
    return jnp.concatenate(padded), offs, n


def kernel(x, c, norm_g, ada_w, ada_b, sc_w_in, sc_conv_w, sc_w_out, lru_w_in, lru_conv_w, lru_conv_b, lru_w_a, lru_b_a, lru_w_x, lru_b_x, lru_lambda, lru_w_out, final_g, loss_target, m_norm_g, m_ada_w, m_ada_b, m_sc_w_in, m_sc_conv_w, m_sc_w_out, m_lru_w_in, m_lru_conv_w, m_lru_conv_b, m_lru_w_a, m_lru_b_a, m_lru_w_x, m_lru_b_x, m_lru_lambda, m_lru_w_out, m_final_g, v_norm_g, v_ada_w, v_ada_b, v_sc_w_in, v_sc_conv_w, v_sc_w_out, v_lru_w_in, v_lru_conv_w, v_lru_conv_b, v_lru_w_a, v_lru_b_a, v_lru_w_x, v_lru_b_x, v_lru_lambda, v_lru_w_out, v_final_g):
    xi, yi, ci = _pos()
    chip = 2 * xi + yi
    batch = 4 * xi + 2 * yi + ci
    core_op = jnp.reshape(ci, (1,)).astype(jnp.int32)
    chip_op = jnp.reshape(chip, (1,)).astype(jnp.int32)
    where_op = jnp.stack([chip, ci]).astype(jnp.int32)

    x2d, tgt = x[0], loss_target[0]
    s, d = x2d.shape
    es = sc_conv_w.shape[2]
    e = 4 * es
    n_heads = lru_w_a.shape[1]
    hj = lru_b_a.shape[2]
    f = ada_w.shape[2]
    row = lambda t: t.reshape(1, -1)

    shards = [sc_w_in[0], sc_w_out[0], lru_w_in[0], lru_w_a[0].reshape(n_heads * hj, HEAD_DIM),
              lru_w_x[0].reshape(n_heads * hj, HEAD_DIM), lru_w_out[0]]
    names = ["sc_w_in", "sc_w_out", "lru_w_in", "lru_w_a", "lru_w_x", "lru_w_out"]
    slots = [_cast_into_slot(chip_op, w, "cast_" + nm) for w, nm in zip(shards, names)]
    in_flight, started = _gather_start([[slots[0]], [slots[1]], slots[2:5], [slots[5]]], "ag_start")

    def arrived(g, after, tag):
        send1, recv1, bufs = in_flight[g]
        send2, recv2, bufs, passed = _gather_forward(send1, recv1, bufs, after, "ag_forward_" + tag)
        return _gather_finish(send2, recv2, bufs, passed, "ag_finish_" + tag)

    small_parts = [c, sc_conv_w, lru_conv_w, lru_conv_b, lru_b_a, lru_b_x, lru_lambda]
    small, offs, n_small = _pack(small_parts)
    got = _allgather8([small.reshape(8, n_small // 8)], False, "ag_small", after=started)[0].reshape(8, n_small)
    c_all = got[:, :d]
    per_chip = got[0::2]

    def chip_part(k, shape):
        size = 1
        for dim in shape:
            size *= dim
        return per_chip[:, offs[k]:offs[k] + size].reshape((4,) + shape)

    conv_w0 = jnp.transpose(chip_part(1, (3, es)), (1, 0, 2)).reshape(3, e)
    conv_w1 = jnp.transpose(chip_part(2, (4, es)), (1, 0, 2)).reshape(4, e)
    conv_b1 = chip_part(3, (es,)).reshape(1, e)
    b_a = jnp.transpose(chip_part(4, (n_heads, hj)), (1, 0, 2)).reshape(n_heads, 1, 4 * hj)
    b_x = jnp.transpose(chip_part(5, (n_heads, hj)), (1, 0, 2)).reshape(n_heads, 1, 4 * hj)
    lam = chip_part(6, (es,)).reshape(1, e)

    mod_nb = _ada_fwd(c_all, ada_w, "ada_fwd")
    mods = _allgather8([mod_nb.reshape(16, f)], False, "ag_mod")[0].reshape(8, 2, 8, f)[0::2]
    mine = lax.dynamic_index_in_dim(mods, batch, axis=2, keepdims=False)
    mod = jnp.transpose(mine, (1, 0, 2)).reshape(2, 4 * f) + ada_b
    shift = [row(mod[l, :d]) for l in range(2)]
    scale = [row(mod[l, d:2 * d]) for l in range(2)]
    gate = [row(mod[l, 2 * d:]) for l in range(2)]
    ng = [row(norm_g[l]) for l in range(2)]

    h0 = _norm_mod_fwd(x2d, ng[0], scale[0], shift[0], "norm0")
    w_in0 = arrived(0, h0, "sc_w_in")[0].reshape(4, d, e)
    proj0 = _mm(h0[None], w_in0, "nn", BF16, "sc_in")
    w_out0 = arrived(1, proj0, "sc_w_out")[0].reshape(1, e, d)
    y0 = _sc_fwd(proj0, conv_w0, "sc_mix")
    o0 = _mm(y0[None], w_out0, "nn", F32, "sc_out")[0]
    x1, h1 = _norm_mod_fwd(x2d, ng[1], scale[1], shift[1], "norm1", o=o0, gate=gate[0])
    lru_ws = arrived(2, h1, "lru_w_in")
    w_in1 = lru_ws[0].reshape(4, d, e // 2)
    w_a = lru_ws[1].reshape(4, n_heads, hj, HEAD_DIM)
    w_x = lru_ws[2].reshape(4, n_heads, hj, HEAD_DIM)
    proj1 = _mm(h1[None], w_in1, "nn", BF16, "lru_in")
    pairs1 = proj1.reshape(2, 2, s, e // 2)
    y1 = _lru_fwd(pairs1, conv_w1, conv_b1, w_a, b_a, w_x, b_x, lam, "lru_mix")
    w_out1 = arrived(3, y1, "lru_w_out")[0].reshape(1, e, d)
    o1 = _mm(y1[None], w_out1, "nn", F32, "lru_out")[0]
    dx2, do1, acc_f = _final_loss(x1, o1, gate[1], row(final_g), tgt, "final_loss")

    dy1 = _mm(do1[None], w_out1, "nt", BF16, "lru_out_dx")[0]
    g_w_out1 = _mm(y1[None], do1[None], "tn", F32, "lru_out_dw", bm=512, bn=2048)
    dpairs1, g_wa, g_wx, g_ba, g_bx, g_lam, g_cw1, g_cb1 = _lru_bwd(
        pairs1, dy1, conv_w1, conv_b1, w_a, b_a, w_x, b_x, lam, "lru_mix_bwd")
    dproj1 = dpairs1.reshape(4, s, e // 2)
    dh1 = _mm(dproj1, w_in1, "nt", F32, "lru_in_dx")[0]
    g_w_in1 = _mm(h1[None], dproj1, "tn", F32, "lru_in_dw", bm=512, bn=2048)
    dx1, do0, acc1 = _norm_mod_bwd(dh1, x1, ng[1], scale[1], dx2, "norm1_bwd", o_prev=o0, gate_prev=gate[0])

    dy0 = _mm(do0[None], w_out0, "nt", BF16, "sc_out_dx")[0]
    g_w_out0 = _mm(y0[None], do0[None], "tn", F32, "sc_out_dw", bm=512, bn=2048)
    dproj0, g_cw0 = _sc_bwd(proj0, dy0, conv_w0, "sc_mix_bwd")
    dh0 = _mm(dproj0, w_in0, "nt", F32, "sc_in_dx")[0]
    g_w_in0 = _mm(h0[None], dproj0, "tn", F32, "sc_in_dw", bm=512, bn=2048)
    grad_x, acc0 = _norm_mod_bwd(dh0, x2d, ng[0], scale[0], dx1, "norm0_bwd")

    def head_major_to_chip_major(t):
        return jnp.transpose(t.reshape(n_heads, 4, hj, HEAD_DIM), (1, 0, 2, 3)).reshape(4, n_heads * hj, HEAD_DIM)

    grads = [g_w_in0, g_w_out0.reshape(4, es, d), g_w_in1, head_major_to_chip_major(g_wa),
             head_major_to_chip_major(g_wx), g_w_out1.reshape(4, es, d)]
    from_sibling = _sibling_send_other_half(grads)
    parts = [_sum_own_and_sibling(core_op, g, r1, "rs_sum1_" + nm) for g, r1, nm in zip(grads, from_sibling, names)]
    from_chips = _chips_send_partials(parts)
    halves = [_sum_chips(where_op, p, r2, "rs_sum2_" + nm) for p, r2, nm in zip(parts, from_chips, names)]
    g_full = _sibling_share_half(halves)

    big_m = [m_sc_w_in, m_sc_w_out, m_lru_w_in, m_lru_w_a, m_lru_w_x, m_lru_w_out]
    big_v = [v_sc_w_in, v_sc_w_out, v_lru_w_in, v_lru_w_a, v_lru_w_x, v_lru_w_out]
    big_w = [sc_w_in, sc_w_out, lru_w_in, lru_w_a, lru_w_x, lru_w_out]
    big = {}
    for nm, w2, g2, m4, v4, w4 in zip(names, shards, g_full, big_m, big_v, big_w):
        dl, m2, v2 = _adamw(w2, g2, m4.reshape(w2.shape), v4.reshape(w2.shape), "adamw_" + nm)
        big[nm] = tuple(t.reshape(w4.shape) for t in (g2, dl, m2, v2))

    dmod = jnp.stack([jnp.concatenate([acc0[1], acc0[0], acc1[3]]),
                      jnp.concatenate([acc1[1], acc1[0], acc_f[1]])])
    part_list = [jnp.stack([acc0[2], acc1[2]]), acc_f[0], acc_f[2, :1], g_cw0, g_cw1, g_cb1, g_ba, g_bx,
                 g_lam, dmod]
    partials, poffs, n_part = _pack(part_list)
    every = _allgather8([partials.reshape(8, n_part // 8)], False, "ag_partials")[0].reshape(8, n_part)
    total = _sum_rows8(every, "sum_partials")[0]

    def tot(k, shape):
        size = 1
        for dim in shape:
            size *= dim
        return total[poffs[k]:poffs[k] + size].reshape(shape)

    def my_cols(t, width):
        return lax.dynamic_slice_in_dim(t, chip * width, width, axis=t.ndim - 1)

    loss = tot(2, (1,))[0]
    g_norm_g, g_final_g, g_ada_b = tot(0, (2, d)), tot(1, (d,)), tot(9, (2, 3 * d))
    g_sc_conv_w = my_cols(tot(3, (3, e)), es)[None]
    g_lru_conv_w = my_cols(tot(4, (4, e)), es)[None]
    g_lru_conv_b = my_cols(tot(5, (1, e)), es)
    g_lru_b_a = my_cols(tot(6, (n_heads, 4 * hj)), hj)[None]
    g_lru_b_x = my_cols(tot(7, (n_heads, 4 * hj)), hj)[None]
    g_lru_lambda = my_cols(tot(8, (1, e)), es)

    dmod_all = every[:, poffs[9]:poffs[9] + 6 * d].reshape(8, 2, 3 * d)
    dmod_mine = jnp.transpose(my_cols(dmod_all, f), (1, 0, 2))
    ada = _ada_bwd_adamw(jnp.transpose(c_all), dmod_mine, ada_w, m_ada_w, v_ada_w, "ada_bwd_adamw")

    small_names = ["norm_g", "ada_b", "final_g", "sc_conv_w", "lru_conv_w", "lru_conv_b", "lru_b_a",
                   "lru_b_x", "lru_lambda"]
    small_w = [norm_g, ada_b, final_g, sc_conv_w, lru_conv_w, lru_conv_b, lru_b_a, lru_b_x, lru_lambda]
    small_g = [g_norm_g, g_ada_b, g_final_g, g_sc_conv_w, g_lru_conv_w, g_lru_conv_b, g_lru_b_a,
               g_lru_b_x, g_lru_lambda]
    small_m = [m_norm_g, m_ada_b, m_final_g, m_sc_conv_w, m_lru_conv_w, m_lru_conv_b, m_lru_b_a,
               m_lru_b_x, m_lru_lambda]
    small_v = [v_norm_g, v_ada_b, v_final_g, v_sc_conv_w, v_lru_conv_w, v_lru_conv_b, v_lru_b_a,
               v_lru_b_x, v_lru_lambda]
    pw, soffs, n_s = _pack(small_w)
    pg, pm, pv = _pack(small_g)[0], _pack(small_m)[0], _pack(small_v)[0]
    shape2 = (n_s // PACK, PACK)
    pd, pm2, pv2 = _adamw(pw.reshape(shape2), pg.reshape(shape2), pm.reshape(shape2), pv.reshape(shape2),
                          "adamw_small")
    small = {}
    for k, (nm, w_) in enumerate(zip(small_names, small_w)):
        take = lambda t: t.reshape(-1)[soffs[k]:soffs[k] + w_.size].reshape(w_.shape)
        small[nm] = (small_g[k].reshape(w_.shape), take(pd), take(pm2), take(pv2))

    results = dict(small)
    results.update(big)
    results["ada_w"] = tuple(ada)
    order = ["norm_g", "ada_w", "ada_b", "sc_w_in", "sc_conv_w", "sc_w_out", "lru_w_in", "lru_conv_w",
             "lru_conv_b", "lru_w_a", "lru_b_a", "lru_w_x", "lru_b_x", "lru_lambda", "lru_w_out", "final_g"]
    out = [loss, grad_x[None]]
    for kind in range(4):
        out += [results[nm][kind] for nm in order]
    return tuple(out)
```

```python
import functools

import jax
import jax.numpy as jnp
from jax import lax
from jax.experimental import pallas as pl
from jax.experimental.pallas import tpu as pltpu

F32 = jnp.float32
BF16 = jnp.bfloat16
MESH = pl.DeviceIdType.MESH
ANY = pl.BlockSpec(memory_space=pl.ANY)

RMS_EPS = 1e-6
RGLRU_C = 8.0
HEAD_DIM = 256
ADAM_LR = 0.001
ADAM_B1 = 0.9
ADAM_B2 = 0.999
ADAM_EPS = 1e-08
ADAM_WD = 0.01
ADAM_STEP = 10
V7X_VMEM_LIMIT = 56 * 1024 * 1024
LANES = 128
PACK = 8 * LANES


def _blk(dim, pref, unit=LANES):
    if dim <= pref:
        return dim
    b = (pref // unit) * unit
    while b > unit and dim % b:
        b -= unit
    assert dim % b == 0, (dim, pref, unit)
    return b


def _params(sem=None):
    return pltpu.CompilerParams(dimension_semantics=sem, vmem_limit_bytes=V7X_VMEM_LIMIT)


def _ordered(body, n_in, after):
    if after is None:
        return body, [], []

    def ordered_body(*refs):
        return body(*refs[:n_in], *refs[n_in + 1:])

    return ordered_body, [ANY], [after]


def _pos():
    return lax.axis_index("x"), lax.axis_index("y"), lax.axis_index("c")


def _other_chips(x, y):
    return [(1 - x, y), (x, 1 - y), (1 - x, 1 - y)]


def _allgather8(arrs, name):
    n_t = len(arrs)
    ms = [a.shape[0] for a in arrs]

    def body(*refs):
        ins, outs = refs[:n_t], refs[n_t:2 * n_t]
        send_sems, recv_sems, local_sems = refs[2 * n_t:]
        x, y, c = _pos()
        me, sibling = (x, y, c), (x, y, 1 - c)
        chips = _other_chips(x, y)

        def rows(t, px, py, pc):
            return outs[t].at[pl.ds((4 * px + 2 * py + pc) * ms[t], ms[t])]

        def copy(t, k, block, to, src=None):
            return pltpu.make_async_remote_copy(
                src_ref=rows(t, *block) if src is None else src, dst_ref=rows(t, *block),
                send_sem=send_sems.at[7 * t + k], recv_sem=recv_sems.at[7 * t + k],
                device_id=to, device_id_type=MESH)

        mine, first, passed = [], [], []
        for t in range(n_t):
            src = ins[t]
            cp = pltpu.make_async_copy(src, rows(t, *me), local_sems.at[t])
            cp.start()
            mine.append(cp)
            sends = [copy(t, 0, me, sibling, src=src)]
            sends += [copy(t, 1 + j, me, (*chip, c), src=src) for j, chip in enumerate(chips)]
            for cp in sends:
                cp.start()
            first += sends
        for t in range(n_t):
            for j, chip in enumerate(chips):
                copy(t, 1 + j, (*chip, c), me).wait_recv()
                cp = copy(t, 4 + j, (*chip, c), sibling)
                cp.start()
                passed.append(cp)
        for t in range(n_t):
            copy(t, 0, sibling, me).wait_recv()
            for j, chip in enumerate(chips):
                copy(t, 4 + j, (*chip, 1 - c), me).wait_recv()
        for cp in first + passed:
            cp.wait_send()
        for cp in mine:
            cp.wait()

    return pl.pallas_call(
        body, name=name,
        out_shape=[jax.ShapeDtypeStruct((8 * m, a.shape[1]), a.dtype) for m, a in zip(ms, arrs)],
        in_specs=[ANY] * n_t, out_specs=[ANY] * n_t,
        scratch_shapes=[pltpu.SemaphoreType.DMA((7 * n_t,)), pltpu.SemaphoreType.DMA((7 * n_t,)),
                        pltpu.SemaphoreType.DMA((n_t,))],
    )(*arrs)


HBM = pl.BlockSpec(memory_space=pltpu.HBM)
SEM = pl.BlockSpec(memory_space=pltpu.SEMAPHORE)
TOKEN = pl.BlockSpec(memory_space=pltpu.VMEM)
IN_FLIGHT = pltpu.CompilerParams(has_side_effects=pltpu.SideEffectType.DATAFLOW_SIDE_EFFECTING)


def _in_hbm(arrs):
    return [pltpu.with_memory_space_constraint(a, pltpu.HBM) for a in arrs]


def _shard_rows(ref, h, px, py, pc):
    return ref.at[pl.ds((4 * px + 2 * py + pc) * h, h)]


def _gather_start(groups, after, name):
    bufs = [b for grp in groups for b in grp]
    n_t, n_g = len(bufs), len(groups)

    def body(*refs):
        sems, thru = refs[n_t + 1:n_t + 1 + 2 * n_g], refs[n_t + 1 + 2 * n_g:2 * n_t + 1 + 2 * n_g]
        token = refs[-1]
        x, y, c = _pos()
        t = 0
        for g, grp in enumerate(groups):
            for i in range(len(grp)):
                h = bufs[t].shape[0] // 8
                rows = _shard_rows(thru[t], h, x, y, c)
                for j, chip in enumerate(_other_chips(x, y)):
                    pltpu.make_async_remote_copy(
                        src_ref=rows, dst_ref=rows, send_sem=sems[2 * g].at[3 * i + j],
                        recv_sem=sems[2 * g + 1].at[3 * i + j], device_id=(*chip, c),
                        device_id_type=MESH).start()
                t += 1
        token[...] = jnp.zeros_like(token)

    sem_shapes = []
    for grp in groups:
        sem_shapes += [pltpu.SemaphoreType.DMA((3 * len(grp),))] * 2
    out = pl.pallas_call(
        body, name=name,
        out_shape=sem_shapes + [pltpu.HBM(b.shape, b.dtype) for b in bufs] + [jax.ShapeDtypeStruct((8, LANES), F32)],
        in_specs=[HBM] * n_t + [ANY], out_specs=[SEM] * (2 * n_g) + [HBM] * n_t + [TOKEN],
        input_output_aliases={t: 2 * n_g + t for t in range(n_t)},
        compiler_params=IN_FLIGHT,
    )(*_in_hbm(bufs), after)
    sems, thru, token = out[:2 * n_g], out[2 * n_g:2 * n_g + n_t], out[-1]
    per_group, t = [], 0
    for g, grp in enumerate(groups):
        per_group.append((sems[2 * g], sems[2 * g + 1], thru[t:t + len(grp)]))
        t += len(grp)
    return per_group, token


def _gather_forward(send_sems, recv_sems, bufs, after, name):
    n_t = len(bufs)

    def body(*refs):
        ins = refs[:n_t]
        send1, recv1 = refs[n_t], refs[n_t + 1]
        send2, recv2 = refs[n_t + 3], refs[n_t + 4]
        token = refs[-1]
        x, y, c = _pos()
        for t in range(n_t):
            h = bufs[t].shape[0] // 8
            mine = _shard_rows(ins[t], h, x, y, c)
            for j, chip in enumerate(_other_chips(x, y)):
                landed = _shard_rows(ins[t], h, *chip, c)
                pltpu.make_async_remote_copy(
                    src_ref=mine, dst_ref=landed, send_sem=send1.at[3 * t + j], recv_sem=recv1.at[3 * t + j],
                    device_id=(*chip, c), device_id_type=MESH).wait_recv()
                pltpu.make_async_remote_copy(
                    src_ref=landed, dst_ref=landed, send_sem=send2.at[3 * t + j], recv_sem=recv2.at[3 * t + j],
                    device_id=(x, y, 1 - c), device_id_type=MESH).start()
        for t in range(n_t):
            h = bufs[t].shape[0] // 8
            mine = _shard_rows(ins[t], h, x, y, c)
            for j, chip in enumerate(_other_chips(x, y)):
                pltpu.make_async_remote_copy(
                    src_ref=mine, dst_ref=mine, send_sem=send1.at[3 * t + j], recv_sem=recv1.at[3 * t + j],
                    device_id=(*chip, c), device_id_type=MESH).wait_send()
        token[...] = jnp.zeros_like(token)

    out = pl.pallas_call(
        body, name=name,
        out_shape=[pltpu.SemaphoreType.DMA((3 * n_t,))] * 2 + [pltpu.HBM(b.shape, b.dtype) for b in bufs]
        + [jax.ShapeDtypeStruct((8, LANES), F32)],
        in_specs=[HBM] * n_t + [SEM, SEM, ANY], out_specs=[SEM, SEM] + [HBM] * n_t + [TOKEN],
        input_output_aliases={t: 2 + t for t in range(n_t)},
        compiler_params=IN_FLIGHT,
    )(*bufs, send_sems, recv_sems, after)
    return out[0], out[1], out[2:2 + n_t], out[-1]


def _gather_finish(send_sems, recv_sems, bufs, after, name):
    n_t = len(bufs)

    def body(*refs):
        ins = refs[:n_t]
        send2, recv2 = refs[n_t], refs[n_t + 1]
        x, y, c = _pos()
        for t in range(n_t):
            h = bufs[t].shape[0] // 8
            for j, chip in enumerate(_other_chips(x, y)):
                sent = _shard_rows(ins[t], h, *chip, c)
                got = _shard_rows(ins[t], h, *chip, 1 - c)
                cp = pltpu.make_async_remote_copy(
                    src_ref=sent, dst_ref=got, send_sem=send2.at[3 * t + j], recv_sem=recv2.at[3 * t + j],
                    device_id=(x, y, 1 - c), device_id_type=MESH)
                cp.wait_send()
                cp.wait_recv()

    return pl.pallas_call(
        body, name=name, out_shape=[pltpu.HBM(b.shape, b.dtype) for b in bufs],
        in_specs=[HBM] * n_t + [SEM, SEM, ANY], out_specs=[HBM] * n_t,
        input_output_aliases={t: t for t in range(n_t)},
        compiler_params=IN_FLIGHT,
    )(*bufs, send_sems, recv_sems, after)


def _exchange_start(name, srcs, lands, n_copies, plan, after):
    ns, nl = len(srcs), len(lands)

    def body(*refs):
        base = ns + nl + 1
        send_sems, recv_sems = refs[base], refs[base + 1]
        src_refs, land_refs = refs[base + 2:base + 2 + ns], refs[base + 2 + ns:base + 2 + ns + nl]
        token = refs[-1]
        x, y, c = _pos()
        copies = plan(src_refs, land_refs, x, y, c)
        assert len(copies) == n_copies
        for k, (src, dst, dev) in enumerate(copies):
            pltpu.make_async_remote_copy(
                src_ref=src, dst_ref=dst, send_sem=send_sems.at[k], recv_sem=recv_sems.at[k],
                device_id=dev, device_id_type=MESH).start()
        token[...] = jnp.zeros_like(token)

    out = pl.pallas_call(
        body, name=name,
        out_shape=[pltpu.SemaphoreType.DMA((n_copies,))] * 2
        + [pltpu.HBM(a.shape, a.dtype) for a in list(srcs) + list(lands)] + [jax.ShapeDtypeStruct((8, LANES), F32)],
        in_specs=[HBM] * (ns + nl) + [ANY], out_specs=[SEM, SEM] + [HBM] * (ns + nl) + [TOKEN],
        input_output_aliases={i: 2 + i for i in range(ns + nl)},
        compiler_params=IN_FLIGHT,
    )(*_in_hbm(list(srcs) + list(lands)), after)
    return out[0], out[1], out[2:2 + ns], out[2 + ns:2 + ns + nl], out[-1]


def _exchange_wait(name, send_sems, recv_sems, srcs, lands, plan, after):
    ns, nl = len(srcs), len(lands)

    def body(*refs):
        src_refs, land_refs = refs[:ns], refs[ns:ns + nl]
        send, recv = refs[ns + nl], refs[ns + nl + 1]
        x, y, c = _pos()
        for k, (src, dst, dev) in enumerate(plan(src_refs, land_refs, x, y, c)):
            cp = pltpu.make_async_remote_copy(
                src_ref=src, dst_ref=dst, send_sem=send.at[k], recv_sem=recv.at[k],
                device_id=dev, device_id_type=MESH)
            cp.wait_send()
            cp.wait_recv()

    out = pl.pallas_call(
        body, name=name, out_shape=[pltpu.HBM(a.shape, a.dtype) for a in list(srcs) + list(lands)],
        in_specs=[HBM] * (ns + nl) + [SEM, SEM, ANY], out_specs=[HBM] * (ns + nl),
        input_output_aliases={i: i for i in range(ns + nl)},
        compiler_params=IN_FLIGHT,
    )(*srcs, *lands, send_sems, recv_sems, after)
    return out[:ns], out[ns:]


def _plan_other_half_to_sibling(src_refs, land_refs, x, y, c):
    out = []
    for g_ref, r_ref in zip(src_refs, land_refs):
        h = g_ref.shape[1] // 2
        out.append((g_ref.at[:, pl.ds((1 - c) * h, h), :], r_ref, (x, y, 1 - c)))
    return out


def _plan_partials_to_chips(src_refs, land_refs, x, y, c):
    out = []
    for p_ref, r_ref in zip(src_refs, land_refs):
        for j, (px, py) in enumerate(_other_chips(x, y)):
            out.append((p_ref.at[2 * px + py], r_ref.at[j], (px, py, c)))
    return out


def _plan_share_half(src_refs, land_refs, x, y, c):
    out = []
    for f_ref in land_refs:
        h = f_ref.shape[0] // 2
        rows = f_ref.at[pl.ds(c * h, h)]
        out.append((rows, rows, (x, y, 1 - c)))
    return out


def _cast_into_slot(chip, w, name):
    r, c = w.shape
    br, bc = _blk(r, 512, 8), _blk(c, 2048)
    nb = r // br

    def body(chip_ref, w_ref, o_ref):
        o_ref[...] = w_ref[...].astype(BF16)

    grid_spec = pltpu.PrefetchScalarGridSpec(
        num_scalar_prefetch=1, grid=(nb, c // bc),
        in_specs=[pl.BlockSpec((br, bc), lambda i, j, chip_ref: (i, j))],
        out_specs=pl.BlockSpec((br, bc), lambda i, j, chip_ref: (chip_ref[0] * nb + i, j)))
    return pl.pallas_call(
        body, name=name, grid_spec=grid_spec,
        out_shape=jax.ShapeDtypeStruct((4 * r, c), BF16),
        compiler_params=_params(("parallel", "parallel")),
    )(chip, w)


def _sum_own_and_sibling(core, g, r1, name):
    _, r, c = g.shape
    h = r // 2
    br, bc = _blk(h, 256, 8), _blk(c, 2048)
    nb = h // br

    def body(core_ref, g_ref, r_ref, o_ref):
        o_ref[...] = (g_ref[...] + r_ref[...]).astype(BF16)

    grid_spec = pltpu.PrefetchScalarGridSpec(
        num_scalar_prefetch=1, grid=(4, nb, c // bc),
        in_specs=[pl.BlockSpec((None, br, bc), lambda k, i, j, core_ref: (k, core_ref[0] * nb + i, j)),
                  pl.BlockSpec((None, br, bc), lambda k, i, j, core_ref: (k, i, j))],
        out_specs=pl.BlockSpec((None, br, bc), lambda k, i, j, core_ref: (k, i, j)))
    return pl.pallas_call(
        body, name=name, grid_spec=grid_spec,
        out_shape=jax.ShapeDtypeStruct((4, h, c), BF16),
        compiler_params=_params(("parallel", "parallel", "parallel")),
    )(core, g, r1)


def _sum_chips(where, p, r2, name):
    _, h, c = p.shape
    br, bc = _blk(h, 256, 8), _blk(c, 2048)
    nb = h // br

    def body(where_ref, p_ref, r_ref, o_ref):
        acc = p_ref[...].astype(F32)
        for j in range(3):
            acc = acc + r_ref[j].astype(F32)
        o_ref[...] = acc

    grid_spec = pltpu.PrefetchScalarGridSpec(
        num_scalar_prefetch=1, grid=(nb, c // bc),
        in_specs=[pl.BlockSpec((None, br, bc), lambda i, j, where_ref: (where_ref[0], i, j)),
                  pl.BlockSpec((3, br, bc), lambda i, j, where_ref: (0, i, j))],
        out_specs=pl.BlockSpec((br, bc), lambda i, j, where_ref: (where_ref[1] * nb + i, j)))
    return pl.pallas_call(
        body, name=name, grid_spec=grid_spec,
        out_shape=jax.ShapeDtypeStruct((2 * h, c), F32),
        compiler_params=_params(("parallel", "parallel")),
    )(where, p, r2)


def _adamw_math(w, g, m, v):
    m2 = ADAM_B1 * m + (1.0 - ADAM_B1) * g
    v2 = ADAM_B2 * v + (1.0 - ADAM_B2) * (g * g)
    m_hat = m2 / (1.0 - ADAM_B1 ** ADAM_STEP)
    v_hat = v2 / (1.0 - ADAM_B2 ** ADAM_STEP)
    delta = -ADAM_LR * (m_hat / (jnp.sqrt(v_hat) + ADAM_EPS) + ADAM_WD * w)
    return delta, m2, v2


def _adamw(w, g, m, v, name, after=None):
    r, c = w.shape
    br, bc = _blk(r, 128, 8), _blk(c, 2048)

    def body(w_ref, g_ref, m_ref, v_ref, d_ref, m2_ref, v2_ref):
        d, m2, v2 = _adamw_math(w_ref[...], g_ref[...], m_ref[...], v_ref[...])
        d_ref[...] = d
        m2_ref[...] = m2
        v2_ref[...] = v2

    spec = pl.BlockSpec((br, bc), lambda i, j: (i, j))
    body, extra_specs, extra = _ordered(body, 4, after)
    return pl.pallas_call(
        body, name=name, grid=(r // br, c // bc),
        in_specs=[spec] * 4 + extra_specs, out_specs=[spec] * 3,
        out_shape=[jax.ShapeDtypeStruct((r, c), F32)] * 3,
        compiler_params=_params(("parallel", "parallel")),
    )(w, g, m, v, *extra)


def _sum_rows8(g, name, after=None):
    n = g.shape[1]

    def body(g_ref, o_ref):
        acc = g_ref[0:1, :]
        for k in range(1, 8):
            acc = acc + g_ref[k:k + 1, :]
        o_ref[...] = acc

    body, extra_specs, extra = _ordered(body, 1, after)
    return pl.pallas_call(
        body, name=name, out_shape=jax.ShapeDtypeStruct((1, n), F32),
        in_specs=[pl.BlockSpec(memory_space=pltpu.VMEM)] + extra_specs,
        out_specs=pl.BlockSpec(memory_space=pltpu.VMEM),
        compiler_params=_params(),
    )(g, *extra)


def _mm(a, b, mode, out_dtype, name, bm=1024, bn=None, after=None):
    if mode == "nn":
        (_, m, k), (g, _, n) = a.shape, b.shape
    elif mode == "tn":
        (_, k, m), (g, _, n) = a.shape, b.shape
    else:
        (g, m, k), (_, n, _) = a.shape, b.shape
    if bn is None:
        bn = 1024 if k <= 2048 else 512
    bm, bn = _blk(m, bm), _blk(n, bn)

    if mode == "nt":
        def body(a_ref, b_ref, o_ref, acc_ref):
            part = lax.dot_general(a_ref[...], b_ref[...], (((1,), (1,)), ((), ())),
                                   preferred_element_type=F32)
            if g == 1:
                o_ref[...] = part.astype(out_dtype)
            else:
                gi = pl.program_id(2)

                @pl.when(gi == 0)
                def _():
                    acc_ref[...] = part

                @pl.when(gi > 0)
                def _():
                    acc_ref[...] += part

                @pl.when(gi == g - 1)
                def _():
                    o_ref[...] = acc_ref[...].astype(out_dtype)

        body, extra_specs, extra = _ordered(body, 2, after)
        return pl.pallas_call(
            body, name=name, grid=(m // bm, n // bn, g),
            in_specs=[pl.BlockSpec((None, bm, k), lambda i, j, gi: (gi, i, 0)),
                      pl.BlockSpec((None, bn, k), lambda i, j, gi: (gi, j, 0))] + extra_specs,
            out_specs=pl.BlockSpec((None, bm, bn), lambda i, j, gi: (0, i, j)),
            out_shape=jax.ShapeDtypeStruct((1, m, n), out_dtype),
            scratch_shapes=[pltpu.VMEM((bm, bn), F32)],
            compiler_params=_params(("parallel", "parallel", "arbitrary")),
        )(a, b, *extra)

    contract = (((1,), (0,)), ((), ())) if mode == "nn" else (((0,), (0,)), ((), ()))

    def body(a_ref, b_ref, o_ref):
        o_ref[...] = lax.dot_general(a_ref[...], b_ref[...], contract,
                                     preferred_element_type=F32).astype(out_dtype)

    a_spec = (pl.BlockSpec((None, bm, k), lambda i, gi, j: (0, i, 0)) if mode == "nn"
              else pl.BlockSpec((None, k, bm), lambda i, gi, j: (0, 0, i)))
    body, extra_specs, extra = _ordered(body, 2, after)
    return pl.pallas_call(
        body, name=name, grid=(m // bm, g, n // bn),
        in_specs=[a_spec, pl.BlockSpec((None, k, bn), lambda i, gi, j: (gi, 0, j))] + extra_specs,
        out_specs=pl.BlockSpec((None, bm, bn), lambda i, gi, j: (gi, i, j)),
        out_shape=jax.ShapeDtypeStruct((g, m, n), out_dtype),
        compiler_params=_params(("parallel", "parallel", "parallel")),
    )(a, b, *extra)


def _row_specs(br, d):
    return (pl.BlockSpec((br, d), lambda i: (i, 0)), pl.BlockSpec((1, d), lambda i: (0, 0)),
            pl.BlockSpec((8, d), lambda i: (0, 0)))


def _rstd(xv):
    return lax.rsqrt(jnp.mean(xv * xv, axis=-1, keepdims=True) + RMS_EPS)


def _colsum(v):
    return jnp.sum(v, axis=0, keepdims=True)


def _norm_mod_fwd(x, g, scale, shift, name, o=None, gate=None):
    s, d = x.shape
    br = _blk(s, 256, 8)
    has_res = o is not None
    row, vec, _ = _row_specs(br, d)

    def body(*refs):
        if has_res:
            x_ref, o_ref, gate_ref, g_ref, sc_ref, sh_ref, x1_ref, h_ref = refs
            xv = x_ref[...] + gate_ref[...] * o_ref[...]
            x1_ref[...] = xv
        else:
            x_ref, g_ref, sc_ref, sh_ref, h_ref = refs
            xv = x_ref[...]
        n = xv * _rstd(xv) * g_ref[...]
        h_ref[...] = (n * (1.0 + sc_ref[...]) + sh_ref[...]).astype(BF16)

    ins = [x] + ([o, gate] if has_res else []) + [g, scale, shift]
    in_specs = [row] + ([row, vec] if has_res else []) + [vec] * 3
    out_shape = ([jax.ShapeDtypeStruct((s, d), F32)] if has_res else []) + [jax.ShapeDtypeStruct((s, d), BF16)]
    out = pl.pallas_call(
        body, name=name, grid=(s // br,), in_specs=in_specs, out_specs=[row] * len(out_shape),
        out_shape=out_shape, compiler_params=_params(("parallel",)),
    )(*ins)
    return out if has_res else out[0]


def _final_loss(x1, o1, gate1, final_g, tgt, name):
    s, d = x1.shape
    br = _blk(s, 256, 8)
    row, vec, acc = _row_specs(br, d)

    def body(x1_ref, o_ref, gate_ref, g_ref, t_ref, dx_ref, do_ref, acc_ref):
        @pl.when(pl.program_id(0) == 0)
        def _():
            acc_ref[...] = jnp.zeros_like(acc_ref)

        gate, o, g = gate_ref[...], o_ref[...], g_ref[...]
        x2 = x1_ref[...] + gate * o
        r = _rstd(x2)
        xh = x2 * r
        err = xh * g - t_ref[...]
        loss = 0.5 * _colsum(jnp.mean(err * err, axis=-1, keepdims=True))
        dout = err * (1.0 / d)
        dxh = dout * g
        dx2 = r * (dxh - xh * jnp.mean(dxh * xh, axis=-1, keepdims=True))
        dx_ref[...] = dx2
        do_ref[...] = (dx2 * gate).astype(BF16)
        acc_ref[0:1, :] += _colsum(dout * xh)
        acc_ref[1:2, :] += _colsum(dx2 * o)
        acc_ref[2:3, :] += jnp.broadcast_to(loss, (1, d))

    return pl.pallas_call(
        body, name=name, grid=(s // br,),
        in_specs=[row, row, vec, vec, row], out_specs=[row, row, acc],
        out_shape=[jax.ShapeDtypeStruct((s, d), F32), jax.ShapeDtypeStruct((s, d), BF16),
                   jax.ShapeDtypeStruct((8, d), F32)],
        compiler_params=_params(("arbitrary",)),
    )(x1, o1, gate1, final_g, tgt)


def _norm_mod_bwd(dh, x, g, scale, dx_next, name, o_prev=None, gate_prev=None, after=None):
    s, d = x.shape
    br = _blk(s, 256, 8)
    has_prev = o_prev is not None
    row, vec, acc = _row_specs(br, d)

    def body(*refs):
        if has_prev:
            dh_ref, x_ref, g_ref, sc_ref, dxn_ref, o_ref, gate_ref, dx_ref, do_ref, acc_ref = refs
        else:
            dh_ref, x_ref, g_ref, sc_ref, dxn_ref, dx_ref, acc_ref = refs

        @pl.when(pl.program_id(0) == 0)
        def _():
            acc_ref[...] = jnp.zeros_like(acc_ref)

        xv, gv, dhv = x_ref[...], g_ref[...], dh_ref[...]
        r = _rstd(xv)
        xh = xv * r
        acc_ref[0:1, :] += _colsum(dhv * (xh * gv))
        acc_ref[1:2, :] += _colsum(dhv)
        dn = dhv * (1.0 + sc_ref[...])
        acc_ref[2:3, :] += _colsum(dn * xh)
        dxh = dn * gv
        dx = dxn_ref[...] + r * (dxh - xh * jnp.mean(dxh * xh, axis=-1, keepdims=True))
        dx_ref[...] = dx
        if has_prev:
            acc_ref[3:4, :] += _colsum(dx * o_ref[...])
            do_ref[...] = (dx * gate_ref[...]).astype(BF16)

    ins = [dh, x, g, scale, dx_next] + ([o_prev, gate_prev] if has_prev else [])
    in_specs = [row, row, vec, vec, row] + ([row, vec] if has_prev else [])
    out_shape = [jax.ShapeDtypeStruct((s, d), F32)]
    out_specs = [row]
    if has_prev:
        out_shape.append(jax.ShapeDtypeStruct((s, d), BF16))
        out_specs.append(row)
    out_shape.append(jax.ShapeDtypeStruct((8, d), F32))
    out_specs.append(acc)
    body, extra_specs, extra = _ordered(body, len(ins), after)
    return pl.pallas_call(
        body, name=name, grid=(s // br,), in_specs=in_specs + extra_specs, out_specs=out_specs,
        out_shape=out_shape, compiler_params=_params(("arbitrary",)),
    )(*ins, *extra)


def _shift_down(p, k):
    if k == 0:
        return p
    rows = lax.broadcasted_iota(jnp.int32, p.shape, 0)
    return jnp.where(rows >= k, pltpu.roll(p, k, 0), 0.0)


def _shift_up(p, k):
    if k == 0:
        return p
    s = p.shape[0]
    rows = lax.broadcasted_iota(jnp.int32, p.shape, 0)
    return jnp.where(rows < s - k, pltpu.roll(p, s - k, 0), 0.0)


def _sigmoid(z):
    return 1.0 / (1.0 + jnp.exp(-z))


def _sc_parts(proj_ref, w_ref):
    b, cg, v, g = (proj_ref[i].astype(F32) for i in range(4))
    p = cg * v
    u = w_ref[2:3, :] * p + w_ref[1:2, :] * _shift_down(p, 1) + w_ref[0:1, :] * _shift_down(p, 2)
    return b, cg, v, g, p, u


def _sc_fwd(proj, conv_w, name):
    _, s, e = proj.shape
    bc = _blk(e, 256)

    def body(proj_ref, w_ref, y_ref):
        b, _, _, g, _, u = _sc_parts(proj_ref, w_ref)
        y_ref[...] = (b * u * (g * _sigmoid(g))).astype(BF16)

    return pl.pallas_call(
        body, name=name, grid=(e // bc,),
        in_specs=[pl.BlockSpec((4, s, bc), lambda j: (0, 0, j)), pl.BlockSpec((3, bc), lambda j: (0, j))],
        out_specs=pl.BlockSpec((s, bc), lambda j: (0, j)),
        out_shape=jax.ShapeDtypeStruct((s, e), BF16),
        compiler_params=_params(("parallel",)),
    )(proj, conv_w)


def _sc_bwd(proj, dy, conv_w, name):
    _, s, e = proj.shape
    bc = _blk(e, 256)

    def body(proj_ref, dy_ref, w_ref, dp_ref, dw_ref):
        b, cg, v, g, p, u = _sc_parts(proj_ref, w_ref)
        dyv = dy_ref[...].astype(F32)
        sig = _sigmoid(g)
        t = dyv * (g * sig)
        du = t * b
        dp_ref[0] = (t * u).astype(BF16)
        dp_ref[3] = (dyv * b * u * (sig * (1.0 + g * (1.0 - sig)))).astype(BF16)
        dpp = w_ref[2:3, :] * du + w_ref[1:2, :] * _shift_up(du, 1) + w_ref[0:1, :] * _shift_up(du, 2)
        dp_ref[1] = (dpp * v).astype(BF16)
        dp_ref[2] = (dpp * cg).astype(BF16)
        dw_ref[2:3, :] = _colsum(du * p)
        dw_ref[1:2, :] = _colsum(du * _shift_down(p, 1))
        dw_ref[0:1, :] = _colsum(du * _shift_down(p, 2))

    return pl.pallas_call(
        body, name=name, grid=(e // bc,),
        in_specs=[pl.BlockSpec((4, s, bc), lambda j: (0, 0, j)), pl.BlockSpec((s, bc), lambda j: (0, j)),
                  pl.BlockSpec((3, bc), lambda j: (0, j))],
        out_specs=[pl.BlockSpec((4, s, bc), lambda j: (0, 0, j)), pl.BlockSpec((3, bc), lambda j: (0, j))],
        out_shape=[jax.ShapeDtypeStruct((4, s, e), BF16), jax.ShapeDtypeStruct((3, e), F32)],
        compiler_params=_params(("parallel",)),
    )(proj, dy, conv_w)


def _softplus_neg(lam):
    u = jnp.exp(-jnp.abs(lam))
    w = 1.0 + u
    log1p = jnp.where(w == 1.0, u, jnp.log(w) * (u / jnp.where(w == 1.0, 1.0, w - 1.0)))
    return jnp.maximum(-lam, 0.0) + log1p


def _one_minus_exp(z):
    series = -z * (1.0 + z * (0.5 + z * (1.0 / 6.0 + z * (1.0 / 24.0))))
    return jnp.where(z > -0.02, series, 1.0 - jnp.exp(z))


def _scan_in_tiles(a, b, reverse):
    s = a.shape[0]
    rows = lax.broadcasted_iota(jnp.int32, a.shape, 0) & 7
    for step in (1, 2, 4):
        if reverse:
            ok = rows < 8 - step
            a_s, b_s = pltpu.roll(a, s - step, 0), pltpu.roll(b, s - step, 0)
        else:
            ok = rows >= step
            a_s, b_s = pltpu.roll(a, step, 0), pltpu.roll(b, step, 0)
        b = jnp.where(ok, a * b_s + b, b)
        a = jnp.where(ok, a * a_s, a)
    return a, b


def _scan(a, b, a_ref, b_ref, h_ref, reverse):
    s, c = a.shape
    n = s // 8
    a_t, b_t = _scan_in_tiles(a, b, reverse)
    a_ref[...] = a_t
    b_ref[...] = b_t

    def step(i, carry):
        gi = n - 1 - i if reverse else i
        sl = pl.ds(pl.multiple_of(gi * 8, 8), 8)
        h = b_ref[sl, :] + a_ref[sl, :] * carry
        h_ref[sl, :] = h
        return h[0:1, :] if reverse else h[7:8, :]

    lax.fori_loop(0, n, step, jnp.zeros((1, c), F32))
    return h_ref[...]


def _lru_specs(s, e_half, n_heads):
    hp = e_half // HEAD_DIM
    c = HEAD_DIM
    return dict(
        pair=pl.BlockSpec((2, None, s, c), lambda h: (0, h // hp, 0, h % hp)),
        conv_w=pl.BlockSpec((4, c), lambda h: (0, h)),
        chan=pl.BlockSpec((1, c), lambda h: (0, h)),
        w=pl.BlockSpec((4, None, c // 4, c), lambda h: (0, h, 0, 0)),
        bias=pl.BlockSpec((None, 1, c), lambda h: (h, 0, 0)),
        plane=pl.BlockSpec((s, c), lambda h: (0, h)),
    )


def _lru_forward_parts(vp, cw_ref, cb_ref, wa_ref, ba_ref, wx_ref, bx_ref, lam_ref):
    c = HEAD_DIM
    v = cb_ref[...] + sum(cw_ref[k:k + 1, :] * _shift_down(vp, 3 - k) for k in range(4))
    vb = v.astype(BF16)
    wa = wa_ref[...].reshape(c, c)
    wx = wx_ref[...].reshape(c, c)
    r = _sigmoid(jnp.dot(vb, wa, preferred_element_type=F32) + ba_ref[...])
    i = _sigmoid(jnp.dot(vb, wx, preferred_element_type=F32) + bx_ref[...])
    sp = _softplus_neg(lam_ref[...])
    la = (-RGLRU_C) * sp * r
    a = jnp.exp(la)
    nm = jnp.sqrt(_one_minus_exp(2.0 * la))
    return v, vb, wa, wx, r, i, sp, a, nm


def _lru_fwd(proj, conv_w, conv_b, w_a, b_a, w_x, b_x, lam, name):
    _, _, s, e_half = proj.shape
    n_heads = 2 * e_half // HEAD_DIM
    sp_ = _lru_specs(s, e_half, n_heads)

    def body(pg_ref, cw_ref, cb_ref, wa_ref, ba_ref, wx_ref, bx_ref, lam_ref, y_ref,
             sa_ref, sb_ref, sh_ref):
        v, _, _, _, _, i, _, a, nm = _lru_forward_parts(
            pg_ref[0].astype(F32), cw_ref, cb_ref, wa_ref, ba_ref, wx_ref, bx_ref, lam_ref)
        hs = _scan(a, nm * (i * v), sa_ref, sb_ref, sh_ref, reverse=False)
        g = pg_ref[1].astype(F32)
        y_ref[...] = (hs * (g * _sigmoid(g))).astype(BF16)

    return pl.pallas_call(
        body, name=name, grid=(n_heads,),
        in_specs=[sp_["pair"], sp_["conv_w"], sp_["chan"], sp_["w"], sp_["bias"], sp_["w"],
                  sp_["bias"], sp_["chan"]],
        out_specs=sp_["plane"],
        out_shape=jax.ShapeDtypeStruct((s, 2 * e_half), BF16),
        scratch_shapes=[pltpu.VMEM((s, HEAD_DIM), F32)] * 3,
        compiler_params=_params(("parallel",)),
    )(proj, conv_w, conv_b, w_a, b_a, w_x, b_x, lam)


def _lru_bwd(proj, dy, conv_w, conv_b, w_a, b_a, w_x, b_x, lam, name):
    _, _, s, e_half = proj.shape
    e = 2 * e_half
    c = HEAD_DIM
    n_heads = e // c
    hp = e_half // c
    sp_ = _lru_specs(s, e_half, n_heads)

    def body(pg_ref, dy_ref, cw_ref, cb_ref, wa_ref, ba_ref, wx_ref, bx_ref, lam_ref,
             dpg_ref, dwa_ref, dwx_ref, dba_ref, dbx_ref, dlam_ref, dcw_ref, dcb_ref,
             sa_ref, sb_ref, sh_ref, sd_ref):
        vp = pg_ref[0].astype(F32)
        v, vb, wa, wx, r, i, sp, a, nm = _lru_forward_parts(
            vp, cw_ref, cb_ref, wa_ref, ba_ref, wx_ref, bx_ref, lam_ref)
        iv = i * v
        hs = _scan(a, nm * iv, sa_ref, sb_ref, sh_ref, reverse=False)
        g = pg_ref[1].astype(F32)
        dyv = dy_ref[...].astype(F32)
        sig = _sigmoid(g)
        dpg_ref[1] = (dyv * hs * (sig * (1.0 + g * (1.0 - sig)))).astype(BF16)
        dh = _scan(_shift_up(a, 1), dyv * (g * sig), sa_ref, sb_ref, sd_ref, reverse=True)
        da = dh * _shift_down(hs, 1)
        div = dh * nm
        a2 = a * a
        dla = da * a - (dh * iv) * (a2 / nm)
        dzr = (dla * ((-RGLRU_C) * sp)) * (r * (1.0 - r))
        dzi = (div * v) * (i * (1.0 - i))
        lam = lam_ref[...]
        dlam_ref[...] = _colsum(dla * ((-RGLRU_C) * r)) * (-_sigmoid(-lam))
        dba_ref[...] = _colsum(dzr)
        dbx_ref[...] = _colsum(dzi)
        dzr_b, dzi_b = dzr.astype(BF16), dzi.astype(BF16)
        tn = (((0,), (0,)), ((), ()))
        nt = (((1,), (1,)), ((), ()))
        dwa_ref[...] = lax.dot_general(vb, dzr_b, tn, preferred_element_type=F32)
        dwx_ref[...] = lax.dot_general(vb, dzi_b, tn, preferred_element_type=F32)
        dv = (div * i + lax.dot_general(dzr_b, wa, nt, preferred_element_type=F32)
              + lax.dot_general(dzi_b, wx, nt, preferred_element_type=F32))
        dcb_ref[...] = _colsum(dv)
        dvp = jnp.zeros_like(dv)
        for k in range(4):
            dvp = dvp + cw_ref[k:k + 1, :] * _shift_up(dv, 3 - k)
            dcw_ref[k:k + 1, :] = _colsum(dv * _shift_down(vp, 3 - k))
        dpg_ref[0] = dvp.astype(BF16)

    head_mat = pl.BlockSpec((None, c, c), lambda h: (h, 0, 0))
    outs = pl.pallas_call(
        body, name=name, grid=(n_heads,),
        in_specs=[sp_["pair"], sp_["plane"], sp_["conv_w"], sp_["chan"], sp_["w"], sp_["bias"],
                  sp_["w"], sp_["bias"], sp_["chan"]],
        out_specs=[sp_["pair"], head_mat, head_mat, sp_["bias"], sp_["bias"],
                   sp_["chan"], sp_["conv_w"], sp_["chan"]],
        out_shape=[jax.ShapeDtypeStruct((2, 2, s, e_half), BF16),
                   jax.ShapeDtypeStruct((n_heads, c, c), F32), jax.ShapeDtypeStruct((n_heads, c, c), F32),
                   jax.ShapeDtypeStruct((n_heads, 1, c), F32), jax.ShapeDtypeStruct((n_heads, 1, c), F32),
                   jax.ShapeDtypeStruct((1, e), F32), jax.ShapeDtypeStruct((4, e), F32),
                   jax.ShapeDtypeStruct((1, e), F32)],
        scratch_shapes=[pltpu.VMEM((s, c), F32)] * 4,
        compiler_params=_params(("parallel",)),
    )(proj, dy, conv_w, conv_b, w_a, b_a, w_x, b_x, lam)
    return tuple(outs)


def _ada_fwd(c_all, ada_w, name):
    n_l, d, f = ada_w.shape
    bf = _blk(f, 512)

    def body(c_ref, w_ref, o_ref):
        cv = c_ref[...]
        sc = (cv * _sigmoid(cv)).astype(BF16)
        o_ref[...] = jnp.dot(sc, w_ref[...].astype(BF16), preferred_element_type=F32)

    return pl.pallas_call(
        body, name=name, grid=(n_l, f // bf),
        in_specs=[pl.BlockSpec((8, d), lambda l, j: (0, 0)), pl.BlockSpec((None, d, bf), lambda l, j: (l, 0, j))],
        out_specs=pl.BlockSpec((None, 8, bf), lambda l, j: (l, 0, j)),
        out_shape=jax.ShapeDtypeStruct((n_l, 8, f), F32),
        compiler_params=_params(("parallel", "parallel")),
    )(c_all, ada_w)


def _ada_bwd_adamw(c_t, dmod, w, m, v, name, after=None):
    n_l, d, f = w.shape
    bf = _blk(f, 256)

    def body(c_ref, dm_ref, w_ref, m_ref, v_ref, g_ref, d_ref, m2_ref, v2_ref):
        cv = c_ref[...]
        sc = cv * _sigmoid(cv)
        dm = dm_ref[...]
        g = sc[:, 0:1] * dm[0:1, :]
        for b in range(1, 8):
            g = g + sc[:, b:b + 1] * dm[b:b + 1, :]
        g_ref[...] = g
        dl, m2, v2 = _adamw_math(w_ref[...], g, m_ref[...], v_ref[...])
        d_ref[...] = dl
        m2_ref[...] = m2
        v2_ref[...] = v2

    big = pl.BlockSpec((None, d, bf), lambda l, j: (l, 0, j))
    body, extra_specs, extra = _ordered(body, 5, after)
    return pl.pallas_call(
        body, name=name, grid=(n_l, f // bf),
        in_specs=[pl.BlockSpec((d, 8), lambda l, j: (0, 0)), pl.BlockSpec((None, 8, bf), lambda l, j: (l, 0, j)),
                  big, big, big] + extra_specs,
        out_specs=[big] * 4, out_shape=[jax.ShapeDtypeStruct((n_l, d, f), F32)] * 4,
        compiler_params=_params(("parallel", "parallel")),
    )(c_t, dmod, w, m, v, *extra)


def _pack(parts):
    padded, offs, n = [], [], 0
    for p in parts:
        p = p.reshape(-1)
        size = -(-p.shape[0] // PACK) * PACK
        offs.append(n)
        n += size
        padded.append(jnp.pad(p, (0, size - p.shape[0])) if size != p.shape[0] else p)
    return jnp.concatenate(padded), offs, n


def kernel(x, c, norm_g, ada_w, ada_b, sc_w_in, sc_conv_w, sc_w_out, lru_w_in, lru_conv_w, lru_conv_b, lru_w_a, lru_b_a, lru_w_x, lru_b_x, lru_lambda, lru_w_out, final_g, loss_target, m_norm_g, m_ada_w, m_ada_b, m_sc_w_in, m_sc_conv_w, m_sc_w_out, m_lru_w_in, m_lru_conv_w, m_lru_conv_b, m_lru_w_a, m_lru_b_a, m_lru_w_x, m_lru_b_x, m_lru_lambda, m_lru_w_out, m_final_g, v_norm_g, v_ada_w, v_ada_b, v_sc_w_in, v_sc_conv_w, v_sc_w_out, v_lru_w_in, v_lru_conv_w, v_lru_conv_b, v_lru_w_a, v_lru_b_a, v_lru_w_x, v_lru_b_x, v_lru_lambda, v_lru_w_out, v_final_g):
    xi, yi, ci = _pos()
    chip = 2 * xi + yi
    batch = 4 * xi + 2 * yi + ci
    core_op = jnp.reshape(ci, (1,)).astype(jnp.int32)
    chip_op = jnp.reshape(chip, (1,)).astype(jnp.int32)
    where_op = jnp.stack([chip, ci]).astype(jnp.int32)

    x2d, tgt = x[0], loss_target[0]
    s, d = x2d.shape
    es = sc_conv_w.shape[2]
    e = 4 * es
    n_heads = lru_w_a.shape[1]
    hj = lru_b_a.shape[2]
    f = ada_w.shape[2]
    row = lambda t: t.reshape(1, -1)

    small_parts = [c, sc_conv_w, lru_conv_w, lru_conv_b, lru_b_a, lru_b_x, lru_lambda]
    small, offs, n_small = _pack(small_parts)
    got = _allgather8([small.reshape(8, n_small // 8)], "ag_small")[0].reshape(8, n_small)
    c_all = got[:, :d]
    per_chip = got[0::2]

    def chip_part(k, shape):
        size = 1
        for dim in shape:
            size *= dim
        return per_chip[:, offs[k]:offs[k] + size].reshape((4,) + shape)

    conv_w0 = jnp.transpose(chip_part(1, (3, es)), (1, 0, 2)).reshape(3, e)
    conv_w1 = jnp.transpose(chip_part(2, (4, es)), (1, 0, 2)).reshape(4, e)
    conv_b1 = chip_part(3, (es,)).reshape(1, e)
    b_a = jnp.transpose(chip_part(4, (n_heads, hj)), (1, 0, 2)).reshape(n_heads, 1, 4 * hj)
    b_x = jnp.transpose(chip_part(5, (n_heads, hj)), (1, 0, 2)).reshape(n_heads, 1, 4 * hj)
    lam = chip_part(6, (es,)).reshape(1, e)

    mod_nb = _ada_fwd(c_all, ada_w, "ada_fwd")
    mods = _allgather8([mod_nb.reshape(16, f)], "ag_mod")[0].reshape(8, 2, 8, f)[0::2]
    mine = lax.dynamic_index_in_dim(mods, batch, axis=2, keepdims=False)
    mod = jnp.transpose(mine, (1, 0, 2)).reshape(2, 4 * f) + ada_b
    shift = [row(mod[l, :d]) for l in range(2)]
    scale = [row(mod[l, d:2 * d]) for l in range(2)]
    gate = [row(mod[l, 2 * d:]) for l in range(2)]
    ng = [row(norm_g[l]) for l in range(2)]

    h0 = _norm_mod_fwd(x2d, ng[0], scale[0], shift[0], "norm0")

    shards = [sc_w_in[0], sc_w_out[0], lru_w_in[0], lru_w_a[0].reshape(n_heads * hj, HEAD_DIM),
              lru_w_x[0].reshape(n_heads * hj, HEAD_DIM), lru_w_out[0]]
    names = ["sc_w_in", "sc_w_out", "lru_w_in", "lru_w_a", "lru_w_x", "lru_w_out"]
    slots = [_cast_into_slot(chip_op, w, "cast_" + nm) for w, nm in zip(shards, names)]
    in_flight, _ = _gather_start([[slots[0]], [slots[1]], slots[2:5], [slots[5]]], h0, "ag_start")

    def arrived(g, after, tag):
        send1, recv1, bufs = in_flight[g]
        send2, recv2, bufs, passed = _gather_forward(send1, recv1, bufs, after, "ag_forward_" + tag)
        return _gather_finish(send2, recv2, bufs, passed, "ag_finish_" + tag)

    w_in0 = arrived(0, h0, "sc_w_in")[0].reshape(4, d, e)
    proj0 = _mm(h0[None], w_in0, "nn", BF16, "sc_in")
    w_out0 = arrived(1, proj0, "sc_w_out")[0].reshape(1, e, d)
    y0 = _sc_fwd(proj0, conv_w0, "sc_mix")
    o0 = _mm(y0[None], w_out0, "nn", F32, "sc_out")[0]
    x1, h1 = _norm_mod_fwd(x2d, ng[1], scale[1], shift[1], "norm1", o=o0, gate=gate[0])
    lru_ws = arrived(2, h1, "lru_w_in")
    w_in1 = lru_ws[0].reshape(4, d, e // 2)
    w_a = lru_ws[1].reshape(4, n_heads, hj, HEAD_DIM)
    w_x = lru_ws[2].reshape(4, n_heads, hj, HEAD_DIM)
    proj1 = _mm(h1[None], w_in1, "nn", BF16, "lru_in")
    pairs1 = proj1.reshape(2, 2, s, e // 2)
    y1 = _lru_fwd(pairs1, conv_w1, conv_b1, w_a, b_a, w_x, b_x, lam, "lru_mix")
    w_out1 = arrived(3, y1, "lru_w_out")[0].reshape(1, e, d)
    o1 = _mm(y1[None], w_out1, "nn", F32, "lru_out")[0]
    dx2, do1, acc_f = _final_loss(x1, o1, gate[1], row(final_g), tgt, "final_loss")

    def reduce_stage1(tag, grads):
        lands = [lax.empty((4, g.shape[1] // 2, g.shape[2]), F32) for g in grads]
        return _exchange_start("rs_sibling_start_" + tag, grads, lands, len(grads),
                               _plan_other_half_to_sibling, grads[-1])

    def reduce_stage2(tag, stage1, nms, after):
        send, recv, grads, lands, _ = stage1
        grads, lands = _exchange_wait("rs_sibling_wait_" + tag, send, recv, grads, lands,
                                      _plan_other_half_to_sibling, after)
        parts = [_sum_own_and_sibling(core_op, g, r1, "rs_sum1_" + nm) for g, r1, nm in zip(grads, lands, nms)]
        lands = [lax.empty((3,) + p.shape[1:], BF16) for p in parts]
        return _exchange_start("rs_chips_start_" + tag, parts, lands, 3 * len(parts),
                               _plan_partials_to_chips, parts[-1])

    def reduce_stage3(tag, stage2, nms, after):
        send, recv, parts, lands, _ = stage2
        parts, lands = _exchange_wait("rs_chips_wait_" + tag, send, recv, parts, lands,
                                      _plan_partials_to_chips, after)
        halves = [_sum_chips(where_op, p, r2, "rs_sum2_" + nm) for p, r2, nm in zip(parts, lands, nms)]
        return _exchange_start("rs_share_start_" + tag, [], halves, len(halves), _plan_share_half, halves[-1])

    def reduce_done(tag, stage3, after):
        send, recv, _, fulls, _ = stage3
        return _exchange_wait("rs_share_wait_" + tag, send, recv, [], fulls, _plan_share_half, after)[1]

    def head_major_to_chip_major(t):
        return jnp.transpose(t.reshape(n_heads, 4, hj, HEAD_DIM), (1, 0, 2, 3)).reshape(4, n_heads * hj, HEAD_DIM)

    g_w_out1 = _mm(y1[None], do1[None], "tn", F32, "lru_out_dw", bm=512, bn=2048)
    dy1 = _mm(do1[None], w_out1, "nt", BF16, "lru_out_dx")[0]
    dpairs1, g_wa, g_wx, g_ba, g_bx, g_lam, g_cw1, g_cb1 = _lru_bwd(
        pairs1, dy1, conv_w1, conv_b1, w_a, b_a, w_x, b_x, lam, "lru_mix_bwd")
    dproj1 = dpairs1.reshape(4, s, e // 2)
    g_w_in1 = _mm(h1[None], dproj1, "tn", F32, "lru_in_dw", bm=512, bn=2048)
    lru_names = ["lru_w_out", "lru_w_a", "lru_w_x", "lru_w_in"]
    lru_rs = reduce_stage1("lru", [g_w_out1.reshape(4, es, d), head_major_to_chip_major(g_wa),
                                   head_major_to_chip_major(g_wx), g_w_in1])
    dh1 = _mm(dproj1, w_in1, "nt", F32, "lru_in_dx", after=lru_rs[4])[0]
    lru_rs = reduce_stage2("lru", lru_rs, lru_names, dh1)
    dx1, do0, acc1 = _norm_mod_bwd(dh1, x1, ng[1], scale[1], dx2, "norm1_bwd", o_prev=o0, gate_prev=gate[0],
                                   after=lru_rs[4])

    g_w_out0 = _mm(y0[None], do0[None], "tn", F32, "sc_out_dw", bm=512, bn=2048)
    dy0 = _mm(do0[None], w_out0, "nt", BF16, "sc_out_dx")[0]
    dproj0, g_cw0 = _sc_bwd(proj0, dy0, conv_w0, "sc_mix_bwd")
    g_w_in0 = _mm(h0[None], dproj0, "tn", F32, "sc_in_dw", bm=512, bn=2048)
    lru_rs = reduce_stage3("lru", lru_rs, lru_names, g_w_in0)
    sc_names = ["sc_w_out", "sc_w_in"]
    sc_rs = reduce_stage1("sc", [g_w_out0.reshape(4, es, d), g_w_in0])
    dh0 = _mm(dproj0, w_in0, "nt", F32, "sc_in_dx", after=sc_rs[4])[0]
    grad_x, acc0 = _norm_mod_bwd(dh0, x2d, ng[0], scale[0], dx1, "norm0_bwd")

    dmod = jnp.stack([jnp.concatenate([acc0[1], acc0[0], acc1[3]]),
                      jnp.concatenate([acc1[1], acc1[0], acc_f[1]])])
    part_list = [jnp.stack([acc0[2], acc1[2]]), acc_f[0], acc_f[2, :1], g_cw0, g_cw1, g_cb1, g_ba, g_bx,
                 g_lam, dmod]
    partials, poffs, n_part = _pack(part_list)
    every = _allgather8([partials.reshape(8, n_part // 8)], "ag_partials")[0].reshape(8, n_part)
    sc_rs = reduce_stage2("sc", sc_rs, sc_names, every)
    total = _sum_rows8(every, "sum_partials", after=sc_rs[4])[0]

    def tot(k, shape):
        size = 1
        for dim in shape:
            size *= dim
        return total[poffs[k]:poffs[k] + size].reshape(shape)

    def my_cols(t, width):
        return lax.dynamic_slice_in_dim(t, chip * width, width, axis=t.ndim - 1)

    loss = tot(2, (1,))[0]
    g_norm_g, g_final_g, g_ada_b = tot(0, (2, d)), tot(1, (d,)), tot(9, (2, 3 * d))
    g_sc_conv_w = my_cols(tot(3, (3, e)), es)[None]
    g_lru_conv_w = my_cols(tot(4, (4, e)), es)[None]
    g_lru_conv_b = my_cols(tot(5, (1, e)), es)
    g_lru_b_a = my_cols(tot(6, (n_heads, 4 * hj)), hj)[None]
    g_lru_b_x = my_cols(tot(7, (n_heads, 4 * hj)), hj)[None]
    g_lru_lambda = my_cols(tot(8, (1, e)), es)

    dmod_all = every[:, poffs[9]:poffs[9] + 6 * d].reshape(8, 2, 3 * d)
    dmod_mine = jnp.transpose(my_cols(dmod_all, f), (1, 0, 2))
    ada = _ada_bwd_adamw(jnp.transpose(c_all), dmod_mine, ada_w, m_ada_w, v_ada_w, "ada_bwd_adamw", after=total)

    big_state = dict(zip(names, zip(shards, [m_sc_w_in, m_sc_w_out, m_lru_w_in, m_lru_w_a, m_lru_w_x, m_lru_w_out],
                                    [v_sc_w_in, v_sc_w_out, v_lru_w_in, v_lru_w_a, v_lru_w_x, v_lru_w_out],
                                    [sc_w_in, sc_w_out, lru_w_in, lru_w_a, lru_w_x, lru_w_out])))
    big = {}

    def update(nms, fulls, after):
        for nm, g2 in zip(nms, fulls):
            w2, m4, v4, w4 = big_state[nm]
            dl, m2, v2 = _adamw(w2, g2, m4.reshape(w2.shape), v4.reshape(w2.shape), "adamw_" + nm, after=after)
            big[nm] = tuple(t.reshape(w4.shape) for t in (g2, dl, m2, v2))
            after = dl
        return after

    last = update(lru_names, reduce_done("lru", lru_rs, sc_rs[4]), ada[0])

    small_names = ["norm_g", "ada_b", "final_g", "sc_conv_w", "lru_conv_w", "lru_conv_b", "lru_b_a",
                   "lru_b_x", "lru_lambda"]
    small_w = [norm_g, ada_b, final_g, sc_conv_w, lru_conv_w, lru_conv_b, lru_b_a, lru_b_x, lru_lambda]
    small_g = [g_norm_g, g_ada_b, g_final_g, g_sc_conv_w, g_lru_conv_w, g_lru_conv_b, g_lru_b_a,
               g_lru_b_x, g_lru_lambda]
    small_m = [m_norm_g, m_ada_b, m_final_g, m_sc_conv_w, m_lru_conv_w, m_lru_conv_b, m_lru_b_a,
               m_lru_b_x, m_lru_lambda]
    small_v = [v_norm_g, v_ada_b, v_final_g, v_sc_conv_w, v_lru_conv_w, v_lru_conv_b, v_lru_b_a,
               v_lru_b_x, v_lru_lambda]
    pw, soffs, n_s = _pack(small_w)
    pg, pm, pv = _pack(small_g)[0], _pack(small_m)[0], _pack(small_v)[0]
    shape2 = (n_s // PACK, PACK)
    pd, pm2, pv2 = _adamw(pw.reshape(shape2), pg.reshape(shape2), pm.reshape(shape2), pv.reshape(shape2),
                          "adamw_small", after=last)
    sc_rs = reduce_stage3("sc", sc_rs, sc_names, pd)
    update(sc_names, reduce_done("sc", sc_rs, sc_rs[4]), None)
    small = {}
    for k, (nm, w_) in enumerate(zip(small_names, small_w)):
        take = lambda t: t.reshape(-1)[soffs[k]:soffs[k] + w_.size].reshape(w_.shape)
        small[nm] = (small_g[k].reshape(w_.shape), take(pd), take(pm2), take(pv2))

    results = dict(small)
    results.update(big)
    results["ada_w"] = tuple(ada)
    order = ["norm_g", "ada_w", "ada_b", "sc_w_in", "sc_conv_w", "sc_w_out", "lru_w_in", "lru_conv_w",
             "lru_conv_b", "lru_w_a", "lru_b_a", "lru_w_x", "lru_b_x", "lru_lambda", "lru_w_out", "final_g"]
    out = [loss, grad_x[None]]
    for kind in range(4):
        out += [results[nm][kind] for nm in order]
    return tuple(out)
```

```python
import functools

import jax
import jax.numpy as jnp
from jax import lax
from jax.experimental import pallas as pl
from jax.experimental.pallas import tpu as pltpu

F32 = jnp.float32
BF16 = jnp.bfloat16
MESH = pl.DeviceIdType.MESH
ANY = pl.BlockSpec(memory_space=pl.ANY)

RMS_EPS = 1e-6
RGLRU_C = 8.0
HEAD_DIM = 256
ADAM_LR = 0.001
ADAM_B1 = 0.9
ADAM_B2 = 0.999
ADAM_EPS = 1e-08
ADAM_WD = 0.01
ADAM_STEP = 10
V7X_VMEM_LIMIT = 56 * 1024 * 1024
LANES = 128
PACK = 8 * LANES


def _blk(dim, pref, unit=LANES):
    if dim <= pref:
        return dim
    b = (pref // unit) * unit
    while b > unit and dim % b:
        b -= unit
    assert dim % b == 0, (dim, pref, unit)
    return b


def _params(sem=None):
    return pltpu.CompilerParams(dimension_semantics=sem, vmem_limit_bytes=V7X_VMEM_LIMIT)


def _ordered(body, n_in, after):
    if after is None:
        return body, [], []

    def ordered_body(*refs):
        return body(*refs[:n_in], *refs[n_in + 1:])

    return ordered_body, [ANY], [after]


def _pos():
    return lax.axis_index("x"), lax.axis_index("y"), lax.axis_index("c")


def _other_chips(x, y):
    return [(1 - x, y), (x, 1 - y), (1 - x, 1 - y)]


def _allgather8(arrs, name, after=None):
    n_t = len(arrs)
    ms = [a.shape[0] for a in arrs]

    def gather(*refs):
        ins, outs = refs[:n_t], refs[n_t:2 * n_t]
        send_sems, recv_sems, local_sems = refs[2 * n_t:]
        x, y, c = _pos()
        me, sibling = (x, y, c), (x, y, 1 - c)
        chips = _other_chips(x, y)

        def rows(t, px, py, pc):
            return outs[t].at[pl.ds((4 * px + 2 * py + pc) * ms[t], ms[t])]

        def copy(t, k, block, to, src=None):
            return pltpu.make_async_remote_copy(
                src_ref=rows(t, *block) if src is None else src, dst_ref=rows(t, *block),
                send_sem=send_sems.at[7 * t + k], recv_sem=recv_sems.at[7 * t + k],
                device_id=to, device_id_type=MESH)

        mine, first, passed = [], [], []
        for t in range(n_t):
            src = ins[t]
            cp = pltpu.make_async_copy(src, rows(t, *me), local_sems.at[t])
            cp.start()
            mine.append(cp)
            sends = [copy(t, 0, me, sibling, src=src)]
            sends += [copy(t, 1 + j, me, (*chip, c), src=src) for j, chip in enumerate(chips)]
            for cp in sends:
                cp.start()
            first += sends
        for t in range(n_t):
            for j, chip in enumerate(chips):
                copy(t, 1 + j, (*chip, c), me).wait_recv()
                cp = copy(t, 4 + j, (*chip, c), sibling)
                cp.start()
                passed.append(cp)
        for t in range(n_t):
            copy(t, 0, sibling, me).wait_recv()
            for j, chip in enumerate(chips):
                copy(t, 4 + j, (*chip, 1 - c), me).wait_recv()
        for cp in first + passed:
            cp.wait_send()
        for cp in mine:
            cp.wait()

    body, extra_specs, extra = _ordered(gather, n_t, after)
    return pl.pallas_call(
        body, name=name,
        out_shape=[jax.ShapeDtypeStruct((8 * m, a.shape[1]), a.dtype) for m, a in zip(ms, arrs)],
        in_specs=[ANY] * n_t + extra_specs, out_specs=[ANY] * n_t,
        scratch_shapes=[pltpu.SemaphoreType.DMA((7 * n_t,)), pltpu.SemaphoreType.DMA((7 * n_t,)),
                        pltpu.SemaphoreType.DMA((n_t,))],
    )(*arrs, *extra)


HBM = pl.BlockSpec(memory_space=pltpu.HBM)
SEM = pl.BlockSpec(memory_space=pltpu.SEMAPHORE)
TOKEN = pl.BlockSpec(memory_space=pltpu.VMEM)
IN_FLIGHT = pltpu.CompilerParams(has_side_effects=pltpu.SideEffectType.DATAFLOW_SIDE_EFFECTING)


def _in_hbm(arrs):
    return [pltpu.with_memory_space_constraint(a, pltpu.HBM) for a in arrs]


def _shard_rows(ref, h, px, py, pc):
    return ref.at[pl.ds((4 * px + 2 * py + pc) * h, h)]


def _gather_start(groups, after, name):
    bufs = [b for grp in groups for b in grp]
    n_t, n_g = len(bufs), len(groups)

    def body(*refs):
        sems, thru = refs[n_t + 1:n_t + 1 + 2 * n_g], refs[n_t + 1 + 2 * n_g:2 * n_t + 1 + 2 * n_g]
        token = refs[-1]
        x, y, c = _pos()
        t = 0
        for g, grp in enumerate(groups):
            for i in range(len(grp)):
                h = bufs[t].shape[0] // 8
                rows = _shard_rows(thru[t], h, x, y, c)
                for j, chip in enumerate(_other_chips(x, y)):
                    pltpu.make_async_remote_copy(
                        src_ref=rows, dst_ref=rows, send_sem=sems[2 * g].at[3 * i + j],
                        recv_sem=sems[2 * g + 1].at[3 * i + j], device_id=(*chip, c),
                        device_id_type=MESH).start()
                t += 1
        token[...] = jnp.zeros_like(token)

    sem_shapes = []
    for grp in groups:
        sem_shapes += [pltpu.SemaphoreType.DMA((3 * len(grp),))] * 2
    out = pl.pallas_call(
        body, name=name,
        out_shape=sem_shapes + [pltpu.HBM(b.shape, b.dtype) for b in bufs] + [jax.ShapeDtypeStruct((8, LANES), F32)],
        in_specs=[HBM] * n_t + [ANY], out_specs=[SEM] * (2 * n_g) + [HBM] * n_t + [TOKEN],
        input_output_aliases={t: 2 * n_g + t for t in range(n_t)},
        compiler_params=IN_FLIGHT,
    )(*_in_hbm(bufs), after)
    sems, thru, token = out[:2 * n_g], out[2 * n_g:2 * n_g + n_t], out[-1]
    per_group, t = [], 0
    for g, grp in enumerate(groups):
        per_group.append((sems[2 * g], sems[2 * g + 1], thru[t:t + len(grp)]))
        t += len(grp)
    return per_group, token


def _gather_forward(send_sems, recv_sems, bufs, after, name):
    n_t = len(bufs)

    def body(*refs):
        ins = refs[:n_t]
        send1, recv1 = refs[n_t], refs[n_t + 1]
        send2, recv2 = refs[n_t + 3], refs[n_t + 4]
        token = refs[-1]
        x, y, c = _pos()
        for t in range(n_t):
            h = bufs[t].shape[0] // 8
            mine = _shard_rows(ins[t], h, x, y, c)
            for j, chip in enumerate(_other_chips(x, y)):
                landed = _shard_rows(ins[t], h, *chip, c)
                pltpu.make_async_remote_copy(
                    src_ref=mine, dst_ref=landed, send_sem=send1.at[3 * t + j], recv_sem=recv1.at[3 * t + j],
                    device_id=(*chip, c), device_id_type=MESH).wait_recv()
                pltpu.make_async_remote_copy(
                    src_ref=landed, dst_ref=landed, send_sem=send2.at[3 * t + j], recv_sem=recv2.at[3 * t + j],
                    device_id=(x, y, 1 - c), device_id_type=MESH).start()
        for t in range(n_t):
            h = bufs[t].shape[0] // 8
            mine = _shard_rows(ins[t], h, x, y, c)
            for j, chip in enumerate(_other_chips(x, y)):
                pltpu.make_async_remote_copy(
                    src_ref=mine, dst_ref=mine, send_sem=send1.at[3 * t + j], recv_sem=recv1.at[3 * t + j],
                    device_id=(*chip, c), device_id_type=MESH).wait_send()
        token[...] = jnp.zeros_like(token)

    out = pl.pallas_call(
        body, name=name,
        out_shape=[pltpu.SemaphoreType.DMA((3 * n_t,))] * 2 + [pltpu.HBM(b.shape, b.dtype) for b in bufs]
        + [jax.ShapeDtypeStruct((8, LANES), F32)],
        in_specs=[HBM] * n_t + [SEM, SEM, ANY], out_specs=[SEM, SEM] + [HBM] * n_t + [TOKEN],
        input_output_aliases={t: 2 + t for t in range(n_t)},
        compiler_params=IN_FLIGHT,
    )(*bufs, send_sems, recv_sems, after)
    return out[0], out[1], out[2:2 + n_t], out[-1]


def _gather_finish(send_sems, recv_sems, bufs, after, name):
    n_t = len(bufs)

    def body(*refs):
        ins = refs[:n_t]
        send2, recv2 = refs[n_t], refs[n_t + 1]
        x, y, c = _pos()
        for t in range(n_t):
            h = bufs[t].shape[0] // 8
            for j, chip in enumerate(_other_chips(x, y)):
                sent = _shard_rows(ins[t], h, *chip, c)
                got = _shard_rows(ins[t], h, *chip, 1 - c)
                cp = pltpu.make_async_remote_copy(
                    src_ref=sent, dst_ref=got, send_sem=send2.at[3 * t + j], recv_sem=recv2.at[3 * t + j],
                    device_id=(x, y, 1 - c), device_id_type=MESH)
                cp.wait_send()
                cp.wait_recv()

    return pl.pallas_call(
        body, name=name, out_shape=[pltpu.HBM(b.shape, b.dtype) for b in bufs],
        in_specs=[HBM] * n_t + [SEM, SEM, ANY], out_specs=[HBM] * n_t,
        input_output_aliases={t: t for t in range(n_t)},
        compiler_params=IN_FLIGHT,
    )(*bufs, send_sems, recv_sems, after)


def _exchange_start(name, srcs, lands, n_copies, plan, after):
    ns, nl = len(srcs), len(lands)
    extra = [] if after is None else [after]

    def body(*refs):
        base = ns + nl + len(extra)
        send_sems, recv_sems = refs[base], refs[base + 1]
        src_refs, land_refs = refs[base + 2:base + 2 + ns], refs[base + 2 + ns:base + 2 + ns + nl]
        token = refs[-1]
        x, y, c = _pos()
        copies = plan(src_refs, land_refs, x, y, c)
        assert len(copies) == n_copies
        for k, (src, dst, dev) in enumerate(copies):
            pltpu.make_async_remote_copy(
                src_ref=src, dst_ref=dst, send_sem=send_sems.at[k], recv_sem=recv_sems.at[k],
                device_id=dev, device_id_type=MESH).start()
        token[...] = jnp.zeros_like(token)

    out = pl.pallas_call(
        body, name=name,
        out_shape=[pltpu.SemaphoreType.DMA((n_copies,))] * 2
        + [pltpu.HBM(a.shape, a.dtype) for a in list(srcs) + list(lands)] + [jax.ShapeDtypeStruct((8, LANES), F32)],
        in_specs=[HBM] * (ns + nl) + [ANY] * len(extra), out_specs=[SEM, SEM] + [HBM] * (ns + nl) + [TOKEN],
        input_output_aliases={i: 2 + i for i in range(ns + nl)},
        compiler_params=IN_FLIGHT,
    )(*_in_hbm(list(srcs) + list(lands)), *extra)
    return out[0], out[1], out[2:2 + ns], out[2 + ns:2 + ns + nl], out[-1]


def _exchange_wait(name, send_sems, recv_sems, srcs, lands, plan, after):
    ns, nl = len(srcs), len(lands)

    def body(*refs):
        src_refs, land_refs = refs[:ns], refs[ns:ns + nl]
        send, recv = refs[ns + nl], refs[ns + nl + 1]
        x, y, c = _pos()
        for k, (src, dst, dev) in enumerate(plan(src_refs, land_refs, x, y, c)):
            cp = pltpu.make_async_remote_copy(
                src_ref=src, dst_ref=dst, send_sem=send.at[k], recv_sem=recv.at[k],
                device_id=dev, device_id_type=MESH)
            cp.wait_send()
            cp.wait_recv()

    out = pl.pallas_call(
        body, name=name, out_shape=[pltpu.HBM(a.shape, a.dtype) for a in list(srcs) + list(lands)],
        in_specs=[HBM] * (ns + nl) + [SEM, SEM, ANY], out_specs=[HBM] * (ns + nl),
        input_output_aliases={i: i for i in range(ns + nl)},
        compiler_params=IN_FLIGHT,
    )(*srcs, *lands, send_sems, recv_sems, after)
    return out[:ns], out[ns:]


def _plan_other_half_to_sibling(src_refs, land_refs, x, y, c):
    out = []
    for g_ref, r_ref in zip(src_refs, land_refs):
        h = g_ref.shape[1] // 2
        out.append((g_ref.at[:, pl.ds((1 - c) * h, h), :], r_ref, (x, y, 1 - c)))
    return out


def _plan_partials_to_chips(src_refs, land_refs, x, y, c):
    out = []
    for p_ref, r_ref in zip(src_refs, land_refs):
        for j, (px, py) in enumerate(_other_chips(x, y)):
            out.append((p_ref.at[2 * px + py], r_ref.at[j], (px, py, c)))
    return out


def _plan_share_half(src_refs, land_refs, x, y, c):
    out = []
    for f_ref in land_refs:
        h = f_ref.shape[0] // 2
        rows = f_ref.at[pl.ds(c * h, h)]
        out.append((rows, rows, (x, y, 1 - c)))
    return out


def _cast_into_slot(chip, w, name):
    r, c = w.shape
    br, bc = _blk(r, 512, 8), _blk(c, 2048)
    nb = r // br

    def body(chip_ref, w_ref, o_ref):
        o_ref[...] = w_ref[...].astype(BF16)

    grid_spec = pltpu.PrefetchScalarGridSpec(
        num_scalar_prefetch=1, grid=(nb, c // bc),
        in_specs=[pl.BlockSpec((br, bc), lambda i, j, chip_ref: (i, j))],
        out_specs=pl.BlockSpec((br, bc), lambda i, j, chip_ref: (chip_ref[0] * nb + i, j)))
    return pl.pallas_call(
        body, name=name, grid_spec=grid_spec,
        out_shape=jax.ShapeDtypeStruct((4 * r, c), BF16),
        compiler_params=_params(("parallel", "parallel")),
    )(chip, w)


def _sum_own_and_sibling(core, g, r1, name):
    _, r, c = g.shape
    h = r // 2
    br, bc = _blk(h, 256, 8), _blk(c, 2048)
    nb = h // br

    def body(core_ref, g_ref, r_ref, o_ref):
        o_ref[...] = (g_ref[...] + r_ref[...]).astype(BF16)

    grid_spec = pltpu.PrefetchScalarGridSpec(
        num_scalar_prefetch=1, grid=(4, nb, c // bc),
        in_specs=[pl.BlockSpec((None, br, bc), lambda k, i, j, core_ref: (k, core_ref[0] * nb + i, j)),
                  pl.BlockSpec((None, br, bc), lambda k, i, j, core_ref: (k, i, j))],
        out_specs=pl.BlockSpec((None, br, bc), lambda k, i, j, core_ref: (k, i, j)))
    return pl.pallas_call(
        body, name=name, grid_spec=grid_spec,
        out_shape=jax.ShapeDtypeStruct((4, h, c), BF16),
        compiler_params=_params(("parallel", "parallel", "parallel")),
    )(core, g, r1)


def _sum_chips(where, p, r2, name):
    _, h, c = p.shape
    br, bc = _blk(h, 256, 8), _blk(c, 2048)
    nb = h // br

    def body(where_ref, p_ref, r_ref, o_ref):
        acc = p_ref[...].astype(F32)
        for j in range(3):
            acc = acc + r_ref[j].astype(F32)
        o_ref[...] = acc

    grid_spec = pltpu.PrefetchScalarGridSpec(
        num_scalar_prefetch=1, grid=(nb, c // bc),
        in_specs=[pl.BlockSpec((None, br, bc), lambda i, j, where_ref: (where_ref[0], i, j)),
                  pl.BlockSpec((3, br, bc), lambda i, j, where_ref: (0, i, j))],
        out_specs=pl.BlockSpec((br, bc), lambda i, j, where_ref: (where_ref[1] * nb + i, j)))
    return pl.pallas_call(
        body, name=name, grid_spec=grid_spec,
        out_shape=jax.ShapeDtypeStruct((2 * h, c), F32),
        compiler_params=_params(("parallel", "parallel")),
    )(where, p, r2)


def _adamw_math(w, g, m, v):
    m2 = ADAM_B1 * m + (1.0 - ADAM_B1) * g
    v2 = ADAM_B2 * v + (1.0 - ADAM_B2) * (g * g)
    m_hat = m2 / (1.0 - ADAM_B1 ** ADAM_STEP)
    v_hat = v2 / (1.0 - ADAM_B2 ** ADAM_STEP)
    delta = -ADAM_LR * (m_hat / (jnp.sqrt(v_hat) + ADAM_EPS) + ADAM_WD * w)
    return delta, m2, v2


def _adamw(w, g, m, v, name, after=None):
    r, c = w.shape
    br, bc = _blk(r, 128, 8), _blk(c, 2048)

    def body(w_ref, g_ref, m_ref, v_ref, d_ref, m2_ref, v2_ref):
        d, m2, v2 = _adamw_math(w_ref[...], g_ref[...], m_ref[...], v_ref[...])
        d_ref[...] = d
        m2_ref[...] = m2
        v2_ref[...] = v2

    spec = pl.BlockSpec((br, bc), lambda i, j: (i, j))
    body, extra_specs, extra = _ordered(body, 4, after)
    return pl.pallas_call(
        body, name=name, grid=(r // br, c // bc),
        in_specs=[spec] * 4 + extra_specs, out_specs=[spec] * 3,
        out_shape=[jax.ShapeDtypeStruct((r, c), F32)] * 3,
        compiler_params=_params(("parallel", "parallel")),
    )(w, g, m, v, *extra)


def _sum_rows8(g, name, after=None):
    n = g.shape[1]

    def body(g_ref, o_ref):
        acc = g_ref[0:1, :]
        for k in range(1, 8):
            acc = acc + g_ref[k:k + 1, :]
        o_ref[...] = acc

    body, extra_specs, extra = _ordered(body, 1, after)
    return pl.pallas_call(
        body, name=name, out_shape=jax.ShapeDtypeStruct((1, n), F32),
        in_specs=[pl.BlockSpec(memory_space=pltpu.VMEM)] + extra_specs,
        out_specs=pl.BlockSpec(memory_space=pltpu.VMEM),
        compiler_params=_params(),
    )(g, *extra)


def _mm(a, b, mode, out_dtype, name, bm=1024, bn=None, after=None):
    if mode == "nn":
        (_, m, k), (g, _, n) = a.shape, b.shape
    elif mode == "tn":
        (_, k, m), (g, _, n) = a.shape, b.shape
    else:
        (g, m, k), (_, n, _) = a.shape, b.shape
    if bn is None:
        bn = 1024 if k <= 2048 else 512
    bm, bn = _blk(m, bm), _blk(n, bn)

    if mode == "nt":
        def body(a_ref, b_ref, o_ref, acc_ref):
            part = lax.dot_general(a_ref[...], b_ref[...], (((1,), (1,)), ((), ())),
                                   preferred_element_type=F32)
            if g == 1:
                o_ref[...] = part.astype(out_dtype)
            else:
                gi = pl.program_id(2)

                @pl.when(gi == 0)
                def _():
                    acc_ref[...] = part

                @pl.when(gi > 0)
                def _():
                    acc_ref[...] += part

                @pl.when(gi == g - 1)
                def _():
                    o_ref[...] = acc_ref[...].astype(out_dtype)

        body, extra_specs, extra = _ordered(body, 2, after)
        return pl.pallas_call(
            body, name=name, grid=(m // bm, n // bn, g),
            in_specs=[pl.BlockSpec((None, bm, k), lambda i, j, gi: (gi, i, 0)),
                      pl.BlockSpec((None, bn, k), lambda i, j, gi: (gi, j, 0))] + extra_specs,
            out_specs=pl.BlockSpec((None, bm, bn), lambda i, j, gi: (0, i, j)),
            out_shape=jax.ShapeDtypeStruct((1, m, n), out_dtype),
            scratch_shapes=[pltpu.VMEM((bm, bn), F32)],
            compiler_params=_params(("parallel", "parallel", "arbitrary")),
        )(a, b, *extra)

    contract = (((1,), (0,)), ((), ())) if mode == "nn" else (((0,), (0,)), ((), ()))

    def body(a_ref, b_ref, o_ref):
        o_ref[...] = lax.dot_general(a_ref[...], b_ref[...], contract,
                                     preferred_element_type=F32).astype(out_dtype)

    a_spec = (pl.BlockSpec((None, bm, k), lambda i, gi, j: (0, i, 0)) if mode == "nn"
              else pl.BlockSpec((None, k, bm), lambda i, gi, j: (0, 0, i)))
    body, extra_specs, extra = _ordered(body, 2, after)
    return pl.pallas_call(
        body, name=name, grid=(m // bm, g, n // bn),
        in_specs=[a_spec, pl.BlockSpec((None, k, bn), lambda i, gi, j: (gi, 0, j))] + extra_specs,
        out_specs=pl.BlockSpec((None, bm, bn), lambda i, gi, j: (gi, i, j)),
        out_shape=jax.ShapeDtypeStruct((g, m, n), out_dtype),
        compiler_params=_params(("parallel", "parallel", "parallel")),
    )(a, b, *extra)


def _row_specs(br, d):
    return (pl.BlockSpec((br, d), lambda i: (i, 0)), pl.BlockSpec((1, d), lambda i: (0, 0)),
            pl.BlockSpec((8, d), lambda i: (0, 0)))


def _rstd(xv):
    return lax.rsqrt(jnp.mean(xv * xv, axis=-1, keepdims=True) + RMS_EPS)


def _colsum(v):
    return jnp.sum(v, axis=0, keepdims=True)


def _norm_mod_fwd(x, g, scale, shift, name, o=None, gate=None, after=None):
    s, d = x.shape
    br = _blk(s, 256, 8)
    has_res = o is not None
    row, vec, _ = _row_specs(br, d)

    def body(*refs):
        if has_res:
            x_ref, o_ref, gate_ref, g_ref, sc_ref, sh_ref, x1_ref, h_ref = refs
            xv = x_ref[...] + gate_ref[...] * o_ref[...]
            x1_ref[...] = xv
        else:
            x_ref, g_ref, sc_ref, sh_ref, h_ref = refs
            xv = x_ref[...]
        n = xv * _rstd(xv) * g_ref[...]
        h_ref[...] = (n * (1.0 + sc_ref[...]) + sh_ref[...]).astype(BF16)

    ins = [x] + ([o, gate] if has_res else []) + [g, scale, shift]
    in_specs = [row] + ([row, vec] if has_res else []) + [vec] * 3
    out_shape = ([jax.ShapeDtypeStruct((s, d), F32)] if has_res else []) + [jax.ShapeDtypeStruct((s, d), BF16)]
    body, extra_specs, extra = _ordered(body, len(ins), after)
    out = pl.pallas_call(
        body, name=name, grid=(s // br,), in_specs=in_specs + extra_specs, out_specs=[row] * len(out_shape),
        out_shape=out_shape, compiler_params=_params(("parallel",)),
    )(*ins, *extra)
    return out if has_res else out[0]


def _final_loss(x1, o1, gate1, final_g, tgt, name):
    s, d = x1.shape
    br = _blk(s, 256, 8)
    row, vec, acc = _row_specs(br, d)

    def body(x1_ref, o_ref, gate_ref, g_ref, t_ref, dx_ref, do_ref, acc_ref):
        @pl.when(pl.program_id(0) == 0)
        def _():
            acc_ref[...] = jnp.zeros_like(acc_ref)

        gate, o, g = gate_ref[...], o_ref[...], g_ref[...]
        x2 = x1_ref[...] + gate * o
        r = _rstd(x2)
        xh = x2 * r
        err = xh * g - t_ref[...]
        loss = 0.5 * _colsum(jnp.mean(err * err, axis=-1, keepdims=True))
        dout = err * (1.0 / d)
        dxh = dout * g
        dx2 = r * (dxh - xh * jnp.mean(dxh * xh, axis=-1, keepdims=True))
        dx_ref[...] = dx2
        do_ref[...] = (dx2 * gate).astype(BF16)
        acc_ref[0:1, :] += _colsum(dout * xh)
        acc_ref[1:2, :] += _colsum(dx2 * o)
        acc_ref[2:3, :] += jnp.broadcast_to(loss, (1, d))

    return pl.pallas_call(
        body, name=name, grid=(s // br,),
        in_specs=[row, row, vec, vec, row], out_specs=[row, row, acc],
        out_shape=[jax.ShapeDtypeStruct((s, d), F32), jax.ShapeDtypeStruct((s, d), BF16),
                   jax.ShapeDtypeStruct((8, d), F32)],
        compiler_params=_params(("arbitrary",)),
    )(x1, o1, gate1, final_g, tgt)


def _norm_mod_bwd(dh, x, g, scale, dx_next, name, o_prev=None, gate_prev=None, after=None):
    s, d = x.shape
    br = _blk(s, 256, 8)
    has_prev = o_prev is not None
    row, vec, acc = _row_specs(br, d)

    def body(*refs):
        if has_prev:
            dh_ref, x_ref, g_ref, sc_ref, dxn_ref, o_ref, gate_ref, dx_ref, do_ref, acc_ref = refs
        else:
            dh_ref, x_ref, g_ref, sc_ref, dxn_ref, dx_ref, acc_ref = refs

        @pl.when(pl.program_id(0) == 0)
        def _():
            acc_ref[...] = jnp.zeros_like(acc_ref)

        xv, gv, dhv = x_ref[...], g_ref[...], dh_ref[...]
        r = _rstd(xv)
        xh = xv * r
        acc_ref[0:1, :] += _colsum(dhv * (xh * gv))
        acc_ref[1:2, :] += _colsum(dhv)
        dn = dhv * (1.0 + sc_ref[...])
        acc_ref[2:3, :] += _colsum(dn * xh)
        dxh = dn * gv
        dx = dxn_ref[...] + r * (dxh - xh * jnp.mean(dxh * xh, axis=-1, keepdims=True))
        dx_ref[...] = dx
        if has_prev:
            acc_ref[3:4, :] += _colsum(dx * o_ref[...])
            do_ref[...] = (dx * gate_ref[...]).astype(BF16)

    ins = [dh, x, g, scale, dx_next] + ([o_prev, gate_prev] if has_prev else [])
    in_specs = [row, row, vec, vec, row] + ([row, vec] if has_prev else [])
    out_shape = [jax.ShapeDtypeStruct((s, d), F32)]
    out_specs = [row]
    if has_prev:
        out_shape.append(jax.ShapeDtypeStruct((s, d), BF16))
        out_specs.append(row)
    out_shape.append(jax.ShapeDtypeStruct((8, d), F32))
    out_specs.append(acc)
    body, extra_specs, extra = _ordered(body, len(ins), after)
    return pl.pallas_call(
        body, name=name, grid=(s // br,), in_specs=in_specs + extra_specs, out_specs=out_specs,
        out_shape=out_shape, compiler_params=_params(("arbitrary",)),
    )(*ins, *extra)


def _shift_down(p, k):
    if k == 0:
        return p
    rows = lax.broadcasted_iota(jnp.int32, p.shape, 0)
    return jnp.where(rows >= k, pltpu.roll(p, k, 0), 0.0)


def _shift_up(p, k):
    if k == 0:
        return p
    s = p.shape[0]
    rows = lax.broadcasted_iota(jnp.int32, p.shape, 0)
    return jnp.where(rows < s - k, pltpu.roll(p, s - k, 0), 0.0)


def _sigmoid(z):
    return 1.0 / (1.0 + jnp.exp(-z))


def _sc_parts(proj_ref, w_ref):
    b, cg, v, g = (proj_ref[i].astype(F32) for i in range(4))
    p = cg * v
    u = w_ref[2:3, :] * p + w_ref[1:2, :] * _shift_down(p, 1) + w_ref[0:1, :] * _shift_down(p, 2)
    return b, cg, v, g, p, u


def _sc_fwd(proj, conv_w, name):
    _, s, e = proj.shape
    bc = _blk(e, 256)

    def body(proj_ref, w_ref, y_ref):
        b, _, _, g, _, u = _sc_parts(proj_ref, w_ref)
        y_ref[...] = (b * u * (g * _sigmoid(g))).astype(BF16)

    return pl.pallas_call(
        body, name=name, grid=(e // bc,),
        in_specs=[pl.BlockSpec((4, s, bc), lambda j: (0, 0, j)), pl.BlockSpec((3, bc), lambda j: (0, j))],
        out_specs=pl.BlockSpec((s, bc), lambda j: (0, j)),
        out_shape=jax.ShapeDtypeStruct((s, e), BF16),
        compiler_params=_params(("parallel",)),
    )(proj, conv_w)


def _sc_bwd(proj, dy, conv_w, name, after=None):
    _, s, e = proj.shape
    bc = _blk(e, 256)

    def body(proj_ref, dy_ref, w_ref, dp_ref, dw_ref):
        b, cg, v, g, p, u = _sc_parts(proj_ref, w_ref)
        dyv = dy_ref[...].astype(F32)
        sig = _sigmoid(g)
        t = dyv * (g * sig)
        du = t * b
        dp_ref[0] = (t * u).astype(BF16)
        dp_ref[3] = (dyv * b * u * (sig * (1.0 + g * (1.0 - sig)))).astype(BF16)
        dpp = w_ref[2:3, :] * du + w_ref[1:2, :] * _shift_up(du, 1) + w_ref[0:1, :] * _shift_up(du, 2)
        dp_ref[1] = (dpp * v).astype(BF16)
        dp_ref[2] = (dpp * cg).astype(BF16)
        dw_ref[2:3, :] = _colsum(du * p)
        dw_ref[1:2, :] = _colsum(du * _shift_down(p, 1))
        dw_ref[0:1, :] = _colsum(du * _shift_down(p, 2))

    body, extra_specs, extra = _ordered(body, 3, after)
    return pl.pallas_call(
        body, name=name, grid=(e // bc,),
        in_specs=[pl.BlockSpec((4, s, bc), lambda j: (0, 0, j)), pl.BlockSpec((s, bc), lambda j: (0, j)),
                  pl.BlockSpec((3, bc), lambda j: (0, j))] + extra_specs,
        out_specs=[pl.BlockSpec((4, s, bc), lambda j: (0, 0, j)), pl.BlockSpec((3, bc), lambda j: (0, j))],
        out_shape=[jax.ShapeDtypeStruct((4, s, e), BF16), jax.ShapeDtypeStruct((3, e), F32)],
        compiler_params=_params(("parallel",)),
    )(proj, dy, conv_w, *extra)


def _softplus_neg(lam):
    u = jnp.exp(-jnp.abs(lam))
    w = 1.0 + u
    log1p = jnp.where(w == 1.0, u, jnp.log(w) * (u / jnp.where(w == 1.0, 1.0, w - 1.0)))
    return jnp.maximum(-lam, 0.0) + log1p


def _one_minus_exp(z):
    series = -z * (1.0 + z * (0.5 + z * (1.0 / 6.0 + z * (1.0 / 24.0))))
    return jnp.where(z > -0.02, series, 1.0 - jnp.exp(z))


def _scan_in_tiles(a, b, reverse):
    s = a.shape[0]
    rows = lax.broadcasted_iota(jnp.int32, a.shape, 0) & 7
    for step in (1, 2, 4):
        if reverse:
            ok = rows < 8 - step
            a_s, b_s = pltpu.roll(a, s - step, 0), pltpu.roll(b, s - step, 0)
        else:
            ok = rows >= step
            a_s, b_s = pltpu.roll(a, step, 0), pltpu.roll(b, step, 0)
        b = jnp.where(ok, a * b_s + b, b)
        a = jnp.where(ok, a * a_s, a)
    return a, b


def _scan(a, b, a_ref, b_ref, h_ref, reverse):
    s, c = a.shape
    n = s // 8
    a_t, b_t = _scan_in_tiles(a, b, reverse)
    a_ref[...] = a_t
    b_ref[...] = b_t

    def step(i, carry):
        gi = n - 1 - i if reverse else i
        sl = pl.ds(pl.multiple_of(gi * 8, 8), 8)
        h = b_ref[sl, :] + a_ref[sl, :] * carry
        h_ref[sl, :] = h
        return h[0:1, :] if reverse else h[7:8, :]

    lax.fori_loop(0, n, step, jnp.zeros((1, c), F32))
    return h_ref[...]


def _lru_specs(s, e_half, n_heads):
    hp = e_half // HEAD_DIM
    c = HEAD_DIM
    return dict(
        pair=pl.BlockSpec((2, None, s, c), lambda h: (0, h // hp, 0, h % hp)),
        conv_w=pl.BlockSpec((4, c), lambda h: (0, h)),
        chan=pl.BlockSpec((1, c), lambda h: (0, h)),
        w=pl.BlockSpec((4, None, c // 4, c), lambda h: (0, h, 0, 0)),
        bias=pl.BlockSpec((None, 1, c), lambda h: (h, 0, 0)),
        plane=pl.BlockSpec((s, c), lambda h: (0, h)),
    )


def _lru_forward_parts(vp, cw_ref, cb_ref, wa_ref, ba_ref, wx_ref, bx_ref, lam_ref):
    c = HEAD_DIM
    v = cb_ref[...] + sum(cw_ref[k:k + 1, :] * _shift_down(vp, 3 - k) for k in range(4))
    vb = v.astype(BF16)
    wa = wa_ref[...].reshape(c, c)
    wx = wx_ref[...].reshape(c, c)
    r = _sigmoid(jnp.dot(vb, wa, preferred_element_type=F32) + ba_ref[...])
    i = _sigmoid(jnp.dot(vb, wx, preferred_element_type=F32) + bx_ref[...])
    sp = _softplus_neg(lam_ref[...])
    la = (-RGLRU_C) * sp * r
    a = jnp.exp(la)
    nm = jnp.sqrt(_one_minus_exp(2.0 * la))
    return v, vb, wa, wx, r, i, sp, a, nm


def _lru_fwd(proj, conv_w, conv_b, w_a, b_a, w_x, b_x, lam, name):
    _, _, s, e_half = proj.shape
    n_heads = 2 * e_half // HEAD_DIM
    sp_ = _lru_specs(s, e_half, n_heads)

    def body(pg_ref, cw_ref, cb_ref, wa_ref, ba_ref, wx_ref, bx_ref, lam_ref, y_ref,
             sa_ref, sb_ref, sh_ref):
        v, _, _, _, _, i, _, a, nm = _lru_forward_parts(
            pg_ref[0].astype(F32), cw_ref, cb_ref, wa_ref, ba_ref, wx_ref, bx_ref, lam_ref)
        hs = _scan(a, nm * (i * v), sa_ref, sb_ref, sh_ref, reverse=False)
        g = pg_ref[1].astype(F32)
        y_ref[...] = (hs * (g * _sigmoid(g))).astype(BF16)

    return pl.pallas_call(
        body, name=name, grid=(n_heads,),
        in_specs=[sp_["pair"], sp_["conv_w"], sp_["chan"], sp_["w"], sp_["bias"], sp_["w"],
                  sp_["bias"], sp_["chan"]],
        out_specs=sp_["plane"],
        out_shape=jax.ShapeDtypeStruct((s, 2 * e_half), BF16),
        scratch_shapes=[pltpu.VMEM((s, HEAD_DIM), F32)] * 3,
        compiler_params=_params(("parallel",)),
    )(proj, conv_w, conv_b, w_a, b_a, w_x, b_x, lam)


def _lru_bwd(proj, dy, conv_w, conv_b, w_a, b_a, w_x, b_x, lam, name):
    _, _, s, e_half = proj.shape
    e = 2 * e_half
    c = HEAD_DIM
    n_heads = e // c
    hp = e_half // c
    sp_ = _lru_specs(s, e_half, n_heads)

    def body(pg_ref, dy_ref, cw_ref, cb_ref, wa_ref, ba_ref, wx_ref, bx_ref, lam_ref,
             dpg_ref, dwa_ref, dwx_ref, dba_ref, dbx_ref, dlam_ref, dcw_ref, dcb_ref,
             sa_ref, sb_ref, sh_ref, sd_ref):
        vp = pg_ref[0].astype(F32)
        v, vb, wa, wx, r, i, sp, a, nm = _lru_forward_parts(
            vp, cw_ref, cb_ref, wa_ref, ba_ref, wx_ref, bx_ref, lam_ref)
        iv = i * v
        hs = _scan(a, nm * iv, sa_ref, sb_ref, sh_ref, reverse=False)
        g = pg_ref[1].astype(F32)
        dyv = dy_ref[...].astype(F32)
        sig = _sigmoid(g)
        dpg_ref[1] = (dyv * hs * (sig * (1.0 + g * (1.0 - sig)))).astype(BF16)
        dh = _scan(_shift_up(a, 1), dyv * (g * sig), sa_ref, sb_ref, sd_ref, reverse=True)
        da = dh * _shift_down(hs, 1)
        div = dh * nm
        a2 = a * a
        dla = da * a - (dh * iv) * (a2 / nm)
        dzr = (dla * ((-RGLRU_C) * sp)) * (r * (1.0 - r))
        dzi = (div * v) * (i * (1.0 - i))
        lam = lam_ref[...]
        dlam_ref[...] = _colsum(dla * ((-RGLRU_C) * r)) * (-_sigmoid(-lam))
        dba_ref[...] = _colsum(dzr)
        dbx_ref[...] = _colsum(dzi)
        dzr_b, dzi_b = dzr.astype(BF16), dzi.astype(BF16)
        tn = (((0,), (0,)), ((), ()))
        nt = (((1,), (1,)), ((), ()))
        dwa_ref[...] = lax.dot_general(vb, dzr_b, tn, preferred_element_type=F32)
        dwx_ref[...] = lax.dot_general(vb, dzi_b, tn, preferred_element_type=F32)
        dv = (div * i + lax.dot_general(dzr_b, wa, nt, preferred_element_type=F32)
              + lax.dot_general(dzi_b, wx, nt, preferred_element_type=F32))
        dcb_ref[...] = _colsum(dv)
        dvp = jnp.zeros_like(dv)
        for k in range(4):
            dvp = dvp + cw_ref[k:k + 1, :] * _shift_up(dv, 3 - k)
            dcw_ref[k:k + 1, :] = _colsum(dv * _shift_down(vp, 3 - k))
        dpg_ref[0] = dvp.astype(BF16)

    head_mat = pl.BlockSpec((None, c, c), lambda h: (h, 0, 0))
    outs = pl.pallas_call(
        body, name=name, grid=(n_heads,),
        in_specs=[sp_["pair"], sp_["plane"], sp_["conv_w"], sp_["chan"], sp_["w"], sp_["bias"],
                  sp_["w"], sp_["bias"], sp_["chan"]],
        out_specs=[sp_["pair"], head_mat, head_mat, sp_["bias"], sp_["bias"],
                   sp_["chan"], sp_["conv_w"], sp_["chan"]],
        out_shape=[jax.ShapeDtypeStruct((2, 2, s, e_half), BF16),
                   jax.ShapeDtypeStruct((n_heads, c, c), F32), jax.ShapeDtypeStruct((n_heads, c, c), F32),
                   jax.ShapeDtypeStruct((n_heads, 1, c), F32), jax.ShapeDtypeStruct((n_heads, 1, c), F32),
                   jax.ShapeDtypeStruct((1, e), F32), jax.ShapeDtypeStruct((4, e), F32),
                   jax.ShapeDtypeStruct((1, e), F32)],
        scratch_shapes=[pltpu.VMEM((s, c), F32)] * 4,
        compiler_params=_params(("parallel",)),
    )(proj, dy, conv_w, conv_b, w_a, b_a, w_x, b_x, lam)
    return tuple(outs)


def _ada_fwd(c_all, ada_w, name):
    n_l, d, f = ada_w.shape
    bf = _blk(f, 512)

    def body(c_ref, w_ref, o_ref):
        cv = c_ref[...]
        sc = (cv * _sigmoid(cv)).astype(BF16)
        o_ref[...] = jnp.dot(sc, w_ref[...].astype(BF16), preferred_element_type=F32)

    return pl.pallas_call(
        body, name=name, grid=(n_l, f // bf),
        in_specs=[pl.BlockSpec((8, d), lambda l, j: (0, 0)), pl.BlockSpec((None, d, bf), lambda l, j: (l, 0, j))],
        out_specs=pl.BlockSpec((None, 8, bf), lambda l, j: (l, 0, j)),
        out_shape=jax.ShapeDtypeStruct((n_l, 8, f), F32),
        compiler_params=_params(("parallel", "parallel")),
    )(c_all, ada_w)


def _ada_bwd_adamw(c_t, dmod, w, m, v, name, after=None):
    n_l, d, f = w.shape
    bf = _blk(f, 256)

    def body(c_ref, dm_ref, w_ref, m_ref, v_ref, g_ref, d_ref, m2_ref, v2_ref):
        cv = c_ref[...]
        sc = cv * _sigmoid(cv)
        dm = dm_ref[...]
        g = sc[:, 0:1] * dm[0:1, :]
        for b in range(1, 8):
            g = g + sc[:, b:b + 1] * dm[b:b + 1, :]
        g_ref[...] = g
        dl, m2, v2 = _adamw_math(w_ref[...], g, m_ref[...], v_ref[...])
        d_ref[...] = dl
        m2_ref[...] = m2
        v2_ref[...] = v2

    big = pl.BlockSpec((None, d, bf), lambda l, j: (l, 0, j))
    body, extra_specs, extra = _ordered(body, 5, after)
    return pl.pallas_call(
        body, name=name, grid=(n_l, f // bf),
        in_specs=[pl.BlockSpec((d, 8), lambda l, j: (0, 0)), pl.BlockSpec((None, 8, bf), lambda l, j: (l, 0, j)),
                  big, big, big] + extra_specs,
        out_specs=[big] * 4, out_shape=[jax.ShapeDtypeStruct((n_l, d, f), F32)] * 4,
        compiler_params=_params(("parallel", "parallel")),
    )(c_t, dmod, w, m, v, *extra)


def _pack(parts):
    padded, offs, n = [], [], 0
    for p in parts:
        p = p.reshape(-1)
        size = -(-p.shape[0] // PACK) * PACK
        offs.append(n)
        n += size
        padded.append(jnp.pad(p, (0, size - p.shape[0])) if size != p.shape[0] else p)
    return jnp.concatenate(padded), offs, n


def kernel(x, c, norm_g, ada_w, ada_b, sc_w_in, sc_conv_w, sc_w_out, lru_w_in, lru_conv_w, lru_conv_b, lru_w_a, lru_b_a, lru_w_x, lru_b_x, lru_lambda, lru_w_out, final_g, loss_target, m_norm_g, m_ada_w, m_ada_b, m_sc_w_in, m_sc_conv_w, m_sc_w_out, m_lru_w_in, m_lru_conv_w, m_lru_conv_b, m_lru_w_a, m_lru_b_a, m_lru_w_x, m_lru_b_x, m_lru_lambda, m_lru_w_out, m_final_g, v_norm_g, v_ada_w, v_ada_b, v_sc_w_in, v_sc_conv_w, v_sc_w_out, v_lru_w_in, v_lru_conv_w, v_lru_conv_b, v_lru_w_a, v_lru_b_a, v_lru_w_x, v_lru_b_x, v_lru_lambda, v_lru_w_out, v_final_g):
    xi, yi, ci = _pos()
    chip = 2 * xi + yi
    batch = 4 * xi + 2 * yi + ci
    core_op = jnp.reshape(ci, (1,)).astype(jnp.int32)
    chip_op = jnp.reshape(chip, (1,)).astype(jnp.int32)
    where_op = jnp.stack([chip, ci]).astype(jnp.int32)

    x2d, tgt = x[0], loss_target[0]
    s, d = x2d.shape
    es = sc_conv_w.shape[2]
    e = 4 * es
    n_heads = lru_w_a.shape[1]
    hj = lru_b_a.shape[2]
    f = ada_w.shape[2]
    row = lambda t: t.reshape(1, -1)

    small_parts = [c, sc_conv_w, lru_conv_w, lru_conv_b, lru_b_a, lru_b_x, lru_lambda]
    small, offs, n_small = _pack(small_parts)
    got = _allgather8([small.reshape(8, n_small // 8)], "ag_small")[0].reshape(8, n_small)
    c_all = got[:, :d]
    per_chip = got[0::2]

    def chip_part(k, shape):
        size = 1
        for dim in shape:
            size *= dim
        return per_chip[:, offs[k]:offs[k] + size].reshape((4,) + shape)

    conv_w0 = jnp.transpose(chip_part(1, (3, es)), (1, 0, 2)).reshape(3, e)
    conv_w1 = jnp.transpose(chip_part(2, (4, es)), (1, 0, 2)).reshape(4, e)
    conv_b1 = chip_part(3, (es,)).reshape(1, e)
    b_a = jnp.transpose(chip_part(4, (n_heads, hj)), (1, 0, 2)).reshape(n_heads, 1, 4 * hj)
    b_x = jnp.transpose(chip_part(5, (n_heads, hj)), (1, 0, 2)).reshape(n_heads, 1, 4 * hj)
    lam = chip_part(6, (es,)).reshape(1, e)

    mod_nb = _ada_fwd(c_all, ada_w, "ada_fwd")
    mods = _allgather8([mod_nb.reshape(16, f)], "ag_mod")[0].reshape(8, 2, 8, f)[0::2]
    mine = lax.dynamic_index_in_dim(mods, batch, axis=2, keepdims=False)
    mod = jnp.transpose(mine, (1, 0, 2)).reshape(2, 4 * f) + ada_b
    shift = [row(mod[l, :d]) for l in range(2)]
    scale = [row(mod[l, d:2 * d]) for l in range(2)]
    gate = [row(mod[l, 2 * d:]) for l in range(2)]
    ng = [row(norm_g[l]) for l in range(2)]

    shards = [sc_w_in[0], sc_w_out[0], lru_w_in[0], lru_w_a[0].reshape(n_heads * hj, HEAD_DIM),
              lru_w_x[0].reshape(n_heads * hj, HEAD_DIM), lru_w_out[0]]
    names = ["sc_w_in", "sc_w_out", "lru_w_in", "lru_w_a", "lru_w_x", "lru_w_out"]
    slots = [_cast_into_slot(chip_op, w, "cast_" + nm) for w, nm in zip(shards, names)]
    in_flight, started = _gather_start([[slots[0]], [slots[1]], slots[2:5], [slots[5]]], mod, "ag_start")
    h0 = _norm_mod_fwd(x2d, ng[0], scale[0], shift[0], "norm0", after=started)

    def arrived(g, after, tag):
        send1, recv1, bufs = in_flight[g]
        send2, recv2, bufs, passed = _gather_forward(send1, recv1, bufs, after, "ag_forward_" + tag)
        return _gather_finish(send2, recv2, bufs, passed, "ag_finish_" + tag)

    w_in0 = arrived(0, h0, "sc_w_in")[0].reshape(4, d, e)
    proj0 = _mm(h0[None], w_in0, "nn", BF16, "sc_in")
    w_out0 = arrived(1, proj0, "sc_w_out")[0].reshape(1, e, d)
    y0 = _sc_fwd(proj0, conv_w0, "sc_mix")
    o0 = _mm(y0[None], w_out0, "nn", F32, "sc_out")[0]
    x1, h1 = _norm_mod_fwd(x2d, ng[1], scale[1], shift[1], "norm1", o=o0, gate=gate[0])
    lru_ws = arrived(2, h1, "lru_w_in")
    w_in1 = lru_ws[0].reshape(4, d, e // 2)
    w_a = lru_ws[1].reshape(4, n_heads, hj, HEAD_DIM)
    w_x = lru_ws[2].reshape(4, n_heads, hj, HEAD_DIM)
    proj1 = _mm(h1[None], w_in1, "nn", BF16, "lru_in")
    pairs1 = proj1.reshape(2, 2, s, e // 2)
    y1 = _lru_fwd(pairs1, conv_w1, conv_b1, w_a, b_a, w_x, b_x, lam, "lru_mix")
    w_out1 = arrived(3, y1, "lru_w_out")[0].reshape(1, e, d)
    o1 = _mm(y1[None], w_out1, "nn", F32, "lru_out")[0]
    dx2, do1, acc_f = _final_loss(x1, o1, gate[1], row(final_g), tgt, "final_loss")

    def reduce_stage1(tag, grads):
        lands = [lax.empty((4, g.shape[1] // 2, g.shape[2]), F32) for g in grads]
        return _exchange_start("rs_sibling_start_" + tag, grads, lands, len(grads),
                               _plan_other_half_to_sibling, None)

    def reduce_stage2(tag, stage1, nms, after):
        send, recv, grads, lands, _ = stage1
        grads, lands = _exchange_wait("rs_sibling_wait_" + tag, send, recv, grads, lands,
                                      _plan_other_half_to_sibling, after)
        parts = [_sum_own_and_sibling(core_op, g, r1, "rs_sum1_" + nm) for g, r1, nm in zip(grads, lands, nms)]
        lands = [lax.empty((3,) + p.shape[1:], BF16) for p in parts]
        return _exchange_start("rs_chips_start_" + tag, parts, lands, 3 * len(parts),
                               _plan_partials_to_chips, None)

    def reduce_stage3(tag, stage2, nms, after):
        send, recv, parts, lands, _ = stage2
        parts, lands = _exchange_wait("rs_chips_wait_" + tag, send, recv, parts, lands,
                                      _plan_partials_to_chips, after)
        halves = [_sum_chips(where_op, p, r2, "rs_sum2_" + nm) for p, r2, nm in zip(parts, lands, nms)]
        return _exchange_start("rs_share_start_" + tag, [], halves, len(halves), _plan_share_half, None)

    def reduce_done(tag, stage3, after):
        send, recv, _, fulls, _ = stage3
        return _exchange_wait("rs_share_wait_" + tag, send, recv, [], fulls, _plan_share_half, after)[1]

    def head_major_to_chip_major(t):
        return jnp.transpose(t.reshape(n_heads, 4, hj, HEAD_DIM), (1, 0, 2, 3)).reshape(4, n_heads * hj, HEAD_DIM)

    big_state = dict(zip(names, zip(shards, [m_sc_w_in, m_sc_w_out, m_lru_w_in, m_lru_w_a, m_lru_w_x, m_lru_w_out],
                                    [v_sc_w_in, v_sc_w_out, v_lru_w_in, v_lru_w_a, v_lru_w_x, v_lru_w_out],
                                    [sc_w_in, sc_w_out, lru_w_in, lru_w_a, lru_w_x, lru_w_out])))
    big = {}

    def update(nms, fulls, after):
        for nm, g2 in zip(nms, fulls):
            w2, m4, v4, w4 = big_state[nm]
            dl, m2, v2 = _adamw(w2, g2, m4.reshape(w2.shape), v4.reshape(w2.shape), "adamw_" + nm, after=after)
            big[nm] = tuple(t.reshape(w4.shape) for t in (g2, dl, m2, v2))
            after = dl
        return after

    g_w_out1 = _mm(y1[None], do1[None], "tn", F32, "lru_out_dw", bm=512, bn=2048)
    dy1 = _mm(do1[None], w_out1, "nt", BF16, "lru_out_dx")[0]
    dpairs1, g_wa, g_wx, g_ba, g_bx, g_lam, g_cw1, g_cb1 = _lru_bwd(
        pairs1, dy1, conv_w1, conv_b1, w_a, b_a, w_x, b_x, lam, "lru_mix_bwd")
    dproj1 = dpairs1.reshape(4, s, e // 2)
    g_w_in1 = _mm(h1[None], dproj1, "tn", F32, "lru_in_dw", bm=512, bn=2048)
    lru_names = ["lru_w_out", "lru_w_a", "lru_w_x", "lru_w_in"]
    lru_rs = reduce_stage1("lru", [g_w_out1.reshape(4, es, d), head_major_to_chip_major(g_wa),
                                   head_major_to_chip_major(g_wx), g_w_in1])
    dh1 = _mm(dproj1, w_in1, "nt", F32, "lru_in_dx", after=lru_rs[4])[0]
    lru_rs = reduce_stage2("lru", lru_rs, lru_names, dh1)
    dx1, do0, acc1 = _norm_mod_bwd(dh1, x1, ng[1], scale[1], dx2, "norm1_bwd", o_prev=o0, gate_prev=gate[0],
                                   after=lru_rs[4])

    g_w_out0 = _mm(y0[None], do0[None], "tn", F32, "sc_out_dw", bm=512, bn=2048)
    out_rs = reduce_stage1("sc_out", [g_w_out0.reshape(4, es, d)])
    dy0 = _mm(do0[None], w_out0, "nt", BF16, "sc_out_dx", after=out_rs[4])[0]
    out_rs = reduce_stage2("sc_out", out_rs, ["sc_w_out"], dy0)
    dproj0, g_cw0 = _sc_bwd(proj0, dy0, conv_w0, "sc_mix_bwd", after=out_rs[4])
    g_w_in0 = _mm(h0[None], dproj0, "tn", F32, "sc_in_dw", bm=512, bn=2048)
    in_rs = reduce_stage1("sc_in", [g_w_in0])
    lru_rs = reduce_stage3("lru", lru_rs, lru_names, in_rs[4])
    in_rs = reduce_stage2("sc_in", in_rs, ["sc_w_in"], lru_rs[4])
    dh0 = _mm(dproj0, w_in0, "nt", F32, "sc_in_dx", after=in_rs[4])[0]
    grad_x, acc0 = _norm_mod_bwd(dh0, x2d, ng[0], scale[0], dx1, "norm0_bwd")
    out_rs = reduce_stage3("sc_out", out_rs, ["sc_w_out"], acc0)
    last = update(lru_names, reduce_done("lru", lru_rs, out_rs[4]), None)
    last = update(["sc_w_out"], reduce_done("sc_out", out_rs, last), None)
    in_rs = reduce_stage3("sc_in", in_rs, ["sc_w_in"], last)

    dmod = jnp.stack([jnp.concatenate([acc0[1], acc0[0], acc1[3]]),
                      jnp.concatenate([acc1[1], acc1[0], acc_f[1]])])
    part_list = [jnp.stack([acc0[2], acc1[2]]), acc_f[0], acc_f[2, :1], g_cw0, g_cw1, g_cb1, g_ba, g_bx,
                 g_lam, dmod]
    partials, poffs, n_part = _pack(part_list)
    every = _allgather8([partials.reshape(8, n_part // 8)], "ag_partials", after=in_rs[4])[0].reshape(8, n_part)
    total = _sum_rows8(every, "sum_partials")[0]

    def tot(k, shape):
        size = 1
        for dim in shape:
            size *= dim
        return total[poffs[k]:poffs[k] + size].reshape(shape)

    def my_cols(t, width):
        return lax.dynamic_slice_in_dim(t, chip * width, width, axis=t.ndim - 1)

    loss = tot(2, (1,))[0]
    g_norm_g, g_final_g, g_ada_b = tot(0, (2, d)), tot(1, (d,)), tot(9, (2, 3 * d))
    g_sc_conv_w = my_cols(tot(3, (3, e)), es)[None]
    g_lru_conv_w = my_cols(tot(4, (4, e)), es)[None]
    g_lru_conv_b = my_cols(tot(5, (1, e)), es)
    g_lru_b_a = my_cols(tot(6, (n_heads, 4 * hj)), hj)[None]
    g_lru_b_x = my_cols(tot(7, (n_heads, 4 * hj)), hj)[None]
    g_lru_lambda = my_cols(tot(8, (1, e)), es)

    dmod_all = every[:, poffs[9]:poffs[9] + 6 * d].reshape(8, 2, 3 * d)
    dmod_mine = jnp.transpose(my_cols(dmod_all, f), (1, 0, 2))
    ada = _ada_bwd_adamw(jnp.transpose(c_all), dmod_mine, ada_w, m_ada_w, v_ada_w, "ada_bwd_adamw", after=total)

    small_names = ["norm_g", "ada_b", "final_g", "sc_conv_w", "lru_conv_w", "lru_conv_b", "lru_b_a",
                   "lru_b_x", "lru_lambda"]
    small_w = [norm_g, ada_b, final_g, sc_conv_w, lru_conv_w, lru_conv_b, lru_b_a, lru_b_x, lru_lambda]
    small_g = [g_norm_g, g_ada_b, g_final_g, g_sc_conv_w, g_lru_conv_w, g_lru_conv_b, g_lru_b_a,
               g_lru_b_x, g_lru_lambda]
    small_m = [m_norm_g, m_ada_b, m_final_g, m_sc_conv_w, m_lru_conv_w, m_lru_conv_b, m_lru_b_a,
               m_lru_b_x, m_lru_lambda]
    small_v = [v_norm_g, v_ada_b, v_final_g, v_sc_conv_w, v_lru_conv_w, v_lru_conv_b, v_lru_b_a,
               v_lru_b_x, v_lru_lambda]
    pw, soffs, n_s = _pack(small_w)
    pg, pm, pv = _pack(small_g)[0], _pack(small_m)[0], _pack(small_v)[0]
    shape2 = (n_s // PACK, PACK)
    pd, pm2, pv2 = _adamw(pw.reshape(shape2), pg.reshape(shape2), pm.reshape(shape2), pv.reshape(shape2),
                          "adamw_small", after=ada[0])
    update(["sc_w_in"], reduce_done("sc_in", in_rs, pd), None)
    small = {}
    for k, (nm, w_) in enumerate(zip(small_names, small_w)):
        take = lambda t: t.reshape(-1)[soffs[k]:soffs[k] + w_.size].reshape(w_.shape)
        small[nm] = (small_g[k].reshape(w_.shape), take(pd), take(pm2), take(pv2))

    results = dict(small)
    results.update(big)
    results["ada_w"] = tuple(ada)
    order = ["norm_g", "ada_w", "ada_b", "sc_w_in", "sc_conv_w", "sc_w_out", "lru_w_in", "lru_conv_w",
             "lru_conv_b", "lru_w_a", "lru_b_a", "lru_w_x", "lru_b_x", "lru_lambda", "lru_w_out", "final_g"]
    out = [loss, grad_x[None]]
    for kind in range(4):
        out += [results[nm][kind] for nm in order]
    return tuple(out)
```

```python
import functools

import jax
import jax.numpy as jnp
from jax import lax
from jax.experimental import pallas as pl
from jax.experimental.pallas import tpu as pltpu

F32 = jnp.float32
BF16 = jnp.bfloat16
MESH = pl.DeviceIdType.MESH
ANY = pl.BlockSpec(memory_space=pl.ANY)

RMS_EPS = 1e-6
RGLRU_C = 8.0
HEAD_DIM = 256
ADAM_LR = 0.001
ADAM_B1 = 0.9
ADAM_B2 = 0.999
ADAM_EPS = 1e-08
ADAM_WD = 0.01
ADAM_STEP = 10
V7X_VMEM_LIMIT = 56 * 1024 * 1024
LANES = 128
SUBLANES = 8
PACK = SUBLANES * LANES


def _blk(dim, pref, unit=LANES):
    if dim <= pref:
        return dim
    b = (pref // unit) * unit
    while b > unit and dim % b:
        b -= unit
    assert dim % b == 0, (dim, pref, unit)
    return b


def _params(sem=None):
    return pltpu.CompilerParams(dimension_semantics=sem, vmem_limit_bytes=V7X_VMEM_LIMIT)


def _ordered(body, n_in, after):
    if after is None:
        return body, [], []

    def ordered_body(*refs):
        return body(*refs[:n_in], *refs[n_in + 1:])

    return ordered_body, [ANY], [after]


def _pos():
    return lax.axis_index("x"), lax.axis_index("y"), lax.axis_index("c")


def _other_chips(x, y):
    return [(1 - x, y), (x, 1 - y), (1 - x, 1 - y)]


def _allgather8(arrs, name, after=None):
    n_t = len(arrs)
    ms = [a.shape[0] for a in arrs]

    def gather(*refs):
        ins, outs = refs[:n_t], refs[n_t:2 * n_t]
        send_sems, recv_sems, local_sems = refs[2 * n_t:]
        x, y, c = _pos()
        me, sibling = (x, y, c), (x, y, 1 - c)
        chips = _other_chips(x, y)

        def rows(t, px, py, pc):
            return outs[t].at[pl.ds((4 * px + 2 * py + pc) * ms[t], ms[t])]

        def copy(t, k, block, to, src=None):
            return pltpu.make_async_remote_copy(
                src_ref=rows(t, *block) if src is None else src, dst_ref=rows(t, *block),
                send_sem=send_sems.at[7 * t + k], recv_sem=recv_sems.at[7 * t + k],
                device_id=to, device_id_type=MESH)

        mine, first, passed = [], [], []
        for t in range(n_t):
            src = ins[t]
            cp = pltpu.make_async_copy(src, rows(t, *me), local_sems.at[t])
            cp.start()
            mine.append(cp)
            sends = [copy(t, 0, me, sibling, src=src)]
            sends += [copy(t, 1 + j, me, (*chip, c), src=src) for j, chip in enumerate(chips)]
            for cp in sends:
                cp.start()
            first += sends
        for t in range(n_t):
            for j, chip in enumerate(chips):
                copy(t, 1 + j, (*chip, c), me).wait_recv()
                cp = copy(t, 4 + j, (*chip, c), sibling)
                cp.start()
                passed.append(cp)
        for t in range(n_t):
            copy(t, 0, sibling, me).wait_recv()
            for j, chip in enumerate(chips):
                copy(t, 4 + j, (*chip, 1 - c), me).wait_recv()
        for cp in first + passed:
            cp.wait_send()
        for cp in mine:
            cp.wait()

    body, extra_specs, extra = _ordered(gather, n_t, after)
    return pl.pallas_call(
        body, name=name,
        out_shape=[jax.ShapeDtypeStruct((8 * m, a.shape[1]), a.dtype) for m, a in zip(ms, arrs)],
        in_specs=[ANY] * n_t + extra_specs, out_specs=[ANY] * n_t,
        scratch_shapes=[pltpu.SemaphoreType.DMA((7 * n_t,)), pltpu.SemaphoreType.DMA((7 * n_t,)),
                        pltpu.SemaphoreType.DMA((n_t,))],
    )(*arrs, *extra)


HBM = pl.BlockSpec(memory_space=pltpu.HBM)
SEM = pl.BlockSpec(memory_space=pltpu.SEMAPHORE)
TOKEN = pl.BlockSpec(memory_space=pltpu.VMEM)
IN_FLIGHT = pltpu.CompilerParams(has_side_effects=pltpu.SideEffectType.DATAFLOW_SIDE_EFFECTING)


def _in_hbm(arrs):
    return [pltpu.with_memory_space_constraint(a, pltpu.HBM) for a in arrs]


def _shard_rows(ref, h, px, py, pc):
    return ref.at[pl.ds((4 * px + 2 * py + pc) * h, h)]


def _gather_start(groups, after, name):
    bufs = [b for grp in groups for b in grp]
    n_t, n_g = len(bufs), len(groups)

    def body(*refs):
        sems, thru = refs[n_t + 1:n_t + 1 + 2 * n_g], refs[n_t + 1 + 2 * n_g:2 * n_t + 1 + 2 * n_g]
        token = refs[-1]
        x, y, c = _pos()
        t = 0
        for g, grp in enumerate(groups):
            for i in range(len(grp)):
                h = bufs[t].shape[0] // 8
                rows = _shard_rows(thru[t], h, x, y, c)
                for j, chip in enumerate(_other_chips(x, y)):
                    pltpu.make_async_remote_copy(
                        src_ref=rows, dst_ref=rows, send_sem=sems[2 * g].at[3 * i + j],
                        recv_sem=sems[2 * g + 1].at[3 * i + j], device_id=(*chip, c),
                        device_id_type=MESH).start()
                t += 1
        token[...] = jnp.zeros_like(token)

    sem_shapes = []
    for grp in groups:
        sem_shapes += [pltpu.SemaphoreType.DMA((3 * len(grp),))] * 2
    out = pl.pallas_call(
        body, name=name,
        out_shape=sem_shapes + [pltpu.HBM(b.shape, b.dtype) for b in bufs] + [jax.ShapeDtypeStruct((8, LANES), F32)],
        in_specs=[HBM] * n_t + [ANY], out_specs=[SEM] * (2 * n_g) + [HBM] * n_t + [TOKEN],
        input_output_aliases={t: 2 * n_g + t for t in range(n_t)},
        compiler_params=IN_FLIGHT,
    )(*_in_hbm(bufs), after)
    sems, thru, token = out[:2 * n_g], out[2 * n_g:2 * n_g + n_t], out[-1]
    per_group, t = [], 0
    for g, grp in enumerate(groups):
        per_group.append((sems[2 * g], sems[2 * g + 1], thru[t:t + len(grp)]))
        t += len(grp)
    return per_group, token


def _gather_forward(send_sems, recv_sems, bufs, after, name):
    n_t = len(bufs)

    def body(*refs):
        ins = refs[:n_t]
        send1, recv1 = refs[n_t], refs[n_t + 1]
        send2, recv2 = refs[n_t + 3], refs[n_t + 4]
        token = refs[-1]
        x, y, c = _pos()
        for t in range(n_t):
            h = bufs[t].shape[0] // 8
            mine = _shard_rows(ins[t], h, x, y, c)
            for j, chip in enumerate(_other_chips(x, y)):
                landed = _shard_rows(ins[t], h, *chip, c)
                pltpu.make_async_remote_copy(
                    src_ref=mine, dst_ref=landed, send_sem=send1.at[3 * t + j], recv_sem=recv1.at[3 * t + j],
                    device_id=(*chip, c), device_id_type=MESH).wait_recv()
                pltpu.make_async_remote_copy(
                    src_ref=landed, dst_ref=landed, send_sem=send2.at[3 * t + j], recv_sem=recv2.at[3 * t + j],
                    device_id=(x, y, 1 - c), device_id_type=MESH).start()
        for t in range(n_t):
            h = bufs[t].shape[0] // 8
            mine = _shard_rows(ins[t], h, x, y, c)
            for j, chip in enumerate(_other_chips(x, y)):
                pltpu.make_async_remote_copy(
                    src_ref=mine, dst_ref=mine, send_sem=send1.at[3 * t + j], recv_sem=recv1.at[3 * t + j],
                    device_id=(*chip, c), device_id_type=MESH).wait_send()
        token[...] = jnp.zeros_like(token)

    out = pl.pallas_call(
        body, name=name,
        out_shape=[pltpu.SemaphoreType.DMA((3 * n_t,))] * 2 + [pltpu.HBM(b.shape, b.dtype) for b in bufs]
        + [jax.ShapeDtypeStruct((8, LANES), F32)],
        in_specs=[HBM] * n_t + [SEM, SEM, ANY], out_specs=[SEM, SEM] + [HBM] * n_t + [TOKEN],
        input_output_aliases={t: 2 + t for t in range(n_t)},
        compiler_params=IN_FLIGHT,
    )(*bufs, send_sems, recv_sems, after)
    return out[0], out[1], out[2:2 + n_t], out[-1]


def _gather_finish(send_sems, recv_sems, bufs, after, name):
    n_t = len(bufs)

    def body(*refs):
        ins = refs[:n_t]
        send2, recv2 = refs[n_t], refs[n_t + 1]
        x, y, c = _pos()
        for t in range(n_t):
            h = bufs[t].shape[0] // 8
            for j, chip in enumerate(_other_chips(x, y)):
                sent = _shard_rows(ins[t], h, *chip, c)
                got = _shard_rows(ins[t], h, *chip, 1 - c)
                cp = pltpu.make_async_remote_copy(
                    src_ref=sent, dst_ref=got, send_sem=send2.at[3 * t + j], recv_sem=recv2.at[3 * t + j],
                    device_id=(x, y, 1 - c), device_id_type=MESH)
                cp.wait_send()
                cp.wait_recv()

    return pl.pallas_call(
        body, name=name, out_shape=[pltpu.HBM(b.shape, b.dtype) for b in bufs],
        in_specs=[HBM] * n_t + [SEM, SEM, ANY], out_specs=[HBM] * n_t,
        input_output_aliases={t: t for t in range(n_t)},
        compiler_params=IN_FLIGHT,
    )(*bufs, send_sems, recv_sems, after)


def _exchange_start(name, srcs, lands, n_copies, plan, after):
    ns, nl = len(srcs), len(lands)
    extra = [] if after is None else [after]

    def body(*refs):
        base = ns + nl + len(extra)
        send_sems, recv_sems = refs[base], refs[base + 1]
        src_refs, land_refs = refs[base + 2:base + 2 + ns], refs[base + 2 + ns:base + 2 + ns + nl]
        token = refs[-1]
        x, y, c = _pos()
        copies = plan(src_refs, land_refs, x, y, c)
        assert len(copies) == n_copies
        for k, (src, dst, dev) in enumerate(copies):
            pltpu.make_async_remote_copy(
                src_ref=src, dst_ref=dst, send_sem=send_sems.at[k], recv_sem=recv_sems.at[k],
                device_id=dev, device_id_type=MESH).start()
        token[...] = jnp.zeros_like(token)

    out = pl.pallas_call(
        body, name=name,
        out_shape=[pltpu.SemaphoreType.DMA((n_copies,))] * 2
        + [pltpu.HBM(a.shape, a.dtype) for a in list(srcs) + list(lands)] + [jax.ShapeDtypeStruct((8, LANES), F32)],
        in_specs=[HBM] * (ns + nl) + [ANY] * len(extra), out_specs=[SEM, SEM] + [HBM] * (ns + nl) + [TOKEN],
        input_output_aliases={i: 2 + i for i in range(ns + nl)},
        compiler_params=IN_FLIGHT,
    )(*_in_hbm(list(srcs) + list(lands)), *extra)
    return out[0], out[1], out[2:2 + ns], out[2 + ns:2 + ns + nl], out[-1]


def _exchange_wait(name, send_sems, recv_sems, srcs, lands, plan, after):
    ns, nl = len(srcs), len(lands)

    def body(*refs):
        src_refs, land_refs = refs[:ns], refs[ns:ns + nl]
        send, recv = refs[ns + nl], refs[ns + nl + 1]
        x, y, c = _pos()
        for k, (src, dst, dev) in enumerate(plan(src_refs, land_refs, x, y, c)):
            cp = pltpu.make_async_remote_copy(
                src_ref=src, dst_ref=dst, send_sem=send.at[k], recv_sem=recv.at[k],
                device_id=dev, device_id_type=MESH)
            cp.wait_send()
            cp.wait_recv()

    out = pl.pallas_call(
        body, name=name, out_shape=[pltpu.HBM(a.shape, a.dtype) for a in list(srcs) + list(lands)],
        in_specs=[HBM] * (ns + nl) + [SEM, SEM, ANY], out_specs=[HBM] * (ns + nl),
        input_output_aliases={i: i for i in range(ns + nl)},
        compiler_params=IN_FLIGHT,
    )(*srcs, *lands, send_sems, recv_sems, after)
    return out[:ns], out[ns:]


def _plan_other_half_to_sibling(src_refs, land_refs, x, y, c):
    out = []
    for g_ref, r_ref in zip(src_refs, land_refs):
        h = g_ref.shape[1] // 2
        out.append((g_ref.at[:, pl.ds((1 - c) * h, h), :], r_ref, (x, y, 1 - c)))
    return out


def _plan_partials_to_chips(src_refs, land_refs, x, y, c):
    out = []
    for p_ref, r_ref in zip(src_refs, land_refs):
        for j, (px, py) in enumerate(_other_chips(x, y)):
            out.append((p_ref.at[2 * px + py], r_ref.at[j], (px, py, c)))
    return out


def _plan_share_half(src_refs, land_refs, x, y, c):
    out = []
    for f_ref in land_refs:
        h = f_ref.shape[0] // 2
        rows = f_ref.at[pl.ds(c * h, h)]
        out.append((rows, rows, (x, y, 1 - c)))
    return out


def _cast_into_slot(chip, w, name):
    r, c = w.shape
    br, bc = _blk(r, 512, 8), _blk(c, 2048)
    nb = r // br

    def body(chip_ref, w_ref, o_ref):
        o_ref[...] = w_ref[...].astype(BF16)

    grid_spec = pltpu.PrefetchScalarGridSpec(
        num_scalar_prefetch=1, grid=(nb, c // bc),
        in_specs=[pl.BlockSpec((br, bc), lambda i, j, chip_ref: (i, j))],
        out_specs=pl.BlockSpec((br, bc), lambda i, j, chip_ref: (chip_ref[0] * nb + i, j)))
    return pl.pallas_call(
        body, name=name, grid_spec=grid_spec,
        out_shape=jax.ShapeDtypeStruct((4 * r, c), BF16),
        compiler_params=_params(("parallel", "parallel")),
    )(chip, w)


def _sum_own_and_sibling(core, g, r1, name):
    _, r, c = g.shape
    h = r // 2
    br, bc = _blk(h, 256, 8), _blk(c, 2048)
    nb = h // br

    def body(core_ref, g_ref, r_ref, o_ref):
        o_ref[...] = (g_ref[...] + r_ref[...]).astype(BF16)

    grid_spec = pltpu.PrefetchScalarGridSpec(
        num_scalar_prefetch=1, grid=(4, nb, c // bc),
        in_specs=[pl.BlockSpec((None, br, bc), lambda k, i, j, core_ref: (k, core_ref[0] * nb + i, j)),
                  pl.BlockSpec((None, br, bc), lambda k, i, j, core_ref: (k, i, j))],
        out_specs=pl.BlockSpec((None, br, bc), lambda k, i, j, core_ref: (k, i, j)))
    return pl.pallas_call(
        body, name=name, grid_spec=grid_spec,
        out_shape=jax.ShapeDtypeStruct((4, h, c), BF16),
        compiler_params=_params(("parallel", "parallel", "parallel")),
    )(core, g, r1)


def _sum_chips(where, p, r2, name):
    _, h, c = p.shape
    br, bc = _blk(h, 256, 8), _blk(c, 2048)
    nb = h // br

    def body(where_ref, p_ref, r_ref, o_ref):
        acc = p_ref[...].astype(F32)
        for j in range(3):
            acc = acc + r_ref[j].astype(F32)
        o_ref[...] = acc

    grid_spec = pltpu.PrefetchScalarGridSpec(
        num_scalar_prefetch=1, grid=(nb, c // bc),
        in_specs=[pl.BlockSpec((None, br, bc), lambda i, j, where_ref: (where_ref[0], i, j)),
                  pl.BlockSpec((3, br, bc), lambda i, j, where_ref: (0, i, j))],
        out_specs=pl.BlockSpec((br, bc), lambda i, j, where_ref: (where_ref[1] * nb + i, j)))
    return pl.pallas_call(
        body, name=name, grid_spec=grid_spec,
        out_shape=jax.ShapeDtypeStruct((2 * h, c), F32),
        compiler_params=_params(("parallel", "parallel")),
    )(where, p, r2)


def _adamw_math(w, g, m, v):
    m2 = ADAM_B1 * m + (1.0 - ADAM_B1) * g
    v2 = ADAM_B2 * v + (1.0 - ADAM_B2) * (g * g)
    m_hat = m2 / (1.0 - ADAM_B1 ** ADAM_STEP)
    v_hat = v2 / (1.0 - ADAM_B2 ** ADAM_STEP)
    delta = -ADAM_LR * (m_hat / (jnp.sqrt(v_hat) + ADAM_EPS) + ADAM_WD * w)
    return delta, m2, v2


def _adamw(w, g, m, v, name, after=None):
    r, c = w.shape
    br, bc = _blk(r, 128, 8), _blk(c, 2048)

    def body(w_ref, g_ref, m_ref, v_ref, go_ref, d_ref, m2_ref, v2_ref):
        gv = g_ref[...]
        d, m2, v2 = _adamw_math(w_ref[...], gv, m_ref[...], v_ref[...])
        go_ref[...] = gv
        d_ref[...] = d
        m2_ref[...] = m2
        v2_ref[...] = v2

    spec = pl.BlockSpec((br, bc), lambda i, j: (i, j))
    body, extra_specs, extra = _ordered(body, 4, after)
    return pl.pallas_call(
        body, name=name, grid=(r // br, c // bc),
        in_specs=[spec] * 4 + extra_specs, out_specs=[spec] * 4,
        out_shape=[jax.ShapeDtypeStruct((r, c), F32)] * 4,
        compiler_params=_params(("parallel", "parallel")),
    )(w, g, m, v, *extra)


def _sum_rows8(g, name, after=None):
    n = g.shape[1]

    def body(g_ref, o_ref):
        acc = g_ref[0:1, :]
        for k in range(1, 8):
            acc = acc + g_ref[k:k + 1, :]
        o_ref[...] = acc

    body, extra_specs, extra = _ordered(body, 1, after)
    return pl.pallas_call(
        body, name=name, out_shape=jax.ShapeDtypeStruct((1, n), F32),
        in_specs=[pl.BlockSpec(memory_space=pltpu.VMEM)] + extra_specs,
        out_specs=pl.BlockSpec(memory_space=pltpu.VMEM),
        compiler_params=_params(),
    )(g, *extra)


def _mm(a, b, mode, out_dtype, name, bm=1024, bn=None, after=None):
    if mode == "nn":
        (_, m, k), (g, _, n) = a.shape, b.shape
    elif mode == "tn":
        (_, k, m), (g, _, n) = a.shape, b.shape
    else:
        (g, m, k), (_, n, _) = a.shape, b.shape
    if bn is None:
        bn = 1024 if k <= 2048 else 512
    bm, bn = _blk(m, bm), _blk(n, bn)

    if mode == "nt":
        def body(a_ref, b_ref, o_ref, acc_ref):
            part = lax.dot_general(a_ref[...], b_ref[...], (((1,), (1,)), ((), ())),
                                   preferred_element_type=F32)
            if g == 1:
                o_ref[...] = part.astype(out_dtype)
            else:
                gi = pl.program_id(2)

                @pl.when(gi == 0)
                def _():
                    acc_ref[...] = part

                @pl.when(gi > 0)
                def _():
                    acc_ref[...] += part

                @pl.when(gi == g - 1)
                def _():
                    o_ref[...] = acc_ref[...].astype(out_dtype)

        body, extra_specs, extra = _ordered(body, 2, after)
        return pl.pallas_call(
            body, name=name, grid=(m // bm, n // bn, g),
            in_specs=[pl.BlockSpec((None, bm, k), lambda i, j, gi: (gi, i, 0)),
                      pl.BlockSpec((None, bn, k), lambda i, j, gi: (gi, j, 0))] + extra_specs,
            out_specs=pl.BlockSpec((None, bm, bn), lambda i, j, gi: (0, i, j)),
            out_shape=jax.ShapeDtypeStruct((1, m, n), out_dtype),
            scratch_shapes=[pltpu.VMEM((bm, bn), F32)],
            compiler_params=_params(("parallel", "parallel", "arbitrary")),
        )(a, b, *extra)

    contract = (((1,), (0,)), ((), ())) if mode == "nn" else (((0,), (0,)), ((), ()))

    def body(a_ref, b_ref, o_ref):
        o_ref[...] = lax.dot_general(a_ref[...], b_ref[...], contract,
                                     preferred_element_type=F32).astype(out_dtype)

    a_spec = (pl.BlockSpec((None, bm, k), lambda i, gi, j: (0, i, 0)) if mode == "nn"
              else pl.BlockSpec((None, k, bm), lambda i, gi, j: (0, 0, i)))
    body, extra_specs, extra = _ordered(body, 2, after)
    return pl.pallas_call(
        body, name=name, grid=(m // bm, g, n // bn),
        in_specs=[a_spec, pl.BlockSpec((None, k, bn), lambda i, gi, j: (gi, 0, j))] + extra_specs,
        out_specs=pl.BlockSpec((None, bm, bn), lambda i, gi, j: (gi, i, j)),
        out_shape=jax.ShapeDtypeStruct((g, m, n), out_dtype),
        compiler_params=_params(("parallel", "parallel", "parallel")),
    )(a, b, *extra)


def _row_specs(br, d):
    return (pl.BlockSpec((br, d), lambda i: (i, 0)), pl.BlockSpec((1, d), lambda i: (0, 0)),
            pl.BlockSpec((8, d), lambda i: (0, 0)))


def _rstd(xv):
    return lax.rsqrt(jnp.mean(xv * xv, axis=-1, keepdims=True) + RMS_EPS)


def _colsum(v):
    return jnp.sum(v, axis=0, keepdims=True)


def _norm_mod_fwd(x, g, scale, shift, name, o=None, gate=None, after=None):
    s, d = x.shape
    br = _blk(s, 256, 8)
    has_res = o is not None
    row, vec, _ = _row_specs(br, d)

    def body(*refs):
        if has_res:
            x_ref, o_ref, gate_ref, g_ref, sc_ref, sh_ref, x1_ref, h_ref = refs
            xv = x_ref[...] + gate_ref[...] * o_ref[...]
            x1_ref[...] = xv
        else:
            x_ref, g_ref, sc_ref, sh_ref, h_ref = refs
            xv = x_ref[...]
        n = xv * _rstd(xv) * g_ref[...]
        h_ref[...] = (n * (1.0 + sc_ref[...]) + sh_ref[...]).astype(BF16)

    ins = [x] + ([o, gate] if has_res else []) + [g, scale, shift]
    in_specs = [row] + ([row, vec] if has_res else []) + [vec] * 3
    out_shape = ([jax.ShapeDtypeStruct((s, d), F32)] if has_res else []) + [jax.ShapeDtypeStruct((s, d), BF16)]
    body, extra_specs, extra = _ordered(body, len(ins), after)
    out = pl.pallas_call(
        body, name=name, grid=(s // br,), in_specs=in_specs + extra_specs, out_specs=[row] * len(out_shape),
        out_shape=out_shape, compiler_params=_params(("parallel",)),
    )(*ins, *extra)
    return out if has_res else out[0]


def _final_loss(x1, o1, gate1, final_g, tgt, name):
    s, d = x1.shape
    br = _blk(s, 256, 8)
    row, vec, acc = _row_specs(br, d)

    def body(x1_ref, o_ref, gate_ref, g_ref, t_ref, dx_ref, do_ref, acc_ref):
        @pl.when(pl.program_id(0) == 0)
        def _():
            acc_ref[...] = jnp.zeros_like(acc_ref)

        gate, o, g = gate_ref[...], o_ref[...], g_ref[...]
        x2 = x1_ref[...] + gate * o
        r = _rstd(x2)
        xh = x2 * r
        err = xh * g - t_ref[...]
        loss = 0.5 * _colsum(jnp.mean(err * err, axis=-1, keepdims=True))
        dout = err * (1.0 / d)
        dxh = dout * g
        dx2 = r * (dxh - xh * jnp.mean(dxh * xh, axis=-1, keepdims=True))
        dx_ref[...] = dx2
        do_ref[...] = (dx2 * gate).astype(BF16)
        acc_ref[0:1, :] += _colsum(dout * xh)
        acc_ref[1:2, :] += _colsum(dx2 * o)
        acc_ref[2:3, :] += jnp.broadcast_to(loss, (1, d))

    return pl.pallas_call(
        body, name=name, grid=(s // br,),
        in_specs=[row, row, vec, vec, row], out_specs=[row, row, acc],
        out_shape=[jax.ShapeDtypeStruct((s, d), F32), jax.ShapeDtypeStruct((s, d), BF16),
                   jax.ShapeDtypeStruct((8, d), F32)],
        compiler_params=_params(("arbitrary",)),
    )(x1, o1, gate1, final_g, tgt)


def _norm_mod_bwd(dh, x, g, scale, dx_next, name, o_prev=None, gate_prev=None, after=None):
    s, d = x.shape
    br = _blk(s, 256, 8)
    has_prev = o_prev is not None
    row, vec, acc = _row_specs(br, d)

    def body(*refs):
        if has_prev:
            dh_ref, x_ref, g_ref, sc_ref, dxn_ref, o_ref, gate_ref, dx_ref, do_ref, acc_ref = refs
        else:
            dh_ref, x_ref, g_ref, sc_ref, dxn_ref, dx_ref, acc_ref = refs

        @pl.when(pl.program_id(0) == 0)
        def _():
            acc_ref[...] = jnp.zeros_like(acc_ref)

        xv, gv, dhv = x_ref[...], g_ref[...], dh_ref[...]
        r = _rstd(xv)
        xh = xv * r
        acc_ref[0:1, :] += _colsum(dhv * (xh * gv))
        acc_ref[1:2, :] += _colsum(dhv)
        dn = dhv * (1.0 + sc_ref[...])
        acc_ref[2:3, :] += _colsum(dn * xh)
        dxh = dn * gv
        dx = dxn_ref[...] + r * (dxh - xh * jnp.mean(dxh * xh, axis=-1, keepdims=True))
        dx_ref[...] = dx
        if has_prev:
            acc_ref[3:4, :] += _colsum(dx * o_ref[...])
            do_ref[...] = (dx * gate_ref[...]).astype(BF16)

    ins = [dh, x, g, scale, dx_next] + ([o_prev, gate_prev] if has_prev else [])
    in_specs = [row, row, vec, vec, row] + ([row, vec] if has_prev else [])
    out_shape = [jax.ShapeDtypeStruct((s, d), F32)]
    out_specs = [row]
    if has_prev:
        out_shape.append(jax.ShapeDtypeStruct((s, d), BF16))
        out_specs.append(row)
    out_shape.append(jax.ShapeDtypeStruct((8, d), F32))
    out_specs.append(acc)
    body, extra_specs, extra = _ordered(body, len(ins), after)
    return pl.pallas_call(
        body, name=name, grid=(s // br,), in_specs=in_specs + extra_specs, out_specs=out_specs,
        out_shape=out_shape, compiler_params=_params(("arbitrary",)),
    )(*ins, *extra)


def _tiles(p):
    s, c = p.shape
    return p.reshape(s // SUBLANES, SUBLANES, c)


def _shift_down(p, k):
    if k == 0:
        return p
    r = pltpu.roll(_tiles(p), k, 1)
    before = jnp.concatenate([jnp.zeros_like(r[:1]), r[:-1]], axis=0)
    rows = lax.broadcasted_iota(jnp.int32, r.shape, 1)
    return jnp.where(rows >= k, r, before).reshape(p.shape)


def _shift_up(p, k):
    if k == 0:
        return p
    r = pltpu.roll(_tiles(p), SUBLANES - k, 1)
    after = jnp.concatenate([r[1:], jnp.zeros_like(r[:1])], axis=0)
    rows = lax.broadcasted_iota(jnp.int32, r.shape, 1)
    return jnp.where(rows < SUBLANES - k, r, after).reshape(p.shape)


def _sigmoid(z):
    return 0.5 * (jnp.tanh(0.5 * z) + 1.0)


def _sc_parts(proj_ref, w_ref):
    b, cg, v, g = (proj_ref[i].astype(F32) for i in range(4))
    p = cg * v
    u = w_ref[2:3, :] * p + w_ref[1:2, :] * _shift_down(p, 1) + w_ref[0:1, :] * _shift_down(p, 2)
    return b, cg, v, g, p, u


def _sc_fwd(proj, conv_w, name):
    _, s, e = proj.shape
    bc = _blk(e, 256)

    def body(proj_ref, w_ref, y_ref):
        b, _, _, g, _, u = _sc_parts(proj_ref, w_ref)
        y_ref[...] = (b * u * (g * _sigmoid(g))).astype(BF16)

    return pl.pallas_call(
        body, name=name, grid=(e // bc,),
        in_specs=[pl.BlockSpec((4, s, bc), lambda j: (0, 0, j)), pl.BlockSpec((3, bc), lambda j: (0, j))],
        out_specs=pl.BlockSpec((s, bc), lambda j: (0, j)),
        out_shape=jax.ShapeDtypeStruct((s, e), BF16),
        compiler_params=_params(("parallel",)),
    )(proj, conv_w)


def _sc_bwd(proj, dy, conv_w, name, after=None):
    _, s, e = proj.shape
    bc = _blk(e, 256)

    def body(proj_ref, dy_ref, w_ref, dp_ref, dw_ref):
        b, cg, v, g, p, u = _sc_parts(proj_ref, w_ref)
        dyv = dy_ref[...].astype(F32)
        sig = _sigmoid(g)
        t = dyv * (g * sig)
        du = t * b
        dp_ref[0] = (t * u).astype(BF16)
        dp_ref[3] = (dyv * b * u * (sig * (1.0 + g * (1.0 - sig)))).astype(BF16)
        dpp = w_ref[2:3, :] * du + w_ref[1:2, :] * _shift_up(du, 1) + w_ref[0:1, :] * _shift_up(du, 2)
        dp_ref[1] = (dpp * v).astype(BF16)
        dp_ref[2] = (dpp * cg).astype(BF16)
        dw_ref[2:3, :] = _colsum(du * p)
        dw_ref[1:2, :] = _colsum(du * _shift_down(p, 1))
        dw_ref[0:1, :] = _colsum(du * _shift_down(p, 2))

    body, extra_specs, extra = _ordered(body, 3, after)
    return pl.pallas_call(
        body, name=name, grid=(e // bc,),
        in_specs=[pl.BlockSpec((4, s, bc), lambda j: (0, 0, j)), pl.BlockSpec((s, bc), lambda j: (0, j)),
                  pl.BlockSpec((3, bc), lambda j: (0, j))] + extra_specs,
        out_specs=[pl.BlockSpec((4, s, bc), lambda j: (0, 0, j)), pl.BlockSpec((3, bc), lambda j: (0, j))],
        out_shape=[jax.ShapeDtypeStruct((4, s, e), BF16), jax.ShapeDtypeStruct((3, e), F32)],
        compiler_params=_params(("parallel",)),
    )(proj, dy, conv_w, *extra)


def _softplus_neg(lam):
    u = jnp.exp(-jnp.abs(lam))
    w = 1.0 + u
    log1p = jnp.where(w == 1.0, u, jnp.log(w) * (u / jnp.where(w == 1.0, 1.0, w - 1.0)))
    return jnp.maximum(-lam, 0.0) + log1p


def _one_minus_exp(z):
    series = -z * (1.0 + z * (0.5 + z * (1.0 / 6.0 + z * (1.0 / 24.0))))
    return jnp.where(z > -0.02, series, 1.0 - jnp.exp(z))


def _scan_in_tiles(a, b, reverse):
    shape = a.shape
    a, b = _tiles(a), _tiles(b)
    rows = lax.broadcasted_iota(jnp.int32, a.shape, 1)
    for step in (1, 2, 4):
        shift = SUBLANES - step if reverse else step
        ok = rows < SUBLANES - step if reverse else rows >= step
        a_s, b_s = pltpu.roll(a, shift, 1), pltpu.roll(b, shift, 1)
        b = jnp.where(ok, a * b_s + b, b)
        a = jnp.where(ok, a * a_s, a)
    return a.reshape(shape), b.reshape(shape)


def _scan(a, b, a_ref, b_ref, h_ref, reverse):
    s, c = a.shape
    n = s // 8
    a_t, b_t = _scan_in_tiles(a, b, reverse)
    a_ref[...] = a_t
    b_ref[...] = b_t

    def step(i, carry):
        gi = n - 1 - i if reverse else i
        sl = pl.ds(pl.multiple_of(gi * 8, 8), 8)
        h = b_ref[sl, :] + a_ref[sl, :] * carry
        h_ref[sl, :] = h
        return h[0:1, :] if reverse else h[7:8, :]

    lax.fori_loop(0, n, step, jnp.zeros((1, c), F32))
    return h_ref[...]


def _lru_specs(s, e_half, n_heads):
    hp = e_half // HEAD_DIM
    c = HEAD_DIM
    return dict(
        pair=pl.BlockSpec((2, None, s, c), lambda h: (0, h // hp, 0, h % hp)),
        conv_w=pl.BlockSpec((4, c), lambda h: (0, h)),
        chan=pl.BlockSpec((1, c), lambda h: (0, h)),
        w=pl.BlockSpec((4, None, c // 4, c), lambda h: (0, h, 0, 0)),
        bias=pl.BlockSpec((None, 1, c), lambda h: (h, 0, 0)),
        plane=pl.BlockSpec((s, c), lambda h: (0, h)),
    )


def _lru_forward_parts(vp, cw_ref, cb_ref, wa_ref, ba_ref, wx_ref, bx_ref, lam_ref):
    c = HEAD_DIM
    taps = [_shift_down(vp, 3 - k) for k in range(4)]
    v = cb_ref[...] + sum(cw_ref[k:k + 1, :] * taps[k] for k in range(4))
    vb = v.astype(BF16)
    wa = wa_ref[...].reshape(c, c)
    wx = wx_ref[...].reshape(c, c)
    r = _sigmoid(jnp.dot(vb, wa, preferred_element_type=F32) + ba_ref[...])
    i = _sigmoid(jnp.dot(vb, wx, preferred_element_type=F32) + bx_ref[...])
    sp = _softplus_neg(lam_ref[...])
    la = (-RGLRU_C) * sp * r
    a = jnp.exp(la)
    nm = jnp.sqrt(_one_minus_exp(2.0 * la))
    return taps, v, vb, wa, wx, r, i, sp, a, nm


def _lru_fwd(proj, conv_w, conv_b, w_a, b_a, w_x, b_x, lam, name):
    _, _, s, e_half = proj.shape
    n_heads = 2 * e_half // HEAD_DIM
    sp_ = _lru_specs(s, e_half, n_heads)

    def body(pg_ref, cw_ref, cb_ref, wa_ref, ba_ref, wx_ref, bx_ref, lam_ref, y_ref,
             sa_ref, sb_ref, sh_ref):
        _, v, _, _, _, _, i, _, a, nm = _lru_forward_parts(
            pg_ref[0].astype(F32), cw_ref, cb_ref, wa_ref, ba_ref, wx_ref, bx_ref, lam_ref)
        hs = _scan(a, nm * (i * v), sa_ref, sb_ref, sh_ref, reverse=False)
        g = pg_ref[1].astype(F32)
        y_ref[...] = (hs * (g * _sigmoid(g))).astype(BF16)

    return pl.pallas_call(
        body, name=name, grid=(n_heads,),
        in_specs=[sp_["pair"], sp_["conv_w"], sp_["chan"], sp_["w"], sp_["bias"], sp_["w"],
                  sp_["bias"], sp_["chan"]],
        out_specs=sp_["plane"],
        out_shape=jax.ShapeDtypeStruct((s, 2 * e_half), BF16),
        scratch_shapes=[pltpu.VMEM((s, HEAD_DIM), F32)] * 3,
        compiler_params=_params(("parallel",)),
    )(proj, conv_w, conv_b, w_a, b_a, w_x, b_x, lam)


def _lru_bwd(proj, dy, conv_w, conv_b, w_a, b_a, w_x, b_x, lam, name):
    _, _, s, e_half = proj.shape
    e = 2 * e_half
    c = HEAD_DIM
    n_heads = e // c
    hp = e_half // c
    sp_ = _lru_specs(s, e_half, n_heads)

    def body(pg_ref, dy_ref, cw_ref, cb_ref, wa_ref, ba_ref, wx_ref, bx_ref, lam_ref,
             dpg_ref, dwa_ref, dwx_ref, dba_ref, dbx_ref, dlam_ref, dcw_ref, dcb_ref,
             sa_ref, sb_ref, sh_ref, sd_ref):
        taps, v, vb, wa, wx, r, i, sp, a, nm = _lru_forward_parts(
            pg_ref[0].astype(F32), cw_ref, cb_ref, wa_ref, ba_ref, wx_ref, bx_ref, lam_ref)
        iv = i * v
        hs = _scan(a, nm * iv, sa_ref, sb_ref, sh_ref, reverse=False)
        g = pg_ref[1].astype(F32)
        dyv = dy_ref[...].astype(F32)
        sig = _sigmoid(g)
        dpg_ref[1] = (dyv * hs * (sig * (1.0 + g * (1.0 - sig)))).astype(BF16)
        dh = _scan(_shift_up(a, 1), dyv * (g * sig), sa_ref, sb_ref, sd_ref, reverse=True)
        da = dh * _shift_down(hs, 1)
        div = dh * nm
        a2 = a * a
        dla = da * a - (dh * iv) * (a2 / nm)
        dzr = (dla * ((-RGLRU_C) * sp)) * (r * (1.0 - r))
        dzi = (div * v) * (i * (1.0 - i))
        lam = lam_ref[...]
        dlam_ref[...] = _colsum(dla * ((-RGLRU_C) * r)) * (-_sigmoid(-lam))
        dba_ref[...] = _colsum(dzr)
        dbx_ref[...] = _colsum(dzi)
        dzr_b, dzi_b = dzr.astype(BF16), dzi.astype(BF16)
        tn = (((0,), (0,)), ((), ()))
        nt = (((1,), (1,)), ((), ()))
        dwa_ref[...] = lax.dot_general(vb, dzr_b, tn, preferred_element_type=F32)
        dwx_ref[...] = lax.dot_general(vb, dzi_b, tn, preferred_element_type=F32)
        dv = (div * i + lax.dot_general(dzr_b, wa, nt, preferred_element_type=F32)
              + lax.dot_general(dzi_b, wx, nt, preferred_element_type=F32))
        dcb_ref[...] = _colsum(dv)
        dvp = jnp.zeros_like(dv)
        for k in range(4):
            dvp = dvp + cw_ref[k:k + 1, :] * _shift_up(dv, 3 - k)
            dcw_ref[k:k + 1, :] = _colsum(dv * taps[k])
        dpg_ref[0] = dvp.astype(BF16)

    head_mat = pl.BlockSpec((None, c, c), lambda h: (h, 0, 0))
    outs = pl.pallas_call(
        body, name=name, grid=(n_heads,),
        in_specs=[sp_["pair"], sp_["plane"], sp_["conv_w"], sp_["chan"], sp_["w"], sp_["bias"],
                  sp_["w"], sp_["bias"], sp_["chan"]],
        out_specs=[sp_["pair"], head_mat, head_mat, sp_["bias"], sp_["bias"],
                   sp_["chan"], sp_["conv_w"], sp_["chan"]],
        out_shape=[jax.ShapeDtypeStruct((2, 2, s, e_half), BF16),
                   jax.ShapeDtypeStruct((n_heads, c, c), F32), jax.ShapeDtypeStruct((n_heads, c, c), F32),
                   jax.ShapeDtypeStruct((n_heads, 1, c), F32), jax.ShapeDtypeStruct((n_heads, 1, c), F32),
                   jax.ShapeDtypeStruct((1, e), F32), jax.ShapeDtypeStruct((4, e), F32),
                   jax.ShapeDtypeStruct((1, e), F32)],
        scratch_shapes=[pltpu.VMEM((s, c), F32)] * 4,
        compiler_params=_params(("parallel",)),
    )(proj, dy, conv_w, conv_b, w_a, b_a, w_x, b_x, lam)
    return tuple(outs)


def _ada_fwd(c_all, ada_w, name):
    n_l, d, f = ada_w.shape
    bf = _blk(f, 512)

    def body(c_ref, w_ref, o_ref):
        cv = c_ref[...]
        sc = (cv * _sigmoid(cv)).astype(BF16)
        o_ref[...] = jnp.dot(sc, w_ref[...].astype(BF16), preferred_element_type=F32)

    return pl.pallas_call(
        body, name=name, grid=(n_l, f // bf),
        in_specs=[pl.BlockSpec((8, d), lambda l, j: (0, 0)), pl.BlockSpec((None, d, bf), lambda l, j: (l, 0, j))],
        out_specs=pl.BlockSpec((None, 8, bf), lambda l, j: (l, 0, j)),
        out_shape=jax.ShapeDtypeStruct((n_l, 8, f), F32),
        compiler_params=_params(("parallel", "parallel")),
    )(c_all, ada_w)


def _ada_bwd_adamw(c_t, dmod, w, m, v, name, after=None):
    n_l, d, f = w.shape
    bf = _blk(f, 256)

    def body(c_ref, dm_ref, w_ref, m_ref, v_ref, g_ref, d_ref, m2_ref, v2_ref):
        cv = c_ref[...]
        sc = cv * _sigmoid(cv)
        dm = dm_ref[...]
        g = sc[:, 0:1] * dm[0:1, :]
        for b in range(1, 8):
            g = g + sc[:, b:b + 1] * dm[b:b + 1, :]
        g_ref[...] = g
        dl, m2, v2 = _adamw_math(w_ref[...], g, m_ref[...], v_ref[...])
        d_ref[...] = dl
        m2_ref[...] = m2
        v2_ref[...] = v2

    big = pl.BlockSpec((None, d, bf), lambda l, j: (l, 0, j))
    body, extra_specs, extra = _ordered(body, 5, after)
    return pl.pallas_call(
        body, name=name, grid=(n_l, f // bf),
        in_specs=[pl.BlockSpec((d, 8), lambda l, j: (0, 0)), pl.BlockSpec((None, 8, bf), lambda l, j: (l, 0, j)),
                  big, big, big] + extra_specs,
        out_specs=[big] * 4, out_shape=[jax.ShapeDtypeStruct((n_l, d, f), F32)] * 4,
        compiler_params=_params(("parallel", "parallel")),
    )(c_t, dmod, w, m, v, *extra)


def _pack(parts):
    padded, offs, n = [], [], 0
    for p in parts:
        p = p.reshape(-1)
        size = -(-p.shape[0] // PACK) * PACK
        offs.append(n)
        n += size
        padded.append(jnp.pad(p, (0, size - p.shape[0])) if size != p.shape[0] else p)
    return jnp.concatenate(padded), offs, n


def kernel(x, c, norm_g, ada_w, ada_b, sc_w_in, sc_conv_w, sc_w_out, lru_w_in, lru_conv_w, lru_conv_b, lru_w_a, lru_b_a, lru_w_x, lru_b_x, lru_lambda, lru_w_out, final_g, loss_target, m_norm_g, m_ada_w, m_ada_b, m_sc_w_in, m_sc_conv_w, m_sc_w_out, m_lru_w_in, m_lru_conv_w, m_lru_conv_b, m_lru_w_a, m_lru_b_a, m_lru_w_x, m_lru_b_x, m_lru_lambda, m_lru_w_out, m_final_g, v_norm_g, v_ada_w, v_ada_b, v_sc_w_in, v_sc_conv_w, v_sc_w_out, v_lru_w_in, v_lru_conv_w, v_lru_conv_b, v_lru_w_a, v_lru_b_a, v_lru_w_x, v_lru_b_x, v_lru_lambda, v_lru_w_out, v_final_g):
    xi, yi, ci = _pos()
    chip = 2 * xi + yi
    batch = 4 * xi + 2 * yi + ci
    core_op = jnp.reshape(ci, (1,)).astype(jnp.int32)
    chip_op = jnp.reshape(chip, (1,)).astype(jnp.int32)
    where_op = jnp.stack([chip, ci]).astype(jnp.int32)

    x2d, tgt = x[0], loss_target[0]
    s, d = x2d.shape
    es = sc_conv_w.shape[2]
    e = 4 * es
    n_heads = lru_w_a.shape[1]
    hj = lru_b_a.shape[2]
    f = ada_w.shape[2]
    row = lambda t: t.reshape(1, -1)

    small_parts = [c, sc_conv_w, lru_conv_w, lru_conv_b, lru_b_a, lru_b_x, lru_lambda]
    small, offs, n_small = _pack(small_parts)
    got = _allgather8([small.reshape(8, n_small // 8)], "ag_small")[0].reshape(8, n_small)
    c_all = got[:, :d]
    per_chip = got[0::2]

    def chip_part(k, shape):
        size = 1
        for dim in shape:
            size *= dim
        return per_chip[:, offs[k]:offs[k] + size].reshape((4,) + shape)

    conv_w0 = jnp.transpose(chip_part(1, (3, es)), (1, 0, 2)).reshape(3, e)
    conv_w1 = jnp.transpose(chip_part(2, (4, es)), (1, 0, 2)).reshape(4, e)
    conv_b1 = chip_part(3, (es,)).reshape(1, e)
    b_a = jnp.transpose(chip_part(4, (n_heads, hj)), (1, 0, 2)).reshape(n_heads, 1, 4 * hj)
    b_x = jnp.transpose(chip_part(5, (n_heads, hj)), (1, 0, 2)).reshape(n_heads, 1, 4 * hj)
    lam = chip_part(6, (es,)).reshape(1, e)

    mod_nb = _ada_fwd(c_all, ada_w, "ada_fwd")
    mods = _allgather8([mod_nb.reshape(16, f)], "ag_mod")[0].reshape(8, 2, 8, f)[0::2]
    mine = lax.dynamic_index_in_dim(mods, batch, axis=2, keepdims=False)
    mod = jnp.transpose(mine, (1, 0, 2)).reshape(2, 4 * f) + ada_b
    shift = [row(mod[l, :d]) for l in range(2)]
    scale = [row(mod[l, d:2 * d]) for l in range(2)]
    gate = [row(mod[l, 2 * d:]) for l in range(2)]
    ng = [row(norm_g[l]) for l in range(2)]

    shards = [sc_w_in[0], sc_w_out[0], lru_w_in[0], lru_w_a[0].reshape(n_heads * hj, HEAD_DIM),
              lru_w_x[0].reshape(n_heads * hj, HEAD_DIM), lru_w_out[0]]
    names = ["sc_w_in", "sc_w_out", "lru_w_in", "lru_w_a", "lru_w_x", "lru_w_out"]
    slots = [_cast_into_slot(chip_op, w, "cast_" + nm) for w, nm in zip(shards, names)]
    in_flight, started = _gather_start([[slots[0]], [slots[1]], slots[2:5], [slots[5]]], mod, "ag_start")
    h0 = _norm_mod_fwd(x2d, ng[0], scale[0], shift[0], "norm0", after=started)

    def arrived(g, after, tag):
        send1, recv1, bufs = in_flight[g]
        send2, recv2, bufs, passed = _gather_forward(send1, recv1, bufs, after, "ag_forward_" + tag)
        return _gather_finish(send2, recv2, bufs, passed, "ag_finish_" + tag)

    w_in0 = arrived(0, h0, "sc_w_in")[0].reshape(4, d, e)
    proj0 = _mm(h0[None], w_in0, "nn", BF16, "sc_in")
    w_out0 = arrived(1, proj0, "sc_w_out")[0].reshape(1, e, d)
    y0 = _sc_fwd(proj0, conv_w0, "sc_mix")
    o0 = _mm(y0[None], w_out0, "nn", F32, "sc_out")[0]
    x1, h1 = _norm_mod_fwd(x2d, ng[1], scale[1], shift[1], "norm1", o=o0, gate=gate[0])
    lru_ws = arrived(2, h1, "lru_w_in")
    w_in1 = lru_ws[0].reshape(4, d, e // 2)
    w_a = lru_ws[1].reshape(4, n_heads, hj, HEAD_DIM)
    w_x = lru_ws[2].reshape(4, n_heads, hj, HEAD_DIM)
    proj1 = _mm(h1[None], w_in1, "nn", BF16, "lru_in")
    pairs1 = proj1.reshape(2, 2, s, e // 2)
    y1 = _lru_fwd(pairs1, conv_w1, conv_b1, w_a, b_a, w_x, b_x, lam, "lru_mix")
    w_out1 = arrived(3, y1, "lru_w_out")[0].reshape(1, e, d)
    o1 = _mm(y1[None], w_out1, "nn", F32, "lru_out")[0]
    dx2, do1, acc_f = _final_loss(x1, o1, gate[1], row(final_g), tgt, "final_loss")

    def reduce_stage1(tag, grads):
        lands = [lax.empty((4, g.shape[1] // 2, g.shape[2]), F32) for g in grads]
        return _exchange_start("rs_sibling_start_" + tag, grads, lands, len(grads),
                               _plan_other_half_to_sibling, None)

    def reduce_stage2(tag, stage1, nms, after):
        send, recv, grads, lands, _ = stage1
        grads, lands = _exchange_wait("rs_sibling_wait_" + tag, send, recv, grads, lands,
                                      _plan_other_half_to_sibling, after)
        parts = [_sum_own_and_sibling(core_op, g, r1, "rs_sum1_" + nm) for g, r1, nm in zip(grads, lands, nms)]
        lands = [lax.empty((3,) + p.shape[1:], BF16) for p in parts]
        return _exchange_start("rs_chips_start_" + tag, parts, lands, 3 * len(parts),
                               _plan_partials_to_chips, None)

    def reduce_stage3(tag, stage2, nms, after):
        send, recv, parts, lands, _ = stage2
        parts, lands = _exchange_wait("rs_chips_wait_" + tag, send, recv, parts, lands,
                                      _plan_partials_to_chips, after)
        halves = [_sum_chips(where_op, p, r2, "rs_sum2_" + nm) for p, r2, nm in zip(parts, lands, nms)]
        return _exchange_start("rs_share_start_" + tag, [], halves, len(halves), _plan_share_half, None)

    def reduce_done(tag, stage3, after):
        send, recv, _, fulls, _ = stage3
        return _exchange_wait("rs_share_wait_" + tag, send, recv, [], fulls, _plan_share_half, after)[1]

    def head_major_to_chip_major(t):
        return jnp.transpose(t.reshape(n_heads, 4, hj, HEAD_DIM), (1, 0, 2, 3)).reshape(4, n_heads * hj, HEAD_DIM)

    big_state = dict(zip(names, zip(shards, [m_sc_w_in, m_sc_w_out, m_lru_w_in, m_lru_w_a, m_lru_w_x, m_lru_w_out],
                                    [v_sc_w_in, v_sc_w_out, v_lru_w_in, v_lru_w_a, v_lru_w_x, v_lru_w_out],
                                    [sc_w_in, sc_w_out, lru_w_in, lru_w_a, lru_w_x, lru_w_out])))
    big = {}

    def update(nms, fulls, after):
        for nm, g2 in zip(nms, fulls):
            w2, m4, v4, w4 = big_state[nm]
            outs = _adamw(w2, g2, m4.reshape(w2.shape), v4.reshape(w2.shape), "adamw_" + nm, after=after)
            big[nm] = tuple(t.reshape(w4.shape) for t in outs)
            after = outs[1]
        return after

    g_w_out1 = _mm(y1[None], do1[None], "tn", F32, "lru_out_dw", bm=512, bn=2048)
    dy1 = _mm(do1[None], w_out1, "nt", BF16, "lru_out_dx")[0]
    dpairs1, g_wa, g_wx, g_ba, g_bx, g_lam, g_cw1, g_cb1 = _lru_bwd(
        pairs1, dy1, conv_w1, conv_b1, w_a, b_a, w_x, b_x, lam, "lru_mix_bwd")
    dproj1 = dpairs1.reshape(4, s, e // 2)
    g_w_in1 = _mm(h1[None], dproj1, "tn", F32, "lru_in_dw", bm=512, bn=2048)
    lru_names = ["lru_w_out", "lru_w_a", "lru_w_x", "lru_w_in"]
    lru_rs = reduce_stage1("lru", [g_w_out1.reshape(4, es, d), head_major_to_chip_major(g_wa),
                                   head_major_to_chip_major(g_wx), g_w_in1])
    dh1 = _mm(dproj1, w_in1, "nt", F32, "lru_in_dx", after=lru_rs[4])[0]
    lru_rs = reduce_stage2("lru", lru_rs, lru_names, dh1)
    dx1, do0, acc1 = _norm_mod_bwd(dh1, x1, ng[1], scale[1], dx2, "norm1_bwd", o_prev=o0, gate_prev=gate[0],
                                   after=lru_rs[4])

    g_w_out0 = _mm(y0[None], do0[None], "tn", F32, "sc_out_dw", bm=512, bn=2048)
    out_rs = reduce_stage1("sc_out", [g_w_out0.reshape(4, es, d)])
    dy0 = _mm(do0[None], w_out0, "nt", BF16, "sc_out_dx", after=out_rs[4])[0]
    out_rs = reduce_stage2("sc_out", out_rs, ["sc_w_out"], dy0)
    dproj0, g_cw0 = _sc_bwd(proj0, dy0, conv_w0, "sc_mix_bwd", after=out_rs[4])
    g_w_in0 = _mm(h0[None], dproj0, "tn", F32, "sc_in_dw", bm=512, bn=2048)
    in_rs = reduce_stage1("sc_in", [g_w_in0])
    lru_rs = reduce_stage3("lru", lru_rs, lru_names, in_rs[4])
    in_rs = reduce_stage2("sc_in", in_rs, ["sc_w_in"], lru_rs[4])
    dh0 = _mm(dproj0, w_in0, "nt", F32, "sc_in_dx", after=in_rs[4])[0]
    grad_x, acc0 = _norm_mod_bwd(dh0, x2d, ng[0], scale[0], dx1, "norm0_bwd")
    out_rs = reduce_stage3("sc_out", out_rs, ["sc_w_out"], acc0)
    last = update(lru_names, reduce_done("lru", lru_rs, out_rs[4]), None)
    last = update(["sc_w_out"], reduce_done("sc_out", out_rs, last), None)
    in_rs = reduce_stage3("sc_in", in_rs, ["sc_w_in"], last)

    dmod = jnp.stack([jnp.concatenate([acc0[1], acc0[0], acc1[3]]),
                      jnp.concatenate([acc1[1], acc1[0], acc_f[1]])])
    part_list = [jnp.stack([acc0[2], acc1[2]]), acc_f[0], acc_f[2, :1], g_cw0, g_cw1, g_cb1, g_ba, g_bx,
                 g_lam, dmod]
    partials, poffs, n_part = _pack(part_list)
    every = _allgather8([partials.reshape(8, n_part // 8)], "ag_partials", after=in_rs[4])[0].reshape(8, n_part)
    total = _sum_rows8(every, "sum_partials")[0]

    def tot(k, shape):
        size = 1
        for dim in shape:
            size *= dim
        return total[poffs[k]:poffs[k] + size].reshape(shape)

    def my_cols(t, width):
        return lax.dynamic_slice_in_dim(t, chip * width, width, axis=t.ndim - 1)

    loss = tot(2, (1,))[0]
    g_norm_g, g_final_g, g_ada_b = tot(0, (2, d)), tot(1, (d,)), tot(9, (2, 3 * d))
    g_sc_conv_w = my_cols(tot(3, (3, e)), es)[None]
    g_lru_conv_w = my_cols(tot(4, (4, e)), es)[None]
    g_lru_conv_b = my_cols(tot(5, (1, e)), es)
    g_lru_b_a = my_cols(tot(6, (n_heads, 4 * hj)), hj)[None]
    g_lru_b_x = my_cols(tot(7, (n_heads, 4 * hj)), hj)[None]
    g_lru_lambda = my_cols(tot(8, (1, e)), es)

    dmod_all = every[:, poffs[9]:poffs[9] + 6 * d].reshape(8, 2, 3 * d)
    dmod_mine = jnp.transpose(my_cols(dmod_all, f), (1, 0, 2))
    ada = _ada_bwd_adamw(jnp.transpose(c_all), dmod_mine, ada_w, m_ada_w, v_ada_w, "ada_bwd_adamw", after=total)

    small_names = ["norm_g", "ada_b", "final_g", "sc_conv_w", "lru_conv_w", "lru_conv_b", "lru_b_a",
                   "lru_b_x", "lru_lambda"]
    small_w = [norm_g, ada_b, final_g, sc_conv_w, lru_conv_w, lru_conv_b, lru_b_a, lru_b_x, lru_lambda]
    small_g = [g_norm_g, g_ada_b, g_final_g, g_sc_conv_w, g_lru_conv_w, g_lru_conv_b, g_lru_b_a,
               g_lru_b_x, g_lru_lambda]
    small_m = [m_norm_g, m_ada_b, m_final_g, m_sc_conv_w, m_lru_conv_w, m_lru_conv_b, m_lru_b_a,
               m_lru_b_x, m_lru_lambda]
    small_v = [v_norm_g, v_ada_b, v_final_g, v_sc_conv_w, v_lru_conv_w, v_lru_conv_b, v_lru_b_a,
               v_lru_b_x, v_lru_lambda]
    pw, soffs, n_s = _pack(small_w)
    pg, pm, pv = _pack(small_g)[0], _pack(small_m)[0], _pack(small_v)[0]
    shape2 = (n_s // PACK, PACK)
    _, pd, pm2, pv2 = _adamw(pw.reshape(shape2), pg.reshape(shape2), pm.reshape(shape2), pv.reshape(shape2),
                             "adamw_small", after=ada[0])
    update(["sc_w_in"], reduce_done("sc_in", in_rs, pd), None)
    small = {}
    for k, (nm, w_) in enumerate(zip(small_names, small_w)):
        take = lambda t: t.reshape(-1)[soffs[k]:soffs[k] + w_.size].reshape(w_.shape)
        small[nm] = (small_g[k].reshape(w_.shape), take(pd), take(pm2), take(pv2))

    results = dict(small)
    results.update(big)
    results["ada_w"] = tuple(ada)
    order = ["norm_g", "ada_w", "ada_b", "sc_w_in", "sc_conv_w", "sc_w_out", "lru_w_in", "lru_conv_w",
             "lru_conv_b", "lru_w_a", "lru_b_a", "lru_w_x", "lru_b_x", "lru_lambda", "lru_w_out", "final_g"]
    out = [loss, grad_x[None]]
    for kind in range(4):
        out += [results[nm][kind] for nm in order]
    return tuple(out)
```

```python
import functools

import jax
import jax.numpy as jnp
from jax import lax
from jax.experimental import pallas as pl
from jax.experimental.pallas import tpu as pltpu

F32 = jnp.float32
BF16 = jnp.bfloat16
MESH = pl.DeviceIdType.MESH
ANY = pl.BlockSpec(memory_space=pl.ANY)

RMS_EPS = 1e-6
RGLRU_C = 8.0
HEAD_DIM = 256
ADAM_LR = 0.001
ADAM_B1 = 0.9
ADAM_B2 = 0.999
ADAM_EPS = 1e-08
ADAM_WD = 0.01
ADAM_STEP = 10
V7X_VMEM_LIMIT = 56 * 1024 * 1024
LANES = 128
SUBLANES = 8
PACK = SUBLANES * LANES


def _blk(dim, pref, unit=LANES):
    if dim <= pref:
        return dim
    b = (pref // unit) * unit
    while b > unit and dim % b:
        b -= unit
    assert dim % b == 0, (dim, pref, unit)
    return b


def _params(sem=None):
    return pltpu.CompilerParams(dimension_semantics=sem, vmem_limit_bytes=V7X_VMEM_LIMIT)


def _ordered(body, n_in, after):
    if after is None:
        return body, [], []

    def ordered_body(*refs):
        return body(*refs[:n_in], *refs[n_in + 1:])

    return ordered_body, [ANY], [after]


def _pos():
    return lax.axis_index("x"), lax.axis_index("y"), lax.axis_index("c")


def _other_chips(x, y):
    return [(1 - x, y), (x, 1 - y), (1 - x, 1 - y)]


def _allgather8(arrs, name, after=None):
    n_t = len(arrs)
    ms = [a.shape[0] for a in arrs]

    def gather(*refs):
        ins, outs = refs[:n_t], refs[n_t:2 * n_t]
        send_sems, recv_sems, local_sems = refs[2 * n_t:]
        x, y, c = _pos()
        me, sibling = (x, y, c), (x, y, 1 - c)
        chips = _other_chips(x, y)

        def rows(t, px, py, pc):
            return outs[t].at[pl.ds((4 * px + 2 * py + pc) * ms[t], ms[t])]

        def copy(t, k, block, to, src=None):
            return pltpu.make_async_remote_copy(
                src_ref=rows(t, *block) if src is None else src, dst_ref=rows(t, *block),
                send_sem=send_sems.at[7 * t + k], recv_sem=recv_sems.at[7 * t + k],
                device_id=to, device_id_type=MESH)

        mine, first, passed = [], [], []
        for t in range(n_t):
            src = ins[t]
            cp = pltpu.make_async_copy(src, rows(t, *me), local_sems.at[t])
            cp.start()
            mine.append(cp)
            sends = [copy(t, 0, me, sibling, src=src)]
            sends += [copy(t, 1 + j, me, (*chip, c), src=src) for j, chip in enumerate(chips)]
            for cp in sends:
                cp.start()
            first += sends
        for t in range(n_t):
            for j, chip in enumerate(chips):
                copy(t, 1 + j, (*chip, c), me).wait_recv()
                cp = copy(t, 4 + j, (*chip, c), sibling)
                cp.start()
                passed.append(cp)
        for t in range(n_t):
            copy(t, 0, sibling, me).wait_recv()
            for j, chip in enumerate(chips):
                copy(t, 4 + j, (*chip, 1 - c), me).wait_recv()
        for cp in first + passed:
            cp.wait_send()
        for cp in mine:
            cp.wait()

    body, extra_specs, extra = _ordered(gather, n_t, after)
    return pl.pallas_call(
        body, name=name,
        out_shape=[jax.ShapeDtypeStruct((8 * m, a.shape[1]), a.dtype) for m, a in zip(ms, arrs)],
        in_specs=[ANY] * n_t + extra_specs, out_specs=[ANY] * n_t,
        scratch_shapes=[pltpu.SemaphoreType.DMA((7 * n_t,)), pltpu.SemaphoreType.DMA((7 * n_t,)),
                        pltpu.SemaphoreType.DMA((n_t,))],
    )(*arrs, *extra)


HBM = pl.BlockSpec(memory_space=pltpu.HBM)
SEM = pl.BlockSpec(memory_space=pltpu.SEMAPHORE)
TOKEN = pl.BlockSpec(memory_space=pltpu.VMEM)
IN_FLIGHT = pltpu.CompilerParams(has_side_effects=pltpu.SideEffectType.DATAFLOW_SIDE_EFFECTING)


def _in_hbm(arrs):
    return [pltpu.with_memory_space_constraint(a, pltpu.HBM) for a in arrs]


def _shard_rows(ref, h, px, py, pc):
    return ref.at[pl.ds((4 * px + 2 * py + pc) * h, h)]


def _gather_start(groups, after, name):
    bufs = [b for grp in groups for b in grp]
    n_t, n_g = len(bufs), len(groups)

    def body(*refs):
        sems, thru = refs[n_t + 1:n_t + 1 + 2 * n_g], refs[n_t + 1 + 2 * n_g:2 * n_t + 1 + 2 * n_g]
        token = refs[-1]
        x, y, c = _pos()
        t = 0
        for g, grp in enumerate(groups):
            for i in range(len(grp)):
                h = bufs[t].shape[0] // 8
                rows = _shard_rows(thru[t], h, x, y, c)
                for j, chip in enumerate(_other_chips(x, y)):
                    pltpu.make_async_remote_copy(
                        src_ref=rows, dst_ref=rows, send_sem=sems[2 * g].at[3 * i + j],
                        recv_sem=sems[2 * g + 1].at[3 * i + j], device_id=(*chip, c),
                        device_id_type=MESH).start()
                t += 1
        token[...] = jnp.zeros_like(token)

    sem_shapes = []
    for grp in groups:
        sem_shapes += [pltpu.SemaphoreType.DMA((3 * len(grp),))] * 2
    out = pl.pallas_call(
        body, name=name,
        out_shape=sem_shapes + [pltpu.HBM(b.shape, b.dtype) for b in bufs] + [jax.ShapeDtypeStruct((8, LANES), F32)],
        in_specs=[HBM] * n_t + [ANY], out_specs=[SEM] * (2 * n_g) + [HBM] * n_t + [TOKEN],
        input_output_aliases={t: 2 * n_g + t for t in range(n_t)},
        compiler_params=IN_FLIGHT,
    )(*_in_hbm(bufs), after)
    sems, thru, token = out[:2 * n_g], out[2 * n_g:2 * n_g + n_t], out[-1]
    per_group, t = [], 0
    for g, grp in enumerate(groups):
        per_group.append((sems[2 * g], sems[2 * g + 1], thru[t:t + len(grp)]))
        t += len(grp)
    return per_group, token


def _gather_forward(send_sems, recv_sems, bufs, after, name, sources=(0, 1, 2)):
    n_t = len(bufs)

    def body(*refs):
        ins = refs[:n_t]
        send1, recv1 = refs[n_t], refs[n_t + 1]
        send2, recv2 = refs[n_t + 3], refs[n_t + 4]
        token = refs[-1]
        x, y, c = _pos()
        chips = _other_chips(x, y)
        for t in range(n_t):
            h = bufs[t].shape[0] // 8
            mine = _shard_rows(ins[t], h, x, y, c)
            for j in sources:
                chip = chips[j]
                landed = _shard_rows(ins[t], h, *chip, c)
                pltpu.make_async_remote_copy(
                    src_ref=mine, dst_ref=landed, send_sem=send1.at[3 * t + j], recv_sem=recv1.at[3 * t + j],
                    device_id=(*chip, c), device_id_type=MESH).wait_recv()
                pltpu.make_async_remote_copy(
                    src_ref=landed, dst_ref=landed, send_sem=send2.at[3 * t + j], recv_sem=recv2.at[3 * t + j],
                    device_id=(x, y, 1 - c), device_id_type=MESH).start()
        for t in range(n_t):
            h = bufs[t].shape[0] // 8
            mine = _shard_rows(ins[t], h, x, y, c)
            for j in sources:
                pltpu.make_async_remote_copy(
                    src_ref=mine, dst_ref=mine, send_sem=send1.at[3 * t + j], recv_sem=recv1.at[3 * t + j],
                    device_id=(*chips[j], c), device_id_type=MESH).wait_send()
        token[...] = jnp.zeros_like(token)

    out = pl.pallas_call(
        body, name=name,
        out_shape=[pltpu.SemaphoreType.DMA((3 * n_t,))] * 2 + [pltpu.HBM(b.shape, b.dtype) for b in bufs]
        + [jax.ShapeDtypeStruct((8, LANES), F32)],
        in_specs=[HBM] * n_t + [SEM, SEM, ANY], out_specs=[SEM, SEM] + [HBM] * n_t + [TOKEN],
        input_output_aliases={t: 2 + t for t in range(n_t)},
        compiler_params=IN_FLIGHT,
    )(*bufs, send_sems, recv_sems, after)
    return out[0], out[1], out[2:2 + n_t], out[-1]


def _gather_finish(send_sems, recv_sems, bufs, after, name, sources=(0, 1, 2)):
    n_t = len(bufs)

    def body(*refs):
        ins = refs[:n_t]
        send2, recv2 = refs[n_t], refs[n_t + 1]
        x, y, c = _pos()
        chips = _other_chips(x, y)
        for t in range(n_t):
            h = bufs[t].shape[0] // 8
            for j in sources:
                chip = chips[j]
                sent = _shard_rows(ins[t], h, *chip, c)
                got = _shard_rows(ins[t], h, *chip, 1 - c)
                cp = pltpu.make_async_remote_copy(
                    src_ref=sent, dst_ref=got, send_sem=send2.at[3 * t + j], recv_sem=recv2.at[3 * t + j],
                    device_id=(x, y, 1 - c), device_id_type=MESH)
                cp.wait_send()
                cp.wait_recv()

    return pl.pallas_call(
        body, name=name, out_shape=[pltpu.HBM(b.shape, b.dtype) for b in bufs],
        in_specs=[HBM] * n_t + [SEM, SEM, ANY], out_specs=[HBM] * n_t,
        input_output_aliases={t: t for t in range(n_t)},
        compiler_params=IN_FLIGHT,
    )(*bufs, send_sems, recv_sems, after)


def _exchange_start(name, srcs, lands, n_copies, plan, after):
    ns, nl = len(srcs), len(lands)
    extra = [] if after is None else [after]

    def body(*refs):
        base = ns + nl + len(extra)
        send_sems, recv_sems = refs[base], refs[base + 1]
        src_refs, land_refs = refs[base + 2:base + 2 + ns], refs[base + 2 + ns:base + 2 + ns + nl]
        token = refs[-1]
        x, y, c = _pos()
        copies = plan(src_refs, land_refs, x, y, c)
        assert len(copies) == n_copies
        for k, (src, dst, dev) in enumerate(copies):
            pltpu.make_async_remote_copy(
                src_ref=src, dst_ref=dst, send_sem=send_sems.at[k], recv_sem=recv_sems.at[k],
                device_id=dev, device_id_type=MESH).start()
        token[...] = jnp.zeros_like(token)

    out = pl.pallas_call(
        body, name=name,
        out_shape=[pltpu.SemaphoreType.DMA((n_copies,))] * 2
        + [pltpu.HBM(a.shape, a.dtype) for a in list(srcs) + list(lands)] + [jax.ShapeDtypeStruct((8, LANES), F32)],
        in_specs=[HBM] * (ns + nl) + [ANY] * len(extra), out_specs=[SEM, SEM] + [HBM] * (ns + nl) + [TOKEN],
        input_output_aliases={i: 2 + i for i in range(ns + nl)},
        compiler_params=IN_FLIGHT,
    )(*_in_hbm(list(srcs) + list(lands)), *extra)
    return out[0], out[1], out[2:2 + ns], out[2 + ns:2 + ns + nl], out[-1]


def _exchange_wait(name, send_sems, recv_sems, srcs, lands, plan, after):
    ns, nl = len(srcs), len(lands)

    def body(*refs):
        src_refs, land_refs = refs[:ns], refs[ns:ns + nl]
        send, recv = refs[ns + nl], refs[ns + nl + 1]
        x, y, c = _pos()
        for k, (src, dst, dev) in enumerate(plan(src_refs, land_refs, x, y, c)):
            cp = pltpu.make_async_remote_copy(
                src_ref=src, dst_ref=dst, send_sem=send.at[k], recv_sem=recv.at[k],
                device_id=dev, device_id_type=MESH)
            cp.wait_send()
            cp.wait_recv()

    out = pl.pallas_call(
        body, name=name, out_shape=[pltpu.HBM(a.shape, a.dtype) for a in list(srcs) + list(lands)],
        in_specs=[HBM] * (ns + nl) + [SEM, SEM, ANY], out_specs=[HBM] * (ns + nl),
        input_output_aliases={i: i for i in range(ns + nl)},
        compiler_params=IN_FLIGHT,
    )(*srcs, *lands, send_sems, recv_sems, after)
    return out[:ns], out[ns:]


def _plan_other_half_to_sibling(src_refs, land_refs, x, y, c):
    out = []
    for g_ref, r_ref in zip(src_refs, land_refs):
        h = g_ref.shape[1] // 2
        out.append((g_ref.at[:, pl.ds((1 - c) * h, h), :], r_ref, (x, y, 1 - c)))
    return out


def _plan_partials_to_chips(src_refs, land_refs, x, y, c):
    out = []
    for p_ref, r_ref in zip(src_refs, land_refs):
        for j, (px, py) in enumerate(_other_chips(x, y)):
            out.append((p_ref.at[2 * px + py], r_ref.at[j], (px, py, c)))
    return out


def _plan_share_half(src_refs, land_refs, x, y, c):
    out = []
    for f_ref in land_refs:
        h = f_ref.shape[0] // 2
        rows = f_ref.at[pl.ds(c * h, h)]
        out.append((rows, rows, (x, y, 1 - c)))
    return out


def _cast_into_slot(chip, w, name):
    r, c = w.shape
    br, bc = _blk(r, 512, 8), _blk(c, 2048)
    nb = r // br

    def body(chip_ref, w_ref, o_ref):
        o_ref[...] = w_ref[...].astype(BF16)

    grid_spec = pltpu.PrefetchScalarGridSpec(
        num_scalar_prefetch=1, grid=(nb, c // bc),
        in_specs=[pl.BlockSpec((br, bc), lambda i, j, chip_ref: (i, j))],
        out_specs=pl.BlockSpec((br, bc), lambda i, j, chip_ref: (chip_ref[0] * nb + i, j)))
    return pl.pallas_call(
        body, name=name, grid_spec=grid_spec,
        out_shape=jax.ShapeDtypeStruct((4 * r, c), BF16),
        compiler_params=_params(("parallel", "parallel")),
    )(chip, w)


def _sum_own_and_sibling(core, g, r1, name):
    _, r, c = g.shape
    h = r // 2
    br, bc = _blk(h, 256, 8), _blk(c, 2048)
    nb = h // br

    def body(core_ref, g_ref, r_ref, o_ref):
        o_ref[...] = (g_ref[...] + r_ref[...]).astype(BF16)

    grid_spec = pltpu.PrefetchScalarGridSpec(
        num_scalar_prefetch=1, grid=(4, nb, c // bc),
        in_specs=[pl.BlockSpec((None, br, bc), lambda k, i, j, core_ref: (k, core_ref[0] * nb + i, j)),
                  pl.BlockSpec((None, br, bc), lambda k, i, j, core_ref: (k, i, j))],
        out_specs=pl.BlockSpec((None, br, bc), lambda k, i, j, core_ref: (k, i, j)))
    return pl.pallas_call(
        body, name=name, grid_spec=grid_spec,
        out_shape=jax.ShapeDtypeStruct((4, h, c), BF16),
        compiler_params=_params(("parallel", "parallel", "parallel")),
    )(core, g, r1)


def _sum_chips(where, p, r2, name):
    _, h, c = p.shape
    br, bc = _blk(h, 256, 8), _blk(c, 2048)
    nb = h // br

    def body(where_ref, p_ref, r_ref, o_ref):
        acc = p_ref[...].astype(F32)
        for j in range(3):
            acc = acc + r_ref[j].astype(F32)
        o_ref[...] = acc

    grid_spec = pltpu.PrefetchScalarGridSpec(
        num_scalar_prefetch=1, grid=(nb, c // bc),
        in_specs=[pl.BlockSpec((None, br, bc), lambda i, j, where_ref: (where_ref[0], i, j)),
                  pl.BlockSpec((3, br, bc), lambda i, j, where_ref: (0, i, j))],
        out_specs=pl.BlockSpec((br, bc), lambda i, j, where_ref: (where_ref[1] * nb + i, j)))
    return pl.pallas_call(
        body, name=name, grid_spec=grid_spec,
        out_shape=jax.ShapeDtypeStruct((2 * h, c), F32),
        compiler_params=_params(("parallel", "parallel")),
    )(where, p, r2)


def _adamw_math(w, g, m, v):
    m2 = ADAM_B1 * m + (1.0 - ADAM_B1) * g
    v2 = ADAM_B2 * v + (1.0 - ADAM_B2) * (g * g)
    m_hat = m2 / (1.0 - ADAM_B1 ** ADAM_STEP)
    v_hat = v2 / (1.0 - ADAM_B2 ** ADAM_STEP)
    delta = -ADAM_LR * (m_hat / (jnp.sqrt(v_hat) + ADAM_EPS) + ADAM_WD * w)
    return delta, m2, v2


def _adamw(w, g, m, v, name, after=None):
    r, c = w.shape
    br, bc = _blk(r, 128, 8), _blk(c, 2048)

    def body(w_ref, g_ref, m_ref, v_ref, go_ref, d_ref, m2_ref, v2_ref):
        gv = g_ref[...]
        d, m2, v2 = _adamw_math(w_ref[...], gv, m_ref[...], v_ref[...])
        go_ref[...] = gv
        d_ref[...] = d
        m2_ref[...] = m2
        v2_ref[...] = v2

    spec = pl.BlockSpec((br, bc), lambda i, j: (i, j))
    body, extra_specs, extra = _ordered(body, 4, after)
    return pl.pallas_call(
        body, name=name, grid=(r // br, c // bc),
        in_specs=[spec] * 4 + extra_specs, out_specs=[spec] * 4,
        out_shape=[jax.ShapeDtypeStruct((r, c), F32)] * 4,
        compiler_params=_params(("parallel", "parallel")),
    )(w, g, m, v, *extra)


def _sum_rows8(g, name, after=None):
    n = g.shape[1]

    def body(g_ref, o_ref):
        acc = g_ref[0:1, :]
        for k in range(1, 8):
            acc = acc + g_ref[k:k + 1, :]
        o_ref[...] = acc

    body, extra_specs, extra = _ordered(body, 1, after)
    return pl.pallas_call(
        body, name=name, out_shape=jax.ShapeDtypeStruct((1, n), F32),
        in_specs=[pl.BlockSpec(memory_space=pltpu.VMEM)] + extra_specs,
        out_specs=pl.BlockSpec(memory_space=pltpu.VMEM),
        compiler_params=_params(),
    )(g, *extra)


def _mm(a, b, mode, out_dtype, name, bm=1024, bn=None, after=None):
    if mode == "nn":
        (_, m, k), (g, _, n) = a.shape, b.shape
    elif mode == "tn":
        (_, k, m), (g, _, n) = a.shape, b.shape
    else:
        (g, m, k), (_, n, _) = a.shape, b.shape
    if bn is None:
        bn = 1024 if k <= 2048 else 512
    bm, bn = _blk(m, bm), _blk(n, bn)

    if mode == "nt":
        def body(a_ref, b_ref, o_ref, acc_ref):
            part = lax.dot_general(a_ref[...], b_ref[...], (((1,), (1,)), ((), ())),
                                   preferred_element_type=F32)
            if g == 1:
                o_ref[...] = part.astype(out_dtype)
            else:
                gi = pl.program_id(2)

                @pl.when(gi == 0)
                def _():
                    acc_ref[...] = part

                @pl.when(gi > 0)
                def _():
                    acc_ref[...] += part

                @pl.when(gi == g - 1)
                def _():
                    o_ref[...] = acc_ref[...].astype(out_dtype)

        body, extra_specs, extra = _ordered(body, 2, after)
        return pl.pallas_call(
            body, name=name, grid=(m // bm, n // bn, g),
            in_specs=[pl.BlockSpec((None, bm, k), lambda i, j, gi: (gi, i, 0)),
                      pl.BlockSpec((None, bn, k), lambda i, j, gi: (gi, j, 0))] + extra_specs,
            out_specs=pl.BlockSpec((None, bm, bn), lambda i, j, gi: (0, i, j)),
            out_shape=jax.ShapeDtypeStruct((1, m, n), out_dtype),
            scratch_shapes=[pltpu.VMEM((bm, bn), F32)],
            compiler_params=_params(("parallel", "parallel", "arbitrary")),
        )(a, b, *extra)

    contract = (((1,), (0,)), ((), ())) if mode == "nn" else (((0,), (0,)), ((), ()))

    def body(a_ref, b_ref, o_ref):
        o_ref[...] = lax.dot_general(a_ref[...], b_ref[...], contract,
                                     preferred_element_type=F32).astype(out_dtype)

    a_spec = (pl.BlockSpec((None, bm, k), lambda i, gi, j: (0, i, 0)) if mode == "nn"
              else pl.BlockSpec((None, k, bm), lambda i, gi, j: (0, 0, i)))
    body, extra_specs, extra = _ordered(body, 2, after)
    return pl.pallas_call(
        body, name=name, grid=(m // bm, g, n // bn),
        in_specs=[a_spec, pl.BlockSpec((None, k, bn), lambda i, gi, j: (gi, 0, j))] + extra_specs,
        out_specs=pl.BlockSpec((None, bm, bn), lambda i, gi, j: (gi, i, j)),
        out_shape=jax.ShapeDtypeStruct((g, m, n), out_dtype),
        compiler_params=_params(("parallel", "parallel", "parallel")),
    )(a, b, *extra)


def _mm_plane(a, b, plane, name, planes_so_far=None, after=None, bm=1024, bn=1024):
    (m, k), (g, _, n) = a.shape, b.shape
    bm, bn = _blk(m, bm), _blk(n, bn)

    def body(plane_ref, a_ref, b_ref, *rest):
        rest[-1][...] = jnp.dot(a_ref[...], b_ref[...], preferred_element_type=F32).astype(BF16)

    extra_specs, extra, aliases = [], [], {}
    if planes_so_far is not None:
        extra_specs.append(ANY)
        extra.append(planes_so_far)
        aliases = {3: 0}
    if after is not None:
        extra_specs.append(ANY)
        extra.append(after)
    grid_spec = pltpu.PrefetchScalarGridSpec(
        num_scalar_prefetch=1, grid=(m // bm, n // bn),
        in_specs=[pl.BlockSpec((bm, k), lambda i, j, p: (i, 0)),
                  pl.BlockSpec((None, k, bn), lambda i, j, p: (p[0], 0, j))] + extra_specs,
        out_specs=pl.BlockSpec((None, bm, bn), lambda i, j, p: (p[0], i, j)))
    return pl.pallas_call(
        body, name=name, grid_spec=grid_spec,
        out_shape=jax.ShapeDtypeStruct((g, m, n), BF16), input_output_aliases=aliases,
        compiler_params=_params(("parallel", "parallel")),
    )(plane, a, b, *extra)


def _row_specs(br, d):
    return (pl.BlockSpec((br, d), lambda i: (i, 0)), pl.BlockSpec((1, d), lambda i: (0, 0)),
            pl.BlockSpec((8, d), lambda i: (0, 0)))


def _rstd(xv):
    return lax.rsqrt(jnp.mean(xv * xv, axis=-1, keepdims=True) + RMS_EPS)


def _colsum(v):
    return jnp.sum(v, axis=0, keepdims=True)


def _norm_mod_fwd(x, g, scale, shift, name, o=None, gate=None, after=None):
    s, d = x.shape
    br = _blk(s, 256, 8)
    has_res = o is not None
    row, vec, _ = _row_specs(br, d)

    def body(*refs):
        if has_res:
            x_ref, o_ref, gate_ref, g_ref, sc_ref, sh_ref, x1_ref, h_ref = refs
            xv = x_ref[...] + gate_ref[...] * o_ref[...]
            x1_ref[...] = xv
        else:
            x_ref, g_ref, sc_ref, sh_ref, h_ref = refs
            xv = x_ref[...]
        n = xv * _rstd(xv) * g_ref[...]
        h_ref[...] = (n * (1.0 + sc_ref[...]) + sh_ref[...]).astype(BF16)

    ins = [x] + ([o, gate] if has_res else []) + [g, scale, shift]
    in_specs = [row] + ([row, vec] if has_res else []) + [vec] * 3
    out_shape = ([jax.ShapeDtypeStruct((s, d), F32)] if has_res else []) + [jax.ShapeDtypeStruct((s, d), BF16)]
    body, extra_specs, extra = _ordered(body, len(ins), after)
    out = pl.pallas_call(
        body, name=name, grid=(s // br,), in_specs=in_specs + extra_specs, out_specs=[row] * len(out_shape),
        out_shape=out_shape, compiler_params=_params(("parallel",)),
    )(*ins, *extra)
    return out if has_res else out[0]


def _final_loss(x1, o1, gate1, final_g, tgt, name):
    s, d = x1.shape
    br = _blk(s, 256, 8)
    row, vec, acc = _row_specs(br, d)

    def body(x1_ref, o_ref, gate_ref, g_ref, t_ref, dx_ref, do_ref, acc_ref):
        @pl.when(pl.program_id(0) == 0)
        def _():
            acc_ref[...] = jnp.zeros_like(acc_ref)

        gate, o, g = gate_ref[...], o_ref[...], g_ref[...]
        x2 = x1_ref[...] + gate * o
        r = _rstd(x2)
        xh = x2 * r
        err = xh * g - t_ref[...]
        loss = 0.5 * _colsum(jnp.mean(err * err, axis=-1, keepdims=True))
        dout = err * (1.0 / d)
        dxh = dout * g
        dx2 = r * (dxh - xh * jnp.mean(dxh * xh, axis=-1, keepdims=True))
        dx_ref[...] = dx2
        do_ref[...] = (dx2 * gate).astype(BF16)
        acc_ref[0:1, :] += _colsum(dout * xh)
        acc_ref[1:2, :] += _colsum(dx2 * o)
        acc_ref[2:3, :] += jnp.broadcast_to(loss, (1, d))

    return pl.pallas_call(
        body, name=name, grid=(s // br,),
        in_specs=[row, row, vec, vec, row], out_specs=[row, row, acc],
        out_shape=[jax.ShapeDtypeStruct((s, d), F32), jax.ShapeDtypeStruct((s, d), BF16),
                   jax.ShapeDtypeStruct((8, d), F32)],
        compiler_params=_params(("arbitrary",)),
    )(x1, o1, gate1, final_g, tgt)


def _norm_mod_bwd(dh, x, g, scale, dx_next, name, o_prev=None, gate_prev=None, after=None):
    s, d = x.shape
    br = _blk(s, 256, 8)
    has_prev = o_prev is not None
    row, vec, acc = _row_specs(br, d)

    def body(*refs):
        if has_prev:
            dh_ref, x_ref, g_ref, sc_ref, dxn_ref, o_ref, gate_ref, dx_ref, do_ref, acc_ref = refs
        else:
            dh_ref, x_ref, g_ref, sc_ref, dxn_ref, dx_ref, acc_ref = refs

        @pl.when(pl.program_id(0) == 0)
        def _():
            acc_ref[...] = jnp.zeros_like(acc_ref)

        xv, gv, dhv = x_ref[...], g_ref[...], dh_ref[...]
        r = _rstd(xv)
        xh = xv * r
        acc_ref[0:1, :] += _colsum(dhv * (xh * gv))
        acc_ref[1:2, :] += _colsum(dhv)
        dn = dhv * (1.0 + sc_ref[...])
        acc_ref[2:3, :] += _colsum(dn * xh)
        dxh = dn * gv
        dx = dxn_ref[...] + r * (dxh - xh * jnp.mean(dxh * xh, axis=-1, keepdims=True))
        dx_ref[...] = dx
        if has_prev:
            acc_ref[3:4, :] += _colsum(dx * o_ref[...])
            do_ref[...] = (dx * gate_ref[...]).astype(BF16)

    ins = [dh, x, g, scale, dx_next] + ([o_prev, gate_prev] if has_prev else [])
    in_specs = [row, row, vec, vec, row] + ([row, vec] if has_prev else [])
    out_shape = [jax.ShapeDtypeStruct((s, d), F32)]
    out_specs = [row]
    if has_prev:
        out_shape.append(jax.ShapeDtypeStruct((s, d), BF16))
        out_specs.append(row)
    out_shape.append(jax.ShapeDtypeStruct((8, d), F32))
    out_specs.append(acc)
    body, extra_specs, extra = _ordered(body, len(ins), after)
    return pl.pallas_call(
        body, name=name, grid=(s // br,), in_specs=in_specs + extra_specs, out_specs=out_specs,
        out_shape=out_shape, compiler_params=_params(("arbitrary",)),
    )(*ins, *extra)


def _tiles(p):
    s, c = p.shape
    return p.reshape(s // SUBLANES, SUBLANES, c)


def _shift_down(p, k):
    if k == 0:
        return p
    r = pltpu.roll(_tiles(p), k, 1)
    before = jnp.concatenate([jnp.zeros_like(r[:1]), r[:-1]], axis=0)
    rows = lax.broadcasted_iota(jnp.int32, r.shape, 1)
    return jnp.where(rows >= k, r, before).reshape(p.shape)


def _shift_up(p, k):
    if k == 0:
        return p
    r = pltpu.roll(_tiles(p), SUBLANES - k, 1)
    after = jnp.concatenate([r[1:], jnp.zeros_like(r[:1])], axis=0)
    rows = lax.broadcasted_iota(jnp.int32, r.shape, 1)
    return jnp.where(rows < SUBLANES - k, r, after).reshape(p.shape)


def _sigmoid(z):
    return 0.5 * (jnp.tanh(0.5 * z) + 1.0)


def _sc_parts(proj_ref, w_ref):
    b, cg, v, g = (proj_ref[i].astype(F32) for i in range(4))
    p = cg * v
    u = w_ref[2:3, :] * p + w_ref[1:2, :] * _shift_down(p, 1) + w_ref[0:1, :] * _shift_down(p, 2)
    return b, cg, v, g, p, u


def _sc_fwd(proj, conv_w, name):
    _, s, e = proj.shape
    bc = _blk(e, 256)

    def body(proj_ref, w_ref, y_ref):
        b, _, _, g, _, u = _sc_parts(proj_ref, w_ref)
        y_ref[...] = (b * u * (g * _sigmoid(g))).astype(BF16)

    return pl.pallas_call(
        body, name=name, grid=(e // bc,),
        in_specs=[pl.BlockSpec((4, s, bc), lambda j: (0, 0, j)), pl.BlockSpec((3, bc), lambda j: (0, j))],
        out_specs=pl.BlockSpec((s, bc), lambda j: (0, j)),
        out_shape=jax.ShapeDtypeStruct((s, e), BF16),
        compiler_params=_params(("parallel",)),
    )(proj, conv_w)


def _sc_bwd(proj, dy, conv_w, name, after=None):
    _, s, e = proj.shape
    bc = _blk(e, 256)

    def body(proj_ref, dy_ref, w_ref, dp_ref, dw_ref):
        b, cg, v, g, p, u = _sc_parts(proj_ref, w_ref)
        dyv = dy_ref[...].astype(F32)
        sig = _sigmoid(g)
        t = dyv * (g * sig)
        du = t * b
        dp_ref[0] = (t * u).astype(BF16)
        dp_ref[3] = (dyv * b * u * (sig * (1.0 + g * (1.0 - sig)))).astype(BF16)
        dpp = w_ref[2:3, :] * du + w_ref[1:2, :] * _shift_up(du, 1) + w_ref[0:1, :] * _shift_up(du, 2)
        dp_ref[1] = (dpp * v).astype(BF16)
        dp_ref[2] = (dpp * cg).astype(BF16)
        dw_ref[2:3, :] = _colsum(du * p)
        dw_ref[1:2, :] = _colsum(du * _shift_down(p, 1))
        dw_ref[0:1, :] = _colsum(du * _shift_down(p, 2))

    body, extra_specs, extra = _ordered(body, 3, after)
    return pl.pallas_call(
        body, name=name, grid=(e // bc,),
        in_specs=[pl.BlockSpec((4, s, bc), lambda j: (0, 0, j)), pl.BlockSpec((s, bc), lambda j: (0, j)),
                  pl.BlockSpec((3, bc), lambda j: (0, j))] + extra_specs,
        out_specs=[pl.BlockSpec((4, s, bc), lambda j: (0, 0, j)), pl.BlockSpec((3, bc), lambda j: (0, j))],
        out_shape=[jax.ShapeDtypeStruct((4, s, e), BF16), jax.ShapeDtypeStruct((3, e), F32)],
        compiler_params=_params(("parallel",)),
    )(proj, dy, conv_w, *extra)


def _softplus_neg(lam):
    u = jnp.exp(-jnp.abs(lam))
    w = 1.0 + u
    log1p = jnp.where(w == 1.0, u, jnp.log(w) * (u / jnp.where(w == 1.0, 1.0, w - 1.0)))
    return jnp.maximum(-lam, 0.0) + log1p


def _one_minus_exp(z):
    series = -z * (1.0 + z * (0.5 + z * (1.0 / 6.0 + z * (1.0 / 24.0))))
    return jnp.where(z > -0.02, series, 1.0 - jnp.exp(z))


def _scan_in_tiles(a, b, reverse):
    shape = a.shape
    a, b = _tiles(a), _tiles(b)
    rows = lax.broadcasted_iota(jnp.int32, a.shape, 1)
    for step in (1, 2, 4):
        shift = SUBLANES - step if reverse else step
        ok = rows < SUBLANES - step if reverse else rows >= step
        a_s, b_s = pltpu.roll(a, shift, 1), pltpu.roll(b, shift, 1)
        b = jnp.where(ok, a * b_s + b, b)
        a = jnp.where(ok, a * a_s, a)
    return a.reshape(shape), b.reshape(shape)


def _scan(a, b, a_ref, b_ref, h_ref, reverse):
    s, c = a.shape
    n = s // 8
    a_t, b_t = _scan_in_tiles(a, b, reverse)
    a_ref[...] = a_t
    b_ref[...] = b_t

    def step(i, carry):
        gi = n - 1 - i if reverse else i
        sl = pl.ds(pl.multiple_of(gi * 8, 8), 8)
        h = b_ref[sl, :] + a_ref[sl, :] * carry
        h_ref[sl, :] = h
        return h[0:1, :] if reverse else h[7:8, :]

    lax.fori_loop(0, n, step, jnp.zeros((1, c), F32))
    return h_ref[...]


def _lru_specs(s, e_half, n_heads):
    hp = e_half // HEAD_DIM
    c = HEAD_DIM
    return dict(
        pair=pl.BlockSpec((2, None, s, c), lambda h: (0, h // hp, 0, h % hp)),
        conv_w=pl.BlockSpec((4, c), lambda h: (0, h)),
        chan=pl.BlockSpec((1, c), lambda h: (0, h)),
        w=pl.BlockSpec((4, None, c // 4, c), lambda h: (0, h, 0, 0)),
        bias=pl.BlockSpec((None, 1, c), lambda h: (h, 0, 0)),
        plane=pl.BlockSpec((s, c), lambda h: (0, h)),
    )


def _lru_forward_parts(vp, cw_ref, cb_ref, wa_ref, ba_ref, wx_ref, bx_ref, lam_ref):
    c = HEAD_DIM
    taps = [_shift_down(vp, 3 - k) for k in range(4)]
    v = cb_ref[...] + sum(cw_ref[k:k + 1, :] * taps[k] for k in range(4))
    vb = v.astype(BF16)
    wa = wa_ref[...].reshape(c, c)
    wx = wx_ref[...].reshape(c, c)
    r = _sigmoid(jnp.dot(vb, wa, preferred_element_type=F32) + ba_ref[...])
    i = _sigmoid(jnp.dot(vb, wx, preferred_element_type=F32) + bx_ref[...])
    sp = _softplus_neg(lam_ref[...])
    la = (-RGLRU_C) * sp * r
    a = jnp.exp(la)
    nm = jnp.sqrt(_one_minus_exp(2.0 * la))
    return taps, v, vb, wa, wx, r, i, sp, a, nm


def _lru_fwd(proj, conv_w, conv_b, w_a, b_a, w_x, b_x, lam, name):
    _, _, s, e_half = proj.shape
    n_heads = 2 * e_half // HEAD_DIM
    sp_ = _lru_specs(s, e_half, n_heads)

    def body(pg_ref, cw_ref, cb_ref, wa_ref, ba_ref, wx_ref, bx_ref, lam_ref, y_ref,
             sa_ref, sb_ref, sh_ref):
        _, v, _, _, _, _, i, _, a, nm = _lru_forward_parts(
            pg_ref[0].astype(F32), cw_ref, cb_ref, wa_ref, ba_ref, wx_ref, bx_ref, lam_ref)
        hs = _scan(a, nm * (i * v), sa_ref, sb_ref, sh_ref, reverse=False)
        g = pg_ref[1].astype(F32)
        y_ref[...] = (hs * (g * _sigmoid(g))).astype(BF16)

    return pl.pallas_call(
        body, name=name, grid=(n_heads,),
        in_specs=[sp_["pair"], sp_["conv_w"], sp_["chan"], sp_["w"], sp_["bias"], sp_["w"],
                  sp_["bias"], sp_["chan"]],
        out_specs=sp_["plane"],
        out_shape=jax.ShapeDtypeStruct((s, 2 * e_half), BF16),
        scratch_shapes=[pltpu.VMEM((s, HEAD_DIM), F32)] * 3,
        compiler_params=_params(("parallel",)),
    )(proj, conv_w, conv_b, w_a, b_a, w_x, b_x, lam)


def _lru_bwd(proj, dy, conv_w, conv_b, w_a, b_a, w_x, b_x, lam, name):
    _, _, s, e_half = proj.shape
    e = 2 * e_half
    c = HEAD_DIM
    n_heads = e // c
    hp = e_half // c
    sp_ = _lru_specs(s, e_half, n_heads)

    def body(pg_ref, dy_ref, cw_ref, cb_ref, wa_ref, ba_ref, wx_ref, bx_ref, lam_ref,
             dpg_ref, dwa_ref, dwx_ref, dba_ref, dbx_ref, dlam_ref, dcw_ref, dcb_ref,
             sa_ref, sb_ref, sh_ref, sd_ref):
        taps, v, vb, wa, wx, r, i, sp, a, nm = _lru_forward_parts(
            pg_ref[0].astype(F32), cw_ref, cb_ref, wa_ref, ba_ref, wx_ref, bx_ref, lam_ref)
        iv = i * v
        hs = _scan(a, nm * iv, sa_ref, sb_ref, sh_ref, reverse=False)
        g = pg_ref[1].astype(F32)
        dyv = dy_ref[...].astype(F32)
        sig = _sigmoid(g)
        dpg_ref[1] = (dyv * hs * (sig * (1.0 + g * (1.0 - sig)))).astype(BF16)
        dh = _scan(_shift_up(a, 1), dyv * (g * sig), sa_ref, sb_ref, sd_ref, reverse=True)
        da = dh * _shift_down(hs, 1)
        div = dh * nm
        a2 = a * a
        dla = da * a - (dh * iv) * (a2 / nm)
        dzr = (dla * ((-RGLRU_C) * sp)) * (r * (1.0 - r))
        dzi = (div * v) * (i * (1.0 - i))
        lam = lam_ref[...]
        dlam_ref[...] = _colsum(dla * ((-RGLRU_C) * r)) * (-_sigmoid(-lam))
        dba_ref[...] = _colsum(dzr)
        dbx_ref[...] = _colsum(dzi)
        dzr_b, dzi_b = dzr.astype(BF16), dzi.astype(BF16)
        tn = (((0,), (0,)), ((), ()))
        nt = (((1,), (1,)), ((), ()))
        dwa_ref[...] = lax.dot_general(vb, dzr_b, tn, preferred_element_type=F32)
        dwx_ref[...] = lax.dot_general(vb, dzi_b, tn, preferred_element_type=F32)
        dv = (div * i + lax.dot_general(dzr_b, wa, nt, preferred_element_type=F32)
              + lax.dot_general(dzi_b, wx, nt, preferred_element_type=F32))
        dcb_ref[...] = _colsum(dv)
        dvp = jnp.zeros_like(dv)
        for k in range(4):
            dvp = dvp + cw_ref[k:k + 1, :] * _shift_up(dv, 3 - k)
            dcw_ref[k:k + 1, :] = _colsum(dv * taps[k])
        dpg_ref[0] = dvp.astype(BF16)

    head_mat = pl.BlockSpec((None, c, c), lambda h: (h, 0, 0))
    outs = pl.pallas_call(
        body, name=name, grid=(n_heads,),
        in_specs=[sp_["pair"], sp_["plane"], sp_["conv_w"], sp_["chan"], sp_["w"], sp_["bias"],
                  sp_["w"], sp_["bias"], sp_["chan"]],
        out_specs=[sp_["pair"], head_mat, head_mat, sp_["bias"], sp_["bias"],
                   sp_["chan"], sp_["conv_w"], sp_["chan"]],
        out_shape=[jax.ShapeDtypeStruct((2, 2, s, e_half), BF16),
                   jax.ShapeDtypeStruct((n_heads, c, c), F32), jax.ShapeDtypeStruct((n_heads, c, c), F32),
                   jax.ShapeDtypeStruct((n_heads, 1, c), F32), jax.ShapeDtypeStruct((n_heads, 1, c), F32),
                   jax.ShapeDtypeStruct((1, e), F32), jax.ShapeDtypeStruct((4, e), F32),
                   jax.ShapeDtypeStruct((1, e), F32)],
        scratch_shapes=[pltpu.VMEM((s, c), F32)] * 4,
        compiler_params=_params(("parallel",)),
    )(proj, dy, conv_w, conv_b, w_a, b_a, w_x, b_x, lam)
    return tuple(outs)


def _ada_fwd(c_all, ada_w, name):
    n_l, d, f = ada_w.shape
    bf = _blk(f, 512)

    def body(c_ref, w_ref, o_ref):
        cv = c_ref[...]
        sc = (cv * _sigmoid(cv)).astype(BF16)
        o_ref[...] = jnp.dot(sc, w_ref[...].astype(BF16), preferred_element_type=F32)

    return pl.pallas_call(
        body, name=name, grid=(n_l, f // bf),
        in_specs=[pl.BlockSpec((8, d), lambda l, j: (0, 0)), pl.BlockSpec((None, d, bf), lambda l, j: (l, 0, j))],
        out_specs=pl.BlockSpec((None, 8, bf), lambda l, j: (l, 0, j)),
        out_shape=jax.ShapeDtypeStruct((n_l, 8, f), F32),
        compiler_params=_params(("parallel", "parallel")),
    )(c_all, ada_w)


def _ada_bwd_adamw(c_t, dmod, w, m, v, name, after=None):
    n_l, d, f = w.shape
    bf = _blk(f, 256)

    def body(c_ref, dm_ref, w_ref, m_ref, v_ref, g_ref, d_ref, m2_ref, v2_ref):
        cv = c_ref[...]
        sc = cv * _sigmoid(cv)
        dm = dm_ref[...]
        g = sc[:, 0:1] * dm[0:1, :]
        for b in range(1, 8):
            g = g + sc[:, b:b + 1] * dm[b:b + 1, :]
        g_ref[...] = g
        dl, m2, v2 = _adamw_math(w_ref[...], g, m_ref[...], v_ref[...])
        d_ref[...] = dl
        m2_ref[...] = m2
        v2_ref[...] = v2

    big = pl.BlockSpec((None, d, bf), lambda l, j: (l, 0, j))
    body, extra_specs, extra = _ordered(body, 5, after)
    return pl.pallas_call(
        body, name=name, grid=(n_l, f // bf),
        in_specs=[pl.BlockSpec((d, 8), lambda l, j: (0, 0)), pl.BlockSpec((None, 8, bf), lambda l, j: (l, 0, j)),
                  big, big, big] + extra_specs,
        out_specs=[big] * 4, out_shape=[jax.ShapeDtypeStruct((n_l, d, f), F32)] * 4,
        compiler_params=_params(("parallel", "parallel")),
    )(c_t, dmod, w, m, v, *extra)


def _pack(parts):
    padded, offs, n = [], [], 0
    for p in parts:
        p = p.reshape(-1)
        size = -(-p.shape[0] // PACK) * PACK
        offs.append(n)
        n += size
        padded.append(jnp.pad(p, (0, size - p.shape[0])) if size != p.shape[0] else p)
    return jnp.concatenate(padded), offs, n


def kernel(x, c, norm_g, ada_w, ada_b, sc_w_in, sc_conv_w, sc_w_out, lru_w_in, lru_conv_w, lru_conv_b, lru_w_a, lru_b_a, lru_w_x, lru_b_x, lru_lambda, lru_w_out, final_g, loss_target, m_norm_g, m_ada_w, m_ada_b, m_sc_w_in, m_sc_conv_w, m_sc_w_out, m_lru_w_in, m_lru_conv_w, m_lru_conv_b, m_lru_w_a, m_lru_b_a, m_lru_w_x, m_lru_b_x, m_lru_lambda, m_lru_w_out, m_final_g, v_norm_g, v_ada_w, v_ada_b, v_sc_w_in, v_sc_conv_w, v_sc_w_out, v_lru_w_in, v_lru_conv_w, v_lru_conv_b, v_lru_w_a, v_lru_b_a, v_lru_w_x, v_lru_b_x, v_lru_lambda, v_lru_w_out, v_final_g):
    xi, yi, ci = _pos()
    chip = 2 * xi + yi
    batch = 4 * xi + 2 * yi + ci
    core_op = jnp.reshape(ci, (1,)).astype(jnp.int32)
    chip_op = jnp.reshape(chip, (1,)).astype(jnp.int32)
    where_op = jnp.stack([chip, ci]).astype(jnp.int32)

    x2d, tgt = x[0], loss_target[0]
    s, d = x2d.shape
    es = sc_conv_w.shape[2]
    e = 4 * es
    n_heads = lru_w_a.shape[1]
    hj = lru_b_a.shape[2]
    f = ada_w.shape[2]
    row = lambda t: t.reshape(1, -1)

    small_parts = [c, sc_conv_w, lru_conv_w, lru_conv_b, lru_b_a, lru_b_x, lru_lambda]
    small, offs, n_small = _pack(small_parts)
    got = _allgather8([small.reshape(8, n_small // 8)], "ag_small")[0].reshape(8, n_small)
    c_all = got[:, :d]
    per_chip = got[0::2]

    def chip_part(k, shape):
        size = 1
        for dim in shape:
            size *= dim
        return per_chip[:, offs[k]:offs[k] + size].reshape((4,) + shape)

    conv_w0 = jnp.transpose(chip_part(1, (3, es)), (1, 0, 2)).reshape(3, e)
    conv_w1 = jnp.transpose(chip_part(2, (4, es)), (1, 0, 2)).reshape(4, e)
    conv_b1 = chip_part(3, (es,)).reshape(1, e)
    b_a = jnp.transpose(chip_part(4, (n_heads, hj)), (1, 0, 2)).reshape(n_heads, 1, 4 * hj)
    b_x = jnp.transpose(chip_part(5, (n_heads, hj)), (1, 0, 2)).reshape(n_heads, 1, 4 * hj)
    lam = chip_part(6, (es,)).reshape(1, e)

    mod_nb = _ada_fwd(c_all, ada_w, "ada_fwd")
    mods = _allgather8([mod_nb.reshape(16, f)], "ag_mod")[0].reshape(8, 2, 8, f)[0::2]
    mine = lax.dynamic_index_in_dim(mods, batch, axis=2, keepdims=False)
    mod = jnp.transpose(mine, (1, 0, 2)).reshape(2, 4 * f) + ada_b
    shift = [row(mod[l, :d]) for l in range(2)]
    scale = [row(mod[l, d:2 * d]) for l in range(2)]
    gate = [row(mod[l, 2 * d:]) for l in range(2)]
    ng = [row(norm_g[l]) for l in range(2)]

    shards = [sc_w_in[0], sc_w_out[0], lru_w_in[0], lru_w_a[0].reshape(n_heads * hj, HEAD_DIM),
              lru_w_x[0].reshape(n_heads * hj, HEAD_DIM), lru_w_out[0]]
    names = ["sc_w_in", "sc_w_out", "lru_w_in", "lru_w_a", "lru_w_x", "lru_w_out"]
    slots = [_cast_into_slot(chip_op, w, "cast_" + nm) for w, nm in zip(shards, names)]
    in_flight, started = _gather_start([[slots[0]], [slots[1]], slots[2:5], [slots[5]]], mod, "ag_start")
    h0 = _norm_mod_fwd(x2d, ng[0], scale[0], shift[0], "norm0", after=started)

    def arrived(g, after, tag):
        send1, recv1, bufs = in_flight[g]
        send2, recv2, bufs, passed = _gather_forward(send1, recv1, bufs, after, "ag_forward_" + tag)
        return _gather_finish(send2, recv2, bufs, passed, "ag_finish_" + tag)

    send1, recv1, bufs = in_flight[0]
    proj0 = _mm_plane(h0, bufs[0].reshape(4, d, e), chip_op, "sc_in_own")
    for j, (px, py) in enumerate(_other_chips(xi, yi)):
        send2, recv2, bufs, passed = _gather_forward(send1, recv1, bufs, proj0, "ag_forward_sc_w_in_%d" % j,
                                                     sources=(j,))
        bufs = _gather_finish(send2, recv2, bufs, passed, "ag_finish_sc_w_in_%d" % j, sources=(j,))
        plane = jnp.reshape(2 * px + py, (1,)).astype(jnp.int32)
        proj0 = _mm_plane(h0, bufs[0].reshape(4, d, e), plane, "sc_in_%d" % j, planes_so_far=proj0)
    w_in0 = bufs[0].reshape(4, d, e)
    w_out0 = arrived(1, proj0, "sc_w_out")[0].reshape(1, e, d)
    y0 = _sc_fwd(proj0, conv_w0, "sc_mix")
    o0 = _mm(y0[None], w_out0, "nn", F32, "sc_out")[0]
    x1, h1 = _norm_mod_fwd(x2d, ng[1], scale[1], shift[1], "norm1", o=o0, gate=gate[0])
    lru_ws = arrived(2, h1, "lru_w_in")
    w_in1 = lru_ws[0].reshape(4, d, e // 2)
    w_a = lru_ws[1].reshape(4, n_heads, hj, HEAD_DIM)
    w_x = lru_ws[2].reshape(4, n_heads, hj, HEAD_DIM)
    proj1 = _mm(h1[None], w_in1, "nn", BF16, "lru_in")
    pairs1 = proj1.reshape(2, 2, s, e // 2)
    y1 = _lru_fwd(pairs1, conv_w1, conv_b1, w_a, b_a, w_x, b_x, lam, "lru_mix")
    w_out1 = arrived(3, y1, "lru_w_out")[0].reshape(1, e, d)
    o1 = _mm(y1[None], w_out1, "nn", F32, "lru_out")[0]
    dx2, do1, acc_f = _final_loss(x1, o1, gate[1], row(final_g), tgt, "final_loss")

    def reduce_stage1(tag, grads):
        lands = [lax.empty((4, g.shape[1] // 2, g.shape[2]), F32) for g in grads]
        return _exchange_start("rs_sibling_start_" + tag, grads, lands, len(grads),
                               _plan_other_half_to_sibling, None)

    def reduce_stage2(tag, stage1, nms, after):
        send, recv, grads, lands, _ = stage1
        grads, lands = _exchange_wait("rs_sibling_wait_" + tag, send, recv, grads, lands,
                                      _plan_other_half_to_sibling, after)
        parts = [_sum_own_and_sibling(core_op, g, r1, "rs_sum1_" + nm) for g, r1, nm in zip(grads, lands, nms)]
        lands = [lax.empty((3,) + p.shape[1:], BF16) for p in parts]
        return _exchange_start("rs_chips_start_" + tag, parts, lands, 3 * len(parts),
                               _plan_partials_to_chips, None)

    def reduce_stage3(tag, stage2, nms, after):
        send, recv, parts, lands, _ = stage2
        parts, lands = _exchange_wait("rs_chips_wait_" + tag, send, recv, parts, lands,
                                      _plan_partials_to_chips, after)
        halves = [_sum_chips(where_op, p, r2, "rs_sum2_" + nm) for p, r2, nm in zip(parts, lands, nms)]
        return _exchange_start("rs_share_start_" + tag, [], halves, len(halves), _plan_share_half, None)

    def reduce_done(tag, stage3, after):
        send, recv, _, fulls, _ = stage3
        return _exchange_wait("rs_share_wait_" + tag, send, recv, [], fulls, _plan_share_half, after)[1]

    def head_major_to_chip_major(t):
        return jnp.transpose(t.reshape(n_heads, 4, hj, HEAD_DIM), (1, 0, 2, 3)).reshape(4, n_heads * hj, HEAD_DIM)

    big_state = dict(zip(names, zip(shards, [m_sc_w_in, m_sc_w_out, m_lru_w_in, m_lru_w_a, m_lru_w_x, m_lru_w_out],
                                    [v_sc_w_in, v_sc_w_out, v_lru_w_in, v_lru_w_a, v_lru_w_x, v_lru_w_out],
                                    [sc_w_in, sc_w_out, lru_w_in, lru_w_a, lru_w_x, lru_w_out])))
    big = {}

    def update(nms, fulls, after):
        for nm, g2 in zip(nms, fulls):
            w2, m4, v4, w4 = big_state[nm]
            outs = _adamw(w2, g2, m4.reshape(w2.shape), v4.reshape(w2.shape), "adamw_" + nm, after=after)
            big[nm] = tuple(t.reshape(w4.shape) for t in outs)
            after = outs[1]
        return after

    g_w_out1 = _mm(y1[None], do1[None], "tn", F32, "lru_out_dw", bm=512, bn=2048)
    dy1 = _mm(do1[None], w_out1, "nt", BF16, "lru_out_dx")[0]
    dpairs1, g_wa, g_wx, g_ba, g_bx, g_lam, g_cw1, g_cb1 = _lru_bwd(
        pairs1, dy1, conv_w1, conv_b1, w_a, b_a, w_x, b_x, lam, "lru_mix_bwd")
    dproj1 = dpairs1.reshape(4, s, e // 2)
    g_w_in1 = _mm(h1[None], dproj1, "tn", F32, "lru_in_dw", bm=512, bn=2048)
    lru_names = ["lru_w_out", "lru_w_a", "lru_w_x", "lru_w_in"]
    lru_rs = reduce_stage1("lru", [g_w_out1.reshape(4, es, d), head_major_to_chip_major(g_wa),
                                   head_major_to_chip_major(g_wx), g_w_in1])
    dh1 = _mm(dproj1, w_in1, "nt", F32, "lru_in_dx", after=lru_rs[4])[0]
    lru_rs = reduce_stage2("lru", lru_rs, lru_names, dh1)
    dx1, do0, acc1 = _norm_mod_bwd(dh1, x1, ng[1], scale[1], dx2, "norm1_bwd", o_prev=o0, gate_prev=gate[0],
                                   after=lru_rs[4])

    g_w_out0 = _mm(y0[None], do0[None], "tn", F32, "sc_out_dw", bm=512, bn=2048)
    out_rs = reduce_stage1("sc_out", [g_w_out0.reshape(4, es, d)])
    dy0 = _mm(do0[None], w_out0, "nt", BF16, "sc_out_dx", after=out_rs[4])[0]
    out_rs = reduce_stage2("sc_out", out_rs, ["sc_w_out"], dy0)
    dproj0, g_cw0 = _sc_bwd(proj0, dy0, conv_w0, "sc_mix_bwd", after=out_rs[4])
    g_w_in0 = _mm(h0[None], dproj0, "tn", F32, "sc_in_dw", bm=512, bn=2048)
    in_rs = reduce_stage1("sc_in", [g_w_in0])
    lru_rs = reduce_stage3("lru", lru_rs, lru_names, in_rs[4])
    in_rs = reduce_stage2("sc_in", in_rs, ["sc_w_in"], lru_rs[4])
    dh0 = _mm(dproj0, w_in0, "nt", F32, "sc_in_dx", after=in_rs[4])[0]
    grad_x, acc0 = _norm_mod_bwd(dh0, x2d, ng[0], scale[0], dx1, "norm0_bwd")
    out_rs = reduce_stage3("sc_out", out_rs, ["sc_w_out"], acc0)
    last = update(lru_names, reduce_done("lru", lru_rs, out_rs[4]), None)
    last = update(["sc_w_out"], reduce_done("sc_out", out_rs, last), None)
    in_rs = reduce_stage3("sc_in", in_rs, ["sc_w_in"], last)

    dmod = jnp.stack([jnp.concatenate([acc0[1], acc0[0], acc1[3]]),
                      jnp.concatenate([acc1[1], acc1[0], acc_f[1]])])
    part_list = [jnp.stack([acc0[2], acc1[2]]), acc_f[0], acc_f[2, :1], g_cw0, g_cw1, g_cb1, g_ba, g_bx,
                 g_lam, dmod]
    partials, poffs, n_part = _pack(part_list)
    every = _allgather8([partials.reshape(8, n_part // 8)], "ag_partials", after=in_rs[4])[0].reshape(8, n_part)
    total = _sum_rows8(every, "sum_partials")[0]

    def tot(k, shape):
        size = 1
        for dim in shape:
            size *= dim
        return total[poffs[k]:poffs[k] + size].reshape(shape)

    def my_cols(t, width):
        return lax.dynamic_slice_in_dim(t, chip * width, width, axis=t.ndim - 1)

    loss = tot(2, (1,))[0]
    g_norm_g, g_final_g, g_ada_b = tot(0, (2, d)), tot(1, (d,)), tot(9, (2, 3 * d))
    g_sc_conv_w = my_cols(tot(3, (3, e)), es)[None]
    g_lru_conv_w = my_cols(tot(4, (4, e)), es)[None]
    g_lru_conv_b = my_cols(tot(5, (1, e)), es)
    g_lru_b_a = my_cols(tot(6, (n_heads, 4 * hj)), hj)[None]
    g_lru_b_x = my_cols(tot(7, (n_heads, 4 * hj)), hj)[None]
    g_lru_lambda = my_cols(tot(8, (1, e)), es)

    dmod_all = every[:, poffs[9]:poffs[9] + 6 * d].reshape(8, 2, 3 * d)
    dmod_mine = jnp.transpose(my_cols(dmod_all, f), (1, 0, 2))
    ada = _ada_bwd_adamw(jnp.transpose(c_all), dmod_mine, ada_w, m_ada_w, v_ada_w, "ada_bwd_adamw", after=total)

    small_names = ["norm_g", "ada_b", "final_g", "sc_conv_w", "lru_conv_w", "lru_conv_b", "lru_b_a",
                   "lru_b_x", "lru_lambda"]
    small_w = [norm_g, ada_b, final_g, sc_conv_w, lru_conv_w, lru_conv_b, lru_b_a, lru_b_x, lru_lambda]
    small_g = [g_norm_g, g_ada_b, g_final_g, g_sc_conv_w, g_lru_conv_w, g_lru_conv_b, g_lru_b_a,
               g_lru_b_x, g_lru_lambda]
    small_m = [m_norm_g, m_ada_b, m_final_g, m_sc_conv_w, m_lru_conv_w, m_lru_conv_b, m_lru_b_a,
               m_lru_b_x, m_lru_lambda]
    small_v = [v_norm_g, v_ada_b, v_final_g, v_sc_conv_w, v_lru_conv_w, v_lru_conv_b, v_lru_b_a,
               v_lru_b_x, v_lru_lambda]
    pw, soffs, n_s = _pack(small_w)
    pg, pm, pv = _pack(small_g)[0], _pack(small_m)[0], _pack(small_v)[0]
    shape2 = (n_s // PACK, PACK)
    _, pd, pm2, pv2 = _adamw(pw.reshape(shape2), pg.reshape(shape2), pm.reshape(shape2), pv.reshape(shape2),
                             "adamw_small", after=ada[0])
    update(["sc_w_in"], reduce_done("sc_in", in_rs, pd), None)
    small = {}
    for k, (nm, w_) in enumerate(zip(small_names, small_w)):
        take = lambda t: t.reshape(-1)[soffs[k]:soffs[k] + w_.size].reshape(w_.shape)
        small[nm] = (small_g[k].reshape(w_.shape), take(pd), take(pm2), take(pv2))

    results = dict(small)
    results.update(big)
    results["ada_w"] = tuple(ada)
    order = ["norm_g", "ada_w", "ada_b", "sc_w_in", "sc_conv_w", "sc_w_out", "lru_w_in", "lru_conv_w",
             "lru_conv_b", "lru_w_a", "lru_b_a", "lru_w_x", "lru_b_x", "lru_lambda", "lru_w_out", "final_g"]
    out = [loss, grad_x[None]]
    for kind in range(4):
        out += [results[nm][kind] for nm in order]
    return tuple(out)
```

```python
import functools

import jax
import jax.numpy as jnp
from jax import lax
from jax.experimental import pallas as pl
from jax.experimental.pallas import tpu as pltpu

F32 = jnp.float32
BF16 = jnp.bfloat16
MESH = pl.DeviceIdType.MESH
ANY = pl.BlockSpec(memory_space=pl.ANY)

RMS_EPS = 1e-6
RGLRU_C = 8.0
HEAD_DIM = 256
ADAM_LR = 0.001
ADAM_B1 = 0.9
ADAM_B2 = 0.999
ADAM_EPS = 1e-08
ADAM_WD = 0.01
ADAM_STEP = 10
V7X_VMEM_LIMIT = 56 * 1024 * 1024
IN_PIECES = 4
LANES = 128
SUBLANES = 8
PACK = SUBLANES * LANES


def _blk(dim, pref, unit=LANES):
    if dim <= pref:
        return dim
    b = (pref // unit) * unit
    while b > unit and dim % b:
        b -= unit
    assert dim % b == 0, (dim, pref, unit)
    return b


def _params(sem=None):
    return pltpu.CompilerParams(dimension_semantics=sem, vmem_limit_bytes=V7X_VMEM_LIMIT)


def _ordered(body, n_in, after):
    if after is None:
        return body, [], []

    def ordered_body(*refs):
        return body(*refs[:n_in], *refs[n_in + 1:])

    return ordered_body, [ANY], [after]


def _pos():
    return lax.axis_index("x"), lax.axis_index("y"), lax.axis_index("c")


def _other_chips(x, y):
    return [(1 - x, y), (x, 1 - y), (1 - x, 1 - y)]


def _allgather8(arrs, name, after=None):
    n_t = len(arrs)
    ms = [a.shape[0] for a in arrs]

    def gather(*refs):
        ins, outs = refs[:n_t], refs[n_t:2 * n_t]
        send_sems, recv_sems, local_sems = refs[2 * n_t:]
        x, y, c = _pos()
        me, sibling = (x, y, c), (x, y, 1 - c)
        chips = _other_chips(x, y)

        def rows(t, px, py, pc):
            return outs[t].at[pl.ds((4 * px + 2 * py + pc) * ms[t], ms[t])]

        def copy(t, k, block, to, src=None):
            return pltpu.make_async_remote_copy(
                src_ref=rows(t, *block) if src is None else src, dst_ref=rows(t, *block),
                send_sem=send_sems.at[7 * t + k], recv_sem=recv_sems.at[7 * t + k],
                device_id=to, device_id_type=MESH)

        mine, first, passed = [], [], []
        for t in range(n_t):
            src = ins[t]
            cp = pltpu.make_async_copy(src, rows(t, *me), local_sems.at[t])
            cp.start()
            mine.append(cp)
            sends = [copy(t, 0, me, sibling, src=src)]
            sends += [copy(t, 1 + j, me, (*chip, c), src=src) for j, chip in enumerate(chips)]
            for cp in sends:
                cp.start()
            first += sends
        for t in range(n_t):
            for j, chip in enumerate(chips):
                copy(t, 1 + j, (*chip, c), me).wait_recv()
                cp = copy(t, 4 + j, (*chip, c), sibling)
                cp.start()
                passed.append(cp)
        for t in range(n_t):
            copy(t, 0, sibling, me).wait_recv()
            for j, chip in enumerate(chips):
                copy(t, 4 + j, (*chip, 1 - c), me).wait_recv()
        for cp in first + passed:
            cp.wait_send()
        for cp in mine:
            cp.wait()

    body, extra_specs, extra = _ordered(gather, n_t, after)
    return pl.pallas_call(
        body, name=name,
        out_shape=[jax.ShapeDtypeStruct((8 * m, a.shape[1]), a.dtype) for m, a in zip(ms, arrs)],
        in_specs=[ANY] * n_t + extra_specs, out_specs=[ANY] * n_t,
        scratch_shapes=[pltpu.SemaphoreType.DMA((7 * n_t,)), pltpu.SemaphoreType.DMA((7 * n_t,)),
                        pltpu.SemaphoreType.DMA((n_t,))],
    )(*arrs, *extra)


HBM = pl.BlockSpec(memory_space=pltpu.HBM)
SEM = pl.BlockSpec(memory_space=pltpu.SEMAPHORE)
TOKEN = pl.BlockSpec(memory_space=pltpu.VMEM)
IN_FLIGHT = pltpu.CompilerParams(has_side_effects=pltpu.SideEffectType.DATAFLOW_SIDE_EFFECTING)


def _in_hbm(arrs):
    return [pltpu.with_memory_space_constraint(a, pltpu.HBM) for a in arrs]


def _shard_rows(ref, h, px, py, pc):
    return ref.at[pl.ds((4 * px + 2 * py + pc) * h, h)]


def _gather_start(groups, after, name):
    bufs = [b for grp in groups for b in grp]
    n_t, n_g = len(bufs), len(groups)

    def body(*refs):
        sems, thru = refs[n_t + 1:n_t + 1 + 2 * n_g], refs[n_t + 1 + 2 * n_g:2 * n_t + 1 + 2 * n_g]
        token = refs[-1]
        x, y, c = _pos()
        t = 0
        for g, grp in enumerate(groups):
            for i in range(len(grp)):
                h = bufs[t].shape[0] // 8
                rows = _shard_rows(thru[t], h, x, y, c)
                for j, chip in enumerate(_other_chips(x, y)):
                    pltpu.make_async_remote_copy(
                        src_ref=rows, dst_ref=rows, send_sem=sems[2 * g].at[3 * i + j],
                        recv_sem=sems[2 * g + 1].at[3 * i + j], device_id=(*chip, c),
                        device_id_type=MESH).start()
                t += 1
        token[...] = jnp.zeros_like(token)

    sem_shapes = []
    for grp in groups:
        sem_shapes += [pltpu.SemaphoreType.DMA((3 * len(grp),))] * 2
    out = pl.pallas_call(
        body, name=name,
        out_shape=sem_shapes + [pltpu.HBM(b.shape, b.dtype) for b in bufs] + [jax.ShapeDtypeStruct((8, LANES), F32)],
        in_specs=[HBM] * n_t + [ANY], out_specs=[SEM] * (2 * n_g) + [HBM] * n_t + [TOKEN],
        input_output_aliases={t: 2 * n_g + t for t in range(n_t)},
        compiler_params=IN_FLIGHT,
    )(*_in_hbm(bufs), after)
    sems, thru, token = out[:2 * n_g], out[2 * n_g:2 * n_g + n_t], out[-1]
    per_group, t = [], 0
    for g, grp in enumerate(groups):
        per_group.append((sems[2 * g], sems[2 * g + 1], thru[t:t + len(grp)]))
        t += len(grp)
    return per_group, token


def _gather_forward(send_sems, recv_sems, bufs, after, name, sources=(0, 1, 2)):
    n_t = len(bufs)

    def body(*refs):
        ins = refs[:n_t]
        send1, recv1 = refs[n_t], refs[n_t + 1]
        send2, recv2 = refs[n_t + 3], refs[n_t + 4]
        token = refs[-1]
        x, y, c = _pos()
        chips = _other_chips(x, y)
        for t in range(n_t):
            h = bufs[t].shape[0] // 8
            mine = _shard_rows(ins[t], h, x, y, c)
            for j in sources:
                chip = chips[j]
                landed = _shard_rows(ins[t], h, *chip, c)
                pltpu.make_async_remote_copy(
                    src_ref=mine, dst_ref=landed, send_sem=send1.at[3 * t + j], recv_sem=recv1.at[3 * t + j],
                    device_id=(*chip, c), device_id_type=MESH).wait_recv()
                pltpu.make_async_remote_copy(
                    src_ref=landed, dst_ref=landed, send_sem=send2.at[3 * t + j], recv_sem=recv2.at[3 * t + j],
                    device_id=(x, y, 1 - c), device_id_type=MESH).start()
        for t in range(n_t):
            h = bufs[t].shape[0] // 8
            mine = _shard_rows(ins[t], h, x, y, c)
            for j in sources:
                pltpu.make_async_remote_copy(
                    src_ref=mine, dst_ref=mine, send_sem=send1.at[3 * t + j], recv_sem=recv1.at[3 * t + j],
                    device_id=(*chips[j], c), device_id_type=MESH).wait_send()
        token[...] = jnp.zeros_like(token)

    out = pl.pallas_call(
        body, name=name,
        out_shape=[pltpu.SemaphoreType.DMA((3 * n_t,))] * 2 + [pltpu.HBM(b.shape, b.dtype) for b in bufs]
        + [jax.ShapeDtypeStruct((8, LANES), F32)],
        in_specs=[HBM] * n_t + [SEM, SEM, ANY], out_specs=[SEM, SEM] + [HBM] * n_t + [TOKEN],
        input_output_aliases={t: 2 + t for t in range(n_t)},
        compiler_params=IN_FLIGHT,
    )(*bufs, send_sems, recv_sems, after)
    return out[0], out[1], out[2:2 + n_t], out[-1]


def _gather_finish(send_sems, recv_sems, bufs, after, name, sources=(0, 1, 2)):
    n_t = len(bufs)

    def body(*refs):
        ins = refs[:n_t]
        send2, recv2 = refs[n_t], refs[n_t + 1]
        x, y, c = _pos()
        chips = _other_chips(x, y)
        for t in range(n_t):
            h = bufs[t].shape[0] // 8
            for j in sources:
                chip = chips[j]
                sent = _shard_rows(ins[t], h, *chip, c)
                got = _shard_rows(ins[t], h, *chip, 1 - c)
                cp = pltpu.make_async_remote_copy(
                    src_ref=sent, dst_ref=got, send_sem=send2.at[3 * t + j], recv_sem=recv2.at[3 * t + j],
                    device_id=(x, y, 1 - c), device_id_type=MESH)
                cp.wait_send()
                cp.wait_recv()

    return pl.pallas_call(
        body, name=name, out_shape=[pltpu.HBM(b.shape, b.dtype) for b in bufs],
        in_specs=[HBM] * n_t + [SEM, SEM, ANY], out_specs=[HBM] * n_t,
        input_output_aliases={t: t for t in range(n_t)},
        compiler_params=IN_FLIGHT,
    )(*bufs, send_sems, recv_sems, after)


def _exchange_start(name, srcs, lands, n_copies, plan, after):
    ns, nl = len(srcs), len(lands)
    extra = [] if after is None else [after]

    def body(*refs):
        base = ns + nl + len(extra)
        send_sems, recv_sems = refs[base], refs[base + 1]
        src_refs, land_refs = refs[base + 2:base + 2 + ns], refs[base + 2 + ns:base + 2 + ns + nl]
        token = refs[-1]
        x, y, c = _pos()
        copies = plan(src_refs, land_refs, x, y, c)
        assert len(copies) == n_copies
        for k, (src, dst, dev) in enumerate(copies):
            pltpu.make_async_remote_copy(
                src_ref=src, dst_ref=dst, send_sem=send_sems.at[k], recv_sem=recv_sems.at[k],
                device_id=dev, device_id_type=MESH).start()
        token[...] = jnp.zeros_like(token)

    out = pl.pallas_call(
        body, name=name,
        out_shape=[pltpu.SemaphoreType.DMA((n_copies,))] * 2
        + [pltpu.HBM(a.shape, a.dtype) for a in list(srcs) + list(lands)] + [jax.ShapeDtypeStruct((8, LANES), F32)],
        in_specs=[HBM] * (ns + nl) + [ANY] * len(extra), out_specs=[SEM, SEM] + [HBM] * (ns + nl) + [TOKEN],
        input_output_aliases={i: 2 + i for i in range(ns + nl)},
        compiler_params=IN_FLIGHT,
    )(*_in_hbm(list(srcs) + list(lands)), *extra)
    return out[0], out[1], out[2:2 + ns], out[2 + ns:2 + ns + nl], out[-1]


def _exchange_wait(name, send_sems, recv_sems, srcs, lands, plan, after):
    ns, nl = len(srcs), len(lands)

    def body(*refs):
        src_refs, land_refs = refs[:ns], refs[ns:ns + nl]
        send, recv = refs[ns + nl], refs[ns + nl + 1]
        x, y, c = _pos()
        for k, (src, dst, dev) in enumerate(plan(src_refs, land_refs, x, y, c)):
            cp = pltpu.make_async_remote_copy(
                src_ref=src, dst_ref=dst, send_sem=send.at[k], recv_sem=recv.at[k],
                device_id=dev, device_id_type=MESH)
            cp.wait_send()
            cp.wait_recv()

    out = pl.pallas_call(
        body, name=name, out_shape=[pltpu.HBM(a.shape, a.dtype) for a in list(srcs) + list(lands)],
        in_specs=[HBM] * (ns + nl) + [SEM, SEM, ANY], out_specs=[HBM] * (ns + nl),
        input_output_aliases={i: i for i in range(ns + nl)},
        compiler_params=IN_FLIGHT,
    )(*srcs, *lands, send_sems, recv_sems, after)
    return out[:ns], out[ns:]


def _plan_other_half_to_sibling(src_refs, land_refs, x, y, c):
    out = []
    for g_ref, r_ref in zip(src_refs, land_refs):
        h = g_ref.shape[1] // 2
        out.append((g_ref.at[:, pl.ds((1 - c) * h, h), :], r_ref, (x, y, 1 - c)))
    return out


def _plan_partials_to_chips(src_refs, land_refs, x, y, c):
    out = []
    for p_ref, r_ref in zip(src_refs, land_refs):
        for j, (px, py) in enumerate(_other_chips(x, y)):
            out.append((p_ref.at[2 * px + py], r_ref.at[j], (px, py, c)))
    return out


def _plan_share_half(src_refs, land_refs, x, y, c):
    out = []
    for f_ref in land_refs:
        h = f_ref.shape[0] // 2
        rows = f_ref.at[pl.ds(c * h, h)]
        out.append((rows, rows, (x, y, 1 - c)))
    return out


def _cast_into_slot(chip, w, name):
    r, c = w.shape
    br, bc = _blk(r, 512, 8), _blk(c, 2048)
    nb = r // br

    def body(chip_ref, w_ref, o_ref):
        o_ref[...] = w_ref[...].astype(BF16)

    grid_spec = pltpu.PrefetchScalarGridSpec(
        num_scalar_prefetch=1, grid=(nb, c // bc),
        in_specs=[pl.BlockSpec((br, bc), lambda i, j, chip_ref: (i, j))],
        out_specs=pl.BlockSpec((br, bc), lambda i, j, chip_ref: (chip_ref[0] * nb + i, j)))
    return pl.pallas_call(
        body, name=name, grid_spec=grid_spec,
        out_shape=jax.ShapeDtypeStruct((4 * r, c), BF16),
        compiler_params=_params(("parallel", "parallel")),
    )(chip, w)


def _sum_own_and_sibling(core, g, r1, name):
    _, r, c = g.shape
    h = r // 2
    br, bc = _blk(h, 256, 8), _blk(c, 2048)
    nb = h // br

    def body(core_ref, g_ref, r_ref, o_ref):
        o_ref[...] = (g_ref[...] + r_ref[...]).astype(BF16)

    grid_spec = pltpu.PrefetchScalarGridSpec(
        num_scalar_prefetch=1, grid=(4, nb, c // bc),
        in_specs=[pl.BlockSpec((None, br, bc), lambda k, i, j, core_ref: (k, core_ref[0] * nb + i, j)),
                  pl.BlockSpec((None, br, bc), lambda k, i, j, core_ref: (k, i, j))],
        out_specs=pl.BlockSpec((None, br, bc), lambda k, i, j, core_ref: (k, i, j)))
    return pl.pallas_call(
        body, name=name, grid_spec=grid_spec,
        out_shape=jax.ShapeDtypeStruct((4, h, c), BF16),
        compiler_params=_params(("parallel", "parallel", "parallel")),
    )(core, g, r1)


def _sum_chips(where, p, r2, name, piece=(0, 1), so_far=None):
    _, h, c = p.shape
    k, n = piece
    br, bc = _blk(h, 256, 8), _blk(c, 2048)
    nb, ncb = h // br, c // bc

    def body(where_ref, p_ref, r_ref, *rest):
        acc = p_ref[...].astype(F32)
        for j in range(3):
            acc = acc + r_ref[j].astype(F32)
        rest[-1][...] = acc

    extra_specs, extra, aliases = ([], [], {}) if so_far is None else ([ANY], [so_far], {3: 0})
    grid_spec = pltpu.PrefetchScalarGridSpec(
        num_scalar_prefetch=1, grid=(nb, ncb),
        in_specs=[pl.BlockSpec((None, br, bc), lambda i, j, where_ref: (where_ref[0], i, j)),
                  pl.BlockSpec((3, br, bc), lambda i, j, where_ref: (0, i, j))] + extra_specs,
        out_specs=pl.BlockSpec((br, bc), lambda i, j, where_ref: (where_ref[1] * nb + i, k * ncb + j)))
    return pl.pallas_call(
        body, name=name, grid_spec=grid_spec,
        out_shape=jax.ShapeDtypeStruct((2 * h, n * c), F32), input_output_aliases=aliases,
        compiler_params=_params(("parallel", "parallel")),
    )(where, p, r2, *extra)


def _adamw_math(w, g, m, v):
    m2 = ADAM_B1 * m + (1.0 - ADAM_B1) * g
    v2 = ADAM_B2 * v + (1.0 - ADAM_B2) * (g * g)
    m_hat = m2 / (1.0 - ADAM_B1 ** ADAM_STEP)
    v_hat = v2 / (1.0 - ADAM_B2 ** ADAM_STEP)
    delta = -ADAM_LR * (m_hat / (jnp.sqrt(v_hat) + ADAM_EPS) + ADAM_WD * w)
    return delta, m2, v2


def _adamw(w, g, m, v, name, after=None):
    r, c = w.shape
    br, bc = _blk(r, 128, 8), _blk(c, 2048)

    def body(w_ref, g_ref, m_ref, v_ref, go_ref, d_ref, m2_ref, v2_ref):
        gv = g_ref[...]
        d, m2, v2 = _adamw_math(w_ref[...], gv, m_ref[...], v_ref[...])
        go_ref[...] = gv
        d_ref[...] = d
        m2_ref[...] = m2
        v2_ref[...] = v2

    spec = pl.BlockSpec((br, bc), lambda i, j: (i, j))
    body, extra_specs, extra = _ordered(body, 4, after)
    return pl.pallas_call(
        body, name=name, grid=(r // br, c // bc),
        in_specs=[spec] * 4 + extra_specs, out_specs=[spec] * 4,
        out_shape=[jax.ShapeDtypeStruct((r, c), F32)] * 4,
        compiler_params=_params(("parallel", "parallel")),
    )(w, g, m, v, *extra)


def _sum_rows8(g, name, after=None):
    n = g.shape[1]

    def body(g_ref, o_ref):
        acc = g_ref[0:1, :]
        for k in range(1, 8):
            acc = acc + g_ref[k:k + 1, :]
        o_ref[...] = acc

    body, extra_specs, extra = _ordered(body, 1, after)
    return pl.pallas_call(
        body, name=name, out_shape=jax.ShapeDtypeStruct((1, n), F32),
        in_specs=[pl.BlockSpec(memory_space=pltpu.VMEM)] + extra_specs,
        out_specs=pl.BlockSpec(memory_space=pltpu.VMEM),
        compiler_params=_params(),
    )(g, *extra)


def _mm(a, b, mode, out_dtype, name, bm=1024, bn=None, after=None, col_blocks=None):
    if mode == "nn":
        (_, m, k), (g, _, n) = a.shape, b.shape
    elif mode == "tn":
        (_, k, m), (g, _, n) = a.shape, b.shape
    else:
        (g, m, k), (_, n, _) = a.shape, b.shape
    if bn is None:
        bn = 1024 if k <= 2048 else 512
    bm, bn = _blk(m, bm), _blk(n, bn)

    if mode == "nt":
        def body(a_ref, b_ref, o_ref, acc_ref):
            part = lax.dot_general(a_ref[...], b_ref[...], (((1,), (1,)), ((), ())),
                                   preferred_element_type=F32)
            if g == 1:
                o_ref[...] = part.astype(out_dtype)
            else:
                gi = pl.program_id(2)

                @pl.when(gi == 0)
                def _():
                    acc_ref[...] = part

                @pl.when(gi > 0)
                def _():
                    acc_ref[...] += part

                @pl.when(gi == g - 1)
                def _():
                    o_ref[...] = acc_ref[...].astype(out_dtype)

        body, extra_specs, extra = _ordered(body, 2, after)
        return pl.pallas_call(
            body, name=name, grid=(m // bm, n // bn, g),
            in_specs=[pl.BlockSpec((None, bm, k), lambda i, j, gi: (gi, i, 0)),
                      pl.BlockSpec((None, bn, k), lambda i, j, gi: (gi, j, 0))] + extra_specs,
            out_specs=pl.BlockSpec((None, bm, bn), lambda i, j, gi: (0, i, j)),
            out_shape=jax.ShapeDtypeStruct((1, m, n), out_dtype),
            scratch_shapes=[pltpu.VMEM((bm, bn), F32)],
            compiler_params=_params(("parallel", "parallel", "arbitrary")),
        )(a, b, *extra)

    contract = (((1,), (0,)), ((), ())) if mode == "nn" else (((0,), (0,)), ((), ()))

    def body(a_ref, b_ref, o_ref):
        o_ref[...] = lax.dot_general(a_ref[...], b_ref[...], contract,
                                     preferred_element_type=F32).astype(out_dtype)

    a_spec = (pl.BlockSpec((None, bm, k), lambda i, gi, j: (0, i, 0)) if mode == "nn"
              else pl.BlockSpec((None, k, bm), lambda i, gi, j: (0, 0, i)))
    first, count = (0, n // bn) if col_blocks is None else col_blocks
    body, extra_specs, extra = _ordered(body, 2, after)
    return pl.pallas_call(
        body, name=name, grid=(m // bm, g, count),
        in_specs=[a_spec, pl.BlockSpec((None, k, bn), lambda i, gi, j: (gi, 0, first + j))] + extra_specs,
        out_specs=pl.BlockSpec((None, bm, bn), lambda i, gi, j: (gi, i, j)),
        out_shape=jax.ShapeDtypeStruct((g, m, count * bn), out_dtype),
        compiler_params=_params(("parallel", "parallel", "parallel")),
    )(a, b, *extra)


def _mm_plane(a, b, plane, name, planes_so_far=None, after=None, bm=1024, bn=1024):
    (m, k), (g, _, n) = a.shape, b.shape
    bm, bn = _blk(m, bm), _blk(n, bn)

    def body(plane_ref, a_ref, b_ref, *rest):
        rest[-1][...] = jnp.dot(a_ref[...], b_ref[...], preferred_element_type=F32).astype(BF16)

    extra_specs, extra, aliases = [], [], {}
    if planes_so_far is not None:
        extra_specs.append(ANY)
        extra.append(planes_so_far)
        aliases = {3: 0}
    if after is not None:
        extra_specs.append(ANY)
        extra.append(after)
    grid_spec = pltpu.PrefetchScalarGridSpec(
        num_scalar_prefetch=1, grid=(m // bm, n // bn),
        in_specs=[pl.BlockSpec((bm, k), lambda i, j, p: (i, 0)),
                  pl.BlockSpec((None, k, bn), lambda i, j, p: (p[0], 0, j))] + extra_specs,
        out_specs=pl.BlockSpec((None, bm, bn), lambda i, j, p: (p[0], i, j)))
    return pl.pallas_call(
        body, name=name, grid_spec=grid_spec,
        out_shape=jax.ShapeDtypeStruct((g, m, n), BF16), input_output_aliases=aliases,
        compiler_params=_params(("parallel", "parallel")),
    )(plane, a, b, *extra)


def _row_specs(br, d):
    return (pl.BlockSpec((br, d), lambda i: (i, 0)), pl.BlockSpec((1, d), lambda i: (0, 0)),
            pl.BlockSpec((8, d), lambda i: (0, 0)))


def _rstd(xv):
    return lax.rsqrt(jnp.mean(xv * xv, axis=-1, keepdims=True) + RMS_EPS)


def _colsum(v):
    return jnp.sum(v, axis=0, keepdims=True)


def _norm_mod_fwd(x, g, scale, shift, name, o=None, gate=None, after=None):
    s, d = x.shape
    br = _blk(s, 256, 8)
    has_res = o is not None
    row, vec, _ = _row_specs(br, d)

    def body(*refs):
        if has_res:
            x_ref, o_ref, gate_ref, g_ref, sc_ref, sh_ref, x1_ref, h_ref = refs
            xv = x_ref[...] + gate_ref[...] * o_ref[...]
            x1_ref[...] = xv
        else:
            x_ref, g_ref, sc_ref, sh_ref, h_ref = refs
            xv = x_ref[...]
        n = xv * _rstd(xv) * g_ref[...]
        h_ref[...] = (n * (1.0 + sc_ref[...]) + sh_ref[...]).astype(BF16)

    ins = [x] + ([o, gate] if has_res else []) + [g, scale, shift]
    in_specs = [row] + ([row, vec] if has_res else []) + [vec] * 3
    out_shape = ([jax.ShapeDtypeStruct((s, d), F32)] if has_res else []) + [jax.ShapeDtypeStruct((s, d), BF16)]
    body, extra_specs, extra = _ordered(body, len(ins), after)
    out = pl.pallas_call(
        body, name=name, grid=(s // br,), in_specs=in_specs + extra_specs, out_specs=[row] * len(out_shape),
        out_shape=out_shape, compiler_params=_params(("parallel",)),
    )(*ins, *extra)
    return out if has_res else out[0]


def _final_loss(x1, o1, gate1, final_g, tgt, name):
    s, d = x1.shape
    br = _blk(s, 256, 8)
    row, vec, acc = _row_specs(br, d)

    def body(x1_ref, o_ref, gate_ref, g_ref, t_ref, dx_ref, do_ref, acc_ref):
        @pl.when(pl.program_id(0) == 0)
        def _():
            acc_ref[...] = jnp.zeros_like(acc_ref)

        gate, o, g = gate_ref[...], o_ref[...], g_ref[...]
        x2 = x1_ref[...] + gate * o
        r = _rstd(x2)
        xh = x2 * r
        err = xh * g - t_ref[...]
        loss = 0.5 * _colsum(jnp.mean(err * err, axis=-1, keepdims=True))
        dout = err * (1.0 / d)
        dxh = dout * g
        dx2 = r * (dxh - xh * jnp.mean(dxh * xh, axis=-1, keepdims=True))
        dx_ref[...] = dx2
        do_ref[...] = (dx2 * gate).astype(BF16)
        acc_ref[0:1, :] += _colsum(dout * xh)
        acc_ref[1:2, :] += _colsum(dx2 * o)
        acc_ref[2:3, :] += jnp.broadcast_to(loss, (1, d))

    return pl.pallas_call(
        body, name=name, grid=(s // br,),
        in_specs=[row, row, vec, vec, row], out_specs=[row, row, acc],
        out_shape=[jax.ShapeDtypeStruct((s, d), F32), jax.ShapeDtypeStruct((s, d), BF16),
                   jax.ShapeDtypeStruct((8, d), F32)],
        compiler_params=_params(("arbitrary",)),
    )(x1, o1, gate1, final_g, tgt)


def _norm_mod_bwd(dh, x, g, scale, dx_next, name, o_prev=None, gate_prev=None, after=None):
    s, d = x.shape
    br = _blk(s, 256, 8)
    has_prev = o_prev is not None
    row, vec, acc = _row_specs(br, d)

    def body(*refs):
        if has_prev:
            dh_ref, x_ref, g_ref, sc_ref, dxn_ref, o_ref, gate_ref, dx_ref, do_ref, acc_ref = refs
        else:
            dh_ref, x_ref, g_ref, sc_ref, dxn_ref, dx_ref, acc_ref = refs

        @pl.when(pl.program_id(0) == 0)
        def _():
            acc_ref[...] = jnp.zeros_like(acc_ref)

        xv, gv, dhv = x_ref[...], g_ref[...], dh_ref[...]
        r = _rstd(xv)
        xh = xv * r
        acc_ref[0:1, :] += _colsum(dhv * (xh * gv))
        acc_ref[1:2, :] += _colsum(dhv)
        dn = dhv * (1.0 + sc_ref[...])
        acc_ref[2:3, :] += _colsum(dn * xh)
        dxh = dn * gv
        dx = dxn_ref[...] + r * (dxh - xh * jnp.mean(dxh * xh, axis=-1, keepdims=True))
        dx_ref[...] = dx
        if has_prev:
            acc_ref[3:4, :] += _colsum(dx * o_ref[...])
            do_ref[...] = (dx * gate_ref[...]).astype(BF16)

    ins = [dh, x, g, scale, dx_next] + ([o_prev, gate_prev] if has_prev else [])
    in_specs = [row, row, vec, vec, row] + ([row, vec] if has_prev else [])
    out_shape = [jax.ShapeDtypeStruct((s, d), F32)]
    out_specs = [row]
    if has_prev:
        out_shape.append(jax.ShapeDtypeStruct((s, d), BF16))
        out_specs.append(row)
    out_shape.append(jax.ShapeDtypeStruct((8, d), F32))
    out_specs.append(acc)
    body, extra_specs, extra = _ordered(body, len(ins), after)
    return pl.pallas_call(
        body, name=name, grid=(s // br,), in_specs=in_specs + extra_specs, out_specs=out_specs,
        out_shape=out_shape, compiler_params=_params(("arbitrary",)),
    )(*ins, *extra)


def _tiles(p):
    s, c = p.shape
    return p.reshape(s // SUBLANES, SUBLANES, c)


def _shift_down(p, k):
    if k == 0:
        return p
    r = pltpu.roll(_tiles(p), k, 1)
    before = jnp.concatenate([jnp.zeros_like(r[:1]), r[:-1]], axis=0)
    rows = lax.broadcasted_iota(jnp.int32, r.shape, 1)
    return jnp.where(rows >= k, r, before).reshape(p.shape)


def _shift_up(p, k):
    if k == 0:
        return p
    r = pltpu.roll(_tiles(p), SUBLANES - k, 1)
    after = jnp.concatenate([r[1:], jnp.zeros_like(r[:1])], axis=0)
    rows = lax.broadcasted_iota(jnp.int32, r.shape, 1)
    return jnp.where(rows < SUBLANES - k, r, after).reshape(p.shape)


def _sigmoid(z):
    return 0.5 * (jnp.tanh(0.5 * z) + 1.0)


def _sc_parts(proj_ref, w_ref):
    b, cg, v, g = (proj_ref[i].astype(F32) for i in range(4))
    p = cg * v
    u = w_ref[2:3, :] * p + w_ref[1:2, :] * _shift_down(p, 1) + w_ref[0:1, :] * _shift_down(p, 2)
    return b, cg, v, g, p, u


def _sc_fwd(proj, conv_w, name):
    _, s, e = proj.shape
    bc = _blk(e, 256)

    def body(proj_ref, w_ref, y_ref):
        b, _, _, g, _, u = _sc_parts(proj_ref, w_ref)
        y_ref[...] = (b * u * (g * _sigmoid(g))).astype(BF16)

    return pl.pallas_call(
        body, name=name, grid=(e // bc,),
        in_specs=[pl.BlockSpec((4, s, bc), lambda j: (0, 0, j)), pl.BlockSpec((3, bc), lambda j: (0, j))],
        out_specs=pl.BlockSpec((s, bc), lambda j: (0, j)),
        out_shape=jax.ShapeDtypeStruct((s, e), BF16),
        compiler_params=_params(("parallel",)),
    )(proj, conv_w)


def _sc_bwd(proj, dy, conv_w, name, after=None):
    _, s, e = proj.shape
    bc = _blk(e, 256)

    def body(proj_ref, dy_ref, w_ref, dp_ref, dw_ref):
        b, cg, v, g, p, u = _sc_parts(proj_ref, w_ref)
        dyv = dy_ref[...].astype(F32)
        sig = _sigmoid(g)
        t = dyv * (g * sig)
        du = t * b
        dp_ref[0] = (t * u).astype(BF16)
        dp_ref[3] = (dyv * b * u * (sig * (1.0 + g * (1.0 - sig)))).astype(BF16)
        dpp = w_ref[2:3, :] * du + w_ref[1:2, :] * _shift_up(du, 1) + w_ref[0:1, :] * _shift_up(du, 2)
        dp_ref[1] = (dpp * v).astype(BF16)
        dp_ref[2] = (dpp * cg).astype(BF16)
        dw_ref[2:3, :] = _colsum(du * p)
        dw_ref[1:2, :] = _colsum(du * _shift_down(p, 1))
        dw_ref[0:1, :] = _colsum(du * _shift_down(p, 2))

    body, extra_specs, extra = _ordered(body, 3, after)
    return pl.pallas_call(
        body, name=name, grid=(e // bc,),
        in_specs=[pl.BlockSpec((4, s, bc), lambda j: (0, 0, j)), pl.BlockSpec((s, bc), lambda j: (0, j)),
                  pl.BlockSpec((3, bc), lambda j: (0, j))] + extra_specs,
        out_specs=[pl.BlockSpec((4, s, bc), lambda j: (0, 0, j)), pl.BlockSpec((3, bc), lambda j: (0, j))],
        out_shape=[jax.ShapeDtypeStruct((4, s, e), BF16), jax.ShapeDtypeStruct((3, e), F32)],
        compiler_params=_params(("parallel",)),
    )(proj, dy, conv_w, *extra)


def _softplus_neg(lam):
    u = jnp.exp(-jnp.abs(lam))
    w = 1.0 + u
    log1p = jnp.where(w == 1.0, u, jnp.log(w) * (u / jnp.where(w == 1.0, 1.0, w - 1.0)))
    return jnp.maximum(-lam, 0.0) + log1p


def _one_minus_exp(z):
    series = -z * (1.0 + z * (0.5 + z * (1.0 / 6.0 + z * (1.0 / 24.0))))
    return jnp.where(z > -0.02, series, 1.0 - jnp.exp(z))


def _scan_in_tiles(a, b, reverse):
    shape = a.shape
    a, b = _tiles(a), _tiles(b)
    rows = lax.broadcasted_iota(jnp.int32, a.shape, 1)
    for step in (1, 2, 4):
        shift = SUBLANES - step if reverse else step
        ok = rows < SUBLANES - step if reverse else rows >= step
        a_s, b_s = pltpu.roll(a, shift, 1), pltpu.roll(b, shift, 1)
        b = jnp.where(ok, a * b_s + b, b)
        a = jnp.where(ok, a * a_s, a)
    return a.reshape(shape), b.reshape(shape)


def _scan(a, b, a_ref, b_ref, h_ref, reverse):
    s, c = a.shape
    n = s // 8
    a_t, b_t = _scan_in_tiles(a, b, reverse)
    a_ref[...] = a_t
    b_ref[...] = b_t

    def step(i, carry):
        gi = n - 1 - i if reverse else i
        sl = pl.ds(pl.multiple_of(gi * 8, 8), 8)
        h = b_ref[sl, :] + a_ref[sl, :] * carry
        h_ref[sl, :] = h
        return h[0:1, :] if reverse else h[7:8, :]

    lax.fori_loop(0, n, step, jnp.zeros((1, c), F32))
    return h_ref[...]


def _lru_specs(s, e_half, n_heads):
    hp = e_half // HEAD_DIM
    c = HEAD_DIM
    return dict(
        pair=pl.BlockSpec((2, None, s, c), lambda h: (0, h // hp, 0, h % hp)),
        conv_w=pl.BlockSpec((4, c), lambda h: (0, h)),
        chan=pl.BlockSpec((1, c), lambda h: (0, h)),
        w=pl.BlockSpec((4, None, c // 4, c), lambda h: (0, h, 0, 0)),
        bias=pl.BlockSpec((None, 1, c), lambda h: (h, 0, 0)),
        plane=pl.BlockSpec((s, c), lambda h: (0, h)),
    )


def _lru_forward_parts(vp, cw_ref, cb_ref, wa_ref, ba_ref, wx_ref, bx_ref, lam_ref):
    c = HEAD_DIM
    taps = [_shift_down(vp, 3 - k) for k in range(4)]
    v = cb_ref[...] + sum(cw_ref[k:k + 1, :] * taps[k] for k in range(4))
    vb = v.astype(BF16)
    wa = wa_ref[...].reshape(c, c)
    wx = wx_ref[...].reshape(c, c)
    r = _sigmoid(jnp.dot(vb, wa, preferred_element_type=F32) + ba_ref[...])
    i = _sigmoid(jnp.dot(vb, wx, preferred_element_type=F32) + bx_ref[...])
    sp = _softplus_neg(lam_ref[...])
    la = (-RGLRU_C) * sp * r
    a = jnp.exp(la)
    nm = jnp.sqrt(_one_minus_exp(2.0 * la))
    return taps, v, vb, wa, wx, r, i, sp, a, nm


def _lru_fwd(proj, conv_w, conv_b, w_a, b_a, w_x, b_x, lam, name):
    _, _, s, e_half = proj.shape
    n_heads = 2 * e_half // HEAD_DIM
    sp_ = _lru_specs(s, e_half, n_heads)

    def body(pg_ref, cw_ref, cb_ref, wa_ref, ba_ref, wx_ref, bx_ref, lam_ref, y_ref,
             sa_ref, sb_ref, sh_ref):
        _, v, _, _, _, _, i, _, a, nm = _lru_forward_parts(
            pg_ref[0].astype(F32), cw_ref, cb_ref, wa_ref, ba_ref, wx_ref, bx_ref, lam_ref)
        hs = _scan(a, nm * (i * v), sa_ref, sb_ref, sh_ref, reverse=False)
        g = pg_ref[1].astype(F32)
        y_ref[...] = (hs * (g * _sigmoid(g))).astype(BF16)

    return pl.pallas_call(
        body, name=name, grid=(n_heads,),
        in_specs=[sp_["pair"], sp_["conv_w"], sp_["chan"], sp_["w"], sp_["bias"], sp_["w"],
                  sp_["bias"], sp_["chan"]],
        out_specs=sp_["plane"],
        out_shape=jax.ShapeDtypeStruct((s, 2 * e_half), BF16),
        scratch_shapes=[pltpu.VMEM((s, HEAD_DIM), F32)] * 3,
        compiler_params=_params(("parallel",)),
    )(proj, conv_w, conv_b, w_a, b_a, w_x, b_x, lam)


def _lru_bwd(proj, dy, conv_w, conv_b, w_a, b_a, w_x, b_x, lam, name):
    _, _, s, e_half = proj.shape
    e = 2 * e_half
    c = HEAD_DIM
    n_heads = e // c
    hp = e_half // c
    sp_ = _lru_specs(s, e_half, n_heads)

    def body(pg_ref, dy_ref, cw_ref, cb_ref, wa_ref, ba_ref, wx_ref, bx_ref, lam_ref,
             dpg_ref, dwa_ref, dwx_ref, dba_ref, dbx_ref, dlam_ref, dcw_ref, dcb_ref,
             sa_ref, sb_ref, sh_ref, sd_ref):
        taps, v, vb, wa, wx, r, i, sp, a, nm = _lru_forward_parts(
            pg_ref[0].astype(F32), cw_ref, cb_ref, wa_ref, ba_ref, wx_ref, bx_ref, lam_ref)
        iv = i * v
        hs = _scan(a, nm * iv, sa_ref, sb_ref, sh_ref, reverse=False)
        g = pg_ref[1].astype(F32)
        dyv = dy_ref[...].astype(F32)
        sig = _sigmoid(g)
        dpg_ref[1] = (dyv * hs * (sig * (1.0 + g * (1.0 - sig)))).astype(BF16)
        dh = _scan(_shift_up(a, 1), dyv * (g * sig), sa_ref, sb_ref, sd_ref, reverse=True)
        da = dh * _shift_down(hs, 1)
        div = dh * nm
        a2 = a * a
        dla = da * a - (dh * iv) * (a2 / nm)
        dzr = (dla * ((-RGLRU_C) * sp)) * (r * (1.0 - r))
        dzi = (div * v) * (i * (1.0 - i))
        lam = lam_ref[...]
        dlam_ref[...] = _colsum(dla * ((-RGLRU_C) * r)) * (-_sigmoid(-lam))
        dba_ref[...] = _colsum(dzr)
        dbx_ref[...] = _colsum(dzi)
        dzr_b, dzi_b = dzr.astype(BF16), dzi.astype(BF16)
        tn = (((0,), (0,)), ((), ()))
        nt = (((1,), (1,)), ((), ()))
        dwa_ref[...] = lax.dot_general(vb, dzr_b, tn, preferred_element_type=F32)
        dwx_ref[...] = lax.dot_general(vb, dzi_b, tn, preferred_element_type=F32)
        dv = (div * i + lax.dot_general(dzr_b, wa, nt, preferred_element_type=F32)
              + lax.dot_general(dzi_b, wx, nt, preferred_element_type=F32))
        dcb_ref[...] = _colsum(dv)
        dvp = jnp.zeros_like(dv)
        for k in range(4):
            dvp = dvp + cw_ref[k:k + 1, :] * _shift_up(dv, 3 - k)
            dcw_ref[k:k + 1, :] = _colsum(dv * taps[k])
        dpg_ref[0] = dvp.astype(BF16)

    head_mat = pl.BlockSpec((None, c, c), lambda h: (h, 0, 0))
    outs = pl.pallas_call(
        body, name=name, grid=(n_heads,),
        in_specs=[sp_["pair"], sp_["plane"], sp_["conv_w"], sp_["chan"], sp_["w"], sp_["bias"],
                  sp_["w"], sp_["bias"], sp_["chan"]],
        out_specs=[sp_["pair"], head_mat, head_mat, sp_["bias"], sp_["bias"],
                   sp_["chan"], sp_["conv_w"], sp_["chan"]],
        out_shape=[jax.ShapeDtypeStruct((2, 2, s, e_half), BF16),
                   jax.ShapeDtypeStruct((n_heads, c, c), F32), jax.ShapeDtypeStruct((n_heads, c, c), F32),
                   jax.ShapeDtypeStruct((n_heads, 1, c), F32), jax.ShapeDtypeStruct((n_heads, 1, c), F32),
                   jax.ShapeDtypeStruct((1, e), F32), jax.ShapeDtypeStruct((4, e), F32),
                   jax.ShapeDtypeStruct((1, e), F32)],
        scratch_shapes=[pltpu.VMEM((s, c), F32)] * 4,
        compiler_params=_params(("parallel",)),
    )(proj, dy, conv_w, conv_b, w_a, b_a, w_x, b_x, lam)
    return tuple(outs)


def _ada_fwd(c_all, ada_w, name):
    n_l, d, f = ada_w.shape
    bf = _blk(f, 512)

    def body(c_ref, w_ref, o_ref):
        cv = c_ref[...]
        sc = (cv * _sigmoid(cv)).astype(BF16)
        o_ref[...] = jnp.dot(sc, w_ref[...].astype(BF16), preferred_element_type=F32)

    return pl.pallas_call(
        body, name=name, grid=(n_l, f // bf),
        in_specs=[pl.BlockSpec((8, d), lambda l, j: (0, 0)), pl.BlockSpec((None, d, bf), lambda l, j: (l, 0, j))],
        out_specs=pl.BlockSpec((None, 8, bf), lambda l, j: (l, 0, j)),
        out_shape=jax.ShapeDtypeStruct((n_l, 8, f), F32),
        compiler_params=_params(("parallel", "parallel")),
    )(c_all, ada_w)


def _ada_bwd_adamw(c_t, dmod, w, m, v, name, after=None):
    n_l, d, f = w.shape
    bf = _blk(f, 256)

    def body(c_ref, dm_ref, w_ref, m_ref, v_ref, g_ref, d_ref, m2_ref, v2_ref):
        cv = c_ref[...]
        sc = cv * _sigmoid(cv)
        dm = dm_ref[...]
        g = sc[:, 0:1] * dm[0:1, :]
        for b in range(1, 8):
            g = g + sc[:, b:b + 1] * dm[b:b + 1, :]
        g_ref[...] = g
        dl, m2, v2 = _adamw_math(w_ref[...], g, m_ref[...], v_ref[...])
        d_ref[...] = dl
        m2_ref[...] = m2
        v2_ref[...] = v2

    big = pl.BlockSpec((None, d, bf), lambda l, j: (l, 0, j))
    body, extra_specs, extra = _ordered(body, 5, after)
    return pl.pallas_call(
        body, name=name, grid=(n_l, f // bf),
        in_specs=[pl.BlockSpec((d, 8), lambda l, j: (0, 0)), pl.BlockSpec((None, 8, bf), lambda l, j: (l, 0, j)),
                  big, big, big] + extra_specs,
        out_specs=[big] * 4, out_shape=[jax.ShapeDtypeStruct((n_l, d, f), F32)] * 4,
        compiler_params=_params(("parallel", "parallel")),
    )(c_t, dmod, w, m, v, *extra)


def _pack(parts):
    padded, offs, n = [], [], 0
    for p in parts:
        p = p.reshape(-1)
        size = -(-p.shape[0] // PACK) * PACK
        offs.append(n)
        n += size
        padded.append(jnp.pad(p, (0, size - p.shape[0])) if size != p.shape[0] else p)
    return jnp.concatenate(padded), offs, n


def kernel(x, c, norm_g, ada_w, ada_b, sc_w_in, sc_conv_w, sc_w_out, lru_w_in, lru_conv_w, lru_conv_b, lru_w_a, lru_b_a, lru_w_x, lru_b_x, lru_lambda, lru_w_out, final_g, loss_target, m_norm_g, m_ada_w, m_ada_b, m_sc_w_in, m_sc_conv_w, m_sc_w_out, m_lru_w_in, m_lru_conv_w, m_lru_conv_b, m_lru_w_a, m_lru_b_a, m_lru_w_x, m_lru_b_x, m_lru_lambda, m_lru_w_out, m_final_g, v_norm_g, v_ada_w, v_ada_b, v_sc_w_in, v_sc_conv_w, v_sc_w_out, v_lru_w_in, v_lru_conv_w, v_lru_conv_b, v_lru_w_a, v_lru_b_a, v_lru_w_x, v_lru_b_x, v_lru_lambda, v_lru_w_out, v_final_g):
    xi, yi, ci = _pos()
    chip = 2 * xi + yi
    batch = 4 * xi + 2 * yi + ci
    core_op = jnp.reshape(ci, (1,)).astype(jnp.int32)
    chip_op = jnp.reshape(chip, (1,)).astype(jnp.int32)
    where_op = jnp.stack([chip, ci]).astype(jnp.int32)

    x2d, tgt = x[0], loss_target[0]
    s, d = x2d.shape
    es = sc_conv_w.shape[2]
    e = 4 * es
    n_heads = lru_w_a.shape[1]
    hj = lru_b_a.shape[2]
    f = ada_w.shape[2]
    row = lambda t: t.reshape(1, -1)

    small_parts = [c, sc_conv_w, lru_conv_w, lru_conv_b, lru_b_a, lru_b_x, lru_lambda]
    small, offs, n_small = _pack(small_parts)
    got = _allgather8([small.reshape(8, n_small // 8)], "ag_small")[0].reshape(8, n_small)
    c_all = got[:, :d]
    per_chip = got[0::2]

    def chip_part(k, shape):
        size = 1
        for dim in shape:
            size *= dim
        return per_chip[:, offs[k]:offs[k] + size].reshape((4,) + shape)

    conv_w0 = jnp.transpose(chip_part(1, (3, es)), (1, 0, 2)).reshape(3, e)
    conv_w1 = jnp.transpose(chip_part(2, (4, es)), (1, 0, 2)).reshape(4, e)
    conv_b1 = chip_part(3, (es,)).reshape(1, e)
    b_a = jnp.transpose(chip_part(4, (n_heads, hj)), (1, 0, 2)).reshape(n_heads, 1, 4 * hj)
    b_x = jnp.transpose(chip_part(5, (n_heads, hj)), (1, 0, 2)).reshape(n_heads, 1, 4 * hj)
    lam = chip_part(6, (es,)).reshape(1, e)

    mod_nb = _ada_fwd(c_all, ada_w, "ada_fwd")
    mods = _allgather8([mod_nb.reshape(16, f)], "ag_mod")[0].reshape(8, 2, 8, f)[0::2]
    mine = lax.dynamic_index_in_dim(mods, batch, axis=2, keepdims=False)
    mod = jnp.transpose(mine, (1, 0, 2)).reshape(2, 4 * f) + ada_b
    shift = [row(mod[l, :d]) for l in range(2)]
    scale = [row(mod[l, d:2 * d]) for l in range(2)]
    gate = [row(mod[l, 2 * d:]) for l in range(2)]
    ng = [row(norm_g[l]) for l in range(2)]

    shards = [sc_w_in[0], sc_w_out[0], lru_w_in[0], lru_w_a[0].reshape(n_heads * hj, HEAD_DIM),
              lru_w_x[0].reshape(n_heads * hj, HEAD_DIM), lru_w_out[0]]
    names = ["sc_w_in", "sc_w_out", "lru_w_in", "lru_w_a", "lru_w_x", "lru_w_out"]
    slots = [_cast_into_slot(chip_op, w, "cast_" + nm) for w, nm in zip(shards, names)]
    in_flight, started = _gather_start([[slots[0]], [slots[1]], slots[2:5], [slots[5]]], mod, "ag_start")
    h0 = _norm_mod_fwd(x2d, ng[0], scale[0], shift[0], "norm0", after=started)

    def arrived(g, after, tag):
        send1, recv1, bufs = in_flight[g]
        send2, recv2, bufs, passed = _gather_forward(send1, recv1, bufs, after, "ag_forward_" + tag)
        return _gather_finish(send2, recv2, bufs, passed, "ag_finish_" + tag)

    send1, recv1, bufs = in_flight[0]
    proj0 = _mm_plane(h0, bufs[0].reshape(4, d, e), chip_op, "sc_in_own")
    for j, (px, py) in enumerate(_other_chips(xi, yi)):
        send2, recv2, bufs, passed = _gather_forward(send1, recv1, bufs, proj0, "ag_forward_sc_w_in_%d" % j,
                                                     sources=(j,))
        bufs = _gather_finish(send2, recv2, bufs, passed, "ag_finish_sc_w_in_%d" % j, sources=(j,))
        plane = jnp.reshape(2 * px + py, (1,)).astype(jnp.int32)
        proj0 = _mm_plane(h0, bufs[0].reshape(4, d, e), plane, "sc_in_%d" % j, planes_so_far=proj0)
    w_in0 = bufs[0].reshape(4, d, e)
    w_out0 = arrived(1, proj0, "sc_w_out")[0].reshape(1, e, d)
    y0 = _sc_fwd(proj0, conv_w0, "sc_mix")
    o0 = _mm(y0[None], w_out0, "nn", F32, "sc_out")[0]
    x1, h1 = _norm_mod_fwd(x2d, ng[1], scale[1], shift[1], "norm1", o=o0, gate=gate[0])
    lru_ws = arrived(2, h1, "lru_w_in")
    w_in1 = lru_ws[0].reshape(4, d, e // 2)
    w_a = lru_ws[1].reshape(4, n_heads, hj, HEAD_DIM)
    w_x = lru_ws[2].reshape(4, n_heads, hj, HEAD_DIM)
    proj1 = _mm(h1[None], w_in1, "nn", BF16, "lru_in")
    pairs1 = proj1.reshape(2, 2, s, e // 2)
    y1 = _lru_fwd(pairs1, conv_w1, conv_b1, w_a, b_a, w_x, b_x, lam, "lru_mix")
    w_out1 = arrived(3, y1, "lru_w_out")[0].reshape(1, e, d)
    o1 = _mm(y1[None], w_out1, "nn", F32, "lru_out")[0]
    dx2, do1, acc_f = _final_loss(x1, o1, gate[1], row(final_g), tgt, "final_loss")

    def reduce_stage1(tag, grads):
        lands = [lax.empty((4, g.shape[1] // 2, g.shape[2]), F32) for g in grads]
        return _exchange_start("rs_sibling_start_" + tag, grads, lands, len(grads),
                               _plan_other_half_to_sibling, None)

    def reduce_stage2(tag, stage1, nms, after):
        send, recv, grads, lands, _ = stage1
        grads, lands = _exchange_wait("rs_sibling_wait_" + tag, send, recv, grads, lands,
                                      _plan_other_half_to_sibling, after)
        parts = [_sum_own_and_sibling(core_op, g, r1, "rs_sum1_" + nm) for g, r1, nm in zip(grads, lands, nms)]
        lands = [lax.empty((3,) + p.shape[1:], BF16) for p in parts]
        return _exchange_start("rs_chips_start_" + tag, parts, lands, 3 * len(parts),
                               _plan_partials_to_chips, None)

    def reduce_stage3(tag, stage2, nms, after):
        send, recv, parts, lands, _ = stage2
        parts, lands = _exchange_wait("rs_chips_wait_" + tag, send, recv, parts, lands,
                                      _plan_partials_to_chips, after)
        halves = [_sum_chips(where_op, p, r2, "rs_sum2_" + nm) for p, r2, nm in zip(parts, lands, nms)]
        return _exchange_start("rs_share_start_" + tag, [], halves, len(halves), _plan_share_half, None)

    def reduce_done(tag, stage3, after):
        send, recv, _, fulls, _ = stage3
        return _exchange_wait("rs_share_wait_" + tag, send, recv, [], fulls, _plan_share_half, after)[1]

    def head_major_to_chip_major(t):
        return jnp.transpose(t.reshape(n_heads, 4, hj, HEAD_DIM), (1, 0, 2, 3)).reshape(4, n_heads * hj, HEAD_DIM)

    big_state = dict(zip(names, zip(shards, [m_sc_w_in, m_sc_w_out, m_lru_w_in, m_lru_w_a, m_lru_w_x, m_lru_w_out],
                                    [v_sc_w_in, v_sc_w_out, v_lru_w_in, v_lru_w_a, v_lru_w_x, v_lru_w_out],
                                    [sc_w_in, sc_w_out, lru_w_in, lru_w_a, lru_w_x, lru_w_out])))
    big = {}

    def update(nms, fulls, after):
        for nm, g2 in zip(nms, fulls):
            w2, m4, v4, w4 = big_state[nm]
            outs = _adamw(w2, g2, m4.reshape(w2.shape), v4.reshape(w2.shape), "adamw_" + nm, after=after)
            big[nm] = tuple(t.reshape(w4.shape) for t in outs)
            after = outs[1]
        return after

    g_w_out1 = _mm(y1[None], do1[None], "tn", F32, "lru_out_dw", bm=512, bn=2048)
    dy1 = _mm(do1[None], w_out1, "nt", BF16, "lru_out_dx")[0]
    dpairs1, g_wa, g_wx, g_ba, g_bx, g_lam, g_cw1, g_cb1 = _lru_bwd(
        pairs1, dy1, conv_w1, conv_b1, w_a, b_a, w_x, b_x, lam, "lru_mix_bwd")
    dproj1 = dpairs1.reshape(4, s, e // 2)
    g_w_in1 = _mm(h1[None], dproj1, "tn", F32, "lru_in_dw", bm=512, bn=2048)
    lru_names = ["lru_w_out", "lru_w_a", "lru_w_x", "lru_w_in"]
    lru_rs = reduce_stage1("lru", [g_w_out1.reshape(4, es, d), head_major_to_chip_major(g_wa),
                                   head_major_to_chip_major(g_wx), g_w_in1])
    dh1 = _mm(dproj1, w_in1, "nt", F32, "lru_in_dx", after=lru_rs[4])[0]
    lru_rs = reduce_stage2("lru", lru_rs, lru_names, dh1)
    dx1, do0, acc1 = _norm_mod_bwd(dh1, x1, ng[1], scale[1], dx2, "norm1_bwd", o_prev=o0, gate_prev=gate[0],
                                   after=lru_rs[4])

    g_w_out0 = _mm(y0[None], do0[None], "tn", F32, "sc_out_dw", bm=512, bn=2048)
    out_rs = reduce_stage1("sc_out", [g_w_out0.reshape(4, es, d)])
    dy0 = _mm(do0[None], w_out0, "nt", BF16, "sc_out_dx", after=out_rs[4])[0]
    out_rs = reduce_stage2("sc_out", out_rs, ["sc_w_out"], dy0)
    dproj0, g_cw0 = _sc_bwd(proj0, dy0, conv_w0, "sc_mix_bwd", after=out_rs[4])
    stage1, stage2, after = [], [], None
    for k in range(IN_PIECES):
        g_piece = _mm(h0[None], dproj0, "tn", F32, "sc_in_dw_%d" % k, bm=512, bn=e // IN_PIECES,
                      col_blocks=(k, 1), after=after)
        stage1.append(reduce_stage1("sc_in_%d" % k, [g_piece]))
        after = stage1[k][4]
        if k:
            stage2.append(reduce_stage2("sc_in_%d" % (k - 1), stage1[k - 1], ["sc_w_in_%d" % (k - 1)], after))
            after = stage2[k - 1][4]
    lru_rs = reduce_stage3("lru", lru_rs, lru_names, after)
    k = IN_PIECES - 1
    stage2.append(reduce_stage2("sc_in_%d" % k, stage1[k], ["sc_w_in_%d" % k], lru_rs[4]))
    dh0 = _mm(dproj0, w_in0, "nt", F32, "sc_in_dx", after=stage2[k][4])[0]
    grad_x, acc0 = _norm_mod_bwd(dh0, x2d, ng[0], scale[0], dx1, "norm0_bwd")
    out_rs = reduce_stage3("sc_out", out_rs, ["sc_w_out"], acc0)
    last = update(lru_names, reduce_done("lru", lru_rs, out_rs[4]), None)
    last = update(["sc_w_out"], reduce_done("sc_out", out_rs, last), None)
    g_half = None
    for k, (send, recv, parts, lands, _) in enumerate(stage2):
        parts, lands = _exchange_wait("rs_chips_wait_sc_in_%d" % k, send, recv, parts, lands,
                                      _plan_partials_to_chips, last)
        g_half = _sum_chips(where_op, parts[0], lands[0], "rs_sum2_sc_w_in_%d" % k, piece=(k, IN_PIECES),
                            so_far=g_half)
    in_rs = _exchange_start("rs_share_start_sc_in", [], [g_half], 1, _plan_share_half, None)

    dmod = jnp.stack([jnp.concatenate([acc0[1], acc0[0], acc1[3]]),
                      jnp.concatenate([acc1[1], acc1[0], acc_f[1]])])
    part_list = [jnp.stack([acc0[2], acc1[2]]), acc_f[0], acc_f[2, :1], g_cw0, g_cw1, g_cb1, g_ba, g_bx,
                 g_lam, dmod]
    partials, poffs, n_part = _pack(part_list)
    every = _allgather8([partials.reshape(8, n_part // 8)], "ag_partials", after=in_rs[4])[0].reshape(8, n_part)
    total = _sum_rows8(every, "sum_partials")[0]

    def tot(k, shape):
        size = 1
        for dim in shape:
            size *= dim
        return total[poffs[k]:poffs[k] + size].reshape(shape)

    def my_cols(t, width):
        return lax.dynamic_slice_in_dim(t, chip * width, width, axis=t.ndim - 1)

    loss = tot(2, (1,))[0]
    g_norm_g, g_final_g, g_ada_b = tot(0, (2, d)), tot(1, (d,)), tot(9, (2, 3 * d))
    g_sc_conv_w = my_cols(tot(3, (3, e)), es)[None]
    g_lru_conv_w = my_cols(tot(4, (4, e)), es)[None]
    g_lru_conv_b = my_cols(tot(5, (1, e)), es)
    g_lru_b_a = my_cols(tot(6, (n_heads, 4 * hj)), hj)[None]
    g_lru_b_x = my_cols(tot(7, (n_heads, 4 * hj)), hj)[None]
    g_lru_lambda = my_cols(tot(8, (1, e)), es)

    dmod_all = every[:, poffs[9]:poffs[9] + 6 * d].reshape(8, 2, 3 * d)
    dmod_mine = jnp.transpose(my_cols(dmod_all, f), (1, 0, 2))
    ada = _ada_bwd_adamw(jnp.transpose(c_all), dmod_mine, ada_w, m_ada_w, v_ada_w, "ada_bwd_adamw", after=total)

    small_names = ["norm_g", "ada_b", "final_g", "sc_conv_w", "lru_conv_w", "lru_conv_b", "lru_b_a",
                   "lru_b_x", "lru_lambda"]
    small_w = [norm_g, ada_b, final_g, sc_conv_w, lru_conv_w, lru_conv_b, lru_b_a, lru_b_x, lru_lambda]
    small_g = [g_norm_g, g_ada_b, g_final_g, g_sc_conv_w, g_lru_conv_w, g_lru_conv_b, g_lru_b_a,
               g_lru_b_x, g_lru_lambda]
    small_m = [m_norm_g, m_ada_b, m_final_g, m_sc_conv_w, m_lru_conv_w, m_lru_conv_b, m_lru_b_a,
               m_lru_b_x, m_lru_lambda]
    small_v = [v_norm_g, v_ada_b, v_final_g, v_sc_conv_w, v_lru_conv_w, v_lru_conv_b, v_lru_b_a,
               v_lru_b_x, v_lru_lambda]
    pw, soffs, n_s = _pack(small_w)
    pg, pm, pv = _pack(small_g)[0], _pack(small_m)[0], _pack(small_v)[0]
    shape2 = (n_s // PACK, PACK)
    _, pd, pm2, pv2 = _adamw(pw.reshape(shape2), pg.reshape(shape2), pm.reshape(shape2), pv.reshape(shape2),
                             "adamw_small", after=ada[0])
    update(["sc_w_in"], reduce_done("sc_in", in_rs, pd), None)
    small = {}
    for k, (nm, w_) in enumerate(zip(small_names, small_w)):
        take = lambda t: t.reshape(-1)[soffs[k]:soffs[k] + w_.size].reshape(w_.shape)
        small[nm] = (small_g[k].reshape(w_.shape), take(pd), take(pm2), take(pv2))

    results = dict(small)
    results.update(big)
    results["ada_w"] = tuple(ada)
    order = ["norm_g", "ada_w", "ada_b", "sc_w_in", "sc_conv_w", "sc_w_out", "lru_w_in", "lru_conv_w",
             "lru_conv_b", "lru_w_a", "lru_b_a", "lru_w_x", "lru_b_x", "lru_lambda", "lru_w_out", "final_g"]
    out = [loss, grad_x[None]]
    for kind in range(4):
        out += [results[nm][kind] for nm in order]
    return tuple(out)
```

```python
import functools

import jax
import jax.numpy as jnp
from jax import lax
from jax.experimental import pallas as pl
from jax.experimental.pallas import tpu as pltpu

F32 = jnp.float32
BF16 = jnp.bfloat16
MESH = pl.DeviceIdType.MESH
ANY = pl.BlockSpec(memory_space=pl.ANY)

RMS_EPS = 1e-6
RGLRU_C = 8.0
HEAD_DIM = 256
ADAM_LR = 0.001
ADAM_B1 = 0.9
ADAM_B2 = 0.999
ADAM_EPS = 1e-08
ADAM_WD = 0.01
ADAM_STEP = 10
V7X_VMEM_LIMIT = 56 * 1024 * 1024
IN_PIECES = 2
LANES = 128
SUBLANES = 8
PACK = SUBLANES * LANES


def _blk(dim, pref, unit=LANES):
    if dim <= pref:
        return dim
    b = (pref // unit) * unit
    while b > unit and dim % b:
        b -= unit
    assert dim % b == 0, (dim, pref, unit)
    return b


def _params(sem=None):
    return pltpu.CompilerParams(dimension_semantics=sem, vmem_limit_bytes=V7X_VMEM_LIMIT)


def _ordered(body, n_in, after):
    if after is None:
        return body, [], []

    def ordered_body(*refs):
        return body(*refs[:n_in], *refs[n_in + 1:])

    return ordered_body, [ANY], [after]


def _pos():
    return lax.axis_index("x"), lax.axis_index("y"), lax.axis_index("c")


def _other_chips(x, y):
    return [(1 - x, y), (x, 1 - y), (1 - x, 1 - y)]


def _allgather8(arrs, name, after=None):
    n_t = len(arrs)
    ms = [a.shape[0] for a in arrs]

    def gather(*refs):
        ins, outs = refs[:n_t], refs[n_t:2 * n_t]
        send_sems, recv_sems, local_sems = refs[2 * n_t:]
        x, y, c = _pos()
        me, sibling = (x, y, c), (x, y, 1 - c)
        chips = _other_chips(x, y)

        def rows(t, px, py, pc):
            return outs[t].at[pl.ds((4 * px + 2 * py + pc) * ms[t], ms[t])]

        def copy(t, k, block, to, src=None):
            return pltpu.make_async_remote_copy(
                src_ref=rows(t, *block) if src is None else src, dst_ref=rows(t, *block),
                send_sem=send_sems.at[7 * t + k], recv_sem=recv_sems.at[7 * t + k],
                device_id=to, device_id_type=MESH)

        mine, first, passed = [], [], []
        for t in range(n_t):
            src = ins[t]
            cp = pltpu.make_async_copy(src, rows(t, *me), local_sems.at[t])
            cp.start()
            mine.append(cp)
            sends = [copy(t, 0, me, sibling, src=src)]
            sends += [copy(t, 1 + j, me, (*chip, c), src=src) for j, chip in enumerate(chips)]
            for cp in sends:
                cp.start()
            first += sends
        for t in range(n_t):
            for j, chip in enumerate(chips):
                copy(t, 1 + j, (*chip, c), me).wait_recv()
                cp = copy(t, 4 + j, (*chip, c), sibling)
                cp.start()
                passed.append(cp)
        for t in range(n_t):
            copy(t, 0, sibling, me).wait_recv()
            for j, chip in enumerate(chips):
                copy(t, 4 + j, (*chip, 1 - c), me).wait_recv()
        for cp in first + passed:
            cp.wait_send()
        for cp in mine:
            cp.wait()

    body, extra_specs, extra = _ordered(gather, n_t, after)
    return pl.pallas_call(
        body, name=name,
        out_shape=[jax.ShapeDtypeStruct((8 * m, a.shape[1]), a.dtype) for m, a in zip(ms, arrs)],
        in_specs=[ANY] * n_t + extra_specs, out_specs=[ANY] * n_t,
        scratch_shapes=[pltpu.SemaphoreType.DMA((7 * n_t,)), pltpu.SemaphoreType.DMA((7 * n_t,)),
                        pltpu.SemaphoreType.DMA((n_t,))],
    )(*arrs, *extra)


HBM = pl.BlockSpec(memory_space=pltpu.HBM)
SEM = pl.BlockSpec(memory_space=pltpu.SEMAPHORE)
TOKEN = pl.BlockSpec(memory_space=pltpu.VMEM)
IN_FLIGHT = pltpu.CompilerParams(has_side_effects=pltpu.SideEffectType.DATAFLOW_SIDE_EFFECTING)


def _in_hbm(arrs):
    return [pltpu.with_memory_space_constraint(a, pltpu.HBM) for a in arrs]


def _shard_rows(ref, h, px, py, pc):
    return ref.at[pl.ds((4 * px + 2 * py + pc) * h, h)]


def _gather_start(groups, after, name):
    bufs = [b for grp in groups for b in grp]
    n_t, n_g = len(bufs), len(groups)

    def body(*refs):
        sems, thru = refs[n_t + 1:n_t + 1 + 2 * n_g], refs[n_t + 1 + 2 * n_g:2 * n_t + 1 + 2 * n_g]
        token = refs[-1]
        x, y, c = _pos()
        t = 0
        for g, grp in enumerate(groups):
            for i in range(len(grp)):
                h = bufs[t].shape[0] // 8
                rows = _shard_rows(thru[t], h, x, y, c)
                for j, chip in enumerate(_other_chips(x, y)):
                    pltpu.make_async_remote_copy(
                        src_ref=rows, dst_ref=rows, send_sem=sems[2 * g].at[3 * i + j],
                        recv_sem=sems[2 * g + 1].at[3 * i + j], device_id=(*chip, c),
                        device_id_type=MESH).start()
                t += 1
        token[...] = jnp.zeros_like(token)

    sem_shapes = []
    for grp in groups:
        sem_shapes += [pltpu.SemaphoreType.DMA((3 * len(grp),))] * 2
    out = pl.pallas_call(
        body, name=name,
        out_shape=sem_shapes + [pltpu.HBM(b.shape, b.dtype) for b in bufs] + [jax.ShapeDtypeStruct((8, LANES), F32)],
        in_specs=[HBM] * n_t + [ANY], out_specs=[SEM] * (2 * n_g) + [HBM] * n_t + [TOKEN],
        input_output_aliases={t: 2 * n_g + t for t in range(n_t)},
        compiler_params=IN_FLIGHT,
    )(*_in_hbm(bufs), after)
    sems, thru, token = out[:2 * n_g], out[2 * n_g:2 * n_g + n_t], out[-1]
    per_group, t = [], 0
    for g, grp in enumerate(groups):
        per_group.append((sems[2 * g], sems[2 * g + 1], thru[t:t + len(grp)]))
        t += len(grp)
    return per_group, token


def _gather_forward(send_sems, recv_sems, bufs, after, name, sources=(0, 1, 2)):
    n_t = len(bufs)

    def body(*refs):
        ins = refs[:n_t]
        send1, recv1 = refs[n_t], refs[n_t + 1]
        send2, recv2 = refs[n_t + 3], refs[n_t + 4]
        token = refs[-1]
        x, y, c = _pos()
        chips = _other_chips(x, y)
        for t in range(n_t):
            h = bufs[t].shape[0] // 8
            mine = _shard_rows(ins[t], h, x, y, c)
            for j in sources:
                chip = chips[j]
                landed = _shard_rows(ins[t], h, *chip, c)
                pltpu.make_async_remote_copy(
                    src_ref=mine, dst_ref=landed, send_sem=send1.at[3 * t + j], recv_sem=recv1.at[3 * t + j],
                    device_id=(*chip, c), device_id_type=MESH).wait_recv()
                pltpu.make_async_remote_copy(
                    src_ref=landed, dst_ref=landed, send_sem=send2.at[3 * t + j], recv_sem=recv2.at[3 * t + j],
                    device_id=(x, y, 1 - c), device_id_type=MESH).start()
        for t in range(n_t):
            h = bufs[t].shape[0] // 8
            mine = _shard_rows(ins[t], h, x, y, c)
            for j in sources:
                pltpu.make_async_remote_copy(
                    src_ref=mine, dst_ref=mine, send_sem=send1.at[3 * t + j], recv_sem=recv1.at[3 * t + j],
                    device_id=(*chips[j], c), device_id_type=MESH).wait_send()
        token[...] = jnp.zeros_like(token)

    out = pl.pallas_call(
        body, name=name,
        out_shape=[pltpu.SemaphoreType.DMA((3 * n_t,))] * 2 + [pltpu.HBM(b.shape, b.dtype) for b in bufs]
        + [jax.ShapeDtypeStruct((8, LANES), F32)],
        in_specs=[HBM] * n_t + [SEM, SEM, ANY], out_specs=[SEM, SEM] + [HBM] * n_t + [TOKEN],
        input_output_aliases={t: 2 + t for t in range(n_t)},
        compiler_params=IN_FLIGHT,
    )(*bufs, send_sems, recv_sems, after)
    return out[0], out[1], out[2:2 + n_t], out[-1]


def _gather_finish(send_sems, recv_sems, bufs, after, name, sources=(0, 1, 2)):
    n_t = len(bufs)

    def body(*refs):
        ins = refs[:n_t]
        send2, recv2 = refs[n_t], refs[n_t + 1]
        x, y, c = _pos()
        chips = _other_chips(x, y)
        for t in range(n_t):
            h = bufs[t].shape[0] // 8
            for j in sources:
                chip = chips[j]
                sent = _shard_rows(ins[t], h, *chip, c)
                got = _shard_rows(ins[t], h, *chip, 1 - c)
                cp = pltpu.make_async_remote_copy(
                    src_ref=sent, dst_ref=got, send_sem=send2.at[3 * t + j], recv_sem=recv2.at[3 * t + j],
                    device_id=(x, y, 1 - c), device_id_type=MESH)
                cp.wait_send()
                cp.wait_recv()

    return pl.pallas_call(
        body, name=name, out_shape=[pltpu.HBM(b.shape, b.dtype) for b in bufs],
        in_specs=[HBM] * n_t + [SEM, SEM, ANY], out_specs=[HBM] * n_t,
        input_output_aliases={t: t for t in range(n_t)},
        compiler_params=IN_FLIGHT,
    )(*bufs, send_sems, recv_sems, after)


def _exchange_start(name, srcs, lands, n_copies, plan, after):
    ns, nl = len(srcs), len(lands)
    extra = [] if after is None else [after]

    def body(*refs):
        base = ns + nl + len(extra)
        send_sems, recv_sems = refs[base], refs[base + 1]
        src_refs, land_refs = refs[base + 2:base + 2 + ns], refs[base + 2 + ns:base + 2 + ns + nl]
        token = refs[-1]
        x, y, c = _pos()
        copies = plan(src_refs, land_refs, x, y, c)
        assert len(copies) == n_copies
        for k, (src, dst, dev) in enumerate(copies):
            pltpu.make_async_remote_copy(
                src_ref=src, dst_ref=dst, send_sem=send_sems.at[k], recv_sem=recv_sems.at[k],
                device_id=dev, device_id_type=MESH).start()
        token[...] = jnp.zeros_like(token)

    out = pl.pallas_call(
        body, name=name,
        out_shape=[pltpu.SemaphoreType.DMA((n_copies,))] * 2
        + [pltpu.HBM(a.shape, a.dtype) for a in list(srcs) + list(lands)] + [jax.ShapeDtypeStruct((8, LANES), F32)],
        in_specs=[HBM] * (ns + nl) + [ANY] * len(extra), out_specs=[SEM, SEM] + [HBM] * (ns + nl) + [TOKEN],
        input_output_aliases={i: 2 + i for i in range(ns + nl)},
        compiler_params=IN_FLIGHT,
    )(*_in_hbm(list(srcs) + list(lands)), *extra)
    return out[0], out[1], out[2:2 + ns], out[2 + ns:2 + ns + nl], out[-1]


def _exchange_wait(name, send_sems, recv_sems, srcs, lands, plan, after):
    ns, nl = len(srcs), len(lands)

    def body(*refs):
        src_refs, land_refs = refs[:ns], refs[ns:ns + nl]
        send, recv = refs[ns + nl], refs[ns + nl + 1]
        x, y, c = _pos()
        for k, (src, dst, dev) in enumerate(plan(src_refs, land_refs, x, y, c)):
            cp = pltpu.make_async_remote_copy(
                src_ref=src, dst_ref=dst, send_sem=send.at[k], recv_sem=recv.at[k],
                device_id=dev, device_id_type=MESH)
            cp.wait_send()
            cp.wait_recv()

    out = pl.pallas_call(
        body, name=name, out_shape=[pltpu.HBM(a.shape, a.dtype) for a in list(srcs) + list(lands)],
        in_specs=[HBM] * (ns + nl) + [SEM, SEM, ANY], out_specs=[HBM] * (ns + nl),
        input_output_aliases={i: i for i in range(ns + nl)},
        compiler_params=IN_FLIGHT,
    )(*srcs, *lands, send_sems, recv_sems, after)
    return out[:ns], out[ns:]


def _plan_other_half_to_sibling(src_refs, land_refs, x, y, c):
    out = []
    for g_ref, r_ref in zip(src_refs, land_refs):
        h = g_ref.shape[1] // 2
        out.append((g_ref.at[:, pl.ds((1 - c) * h, h), :], r_ref, (x, y, 1 - c)))
    return out


def _plan_partials_to_chips(src_refs, land_refs, x, y, c):
    out = []
    for p_ref, r_ref in zip(src_refs, land_refs):
        for j, (px, py) in enumerate(_other_chips(x, y)):
            out.append((p_ref.at[2 * px + py], r_ref.at[j], (px, py, c)))
    return out


def _plan_share_half(src_refs, land_refs, x, y, c):
    out = []
    for f_ref in land_refs:
        h = f_ref.shape[0] // 2
        rows = f_ref.at[pl.ds(c * h, h)]
        out.append((rows, rows, (x, y, 1 - c)))
    return out


def _cast_into_slot(chip, w, name):
    r, c = w.shape
    br, bc = _blk(r, 512, 8), _blk(c, 2048)
    nb = r // br

    def body(chip_ref, w_ref, o_ref):
        o_ref[...] = w_ref[...].astype(BF16)

    grid_spec = pltpu.PrefetchScalarGridSpec(
        num_scalar_prefetch=1, grid=(nb, c // bc),
        in_specs=[pl.BlockSpec((br, bc), lambda i, j, chip_ref: (i, j))],
        out_specs=pl.BlockSpec((br, bc), lambda i, j, chip_ref: (chip_ref[0] * nb + i, j)))
    return pl.pallas_call(
        body, name=name, grid_spec=grid_spec,
        out_shape=jax.ShapeDtypeStruct((4 * r, c), BF16),
        compiler_params=_params(("parallel", "parallel")),
    )(chip, w)


def _sum_own_and_sibling(core, g, r1, name):
    _, r, c = g.shape
    h = r // 2
    br, bc = _blk(h, 256, 8), _blk(c, 2048)
    nb = h // br

    def body(core_ref, g_ref, r_ref, o_ref):
        o_ref[...] = (g_ref[...] + r_ref[...]).astype(BF16)

    grid_spec = pltpu.PrefetchScalarGridSpec(
        num_scalar_prefetch=1, grid=(4, nb, c // bc),
        in_specs=[pl.BlockSpec((None, br, bc), lambda k, i, j, core_ref: (k, core_ref[0] * nb + i, j)),
                  pl.BlockSpec((None, br, bc), lambda k, i, j, core_ref: (k, i, j))],
        out_specs=pl.BlockSpec((None, br, bc), lambda k, i, j, core_ref: (k, i, j)))
    return pl.pallas_call(
        body, name=name, grid_spec=grid_spec,
        out_shape=jax.ShapeDtypeStruct((4, h, c), BF16),
        compiler_params=_params(("parallel", "parallel", "parallel")),
    )(core, g, r1)


def _sum_chips(where, p, r2, name, piece=(0, 1), so_far=None):
    _, h, c = p.shape
    k, n = piece
    br, bc = _blk(h, 256, 8), _blk(c, 2048)
    nb, ncb = h // br, c // bc

    def body(where_ref, p_ref, r_ref, *rest):
        acc = p_ref[...].astype(F32)
        for j in range(3):
            acc = acc + r_ref[j].astype(F32)
        rest[-1][...] = acc

    extra_specs, extra, aliases = ([], [], {}) if so_far is None else ([ANY], [so_far], {3: 0})
    grid_spec = pltpu.PrefetchScalarGridSpec(
        num_scalar_prefetch=1, grid=(nb, ncb),
        in_specs=[pl.BlockSpec((None, br, bc), lambda i, j, where_ref: (where_ref[0], i, j)),
                  pl.BlockSpec((3, br, bc), lambda i, j, where_ref: (0, i, j))] + extra_specs,
        out_specs=pl.BlockSpec((br, bc), lambda i, j, where_ref: (where_ref[1] * nb + i, k * ncb + j)))
    return pl.pallas_call(
        body, name=name, grid_spec=grid_spec,
        out_shape=jax.ShapeDtypeStruct((2 * h, n * c), F32), input_output_aliases=aliases,
        compiler_params=_params(("parallel", "parallel")),
    )(where, p, r2, *extra)


def _adamw_math(w, g, m, v):
    m2 = ADAM_B1 * m + (1.0 - ADAM_B1) * g
    v2 = ADAM_B2 * v + (1.0 - ADAM_B2) * (g * g)
    m_hat = m2 / (1.0 - ADAM_B1 ** ADAM_STEP)
    v_hat = v2 / (1.0 - ADAM_B2 ** ADAM_STEP)
    delta = -ADAM_LR * (m_hat / (jnp.sqrt(v_hat) + ADAM_EPS) + ADAM_WD * w)
    return delta, m2, v2


def _adamw(w, g, m, v, name, after=None):
    r, c = w.shape
    br, bc = _blk(r, 128, 8), _blk(c, 2048)

    def body(w_ref, g_ref, m_ref, v_ref, go_ref, d_ref, m2_ref, v2_ref):
        gv = g_ref[...]
        d, m2, v2 = _adamw_math(w_ref[...], gv, m_ref[...], v_ref[...])
        go_ref[...] = gv
        d_ref[...] = d
        m2_ref[...] = m2
        v2_ref[...] = v2

    spec = pl.BlockSpec((br, bc), lambda i, j: (i, j))
    body, extra_specs, extra = _ordered(body, 4, after)
    return pl.pallas_call(
        body, name=name, grid=(r // br, c // bc),
        in_specs=[spec] * 4 + extra_specs, out_specs=[spec] * 4,
        out_shape=[jax.ShapeDtypeStruct((r, c), F32)] * 4,
        compiler_params=_params(("parallel", "parallel")),
    )(w, g, m, v, *extra)


def _sum_rows8(g, name, after=None):
    n = g.shape[1]

    def body(g_ref, o_ref):
        acc = g_ref[0:1, :]
        for k in range(1, 8):
            acc = acc + g_ref[k:k + 1, :]
        o_ref[...] = acc

    body, extra_specs, extra = _ordered(body, 1, after)
    return pl.pallas_call(
        body, name=name, out_shape=jax.ShapeDtypeStruct((1, n), F32),
        in_specs=[pl.BlockSpec(memory_space=pltpu.VMEM)] + extra_specs,
        out_specs=pl.BlockSpec(memory_space=pltpu.VMEM),
        compiler_params=_params(),
    )(g, *extra)


def _mm(a, b, mode, out_dtype, name, bm=1024, bn=None, after=None, col_blocks=None):
    if mode == "nn":
        (_, m, k), (g, _, n) = a.shape, b.shape
    elif mode == "tn":
        (_, k, m), (g, _, n) = a.shape, b.shape
    else:
        (g, m, k), (_, n, _) = a.shape, b.shape
    if bn is None:
        bn = 1024 if k <= 2048 else 512
    bm, bn = _blk(m, bm), _blk(n, bn)

    if mode == "nt":
        def body(a_ref, b_ref, o_ref, acc_ref):
            part = lax.dot_general(a_ref[...], b_ref[...], (((1,), (1,)), ((), ())),
                                   preferred_element_type=F32)
            if g == 1:
                o_ref[...] = part.astype(out_dtype)
            else:
                gi = pl.program_id(2)

                @pl.when(gi == 0)
                def _():
                    acc_ref[...] = part

                @pl.when(gi > 0)
                def _():
                    acc_ref[...] += part

                @pl.when(gi == g - 1)
                def _():
                    o_ref[...] = acc_ref[...].astype(out_dtype)

        body, extra_specs, extra = _ordered(body, 2, after)
        return pl.pallas_call(
            body, name=name, grid=(m // bm, n // bn, g),
            in_specs=[pl.BlockSpec((None, bm, k), lambda i, j, gi: (gi, i, 0)),
                      pl.BlockSpec((None, bn, k), lambda i, j, gi: (gi, j, 0))] + extra_specs,
            out_specs=pl.BlockSpec((None, bm, bn), lambda i, j, gi: (0, i, j)),
            out_shape=jax.ShapeDtypeStruct((1, m, n), out_dtype),
            scratch_shapes=[pltpu.VMEM((bm, bn), F32)],
            compiler_params=_params(("parallel", "parallel", "arbitrary")),
        )(a, b, *extra)

    contract = (((1,), (0,)), ((), ())) if mode == "nn" else (((0,), (0,)), ((), ()))

    def body(a_ref, b_ref, o_ref):
        o_ref[...] = lax.dot_general(a_ref[...], b_ref[...], contract,
                                     preferred_element_type=F32).astype(out_dtype)

    a_spec = (pl.BlockSpec((None, bm, k), lambda i, gi, j: (0, i, 0)) if mode == "nn"
              else pl.BlockSpec((None, k, bm), lambda i, gi, j: (0, 0, i)))
    first, count = (0, n // bn) if col_blocks is None else col_blocks
    body, extra_specs, extra = _ordered(body, 2, after)
    return pl.pallas_call(
        body, name=name, grid=(m // bm, g, count),
        in_specs=[a_spec, pl.BlockSpec((None, k, bn), lambda i, gi, j: (gi, 0, first + j))] + extra_specs,
        out_specs=pl.BlockSpec((None, bm, bn), lambda i, gi, j: (gi, i, j)),
        out_shape=jax.ShapeDtypeStruct((g, m, count * bn), out_dtype),
        compiler_params=_params(("parallel", "parallel", "parallel")),
    )(a, b, *extra)


def _mm_plane(a, b, plane, name, planes_so_far=None, after=None, bm=1024, bn=1024):
    (m, k), (g, _, n) = a.shape, b.shape
    bm, bn = _blk(m, bm), _blk(n, bn)

    def body(plane_ref, a_ref, b_ref, *rest):
        rest[-1][...] = jnp.dot(a_ref[...], b_ref[...], preferred_element_type=F32).astype(BF16)

    extra_specs, extra, aliases = [], [], {}
    if planes_so_far is not None:
        extra_specs.append(ANY)
        extra.append(planes_so_far)
        aliases = {3: 0}
    if after is not None:
        extra_specs.append(ANY)
        extra.append(after)
    grid_spec = pltpu.PrefetchScalarGridSpec(
        num_scalar_prefetch=1, grid=(m // bm, n // bn),
        in_specs=[pl.BlockSpec((bm, k), lambda i, j, p: (i, 0)),
                  pl.BlockSpec((None, k, bn), lambda i, j, p: (p[0], 0, j))] + extra_specs,
        out_specs=pl.BlockSpec((None, bm, bn), lambda i, j, p: (p[0], i, j)))
    return pl.pallas_call(
        body, name=name, grid_spec=grid_spec,
        out_shape=jax.ShapeDtypeStruct((g, m, n), BF16), input_output_aliases=aliases,
        compiler_params=_params(("parallel", "parallel")),
    )(plane, a, b, *extra)


def _row_specs(br, d):
    return (pl.BlockSpec((br, d), lambda i: (i, 0)), pl.BlockSpec((1, d), lambda i: (0, 0)),
            pl.BlockSpec((8, d), lambda i: (0, 0)))


def _rstd(xv):
    return lax.rsqrt(jnp.mean(xv * xv, axis=-1, keepdims=True) + RMS_EPS)


def _colsum(v):
    return jnp.sum(v, axis=0, keepdims=True)


def _norm_mod_fwd(x, g, scale, shift, name, o=None, gate=None, after=None):
    s, d = x.shape
    br = _blk(s, 256, 8)
    has_res = o is not None
    row, vec, _ = _row_specs(br, d)

    def body(*refs):
        if has_res:
            x_ref, o_ref, gate_ref, g_ref, sc_ref, sh_ref, x1_ref, h_ref = refs
            xv = x_ref[...] + gate_ref[...] * o_ref[...]
            x1_ref[...] = xv
        else:
            x_ref, g_ref, sc_ref, sh_ref, h_ref = refs
            xv = x_ref[...]
        n = xv * _rstd(xv) * g_ref[...]
        h_ref[...] = (n * (1.0 + sc_ref[...]) + sh_ref[...]).astype(BF16)

    ins = [x] + ([o, gate] if has_res else []) + [g, scale, shift]
    in_specs = [row] + ([row, vec] if has_res else []) + [vec] * 3
    out_shape = ([jax.ShapeDtypeStruct((s, d), F32)] if has_res else []) + [jax.ShapeDtypeStruct((s, d), BF16)]
    body, extra_specs, extra = _ordered(body, len(ins), after)
    out = pl.pallas_call(
        body, name=name, grid=(s // br,), in_specs=in_specs + extra_specs, out_specs=[row] * len(out_shape),
        out_shape=out_shape, compiler_params=_params(("parallel",)),
    )(*ins, *extra)
    return out if has_res else out[0]


def _final_loss(x1, o1, gate1, final_g, tgt, name):
    s, d = x1.shape
    br = _blk(s, 256, 8)
    row, vec, acc = _row_specs(br, d)

    def body(x1_ref, o_ref, gate_ref, g_ref, t_ref, dx_ref, do_ref, acc_ref):
        @pl.when(pl.program_id(0) == 0)
        def _():
            acc_ref[...] = jnp.zeros_like(acc_ref)

        gate, o, g = gate_ref[...], o_ref[...], g_ref[...]
        x2 = x1_ref[...] + gate * o
        r = _rstd(x2)
        xh = x2 * r
        err = xh * g - t_ref[...]
        loss = 0.5 * _colsum(jnp.mean(err * err, axis=-1, keepdims=True))
        dout = err * (1.0 / d)
        dxh = dout * g
        dx2 = r * (dxh - xh * jnp.mean(dxh * xh, axis=-1, keepdims=True))
        dx_ref[...] = dx2
        do_ref[...] = (dx2 * gate).astype(BF16)
        acc_ref[0:1, :] += _colsum(dout * xh)
        acc_ref[1:2, :] += _colsum(dx2 * o)
        acc_ref[2:3, :] += jnp.broadcast_to(loss, (1, d))

    return pl.pallas_call(
        body, name=name, grid=(s // br,),
        in_specs=[row, row, vec, vec, row], out_specs=[row, row, acc],
        out_shape=[jax.ShapeDtypeStruct((s, d), F32), jax.ShapeDtypeStruct((s, d), BF16),
                   jax.ShapeDtypeStruct((8, d), F32)],
        compiler_params=_params(("arbitrary",)),
    )(x1, o1, gate1, final_g, tgt)


def _norm_mod_bwd(dh, x, g, scale, dx_next, name, o_prev=None, gate_prev=None, after=None):
    s, d = x.shape
    br = _blk(s, 256, 8)
    has_prev = o_prev is not None
    row, vec, acc = _row_specs(br, d)

    def body(*refs):
        if has_prev:
            dh_ref, x_ref, g_ref, sc_ref, dxn_ref, o_ref, gate_ref, dx_ref, do_ref, acc_ref = refs
        else:
            dh_ref, x_ref, g_ref, sc_ref, dxn_ref, dx_ref, acc_ref = refs

        @pl.when(pl.program_id(0) == 0)
        def _():
            acc_ref[...] = jnp.zeros_like(acc_ref)

        xv, gv, dhv = x_ref[...], g_ref[...], dh_ref[...]
        r = _rstd(xv)
        xh = xv * r
        acc_ref[0:1, :] += _colsum(dhv * (xh * gv))
        acc_ref[1:2, :] += _colsum(dhv)
        dn = dhv * (1.0 + sc_ref[...])
        acc_ref[2:3, :] += _colsum(dn * xh)
        dxh = dn * gv
        dx = dxn_ref[...] + r * (dxh - xh * jnp.mean(dxh * xh, axis=-1, keepdims=True))
        dx_ref[...] = dx
        if has_prev:
            acc_ref[3:4, :] += _colsum(dx * o_ref[...])
            do_ref[...] = (dx * gate_ref[...]).astype(BF16)

    ins = [dh, x, g, scale, dx_next] + ([o_prev, gate_prev] if has_prev else [])
    in_specs = [row, row, vec, vec, row] + ([row, vec] if has_prev else [])
    out_shape = [jax.ShapeDtypeStruct((s, d), F32)]
    out_specs = [row]
    if has_prev:
        out_shape.append(jax.ShapeDtypeStruct((s, d), BF16))
        out_specs.append(row)
    out_shape.append(jax.ShapeDtypeStruct((8, d), F32))
    out_specs.append(acc)
    body, extra_specs, extra = _ordered(body, len(ins), after)
    return pl.pallas_call(
        body, name=name, grid=(s // br,), in_specs=in_specs + extra_specs, out_specs=out_specs,
        out_shape=out_shape, compiler_params=_params(("arbitrary",)),
    )(*ins, *extra)


def _tiles(p):
    s, c = p.shape
    return p.reshape(s // SUBLANES, SUBLANES, c)


def _shift_down(p, k):
    if k == 0:
        return p
    r = pltpu.roll(_tiles(p), k, 1)
    before = jnp.concatenate([jnp.zeros_like(r[:1]), r[:-1]], axis=0)
    rows = lax.broadcasted_iota(jnp.int32, r.shape, 1)
    return jnp.where(rows >= k, r, before).reshape(p.shape)


def _shift_up(p, k):
    if k == 0:
        return p
    r = pltpu.roll(_tiles(p), SUBLANES - k, 1)
    after = jnp.concatenate([r[1:], jnp.zeros_like(r[:1])], axis=0)
    rows = lax.broadcasted_iota(jnp.int32, r.shape, 1)
    return jnp.where(rows < SUBLANES - k, r, after).reshape(p.shape)


def _sigmoid(z):
    return 0.5 * (jnp.tanh(0.5 * z) + 1.0)


def _sc_parts(proj_ref, w_ref):
    b, cg, v, g = (proj_ref[i].astype(F32) for i in range(4))
    p = cg * v
    u = w_ref[2:3, :] * p + w_ref[1:2, :] * _shift_down(p, 1) + w_ref[0:1, :] * _shift_down(p, 2)
    return b, cg, v, g, p, u


def _sc_fwd(proj, conv_w, name):
    _, s, e = proj.shape
    bc = _blk(e, 256)

    def body(proj_ref, w_ref, y_ref):
        b, _, _, g, _, u = _sc_parts(proj_ref, w_ref)
        y_ref[...] = (b * u * (g * _sigmoid(g))).astype(BF16)

    return pl.pallas_call(
        body, name=name, grid=(e // bc,),
        in_specs=[pl.BlockSpec((4, s, bc), lambda j: (0, 0, j)), pl.BlockSpec((3, bc), lambda j: (0, j))],
        out_specs=pl.BlockSpec((s, bc), lambda j: (0, j)),
        out_shape=jax.ShapeDtypeStruct((s, e), BF16),
        compiler_params=_params(("parallel",)),
    )(proj, conv_w)


def _sc_bwd(proj, dy, conv_w, name, after=None):
    _, s, e = proj.shape
    bc = _blk(e, 256)

    def body(proj_ref, dy_ref, w_ref, dp_ref, dw_ref):
        b, cg, v, g, p, u = _sc_parts(proj_ref, w_ref)
        dyv = dy_ref[...].astype(F32)
        sig = _sigmoid(g)
        t = dyv * (g * sig)
        du = t * b
        dp_ref[0] = (t * u).astype(BF16)
        dp_ref[3] = (dyv * b * u * (sig * (1.0 + g * (1.0 - sig)))).astype(BF16)
        dpp = w_ref[2:3, :] * du + w_ref[1:2, :] * _shift_up(du, 1) + w_ref[0:1, :] * _shift_up(du, 2)
        dp_ref[1] = (dpp * v).astype(BF16)
        dp_ref[2] = (dpp * cg).astype(BF16)
        dw_ref[2:3, :] = _colsum(du * p)
        dw_ref[1:2, :] = _colsum(du * _shift_down(p, 1))
        dw_ref[0:1, :] = _colsum(du * _shift_down(p, 2))

    body, extra_specs, extra = _ordered(body, 3, after)
    return pl.pallas_call(
        body, name=name, grid=(e // bc,),
        in_specs=[pl.BlockSpec((4, s, bc), lambda j: (0, 0, j)), pl.BlockSpec((s, bc), lambda j: (0, j)),
                  pl.BlockSpec((3, bc), lambda j: (0, j))] + extra_specs,
        out_specs=[pl.BlockSpec((4, s, bc), lambda j: (0, 0, j)), pl.BlockSpec((3, bc), lambda j: (0, j))],
        out_shape=[jax.ShapeDtypeStruct((4, s, e), BF16), jax.ShapeDtypeStruct((3, e), F32)],
        compiler_params=_params(("parallel",)),
    )(proj, dy, conv_w, *extra)


def _softplus_neg(lam):
    u = jnp.exp(-jnp.abs(lam))
    w = 1.0 + u
    log1p = jnp.where(w == 1.0, u, jnp.log(w) * (u / jnp.where(w == 1.0, 1.0, w - 1.0)))
    return jnp.maximum(-lam, 0.0) + log1p


def _one_minus_exp(z):
    series = -z * (1.0 + z * (0.5 + z * (1.0 / 6.0 + z * (1.0 / 24.0))))
    return jnp.where(z > -0.02, series, 1.0 - jnp.exp(z))


def _scan_in_tiles(a, b, reverse):
    shape = a.shape
    a, b = _tiles(a), _tiles(b)
    rows = lax.broadcasted_iota(jnp.int32, a.shape, 1)
    for step in (1, 2, 4):
        shift = SUBLANES - step if reverse else step
        ok = rows < SUBLANES - step if reverse else rows >= step
        a_s, b_s = pltpu.roll(a, shift, 1), pltpu.roll(b, shift, 1)
        b = jnp.where(ok, a * b_s + b, b)
        a = jnp.where(ok, a * a_s, a)
    return a.reshape(shape), b.reshape(shape)


def _scan(a, b, a_ref, b_ref, h_ref, reverse):
    s, c = a.shape
    n = s // 8
    a_t, b_t = _scan_in_tiles(a, b, reverse)
    a_ref[...] = a_t
    b_ref[...] = b_t

    def step(i, carry):
        gi = n - 1 - i if reverse else i
        sl = pl.ds(pl.multiple_of(gi * 8, 8), 8)
        h = b_ref[sl, :] + a_ref[sl, :] * carry
        h_ref[sl, :] = h
        return h[0:1, :] if reverse else h[7:8, :]

    lax.fori_loop(0, n, step, jnp.zeros((1, c), F32))
    return h_ref[...]


def _lru_specs(s, e_half, n_heads):
    hp = e_half // HEAD_DIM
    c = HEAD_DIM
    return dict(
        pair=pl.BlockSpec((2, None, s, c), lambda h: (0, h // hp, 0, h % hp)),
        conv_w=pl.BlockSpec((4, c), lambda h: (0, h)),
        chan=pl.BlockSpec((1, c), lambda h: (0, h)),
        w=pl.BlockSpec((4, None, c // 4, c), lambda h: (0, h, 0, 0)),
        bias=pl.BlockSpec((None, 1, c), lambda h: (h, 0, 0)),
        plane=pl.BlockSpec((s, c), lambda h: (0, h)),
    )


def _lru_forward_parts(vp, cw_ref, cb_ref, wa_ref, ba_ref, wx_ref, bx_ref, lam_ref):
    c = HEAD_DIM
    taps = [_shift_down(vp, 3 - k) for k in range(4)]
    v = cb_ref[...] + sum(cw_ref[k:k + 1, :] * taps[k] for k in range(4))
    vb = v.astype(BF16)
    wa = wa_ref[...].reshape(c, c)
    wx = wx_ref[...].reshape(c, c)
    r = _sigmoid(jnp.dot(vb, wa, preferred_element_type=F32) + ba_ref[...])
    i = _sigmoid(jnp.dot(vb, wx, preferred_element_type=F32) + bx_ref[...])
    sp = _softplus_neg(lam_ref[...])
    la = (-RGLRU_C) * sp * r
    a = jnp.exp(la)
    nm = jnp.sqrt(_one_minus_exp(2.0 * la))
    return taps, v, vb, wa, wx, r, i, sp, a, nm


def _lru_fwd(proj, conv_w, conv_b, w_a, b_a, w_x, b_x, lam, name):
    _, _, s, e_half = proj.shape
    n_heads = 2 * e_half // HEAD_DIM
    sp_ = _lru_specs(s, e_half, n_heads)

    def body(pg_ref, cw_ref, cb_ref, wa_ref, ba_ref, wx_ref, bx_ref, lam_ref,
             y_ref, a_ref, hs_ref, sa_ref, sb_ref, sh_ref):
        _, v, _, _, _, _, i, _, a, nm = _lru_forward_parts(
            pg_ref[0].astype(F32), cw_ref, cb_ref, wa_ref, ba_ref, wx_ref, bx_ref, lam_ref)
        hs = _scan(a, nm * (i * v), sa_ref, sb_ref, sh_ref, reverse=False)
        g = pg_ref[1].astype(F32)
        y_ref[...] = (hs * (g * _sigmoid(g))).astype(BF16)
        a_ref[...] = a
        hs_ref[...] = hs.astype(BF16)

    e = 2 * e_half
    return pl.pallas_call(
        body, name=name, grid=(n_heads,),
        in_specs=[sp_["pair"], sp_["conv_w"], sp_["chan"], sp_["w"], sp_["bias"], sp_["w"],
                  sp_["bias"], sp_["chan"]],
        out_specs=[sp_["plane"]] * 3,
        out_shape=[jax.ShapeDtypeStruct((s, e), BF16), jax.ShapeDtypeStruct((s, e), F32),
                   jax.ShapeDtypeStruct((s, e), BF16)],
        scratch_shapes=[pltpu.VMEM((s, HEAD_DIM), F32)] * 3,
        compiler_params=_params(("parallel",)),
    )(proj, conv_w, conv_b, w_a, b_a, w_x, b_x, lam)


def _lru_bwd(proj, dy, saved, conv_w, conv_b, w_a, b_a, w_x, b_x, lam, name):
    _, _, s, e_half = proj.shape
    e = 2 * e_half
    c = HEAD_DIM
    n_heads = e // c
    sp_ = _lru_specs(s, e_half, n_heads)

    def body(pg_ref, dy_ref, a_ref, hs_ref, cw_ref, cb_ref, wa_ref, ba_ref, wx_ref, bx_ref, lam_ref,
             dpg_ref, dwa_ref, dwx_ref, dba_ref, dbx_ref, dlam_ref, dcw_ref, dcb_ref,
             sa_ref, sb_ref, sd_ref):
        vp = pg_ref[0].astype(F32)
        taps = [_shift_down(vp, 3 - k) for k in range(4)]
        v = cb_ref[...] + sum(cw_ref[k:k + 1, :] * taps[k] for k in range(4))
        vb = v.astype(BF16)
        wa = wa_ref[...].reshape(c, c)
        wx = wx_ref[...].reshape(c, c)
        r = _sigmoid(jnp.dot(vb, wa, preferred_element_type=F32) + ba_ref[...])
        i = _sigmoid(jnp.dot(vb, wx, preferred_element_type=F32) + bx_ref[...])
        a, hs = a_ref[...], hs_ref[...].astype(F32)
        sp = _softplus_neg(lam_ref[...])
        nm = jnp.sqrt((1.0 - a) * (1.0 + a))
        iv = i * v
        g = pg_ref[1].astype(F32)
        dyv = dy_ref[...].astype(F32)
        sig = _sigmoid(g)
        dpg_ref[1] = (dyv * hs * (sig * (1.0 + g * (1.0 - sig)))).astype(BF16)
        dh = _scan(_shift_up(a, 1), dyv * (g * sig), sa_ref, sb_ref, sd_ref, reverse=True)
        da = dh * _shift_down(hs, 1)
        div = dh * nm
        a2 = a * a
        dla = da * a - (dh * iv) * (a2 / nm)
        dzr = (dla * ((-RGLRU_C) * sp)) * (r * (1.0 - r))
        dzi = (div * v) * (i * (1.0 - i))
        lam = lam_ref[...]
        dlam_ref[...] = _colsum(dla * ((-RGLRU_C) * r)) * (-_sigmoid(-lam))
        dba_ref[...] = _colsum(dzr)
        dbx_ref[...] = _colsum(dzi)
        dzr_b, dzi_b = dzr.astype(BF16), dzi.astype(BF16)
        tn = (((0,), (0,)), ((), ()))
        nt = (((1,), (1,)), ((), ()))
        dwa_ref[...] = lax.dot_general(vb, dzr_b, tn, preferred_element_type=F32)
        dwx_ref[...] = lax.dot_general(vb, dzi_b, tn, preferred_element_type=F32)
        dv = (div * i + lax.dot_general(dzr_b, wa, nt, preferred_element_type=F32)
              + lax.dot_general(dzi_b, wx, nt, preferred_element_type=F32))
        dcb_ref[...] = _colsum(dv)
        dvp = jnp.zeros_like(dv)
        for k in range(4):
            dvp = dvp + cw_ref[k:k + 1, :] * _shift_up(dv, 3 - k)
            dcw_ref[k:k + 1, :] = _colsum(dv * taps[k])
        dpg_ref[0] = dvp.astype(BF16)

    head_mat = pl.BlockSpec((None, c, c), lambda h: (h, 0, 0))
    outs = pl.pallas_call(
        body, name=name, grid=(n_heads,),
        in_specs=[sp_["pair"]] + [sp_["plane"]] * 3 + [sp_["conv_w"], sp_["chan"], sp_["w"], sp_["bias"],
                                                        sp_["w"], sp_["bias"], sp_["chan"]],
        out_specs=[sp_["pair"], head_mat, head_mat, sp_["bias"], sp_["bias"],
                   sp_["chan"], sp_["conv_w"], sp_["chan"]],
        out_shape=[jax.ShapeDtypeStruct((2, 2, s, e_half), BF16),
                   jax.ShapeDtypeStruct((n_heads, c, c), F32), jax.ShapeDtypeStruct((n_heads, c, c), F32),
                   jax.ShapeDtypeStruct((n_heads, 1, c), F32), jax.ShapeDtypeStruct((n_heads, 1, c), F32),
                   jax.ShapeDtypeStruct((1, e), F32), jax.ShapeDtypeStruct((4, e), F32),
                   jax.ShapeDtypeStruct((1, e), F32)],
        scratch_shapes=[pltpu.VMEM((s, c), F32)] * 3,
        compiler_params=_params(("parallel",)),
    )(proj, dy, *saved, conv_w, conv_b, w_a, b_a, w_x, b_x, lam)
    return tuple(outs)


def _ada_fwd(c_all, ada_w, name):
    n_l, d, f = ada_w.shape
    bf = _blk(f, 512)

    def body(c_ref, w_ref, o_ref):
        cv = c_ref[...]
        sc = (cv * _sigmoid(cv)).astype(BF16)
        o_ref[...] = jnp.dot(sc, w_ref[...].astype(BF16), preferred_element_type=F32)

    return pl.pallas_call(
        body, name=name, grid=(n_l, f // bf),
        in_specs=[pl.BlockSpec((8, d), lambda l, j: (0, 0)), pl.BlockSpec((None, d, bf), lambda l, j: (l, 0, j))],
        out_specs=pl.BlockSpec((None, 8, bf), lambda l, j: (l, 0, j)),
        out_shape=jax.ShapeDtypeStruct((n_l, 8, f), F32),
        compiler_params=_params(("parallel", "parallel")),
    )(c_all, ada_w)


def _ada_bwd_adamw(c_t, dmod, w, m, v, name, after=None):
    n_l, d, f = w.shape
    bf = _blk(f, 256)

    def body(c_ref, dm_ref, w_ref, m_ref, v_ref, g_ref, d_ref, m2_ref, v2_ref):
        cv = c_ref[...]
        sc = cv * _sigmoid(cv)
        dm = dm_ref[...]
        g = sc[:, 0:1] * dm[0:1, :]
        for b in range(1, 8):
            g = g + sc[:, b:b + 1] * dm[b:b + 1, :]
        g_ref[...] = g
        dl, m2, v2 = _adamw_math(w_ref[...], g, m_ref[...], v_ref[...])
        d_ref[...] = dl
        m2_ref[...] = m2
        v2_ref[...] = v2

    big = pl.BlockSpec((None, d, bf), lambda l, j: (l, 0, j))
    body, extra_specs, extra = _ordered(body, 5, after)
    return pl.pallas_call(
        body, name=name, grid=(n_l, f // bf),
        in_specs=[pl.BlockSpec((d, 8), lambda l, j: (0, 0)), pl.BlockSpec((None, 8, bf), lambda l, j: (l, 0, j)),
                  big, big, big] + extra_specs,
        out_specs=[big] * 4, out_shape=[jax.ShapeDtypeStruct((n_l, d, f), F32)] * 4,
        compiler_params=_params(("parallel", "parallel")),
    )(c_t, dmod, w, m, v, *extra)


def _pack(parts):
    padded, offs, n = [], [], 0
    for p in parts:
        p = p.reshape(-1)
        size = -(-p.shape[0] // PACK) * PACK
        offs.append(n)
        n += size
        padded.append(jnp.pad(p, (0, size - p.shape[0])) if size != p.shape[0] else p)
    return jnp.concatenate(padded), offs, n


def kernel(x, c, norm_g, ada_w, ada_b, sc_w_in, sc_conv_w, sc_w_out, lru_w_in, lru_conv_w, lru_conv_b, lru_w_a, lru_b_a, lru_w_x, lru_b_x, lru_lambda, lru_w_out, final_g, loss_target, m_norm_g, m_ada_w, m_ada_b, m_sc_w_in, m_sc_conv_w, m_sc_w_out, m_lru_w_in, m_lru_conv_w, m_lru_conv_b, m_lru_w_a, m_lru_b_a, m_lru_w_x, m_lru_b_x, m_lru_lambda, m_lru_w_out, m_final_g, v_norm_g, v_ada_w, v_ada_b, v_sc_w_in, v_sc_conv_w, v_sc_w_out, v_lru_w_in, v_lru_conv_w, v_lru_conv_b, v_lru_w_a, v_lru_b_a, v_lru_w_x, v_lru_b_x, v_lru_lambda, v_lru_w_out, v_final_g):
    xi, yi, ci = _pos()
    chip = 2 * xi + yi
    batch = 4 * xi + 2 * yi + ci
    core_op = jnp.reshape(ci, (1,)).astype(jnp.int32)
    chip_op = jnp.reshape(chip, (1,)).astype(jnp.int32)
    where_op = jnp.stack([chip, ci]).astype(jnp.int32)

    x2d, tgt = x[0], loss_target[0]
    s, d = x2d.shape
    es = sc_conv_w.shape[2]
    e = 4 * es
    n_heads = lru_w_a.shape[1]
    hj = lru_b_a.shape[2]
    f = ada_w.shape[2]
    row = lambda t: t.reshape(1, -1)

    small_parts = [c, sc_conv_w, lru_conv_w, lru_conv_b, lru_b_a, lru_b_x, lru_lambda]
    small, offs, n_small = _pack(small_parts)
    got = _allgather8([small.reshape(8, n_small // 8)], "ag_small")[0].reshape(8, n_small)
    c_all = got[:, :d]
    per_chip = got[0::2]

    def chip_part(k, shape):
        size = 1
        for dim in shape:
            size *= dim
        return per_chip[:, offs[k]:offs[k] + size].reshape((4,) + shape)

    conv_w0 = jnp.transpose(chip_part(1, (3, es)), (1, 0, 2)).reshape(3, e)
    conv_w1 = jnp.transpose(chip_part(2, (4, es)), (1, 0, 2)).reshape(4, e)
    conv_b1 = chip_part(3, (es,)).reshape(1, e)
    b_a = jnp.transpose(chip_part(4, (n_heads, hj)), (1, 0, 2)).reshape(n_heads, 1, 4 * hj)
    b_x = jnp.transpose(chip_part(5, (n_heads, hj)), (1, 0, 2)).reshape(n_heads, 1, 4 * hj)
    lam = chip_part(6, (es,)).reshape(1, e)

    mod_nb = _ada_fwd(c_all, ada_w, "ada_fwd")
    mods = _allgather8([mod_nb.reshape(16, f)], "ag_mod")[0].reshape(8, 2, 8, f)[0::2]
    mine = lax.dynamic_index_in_dim(mods, batch, axis=2, keepdims=False)
    mod = jnp.transpose(mine, (1, 0, 2)).reshape(2, 4 * f) + ada_b
    shift = [row(mod[l, :d]) for l in range(2)]
    scale = [row(mod[l, d:2 * d]) for l in range(2)]
    gate = [row(mod[l, 2 * d:]) for l in range(2)]
    ng = [row(norm_g[l]) for l in range(2)]

    shards = [sc_w_in[0], sc_w_out[0], lru_w_in[0], lru_w_a[0].reshape(n_heads * hj, HEAD_DIM),
              lru_w_x[0].reshape(n_heads * hj, HEAD_DIM), lru_w_out[0]]
    names = ["sc_w_in", "sc_w_out", "lru_w_in", "lru_w_a", "lru_w_x", "lru_w_out"]
    slots = [_cast_into_slot(chip_op, w, "cast_" + nm) for w, nm in zip(shards, names)]
    first_flight, started = _gather_start([[slots[0]]], mod, "ag_start_sc_w_in")
    rest_flight, started = _gather_start([[slots[1]], slots[2:5], [slots[5]]], started, "ag_start")
    in_flight = first_flight + rest_flight
    h0 = _norm_mod_fwd(x2d, ng[0], scale[0], shift[0], "norm0", after=started)

    def arrived(g, after, tag):
        send1, recv1, bufs = in_flight[g]
        send2, recv2, bufs, passed = _gather_forward(send1, recv1, bufs, after, "ag_forward_" + tag)
        return _gather_finish(send2, recv2, bufs, passed, "ag_finish_" + tag)

    send1, recv1, bufs = in_flight[0]
    proj0 = _mm_plane(h0, bufs[0].reshape(4, d, e), chip_op, "sc_in_own")
    for j, (px, py) in enumerate(_other_chips(xi, yi)):
        send2, recv2, bufs, passed = _gather_forward(send1, recv1, bufs, proj0, "ag_forward_sc_w_in_%d" % j,
                                                     sources=(j,))
        bufs = _gather_finish(send2, recv2, bufs, passed, "ag_finish_sc_w_in_%d" % j, sources=(j,))
        plane = jnp.reshape(2 * px + py, (1,)).astype(jnp.int32)
        proj0 = _mm_plane(h0, bufs[0].reshape(4, d, e), plane, "sc_in_%d" % j, planes_so_far=proj0)
    w_in0 = bufs[0].reshape(4, d, e)
    w_out0 = arrived(1, proj0, "sc_w_out")[0].reshape(1, e, d)
    y0 = _sc_fwd(proj0, conv_w0, "sc_mix")
    o0 = _mm(y0[None], w_out0, "nn", F32, "sc_out")[0]
    x1, h1 = _norm_mod_fwd(x2d, ng[1], scale[1], shift[1], "norm1", o=o0, gate=gate[0])
    lru_ws = arrived(2, h1, "lru_w_in")
    w_in1 = lru_ws[0].reshape(4, d, e // 2)
    w_a = lru_ws[1].reshape(4, n_heads, hj, HEAD_DIM)
    w_x = lru_ws[2].reshape(4, n_heads, hj, HEAD_DIM)
    proj1 = _mm(h1[None], w_in1, "nn", BF16, "lru_in")
    pairs1 = proj1.reshape(2, 2, s, e // 2)
    y1, *lru_saved = _lru_fwd(pairs1, conv_w1, conv_b1, w_a, b_a, w_x, b_x, lam, "lru_mix")
    w_out1 = arrived(3, y1, "lru_w_out")[0].reshape(1, e, d)
    o1 = _mm(y1[None], w_out1, "nn", F32, "lru_out")[0]
    dx2, do1, acc_f = _final_loss(x1, o1, gate[1], row(final_g), tgt, "final_loss")

    def reduce_stage1(tag, grads):
        lands = [lax.empty((4, g.shape[1] // 2, g.shape[2]), F32) for g in grads]
        return _exchange_start("rs_sibling_start_" + tag, grads, lands, len(grads),
                               _plan_other_half_to_sibling, None)

    def reduce_stage2(tag, stage1, nms, after):
        send, recv, grads, lands, _ = stage1
        grads, lands = _exchange_wait("rs_sibling_wait_" + tag, send, recv, grads, lands,
                                      _plan_other_half_to_sibling, after)
        parts = [_sum_own_and_sibling(core_op, g, r1, "rs_sum1_" + nm) for g, r1, nm in zip(grads, lands, nms)]
        lands = [lax.empty((3,) + p.shape[1:], BF16) for p in parts]
        return _exchange_start("rs_chips_start_" + tag, parts, lands, 3 * len(parts),
                               _plan_partials_to_chips, None)

    def reduce_stage3(tag, stage2, nms, after):
        send, recv, parts, lands, _ = stage2
        parts, lands = _exchange_wait("rs_chips_wait_" + tag, send, recv, parts, lands,
                                      _plan_partials_to_chips, after)
        halves = [_sum_chips(where_op, p, r2, "rs_sum2_" + nm) for p, r2, nm in zip(parts, lands, nms)]
        return _exchange_start("rs_share_start_" + tag, [], halves, len(halves), _plan_share_half, None)

    def reduce_done(tag, stage3, after):
        send, recv, _, fulls, _ = stage3
        return _exchange_wait("rs_share_wait_" + tag, send, recv, [], fulls, _plan_share_half, after)[1]

    def head_major_to_chip_major(t):
        return jnp.transpose(t.reshape(n_heads, 4, hj, HEAD_DIM), (1, 0, 2, 3)).reshape(4, n_heads * hj, HEAD_DIM)

    big_state = dict(zip(names, zip(shards, [m_sc_w_in, m_sc_w_out, m_lru_w_in, m_lru_w_a, m_lru_w_x, m_lru_w_out],
                                    [v_sc_w_in, v_sc_w_out, v_lru_w_in, v_lru_w_a, v_lru_w_x, v_lru_w_out],
                                    [sc_w_in, sc_w_out, lru_w_in, lru_w_a, lru_w_x, lru_w_out])))
    big = {}

    def update(nms, fulls, after):
        for nm, g2 in zip(nms, fulls):
            w2, m4, v4, w4 = big_state[nm]
            outs = _adamw(w2, g2, m4.reshape(w2.shape), v4.reshape(w2.shape), "adamw_" + nm, after=after)
            big[nm] = tuple(t.reshape(w4.shape) for t in outs)
            after = outs[1]
        return after

    g_w_out1 = _mm(y1[None], do1[None], "tn", F32, "lru_out_dw", bm=512, bn=2048)
    dy1 = _mm(do1[None], w_out1, "nt", BF16, "lru_out_dx")[0]
    dpairs1, g_wa, g_wx, g_ba, g_bx, g_lam, g_cw1, g_cb1 = _lru_bwd(
        pairs1, dy1, lru_saved, conv_w1, conv_b1, w_a, b_a, w_x, b_x, lam, "lru_mix_bwd")
    dproj1 = dpairs1.reshape(4, s, e // 2)
    g_w_in1 = _mm(h1[None], dproj1, "tn", F32, "lru_in_dw", bm=512, bn=2048)
    lru_names = ["lru_w_out", "lru_w_a", "lru_w_x", "lru_w_in"]
    lru_rs = reduce_stage1("lru", [g_w_out1.reshape(4, es, d), head_major_to_chip_major(g_wa),
                                   head_major_to_chip_major(g_wx), g_w_in1])
    dh1 = _mm(dproj1, w_in1, "nt", F32, "lru_in_dx", after=lru_rs[4])[0]
    lru_rs = reduce_stage2("lru", lru_rs, lru_names, dh1)
    dx1, do0, acc1 = _norm_mod_bwd(dh1, x1, ng[1], scale[1], dx2, "norm1_bwd", o_prev=o0, gate_prev=gate[0],
                                   after=lru_rs[4])

    g_w_out0 = _mm(y0[None], do0[None], "tn", F32, "sc_out_dw", bm=512, bn=2048)
    out_rs = reduce_stage1("sc_out", [g_w_out0.reshape(4, es, d)])
    dy0 = _mm(do0[None], w_out0, "nt", BF16, "sc_out_dx", after=out_rs[4])[0]
    out_rs = reduce_stage2("sc_out", out_rs, ["sc_w_out"], dy0)
    dproj0, g_cw0 = _sc_bwd(proj0, dy0, conv_w0, "sc_mix_bwd", after=out_rs[4])
    stage1, stage2, after = [], [], None
    for k in range(IN_PIECES):
        g_piece = _mm(h0[None], dproj0, "tn", F32, "sc_in_dw_%d" % k, bm=512, bn=e // IN_PIECES,
                      col_blocks=(k, 1), after=after)
        stage1.append(reduce_stage1("sc_in_%d" % k, [g_piece]))
        after = stage1[k][4]
        if k:
            stage2.append(reduce_stage2("sc_in_%d" % (k - 1), stage1[k - 1], ["sc_w_in_%d" % (k - 1)], after))
            after = stage2[k - 1][4]
    lru_rs = reduce_stage3("lru", lru_rs, lru_names, after)
    k = IN_PIECES - 1
    stage2.append(reduce_stage2("sc_in_%d" % k, stage1[k], ["sc_w_in_%d" % k], lru_rs[4]))
    dh0 = _mm(dproj0, w_in0, "nt", F32, "sc_in_dx", after=stage2[k][4])[0]
    grad_x, acc0 = _norm_mod_bwd(dh0, x2d, ng[0], scale[0], dx1, "norm0_bwd")
    out_rs = reduce_stage3("sc_out", out_rs, ["sc_w_out"], acc0)
    last = update(lru_names, reduce_done("lru", lru_rs, out_rs[4]), None)
    last = update(["sc_w_out"], reduce_done("sc_out", out_rs, last), None)
    g_half = None
    for k, (send, recv, parts, lands, _) in enumerate(stage2):
        parts, lands = _exchange_wait("rs_chips_wait_sc_in_%d" % k, send, recv, parts, lands,
                                      _plan_partials_to_chips, last)
        g_half = _sum_chips(where_op, parts[0], lands[0], "rs_sum2_sc_w_in_%d" % k, piece=(k, IN_PIECES),
                            so_far=g_half)
    in_rs = _exchange_start("rs_share_start_sc_in", [], [g_half], 1, _plan_share_half, None)

    dmod = jnp.stack([jnp.concatenate([acc0[1], acc0[0], acc1[3]]),
                      jnp.concatenate([acc1[1], acc1[0], acc_f[1]])])
    part_list = [jnp.stack([acc0[2], acc1[2]]), acc_f[0], acc_f[2, :1], g_cw0, g_cw1, g_cb1, g_ba, g_bx,
                 g_lam, dmod]
    partials, poffs, n_part = _pack(part_list)
    every = _allgather8([partials.reshape(8, n_part // 8)], "ag_partials", after=in_rs[4])[0].reshape(8, n_part)
    total = _sum_rows8(every, "sum_partials")[0]

    def tot(k, shape):
        size = 1
        for dim in shape:
            size *= dim
        return total[poffs[k]:poffs[k] + size].reshape(shape)

    def my_cols(t, width):
        return lax.dynamic_slice_in_dim(t, chip * width, width, axis=t.ndim - 1)

    loss = tot(2, (1,))[0]
    g_norm_g, g_final_g, g_ada_b = tot(0, (2, d)), tot(1, (d,)), tot(9, (2, 3 * d))
    g_sc_conv_w = my_cols(tot(3, (3, e)), es)[None]
    g_lru_conv_w = my_cols(tot(4, (4, e)), es)[None]
    g_lru_conv_b = my_cols(tot(5, (1, e)), es)
    g_lru_b_a = my_cols(tot(6, (n_heads, 4 * hj)), hj)[None]
    g_lru_b_x = my_cols(tot(7, (n_heads, 4 * hj)), hj)[None]
    g_lru_lambda = my_cols(tot(8, (1, e)), es)

    dmod_all = every[:, poffs[9]:poffs[9] + 6 * d].reshape(8, 2, 3 * d)
    dmod_mine = jnp.transpose(my_cols(dmod_all, f), (1, 0, 2))
    ada = _ada_bwd_adamw(jnp.transpose(c_all), dmod_mine, ada_w, m_ada_w, v_ada_w, "ada_bwd_adamw", after=total)

    small_names = ["norm_g", "ada_b", "final_g", "sc_conv_w", "lru_conv_w", "lru_conv_b", "lru_b_a",
                   "lru_b_x", "lru_lambda"]
    small_w = [norm_g, ada_b, final_g, sc_conv_w, lru_conv_w, lru_conv_b, lru_b_a, lru_b_x, lru_lambda]
    small_g = [g_norm_g, g_ada_b, g_final_g, g_sc_conv_w, g_lru_conv_w, g_lru_conv_b, g_lru_b_a,
               g_lru_b_x, g_lru_lambda]
    small_m = [m_norm_g, m_ada_b, m_final_g, m_sc_conv_w, m_lru_conv_w, m_lru_conv_b, m_lru_b_a,
               m_lru_b_x, m_lru_lambda]
    small_v = [v_norm_g, v_ada_b, v_final_g, v_sc_conv_w, v_lru_conv_w, v_lru_conv_b, v_lru_b_a,
               v_lru_b_x, v_lru_lambda]
    pw, soffs, n_s = _pack(small_w)
    pg, pm, pv = _pack(small_g)[0], _pack(small_m)[0], _pack(small_v)[0]
    shape2 = (n_s // PACK, PACK)
    _, pd, pm2, pv2 = _adamw(pw.reshape(shape2), pg.reshape(shape2), pm.reshape(shape2), pv.reshape(shape2),
                             "adamw_small", after=ada[0])
    update(["sc_w_in"], reduce_done("sc_in", in_rs, pd), None)
    small = {}
    for k, (nm, w_) in enumerate(zip(small_names, small_w)):
        take = lambda t: t.reshape(-1)[soffs[k]:soffs[k] + w_.size].reshape(w_.shape)
        small[nm] = (small_g[k].reshape(w_.shape), take(pd), take(pm2), take(pv2))

    results = dict(small)
    results.update(big)
    results["ada_w"] = tuple(ada)
    order = ["norm_g", "ada_w", "ada_b", "sc_w_in", "sc_conv_w", "sc_w_out", "lru_w_in", "lru_conv_w",
             "lru_conv_b", "lru_w_a", "lru_b_a", "lru_w_x", "lru_b_x", "lru_lambda", "lru_w_out", "final_g"]
    out = [loss, grad_x[None]]
    for kind in range(4):
        out += [results[nm][kind] for nm in order]
    return tuple(out)
```

```python
import functools

import jax
import jax.numpy as jnp
from jax import lax
from jax.experimental import pallas as pl
from jax.experimental.pallas import tpu as pltpu

F32 = jnp.float32
BF16 = jnp.bfloat16
MESH = pl.DeviceIdType.MESH
ANY = pl.BlockSpec(memory_space=pl.ANY)

RMS_EPS = 1e-6
RGLRU_C = 8.0
HEAD_DIM = 256
ADAM_LR = 0.001
ADAM_B1 = 0.9
ADAM_B2 = 0.999
ADAM_EPS = 1e-08
ADAM_WD = 0.01
ADAM_STEP = 10
V7X_VMEM_LIMIT = 56 * 1024 * 1024
IN_PIECES = 2
LANES = 128
SUBLANES = 8
PACK = SUBLANES * LANES


def _blk(dim, pref, unit=LANES):
    if dim <= pref:
        return dim
    b = (pref // unit) * unit
    while b > unit and dim % b:
        b -= unit
    assert dim % b == 0, (dim, pref, unit)
    return b


def _params(sem=None):
    return pltpu.CompilerParams(dimension_semantics=sem, vmem_limit_bytes=V7X_VMEM_LIMIT)


def _ordered(body, n_in, after):
    if after is None:
        return body, [], []

    def ordered_body(*refs):
        return body(*refs[:n_in], *refs[n_in + 1:])

    return ordered_body, [ANY], [after]


def _pos():
    return lax.axis_index("x"), lax.axis_index("y"), lax.axis_index("c")


def _other_chips(x, y):
    return [(1 - x, y), (x, 1 - y), (1 - x, 1 - y)]


def _allgather8(arrs, name, after=None):
    n_t = len(arrs)
    ms = [a.shape[0] for a in arrs]

    def gather(*refs):
        ins, outs = refs[:n_t], refs[n_t:2 * n_t]
        send_sems, recv_sems, local_sems = refs[2 * n_t:]
        x, y, c = _pos()
        me, sibling = (x, y, c), (x, y, 1 - c)
        chips = _other_chips(x, y)

        def rows(t, px, py, pc):
            return outs[t].at[pl.ds((4 * px + 2 * py + pc) * ms[t], ms[t])]

        def copy(t, k, block, to, src=None):
            return pltpu.make_async_remote_copy(
                src_ref=rows(t, *block) if src is None else src, dst_ref=rows(t, *block),
                send_sem=send_sems.at[7 * t + k], recv_sem=recv_sems.at[7 * t + k],
                device_id=to, device_id_type=MESH)

        mine, first, passed = [], [], []
        for t in range(n_t):
            src = ins[t]
            cp = pltpu.make_async_copy(src, rows(t, *me), local_sems.at[t])
            cp.start()
            mine.append(cp)
            sends = [copy(t, 0, me, sibling, src=src)]
            sends += [copy(t, 1 + j, me, (*chip, c), src=src) for j, chip in enumerate(chips)]
            for cp in sends:
                cp.start()
            first += sends
        for t in range(n_t):
            for j, chip in enumerate(chips):
                copy(t, 1 + j, (*chip, c), me).wait_recv()
                cp = copy(t, 4 + j, (*chip, c), sibling)
                cp.start()
                passed.append(cp)
        for t in range(n_t):
            copy(t, 0, sibling, me).wait_recv()
            for j, chip in enumerate(chips):
                copy(t, 4 + j, (*chip, 1 - c), me).wait_recv()
        for cp in first + passed:
            cp.wait_send()
        for cp in mine:
            cp.wait()

    body, extra_specs, extra = _ordered(gather, n_t, after)
    return pl.pallas_call(
        body, name=name,
        out_shape=[jax.ShapeDtypeStruct((8 * m, a.shape[1]), a.dtype) for m, a in zip(ms, arrs)],
        in_specs=[ANY] * n_t + extra_specs, out_specs=[ANY] * n_t,
        scratch_shapes=[pltpu.SemaphoreType.DMA((7 * n_t,)), pltpu.SemaphoreType.DMA((7 * n_t,)),
                        pltpu.SemaphoreType.DMA((n_t,))],
    )(*arrs, *extra)


HBM = pl.BlockSpec(memory_space=pltpu.HBM)
SEM = pl.BlockSpec(memory_space=pltpu.SEMAPHORE)
TOKEN = pl.BlockSpec(memory_space=pltpu.VMEM)
IN_FLIGHT = pltpu.CompilerParams(has_side_effects=pltpu.SideEffectType.DATAFLOW_SIDE_EFFECTING)


def _in_hbm(arrs):
    return [pltpu.with_memory_space_constraint(a, pltpu.HBM) for a in arrs]


def _shard_rows(ref, h, px, py, pc):
    return ref.at[pl.ds((4 * px + 2 * py + pc) * h, h)]


def _gather_start(groups, after, name):
    bufs = [b for grp in groups for b in grp]
    n_t, n_g = len(bufs), len(groups)

    def body(*refs):
        sems, thru = refs[n_t + 1:n_t + 1 + 2 * n_g], refs[n_t + 1 + 2 * n_g:2 * n_t + 1 + 2 * n_g]
        token = refs[-1]
        x, y, c = _pos()
        t = 0
        for g, grp in enumerate(groups):
            for i in range(len(grp)):
                h = bufs[t].shape[0] // 8
                rows = _shard_rows(thru[t], h, x, y, c)
                for j, chip in enumerate(_other_chips(x, y)):
                    pltpu.make_async_remote_copy(
                        src_ref=rows, dst_ref=rows, send_sem=sems[2 * g].at[3 * i + j],
                        recv_sem=sems[2 * g + 1].at[3 * i + j], device_id=(*chip, c),
                        device_id_type=MESH).start()
                t += 1
        token[...] = jnp.zeros_like(token)

    sem_shapes = []
    for grp in groups:
        sem_shapes += [pltpu.SemaphoreType.DMA((3 * len(grp),))] * 2
    out = pl.pallas_call(
        body, name=name,
        out_shape=sem_shapes + [pltpu.HBM(b.shape, b.dtype) for b in bufs] + [jax.ShapeDtypeStruct((8, LANES), F32)],
        in_specs=[HBM] * n_t + [ANY], out_specs=[SEM] * (2 * n_g) + [HBM] * n_t + [TOKEN],
        input_output_aliases={t: 2 * n_g + t for t in range(n_t)},
        compiler_params=IN_FLIGHT,
    )(*_in_hbm(bufs), after)
    sems, thru, token = out[:2 * n_g], out[2 * n_g:2 * n_g + n_t], out[-1]
    per_group, t = [], 0
    for g, grp in enumerate(groups):
        per_group.append((sems[2 * g], sems[2 * g + 1], thru[t:t + len(grp)]))
        t += len(grp)
    return per_group, token


def _gather_forward(send_sems, recv_sems, bufs, after, name, sources=(0, 1, 2)):
    n_t = len(bufs)

    def body(*refs):
        ins = refs[:n_t]
        send1, recv1 = refs[n_t], refs[n_t + 1]
        send2, recv2 = refs[n_t + 3], refs[n_t + 4]
        token = refs[-1]
        x, y, c = _pos()
        chips = _other_chips(x, y)
        for t in range(n_t):
            h = bufs[t].shape[0] // 8
            mine = _shard_rows(ins[t], h, x, y, c)
            for j in sources:
                chip = chips[j]
                landed = _shard_rows(ins[t], h, *chip, c)
                pltpu.make_async_remote_copy(
                    src_ref=mine, dst_ref=landed, send_sem=send1.at[3 * t + j], recv_sem=recv1.at[3 * t + j],
                    device_id=(*chip, c), device_id_type=MESH).wait_recv()
                pltpu.make_async_remote_copy(
                    src_ref=landed, dst_ref=landed, send_sem=send2.at[3 * t + j], recv_sem=recv2.at[3 * t + j],
                    device_id=(x, y, 1 - c), device_id_type=MESH).start()
        for t in range(n_t):
            h = bufs[t].shape[0] // 8
            mine = _shard_rows(ins[t], h, x, y, c)
            for j in sources:
                pltpu.make_async_remote_copy(
                    src_ref=mine, dst_ref=mine, send_sem=send1.at[3 * t + j], recv_sem=recv1.at[3 * t + j],
                    device_id=(*chips[j], c), device_id_type=MESH).wait_send()
        token[...] = jnp.zeros_like(token)

    out = pl.pallas_call(
        body, name=name,
        out_shape=[pltpu.SemaphoreType.DMA((3 * n_t,))] * 2 + [pltpu.HBM(b.shape, b.dtype) for b in bufs]
        + [jax.ShapeDtypeStruct((8, LANES), F32)],
        in_specs=[HBM] * n_t + [SEM, SEM, ANY], out_specs=[SEM, SEM] + [HBM] * n_t + [TOKEN],
        input_output_aliases={t: 2 + t for t in range(n_t)},
        compiler_params=IN_FLIGHT,
    )(*bufs, send_sems, recv_sems, after)
    return out[0], out[1], out[2:2 + n_t], out[-1]


def _gather_finish(send_sems, recv_sems, bufs, after, name, sources=(0, 1, 2)):
    n_t = len(bufs)

    def body(*refs):
        ins = refs[:n_t]
        send2, recv2 = refs[n_t], refs[n_t + 1]
        x, y, c = _pos()
        chips = _other_chips(x, y)
        for t in range(n_t):
            h = bufs[t].shape[0] // 8
            for j in sources:
                chip = chips[j]
                sent = _shard_rows(ins[t], h, *chip, c)
                got = _shard_rows(ins[t], h, *chip, 1 - c)
                cp = pltpu.make_async_remote_copy(
                    src_ref=sent, dst_ref=got, send_sem=send2.at[3 * t + j], recv_sem=recv2.at[3 * t + j],
                    device_id=(x, y, 1 - c), device_id_type=MESH)
                cp.wait_send()
                cp.wait_recv()

    return pl.pallas_call(
        body, name=name, out_shape=[pltpu.HBM(b.shape, b.dtype) for b in bufs],
        in_specs=[HBM] * n_t + [SEM, SEM, ANY], out_specs=[HBM] * n_t,
        input_output_aliases={t: t for t in range(n_t)},
        compiler_params=IN_FLIGHT,
    )(*bufs, send_sems, recv_sems, after)


def _part_rows(ref, h, px, py, pc, which):
    return ref.at[pl.ds((4 * px + 2 * py + pc) * h + which * (h // 2), h // 2)]


def _remote(src, dst, send_sem, recv_sem, device):
    return pltpu.make_async_remote_copy(src_ref=src, dst_ref=dst, send_sem=send_sem, recv_sem=recv_sem,
                                        device_id=device, device_id_type=MESH)


def _first_start(buf, after, name):
    h = buf.shape[0] // 8

    def body(buf_in, after_ref, send, recv, thru, token):
        x, y, c = _pos()
        rows = _shard_rows(thru, h, x, y, c)
        for j, chip in enumerate(_other_chips(x, y)[:2]):
            _remote(rows, rows, send.at[j], recv.at[j], (*chip, c)).start()
        token[...] = jnp.zeros_like(token)

    return pl.pallas_call(
        body, name=name,
        out_shape=[pltpu.SemaphoreType.DMA((2,))] * 2 + [pltpu.HBM(buf.shape, buf.dtype),
                                                         jax.ShapeDtypeStruct((8, LANES), F32)],
        in_specs=[HBM, ANY], out_specs=[SEM, SEM, HBM, TOKEN], input_output_aliases={0: 2},
        compiler_params=IN_FLIGHT,
    )(*_in_hbm([buf]), after)


def _first_relay(send_a, recv_a, buf, j, after, name):
    h = buf.shape[0] // 8

    def body(buf_in, send_a, recv_a, after_ref, send_b, recv_b, send_c, recv_c, thru, token):
        x, y, c = _pos()
        chips = _other_chips(x, y)
        nbr, other = chips[j], chips[1 - j]
        mine, landed = _shard_rows(buf_in, h, x, y, c), _shard_rows(buf_in, h, *nbr, c)
        _remote(mine, landed, send_a.at[j], recv_a.at[j], (*nbr, c)).wait_recv()
        _remote(landed, landed, send_b.at[0], recv_b.at[0], (x, y, 1 - c)).start()
        part = _part_rows(buf_in, h, *nbr, c, j)
        _remote(part, part, send_c.at[0], recv_c.at[0], (*other, c)).start()
        _remote(mine, mine, send_a.at[j], recv_a.at[j], (*nbr, c)).wait_send()
        token[...] = jnp.zeros_like(token)

    out = pl.pallas_call(
        body, name=name,
        out_shape=[pltpu.SemaphoreType.DMA((1,))] * 4 + [pltpu.HBM(buf.shape, buf.dtype),
                                                         jax.ShapeDtypeStruct((8, LANES), F32)],
        in_specs=[HBM, SEM, SEM, ANY], out_specs=[SEM] * 4 + [HBM, TOKEN], input_output_aliases={0: 4},
        compiler_params=IN_FLIGHT,
    )(buf, send_a, recv_a, after)
    return tuple(out)


def _first_relay_done(send_b, recv_b, buf, j, after, name):
    h = buf.shape[0] // 8

    def body(buf_in, send_b, recv_b, after_ref, thru):
        x, y, c = _pos()
        nbr = _other_chips(x, y)[j]
        cp = _remote(_shard_rows(buf_in, h, *nbr, c), _shard_rows(buf_in, h, *nbr, 1 - c),
                     send_b.at[0], recv_b.at[0], (x, y, 1 - c))
        cp.wait_send()
        cp.wait_recv()

    return pl.pallas_call(
        body, name=name, out_shape=pltpu.HBM(buf.shape, buf.dtype),
        in_specs=[HBM, SEM, SEM, ANY], out_specs=HBM, input_output_aliases={0: 0},
        compiler_params=IN_FLIGHT,
    )(buf, send_b, recv_b, after)


def _first_diagonal(relays, buf, after, name):
    h = buf.shape[0] // 8

    def body(buf_in, send_c0, recv_c0, send_c1, recv_c1, after_ref, send_b, recv_b, thru, token):
        x, y, c = _pos()
        chips = _other_chips(x, y)
        diag = chips[2]
        for j, (send_c, recv_c) in enumerate(((send_c0, recv_c0), (send_c1, recv_c1))):
            part = _part_rows(buf_in, h, *diag, c, j)
            _remote(part, part, send_c.at[0], recv_c.at[0], (*chips[1 - j], c)).wait_recv()
        whole = _shard_rows(buf_in, h, *diag, c)
        _remote(whole, whole, send_b.at[0], recv_b.at[0], (x, y, 1 - c)).start()
        token[...] = jnp.zeros_like(token)

    out = pl.pallas_call(
        body, name=name,
        out_shape=[pltpu.SemaphoreType.DMA((1,))] * 2 + [pltpu.HBM(buf.shape, buf.dtype),
                                                         jax.ShapeDtypeStruct((8, LANES), F32)],
        in_specs=[HBM] + [SEM] * 4 + [ANY], out_specs=[SEM, SEM, HBM, TOKEN], input_output_aliases={0: 2},
        compiler_params=IN_FLIGHT,
    )(buf, relays[0][0], relays[0][1], relays[1][0], relays[1][1], after)
    return tuple(out)


def _first_diagonal_done(send_b, recv_b, relays, buf, after, name):
    h = buf.shape[0] // 8

    def body(buf_in, send_b, recv_b, send_c0, recv_c0, send_c1, recv_c1, after_ref, thru):
        x, y, c = _pos()
        chips = _other_chips(x, y)
        diag = chips[2]
        cp = _remote(_shard_rows(buf_in, h, *diag, c), _shard_rows(buf_in, h, *diag, 1 - c),
                     send_b.at[0], recv_b.at[0], (x, y, 1 - c))
        cp.wait_send()
        cp.wait_recv()
        for j, (send_c, recv_c) in enumerate(((send_c0, recv_c0), (send_c1, recv_c1))):
            part = _part_rows(buf_in, h, *chips[j], c, j)
            _remote(part, part, send_c.at[0], recv_c.at[0], (*chips[1 - j], c)).wait_send()

    return pl.pallas_call(
        body, name=name, out_shape=pltpu.HBM(buf.shape, buf.dtype),
        in_specs=[HBM] + [SEM] * 6 + [ANY], out_specs=HBM, input_output_aliases={0: 0},
        compiler_params=IN_FLIGHT,
    )(buf, send_b, recv_b, relays[0][0], relays[0][1], relays[1][0], relays[1][1], after)


def _exchange_start(name, srcs, lands, n_copies, plan, after):
    ns, nl = len(srcs), len(lands)
    extra = [] if after is None else [after]

    def body(*refs):
        base = ns + nl + len(extra)
        send_sems, recv_sems = refs[base], refs[base + 1]
        src_refs, land_refs = refs[base + 2:base + 2 + ns], refs[base + 2 + ns:base + 2 + ns + nl]
        token = refs[-1]
        x, y, c = _pos()
        copies = plan(src_refs, land_refs, x, y, c)
        assert len(copies) == n_copies
        for k, (src, dst, dev) in enumerate(copies):
            pltpu.make_async_remote_copy(
                src_ref=src, dst_ref=dst, send_sem=send_sems.at[k], recv_sem=recv_sems.at[k],
                device_id=dev, device_id_type=MESH).start()
        token[...] = jnp.zeros_like(token)

    out = pl.pallas_call(
        body, name=name,
        out_shape=[pltpu.SemaphoreType.DMA((n_copies,))] * 2
        + [pltpu.HBM(a.shape, a.dtype) for a in list(srcs) + list(lands)] + [jax.ShapeDtypeStruct((8, LANES), F32)],
        in_specs=[HBM] * (ns + nl) + [ANY] * len(extra), out_specs=[SEM, SEM] + [HBM] * (ns + nl) + [TOKEN],
        input_output_aliases={i: 2 + i for i in range(ns + nl)},
        compiler_params=IN_FLIGHT,
    )(*_in_hbm(list(srcs) + list(lands)), *extra)
    return out[0], out[1], out[2:2 + ns], out[2 + ns:2 + ns + nl], out[-1]


def _exchange_wait(name, send_sems, recv_sems, srcs, lands, plan, after):
    ns, nl = len(srcs), len(lands)

    def body(*refs):
        src_refs, land_refs = refs[:ns], refs[ns:ns + nl]
        send, recv = refs[ns + nl], refs[ns + nl + 1]
        x, y, c = _pos()
        for k, (src, dst, dev) in enumerate(plan(src_refs, land_refs, x, y, c)):
            cp = pltpu.make_async_remote_copy(
                src_ref=src, dst_ref=dst, send_sem=send.at[k], recv_sem=recv.at[k],
                device_id=dev, device_id_type=MESH)
            cp.wait_send()
            cp.wait_recv()

    out = pl.pallas_call(
        body, name=name, out_shape=[pltpu.HBM(a.shape, a.dtype) for a in list(srcs) + list(lands)],
        in_specs=[HBM] * (ns + nl) + [SEM, SEM, ANY], out_specs=[HBM] * (ns + nl),
        input_output_aliases={i: i for i in range(ns + nl)},
        compiler_params=IN_FLIGHT,
    )(*srcs, *lands, send_sems, recv_sems, after)
    return out[:ns], out[ns:]


def _plan_other_half_to_sibling(src_refs, land_refs, x, y, c):
    out = []
    for g_ref, r_ref in zip(src_refs, land_refs):
        h = g_ref.shape[1] // 2
        out.append((g_ref.at[:, pl.ds((1 - c) * h, h), :], r_ref, (x, y, 1 - c)))
    return out


def _plan_partials_to_chips(src_refs, land_refs, x, y, c):
    out = []
    for p_ref, r_ref in zip(src_refs, land_refs):
        for j, (px, py) in enumerate(_other_chips(x, y)):
            out.append((p_ref.at[2 * px + py], r_ref.at[j], (px, py, c)))
    return out


def _plan_share_half(src_refs, land_refs, x, y, c):
    out = []
    for f_ref in land_refs:
        h = f_ref.shape[0] // 2
        rows = f_ref.at[pl.ds(c * h, h)]
        out.append((rows, rows, (x, y, 1 - c)))
    return out


def _cast_into_slot(chip, w, name):
    r, c = w.shape
    br, bc = _blk(r, 512, 8), _blk(c, 2048)
    nb = r // br

    def body(chip_ref, w_ref, o_ref):
        o_ref[...] = w_ref[...].astype(BF16)

    grid_spec = pltpu.PrefetchScalarGridSpec(
        num_scalar_prefetch=1, grid=(nb, c // bc),
        in_specs=[pl.BlockSpec((br, bc), lambda i, j, chip_ref: (i, j))],
        out_specs=pl.BlockSpec((br, bc), lambda i, j, chip_ref: (chip_ref[0] * nb + i, j)))
    return pl.pallas_call(
        body, name=name, grid_spec=grid_spec,
        out_shape=jax.ShapeDtypeStruct((4 * r, c), BF16),
        compiler_params=_params(("parallel", "parallel")),
    )(chip, w)


def _sum_own_and_sibling(core, g, r1, name):
    _, r, c = g.shape
    h = r // 2
    br, bc = _blk(h, 256, 8), _blk(c, 2048)
    nb = h // br

    def body(core_ref, g_ref, r_ref, o_ref):
        o_ref[...] = (g_ref[...] + r_ref[...]).astype(BF16)

    grid_spec = pltpu.PrefetchScalarGridSpec(
        num_scalar_prefetch=1, grid=(4, nb, c // bc),
        in_specs=[pl.BlockSpec((None, br, bc), lambda k, i, j, core_ref: (k, core_ref[0] * nb + i, j)),
                  pl.BlockSpec((None, br, bc), lambda k, i, j, core_ref: (k, i, j))],
        out_specs=pl.BlockSpec((None, br, bc), lambda k, i, j, core_ref: (k, i, j)))
    return pl.pallas_call(
        body, name=name, grid_spec=grid_spec,
        out_shape=jax.ShapeDtypeStruct((4, h, c), BF16),
        compiler_params=_params(("parallel", "parallel", "parallel")),
    )(core, g, r1)


def _sum_chips(where, p, r2, name, piece=(0, 1), so_far=None):
    _, h, c = p.shape
    k, n = piece
    br, bc = _blk(h, 256, 8), _blk(c, 2048)
    nb, ncb = h // br, c // bc

    def body(where_ref, p_ref, r_ref, *rest):
        acc = p_ref[...].astype(F32)
        for j in range(3):
            acc = acc + r_ref[j].astype(F32)
        rest[-1][...] = acc

    extra_specs, extra, aliases = ([], [], {}) if so_far is None else ([ANY], [so_far], {3: 0})
    grid_spec = pltpu.PrefetchScalarGridSpec(
        num_scalar_prefetch=1, grid=(nb, ncb),
        in_specs=[pl.BlockSpec((None, br, bc), lambda i, j, where_ref: (where_ref[0], i, j)),
                  pl.BlockSpec((3, br, bc), lambda i, j, where_ref: (0, i, j))] + extra_specs,
        out_specs=pl.BlockSpec((br, bc), lambda i, j, where_ref: (where_ref[1] * nb + i, k * ncb + j)))
    return pl.pallas_call(
        body, name=name, grid_spec=grid_spec,
        out_shape=jax.ShapeDtypeStruct((2 * h, n * c), F32), input_output_aliases=aliases,
        compiler_params=_params(("parallel", "parallel")),
    )(where, p, r2, *extra)


def _adamw_math(w, g, m, v):
    m2 = ADAM_B1 * m + (1.0 - ADAM_B1) * g
    v2 = ADAM_B2 * v + (1.0 - ADAM_B2) * (g * g)
    m_hat = m2 / (1.0 - ADAM_B1 ** ADAM_STEP)
    v_hat = v2 / (1.0 - ADAM_B2 ** ADAM_STEP)
    delta = -ADAM_LR * (m_hat / (jnp.sqrt(v_hat) + ADAM_EPS) + ADAM_WD * w)
    return delta, m2, v2


def _adamw(w, g, m, v, name, after=None):
    r, c = w.shape
    br, bc = _blk(r, 128, 8), _blk(c, 2048)

    def body(w_ref, g_ref, m_ref, v_ref, go_ref, d_ref, m2_ref, v2_ref):
        gv = g_ref[...]
        d, m2, v2 = _adamw_math(w_ref[...], gv, m_ref[...], v_ref[...])
        go_ref[...] = gv
        d_ref[...] = d
        m2_ref[...] = m2
        v2_ref[...] = v2

    spec = pl.BlockSpec((br, bc), lambda i, j: (i, j))
    body, extra_specs, extra = _ordered(body, 4, after)
    return pl.pallas_call(
        body, name=name, grid=(r // br, c // bc),
        in_specs=[spec] * 4 + extra_specs, out_specs=[spec] * 4,
        out_shape=[jax.ShapeDtypeStruct((r, c), F32)] * 4,
        compiler_params=_params(("parallel", "parallel")),
    )(w, g, m, v, *extra)


def _sum_rows8(g, name, after=None):
    n = g.shape[1]

    def body(g_ref, o_ref):
        acc = g_ref[0:1, :]
        for k in range(1, 8):
            acc = acc + g_ref[k:k + 1, :]
        o_ref[...] = acc

    body, extra_specs, extra = _ordered(body, 1, after)
    return pl.pallas_call(
        body, name=name, out_shape=jax.ShapeDtypeStruct((1, n), F32),
        in_specs=[pl.BlockSpec(memory_space=pltpu.VMEM)] + extra_specs,
        out_specs=pl.BlockSpec(memory_space=pltpu.VMEM),
        compiler_params=_params(),
    )(g, *extra)


def _mm(a, b, mode, out_dtype, name, bm=1024, bn=None, after=None, col_blocks=None):
    if mode == "nn":
        (_, m, k), (g, _, n) = a.shape, b.shape
    elif mode == "tn":
        (_, k, m), (g, _, n) = a.shape, b.shape
    else:
        (g, m, k), (_, n, _) = a.shape, b.shape
    if bn is None:
        bn = 1024 if k <= 2048 else 512
    bm, bn = _blk(m, bm), _blk(n, bn)

    if mode == "nt":
        def body(a_ref, b_ref, o_ref, acc_ref):
            part = lax.dot_general(a_ref[...], b_ref[...], (((1,), (1,)), ((), ())),
                                   preferred_element_type=F32)
            if g == 1:
                o_ref[...] = part.astype(out_dtype)
            else:
                gi = pl.program_id(2)

                @pl.when(gi == 0)
                def _():
                    acc_ref[...] = part

                @pl.when(gi > 0)
                def _():
                    acc_ref[...] += part

                @pl.when(gi == g - 1)
                def _():
                    o_ref[...] = acc_ref[...].astype(out_dtype)

        body, extra_specs, extra = _ordered(body, 2, after)
        return pl.pallas_call(
            body, name=name, grid=(m // bm, n // bn, g),
            in_specs=[pl.BlockSpec((None, bm, k), lambda i, j, gi: (gi, i, 0)),
                      pl.BlockSpec((None, bn, k), lambda i, j, gi: (gi, j, 0))] + extra_specs,
            out_specs=pl.BlockSpec((None, bm, bn), lambda i, j, gi: (0, i, j)),
            out_shape=jax.ShapeDtypeStruct((1, m, n), out_dtype),
            scratch_shapes=[pltpu.VMEM((bm, bn), F32)],
            compiler_params=_params(("parallel", "parallel", "arbitrary")),
        )(a, b, *extra)

    contract = (((1,), (0,)), ((), ())) if mode == "nn" else (((0,), (0,)), ((), ()))

    def body(a_ref, b_ref, o_ref):
        o_ref[...] = lax.dot_general(a_ref[...], b_ref[...], contract,
                                     preferred_element_type=F32).astype(out_dtype)

    a_spec = (pl.BlockSpec((None, bm, k), lambda i, gi, j: (0, i, 0)) if mode == "nn"
              else pl.BlockSpec((None, k, bm), lambda i, gi, j: (0, 0, i)))
    first, count = (0, n // bn) if col_blocks is None else col_blocks
    body, extra_specs, extra = _ordered(body, 2, after)
    return pl.pallas_call(
        body, name=name, grid=(m // bm, g, count),
        in_specs=[a_spec, pl.BlockSpec((None, k, bn), lambda i, gi, j: (gi, 0, first + j))] + extra_specs,
        out_specs=pl.BlockSpec((None, bm, bn), lambda i, gi, j: (gi, i, j)),
        out_shape=jax.ShapeDtypeStruct((g, m, count * bn), out_dtype),
        compiler_params=_params(("parallel", "parallel", "parallel")),
    )(a, b, *extra)


def _mm_plane(a, b, plane, name, planes_so_far=None, after=None, bm=1024, bn=1024):
    (m, k), (g, _, n) = a.shape, b.shape
    bm, bn = _blk(m, bm), _blk(n, bn)

    def body(plane_ref, a_ref, b_ref, *rest):
        rest[-1][...] = jnp.dot(a_ref[...], b_ref[...], preferred_element_type=F32).astype(BF16)

    extra_specs, extra, aliases = [], [], {}
    if planes_so_far is not None:
        extra_specs.append(ANY)
        extra.append(planes_so_far)
        aliases = {3: 0}
    if after is not None:
        extra_specs.append(ANY)
        extra.append(after)
    grid_spec = pltpu.PrefetchScalarGridSpec(
        num_scalar_prefetch=1, grid=(m // bm, n // bn),
        in_specs=[pl.BlockSpec((bm, k), lambda i, j, p: (i, 0)),
                  pl.BlockSpec((None, k, bn), lambda i, j, p: (p[0], 0, j))] + extra_specs,
        out_specs=pl.BlockSpec((None, bm, bn), lambda i, j, p: (p[0], i, j)))
    return pl.pallas_call(
        body, name=name, grid_spec=grid_spec,
        out_shape=jax.ShapeDtypeStruct((g, m, n), BF16), input_output_aliases=aliases,
        compiler_params=_params(("parallel", "parallel")),
    )(plane, a, b, *extra)


def _row_specs(br, d):
    return (pl.BlockSpec((br, d), lambda i: (i, 0)), pl.BlockSpec((1, d), lambda i: (0, 0)),
            pl.BlockSpec((8, d), lambda i: (0, 0)))


def _rstd(xv):
    return lax.rsqrt(jnp.mean(xv * xv, axis=-1, keepdims=True) + RMS_EPS)


def _colsum(v):
    return jnp.sum(v, axis=0, keepdims=True)


def _norm_mod_fwd(x, g, scale, shift, name, o=None, gate=None, after=None):
    s, d = x.shape
    br = _blk(s, 256, 8)
    has_res = o is not None
    row, vec, _ = _row_specs(br, d)

    def body(*refs):
        if has_res:
            x_ref, o_ref, gate_ref, g_ref, sc_ref, sh_ref, x1_ref, h_ref = refs
            xv = x_ref[...] + gate_ref[...] * o_ref[...]
            x1_ref[...] = xv
        else:
            x_ref, g_ref, sc_ref, sh_ref, h_ref = refs
            xv = x_ref[...]
        n = xv * _rstd(xv) * g_ref[...]
        h_ref[...] = (n * (1.0 + sc_ref[...]) + sh_ref[...]).astype(BF16)

    ins = [x] + ([o, gate] if has_res else []) + [g, scale, shift]
    in_specs = [row] + ([row, vec] if has_res else []) + [vec] * 3
    out_shape = ([jax.ShapeDtypeStruct((s, d), F32)] if has_res else []) + [jax.ShapeDtypeStruct((s, d), BF16)]
    body, extra_specs, extra = _ordered(body, len(ins), after)
    out = pl.pallas_call(
        body, name=name, grid=(s // br,), in_specs=in_specs + extra_specs, out_specs=[row] * len(out_shape),
        out_shape=out_shape, compiler_params=_params(("parallel",)),
    )(*ins, *extra)
    return out if has_res else out[0]


def _final_loss(x1, o1, gate1, final_g, tgt, name):
    s, d = x1.shape
    br = _blk(s, 256, 8)
    row, vec, acc = _row_specs(br, d)

    def body(x1_ref, o_ref, gate_ref, g_ref, t_ref, dx_ref, do_ref, acc_ref):
        @pl.when(pl.program_id(0) == 0)
        def _():
            acc_ref[...] = jnp.zeros_like(acc_ref)

        gate, o, g = gate_ref[...], o_ref[...], g_ref[...]
        x2 = x1_ref[...] + gate * o
        r = _rstd(x2)
        xh = x2 * r
        err = xh * g - t_ref[...]
        loss = 0.5 * _colsum(jnp.mean(err * err, axis=-1, keepdims=True))
        dout = err * (1.0 / d)
        dxh = dout * g
        dx2 = r * (dxh - xh * jnp.mean(dxh * xh, axis=-1, keepdims=True))
        dx_ref[...] = dx2
        do_ref[...] = (dx2 * gate).astype(BF16)
        acc_ref[0:1, :] += _colsum(dout * xh)
        acc_ref[1:2, :] += _colsum(dx2 * o)
        acc_ref[2:3, :] += jnp.broadcast_to(loss, (1, d))

    return pl.pallas_call(
        body, name=name, grid=(s // br,),
        in_specs=[row, row, vec, vec, row], out_specs=[row, row, acc],
        out_shape=[jax.ShapeDtypeStruct((s, d), F32), jax.ShapeDtypeStruct((s, d), BF16),
                   jax.ShapeDtypeStruct((8, d), F32)],
        compiler_params=_params(("arbitrary",)),
    )(x1, o1, gate1, final_g, tgt)


def _norm_mod_bwd(dh, x, g, scale, dx_next, name, o_prev=None, gate_prev=None, after=None):
    s, d = x.shape
    br = _blk(s, 256, 8)
    has_prev = o_prev is not None
    row, vec, acc = _row_specs(br, d)

    def body(*refs):
        if has_prev:
            dh_ref, x_ref, g_ref, sc_ref, dxn_ref, o_ref, gate_ref, dx_ref, do_ref, acc_ref = refs
        else:
            dh_ref, x_ref, g_ref, sc_ref, dxn_ref, dx_ref, acc_ref = refs

        @pl.when(pl.program_id(0) == 0)
        def _():
            acc_ref[...] = jnp.zeros_like(acc_ref)

        xv, gv, dhv = x_ref[...], g_ref[...], dh_ref[...]
        r = _rstd(xv)
        xh = xv * r
        acc_ref[0:1, :] += _colsum(dhv * (xh * gv))
        acc_ref[1:2, :] += _colsum(dhv)
        dn = dhv * (1.0 + sc_ref[...])
        acc_ref[2:3, :] += _colsum(dn * xh)
        dxh = dn * gv
        dx = dxn_ref[...] + r * (dxh - xh * jnp.mean(dxh * xh, axis=-1, keepdims=True))
        dx_ref[...] = dx
        if has_prev:
            acc_ref[3:4, :] += _colsum(dx * o_ref[...])
            do_ref[...] = (dx * gate_ref[...]).astype(BF16)

    ins = [dh, x, g, scale, dx_next] + ([o_prev, gate_prev] if has_prev else [])
    in_specs = [row, row, vec, vec, row] + ([row, vec] if has_prev else [])
    out_shape = [jax.ShapeDtypeStruct((s, d), F32)]
    out_specs = [row]
    if has_prev:
        out_shape.append(jax.ShapeDtypeStruct((s, d), BF16))
        out_specs.append(row)
    out_shape.append(jax.ShapeDtypeStruct((8, d), F32))
    out_specs.append(acc)
    body, extra_specs, extra = _ordered(body, len(ins), after)
    return pl.pallas_call(
        body, name=name, grid=(s // br,), in_specs=in_specs + extra_specs, out_specs=out_specs,
        out_shape=out_shape, compiler_params=_params(("arbitrary",)),
    )(*ins, *extra)


def _tiles(p):
    s, c = p.shape
    return p.reshape(s // SUBLANES, SUBLANES, c)


def _shift_down(p, k):
    if k == 0:
        return p
    r = pltpu.roll(_tiles(p), k, 1)
    before = jnp.concatenate([jnp.zeros_like(r[:1]), r[:-1]], axis=0)
    rows = lax.broadcasted_iota(jnp.int32, r.shape, 1)
    return jnp.where(rows >= k, r, before).reshape(p.shape)


def _shift_up(p, k):
    if k == 0:
        return p
    r = pltpu.roll(_tiles(p), SUBLANES - k, 1)
    after = jnp.concatenate([r[1:], jnp.zeros_like(r[:1])], axis=0)
    rows = lax.broadcasted_iota(jnp.int32, r.shape, 1)
    return jnp.where(rows < SUBLANES - k, r, after).reshape(p.shape)


def _sigmoid(z):
    return 0.5 * (jnp.tanh(0.5 * z) + 1.0)


def _sc_parts(proj_ref, w_ref):
    b, cg, v, g = (proj_ref[i].astype(F32) for i in range(4))
    p = cg * v
    u = w_ref[2:3, :] * p + w_ref[1:2, :] * _shift_down(p, 1) + w_ref[0:1, :] * _shift_down(p, 2)
    return b, cg, v, g, p, u


def _sc_fwd(proj, conv_w, name):
    _, s, e = proj.shape
    bc = _blk(e, 256)

    def body(proj_ref, w_ref, y_ref):
        b, _, _, g, _, u = _sc_parts(proj_ref, w_ref)
        y_ref[...] = (b * u * (g * _sigmoid(g))).astype(BF16)

    return pl.pallas_call(
        body, name=name, grid=(e // bc,),
        in_specs=[pl.BlockSpec((4, s, bc), lambda j: (0, 0, j)), pl.BlockSpec((3, bc), lambda j: (0, j))],
        out_specs=pl.BlockSpec((s, bc), lambda j: (0, j)),
        out_shape=jax.ShapeDtypeStruct((s, e), BF16),
        compiler_params=_params(("parallel",)),
    )(proj, conv_w)


def _sc_bwd(proj, dy, conv_w, name, after=None):
    _, s, e = proj.shape
    bc = _blk(e, 256)

    def body(proj_ref, dy_ref, w_ref, dp_ref, dw_ref):
        b, cg, v, g, p, u = _sc_parts(proj_ref, w_ref)
        dyv = dy_ref[...].astype(F32)
        sig = _sigmoid(g)
        t = dyv * (g * sig)
        du = t * b
        dp_ref[0] = (t * u).astype(BF16)
        dp_ref[3] = (dyv * b * u * (sig * (1.0 + g * (1.0 - sig)))).astype(BF16)
        dpp = w_ref[2:3, :] * du + w_ref[1:2, :] * _shift_up(du, 1) + w_ref[0:1, :] * _shift_up(du, 2)
        dp_ref[1] = (dpp * v).astype(BF16)
        dp_ref[2] = (dpp * cg).astype(BF16)
        dw_ref[2:3, :] = _colsum(du * p)
        dw_ref[1:2, :] = _colsum(du * _shift_down(p, 1))
        dw_ref[0:1, :] = _colsum(du * _shift_down(p, 2))

    body, extra_specs, extra = _ordered(body, 3, after)
    return pl.pallas_call(
        body, name=name, grid=(e // bc,),
        in_specs=[pl.BlockSpec((4, s, bc), lambda j: (0, 0, j)), pl.BlockSpec((s, bc), lambda j: (0, j)),
                  pl.BlockSpec((3, bc), lambda j: (0, j))] + extra_specs,
        out_specs=[pl.BlockSpec((4, s, bc), lambda j: (0, 0, j)), pl.BlockSpec((3, bc), lambda j: (0, j))],
        out_shape=[jax.ShapeDtypeStruct((4, s, e), BF16), jax.ShapeDtypeStruct((3, e), F32)],
        compiler_params=_params(("parallel",)),
    )(proj, dy, conv_w, *extra)


def _softplus_neg(lam):
    u = jnp.exp(-jnp.abs(lam))
    w = 1.0 + u
    log1p = jnp.where(w == 1.0, u, jnp.log(w) * (u / jnp.where(w == 1.0, 1.0, w - 1.0)))
    return jnp.maximum(-lam, 0.0) + log1p


def _one_minus_exp(z):
    series = -z * (1.0 + z * (0.5 + z * (1.0 / 6.0 + z * (1.0 / 24.0))))
    return jnp.where(z > -0.02, series, 1.0 - jnp.exp(z))


def _scan_in_tiles(a, b, reverse):
    shape = a.shape
    a, b = _tiles(a), _tiles(b)
    rows = lax.broadcasted_iota(jnp.int32, a.shape, 1)
    for step in (1, 2, 4):
        shift = SUBLANES - step if reverse else step
        ok = rows < SUBLANES - step if reverse else rows >= step
        a_s, b_s = pltpu.roll(a, shift, 1), pltpu.roll(b, shift, 1)
        b = jnp.where(ok, a * b_s + b, b)
        a = jnp.where(ok, a * a_s, a)
    return a.reshape(shape), b.reshape(shape)


def _scan(a, b, a_ref, b_ref, h_ref, reverse):
    s, c = a.shape
    n = s // 8
    a_t, b_t = _scan_in_tiles(a, b, reverse)
    a_ref[...] = a_t
    b_ref[...] = b_t

    def step(i, carry):
        gi = n - 1 - i if reverse else i
        sl = pl.ds(pl.multiple_of(gi * 8, 8), 8)
        h = b_ref[sl, :] + a_ref[sl, :] * carry
        h_ref[sl, :] = h
        return h[0:1, :] if reverse else h[7:8, :]

    lax.fori_loop(0, n, step, jnp.zeros((1, c), F32))
    return h_ref[...]


def _lru_specs(s, e_half, n_heads):
    hp = e_half // HEAD_DIM
    c = HEAD_DIM
    return dict(
        pair=pl.BlockSpec((2, None, s, c), lambda h: (0, h // hp, 0, h % hp)),
        conv_w=pl.BlockSpec((4, c), lambda h: (0, h)),
        chan=pl.BlockSpec((1, c), lambda h: (0, h)),
        w=pl.BlockSpec((4, None, c // 4, c), lambda h: (0, h, 0, 0)),
        bias=pl.BlockSpec((None, 1, c), lambda h: (h, 0, 0)),
        plane=pl.BlockSpec((s, c), lambda h: (0, h)),
    )


def _lru_forward_parts(vp, cw_ref, cb_ref, wa_ref, ba_ref, wx_ref, bx_ref, lam_ref):
    c = HEAD_DIM
    taps = [_shift_down(vp, 3 - k) for k in range(4)]
    v = cb_ref[...] + sum(cw_ref[k:k + 1, :] * taps[k] for k in range(4))
    vb = v.astype(BF16)
    wa = wa_ref[...].reshape(c, c)
    wx = wx_ref[...].reshape(c, c)
    r = _sigmoid(jnp.dot(vb, wa, preferred_element_type=F32) + ba_ref[...])
    i = _sigmoid(jnp.dot(vb, wx, preferred_element_type=F32) + bx_ref[...])
    sp = _softplus_neg(lam_ref[...])
    la = (-RGLRU_C) * sp * r
    a = jnp.exp(la)
    nm = jnp.sqrt(_one_minus_exp(2.0 * la))
    return taps, v, vb, wa, wx, r, i, sp, a, nm


def _lru_fwd(proj, conv_w, conv_b, w_a, b_a, w_x, b_x, lam, name):
    _, _, s, e_half = proj.shape
    n_heads = 2 * e_half // HEAD_DIM
    sp_ = _lru_specs(s, e_half, n_heads)

    def body(pg_ref, cw_ref, cb_ref, wa_ref, ba_ref, wx_ref, bx_ref, lam_ref,
             y_ref, a_ref, hs_ref, sa_ref, sb_ref, sh_ref):
        _, v, _, _, _, _, i, _, a, nm = _lru_forward_parts(
            pg_ref[0].astype(F32), cw_ref, cb_ref, wa_ref, ba_ref, wx_ref, bx_ref, lam_ref)
        hs = _scan(a, nm * (i * v), sa_ref, sb_ref, sh_ref, reverse=False)
        g = pg_ref[1].astype(F32)
        y_ref[...] = (hs * (g * _sigmoid(g))).astype(BF16)
        a_ref[...] = a
        hs_ref[...] = hs.astype(BF16)

    e = 2 * e_half
    return pl.pallas_call(
        body, name=name, grid=(n_heads,),
        in_specs=[sp_["pair"], sp_["conv_w"], sp_["chan"], sp_["w"], sp_["bias"], sp_["w"],
                  sp_["bias"], sp_["chan"]],
        out_specs=[sp_["plane"]] * 3,
        out_shape=[jax.ShapeDtypeStruct((s, e), BF16), jax.ShapeDtypeStruct((s, e), F32),
                   jax.ShapeDtypeStruct((s, e), BF16)],
        scratch_shapes=[pltpu.VMEM((s, HEAD_DIM), F32)] * 3,
        compiler_params=_params(("parallel",)),
    )(proj, conv_w, conv_b, w_a, b_a, w_x, b_x, lam)


def _lru_bwd(proj, dy, saved, conv_w, conv_b, w_a, b_a, w_x, b_x, lam, name):
    _, _, s, e_half = proj.shape
    e = 2 * e_half
    c = HEAD_DIM
    n_heads = e // c
    sp_ = _lru_specs(s, e_half, n_heads)

    def body(pg_ref, dy_ref, a_ref, hs_ref, cw_ref, cb_ref, wa_ref, ba_ref, wx_ref, bx_ref, lam_ref,
             dpg_ref, dwa_ref, dwx_ref, dba_ref, dbx_ref, dlam_ref, dcw_ref, dcb_ref,
             sa_ref, sb_ref, sd_ref):
        vp = pg_ref[0].astype(F32)
        taps = [_shift_down(vp, 3 - k) for k in range(4)]
        v = cb_ref[...] + sum(cw_ref[k:k + 1, :] * taps[k] for k in range(4))
        vb = v.astype(BF16)
        wa = wa_ref[...].reshape(c, c)
        wx = wx_ref[...].reshape(c, c)
        r = _sigmoid(jnp.dot(vb, wa, preferred_element_type=F32) + ba_ref[...])
        i = _sigmoid(jnp.dot(vb, wx, preferred_element_type=F32) + bx_ref[...])
        a, hs = a_ref[...], hs_ref[...].astype(F32)
        sp = _softplus_neg(lam_ref[...])
        nm = jnp.sqrt((1.0 - a) * (1.0 + a))
        iv = i * v
        g = pg_ref[1].astype(F32)
        dyv = dy_ref[...].astype(F32)
        sig = _sigmoid(g)
        dpg_ref[1] = (dyv * hs * (sig * (1.0 + g * (1.0 - sig)))).astype(BF16)
        dh = _scan(_shift_up(a, 1), dyv * (g * sig), sa_ref, sb_ref, sd_ref, reverse=True)
        da = dh * _shift_down(hs, 1)
        div = dh * nm
        a2 = a * a
        dla = da * a - (dh * iv) * (a2 / nm)
        dzr = (dla * ((-RGLRU_C) * sp)) * (r * (1.0 - r))
        dzi = (div * v) * (i * (1.0 - i))
        lam = lam_ref[...]
        dlam_ref[...] = _colsum(dla * ((-RGLRU_C) * r)) * (-_sigmoid(-lam))
        dba_ref[...] = _colsum(dzr)
        dbx_ref[...] = _colsum(dzi)
        dzr_b, dzi_b = dzr.astype(BF16), dzi.astype(BF16)
        tn = (((0,), (0,)), ((), ()))
        nt = (((1,), (1,)), ((), ()))
        dwa_ref[...] = lax.dot_general(vb, dzr_b, tn, preferred_element_type=F32)
        dwx_ref[...] = lax.dot_general(vb, dzi_b, tn, preferred_element_type=F32)
        dv = (div * i + lax.dot_general(dzr_b, wa, nt, preferred_element_type=F32)
              + lax.dot_general(dzi_b, wx, nt, preferred_element_type=F32))
        dcb_ref[...] = _colsum(dv)
        dvp = jnp.zeros_like(dv)
        for k in range(4):
            dvp = dvp + cw_ref[k:k + 1, :] * _shift_up(dv, 3 - k)
            dcw_ref[k:k + 1, :] = _colsum(dv * taps[k])
        dpg_ref[0] = dvp.astype(BF16)

    head_mat = pl.BlockSpec((None, c, c), lambda h: (h, 0, 0))
    outs = pl.pallas_call(
        body, name=name, grid=(n_heads,),
        in_specs=[sp_["pair"]] + [sp_["plane"]] * 3 + [sp_["conv_w"], sp_["chan"], sp_["w"], sp_["bias"],
                                                        sp_["w"], sp_["bias"], sp_["chan"]],
        out_specs=[sp_["pair"], head_mat, head_mat, sp_["bias"], sp_["bias"],
                   sp_["chan"], sp_["conv_w"], sp_["chan"]],
        out_shape=[jax.ShapeDtypeStruct((2, 2, s, e_half), BF16),
                   jax.ShapeDtypeStruct((n_heads, c, c), F32), jax.ShapeDtypeStruct((n_heads, c, c), F32),
                   jax.ShapeDtypeStruct((n_heads, 1, c), F32), jax.ShapeDtypeStruct((n_heads, 1, c), F32),
                   jax.ShapeDtypeStruct((1, e), F32), jax.ShapeDtypeStruct((4, e), F32),
                   jax.ShapeDtypeStruct((1, e), F32)],
        scratch_shapes=[pltpu.VMEM((s, c), F32)] * 3,
        compiler_params=_params(("parallel",)),
    )(proj, dy, *saved, conv_w, conv_b, w_a, b_a, w_x, b_x, lam)
    return tuple(outs)


def _ada_fwd(c_all, ada_w, name):
    n_l, d, f = ada_w.shape
    bf = _blk(f, 512)

    def body(c_ref, w_ref, o_ref):
        cv = c_ref[...]
        sc = (cv * _sigmoid(cv)).astype(BF16)
        o_ref[...] = jnp.dot(sc, w_ref[...].astype(BF16), preferred_element_type=F32)

    return pl.pallas_call(
        body, name=name, grid=(n_l, f // bf),
        in_specs=[pl.BlockSpec((8, d), lambda l, j: (0, 0)), pl.BlockSpec((None, d, bf), lambda l, j: (l, 0, j))],
        out_specs=pl.BlockSpec((None, 8, bf), lambda l, j: (l, 0, j)),
        out_shape=jax.ShapeDtypeStruct((n_l, 8, f), F32),
        compiler_params=_params(("parallel", "parallel")),
    )(c_all, ada_w)


def _ada_bwd_adamw(c_t, dmod, w, m, v, name, after=None):
    n_l, d, f = w.shape
    bf = _blk(f, 256)

    def body(c_ref, dm_ref, w_ref, m_ref, v_ref, g_ref, d_ref, m2_ref, v2_ref):
        cv = c_ref[...]
        sc = cv * _sigmoid(cv)
        dm = dm_ref[...]
        g = sc[:, 0:1] * dm[0:1, :]
        for b in range(1, 8):
            g = g + sc[:, b:b + 1] * dm[b:b + 1, :]
        g_ref[...] = g
        dl, m2, v2 = _adamw_math(w_ref[...], g, m_ref[...], v_ref[...])
        d_ref[...] = dl
        m2_ref[...] = m2
        v2_ref[...] = v2

    big = pl.BlockSpec((None, d, bf), lambda l, j: (l, 0, j))
    body, extra_specs, extra = _ordered(body, 5, after)
    return pl.pallas_call(
        body, name=name, grid=(n_l, f // bf),
        in_specs=[pl.BlockSpec((d, 8), lambda l, j: (0, 0)), pl.BlockSpec((None, 8, bf), lambda l, j: (l, 0, j)),
                  big, big, big] + extra_specs,
        out_specs=[big] * 4, out_shape=[jax.ShapeDtypeStruct((n_l, d, f), F32)] * 4,
        compiler_params=_params(("parallel", "parallel")),
    )(c_t, dmod, w, m, v, *extra)


def _pack(parts):
    padded, offs, n = [], [], 0
    for p in parts:
        p = p.reshape(-1)
        size = -(-p.shape[0] // PACK) * PACK
        offs.append(n)
        n += size
        padded.append(jnp.pad(p, (0, size - p.shape[0])) if size != p.shape[0] else p)
    return jnp.concatenate(padded), offs, n


def kernel(x, c, norm_g, ada_w, ada_b, sc_w_in, sc_conv_w, sc_w_out, lru_w_in, lru_conv_w, lru_conv_b, lru_w_a, lru_b_a, lru_w_x, lru_b_x, lru_lambda, lru_w_out, final_g, loss_target, m_norm_g, m_ada_w, m_ada_b, m_sc_w_in, m_sc_conv_w, m_sc_w_out, m_lru_w_in, m_lru_conv_w, m_lru_conv_b, m_lru_w_a, m_lru_b_a, m_lru_w_x, m_lru_b_x, m_lru_lambda, m_lru_w_out, m_final_g, v_norm_g, v_ada_w, v_ada_b, v_sc_w_in, v_sc_conv_w, v_sc_w_out, v_lru_w_in, v_lru_conv_w, v_lru_conv_b, v_lru_w_a, v_lru_b_a, v_lru_w_x, v_lru_b_x, v_lru_lambda, v_lru_w_out, v_final_g):
    xi, yi, ci = _pos()
    chip = 2 * xi + yi
    batch = 4 * xi + 2 * yi + ci
    core_op = jnp.reshape(ci, (1,)).astype(jnp.int32)
    chip_op = jnp.reshape(chip, (1,)).astype(jnp.int32)
    where_op = jnp.stack([chip, ci]).astype(jnp.int32)

    x2d, tgt = x[0], loss_target[0]
    s, d = x2d.shape
    es = sc_conv_w.shape[2]
    e = 4 * es
    n_heads = lru_w_a.shape[1]
    hj = lru_b_a.shape[2]
    f = ada_w.shape[2]
    row = lambda t: t.reshape(1, -1)

    small_parts = [c, sc_conv_w, lru_conv_w, lru_conv_b, lru_b_a, lru_b_x, lru_lambda]
    small, offs, n_small = _pack(small_parts)
    got = _allgather8([small.reshape(8, n_small // 8)], "ag_small")[0].reshape(8, n_small)
    c_all = got[:, :d]
    per_chip = got[0::2]

    def chip_part(k, shape):
        size = 1
        for dim in shape:
            size *= dim
        return per_chip[:, offs[k]:offs[k] + size].reshape((4,) + shape)

    conv_w0 = jnp.transpose(chip_part(1, (3, es)), (1, 0, 2)).reshape(3, e)
    conv_w1 = jnp.transpose(chip_part(2, (4, es)), (1, 0, 2)).reshape(4, e)
    conv_b1 = chip_part(3, (es,)).reshape(1, e)
    b_a = jnp.transpose(chip_part(4, (n_heads, hj)), (1, 0, 2)).reshape(n_heads, 1, 4 * hj)
    b_x = jnp.transpose(chip_part(5, (n_heads, hj)), (1, 0, 2)).reshape(n_heads, 1, 4 * hj)
    lam = chip_part(6, (es,)).reshape(1, e)

    mod_nb = _ada_fwd(c_all, ada_w, "ada_fwd")
    mods = _allgather8([mod_nb.reshape(16, f)], "ag_mod")[0].reshape(8, 2, 8, f)[0::2]
    mine = lax.dynamic_index_in_dim(mods, batch, axis=2, keepdims=False)
    mod = jnp.transpose(mine, (1, 0, 2)).reshape(2, 4 * f) + ada_b
    shift = [row(mod[l, :d]) for l in range(2)]
    scale = [row(mod[l, d:2 * d]) for l in range(2)]
    gate = [row(mod[l, 2 * d:]) for l in range(2)]
    ng = [row(norm_g[l]) for l in range(2)]

    shards = [sc_w_in[0], sc_w_out[0], lru_w_in[0], lru_w_a[0].reshape(n_heads * hj, HEAD_DIM),
              lru_w_x[0].reshape(n_heads * hj, HEAD_DIM), lru_w_out[0]]
    names = ["sc_w_in", "sc_w_out", "lru_w_in", "lru_w_a", "lru_w_x", "lru_w_out"]
    slots = [_cast_into_slot(chip_op, w, "cast_" + nm) for w, nm in zip(shards, names)]
    send_a, recv_a, buf, started = _first_start(slots[0], mod, "ag_first_start")
    h0 = _norm_mod_fwd(x2d, ng[0], scale[0], shift[0], "norm0", after=started)
    planes = [jnp.reshape(2 * px + py, (1,)).astype(jnp.int32) for px, py in _other_chips(xi, yi)]
    proj0 = _mm_plane(h0, buf.reshape(4, d, e), chip_op, "sc_in_own")
    relays = []
    for j in range(2):
        send_b, recv_b, send_c, recv_c, buf, relayed = _first_relay(send_a, recv_a, buf, j, proj0,
                                                                    "ag_first_relay_%d" % j)
        relays.append((send_c, recv_c))
        if j == 1:
            rest_flight, relayed = _gather_start([[slots[1]], slots[2:5], [slots[5]]], relayed, "ag_start")
        buf = _first_relay_done(send_b, recv_b, buf, j, relayed, "ag_first_relay_done_%d" % j)
        proj0 = _mm_plane(h0, buf.reshape(4, d, e), planes[j], "sc_in_%d" % j, planes_so_far=proj0)
    send_b, recv_b, buf, relayed = _first_diagonal(relays, buf, proj0, "ag_first_diagonal")
    buf = _first_diagonal_done(send_b, recv_b, relays, buf, relayed, "ag_first_diagonal_done")
    proj0 = _mm_plane(h0, buf.reshape(4, d, e), planes[2], "sc_in_2", planes_so_far=proj0)
    w_in0 = buf.reshape(4, d, e)
    in_flight = [None] + rest_flight

    def arrived(g, after, tag):
        send1, recv1, bufs = in_flight[g]
        send2, recv2, bufs, passed = _gather_forward(send1, recv1, bufs, after, "ag_forward_" + tag)
        return _gather_finish(send2, recv2, bufs, passed, "ag_finish_" + tag)

    w_out0 = arrived(1, proj0, "sc_w_out")[0].reshape(1, e, d)
    y0 = _sc_fwd(proj0, conv_w0, "sc_mix")
    o0 = _mm(y0[None], w_out0, "nn", F32, "sc_out")[0]
    x1, h1 = _norm_mod_fwd(x2d, ng[1], scale[1], shift[1], "norm1", o=o0, gate=gate[0])
    lru_ws = arrived(2, h1, "lru_w_in")
    w_in1 = lru_ws[0].reshape(4, d, e // 2)
    w_a = lru_ws[1].reshape(4, n_heads, hj, HEAD_DIM)
    w_x = lru_ws[2].reshape(4, n_heads, hj, HEAD_DIM)
    proj1 = _mm(h1[None], w_in1, "nn", BF16, "lru_in")
    pairs1 = proj1.reshape(2, 2, s, e // 2)
    y1, *lru_saved = _lru_fwd(pairs1, conv_w1, conv_b1, w_a, b_a, w_x, b_x, lam, "lru_mix")
    w_out1 = arrived(3, y1, "lru_w_out")[0].reshape(1, e, d)
    o1 = _mm(y1[None], w_out1, "nn", F32, "lru_out")[0]
    dx2, do1, acc_f = _final_loss(x1, o1, gate[1], row(final_g), tgt, "final_loss")

    def reduce_stage1(tag, grads):
        lands = [lax.empty((4, g.shape[1] // 2, g.shape[2]), F32) for g in grads]
        return _exchange_start("rs_sibling_start_" + tag, grads, lands, len(grads),
                               _plan_other_half_to_sibling, None)

    def reduce_stage2(tag, stage1, nms, after):
        send, recv, grads, lands, _ = stage1
        grads, lands = _exchange_wait("rs_sibling_wait_" + tag, send, recv, grads, lands,
                                      _plan_other_half_to_sibling, after)
        parts = [_sum_own_and_sibling(core_op, g, r1, "rs_sum1_" + nm) for g, r1, nm in zip(grads, lands, nms)]
        lands = [lax.empty((3,) + p.shape[1:], BF16) for p in parts]
        return _exchange_start("rs_chips_start_" + tag, parts, lands, 3 * len(parts),
                               _plan_partials_to_chips, None)

    def reduce_stage3(tag, stage2, nms, after):
        send, recv, parts, lands, _ = stage2
        parts, lands = _exchange_wait("rs_chips_wait_" + tag, send, recv, parts, lands,
                                      _plan_partials_to_chips, after)
        halves = [_sum_chips(where_op, p, r2, "rs_sum2_" + nm) for p, r2, nm in zip(parts, lands, nms)]
        return _exchange_start("rs_share_start_" + tag, [], halves, len(halves), _plan_share_half, None)

    def reduce_done(tag, stage3, after):
        send, recv, _, fulls, _ = stage3
        return _exchange_wait("rs_share_wait_" + tag, send, recv, [], fulls, _plan_share_half, after)[1]

    def head_major_to_chip_major(t):
        return jnp.transpose(t.reshape(n_heads, 4, hj, HEAD_DIM), (1, 0, 2, 3)).reshape(4, n_heads * hj, HEAD_DIM)

    big_state = dict(zip(names, zip(shards, [m_sc_w_in, m_sc_w_out, m_lru_w_in, m_lru_w_a, m_lru_w_x, m_lru_w_out],
                                    [v_sc_w_in, v_sc_w_out, v_lru_w_in, v_lru_w_a, v_lru_w_x, v_lru_w_out],
                                    [sc_w_in, sc_w_out, lru_w_in, lru_w_a, lru_w_x, lru_w_out])))
    big = {}

    def update(nms, fulls, after):
        for nm, g2 in zip(nms, fulls):
            w2, m4, v4, w4 = big_state[nm]
            outs = _adamw(w2, g2, m4.reshape(w2.shape), v4.reshape(w2.shape), "adamw_" + nm, after=after)
            big[nm] = tuple(t.reshape(w4.shape) for t in outs)
            after = outs[1]
        return after

    g_w_out1 = _mm(y1[None], do1[None], "tn", F32, "lru_out_dw", bm=512, bn=2048)
    dy1 = _mm(do1[None], w_out1, "nt", BF16, "lru_out_dx")[0]
    dpairs1, g_wa, g_wx, g_ba, g_bx, g_lam, g_cw1, g_cb1 = _lru_bwd(
        pairs1, dy1, lru_saved, conv_w1, conv_b1, w_a, b_a, w_x, b_x, lam, "lru_mix_bwd")
    dproj1 = dpairs1.reshape(4, s, e // 2)
    g_w_in1 = _mm(h1[None], dproj1, "tn", F32, "lru_in_dw", bm=512, bn=2048)
    lru_names = ["lru_w_out", "lru_w_a", "lru_w_x", "lru_w_in"]
    lru_rs = reduce_stage1("lru", [g_w_out1.reshape(4, es, d), head_major_to_chip_major(g_wa),
                                   head_major_to_chip_major(g_wx), g_w_in1])
    dh1 = _mm(dproj1, w_in1, "nt", F32, "lru_in_dx", after=lru_rs[4])[0]
    lru_rs = reduce_stage2("lru", lru_rs, lru_names, dh1)
    dx1, do0, acc1 = _norm_mod_bwd(dh1, x1, ng[1], scale[1], dx2, "norm1_bwd", o_prev=o0, gate_prev=gate[0],
                                   after=lru_rs[4])

    g_w_out0 = _mm(y0[None], do0[None], "tn", F32, "sc_out_dw", bm=512, bn=2048)
    out_rs = reduce_stage1("sc_out", [g_w_out0.reshape(4, es, d)])
    dy0 = _mm(do0[None], w_out0, "nt", BF16, "sc_out_dx", after=out_rs[4])[0]
    out_rs = reduce_stage2("sc_out", out_rs, ["sc_w_out"], dy0)
    dproj0, g_cw0 = _sc_bwd(proj0, dy0, conv_w0, "sc_mix_bwd", after=out_rs[4])
    stage1, stage2, after = [], [], None
    for k in range(IN_PIECES):
        g_piece = _mm(h0[None], dproj0, "tn", F32, "sc_in_dw_%d" % k, bm=512, bn=e // IN_PIECES,
                      col_blocks=(k, 1), after=after)
        stage1.append(reduce_stage1("sc_in_%d" % k, [g_piece]))
        after = stage1[k][4]
        if k:
            stage2.append(reduce_stage2("sc_in_%d" % (k - 1), stage1[k - 1], ["sc_w_in_%d" % (k - 1)], after))
            after = stage2[k - 1][4]
    lru_rs = reduce_stage3("lru", lru_rs, lru_names, after)
    k = IN_PIECES - 1
    stage2.append(reduce_stage2("sc_in_%d" % k, stage1[k], ["sc_w_in_%d" % k], lru_rs[4]))
    dh0 = _mm(dproj0, w_in0, "nt", F32, "sc_in_dx", after=stage2[k][4])[0]
    grad_x, acc0 = _norm_mod_bwd(dh0, x2d, ng[0], scale[0], dx1, "norm0_bwd")
    out_rs = reduce_stage3("sc_out", out_rs, ["sc_w_out"], acc0)
    last = update(lru_names, reduce_done("lru", lru_rs, out_rs[4]), None)
    last = update(["sc_w_out"], reduce_done("sc_out", out_rs, last), None)
    g_half = None
    for k, (send, recv, parts, lands, _) in enumerate(stage2):
        parts, lands = _exchange_wait("rs_chips_wait_sc_in_%d" % k, send, recv, parts, lands,
                                      _plan_partials_to_chips, last)
        g_half = _sum_chips(where_op, parts[0], lands[0], "rs_sum2_sc_w_in_%d" % k, piece=(k, IN_PIECES),
                            so_far=g_half)
    in_rs = _exchange_start("rs_share_start_sc_in", [], [g_half], 1, _plan_share_half, None)

    dmod = jnp.stack([jnp.concatenate([acc0[1], acc0[0], acc1[3]]),
                      jnp.concatenate([acc1[1], acc1[0], acc_f[1]])])
    part_list = [jnp.stack([acc0[2], acc1[2]]), acc_f[0], acc_f[2, :1], g_cw0, g_cw1, g_cb1, g_ba, g_bx,
                 g_lam, dmod]
    partials, poffs, n_part = _pack(part_list)
    every = _allgather8([partials.reshape(8, n_part // 8)], "ag_partials", after=in_rs[4])[0].reshape(8, n_part)
    total = _sum_rows8(every, "sum_partials")[0]

    def tot(k, shape):
        size = 1
        for dim in shape:
            size *= dim
        return total[poffs[k]:poffs[k] + size].reshape(shape)

    def my_cols(t, width):
        return lax.dynamic_slice_in_dim(t, chip * width, width, axis=t.ndim - 1)

    loss = tot(2, (1,))[0]
    g_norm_g, g_final_g, g_ada_b = tot(0, (2, d)), tot(1, (d,)), tot(9, (2, 3 * d))
    g_sc_conv_w = my_cols(tot(3, (3, e)), es)[None]
    g_lru_conv_w = my_cols(tot(4, (4, e)), es)[None]
    g_lru_conv_b = my_cols(tot(5, (1, e)), es)
    g_lru_b_a = my_cols(tot(6, (n_heads, 4 * hj)), hj)[None]
    g_lru_b_x = my_cols(tot(7, (n_heads, 4 * hj)), hj)[None]
    g_lru_lambda = my_cols(tot(8, (1, e)), es)

    dmod_all = every[:, poffs[9]:poffs[9] + 6 * d].reshape(8, 2, 3 * d)
    dmod_mine = jnp.transpose(my_cols(dmod_all, f), (1, 0, 2))
    ada = _ada_bwd_adamw(jnp.transpose(c_all), dmod_mine, ada_w, m_ada_w, v_ada_w, "ada_bwd_adamw", after=total)

    small_names = ["norm_g", "ada_b", "final_g", "sc_conv_w", "lru_conv_w", "lru_conv_b", "lru_b_a",
                   "lru_b_x", "lru_lambda"]
    small_w = [norm_g, ada_b, final_g, sc_conv_w, lru_conv_w, lru_conv_b, lru_b_a, lru_b_x, lru_lambda]
    small_g = [g_norm_g, g_ada_b, g_final_g, g_sc_conv_w, g_lru_conv_w, g_lru_conv_b, g_lru_b_a,
               g_lru_b_x, g_lru_lambda]
    small_m = [m_norm_g, m_ada_b, m_final_g, m_sc_conv_w, m_lru_conv_w, m_lru_conv_b, m_lru_b_a,
               m_lru_b_x, m_lru_lambda]
    small_v = [v_norm_g, v_ada_b, v_final_g, v_sc_conv_w, v_lru_conv_w, v_lru_conv_b, v_lru_b_a,
               v_lru_b_x, v_lru_lambda]
    pw, soffs, n_s = _pack(small_w)
    pg, pm, pv = _pack(small_g)[0], _pack(small_m)[0], _pack(small_v)[0]
    shape2 = (n_s // PACK, PACK)
    _, pd, pm2, pv2 = _adamw(pw.reshape(shape2), pg.reshape(shape2), pm.reshape(shape2), pv.reshape(shape2),
                             "adamw_small", after=ada[0])
    update(["sc_w_in"], reduce_done("sc_in", in_rs, pd), None)
    small = {}
    for k, (nm, w_) in enumerate(zip(small_names, small_w)):
        take = lambda t: t.reshape(-1)[soffs[k]:soffs[k] + w_.size].reshape(w_.shape)
        small[nm] = (small_g[k].reshape(w_.shape), take(pd), take(pm2), take(pv2))

    results = dict(small)
    results.update(big)
    results["ada_w"] = tuple(ada)
    order = ["norm_g", "ada_w", "ada_b", "sc_w_in", "sc_conv_w", "sc_w_out", "lru_w_in", "lru_conv_w",
             "lru_conv_b", "lru_w_a", "lru_b_a", "lru_w_x", "lru_b_x", "lru_lambda", "lru_w_out", "final_g"]
    out = [loss, grad_x[None]]
    for kind in range(4):
        out += [results[nm][kind] for nm in order]
    return tuple(out)
```

```python
import functools

import jax
import jax.numpy as jnp
from jax import lax
from jax.experimental import pallas as pl
from jax.experimental.pallas import tpu as pltpu

F32 = jnp.float32
BF16 = jnp.bfloat16
MESH = pl.DeviceIdType.MESH
ANY = pl.BlockSpec(memory_space=pl.ANY)

RMS_EPS = 1e-6
RGLRU_C = 8.0
HEAD_DIM = 256
ADAM_LR = 0.001
ADAM_B1 = 0.9
ADAM_B2 = 0.999
ADAM_EPS = 1e-08
ADAM_WD = 0.01
ADAM_STEP = 10
V7X_VMEM_LIMIT = 56 * 1024 * 1024
IN_PIECES = 2
LANES = 128
SUBLANES = 8
PACK = SUBLANES * LANES


def _blk(dim, pref, unit=LANES):
    if dim <= pref:
        return dim
    b = (pref // unit) * unit
    while b > unit and dim % b:
        b -= unit
    assert dim % b == 0, (dim, pref, unit)
    return b


def _params(sem=None):
    return pltpu.CompilerParams(dimension_semantics=sem, vmem_limit_bytes=V7X_VMEM_LIMIT)


def _ordered(body, n_in, after):
    if after is None:
        return body, [], []

    def ordered_body(*refs):
        return body(*refs[:n_in], *refs[n_in + 1:])

    return ordered_body, [ANY], [after]


def _pos():
    return lax.axis_index("x"), lax.axis_index("y"), lax.axis_index("c")


def _other_chips(x, y):
    return [(1 - x, y), (x, 1 - y), (1 - x, 1 - y)]


def _allgather8(arrs, name, after=None):
    n_t = len(arrs)
    ms = [a.shape[0] for a in arrs]

    def gather(*refs):
        ins, outs = refs[:n_t], refs[n_t:2 * n_t]
        send_sems, recv_sems, local_sems = refs[2 * n_t:]
        x, y, c = _pos()
        me, sibling = (x, y, c), (x, y, 1 - c)
        chips = _other_chips(x, y)

        def rows(t, px, py, pc):
            return outs[t].at[pl.ds((4 * px + 2 * py + pc) * ms[t], ms[t])]

        def copy(t, k, block, to, src=None):
            return pltpu.make_async_remote_copy(
                src_ref=rows(t, *block) if src is None else src, dst_ref=rows(t, *block),
                send_sem=send_sems.at[7 * t + k], recv_sem=recv_sems.at[7 * t + k],
                device_id=to, device_id_type=MESH)

        mine, first, passed = [], [], []
        for t in range(n_t):
            src = ins[t]
            cp = pltpu.make_async_copy(src, rows(t, *me), local_sems.at[t])
            cp.start()
            mine.append(cp)
            sends = [copy(t, 0, me, sibling, src=src)]
            sends += [copy(t, 1 + j, me, (*chip, c), src=src) for j, chip in enumerate(chips)]
            for cp in sends:
                cp.start()
            first += sends
        for t in range(n_t):
            for j, chip in enumerate(chips):
                copy(t, 1 + j, (*chip, c), me).wait_recv()
                cp = copy(t, 4 + j, (*chip, c), sibling)
                cp.start()
                passed.append(cp)
        for t in range(n_t):
            copy(t, 0, sibling, me).wait_recv()
            for j, chip in enumerate(chips):
                copy(t, 4 + j, (*chip, 1 - c), me).wait_recv()
        for cp in first + passed:
            cp.wait_send()
        for cp in mine:
            cp.wait()

    body, extra_specs, extra = _ordered(gather, n_t, after)
    return pl.pallas_call(
        body, name=name,
        out_shape=[jax.ShapeDtypeStruct((8 * m, a.shape[1]), a.dtype) for m, a in zip(ms, arrs)],
        in_specs=[ANY] * n_t + extra_specs, out_specs=[ANY] * n_t,
        scratch_shapes=[pltpu.SemaphoreType.DMA((7 * n_t,)), pltpu.SemaphoreType.DMA((7 * n_t,)),
                        pltpu.SemaphoreType.DMA((n_t,))],
    )(*arrs, *extra)


HBM = pl.BlockSpec(memory_space=pltpu.HBM)
SEM = pl.BlockSpec(memory_space=pltpu.SEMAPHORE)
TOKEN = pl.BlockSpec(memory_space=pltpu.VMEM)
IN_FLIGHT = pltpu.CompilerParams(has_side_effects=pltpu.SideEffectType.DATAFLOW_SIDE_EFFECTING)


def _in_hbm(arrs):
    return [pltpu.with_memory_space_constraint(a, pltpu.HBM) for a in arrs]


def _shard_rows(ref, h, px, py, pc):
    return ref.at[pl.ds((4 * px + 2 * py + pc) * h, h)]


def _gather_start(groups, after, name):
    bufs = [b for grp in groups for b in grp]
    n_t, n_g = len(bufs), len(groups)

    def body(*refs):
        sems, thru = refs[n_t + 1:n_t + 1 + 2 * n_g], refs[n_t + 1 + 2 * n_g:2 * n_t + 1 + 2 * n_g]
        token = refs[-1]
        x, y, c = _pos()
        t = 0
        for g, grp in enumerate(groups):
            for i in range(len(grp)):
                h = bufs[t].shape[0] // 8
                rows = _shard_rows(thru[t], h, x, y, c)
                for j, chip in enumerate(_other_chips(x, y)):
                    pltpu.make_async_remote_copy(
                        src_ref=rows, dst_ref=rows, send_sem=sems[2 * g].at[3 * i + j],
                        recv_sem=sems[2 * g + 1].at[3 * i + j], device_id=(*chip, c),
                        device_id_type=MESH).start()
                t += 1
        token[...] = jnp.zeros_like(token)

    sem_shapes = []
    for grp in groups:
        sem_shapes += [pltpu.SemaphoreType.DMA((3 * len(grp),))] * 2
    out = pl.pallas_call(
        body, name=name,
        out_shape=sem_shapes + [pltpu.HBM(b.shape, b.dtype) for b in bufs] + [jax.ShapeDtypeStruct((8, LANES), F32)],
        in_specs=[HBM] * n_t + [ANY], out_specs=[SEM] * (2 * n_g) + [HBM] * n_t + [TOKEN],
        input_output_aliases={t: 2 * n_g + t for t in range(n_t)},
        compiler_params=IN_FLIGHT,
    )(*_in_hbm(bufs), after)
    sems, thru, token = out[:2 * n_g], out[2 * n_g:2 * n_g + n_t], out[-1]
    per_group, t = [], 0
    for g, grp in enumerate(groups):
        per_group.append((sems[2 * g], sems[2 * g + 1], thru[t:t + len(grp)]))
        t += len(grp)
    return per_group, token


def _gather_forward(send_sems, recv_sems, bufs, after, name, sources=(0, 1, 2)):
    n_t = len(bufs)

    def body(*refs):
        ins = refs[:n_t]
        send1, recv1 = refs[n_t], refs[n_t + 1]
        send2, recv2 = refs[n_t + 3], refs[n_t + 4]
        token = refs[-1]
        x, y, c = _pos()
        chips = _other_chips(x, y)
        for t in range(n_t):
            h = bufs[t].shape[0] // 8
            mine = _shard_rows(ins[t], h, x, y, c)
            for j in sources:
                chip = chips[j]
                landed = _shard_rows(ins[t], h, *chip, c)
                pltpu.make_async_remote_copy(
                    src_ref=mine, dst_ref=landed, send_sem=send1.at[3 * t + j], recv_sem=recv1.at[3 * t + j],
                    device_id=(*chip, c), device_id_type=MESH).wait_recv()
                pltpu.make_async_remote_copy(
                    src_ref=landed, dst_ref=landed, send_sem=send2.at[3 * t + j], recv_sem=recv2.at[3 * t + j],
                    device_id=(x, y, 1 - c), device_id_type=MESH).start()
        for t in range(n_t):
            h = bufs[t].shape[0] // 8
            mine = _shard_rows(ins[t], h, x, y, c)
            for j in sources:
                pltpu.make_async_remote_copy(
                    src_ref=mine, dst_ref=mine, send_sem=send1.at[3 * t + j], recv_sem=recv1.at[3 * t + j],
                    device_id=(*chips[j], c), device_id_type=MESH).wait_send()
        token[...] = jnp.zeros_like(token)

    out = pl.pallas_call(
        body, name=name,
        out_shape=[pltpu.SemaphoreType.DMA((3 * n_t,))] * 2 + [pltpu.HBM(b.shape, b.dtype) for b in bufs]
        + [jax.ShapeDtypeStruct((8, LANES), F32)],
        in_specs=[HBM] * n_t + [SEM, SEM, ANY], out_specs=[SEM, SEM] + [HBM] * n_t + [TOKEN],
        input_output_aliases={t: 2 + t for t in range(n_t)},
        compiler_params=IN_FLIGHT,
    )(*bufs, send_sems, recv_sems, after)
    return out[0], out[1], out[2:2 + n_t], out[-1]


def _gather_finish(send_sems, recv_sems, bufs, after, name, sources=(0, 1, 2)):
    n_t = len(bufs)

    def body(*refs):
        ins = refs[:n_t]
        send2, recv2 = refs[n_t], refs[n_t + 1]
        x, y, c = _pos()
        chips = _other_chips(x, y)
        for t in range(n_t):
            h = bufs[t].shape[0] // 8
            for j in sources:
                chip = chips[j]
                sent = _shard_rows(ins[t], h, *chip, c)
                got = _shard_rows(ins[t], h, *chip, 1 - c)
                cp = pltpu.make_async_remote_copy(
                    src_ref=sent, dst_ref=got, send_sem=send2.at[3 * t + j], recv_sem=recv2.at[3 * t + j],
                    device_id=(x, y, 1 - c), device_id_type=MESH)
                cp.wait_send()
                cp.wait_recv()

    return pl.pallas_call(
        body, name=name, out_shape=[pltpu.HBM(b.shape, b.dtype) for b in bufs],
        in_specs=[HBM] * n_t + [SEM, SEM, ANY], out_specs=[HBM] * n_t,
        input_output_aliases={t: t for t in range(n_t)},
        compiler_params=IN_FLIGHT,
    )(*bufs, send_sems, recv_sems, after)


def _part_rows(ref, h, px, py, pc, which):
    return ref.at[pl.ds((4 * px + 2 * py + pc) * h + which * (h // 2), h // 2)]


def _remote(src, dst, send_sem, recv_sem, device):
    return pltpu.make_async_remote_copy(src_ref=src, dst_ref=dst, send_sem=send_sem, recv_sem=recv_sem,
                                        device_id=device, device_id_type=MESH)


def _first_start(buf, after, name):
    h = buf.shape[0] // 8

    def body(buf_in, after_ref, send, recv, thru, token):
        x, y, c = _pos()
        rows = _shard_rows(thru, h, x, y, c)
        for j, chip in enumerate(_other_chips(x, y)[:2]):
            _remote(rows, rows, send.at[j], recv.at[j], (*chip, c)).start()
        token[...] = jnp.zeros_like(token)

    return pl.pallas_call(
        body, name=name,
        out_shape=[pltpu.SemaphoreType.DMA((2,))] * 2 + [pltpu.HBM(buf.shape, buf.dtype),
                                                         jax.ShapeDtypeStruct((8, LANES), F32)],
        in_specs=[HBM, ANY], out_specs=[SEM, SEM, HBM, TOKEN], input_output_aliases={0: 2},
        compiler_params=IN_FLIGHT,
    )(*_in_hbm([buf]), after)


def _first_relay(send_a, recv_a, buf, j, after, name):
    h = buf.shape[0] // 8

    def body(buf_in, send_a, recv_a, *rest):
        send_b, recv_b, send_c, recv_c, _, token = rest[len(after):]
        x, y, c = _pos()
        chips = _other_chips(x, y)
        nbr, other = chips[j], chips[1 - j]
        mine, landed = _shard_rows(buf_in, h, x, y, c), _shard_rows(buf_in, h, *nbr, c)
        _remote(mine, landed, send_a.at[j], recv_a.at[j], (*nbr, c)).wait_recv()
        _remote(landed, landed, send_b.at[0], recv_b.at[0], (x, y, 1 - c)).start()
        part = _part_rows(buf_in, h, *nbr, c, j)
        _remote(part, part, send_c.at[0], recv_c.at[0], (*other, c)).start()
        _remote(mine, mine, send_a.at[j], recv_a.at[j], (*nbr, c)).wait_send()
        token[...] = jnp.zeros_like(token)

    out = pl.pallas_call(
        body, name=name,
        out_shape=[pltpu.SemaphoreType.DMA((1,))] * 4 + [pltpu.HBM(buf.shape, buf.dtype),
                                                         jax.ShapeDtypeStruct((8, LANES), F32)],
        in_specs=[HBM, SEM, SEM] + [ANY] * len(after), out_specs=[SEM] * 4 + [HBM, TOKEN],
        input_output_aliases={0: 4}, compiler_params=IN_FLIGHT,
    )(buf, send_a, recv_a, *after)
    return tuple(out)


def _first_relay_done(send_b, recv_b, buf, j, after, name):
    h = buf.shape[0] // 8

    def body(buf_in, send_b, recv_b, after_ref, thru):
        x, y, c = _pos()
        nbr = _other_chips(x, y)[j]
        cp = _remote(_shard_rows(buf_in, h, *nbr, c), _shard_rows(buf_in, h, *nbr, 1 - c),
                     send_b.at[0], recv_b.at[0], (x, y, 1 - c))
        cp.wait_send()
        cp.wait_recv()

    return pl.pallas_call(
        body, name=name, out_shape=pltpu.HBM(buf.shape, buf.dtype),
        in_specs=[HBM, SEM, SEM, ANY], out_specs=HBM, input_output_aliases={0: 0},
        compiler_params=IN_FLIGHT,
    )(buf, send_b, recv_b, after)


def _first_diagonal(relays, buf, after, name):
    h = buf.shape[0] // 8

    def body(buf_in, send_c0, recv_c0, send_c1, recv_c1, after_ref, send_b, recv_b, thru, token):
        x, y, c = _pos()
        chips = _other_chips(x, y)
        diag = chips[2]
        for j, (send_c, recv_c) in enumerate(((send_c0, recv_c0), (send_c1, recv_c1))):
            part = _part_rows(buf_in, h, *diag, c, j)
            _remote(part, part, send_c.at[0], recv_c.at[0], (*chips[1 - j], c)).wait_recv()
        whole = _shard_rows(buf_in, h, *diag, c)
        _remote(whole, whole, send_b.at[0], recv_b.at[0], (x, y, 1 - c)).start()
        token[...] = jnp.zeros_like(token)

    out = pl.pallas_call(
        body, name=name,
        out_shape=[pltpu.SemaphoreType.DMA((1,))] * 2 + [pltpu.HBM(buf.shape, buf.dtype),
                                                         jax.ShapeDtypeStruct((8, LANES), F32)],
        in_specs=[HBM] + [SEM] * 4 + [ANY], out_specs=[SEM, SEM, HBM, TOKEN], input_output_aliases={0: 2},
        compiler_params=IN_FLIGHT,
    )(buf, relays[0][0], relays[0][1], relays[1][0], relays[1][1], after)
    return tuple(out)


def _first_diagonal_done(send_b, recv_b, relays, buf, after, name):
    h = buf.shape[0] // 8

    def body(buf_in, send_b, recv_b, send_c0, recv_c0, send_c1, recv_c1, after_ref, thru):
        x, y, c = _pos()
        chips = _other_chips(x, y)
        diag = chips[2]
        cp = _remote(_shard_rows(buf_in, h, *diag, c), _shard_rows(buf_in, h, *diag, 1 - c),
                     send_b.at[0], recv_b.at[0], (x, y, 1 - c))
        cp.wait_send()
        cp.wait_recv()
        for j, (send_c, recv_c) in enumerate(((send_c0, recv_c0), (send_c1, recv_c1))):
            part = _part_rows(buf_in, h, *chips[j], c, j)
            _remote(part, part, send_c.at[0], recv_c.at[0], (*chips[1 - j], c)).wait_send()

    return pl.pallas_call(
        body, name=name, out_shape=pltpu.HBM(buf.shape, buf.dtype),
        in_specs=[HBM] + [SEM] * 6 + [ANY], out_specs=HBM, input_output_aliases={0: 0},
        compiler_params=IN_FLIGHT,
    )(buf, send_b, recv_b, relays[0][0], relays[0][1], relays[1][0], relays[1][1], after)


def _exchange_start(name, srcs, lands, n_copies, plan, after):
    ns, nl = len(srcs), len(lands)
    extra = [] if after is None else [after]

    def body(*refs):
        base = ns + nl + len(extra)
        send_sems, recv_sems = refs[base], refs[base + 1]
        src_refs, land_refs = refs[base + 2:base + 2 + ns], refs[base + 2 + ns:base + 2 + ns + nl]
        token = refs[-1]
        x, y, c = _pos()
        copies = plan(src_refs, land_refs, x, y, c)
        assert len(copies) == n_copies
        for k, (src, dst, dev) in enumerate(copies):
            pltpu.make_async_remote_copy(
                src_ref=src, dst_ref=dst, send_sem=send_sems.at[k], recv_sem=recv_sems.at[k],
                device_id=dev, device_id_type=MESH).start()
        token[...] = jnp.zeros_like(token)

    out = pl.pallas_call(
        body, name=name,
        out_shape=[pltpu.SemaphoreType.DMA((n_copies,))] * 2
        + [pltpu.HBM(a.shape, a.dtype) for a in list(srcs) + list(lands)] + [jax.ShapeDtypeStruct((8, LANES), F32)],
        in_specs=[HBM] * (ns + nl) + [ANY] * len(extra), out_specs=[SEM, SEM] + [HBM] * (ns + nl) + [TOKEN],
        input_output_aliases={i: 2 + i for i in range(ns + nl)},
        compiler_params=IN_FLIGHT,
    )(*_in_hbm(list(srcs) + list(lands)), *extra)
    return out[0], out[1], out[2:2 + ns], out[2 + ns:2 + ns + nl], out[-1]


def _exchange_wait(name, send_sems, recv_sems, srcs, lands, plan, after):
    ns, nl = len(srcs), len(lands)

    def body(*refs):
        src_refs, land_refs = refs[:ns], refs[ns:ns + nl]
        send, recv = refs[ns + nl], refs[ns + nl + 1]
        x, y, c = _pos()
        for k, (src, dst, dev) in enumerate(plan(src_refs, land_refs, x, y, c)):
            cp = pltpu.make_async_remote_copy(
                src_ref=src, dst_ref=dst, send_sem=send.at[k], recv_sem=recv.at[k],
                device_id=dev, device_id_type=MESH)
            cp.wait_send()
            cp.wait_recv()

    out = pl.pallas_call(
        body, name=name, out_shape=[pltpu.HBM(a.shape, a.dtype) for a in list(srcs) + list(lands)],
        in_specs=[HBM] * (ns + nl) + [SEM, SEM, ANY], out_specs=[HBM] * (ns + nl),
        input_output_aliases={i: i for i in range(ns + nl)},
        compiler_params=IN_FLIGHT,
    )(*srcs, *lands, send_sems, recv_sems, after)
    return out[:ns], out[ns:]


def _plan_other_half_to_sibling(src_refs, land_refs, x, y, c):
    out = []
    for g_ref, r_ref in zip(src_refs, land_refs):
        h = g_ref.shape[1] // 2
        out.append((g_ref.at[:, pl.ds((1 - c) * h, h), :], r_ref, (x, y, 1 - c)))
    return out


def _plan_partials_to_chips(src_refs, land_refs, x, y, c):
    out = []
    for p_ref, r_ref in zip(src_refs, land_refs):
        for j, (px, py) in enumerate(_other_chips(x, y)):
            out.append((p_ref.at[2 * px + py], r_ref.at[j], (px, py, c)))
    return out


def _plan_share_half(src_refs, land_refs, x, y, c):
    out = []
    for f_ref in land_refs:
        h = f_ref.shape[0] // 2
        rows = f_ref.at[pl.ds(c * h, h)]
        out.append((rows, rows, (x, y, 1 - c)))
    return out


def _cast_into_slot(chip, w, name):
    r, c = w.shape
    br, bc = _blk(r, 512, 8), _blk(c, 2048)
    nb = r // br

    def body(chip_ref, w_ref, o_ref):
        o_ref[...] = w_ref[...].astype(BF16)

    grid_spec = pltpu.PrefetchScalarGridSpec(
        num_scalar_prefetch=1, grid=(nb, c // bc),
        in_specs=[pl.BlockSpec((br, bc), lambda i, j, chip_ref: (i, j))],
        out_specs=pl.BlockSpec((br, bc), lambda i, j, chip_ref: (chip_ref[0] * nb + i, j)))
    return pl.pallas_call(
        body, name=name, grid_spec=grid_spec,
        out_shape=jax.ShapeDtypeStruct((4 * r, c), BF16),
        compiler_params=_params(("parallel", "parallel")),
    )(chip, w)


def _sum_own_and_sibling(core, g, r1, name):
    _, r, c = g.shape
    h = r // 2
    br, bc = _blk(h, 256, 8), _blk(c, 2048)
    nb = h // br

    def body(core_ref, g_ref, r_ref, o_ref):
        o_ref[...] = (g_ref[...] + r_ref[...]).astype(BF16)

    grid_spec = pltpu.PrefetchScalarGridSpec(
        num_scalar_prefetch=1, grid=(4, nb, c // bc),
        in_specs=[pl.BlockSpec((None, br, bc), lambda k, i, j, core_ref: (k, core_ref[0] * nb + i, j)),
                  pl.BlockSpec((None, br, bc), lambda k, i, j, core_ref: (k, i, j))],
        out_specs=pl.BlockSpec((None, br, bc), lambda k, i, j, core_ref: (k, i, j)))
    return pl.pallas_call(
        body, name=name, grid_spec=grid_spec,
        out_shape=jax.ShapeDtypeStruct((4, h, c), BF16),
        compiler_params=_params(("parallel", "parallel", "parallel")),
    )(core, g, r1)


def _sum_chips(where, p, r2, name, piece=(0, 1), so_far=None):
    _, h, c = p.shape
    k, n = piece
    br, bc = _blk(h, 256, 8), _blk(c, 2048)
    nb, ncb = h // br, c // bc

    def body(where_ref, p_ref, r_ref, *rest):
        acc = p_ref[...].astype(F32)
        for j in range(3):
            acc = acc + r_ref[j].astype(F32)
        rest[-1][...] = acc

    extra_specs, extra, aliases = ([], [], {}) if so_far is None else ([ANY], [so_far], {3: 0})
    grid_spec = pltpu.PrefetchScalarGridSpec(
        num_scalar_prefetch=1, grid=(nb, ncb),
        in_specs=[pl.BlockSpec((None, br, bc), lambda i, j, where_ref: (where_ref[0], i, j)),
                  pl.BlockSpec((3, br, bc), lambda i, j, where_ref: (0, i, j))] + extra_specs,
        out_specs=pl.BlockSpec((br, bc), lambda i, j, where_ref: (where_ref[1] * nb + i, k * ncb + j)))
    return pl.pallas_call(
        body, name=name, grid_spec=grid_spec,
        out_shape=jax.ShapeDtypeStruct((2 * h, n * c), F32), input_output_aliases=aliases,
        compiler_params=_params(("parallel", "parallel")),
    )(where, p, r2, *extra)


def _adamw_math(w, g, m, v):
    m2 = ADAM_B1 * m + (1.0 - ADAM_B1) * g
    v2 = ADAM_B2 * v + (1.0 - ADAM_B2) * (g * g)
    m_hat = m2 / (1.0 - ADAM_B1 ** ADAM_STEP)
    v_hat = v2 / (1.0 - ADAM_B2 ** ADAM_STEP)
    delta = -ADAM_LR * (m_hat / (jnp.sqrt(v_hat) + ADAM_EPS) + ADAM_WD * w)
    return delta, m2, v2


def _adamw(w, g, m, v, name, after=None):
    r, c = w.shape
    br, bc = _blk(r, 128, 8), _blk(c, 2048)

    def body(w_ref, g_ref, m_ref, v_ref, go_ref, d_ref, m2_ref, v2_ref):
        gv = g_ref[...]
        d, m2, v2 = _adamw_math(w_ref[...], gv, m_ref[...], v_ref[...])
        go_ref[...] = gv
        d_ref[...] = d
        m2_ref[...] = m2
        v2_ref[...] = v2

    spec = pl.BlockSpec((br, bc), lambda i, j: (i, j))
    body, extra_specs, extra = _ordered(body, 4, after)
    return pl.pallas_call(
        body, name=name, grid=(r // br, c // bc),
        in_specs=[spec] * 4 + extra_specs, out_specs=[spec] * 4,
        out_shape=[jax.ShapeDtypeStruct((r, c), F32)] * 4,
        compiler_params=_params(("parallel", "parallel")),
    )(w, g, m, v, *extra)


def _sum_rows8(g, name, after=None):
    n = g.shape[1]

    def body(g_ref, o_ref):
        acc = g_ref[0:1, :]
        for k in range(1, 8):
            acc = acc + g_ref[k:k + 1, :]
        o_ref[...] = acc

    body, extra_specs, extra = _ordered(body, 1, after)
    return pl.pallas_call(
        body, name=name, out_shape=jax.ShapeDtypeStruct((1, n), F32),
        in_specs=[pl.BlockSpec(memory_space=pltpu.VMEM)] + extra_specs,
        out_specs=pl.BlockSpec(memory_space=pltpu.VMEM),
        compiler_params=_params(),
    )(g, *extra)


def _mm(a, b, mode, out_dtype, name, bm=1024, bn=None, after=None, col_blocks=None):
    if mode == "nn":
        (_, m, k), (g, _, n) = a.shape, b.shape
    elif mode == "tn":
        (_, k, m), (g, _, n) = a.shape, b.shape
    else:
        (g, m, k), (_, n, _) = a.shape, b.shape
    if bn is None:
        bn = 1024 if k <= 2048 else 512
    bm, bn = _blk(m, bm), _blk(n, bn)

    if mode == "nt":
        def body(a_ref, b_ref, o_ref, acc_ref):
            part = lax.dot_general(a_ref[...], b_ref[...], (((1,), (1,)), ((), ())),
                                   preferred_element_type=F32)
            if g == 1:
                o_ref[...] = part.astype(out_dtype)
            else:
                gi = pl.program_id(2)

                @pl.when(gi == 0)
                def _():
                    acc_ref[...] = part

                @pl.when(gi > 0)
                def _():
                    acc_ref[...] += part

                @pl.when(gi == g - 1)
                def _():
                    o_ref[...] = acc_ref[...].astype(out_dtype)

        body, extra_specs, extra = _ordered(body, 2, after)
        return pl.pallas_call(
            body, name=name, grid=(m // bm, n // bn, g),
            in_specs=[pl.BlockSpec((None, bm, k), lambda i, j, gi: (gi, i, 0)),
                      pl.BlockSpec((None, bn, k), lambda i, j, gi: (gi, j, 0))] + extra_specs,
            out_specs=pl.BlockSpec((None, bm, bn), lambda i, j, gi: (0, i, j)),
            out_shape=jax.ShapeDtypeStruct((1, m, n), out_dtype),
            scratch_shapes=[pltpu.VMEM((bm, bn), F32)],
            compiler_params=_params(("parallel", "parallel", "arbitrary")),
        )(a, b, *extra)

    contract = (((1,), (0,)), ((), ())) if mode == "nn" else (((0,), (0,)), ((), ()))

    def body(a_ref, b_ref, o_ref):
        o_ref[...] = lax.dot_general(a_ref[...], b_ref[...], contract,
                                     preferred_element_type=F32).astype(out_dtype)

    a_spec = (pl.BlockSpec((None, bm, k), lambda i, gi, j: (0, i, 0)) if mode == "nn"
              else pl.BlockSpec((None, k, bm), lambda i, gi, j: (0, 0, i)))
    first, count = (0, n // bn) if col_blocks is None else col_blocks
    body, extra_specs, extra = _ordered(body, 2, after)
    return pl.pallas_call(
        body, name=name, grid=(m // bm, g, count),
        in_specs=[a_spec, pl.BlockSpec((None, k, bn), lambda i, gi, j: (gi, 0, first + j))] + extra_specs,
        out_specs=pl.BlockSpec((None, bm, bn), lambda i, gi, j: (gi, i, j)),
        out_shape=jax.ShapeDtypeStruct((g, m, count * bn), out_dtype),
        compiler_params=_params(("parallel", "parallel", "parallel")),
    )(a, b, *extra)


def _mm_plane(a, b, plane, name, planes_so_far=None, after=None, bm=1024, bn=1024):
    (m, k), (g, _, n) = a.shape, b.shape
    bm, bn = _blk(m, bm), _blk(n, bn)

    def body(plane_ref, a_ref, b_ref, *rest):
        rest[-1][...] = jnp.dot(a_ref[...], b_ref[...], preferred_element_type=F32).astype(BF16)

    extra_specs, extra, aliases = [], [], {}
    if planes_so_far is not None:
        extra_specs.append(ANY)
        extra.append(planes_so_far)
        aliases = {3: 0}
    if after is not None:
        extra_specs.append(ANY)
        extra.append(after)
    grid_spec = pltpu.PrefetchScalarGridSpec(
        num_scalar_prefetch=1, grid=(m // bm, n // bn),
        in_specs=[pl.BlockSpec((bm, k), lambda i, j, p: (i, 0)),
                  pl.BlockSpec((None, k, bn), lambda i, j, p: (p[0], 0, j))] + extra_specs,
        out_specs=pl.BlockSpec((None, bm, bn), lambda i, j, p: (p[0], i, j)))
    return pl.pallas_call(
        body, name=name, grid_spec=grid_spec,
        out_shape=jax.ShapeDtypeStruct((g, m, n), BF16), input_output_aliases=aliases,
        compiler_params=_params(("parallel", "parallel")),
    )(plane, a, b, *extra)


def _row_specs(br, d):
    return (pl.BlockSpec((br, d), lambda i: (i, 0)), pl.BlockSpec((1, d), lambda i: (0, 0)),
            pl.BlockSpec((8, d), lambda i: (0, 0)))


def _rstd(xv):
    return lax.rsqrt(jnp.mean(xv * xv, axis=-1, keepdims=True) + RMS_EPS)


def _colsum(v):
    return jnp.sum(v, axis=0, keepdims=True)


def _norm_mod_fwd(x, g, scale, shift, name, o=None, gate=None, after=None):
    s, d = x.shape
    br = _blk(s, 256, 8)
    has_res = o is not None
    row, vec, _ = _row_specs(br, d)

    def body(*refs):
        if has_res:
            x_ref, o_ref, gate_ref, g_ref, sc_ref, sh_ref, x1_ref, h_ref = refs
            xv = x_ref[...] + gate_ref[...] * o_ref[...]
            x1_ref[...] = xv
        else:
            x_ref, g_ref, sc_ref, sh_ref, h_ref = refs
            xv = x_ref[...]
        n = xv * _rstd(xv) * g_ref[...]
        h_ref[...] = (n * (1.0 + sc_ref[...]) + sh_ref[...]).astype(BF16)

    ins = [x] + ([o, gate] if has_res else []) + [g, scale, shift]
    in_specs = [row] + ([row, vec] if has_res else []) + [vec] * 3
    out_shape = ([jax.ShapeDtypeStruct((s, d), F32)] if has_res else []) + [jax.ShapeDtypeStruct((s, d), BF16)]
    body, extra_specs, extra = _ordered(body, len(ins), after)
    out = pl.pallas_call(
        body, name=name, grid=(s // br,), in_specs=in_specs + extra_specs, out_specs=[row] * len(out_shape),
        out_shape=out_shape, compiler_params=_params(("parallel",)),
    )(*ins, *extra)
    return out if has_res else out[0]


def _final_loss(x1, o1, gate1, final_g, tgt, name):
    s, d = x1.shape
    br = _blk(s, 256, 8)
    row, vec, acc = _row_specs(br, d)

    def body(x1_ref, o_ref, gate_ref, g_ref, t_ref, dx_ref, do_ref, acc_ref):
        @pl.when(pl.program_id(0) == 0)
        def _():
            acc_ref[...] = jnp.zeros_like(acc_ref)

        gate, o, g = gate_ref[...], o_ref[...], g_ref[...]
        x2 = x1_ref[...] + gate * o
        r = _rstd(x2)
        xh = x2 * r
        err = xh * g - t_ref[...]
        loss = 0.5 * _colsum(jnp.mean(err * err, axis=-1, keepdims=True))
        dout = err * (1.0 / d)
        dxh = dout * g
        dx2 = r * (dxh - xh * jnp.mean(dxh * xh, axis=-1, keepdims=True))
        dx_ref[...] = dx2
        do_ref[...] = (dx2 * gate).astype(BF16)
        acc_ref[0:1, :] += _colsum(dout * xh)
        acc_ref[1:2, :] += _colsum(dx2 * o)
        acc_ref[2:3, :] += jnp.broadcast_to(loss, (1, d))

    return pl.pallas_call(
        body, name=name, grid=(s // br,),
        in_specs=[row, row, vec, vec, row], out_specs=[row, row, acc],
        out_shape=[jax.ShapeDtypeStruct((s, d), F32), jax.ShapeDtypeStruct((s, d), BF16),
                   jax.ShapeDtypeStruct((8, d), F32)],
        compiler_params=_params(("arbitrary",)),
    )(x1, o1, gate1, final_g, tgt)


def _norm_mod_bwd(dh, x, g, scale, dx_next, name, o_prev=None, gate_prev=None, after=None):
    s, d = x.shape
    br = _blk(s, 256, 8)
    has_prev = o_prev is not None
    row, vec, acc = _row_specs(br, d)

    def body(*refs):
        if has_prev:
            dh_ref, x_ref, g_ref, sc_ref, dxn_ref, o_ref, gate_ref, dx_ref, do_ref, acc_ref = refs
        else:
            dh_ref, x_ref, g_ref, sc_ref, dxn_ref, dx_ref, acc_ref = refs

        @pl.when(pl.program_id(0) == 0)
        def _():
            acc_ref[...] = jnp.zeros_like(acc_ref)

        xv, gv, dhv = x_ref[...], g_ref[...], dh_ref[...]
        r = _rstd(xv)
        xh = xv * r
        acc_ref[0:1, :] += _colsum(dhv * (xh * gv))
        acc_ref[1:2, :] += _colsum(dhv)
        dn = dhv * (1.0 + sc_ref[...])
        acc_ref[2:3, :] += _colsum(dn * xh)
        dxh = dn * gv
        dx = dxn_ref[...] + r * (dxh - xh * jnp.mean(dxh * xh, axis=-1, keepdims=True))
        dx_ref[...] = dx
        if has_prev:
            acc_ref[3:4, :] += _colsum(dx * o_ref[...])
            do_ref[...] = (dx * gate_ref[...]).astype(BF16)

    ins = [dh, x, g, scale, dx_next] + ([o_prev, gate_prev] if has_prev else [])
    in_specs = [row, row, vec, vec, row] + ([row, vec] if has_prev else [])
    out_shape = [jax.ShapeDtypeStruct((s, d), F32)]
    out_specs = [row]
    if has_prev:
        out_shape.append(jax.ShapeDtypeStruct((s, d), BF16))
        out_specs.append(row)
    out_shape.append(jax.ShapeDtypeStruct((8, d), F32))
    out_specs.append(acc)
    body, extra_specs, extra = _ordered(body, len(ins), after)
    return pl.pallas_call(
        body, name=name, grid=(s // br,), in_specs=in_specs + extra_specs, out_specs=out_specs,
        out_shape=out_shape, compiler_params=_params(("arbitrary",)),
    )(*ins, *extra)


def _tiles(p):
    s, c = p.shape
    return p.reshape(s // SUBLANES, SUBLANES, c)


def _shift_down(p, k):
    if k == 0:
        return p
    r = pltpu.roll(_tiles(p), k, 1)
    before = jnp.concatenate([jnp.zeros_like(r[:1]), r[:-1]], axis=0)
    rows = lax.broadcasted_iota(jnp.int32, r.shape, 1)
    return jnp.where(rows >= k, r, before).reshape(p.shape)


def _shift_up(p, k):
    if k == 0:
        return p
    r = pltpu.roll(_tiles(p), SUBLANES - k, 1)
    after = jnp.concatenate([r[1:], jnp.zeros_like(r[:1])], axis=0)
    rows = lax.broadcasted_iota(jnp.int32, r.shape, 1)
    return jnp.where(rows < SUBLANES - k, r, after).reshape(p.shape)


def _sigmoid(z):
    return 0.5 * (jnp.tanh(0.5 * z) + 1.0)


def _sc_parts(proj_ref, w_ref):
    b, cg, v, g = (proj_ref[i].astype(F32) for i in range(4))
    p = cg * v
    u = w_ref[2:3, :] * p + w_ref[1:2, :] * _shift_down(p, 1) + w_ref[0:1, :] * _shift_down(p, 2)
    return b, cg, v, g, p, u


def _sc_fwd(proj, conv_w, name):
    _, s, e = proj.shape
    bc = _blk(e, 256)

    def body(proj_ref, w_ref, y_ref):
        b, _, _, g, _, u = _sc_parts(proj_ref, w_ref)
        y_ref[...] = (b * u * (g * _sigmoid(g))).astype(BF16)

    return pl.pallas_call(
        body, name=name, grid=(e // bc,),
        in_specs=[pl.BlockSpec((4, s, bc), lambda j: (0, 0, j)), pl.BlockSpec((3, bc), lambda j: (0, j))],
        out_specs=pl.BlockSpec((s, bc), lambda j: (0, j)),
        out_shape=jax.ShapeDtypeStruct((s, e), BF16),
        compiler_params=_params(("parallel",)),
    )(proj, conv_w)


def _sc_bwd(proj, dy, conv_w, name, after=None):
    _, s, e = proj.shape
    bc = _blk(e, 256)

    def body(proj_ref, dy_ref, w_ref, dp_ref, dw_ref):
        b, cg, v, g, p, u = _sc_parts(proj_ref, w_ref)
        dyv = dy_ref[...].astype(F32)
        sig = _sigmoid(g)
        t = dyv * (g * sig)
        du = t * b
        dp_ref[0] = (t * u).astype(BF16)
        dp_ref[3] = (dyv * b * u * (sig * (1.0 + g * (1.0 - sig)))).astype(BF16)
        dpp = w_ref[2:3, :] * du + w_ref[1:2, :] * _shift_up(du, 1) + w_ref[0:1, :] * _shift_up(du, 2)
        dp_ref[1] = (dpp * v).astype(BF16)
        dp_ref[2] = (dpp * cg).astype(BF16)
        dw_ref[2:3, :] = _colsum(du * p)
        dw_ref[1:2, :] = _colsum(du * _shift_down(p, 1))
        dw_ref[0:1, :] = _colsum(du * _shift_down(p, 2))

    body, extra_specs, extra = _ordered(body, 3, after)
    return pl.pallas_call(
        body, name=name, grid=(e // bc,),
        in_specs=[pl.BlockSpec((4, s, bc), lambda j: (0, 0, j)), pl.BlockSpec((s, bc), lambda j: (0, j)),
                  pl.BlockSpec((3, bc), lambda j: (0, j))] + extra_specs,
        out_specs=[pl.BlockSpec((4, s, bc), lambda j: (0, 0, j)), pl.BlockSpec((3, bc), lambda j: (0, j))],
        out_shape=[jax.ShapeDtypeStruct((4, s, e), BF16), jax.ShapeDtypeStruct((3, e), F32)],
        compiler_params=_params(("parallel",)),
    )(proj, dy, conv_w, *extra)


def _softplus_neg(lam):
    u = jnp.exp(-jnp.abs(lam))
    w = 1.0 + u
    log1p = jnp.where(w == 1.0, u, jnp.log(w) * (u / jnp.where(w == 1.0, 1.0, w - 1.0)))
    return jnp.maximum(-lam, 0.0) + log1p


def _one_minus_exp(z):
    series = -z * (1.0 + z * (0.5 + z * (1.0 / 6.0 + z * (1.0 / 24.0))))
    return jnp.where(z > -0.02, series, 1.0 - jnp.exp(z))


def _scan_in_tiles(a, b, reverse):
    shape = a.shape
    a, b = _tiles(a), _tiles(b)
    rows = lax.broadcasted_iota(jnp.int32, a.shape, 1)
    for step in (1, 2, 4):
        shift = SUBLANES - step if reverse else step
        ok = rows < SUBLANES - step if reverse else rows >= step
        a_s, b_s = pltpu.roll(a, shift, 1), pltpu.roll(b, shift, 1)
        b = jnp.where(ok, a * b_s + b, b)
        a = jnp.where(ok, a * a_s, a)
    return a.reshape(shape), b.reshape(shape)


def _scan(a, b, a_ref, b_ref, h_ref, reverse):
    s, c = a.shape
    n = s // 8
    a_t, b_t = _scan_in_tiles(a, b, reverse)
    a_ref[...] = a_t
    b_ref[...] = b_t

    def step(i, carry):
        gi = n - 1 - i if reverse else i
        sl = pl.ds(pl.multiple_of(gi * 8, 8), 8)
        h = b_ref[sl, :] + a_ref[sl, :] * carry
        h_ref[sl, :] = h
        return h[0:1, :] if reverse else h[7:8, :]

    lax.fori_loop(0, n, step, jnp.zeros((1, c), F32))
    return h_ref[...]


def _lru_specs(s, e_half, n_heads):
    hp = e_half // HEAD_DIM
    c = HEAD_DIM
    return dict(
        pair=pl.BlockSpec((2, None, s, c), lambda h: (0, h // hp, 0, h % hp)),
        conv_w=pl.BlockSpec((4, c), lambda h: (0, h)),
        chan=pl.BlockSpec((1, c), lambda h: (0, h)),
        w=pl.BlockSpec((4, None, c // 4, c), lambda h: (0, h, 0, 0)),
        bias=pl.BlockSpec((None, 1, c), lambda h: (h, 0, 0)),
        plane=pl.BlockSpec((s, c), lambda h: (0, h)),
    )


def _lru_forward_parts(vp, cw_ref, cb_ref, wa_ref, ba_ref, wx_ref, bx_ref, lam_ref):
    c = HEAD_DIM
    taps = [_shift_down(vp, 3 - k) for k in range(4)]
    v = cb_ref[...] + sum(cw_ref[k:k + 1, :] * taps[k] for k in range(4))
    vb = v.astype(BF16)
    wa = wa_ref[...].reshape(c, c)
    wx = wx_ref[...].reshape(c, c)
    r = _sigmoid(jnp.dot(vb, wa, preferred_element_type=F32) + ba_ref[...])
    i = _sigmoid(jnp.dot(vb, wx, preferred_element_type=F32) + bx_ref[...])
    sp = _softplus_neg(lam_ref[...])
    la = (-RGLRU_C) * sp * r
    a = jnp.exp(la)
    nm = jnp.sqrt(_one_minus_exp(2.0 * la))
    return taps, v, vb, wa, wx, r, i, sp, a, nm


def _lru_fwd(proj, conv_w, conv_b, w_a, b_a, w_x, b_x, lam, name):
    _, _, s, e_half = proj.shape
    n_heads = 2 * e_half // HEAD_DIM
    sp_ = _lru_specs(s, e_half, n_heads)

    def body(pg_ref, cw_ref, cb_ref, wa_ref, ba_ref, wx_ref, bx_ref, lam_ref,
             y_ref, a_ref, hs_ref, sa_ref, sb_ref, sh_ref):
        _, v, _, _, _, _, i, _, a, nm = _lru_forward_parts(
            pg_ref[0].astype(F32), cw_ref, cb_ref, wa_ref, ba_ref, wx_ref, bx_ref, lam_ref)
        hs = _scan(a, nm * (i * v), sa_ref, sb_ref, sh_ref, reverse=False)
        g = pg_ref[1].astype(F32)
        y_ref[...] = (hs * (g * _sigmoid(g))).astype(BF16)
        a_ref[...] = a
        hs_ref[...] = hs.astype(BF16)

    e = 2 * e_half
    return pl.pallas_call(
        body, name=name, grid=(n_heads,),
        in_specs=[sp_["pair"], sp_["conv_w"], sp_["chan"], sp_["w"], sp_["bias"], sp_["w"],
                  sp_["bias"], sp_["chan"]],
        out_specs=[sp_["plane"]] * 3,
        out_shape=[jax.ShapeDtypeStruct((s, e), BF16), jax.ShapeDtypeStruct((s, e), F32),
                   jax.ShapeDtypeStruct((s, e), BF16)],
        scratch_shapes=[pltpu.VMEM((s, HEAD_DIM), F32)] * 3,
        compiler_params=_params(("parallel",)),
    )(proj, conv_w, conv_b, w_a, b_a, w_x, b_x, lam)


def _lru_bwd(proj, dy, saved, conv_w, conv_b, w_a, b_a, w_x, b_x, lam, name):
    _, _, s, e_half = proj.shape
    e = 2 * e_half
    c = HEAD_DIM
    n_heads = e // c
    sp_ = _lru_specs(s, e_half, n_heads)

    def body(pg_ref, dy_ref, a_ref, hs_ref, cw_ref, cb_ref, wa_ref, ba_ref, wx_ref, bx_ref, lam_ref,
             dpg_ref, dwa_ref, dwx_ref, dba_ref, dbx_ref, dlam_ref, dcw_ref, dcb_ref,
             sa_ref, sb_ref, sd_ref):
        vp = pg_ref[0].astype(F32)
        taps = [_shift_down(vp, 3 - k) for k in range(4)]
        v = cb_ref[...] + sum(cw_ref[k:k + 1, :] * taps[k] for k in range(4))
        vb = v.astype(BF16)
        wa = wa_ref[...].reshape(c, c)
        wx = wx_ref[...].reshape(c, c)
        r = _sigmoid(jnp.dot(vb, wa, preferred_element_type=F32) + ba_ref[...])
        i = _sigmoid(jnp.dot(vb, wx, preferred_element_type=F32) + bx_ref[...])
        a, hs = a_ref[...], hs_ref[...].astype(F32)
        sp = _softplus_neg(lam_ref[...])
        nm = jnp.sqrt((1.0 - a) * (1.0 + a))
        iv = i * v
        g = pg_ref[1].astype(F32)
        dyv = dy_ref[...].astype(F32)
        sig = _sigmoid(g)
        dpg_ref[1] = (dyv * hs * (sig * (1.0 + g * (1.0 - sig)))).astype(BF16)
        dh = _scan(_shift_up(a, 1), dyv * (g * sig), sa_ref, sb_ref, sd_ref, reverse=True)
        da = dh * _shift_down(hs, 1)
        div = dh * nm
        a2 = a * a
        dla = da * a - (dh * iv) * (a2 / nm)
        dzr = (dla * ((-RGLRU_C) * sp)) * (r * (1.0 - r))
        dzi = (div * v) * (i * (1.0 - i))
        lam = lam_ref[...]
        dlam_ref[...] = _colsum(dla * ((-RGLRU_C) * r)) * (-_sigmoid(-lam))
        dba_ref[...] = _colsum(dzr)
        dbx_ref[...] = _colsum(dzi)
        dzr_b, dzi_b = dzr.astype(BF16), dzi.astype(BF16)
        tn = (((0,), (0,)), ((), ()))
        nt = (((1,), (1,)), ((), ()))
        dwa_ref[...] = lax.dot_general(vb, dzr_b, tn, preferred_element_type=F32)
        dwx_ref[...] = lax.dot_general(vb, dzi_b, tn, preferred_element_type=F32)
        dv = (div * i + lax.dot_general(dzr_b, wa, nt, preferred_element_type=F32)
              + lax.dot_general(dzi_b, wx, nt, preferred_element_type=F32))
        dcb_ref[...] = _colsum(dv)
        dvp = jnp.zeros_like(dv)
        for k in range(4):
            dvp = dvp + cw_ref[k:k + 1, :] * _shift_up(dv, 3 - k)
            dcw_ref[k:k + 1, :] = _colsum(dv * taps[k])
        dpg_ref[0] = dvp.astype(BF16)

    head_mat = pl.BlockSpec((None, c, c), lambda h: (h, 0, 0))
    outs = pl.pallas_call(
        body, name=name, grid=(n_heads,),
        in_specs=[sp_["pair"]] + [sp_["plane"]] * 3 + [sp_["conv_w"], sp_["chan"], sp_["w"], sp_["bias"],
                                                        sp_["w"], sp_["bias"], sp_["chan"]],
        out_specs=[sp_["pair"], head_mat, head_mat, sp_["bias"], sp_["bias"],
                   sp_["chan"], sp_["conv_w"], sp_["chan"]],
        out_shape=[jax.ShapeDtypeStruct((2, 2, s, e_half), BF16),
                   jax.ShapeDtypeStruct((n_heads, c, c), F32), jax.ShapeDtypeStruct((n_heads, c, c), F32),
                   jax.ShapeDtypeStruct((n_heads, 1, c), F32), jax.ShapeDtypeStruct((n_heads, 1, c), F32),
                   jax.ShapeDtypeStruct((1, e), F32), jax.ShapeDtypeStruct((4, e), F32),
                   jax.ShapeDtypeStruct((1, e), F32)],
        scratch_shapes=[pltpu.VMEM((s, c), F32)] * 3,
        compiler_params=_params(("parallel",)),
    )(proj, dy, *saved, conv_w, conv_b, w_a, b_a, w_x, b_x, lam)
    return tuple(outs)


def _ada_fwd(c_all, ada_w, name):
    n_l, d, f = ada_w.shape
    bf = _blk(f, 512)

    def body(c_ref, w_ref, o_ref):
        cv = c_ref[...]
        sc = (cv * _sigmoid(cv)).astype(BF16)
        o_ref[...] = jnp.dot(sc, w_ref[...].astype(BF16), preferred_element_type=F32)

    return pl.pallas_call(
        body, name=name, grid=(n_l, f // bf),
        in_specs=[pl.BlockSpec((8, d), lambda l, j: (0, 0)), pl.BlockSpec((None, d, bf), lambda l, j: (l, 0, j))],
        out_specs=pl.BlockSpec((None, 8, bf), lambda l, j: (l, 0, j)),
        out_shape=jax.ShapeDtypeStruct((n_l, 8, f), F32),
        compiler_params=_params(("parallel", "parallel")),
    )(c_all, ada_w)


def _ada_bwd_adamw(c_t, dmod, w, m, v, name, after=None):
    n_l, d, f = w.shape
    bf = _blk(f, 256)

    def body(c_ref, dm_ref, w_ref, m_ref, v_ref, g_ref, d_ref, m2_ref, v2_ref):
        cv = c_ref[...]
        sc = cv * _sigmoid(cv)
        dm = dm_ref[...]
        g = sc[:, 0:1] * dm[0:1, :]
        for b in range(1, 8):
            g = g + sc[:, b:b + 1] * dm[b:b + 1, :]
        g_ref[...] = g
        dl, m2, v2 = _adamw_math(w_ref[...], g, m_ref[...], v_ref[...])
        d_ref[...] = dl
        m2_ref[...] = m2
        v2_ref[...] = v2

    big = pl.BlockSpec((None, d, bf), lambda l, j: (l, 0, j))
    body, extra_specs, extra = _ordered(body, 5, after)
    return pl.pallas_call(
        body, name=name, grid=(n_l, f // bf),
        in_specs=[pl.BlockSpec((d, 8), lambda l, j: (0, 0)), pl.BlockSpec((None, 8, bf), lambda l, j: (l, 0, j)),
                  big, big, big] + extra_specs,
        out_specs=[big] * 4, out_shape=[jax.ShapeDtypeStruct((n_l, d, f), F32)] * 4,
        compiler_params=_params(("parallel", "parallel")),
    )(c_t, dmod, w, m, v, *extra)


def _pack(parts):
    padded, offs, n = [], [], 0
    for p in parts:
        p = p.reshape(-1)
        size = -(-p.shape[0] // PACK) * PACK
        offs.append(n)
        n += size
        padded.append(jnp.pad(p, (0, size - p.shape[0])) if size != p.shape[0] else p)
    return jnp.concatenate(padded), offs, n


def kernel(x, c, norm_g, ada_w, ada_b, sc_w_in, sc_conv_w, sc_w_out, lru_w_in, lru_conv_w, lru_conv_b, lru_w_a, lru_b_a, lru_w_x, lru_b_x, lru_lambda, lru_w_out, final_g, loss_target, m_norm_g, m_ada_w, m_ada_b, m_sc_w_in, m_sc_conv_w, m_sc_w_out, m_lru_w_in, m_lru_conv_w, m_lru_conv_b, m_lru_w_a, m_lru_b_a, m_lru_w_x, m_lru_b_x, m_lru_lambda, m_lru_w_out, m_final_g, v_norm_g, v_ada_w, v_ada_b, v_sc_w_in, v_sc_conv_w, v_sc_w_out, v_lru_w_in, v_lru_conv_w, v_lru_conv_b, v_lru_w_a, v_lru_b_a, v_lru_w_x, v_lru_b_x, v_lru_lambda, v_lru_w_out, v_final_g):
    xi, yi, ci = _pos()
    chip = 2 * xi + yi
    batch = 4 * xi + 2 * yi + ci
    core_op = jnp.reshape(ci, (1,)).astype(jnp.int32)
    chip_op = jnp.reshape(chip, (1,)).astype(jnp.int32)
    where_op = jnp.stack([chip, ci]).astype(jnp.int32)

    x2d, tgt = x[0], loss_target[0]
    s, d = x2d.shape
    es = sc_conv_w.shape[2]
    e = 4 * es
    n_heads = lru_w_a.shape[1]
    hj = lru_b_a.shape[2]
    f = ada_w.shape[2]
    row = lambda t: t.reshape(1, -1)

    small_parts = [c, sc_conv_w, lru_conv_w, lru_conv_b, lru_b_a, lru_b_x, lru_lambda]
    small, offs, n_small = _pack(small_parts)
    got = _allgather8([small.reshape(8, n_small // 8)], "ag_small")[0].reshape(8, n_small)
    c_all = got[:, :d]
    per_chip = got[0::2]

    def chip_part(k, shape):
        size = 1
        for dim in shape:
            size *= dim
        return per_chip[:, offs[k]:offs[k] + size].reshape((4,) + shape)

    conv_w0 = jnp.transpose(chip_part(1, (3, es)), (1, 0, 2)).reshape(3, e)
    conv_w1 = jnp.transpose(chip_part(2, (4, es)), (1, 0, 2)).reshape(4, e)
    conv_b1 = chip_part(3, (es,)).reshape(1, e)
    b_a = jnp.transpose(chip_part(4, (n_heads, hj)), (1, 0, 2)).reshape(n_heads, 1, 4 * hj)
    b_x = jnp.transpose(chip_part(5, (n_heads, hj)), (1, 0, 2)).reshape(n_heads, 1, 4 * hj)
    lam = chip_part(6, (es,)).reshape(1, e)

    mod_nb = _ada_fwd(c_all, ada_w, "ada_fwd")
    mods = _allgather8([mod_nb.reshape(16, f)], "ag_mod")[0].reshape(8, 2, 8, f)[0::2]
    mine = lax.dynamic_index_in_dim(mods, batch, axis=2, keepdims=False)
    mod = jnp.transpose(mine, (1, 0, 2)).reshape(2, 4 * f) + ada_b
    shift = [row(mod[l, :d]) for l in range(2)]
    scale = [row(mod[l, d:2 * d]) for l in range(2)]
    gate = [row(mod[l, 2 * d:]) for l in range(2)]
    ng = [row(norm_g[l]) for l in range(2)]

    shards = [sc_w_in[0], sc_w_out[0], lru_w_in[0], lru_w_a[0].reshape(n_heads * hj, HEAD_DIM),
              lru_w_x[0].reshape(n_heads * hj, HEAD_DIM), lru_w_out[0]]
    names = ["sc_w_in", "sc_w_out", "lru_w_in", "lru_w_a", "lru_w_x", "lru_w_out"]
    slots = [_cast_into_slot(chip_op, w, "cast_" + nm) for w, nm in zip(shards, names)]
    send_a, recv_a, buf, started = _first_start(slots[0], mod, "ag_first_start")
    h0 = _norm_mod_fwd(x2d, ng[0], scale[0], shift[0], "norm0", after=started)
    planes = [jnp.reshape(2 * px + py, (1,)).astype(jnp.int32) for px, py in _other_chips(xi, yi)]
    proj0 = _mm_plane(h0, buf.reshape(4, d, e), chip_op, "sc_in_own")
    relays, passed, relayed = [], [], [proj0] + slots[1:]
    for j in range(2):
        send_b, recv_b, send_c, recv_c, buf, token = _first_relay(send_a, recv_a, buf, j, relayed,
                                                                  "ag_first_relay_%d" % j)
        relays.append((send_c, recv_c))
        passed.append((send_b, recv_b))
        relayed = [token]
    rest_flight, relayed = _gather_start([[slots[1]], slots[2:5], [slots[5]]], token, "ag_start")
    for j in range(2):
        buf = _first_relay_done(*passed[j], buf, j, relayed, "ag_first_relay_done_%d" % j)
        proj0 = _mm_plane(h0, buf.reshape(4, d, e), planes[j], "sc_in_%d" % j, planes_so_far=proj0)
        relayed = proj0
    send_b, recv_b, buf, relayed = _first_diagonal(relays, buf, proj0, "ag_first_diagonal")
    buf = _first_diagonal_done(send_b, recv_b, relays, buf, relayed, "ag_first_diagonal_done")
    proj0 = _mm_plane(h0, buf.reshape(4, d, e), planes[2], "sc_in_2", planes_so_far=proj0)
    w_in0 = buf.reshape(4, d, e)
    in_flight = [None] + rest_flight

    def arrived(g, after, tag):
        send1, recv1, bufs = in_flight[g]
        send2, recv2, bufs, passed = _gather_forward(send1, recv1, bufs, after, "ag_forward_" + tag)
        return _gather_finish(send2, recv2, bufs, passed, "ag_finish_" + tag)

    w_out0 = arrived(1, proj0, "sc_w_out")[0].reshape(1, e, d)
    y0 = _sc_fwd(proj0, conv_w0, "sc_mix")
    o0 = _mm(y0[None], w_out0, "nn", F32, "sc_out")[0]
    x1, h1 = _norm_mod_fwd(x2d, ng[1], scale[1], shift[1], "norm1", o=o0, gate=gate[0])
    lru_ws = arrived(2, h1, "lru_w_in")
    w_in1 = lru_ws[0].reshape(4, d, e // 2)
    w_a = lru_ws[1].reshape(4, n_heads, hj, HEAD_DIM)
    w_x = lru_ws[2].reshape(4, n_heads, hj, HEAD_DIM)
    proj1 = _mm(h1[None], w_in1, "nn", BF16, "lru_in")
    pairs1 = proj1.reshape(2, 2, s, e // 2)
    y1, *lru_saved = _lru_fwd(pairs1, conv_w1, conv_b1, w_a, b_a, w_x, b_x, lam, "lru_mix")
    w_out1 = arrived(3, y1, "lru_w_out")[0].reshape(1, e, d)
    o1 = _mm(y1[None], w_out1, "nn", F32, "lru_out")[0]
    dx2, do1, acc_f = _final_loss(x1, o1, gate[1], row(final_g), tgt, "final_loss")

    def reduce_stage1(tag, grads):
        lands = [lax.empty((4, g.shape[1] // 2, g.shape[2]), F32) for g in grads]
        return _exchange_start("rs_sibling_start_" + tag, grads, lands, len(grads),
                               _plan_other_half_to_sibling, None)

    def reduce_stage2(tag, stage1, nms, after):
        send, recv, grads, lands, _ = stage1
        grads, lands = _exchange_wait("rs_sibling_wait_" + tag, send, recv, grads, lands,
                                      _plan_other_half_to_sibling, after)
        parts = [_sum_own_and_sibling(core_op, g, r1, "rs_sum1_" + nm) for g, r1, nm in zip(grads, lands, nms)]
        lands = [lax.empty((3,) + p.shape[1:], BF16) for p in parts]
        return _exchange_start("rs_chips_start_" + tag, parts, lands, 3 * len(parts),
                               _plan_partials_to_chips, None)

    def reduce_stage3(tag, stage2, nms, after):
        send, recv, parts, lands, _ = stage2
        parts, lands = _exchange_wait("rs_chips_wait_" + tag, send, recv, parts, lands,
                                      _plan_partials_to_chips, after)
        halves = [_sum_chips(where_op, p, r2, "rs_sum2_" + nm) for p, r2, nm in zip(parts, lands, nms)]
        return _exchange_start("rs_share_start_" + tag, [], halves, len(halves), _plan_share_half, None)

    def reduce_done(tag, stage3, after):
        send, recv, _, fulls, _ = stage3
        return _exchange_wait("rs_share_wait_" + tag, send, recv, [], fulls, _plan_share_half, after)[1]

    def head_major_to_chip_major(t):
        return jnp.transpose(t.reshape(n_heads, 4, hj, HEAD_DIM), (1, 0, 2, 3)).reshape(4, n_heads * hj, HEAD_DIM)

    big_state = dict(zip(names, zip(shards, [m_sc_w_in, m_sc_w_out, m_lru_w_in, m_lru_w_a, m_lru_w_x, m_lru_w_out],
                                    [v_sc_w_in, v_sc_w_out, v_lru_w_in, v_lru_w_a, v_lru_w_x, v_lru_w_out],
                                    [sc_w_in, sc_w_out, lru_w_in, lru_w_a, lru_w_x, lru_w_out])))
    big = {}

    def update(nms, fulls, after):
        for nm, g2 in zip(nms, fulls):
            w2, m4, v4, w4 = big_state[nm]
            outs = _adamw(w2, g2, m4.reshape(w2.shape), v4.reshape(w2.shape), "adamw_" + nm, after=after)
            big[nm] = tuple(t.reshape(w4.shape) for t in outs)
            after = outs[1]
        return after

    g_w_out1 = _mm(y1[None], do1[None], "tn", F32, "lru_out_dw", bm=512, bn=2048)
    dy1 = _mm(do1[None], w_out1, "nt", BF16, "lru_out_dx")[0]
    dpairs1, g_wa, g_wx, g_ba, g_bx, g_lam, g_cw1, g_cb1 = _lru_bwd(
        pairs1, dy1, lru_saved, conv_w1, conv_b1, w_a, b_a, w_x, b_x, lam, "lru_mix_bwd")
    dproj1 = dpairs1.reshape(4, s, e // 2)
    g_w_in1 = _mm(h1[None], dproj1, "tn", F32, "lru_in_dw", bm=512, bn=2048)
    lru_names = ["lru_w_out", "lru_w_a", "lru_w_x", "lru_w_in"]
    lru_rs = reduce_stage1("lru", [g_w_out1.reshape(4, es, d), head_major_to_chip_major(g_wa),
                                   head_major_to_chip_major(g_wx), g_w_in1])
    dh1 = _mm(dproj1, w_in1, "nt", F32, "lru_in_dx", after=lru_rs[4])[0]
    lru_rs = reduce_stage2("lru", lru_rs, lru_names, dh1)
    dx1, do0, acc1 = _norm_mod_bwd(dh1, x1, ng[1], scale[1], dx2, "norm1_bwd", o_prev=o0, gate_prev=gate[0],
                                   after=lru_rs[4])

    g_w_out0 = _mm(y0[None], do0[None], "tn", F32, "sc_out_dw", bm=512, bn=2048)
    out_rs = reduce_stage1("sc_out", [g_w_out0.reshape(4, es, d)])
    dy0 = _mm(do0[None], w_out0, "nt", BF16, "sc_out_dx", after=out_rs[4])[0]
    out_rs = reduce_stage2("sc_out", out_rs, ["sc_w_out"], dy0)
    dproj0, g_cw0 = _sc_bwd(proj0, dy0, conv_w0, "sc_mix_bwd", after=out_rs[4])
    stage1, stage2, after = [], [], None
    for k in range(IN_PIECES):
        g_piece = _mm(h0[None], dproj0, "tn", F32, "sc_in_dw_%d" % k, bm=512, bn=e // IN_PIECES,
                      col_blocks=(k, 1), after=after)
        stage1.append(reduce_stage1("sc_in_%d" % k, [g_piece]))
        after = stage1[k][4]
        if k:
            stage2.append(reduce_stage2("sc_in_%d" % (k - 1), stage1[k - 1], ["sc_w_in_%d" % (k - 1)], after))
            after = stage2[k - 1][4]
    lru_rs = reduce_stage3("lru", lru_rs, lru_names, after)
    k = IN_PIECES - 1
    stage2.append(reduce_stage2("sc_in_%d" % k, stage1[k], ["sc_w_in_%d" % k], lru_rs[4]))
    dh0 = _mm(dproj0, w_in0, "nt", F32, "sc_in_dx", after=stage2[k][4])[0]
    grad_x, acc0 = _norm_mod_bwd(dh0, x2d, ng[0], scale[0], dx1, "norm0_bwd")
    out_rs = reduce_stage3("sc_out", out_rs, ["sc_w_out"], acc0)
    last = update(lru_names, reduce_done("lru", lru_rs, out_rs[4]), None)
    last = update(["sc_w_out"], reduce_done("sc_out", out_rs, last), None)
    g_half = None
    for k, (send, recv, parts, lands, _) in enumerate(stage2):
        parts, lands = _exchange_wait("rs_chips_wait_sc_in_%d" % k, send, recv, parts, lands,
                                      _plan_partials_to_chips, last)
        g_half = _sum_chips(where_op, parts[0], lands[0], "rs_sum2_sc_w_in_%d" % k, piece=(k, IN_PIECES),
                            so_far=g_half)
    in_rs = _exchange_start("rs_share_start_sc_in", [], [g_half], 1, _plan_share_half, None)

    dmod = jnp.stack([jnp.concatenate([acc0[1], acc0[0], acc1[3]]),
                      jnp.concatenate([acc1[1], acc1[0], acc_f[1]])])
    part_list = [jnp.stack([acc0[2], acc1[2]]), acc_f[0], acc_f[2, :1], g_cw0, g_cw1, g_cb1, g_ba, g_bx,
                 g_lam, dmod]
    partials, poffs, n_part = _pack(part_list)
    every = _allgather8([partials.reshape(8, n_part // 8)], "ag_partials", after=in_rs[4])[0].reshape(8, n_part)
    total = _sum_rows8(every, "sum_partials")[0]

    def tot(k, shape):
        size = 1
        for dim in shape:
            size *= dim
        return total[poffs[k]:poffs[k] + size].reshape(shape)

    def my_cols(t, width):
        return lax.dynamic_slice_in_dim(t, chip * width, width, axis=t.ndim - 1)

    loss = tot(2, (1,))[0]
    g_norm_g, g_final_g, g_ada_b = tot(0, (2, d)), tot(1, (d,)), tot(9, (2, 3 * d))
    g_sc_conv_w = my_cols(tot(3, (3, e)), es)[None]
    g_lru_conv_w = my_cols(tot(4, (4, e)), es)[None]
    g_lru_conv_b = my_cols(tot(5, (1, e)), es)
    g_lru_b_a = my_cols(tot(6, (n_heads, 4 * hj)), hj)[None]
    g_lru_b_x = my_cols(tot(7, (n_heads, 4 * hj)), hj)[None]
    g_lru_lambda = my_cols(tot(8, (1, e)), es)

    dmod_all = every[:, poffs[9]:poffs[9] + 6 * d].reshape(8, 2, 3 * d)
    dmod_mine = jnp.transpose(my_cols(dmod_all, f), (1, 0, 2))
    ada = _ada_bwd_adamw(jnp.transpose(c_all), dmod_mine, ada_w, m_ada_w, v_ada_w, "ada_bwd_adamw", after=total)

    small_names = ["norm_g", "ada_b", "final_g", "sc_conv_w", "lru_conv_w", "lru_conv_b", "lru_b_a",
                   "lru_b_x", "lru_lambda"]
    small_w = [norm_g, ada_b, final_g, sc_conv_w, lru_conv_w, lru_conv_b, lru_b_a, lru_b_x, lru_lambda]
    small_g = [g_norm_g, g_ada_b, g_final_g, g_sc_conv_w, g_lru_conv_w, g_lru_conv_b, g_lru_b_a,
               g_lru_b_x, g_lru_lambda]
    small_m = [m_norm_g, m_ada_b, m_final_g, m_sc_conv_w, m_lru_conv_w, m_lru_conv_b, m_lru_b_a,
               m_lru_b_x, m_lru_lambda]
    small_v = [v_norm_g, v_ada_b, v_final_g, v_sc_conv_w, v_lru_conv_w, v_lru_conv_b, v_lru_b_a,
               v_lru_b_x, v_lru_lambda]
    pw, soffs, n_s = _pack(small_w)
    pg, pm, pv = _pack(small_g)[0], _pack(small_m)[0], _pack(small_v)[0]
    shape2 = (n_s // PACK, PACK)
    _, pd, pm2, pv2 = _adamw(pw.reshape(shape2), pg.reshape(shape2), pm.reshape(shape2), pv.reshape(shape2),
                             "adamw_small", after=ada[0])
    update(["sc_w_in"], reduce_done("sc_in", in_rs, pd), None)
    small = {}
    for k, (nm, w_) in enumerate(zip(small_names, small_w)):
        take = lambda t: t.reshape(-1)[soffs[k]:soffs[k] + w_.size].reshape(w_.shape)
        small[nm] = (small_g[k].reshape(w_.shape), take(pd), take(pm2), take(pv2))

    results = dict(small)
    results.update(big)
    results["ada_w"] = tuple(ada)
    order = ["norm_g", "ada_w", "ada_b", "sc_w_in", "sc_conv_w", "sc_w_out", "lru_w_in", "lru_conv_w",
             "lru_conv_b", "lru_w_a", "lru_b_a", "lru_w_x", "lru_b_x", "lru_lambda", "lru_w_out", "final_g"]
    out = [loss, grad_x[None]]
    for kind in range(4):
        out += [results[nm][kind] for nm in order]
    return tuple(out)
```

```python
import functools

import jax
import jax.numpy as jnp
from jax import lax
from jax.experimental import pallas as pl
from jax.experimental.pallas import tpu as pltpu

F32 = jnp.float32
BF16 = jnp.bfloat16
MESH = pl.DeviceIdType.MESH
ANY = pl.BlockSpec(memory_space=pl.ANY)

RMS_EPS = 1e-6
RGLRU_C = 8.0
HEAD_DIM = 256
ADAM_LR = 0.001
ADAM_B1 = 0.9
ADAM_B2 = 0.999
ADAM_EPS = 1e-08
ADAM_WD = 0.01
ADAM_STEP = 10
V7X_VMEM_LIMIT = 56 * 1024 * 1024
IN_PIECES = 2
LANES = 128
SUBLANES = 8
PACK = SUBLANES * LANES


def _blk(dim, pref, unit=LANES):
    if dim <= pref:
        return dim
    b = (pref // unit) * unit
    while b > unit and dim % b:
        b -= unit
    assert dim % b == 0, (dim, pref, unit)
    return b


def _params(sem=None):
    return pltpu.CompilerParams(dimension_semantics=sem, vmem_limit_bytes=V7X_VMEM_LIMIT)


def _ordered(body, n_in, after):
    if after is None:
        return body, [], []

    def ordered_body(*refs):
        return body(*refs[:n_in], *refs[n_in + 1:])

    return ordered_body, [ANY], [after]


def _pos():
    return lax.axis_index("x"), lax.axis_index("y"), lax.axis_index("c")


def _other_chips(x, y):
    return [(1 - x, y), (x, 1 - y), (1 - x, 1 - y)]


def _allgather8(arrs, name, after=None):
    n_t = len(arrs)
    ms = [a.shape[0] for a in arrs]

    def gather(*refs):
        ins, outs = refs[:n_t], refs[n_t:2 * n_t]
        send_sems, recv_sems, local_sems = refs[2 * n_t:]
        x, y, c = _pos()
        me, sibling = (x, y, c), (x, y, 1 - c)
        chips = _other_chips(x, y)

        def rows(t, px, py, pc):
            return outs[t].at[pl.ds((4 * px + 2 * py + pc) * ms[t], ms[t])]

        def copy(t, k, block, to, src=None):
            return pltpu.make_async_remote_copy(
                src_ref=rows(t, *block) if src is None else src, dst_ref=rows(t, *block),
                send_sem=send_sems.at[7 * t + k], recv_sem=recv_sems.at[7 * t + k],
                device_id=to, device_id_type=MESH)

        mine, first, passed = [], [], []
        for t in range(n_t):
            src = ins[t]
            cp = pltpu.make_async_copy(src, rows(t, *me), local_sems.at[t])
            cp.start()
            mine.append(cp)
            sends = [copy(t, 0, me, sibling, src=src)]
            sends += [copy(t, 1 + j, me, (*chip, c), src=src) for j, chip in enumerate(chips)]
            for cp in sends:
                cp.start()
            first += sends
        for t in range(n_t):
            for j, chip in enumerate(chips):
                copy(t, 1 + j, (*chip, c), me).wait_recv()
                cp = copy(t, 4 + j, (*chip, c), sibling)
                cp.start()
                passed.append(cp)
        for t in range(n_t):
            copy(t, 0, sibling, me).wait_recv()
            for j, chip in enumerate(chips):
                copy(t, 4 + j, (*chip, 1 - c), me).wait_recv()
        for cp in first + passed:
            cp.wait_send()
        for cp in mine:
            cp.wait()

    body, extra_specs, extra = _ordered(gather, n_t, after)
    return pl.pallas_call(
        body, name=name,
        out_shape=[jax.ShapeDtypeStruct((8 * m, a.shape[1]), a.dtype) for m, a in zip(ms, arrs)],
        in_specs=[ANY] * n_t + extra_specs, out_specs=[ANY] * n_t,
        scratch_shapes=[pltpu.SemaphoreType.DMA((7 * n_t,)), pltpu.SemaphoreType.DMA((7 * n_t,)),
                        pltpu.SemaphoreType.DMA((n_t,))],
    )(*arrs, *extra)


HBM = pl.BlockSpec(memory_space=pltpu.HBM)
SEM = pl.BlockSpec(memory_space=pltpu.SEMAPHORE)
TOKEN = pl.BlockSpec(memory_space=pltpu.VMEM)
IN_FLIGHT = pltpu.CompilerParams(has_side_effects=pltpu.SideEffectType.DATAFLOW_SIDE_EFFECTING)


def _in_hbm(arrs):
    return [pltpu.with_memory_space_constraint(a, pltpu.HBM) for a in arrs]


def _shard_rows(ref, h, px, py, pc):
    return ref.at[pl.ds((4 * px + 2 * py + pc) * h, h)]


def _gather_start(groups, after, name):
    bufs = [b for grp in groups for b in grp]
    n_t, n_g = len(bufs), len(groups)

    def body(*refs):
        sems, thru = refs[n_t + 1:n_t + 1 + 2 * n_g], refs[n_t + 1 + 2 * n_g:2 * n_t + 1 + 2 * n_g]
        token = refs[-1]
        x, y, c = _pos()
        t = 0
        for g, grp in enumerate(groups):
            for i in range(len(grp)):
                h = bufs[t].shape[0] // 8
                rows = _shard_rows(thru[t], h, x, y, c)
                for j, chip in enumerate(_other_chips(x, y)):
                    pltpu.make_async_remote_copy(
                        src_ref=rows, dst_ref=rows, send_sem=sems[2 * g].at[3 * i + j],
                        recv_sem=sems[2 * g + 1].at[3 * i + j], device_id=(*chip, c),
                        device_id_type=MESH).start()
                t += 1
        token[...] = jnp.zeros_like(token)

    sem_shapes = []
    for grp in groups:
        sem_shapes += [pltpu.SemaphoreType.DMA((3 * len(grp),))] * 2
    out = pl.pallas_call(
        body, name=name,
        out_shape=sem_shapes + [pltpu.HBM(b.shape, b.dtype) for b in bufs] + [jax.ShapeDtypeStruct((8, LANES), F32)],
        in_specs=[HBM] * n_t + [ANY], out_specs=[SEM] * (2 * n_g) + [HBM] * n_t + [TOKEN],
        input_output_aliases={t: 2 * n_g + t for t in range(n_t)},
        compiler_params=IN_FLIGHT,
    )(*_in_hbm(bufs), after)
    sems, thru, token = out[:2 * n_g], out[2 * n_g:2 * n_g + n_t], out[-1]
    per_group, t = [], 0
    for g, grp in enumerate(groups):
        per_group.append((sems[2 * g], sems[2 * g + 1], thru[t:t + len(grp)]))
        t += len(grp)
    return per_group, token


def _gather_forward(send_sems, recv_sems, bufs, after, name, sources=(0, 1, 2)):
    n_t = len(bufs)

    def body(*refs):
        ins = refs[:n_t]
        send1, recv1 = refs[n_t], refs[n_t + 1]
        send2, recv2 = refs[n_t + 3], refs[n_t + 4]
        token = refs[-1]
        x, y, c = _pos()
        chips = _other_chips(x, y)
        for t in range(n_t):
            h = bufs[t].shape[0] // 8
            mine = _shard_rows(ins[t], h, x, y, c)
            for j in sources:
                chip = chips[j]
                landed = _shard_rows(ins[t], h, *chip, c)
                pltpu.make_async_remote_copy(
                    src_ref=mine, dst_ref=landed, send_sem=send1.at[3 * t + j], recv_sem=recv1.at[3 * t + j],
                    device_id=(*chip, c), device_id_type=MESH).wait_recv()
                pltpu.make_async_remote_copy(
                    src_ref=landed, dst_ref=landed, send_sem=send2.at[3 * t + j], recv_sem=recv2.at[3 * t + j],
                    device_id=(x, y, 1 - c), device_id_type=MESH).start()
        for t in range(n_t):
            h = bufs[t].shape[0] // 8
            mine = _shard_rows(ins[t], h, x, y, c)
            for j in sources:
                pltpu.make_async_remote_copy(
                    src_ref=mine, dst_ref=mine, send_sem=send1.at[3 * t + j], recv_sem=recv1.at[3 * t + j],
                    device_id=(*chips[j], c), device_id_type=MESH).wait_send()
        token[...] = jnp.zeros_like(token)

    out = pl.pallas_call(
        body, name=name,
        out_shape=[pltpu.SemaphoreType.DMA((3 * n_t,))] * 2 + [pltpu.HBM(b.shape, b.dtype) for b in bufs]
        + [jax.ShapeDtypeStruct((8, LANES), F32)],
        in_specs=[HBM] * n_t + [SEM, SEM, ANY], out_specs=[SEM, SEM] + [HBM] * n_t + [TOKEN],
        input_output_aliases={t: 2 + t for t in range(n_t)},
        compiler_params=IN_FLIGHT,
    )(*bufs, send_sems, recv_sems, after)
    return out[0], out[1], out[2:2 + n_t], out[-1]


def _gather_finish(send_sems, recv_sems, bufs, after, name, sources=(0, 1, 2)):
    n_t = len(bufs)

    def body(*refs):
        ins = refs[:n_t]
        send2, recv2 = refs[n_t], refs[n_t + 1]
        x, y, c = _pos()
        chips = _other_chips(x, y)
        for t in range(n_t):
            h = bufs[t].shape[0] // 8
            for j in sources:
                chip = chips[j]
                sent = _shard_rows(ins[t], h, *chip, c)
                got = _shard_rows(ins[t], h, *chip, 1 - c)
                cp = pltpu.make_async_remote_copy(
                    src_ref=sent, dst_ref=got, send_sem=send2.at[3 * t + j], recv_sem=recv2.at[3 * t + j],
                    device_id=(x, y, 1 - c), device_id_type=MESH)
                cp.wait_send()
                cp.wait_recv()

    return pl.pallas_call(
        body, name=name, out_shape=[pltpu.HBM(b.shape, b.dtype) for b in bufs],
        in_specs=[HBM] * n_t + [SEM, SEM, ANY], out_specs=[HBM] * n_t,
        input_output_aliases={t: t for t in range(n_t)},
        compiler_params=IN_FLIGHT,
    )(*bufs, send_sems, recv_sems, after)


def _part_rows(ref, h, px, py, pc, which):
    return ref.at[pl.ds((4 * px + 2 * py + pc) * h + which * (h // 2), h // 2)]


def _remote(src, dst, send_sem, recv_sem, device):
    return pltpu.make_async_remote_copy(src_ref=src, dst_ref=dst, send_sem=send_sem, recv_sem=recv_sem,
                                        device_id=device, device_id_type=MESH)


def _first_start(buf, after, name):
    h = buf.shape[0] // 8

    def body(buf_in, after_ref, send, recv, thru, token):
        x, y, c = _pos()
        rows = _shard_rows(thru, h, x, y, c)
        for j, chip in enumerate(_other_chips(x, y)[:2]):
            _remote(rows, rows, send.at[j], recv.at[j], (*chip, c)).start()
        token[...] = jnp.zeros_like(token)

    return pl.pallas_call(
        body, name=name,
        out_shape=[pltpu.SemaphoreType.DMA((2,))] * 2 + [pltpu.HBM(buf.shape, buf.dtype),
                                                         jax.ShapeDtypeStruct((8, LANES), F32)],
        in_specs=[HBM, ANY], out_specs=[SEM, SEM, HBM, TOKEN], input_output_aliases={0: 2},
        compiler_params=IN_FLIGHT,
    )(*_in_hbm([buf]), after)


def _first_relay(send_a, recv_a, buf, j, after, name):
    h = buf.shape[0] // 8

    def body(buf_in, send_a, recv_a, *rest):
        send_b, recv_b, send_c, recv_c, _, token = rest[len(after):]
        x, y, c = _pos()
        chips = _other_chips(x, y)
        nbr, other = chips[j], chips[1 - j]
        mine, landed = _shard_rows(buf_in, h, x, y, c), _shard_rows(buf_in, h, *nbr, c)
        _remote(mine, landed, send_a.at[j], recv_a.at[j], (*nbr, c)).wait_recv()
        _remote(landed, landed, send_b.at[0], recv_b.at[0], (x, y, 1 - c)).start()
        part = _part_rows(buf_in, h, *nbr, c, j)
        _remote(part, part, send_c.at[0], recv_c.at[0], (*other, c)).start()
        _remote(mine, mine, send_a.at[j], recv_a.at[j], (*nbr, c)).wait_send()
        token[...] = jnp.zeros_like(token)

    out = pl.pallas_call(
        body, name=name,
        out_shape=[pltpu.SemaphoreType.DMA((1,))] * 4 + [pltpu.HBM(buf.shape, buf.dtype),
                                                         jax.ShapeDtypeStruct((8, LANES), F32)],
        in_specs=[HBM, SEM, SEM] + [ANY] * len(after), out_specs=[SEM] * 4 + [HBM, TOKEN],
        input_output_aliases={0: 4}, compiler_params=IN_FLIGHT,
    )(buf, send_a, recv_a, *after)
    return tuple(out)


def _first_relay_done(send_b, recv_b, buf, j, after, name):
    h = buf.shape[0] // 8

    def body(buf_in, send_b, recv_b, after_ref, thru):
        x, y, c = _pos()
        nbr = _other_chips(x, y)[j]
        cp = _remote(_shard_rows(buf_in, h, *nbr, c), _shard_rows(buf_in, h, *nbr, 1 - c),
                     send_b.at[0], recv_b.at[0], (x, y, 1 - c))
        cp.wait_send()
        cp.wait_recv()

    return pl.pallas_call(
        body, name=name, out_shape=pltpu.HBM(buf.shape, buf.dtype),
        in_specs=[HBM, SEM, SEM, ANY], out_specs=HBM, input_output_aliases={0: 0},
        compiler_params=IN_FLIGHT,
    )(buf, send_b, recv_b, after)


def _first_diagonal(relays, buf, after, name):
    h = buf.shape[0] // 8

    def body(buf_in, send_c0, recv_c0, send_c1, recv_c1, after_ref, send_b, recv_b, thru, token):
        x, y, c = _pos()
        chips = _other_chips(x, y)
        diag = chips[2]
        for j, (send_c, recv_c) in enumerate(((send_c0, recv_c0), (send_c1, recv_c1))):
            part = _part_rows(buf_in, h, *diag, c, j)
            _remote(part, part, send_c.at[0], recv_c.at[0], (*chips[1 - j], c)).wait_recv()
        whole = _shard_rows(buf_in, h, *diag, c)
        _remote(whole, whole, send_b.at[0], recv_b.at[0], (x, y, 1 - c)).start()
        token[...] = jnp.zeros_like(token)

    out = pl.pallas_call(
        body, name=name,
        out_shape=[pltpu.SemaphoreType.DMA((1,))] * 2 + [pltpu.HBM(buf.shape, buf.dtype),
                                                         jax.ShapeDtypeStruct((8, LANES), F32)],
        in_specs=[HBM] + [SEM] * 4 + [ANY], out_specs=[SEM, SEM, HBM, TOKEN], input_output_aliases={0: 2},
        compiler_params=IN_FLIGHT,
    )(buf, relays[0][0], relays[0][1], relays[1][0], relays[1][1], after)
    return tuple(out)


def _first_diagonal_done(send_b, recv_b, relays, buf, after, name):
    h = buf.shape[0] // 8

    def body(buf_in, send_b, recv_b, send_c0, recv_c0, send_c1, recv_c1, after_ref, thru):
        x, y, c = _pos()
        chips = _other_chips(x, y)
        diag = chips[2]
        cp = _remote(_shard_rows(buf_in, h, *diag, c), _shard_rows(buf_in, h, *diag, 1 - c),
                     send_b.at[0], recv_b.at[0], (x, y, 1 - c))
        cp.wait_send()
        cp.wait_recv()
        for j, (send_c, recv_c) in enumerate(((send_c0, recv_c0), (send_c1, recv_c1))):
            part = _part_rows(buf_in, h, *chips[j], c, j)
            _remote(part, part, send_c.at[0], recv_c.at[0], (*chips[1 - j], c)).wait_send()

    return pl.pallas_call(
        body, name=name, out_shape=pltpu.HBM(buf.shape, buf.dtype),
        in_specs=[HBM] + [SEM] * 6 + [ANY], out_specs=HBM, input_output_aliases={0: 0},
        compiler_params=IN_FLIGHT,
    )(buf, send_b, recv_b, relays[0][0], relays[0][1], relays[1][0], relays[1][1], after)


def _exchange_start(name, srcs, lands, n_copies, plan, after):
    ns, nl = len(srcs), len(lands)
    extra = [] if after is None else [after]

    def body(*refs):
        base = ns + nl + len(extra)
        send_sems, recv_sems = refs[base], refs[base + 1]
        src_refs, land_refs = refs[base + 2:base + 2 + ns], refs[base + 2 + ns:base + 2 + ns + nl]
        token = refs[-1]
        x, y, c = _pos()
        copies = plan(src_refs, land_refs, x, y, c)
        assert len(copies) == n_copies
        for k, (src, dst, dev) in enumerate(copies):
            pltpu.make_async_remote_copy(
                src_ref=src, dst_ref=dst, send_sem=send_sems.at[k], recv_sem=recv_sems.at[k],
                device_id=dev, device_id_type=MESH).start()
        token[...] = jnp.zeros_like(token)

    out = pl.pallas_call(
        body, name=name,
        out_shape=[pltpu.SemaphoreType.DMA((n_copies,))] * 2
        + [pltpu.HBM(a.shape, a.dtype) for a in list(srcs) + list(lands)] + [jax.ShapeDtypeStruct((8, LANES), F32)],
        in_specs=[HBM] * (ns + nl) + [ANY] * len(extra), out_specs=[SEM, SEM] + [HBM] * (ns + nl) + [TOKEN],
        input_output_aliases={i: 2 + i for i in range(ns + nl)},
        compiler_params=IN_FLIGHT,
    )(*_in_hbm(list(srcs) + list(lands)), *extra)
    return out[0], out[1], out[2:2 + ns], out[2 + ns:2 + ns + nl], out[-1]


def _exchange_wait(name, send_sems, recv_sems, srcs, lands, plan, after):
    ns, nl = len(srcs), len(lands)

    def body(*refs):
        src_refs, land_refs = refs[:ns], refs[ns:ns + nl]
        send, recv = refs[ns + nl], refs[ns + nl + 1]
        x, y, c = _pos()
        for k, (src, dst, dev) in enumerate(plan(src_refs, land_refs, x, y, c)):
            cp = pltpu.make_async_remote_copy(
                src_ref=src, dst_ref=dst, send_sem=send.at[k], recv_sem=recv.at[k],
                device_id=dev, device_id_type=MESH)
            cp.wait_send()
            cp.wait_recv()

    out = pl.pallas_call(
        body, name=name, out_shape=[pltpu.HBM(a.shape, a.dtype) for a in list(srcs) + list(lands)],
        in_specs=[HBM] * (ns + nl) + [SEM, SEM, ANY], out_specs=[HBM] * (ns + nl),
        input_output_aliases={i: i for i in range(ns + nl)},
        compiler_params=IN_FLIGHT,
    )(*srcs, *lands, send_sems, recv_sems, after)
    return out[:ns], out[ns:]


def _plan_other_half_to_sibling(src_refs, land_refs, x, y, c):
    out = []
    for g_ref, r_ref in zip(src_refs, land_refs):
        h = g_ref.shape[1] // 2
        out.append((g_ref.at[:, pl.ds((1 - c) * h, h), :], r_ref, (x, y, 1 - c)))
    return out


def _plan_partials_to_chips(src_refs, land_refs, x, y, c):
    out = []
    for p_ref, r_ref in zip(src_refs, land_refs):
        for j, (px, py) in enumerate(_other_chips(x, y)):
            out.append((p_ref.at[2 * px + py], r_ref.at[j], (px, py, c)))
    return out


def _plan_share_half(src_refs, land_refs, x, y, c):
    out = []
    for f_ref in land_refs:
        h = f_ref.shape[0] // 2
        rows = f_ref.at[pl.ds(c * h, h)]
        out.append((rows, rows, (x, y, 1 - c)))
    return out


def _cast_into_slot(chip, w, name):
    r, c = w.shape
    br, bc = _blk(r, 512, 8), _blk(c, 2048)
    nb = r // br

    def body(chip_ref, w_ref, o_ref):
        o_ref[...] = w_ref[...].astype(BF16)

    grid_spec = pltpu.PrefetchScalarGridSpec(
        num_scalar_prefetch=1, grid=(nb, c // bc),
        in_specs=[pl.BlockSpec((br, bc), lambda i, j, chip_ref: (i, j))],
        out_specs=pl.BlockSpec((br, bc), lambda i, j, chip_ref: (chip_ref[0] * nb + i, j)))
    return pl.pallas_call(
        body, name=name, grid_spec=grid_spec,
        out_shape=jax.ShapeDtypeStruct((4 * r, c), BF16),
        compiler_params=_params(("parallel", "parallel")),
    )(chip, w)


def _sum_own_and_sibling(core, g, r1, name):
    _, r, c = g.shape
    h = r // 2
    br, bc = _blk(h, 256, 8), _blk(c, 2048)
    nb = h // br

    def body(core_ref, g_ref, r_ref, o_ref):
        o_ref[...] = (g_ref[...] + r_ref[...]).astype(BF16)

    grid_spec = pltpu.PrefetchScalarGridSpec(
        num_scalar_prefetch=1, grid=(4, nb, c // bc),
        in_specs=[pl.BlockSpec((None, br, bc), lambda k, i, j, core_ref: (k, core_ref[0] * nb + i, j)),
                  pl.BlockSpec((None, br, bc), lambda k, i, j, core_ref: (k, i, j))],
        out_specs=pl.BlockSpec((None, br, bc), lambda k, i, j, core_ref: (k, i, j)))
    return pl.pallas_call(
        body, name=name, grid_spec=grid_spec,
        out_shape=jax.ShapeDtypeStruct((4, h, c), BF16),
        compiler_params=_params(("parallel", "parallel", "parallel")),
    )(core, g, r1)


def _sum_chips(where, p, r2, name, piece=(0, 1), so_far=None):
    _, h, c = p.shape
    k, n = piece
    br, bc = _blk(h, 256, 8), _blk(c, 2048)
    nb, ncb = h // br, c // bc

    def body(where_ref, p_ref, r_ref, *rest):
        acc = p_ref[...].astype(F32)
        for j in range(3):
            acc = acc + r_ref[j].astype(F32)
        rest[-1][...] = acc

    extra_specs, extra, aliases = ([], [], {}) if so_far is None else ([ANY], [so_far], {3: 0})
    grid_spec = pltpu.PrefetchScalarGridSpec(
        num_scalar_prefetch=1, grid=(nb, ncb),
        in_specs=[pl.BlockSpec((None, br, bc), lambda i, j, where_ref: (where_ref[0], i, j)),
                  pl.BlockSpec((3, br, bc), lambda i, j, where_ref: (0, i, j))] + extra_specs,
        out_specs=pl.BlockSpec((br, bc), lambda i, j, where_ref: (where_ref[1] * nb + i, k * ncb + j)))
    return pl.pallas_call(
        body, name=name, grid_spec=grid_spec,
        out_shape=jax.ShapeDtypeStruct((2 * h, n * c), F32), input_output_aliases=aliases,
        compiler_params=_params(("parallel", "parallel")),
    )(where, p, r2, *extra)


def _adamw_math(w, g, m, v):
    m2 = ADAM_B1 * m + (1.0 - ADAM_B1) * g
    v2 = ADAM_B2 * v + (1.0 - ADAM_B2) * (g * g)
    m_hat = m2 / (1.0 - ADAM_B1 ** ADAM_STEP)
    v_hat = v2 / (1.0 - ADAM_B2 ** ADAM_STEP)
    delta = -ADAM_LR * (m_hat / (jnp.sqrt(v_hat) + ADAM_EPS) + ADAM_WD * w)
    return delta, m2, v2


def _adamw(w, g, m, v, name, after=None):
    r, c = w.shape
    br, bc = _blk(r, 128, 8), _blk(c, 2048)

    def body(w_ref, g_ref, m_ref, v_ref, go_ref, d_ref, m2_ref, v2_ref):
        gv = g_ref[...]
        d, m2, v2 = _adamw_math(w_ref[...], gv, m_ref[...], v_ref[...])
        go_ref[...] = gv
        d_ref[...] = d
        m2_ref[...] = m2
        v2_ref[...] = v2

    spec = pl.BlockSpec((br, bc), lambda i, j: (i, j))
    body, extra_specs, extra = _ordered(body, 4, after)
    return pl.pallas_call(
        body, name=name, grid=(r // br, c // bc),
        in_specs=[spec] * 4 + extra_specs, out_specs=[spec] * 4,
        out_shape=[jax.ShapeDtypeStruct((r, c), F32)] * 4,
        compiler_params=_params(("parallel", "parallel")),
    )(w, g, m, v, *extra)


def _sum_rows8(g, name, after=None):
    n = g.shape[1]

    def body(g_ref, o_ref):
        acc = g_ref[0:1, :]
        for k in range(1, 8):
            acc = acc + g_ref[k:k + 1, :]
        o_ref[...] = acc

    body, extra_specs, extra = _ordered(body, 1, after)
    return pl.pallas_call(
        body, name=name, out_shape=jax.ShapeDtypeStruct((1, n), F32),
        in_specs=[pl.BlockSpec(memory_space=pltpu.VMEM)] + extra_specs,
        out_specs=pl.BlockSpec(memory_space=pltpu.VMEM),
        compiler_params=_params(),
    )(g, *extra)


def _mm(a, b, mode, out_dtype, name, bm=1024, bn=None, after=None, col_blocks=None):
    if mode == "nn":
        (_, m, k), (g, _, n) = a.shape, b.shape
    elif mode == "tn":
        (_, k, m), (g, _, n) = a.shape, b.shape
    else:
        (g, m, k), (_, n, _) = a.shape, b.shape
    if bn is None:
        bn = 1024 if k <= 2048 else 512
    bm, bn = _blk(m, bm), _blk(n, bn)

    if mode == "nt":
        def body(a_ref, b_ref, o_ref, acc_ref):
            part = lax.dot_general(a_ref[...], b_ref[...], (((1,), (1,)), ((), ())),
                                   preferred_element_type=F32)
            if g == 1:
                o_ref[...] = part.astype(out_dtype)
            else:
                gi = pl.program_id(2)

                @pl.when(gi == 0)
                def _():
                    acc_ref[...] = part

                @pl.when(gi > 0)
                def _():
                    acc_ref[...] += part

                @pl.when(gi == g - 1)
                def _():
                    o_ref[...] = acc_ref[...].astype(out_dtype)

        body, extra_specs, extra = _ordered(body, 2, after)
        return pl.pallas_call(
            body, name=name, grid=(m // bm, n // bn, g),
            in_specs=[pl.BlockSpec((None, bm, k), lambda i, j, gi: (gi, i, 0)),
                      pl.BlockSpec((None, bn, k), lambda i, j, gi: (gi, j, 0))] + extra_specs,
            out_specs=pl.BlockSpec((None, bm, bn), lambda i, j, gi: (0, i, j)),
            out_shape=jax.ShapeDtypeStruct((1, m, n), out_dtype),
            scratch_shapes=[pltpu.VMEM((bm, bn), F32)],
            compiler_params=_params(("parallel", "parallel", "arbitrary")),
        )(a, b, *extra)

    contract = (((1,), (0,)), ((), ())) if mode == "nn" else (((0,), (0,)), ((), ()))

    def body(a_ref, b_ref, o_ref):
        o_ref[...] = lax.dot_general(a_ref[...], b_ref[...], contract,
                                     preferred_element_type=F32).astype(out_dtype)

    a_spec = (pl.BlockSpec((None, bm, k), lambda i, gi, j: (0, i, 0)) if mode == "nn"
              else pl.BlockSpec((None, k, bm), lambda i, gi, j: (0, 0, i)))
    first, count = (0, n // bn) if col_blocks is None else col_blocks
    body, extra_specs, extra = _ordered(body, 2, after)
    return pl.pallas_call(
        body, name=name, grid=(m // bm, g, count),
        in_specs=[a_spec, pl.BlockSpec((None, k, bn), lambda i, gi, j: (gi, 0, first + j))] + extra_specs,
        out_specs=pl.BlockSpec((None, bm, bn), lambda i, gi, j: (gi, i, j)),
        out_shape=jax.ShapeDtypeStruct((g, m, count * bn), out_dtype),
        compiler_params=_params(("parallel", "parallel", "parallel")),
    )(a, b, *extra)


def _mm_plane(a, b, plane, name, planes_so_far=None, after=None, bm=1024, bn=1024):
    (m, k), (g, _, n) = a.shape, b.shape
    bm, bn = _blk(m, bm), _blk(n, bn)

    def body(plane_ref, a_ref, b_ref, *rest):
        rest[-1][...] = jnp.dot(a_ref[...], b_ref[...], preferred_element_type=F32).astype(BF16)

    extra_specs, extra, aliases = [], [], {}
    if planes_so_far is not None:
        extra_specs.append(ANY)
        extra.append(planes_so_far)
        aliases = {3: 0}
    if after is not None:
        extra_specs.append(ANY)
        extra.append(after)
    grid_spec = pltpu.PrefetchScalarGridSpec(
        num_scalar_prefetch=1, grid=(m // bm, n // bn),
        in_specs=[pl.BlockSpec((bm, k), lambda i, j, p: (i, 0)),
                  pl.BlockSpec((None, k, bn), lambda i, j, p: (p[0], 0, j))] + extra_specs,
        out_specs=pl.BlockSpec((None, bm, bn), lambda i, j, p: (p[0], i, j)))
    return pl.pallas_call(
        body, name=name, grid_spec=grid_spec,
        out_shape=jax.ShapeDtypeStruct((g, m, n), BF16), input_output_aliases=aliases,
        compiler_params=_params(("parallel", "parallel")),
    )(plane, a, b, *extra)


def _row_specs(br, d):
    return (pl.BlockSpec((br, d), lambda i: (i, 0)), pl.BlockSpec((1, d), lambda i: (0, 0)),
            pl.BlockSpec((8, d), lambda i: (0, 0)))


def _rstd(xv):
    return lax.rsqrt(jnp.mean(xv * xv, axis=-1, keepdims=True) + RMS_EPS)


def _colsum(v):
    return jnp.sum(v, axis=0, keepdims=True)


def _norm_mod_fwd(x, g, scale, shift, name, o=None, gate=None, after=None):
    s, d = x.shape
    br = _blk(s, 256, 8)
    has_res = o is not None
    row, vec, _ = _row_specs(br, d)

    def body(*refs):
        if has_res:
            x_ref, o_ref, gate_ref, g_ref, sc_ref, sh_ref, x1_ref, h_ref = refs
            xv = x_ref[...] + gate_ref[...] * o_ref[...]
            x1_ref[...] = xv
        else:
            x_ref, g_ref, sc_ref, sh_ref, h_ref = refs
            xv = x_ref[...]
        n = xv * _rstd(xv) * g_ref[...]
        h_ref[...] = (n * (1.0 + sc_ref[...]) + sh_ref[...]).astype(BF16)

    ins = [x] + ([o, gate] if has_res else []) + [g, scale, shift]
    in_specs = [row] + ([row, vec] if has_res else []) + [vec] * 3
    out_shape = ([jax.ShapeDtypeStruct((s, d), F32)] if has_res else []) + [jax.ShapeDtypeStruct((s, d), BF16)]
    body, extra_specs, extra = _ordered(body, len(ins), after)
    out = pl.pallas_call(
        body, name=name, grid=(s // br,), in_specs=in_specs + extra_specs, out_specs=[row] * len(out_shape),
        out_shape=out_shape, compiler_params=_params(("parallel",)),
    )(*ins, *extra)
    return out if has_res else out[0]


def _final_loss(x1, o1, gate1, final_g, tgt, name):
    s, d = x1.shape
    br = _blk(s, 256, 8)
    row, vec, acc = _row_specs(br, d)

    def body(x1_ref, o_ref, gate_ref, g_ref, t_ref, dx_ref, do_ref, acc_ref):
        @pl.when(pl.program_id(0) == 0)
        def _():
            acc_ref[...] = jnp.zeros_like(acc_ref)

        gate, o, g = gate_ref[...], o_ref[...], g_ref[...]
        x2 = x1_ref[...] + gate * o
        r = _rstd(x2)
        xh = x2 * r
        err = xh * g - t_ref[...]
        loss = 0.5 * _colsum(jnp.mean(err * err, axis=-1, keepdims=True))
        dout = err * (1.0 / d)
        dxh = dout * g
        dx2 = r * (dxh - xh * jnp.mean(dxh * xh, axis=-1, keepdims=True))
        dx_ref[...] = dx2
        do_ref[...] = (dx2 * gate).astype(BF16)
        acc_ref[0:1, :] += _colsum(dout * xh)
        acc_ref[1:2, :] += _colsum(dx2 * o)
        acc_ref[2:3, :] += jnp.broadcast_to(loss, (1, d))

    return pl.pallas_call(
        body, name=name, grid=(s // br,),
        in_specs=[row, row, vec, vec, row], out_specs=[row, row, acc],
        out_shape=[jax.ShapeDtypeStruct((s, d), F32), jax.ShapeDtypeStruct((s, d), BF16),
                   jax.ShapeDtypeStruct((8, d), F32)],
        compiler_params=_params(("arbitrary",)),
    )(x1, o1, gate1, final_g, tgt)


def _norm_mod_bwd(dh, x, g, scale, dx_next, name, o_prev=None, gate_prev=None, after=None):
    s, d = x.shape
    br = _blk(s, 256, 8)
    has_prev = o_prev is not None
    row, vec, acc = _row_specs(br, d)

    def body(*refs):
        if has_prev:
            dh_ref, x_ref, g_ref, sc_ref, dxn_ref, o_ref, gate_ref, dx_ref, do_ref, acc_ref = refs
        else:
            dh_ref, x_ref, g_ref, sc_ref, dxn_ref, dx_ref, acc_ref = refs

        @pl.when(pl.program_id(0) == 0)
        def _():
            acc_ref[...] = jnp.zeros_like(acc_ref)

        xv, gv, dhv = x_ref[...], g_ref[...], dh_ref[...]
        r = _rstd(xv)
        xh = xv * r
        acc_ref[0:1, :] += _colsum(dhv * (xh * gv))
        acc_ref[1:2, :] += _colsum(dhv)
        dn = dhv * (1.0 + sc_ref[...])
        acc_ref[2:3, :] += _colsum(dn * xh)
        dxh = dn * gv
        dx = dxn_ref[...] + r * (dxh - xh * jnp.mean(dxh * xh, axis=-1, keepdims=True))
        dx_ref[...] = dx
        if has_prev:
            acc_ref[3:4, :] += _colsum(dx * o_ref[...])
            do_ref[...] = (dx * gate_ref[...]).astype(BF16)

    ins = [dh, x, g, scale, dx_next] + ([o_prev, gate_prev] if has_prev else [])
    in_specs = [row, row, vec, vec, row] + ([row, vec] if has_prev else [])
    out_shape = [jax.ShapeDtypeStruct((s, d), F32)]
    out_specs = [row]
    if has_prev:
        out_shape.append(jax.ShapeDtypeStruct((s, d), BF16))
        out_specs.append(row)
    out_shape.append(jax.ShapeDtypeStruct((8, d), F32))
    out_specs.append(acc)
    body, extra_specs, extra = _ordered(body, len(ins), after)
    return pl.pallas_call(
        body, name=name, grid=(s // br,), in_specs=in_specs + extra_specs, out_specs=out_specs,
        out_shape=out_shape, compiler_params=_params(("arbitrary",)),
    )(*ins, *extra)


def _tiles(p):
    s, c = p.shape
    return p.reshape(s // SUBLANES, SUBLANES, c)


def _shift_down(p, k):
    if k == 0:
        return p
    r = pltpu.roll(_tiles(p), k, 1)
    before = jnp.concatenate([jnp.zeros_like(r[:1]), r[:-1]], axis=0)
    rows = lax.broadcasted_iota(jnp.int32, r.shape, 1)
    return jnp.where(rows >= k, r, before).reshape(p.shape)


def _shift_up(p, k):
    if k == 0:
        return p
    r = pltpu.roll(_tiles(p), SUBLANES - k, 1)
    after = jnp.concatenate([r[1:], jnp.zeros_like(r[:1])], axis=0)
    rows = lax.broadcasted_iota(jnp.int32, r.shape, 1)
    return jnp.where(rows < SUBLANES - k, r, after).reshape(p.shape)


def _sigmoid(z):
    return 0.5 * (jnp.tanh(0.5 * z) + 1.0)


def _sc_parts(proj_ref, w_ref):
    b, cg, v, g = (proj_ref[i].astype(F32) for i in range(4))
    p = cg * v
    u = w_ref[2:3, :] * p + w_ref[1:2, :] * _shift_down(p, 1) + w_ref[0:1, :] * _shift_down(p, 2)
    return b, cg, v, g, p, u


def _sc_fwd(proj, conv_w, name):
    _, s, e = proj.shape
    bc = _blk(e, 256)

    def body(proj_ref, w_ref, y_ref):
        b, _, _, g, _, u = _sc_parts(proj_ref, w_ref)
        y_ref[...] = (b * u * (g * _sigmoid(g))).astype(BF16)

    return pl.pallas_call(
        body, name=name, grid=(e // bc,),
        in_specs=[pl.BlockSpec((4, s, bc), lambda j: (0, 0, j)), pl.BlockSpec((3, bc), lambda j: (0, j))],
        out_specs=pl.BlockSpec((s, bc), lambda j: (0, j)),
        out_shape=jax.ShapeDtypeStruct((s, e), BF16),
        compiler_params=_params(("parallel",)),
    )(proj, conv_w)


def _sc_bwd(proj, dy, conv_w, name, after=None):
    _, s, e = proj.shape
    bc = _blk(e, 256)

    def body(proj_ref, dy_ref, w_ref, dp_ref, dw_ref):
        b, cg, v, g, p, u = _sc_parts(proj_ref, w_ref)
        dyv = dy_ref[...].astype(F32)
        sig = _sigmoid(g)
        t = dyv * (g * sig)
        du = t * b
        dp_ref[0] = (t * u).astype(BF16)
        dp_ref[3] = (dyv * b * u * (sig * (1.0 + g * (1.0 - sig)))).astype(BF16)
        dpp = w_ref[2:3, :] * du + w_ref[1:2, :] * _shift_up(du, 1) + w_ref[0:1, :] * _shift_up(du, 2)
        dp_ref[1] = (dpp * v).astype(BF16)
        dp_ref[2] = (dpp * cg).astype(BF16)
        dw_ref[2:3, :] = _colsum(du * p)
        dw_ref[1:2, :] = _colsum(du * _shift_down(p, 1))
        dw_ref[0:1, :] = _colsum(du * _shift_down(p, 2))

    body, extra_specs, extra = _ordered(body, 3, after)
    return pl.pallas_call(
        body, name=name, grid=(e // bc,),
        in_specs=[pl.BlockSpec((4, s, bc), lambda j: (0, 0, j)), pl.BlockSpec((s, bc), lambda j: (0, j)),
                  pl.BlockSpec((3, bc), lambda j: (0, j))] + extra_specs,
        out_specs=[pl.BlockSpec((4, s, bc), lambda j: (0, 0, j)), pl.BlockSpec((3, bc), lambda j: (0, j))],
        out_shape=[jax.ShapeDtypeStruct((4, s, e), BF16), jax.ShapeDtypeStruct((3, e), F32)],
        compiler_params=_params(("parallel",)),
    )(proj, dy, conv_w, *extra)


def _softplus_neg(lam):
    u = jnp.exp(-jnp.abs(lam))
    w = 1.0 + u
    log1p = jnp.where(w == 1.0, u, jnp.log(w) * (u / jnp.where(w == 1.0, 1.0, w - 1.0)))
    return jnp.maximum(-lam, 0.0) + log1p


def _one_minus_exp(z):
    series = -z * (1.0 + z * (0.5 + z * (1.0 / 6.0 + z * (1.0 / 24.0))))
    return jnp.where(z > -0.02, series, 1.0 - jnp.exp(z))


def _scan_in_tiles(a, b, reverse):
    shape = a.shape
    a, b = _tiles(a), _tiles(b)
    rows = lax.broadcasted_iota(jnp.int32, a.shape, 1)
    for step in (1, 2, 4):
        shift = SUBLANES - step if reverse else step
        ok = rows < SUBLANES - step if reverse else rows >= step
        a_s, b_s = pltpu.roll(a, shift, 1), pltpu.roll(b, shift, 1)
        b = jnp.where(ok, a * b_s + b, b)
        a = jnp.where(ok, a * a_s, a)
    return a.reshape(shape), b.reshape(shape)


def _by_rows(fn, arrays, rows=32):
    s = arrays[0].shape[0]
    rows = min(rows, s)
    for t in range(0, s, rows):
        fn(t, *(a[t:t + rows] for a in arrays))


def _scan_carry(a_ref, b_ref, h_ref, reverse):
    s, c = a_ref.shape
    n = s // 8

    def step(i, carry):
        gi = n - 1 - i if reverse else i
        sl = pl.ds(pl.multiple_of(gi * 8, 8), 8)
        h = b_ref[sl, :] + a_ref[sl, :] * carry
        h_ref[sl, :] = h
        return h[0:1, :] if reverse else h[7:8, :]

    lax.fori_loop(0, n, step, jnp.zeros((1, c), F32))


def _lru_specs(s, e_half, n_heads):
    hp = e_half // HEAD_DIM
    c = HEAD_DIM
    return dict(
        pair=pl.BlockSpec((2, None, s, c), lambda h: (0, h // hp, 0, h % hp)),
        conv_w=pl.BlockSpec((4, c), lambda h: (0, h)),
        chan=pl.BlockSpec((1, c), lambda h: (0, h)),
        w=pl.BlockSpec((4, None, c // 4, c), lambda h: (0, h, 0, 0)),
        bias=pl.BlockSpec((None, 1, c), lambda h: (h, 0, 0)),
        plane=pl.BlockSpec((s, c), lambda h: (0, h)),
    )


def _lru_gate_inputs(vp, cw_ref, cb_ref, wa_ref, ba_ref, wx_ref, bx_ref):
    c = HEAD_DIM
    taps = [_shift_down(vp, 3 - k) for k in range(4)]
    v = cb_ref[...] + sum(cw_ref[k:k + 1, :] * taps[k] for k in range(4))
    vb = v.astype(BF16)
    wa = wa_ref[...].reshape(c, c)
    wx = wx_ref[...].reshape(c, c)
    zr = jnp.dot(vb, wa, preferred_element_type=F32) + ba_ref[...]
    zi = jnp.dot(vb, wx, preferred_element_type=F32) + bx_ref[...]
    return taps, v, vb, wa, wx, zr, zi


def _lru_fwd(proj, conv_w, conv_b, w_a, b_a, w_x, b_x, lam, name):
    _, _, s, e_half = proj.shape
    n_heads = 2 * e_half // HEAD_DIM
    sp_ = _lru_specs(s, e_half, n_heads)

    def body(pg_ref, cw_ref, cb_ref, wa_ref, ba_ref, wx_ref, bx_ref, lam_ref,
             y_ref, a_ref, hs_ref, sa_ref, sb_ref, sh_ref):
        _, v, _, _, _, zr, zi = _lru_gate_inputs(
            pg_ref[0].astype(F32), cw_ref, cb_ref, wa_ref, ba_ref, wx_ref, bx_ref)
        rate = (-RGLRU_C) * _softplus_neg(lam_ref[...])

        def decay_and_input(t, v_c, zr_c, zi_c):
            la = rate * _sigmoid(zr_c)
            a = jnp.exp(la)
            b = jnp.sqrt(_one_minus_exp(2.0 * la)) * (_sigmoid(zi_c) * v_c)
            a_ref[t:t + a.shape[0]] = a
            sa_ref[t:t + a.shape[0]], sb_ref[t:t + a.shape[0]] = _scan_in_tiles(a, b, reverse=False)

        _by_rows(decay_and_input, [v, zr, zi])
        _scan_carry(sa_ref, sb_ref, sh_ref, reverse=False)

        def gated_output(t, hs_c, g_c):
            g = g_c.astype(F32)
            y_ref[t:t + g.shape[0]] = (hs_c * (g * _sigmoid(g))).astype(BF16)
            hs_ref[t:t + g.shape[0]] = hs_c.astype(BF16)

        _by_rows(gated_output, [sh_ref, pg_ref.at[1]])

    e = 2 * e_half
    return pl.pallas_call(
        body, name=name, grid=(n_heads,),
        in_specs=[sp_["pair"], sp_["conv_w"], sp_["chan"], sp_["w"], sp_["bias"], sp_["w"],
                  sp_["bias"], sp_["chan"]],
        out_specs=[sp_["plane"]] * 3,
        out_shape=[jax.ShapeDtypeStruct((s, e), BF16), jax.ShapeDtypeStruct((s, e), F32),
                   jax.ShapeDtypeStruct((s, e), BF16)],
        scratch_shapes=[pltpu.VMEM((s, HEAD_DIM), F32)] * 3,
        compiler_params=_params(("parallel",)),
    )(proj, conv_w, conv_b, w_a, b_a, w_x, b_x, lam)


def _lru_bwd(proj, dy, saved, conv_w, conv_b, w_a, b_a, w_x, b_x, lam, name):
    _, _, s, e_half = proj.shape
    e = 2 * e_half
    c = HEAD_DIM
    n_heads = e // c
    sp_ = _lru_specs(s, e_half, n_heads)

    def body(pg_ref, dy_ref, a_ref, hs_ref, cw_ref, cb_ref, wa_ref, ba_ref, wx_ref, bx_ref, lam_ref,
             dpg_ref, dwa_ref, dwx_ref, dba_ref, dbx_ref, dlam_ref, dcw_ref, dcb_ref,
             sa_ref, sb_ref, sd_ref, dzr_ref, dzi_ref):
        taps, v, vb, wa, wx, zr, zi = _lru_gate_inputs(
            pg_ref[0].astype(F32), cw_ref, cb_ref, wa_ref, ba_ref, wx_ref, bx_ref)
        lam = lam_ref[...]
        rate = (-RGLRU_C) * _softplus_neg(lam)
        a = a_ref[...]

        def state_gradient_in_tiles(t, a_next, dy_c, g_c, hs_c):
            g, dyv = g_c.astype(F32), dy_c.astype(F32)
            sig = _sigmoid(g)
            rows = slice(t, t + g.shape[0])
            dpg_ref[1, rows] = (dyv * hs_c.astype(F32) * (sig * (1.0 + g * (1.0 - sig)))).astype(BF16)
            sa_ref[rows], sb_ref[rows] = _scan_in_tiles(a_next, dyv * (g * sig), reverse=True)

        _by_rows(state_gradient_in_tiles, [_shift_up(a, 1), dy_ref, pg_ref.at[1], hs_ref])
        _scan_carry(sa_ref, sb_ref, sd_ref, reverse=True)

        sums = []

        def gate_gradients(t, dh, hs_before, a_c, zr_c, zi_c, v_c):
            r, i = _sigmoid(zr_c), _sigmoid(zi_c)
            q = (1.0 - a_c) * (1.0 + a_c)
            inv_nm = lax.rsqrt(q)
            div = dh * (q * inv_nm)
            dla = (dh * hs_before) * a_c - (dh * (i * v_c)) * (a_c * a_c * inv_nm)
            dzr = (dla * rate) * (r * (1.0 - r))
            dzi = (div * v_c) * (i * (1.0 - i))
            rows = slice(t, t + dh.shape[0])
            dzr_ref[rows], dzi_ref[rows] = dzr.astype(BF16), dzi.astype(BF16)
            sa_ref[rows] = div * i
            sums.append((_colsum(dla * r), _colsum(dzr), _colsum(dzi)))

        _by_rows(gate_gradients, [sd_ref, _shift_down(hs_ref[...].astype(F32), 1), a_ref, zr, zi, v])
        dlam_ref[...] = sum(p[0] for p in sums) * ((-RGLRU_C) * (-_sigmoid(-lam)))
        dba_ref[...] = sum(p[1] for p in sums)
        dbx_ref[...] = sum(p[2] for p in sums)
        dzr_b, dzi_b = dzr_ref[...], dzi_ref[...]
        tn = (((0,), (0,)), ((), ()))
        nt = (((1,), (1,)), ((), ()))
        dwa_ref[...] = lax.dot_general(vb, dzr_b, tn, preferred_element_type=F32)
        dwx_ref[...] = lax.dot_general(vb, dzi_b, tn, preferred_element_type=F32)
        dv = (sa_ref[...] + lax.dot_general(dzr_b, wa, nt, preferred_element_type=F32)
              + lax.dot_general(dzi_b, wx, nt, preferred_element_type=F32))
        dcb_ref[...] = _colsum(dv)
        dvp = jnp.zeros_like(dv)
        for k in range(4):
            dvp = dvp + cw_ref[k:k + 1, :] * _shift_up(dv, 3 - k)
            dcw_ref[k:k + 1, :] = _colsum(dv * taps[k])
        dpg_ref[0] = dvp.astype(BF16)

    head_mat = pl.BlockSpec((None, c, c), lambda h: (h, 0, 0))
    outs = pl.pallas_call(
        body, name=name, grid=(n_heads,),
        in_specs=[sp_["pair"]] + [sp_["plane"]] * 3 + [sp_["conv_w"], sp_["chan"], sp_["w"], sp_["bias"],
                                                        sp_["w"], sp_["bias"], sp_["chan"]],
        out_specs=[sp_["pair"], head_mat, head_mat, sp_["bias"], sp_["bias"],
                   sp_["chan"], sp_["conv_w"], sp_["chan"]],
        out_shape=[jax.ShapeDtypeStruct((2, 2, s, e_half), BF16),
                   jax.ShapeDtypeStruct((n_heads, c, c), F32), jax.ShapeDtypeStruct((n_heads, c, c), F32),
                   jax.ShapeDtypeStruct((n_heads, 1, c), F32), jax.ShapeDtypeStruct((n_heads, 1, c), F32),
                   jax.ShapeDtypeStruct((1, e), F32), jax.ShapeDtypeStruct((4, e), F32),
                   jax.ShapeDtypeStruct((1, e), F32)],
        scratch_shapes=[pltpu.VMEM((s, c), F32)] * 3 + [pltpu.VMEM((s, c), BF16)] * 2,
        compiler_params=_params(("parallel",)),
    )(proj, dy, *saved, conv_w, conv_b, w_a, b_a, w_x, b_x, lam)
    return tuple(outs)


def _ada_fwd(c_all, ada_w, name):
    n_l, d, f = ada_w.shape
    bf = _blk(f, 512)

    def body(c_ref, w_ref, o_ref):
        cv = c_ref[...]
        sc = (cv * _sigmoid(cv)).astype(BF16)
        o_ref[...] = jnp.dot(sc, w_ref[...].astype(BF16), preferred_element_type=F32)

    return pl.pallas_call(
        body, name=name, grid=(n_l, f // bf),
        in_specs=[pl.BlockSpec((8, d), lambda l, j: (0, 0)), pl.BlockSpec((None, d, bf), lambda l, j: (l, 0, j))],
        out_specs=pl.BlockSpec((None, 8, bf), lambda l, j: (l, 0, j)),
        out_shape=jax.ShapeDtypeStruct((n_l, 8, f), F32),
        compiler_params=_params(("parallel", "parallel")),
    )(c_all, ada_w)


def _ada_bwd_adamw(c_t, dmod, w, m, v, name, after=None):
    n_l, d, f = w.shape
    bf = _blk(f, 256)

    def body(c_ref, dm_ref, w_ref, m_ref, v_ref, g_ref, d_ref, m2_ref, v2_ref):
        cv = c_ref[...]
        sc = cv * _sigmoid(cv)
        dm = dm_ref[...]
        g = sc[:, 0:1] * dm[0:1, :]
        for b in range(1, 8):
            g = g + sc[:, b:b + 1] * dm[b:b + 1, :]
        g_ref[...] = g
        dl, m2, v2 = _adamw_math(w_ref[...], g, m_ref[...], v_ref[...])
        d_ref[...] = dl
        m2_ref[...] = m2
        v2_ref[...] = v2

    big = pl.BlockSpec((None, d, bf), lambda l, j: (l, 0, j))
    body, extra_specs, extra = _ordered(body, 5, after)
    return pl.pallas_call(
        body, name=name, grid=(n_l, f // bf),
        in_specs=[pl.BlockSpec((d, 8), lambda l, j: (0, 0)), pl.BlockSpec((None, 8, bf), lambda l, j: (l, 0, j)),
                  big, big, big] + extra_specs,
        out_specs=[big] * 4, out_shape=[jax.ShapeDtypeStruct((n_l, d, f), F32)] * 4,
        compiler_params=_params(("parallel", "parallel")),
    )(c_t, dmod, w, m, v, *extra)


def _pack(parts):
    padded, offs, n = [], [], 0
    for p in parts:
        p = p.reshape(-1)
        size = -(-p.shape[0] // PACK) * PACK
        offs.append(n)
        n += size
        padded.append(jnp.pad(p, (0, size - p.shape[0])) if size != p.shape[0] else p)
    return jnp.concatenate(padded), offs, n


def kernel(x, c, norm_g, ada_w, ada_b, sc_w_in, sc_conv_w, sc_w_out, lru_w_in, lru_conv_w, lru_conv_b, lru_w_a, lru_b_a, lru_w_x, lru_b_x, lru_lambda, lru_w_out, final_g, loss_target, m_norm_g, m_ada_w, m_ada_b, m_sc_w_in, m_sc_conv_w, m_sc_w_out, m_lru_w_in, m_lru_conv_w, m_lru_conv_b, m_lru_w_a, m_lru_b_a, m_lru_w_x, m_lru_b_x, m_lru_lambda, m_lru_w_out, m_final_g, v_norm_g, v_ada_w, v_ada_b, v_sc_w_in, v_sc_conv_w, v_sc_w_out, v_lru_w_in, v_lru_conv_w, v_lru_conv_b, v_lru_w_a, v_lru_b_a, v_lru_w_x, v_lru_b_x, v_lru_lambda, v_lru_w_out, v_final_g):
    xi, yi, ci = _pos()
    chip = 2 * xi + yi
    batch = 4 * xi + 2 * yi + ci
    core_op = jnp.reshape(ci, (1,)).astype(jnp.int32)
    chip_op = jnp.reshape(chip, (1,)).astype(jnp.int32)
    where_op = jnp.stack([chip, ci]).astype(jnp.int32)

    x2d, tgt = x[0], loss_target[0]
    s, d = x2d.shape
    es = sc_conv_w.shape[2]
    e = 4 * es
    n_heads = lru_w_a.shape[1]
    hj = lru_b_a.shape[2]
    f = ada_w.shape[2]
    row = lambda t: t.reshape(1, -1)

    small_parts = [c, sc_conv_w, lru_conv_w, lru_conv_b, lru_b_a, lru_b_x, lru_lambda]
    small, offs, n_small = _pack(small_parts)
    got = _allgather8([small.reshape(8, n_small // 8)], "ag_small")[0].reshape(8, n_small)
    c_all = got[:, :d]
    per_chip = got[0::2]

    def chip_part(k, shape):
        size = 1
        for dim in shape:
            size *= dim
        return per_chip[:, offs[k]:offs[k] + size].reshape((4,) + shape)

    conv_w0 = jnp.transpose(chip_part(1, (3, es)), (1, 0, 2)).reshape(3, e)
    conv_w1 = jnp.transpose(chip_part(2, (4, es)), (1, 0, 2)).reshape(4, e)
    conv_b1 = chip_part(3, (es,)).reshape(1, e)
    b_a = jnp.transpose(chip_part(4, (n_heads, hj)), (1, 0, 2)).reshape(n_heads, 1, 4 * hj)
    b_x = jnp.transpose(chip_part(5, (n_heads, hj)), (1, 0, 2)).reshape(n_heads, 1, 4 * hj)
    lam = chip_part(6, (es,)).reshape(1, e)

    mod_nb = _ada_fwd(c_all, ada_w, "ada_fwd")
    mods = _allgather8([mod_nb.reshape(16, f)], "ag_mod")[0].reshape(8, 2, 8, f)[0::2]
    mine = lax.dynamic_index_in_dim(mods, batch, axis=2, keepdims=False)
    mod = jnp.transpose(mine, (1, 0, 2)).reshape(2, 4 * f) + ada_b
    shift = [row(mod[l, :d]) for l in range(2)]
    scale = [row(mod[l, d:2 * d]) for l in range(2)]
    gate = [row(mod[l, 2 * d:]) for l in range(2)]
    ng = [row(norm_g[l]) for l in range(2)]

    shards = [sc_w_in[0], sc_w_out[0], lru_w_in[0], lru_w_a[0].reshape(n_heads * hj, HEAD_DIM),
              lru_w_x[0].reshape(n_heads * hj, HEAD_DIM), lru_w_out[0]]
    names = ["sc_w_in", "sc_w_out", "lru_w_in", "lru_w_a", "lru_w_x", "lru_w_out"]
    slots = [_cast_into_slot(chip_op, w, "cast_" + nm) for w, nm in zip(shards, names)]
    send_a, recv_a, buf, started = _first_start(slots[0], mod, "ag_first_start")
    h0 = _norm_mod_fwd(x2d, ng[0], scale[0], shift[0], "norm0", after=started)
    planes = [jnp.reshape(2 * px + py, (1,)).astype(jnp.int32) for px, py in _other_chips(xi, yi)]
    proj0 = _mm_plane(h0, buf.reshape(4, d, e), chip_op, "sc_in_own")
    relays, passed, relayed = [], [], [proj0] + slots[1:]
    for j in range(2):
        send_b, recv_b, send_c, recv_c, buf, token = _first_relay(send_a, recv_a, buf, j, relayed,
                                                                  "ag_first_relay_%d" % j)
        relays.append((send_c, recv_c))
        passed.append((send_b, recv_b))
        relayed = [token]
    rest_flight, relayed = _gather_start([[slots[1]], slots[2:5], [slots[5]]], token, "ag_start")
    for j in range(2):
        buf = _first_relay_done(*passed[j], buf, j, relayed, "ag_first_relay_done_%d" % j)
        proj0 = _mm_plane(h0, buf.reshape(4, d, e), planes[j], "sc_in_%d" % j, planes_so_far=proj0)
        relayed = proj0
    send_b, recv_b, buf, relayed = _first_diagonal(relays, buf, proj0, "ag_first_diagonal")
    buf = _first_diagonal_done(send_b, recv_b, relays, buf, relayed, "ag_first_diagonal_done")
    proj0 = _mm_plane(h0, buf.reshape(4, d, e), planes[2], "sc_in_2", planes_so_far=proj0)
    w_in0 = buf.reshape(4, d, e)
    in_flight = [None] + rest_flight

    def arrived(g, after, tag):
        send1, recv1, bufs = in_flight[g]
        send2, recv2, bufs, passed = _gather_forward(send1, recv1, bufs, after, "ag_forward_" + tag)
        return _gather_finish(send2, recv2, bufs, passed, "ag_finish_" + tag)

    w_out0 = arrived(1, proj0, "sc_w_out")[0].reshape(1, e, d)
    y0 = _sc_fwd(proj0, conv_w0, "sc_mix")
    o0 = _mm(y0[None], w_out0, "nn", F32, "sc_out")[0]
    x1, h1 = _norm_mod_fwd(x2d, ng[1], scale[1], shift[1], "norm1", o=o0, gate=gate[0])
    lru_ws = arrived(2, h1, "lru_w_in")
    w_in1 = lru_ws[0].reshape(4, d, e // 2)
    w_a = lru_ws[1].reshape(4, n_heads, hj, HEAD_DIM)
    w_x = lru_ws[2].reshape(4, n_heads, hj, HEAD_DIM)
    proj1 = _mm(h1[None], w_in1, "nn", BF16, "lru_in")
    pairs1 = proj1.reshape(2, 2, s, e // 2)
    y1, *lru_saved = _lru_fwd(pairs1, conv_w1, conv_b1, w_a, b_a, w_x, b_x, lam, "lru_mix")
    w_out1 = arrived(3, y1, "lru_w_out")[0].reshape(1, e, d)
    o1 = _mm(y1[None], w_out1, "nn", F32, "lru_out")[0]
    dx2, do1, acc_f = _final_loss(x1, o1, gate[1], row(final_g), tgt, "final_loss")

    def reduce_stage1(tag, grads):
        lands = [lax.empty((4, g.shape[1] // 2, g.shape[2]), F32) for g in grads]
        return _exchange_start("rs_sibling_start_" + tag, grads, lands, len(grads),
                               _plan_other_half_to_sibling, None)

    def reduce_stage2(tag, stage1, nms, after):
        send, recv, grads, lands, _ = stage1
        grads, lands = _exchange_wait("rs_sibling_wait_" + tag, send, recv, grads, lands,
                                      _plan_other_half_to_sibling, after)
        parts = [_sum_own_and_sibling(core_op, g, r1, "rs_sum1_" + nm) for g, r1, nm in zip(grads, lands, nms)]
        lands = [lax.empty((3,) + p.shape[1:], BF16) for p in parts]
        return _exchange_start("rs_chips_start_" + tag, parts, lands, 3 * len(parts),
                               _plan_partials_to_chips, None)

    def reduce_stage3(tag, stage2, nms, after):
        send, recv, parts, lands, _ = stage2
        parts, lands = _exchange_wait("rs_chips_wait_" + tag, send, recv, parts, lands,
                                      _plan_partials_to_chips, after)
        halves = [_sum_chips(where_op, p, r2, "rs_sum2_" + nm) for p, r2, nm in zip(parts, lands, nms)]
        return _exchange_start("rs_share_start_" + tag, [], halves, len(halves), _plan_share_half, None)

    def reduce_done(tag, stage3, after):
        send, recv, _, fulls, _ = stage3
        return _exchange_wait("rs_share_wait_" + tag, send, recv, [], fulls, _plan_share_half, after)[1]

    def head_major_to_chip_major(t):
        return jnp.transpose(t.reshape(n_heads, 4, hj, HEAD_DIM), (1, 0, 2, 3)).reshape(4, n_heads * hj, HEAD_DIM)

    big_state = dict(zip(names, zip(shards, [m_sc_w_in, m_sc_w_out, m_lru_w_in, m_lru_w_a, m_lru_w_x, m_lru_w_out],
                                    [v_sc_w_in, v_sc_w_out, v_lru_w_in, v_lru_w_a, v_lru_w_x, v_lru_w_out],
                                    [sc_w_in, sc_w_out, lru_w_in, lru_w_a, lru_w_x, lru_w_out])))
    big = {}

    def update(nms, fulls, after):
        for nm, g2 in zip(nms, fulls):
            w2, m4, v4, w4 = big_state[nm]
            outs = _adamw(w2, g2, m4.reshape(w2.shape), v4.reshape(w2.shape), "adamw_" + nm, after=after)
            big[nm] = tuple(t.reshape(w4.shape) for t in outs)
            after = outs[1]
        return after

    g_w_out1 = _mm(y1[None], do1[None], "tn", F32, "lru_out_dw", bm=512, bn=2048)
    dy1 = _mm(do1[None], w_out1, "nt", BF16, "lru_out_dx")[0]
    dpairs1, g_wa, g_wx, g_ba, g_bx, g_lam, g_cw1, g_cb1 = _lru_bwd(
        pairs1, dy1, lru_saved, conv_w1, conv_b1, w_a, b_a, w_x, b_x, lam, "lru_mix_bwd")
    dproj1 = dpairs1.reshape(4, s, e // 2)
    g_w_in1 = _mm(h1[None], dproj1, "tn", F32, "lru_in_dw", bm=512, bn=2048)
    lru_names = ["lru_w_out", "lru_w_a", "lru_w_x", "lru_w_in"]
    lru_rs = reduce_stage1("lru", [g_w_out1.reshape(4, es, d), head_major_to_chip_major(g_wa),
                                   head_major_to_chip_major(g_wx), g_w_in1])
    dh1 = _mm(dproj1, w_in1, "nt", F32, "lru_in_dx", after=lru_rs[4])[0]
    lru_rs = reduce_stage2("lru", lru_rs, lru_names, dh1)
    dx1, do0, acc1 = _norm_mod_bwd(dh1, x1, ng[1], scale[1], dx2, "norm1_bwd", o_prev=o0, gate_prev=gate[0],
                                   after=lru_rs[4])

    g_w_out0 = _mm(y0[None], do0[None], "tn", F32, "sc_out_dw", bm=512, bn=2048)
    out_rs = reduce_stage1("sc_out", [g_w_out0.reshape(4, es, d)])
    dy0 = _mm(do0[None], w_out0, "nt", BF16, "sc_out_dx", after=out_rs[4])[0]
    out_rs = reduce_stage2("sc_out", out_rs, ["sc_w_out"], dy0)
    dproj0, g_cw0 = _sc_bwd(proj0, dy0, conv_w0, "sc_mix_bwd", after=out_rs[4])
    stage1, stage2, after = [], [], None
    for k in range(IN_PIECES):
        g_piece = _mm(h0[None], dproj0, "tn", F32, "sc_in_dw_%d" % k, bm=512, bn=e // IN_PIECES,
                      col_blocks=(k, 1), after=after)
        stage1.append(reduce_stage1("sc_in_%d" % k, [g_piece]))
        after = stage1[k][4]
        if k:
            stage2.append(reduce_stage2("sc_in_%d" % (k - 1), stage1[k - 1], ["sc_w_in_%d" % (k - 1)], after))
            after = stage2[k - 1][4]
    lru_rs = reduce_stage3("lru", lru_rs, lru_names, after)
    k = IN_PIECES - 1
    stage2.append(reduce_stage2("sc_in_%d" % k, stage1[k], ["sc_w_in_%d" % k], lru_rs[4]))
    dh0 = _mm(dproj0, w_in0, "nt", F32, "sc_in_dx", after=stage2[k][4])[0]
    grad_x, acc0 = _norm_mod_bwd(dh0, x2d, ng[0], scale[0], dx1, "norm0_bwd")
    out_rs = reduce_stage3("sc_out", out_rs, ["sc_w_out"], acc0)
    last = update(lru_names, reduce_done("lru", lru_rs, out_rs[4]), None)
    last = update(["sc_w_out"], reduce_done("sc_out", out_rs, last), None)
    g_half = None
    for k, (send, recv, parts, lands, _) in enumerate(stage2):
        parts, lands = _exchange_wait("rs_chips_wait_sc_in_%d" % k, send, recv, parts, lands,
                                      _plan_partials_to_chips, last)
        g_half = _sum_chips(where_op, parts[0], lands[0], "rs_sum2_sc_w_in_%d" % k, piece=(k, IN_PIECES),
                            so_far=g_half)
    in_rs = _exchange_start("rs_share_start_sc_in", [], [g_half], 1, _plan_share_half, None)

    dmod = jnp.stack([jnp.concatenate([acc0[1], acc0[0], acc1[3]]),
                      jnp.concatenate([acc1[1], acc1[0], acc_f[1]])])
    part_list = [jnp.stack([acc0[2], acc1[2]]), acc_f[0], acc_f[2, :1], g_cw0, g_cw1, g_cb1, g_ba, g_bx,
                 g_lam, dmod]
    partials, poffs, n_part = _pack(part_list)
    every = _allgather8([partials.reshape(8, n_part // 8)], "ag_partials", after=in_rs[4])[0].reshape(8, n_part)
    total = _sum_rows8(every, "sum_partials")[0]

    def tot(k, shape):
        size = 1
        for dim in shape:
            size *= dim
        return total[poffs[k]:poffs[k] + size].reshape(shape)

    def my_cols(t, width):
        return lax.dynamic_slice_in_dim(t, chip * width, width, axis=t.ndim - 1)

    loss = tot(2, (1,))[0]
    g_norm_g, g_final_g, g_ada_b = tot(0, (2, d)), tot(1, (d,)), tot(9, (2, 3 * d))
    g_sc_conv_w = my_cols(tot(3, (3, e)), es)[None]
    g_lru_conv_w = my_cols(tot(4, (4, e)), es)[None]
    g_lru_conv_b = my_cols(tot(5, (1, e)), es)
    g_lru_b_a = my_cols(tot(6, (n_heads, 4 * hj)), hj)[None]
    g_lru_b_x = my_cols(tot(7, (n_heads, 4 * hj)), hj)[None]
    g_lru_lambda = my_cols(tot(8, (1, e)), es)

    dmod_all = every[:, poffs[9]:poffs[9] + 6 * d].reshape(8, 2, 3 * d)
    dmod_mine = jnp.transpose(my_cols(dmod_all, f), (1, 0, 2))
    ada = _ada_bwd_adamw(jnp.transpose(c_all), dmod_mine, ada_w, m_ada_w, v_ada_w, "ada_bwd_adamw", after=total)

    small_names = ["norm_g", "ada_b", "final_g", "sc_conv_w", "lru_conv_w", "lru_conv_b", "lru_b_a",
                   "lru_b_x", "lru_lambda"]
    small_w = [norm_g, ada_b, final_g, sc_conv_w, lru_conv_w, lru_conv_b, lru_b_a, lru_b_x, lru_lambda]
    small_g = [g_norm_g, g_ada_b, g_final_g, g_sc_conv_w, g_lru_conv_w, g_lru_conv_b, g_lru_b_a,
               g_lru_b_x, g_lru_lambda]
    small_m = [m_norm_g, m_ada_b, m_final_g, m_sc_conv_w, m_lru_conv_w, m_lru_conv_b, m_lru_b_a,
               m_lru_b_x, m_lru_lambda]
    small_v = [v_norm_g, v_ada_b, v_final_g, v_sc_conv_w, v_lru_conv_w, v_lru_conv_b, v_lru_b_a,
               v_lru_b_x, v_lru_lambda]
    pw, soffs, n_s = _pack(small_w)
    pg, pm, pv = _pack(small_g)[0], _pack(small_m)[0], _pack(small_v)[0]
    shape2 = (n_s // PACK, PACK)
    _, pd, pm2, pv2 = _adamw(pw.reshape(shape2), pg.reshape(shape2), pm.reshape(shape2), pv.reshape(shape2),
                             "adamw_small", after=ada[0])
    update(["sc_w_in"], reduce_done("sc_in", in_rs, pd), None)
    small = {}
    for k, (nm, w_) in enumerate(zip(small_names, small_w)):
        take = lambda t: t.reshape(-1)[soffs[k]:soffs[k] + w_.size].reshape(w_.shape)
        small[nm] = (small_g[k].reshape(w_.shape), take(pd), take(pm2), take(pv2))

    results = dict(small)
    results.update(big)
    results["ada_w"] = tuple(ada)
    order = ["norm_g", "ada_w", "ada_b", "sc_w_in", "sc_conv_w", "sc_w_out", "lru_w_in", "lru_conv_w",
             "lru_conv_b", "lru_w_a", "lru_b_a", "lru_w_x", "lru_b_x", "lru_lambda", "lru_w_out", "final_g"]
    out = [loss, grad_x[None]]
    for kind in range(4):
        out += [results[nm][kind] for nm in order]
    return tuple(out)
```

```python
import functools

import jax
import jax.numpy as jnp
from jax import lax
from jax.experimental import pallas as pl
from jax.experimental.pallas import tpu as pltpu

F32 = jnp.float32
BF16 = jnp.bfloat16
MESH = pl.DeviceIdType.MESH
ANY = pl.BlockSpec(memory_space=pl.ANY)

RMS_EPS = 1e-6
RGLRU_C = 8.0
HEAD_DIM = 256
ADAM_LR = 0.001
ADAM_B1 = 0.9
ADAM_B2 = 0.999
ADAM_EPS = 1e-08
ADAM_WD = 0.01
ADAM_STEP = 10
V7X_VMEM_LIMIT = 56 * 1024 * 1024
IN_PIECES = 2
LANES = 128
SUBLANES = 8
PACK = SUBLANES * LANES


def _blk(dim, pref, unit=LANES):
    if dim <= pref:
        return dim
    b = (pref // unit) * unit
    while b > unit and dim % b:
        b -= unit
    assert dim % b == 0, (dim, pref, unit)
    return b


def _params(sem=None):
    return pltpu.CompilerParams(dimension_semantics=sem, vmem_limit_bytes=V7X_VMEM_LIMIT)


def _ordered(body, n_in, after):
    if after is None:
        return body, [], []

    def ordered_body(*refs):
        return body(*refs[:n_in], *refs[n_in + 1:])

    return ordered_body, [ANY], [after]


def _pos():
    return lax.axis_index("x"), lax.axis_index("y"), lax.axis_index("c")


def _other_chips(x, y):
    return [(1 - x, y), (x, 1 - y), (1 - x, 1 - y)]


def _allgather8(arrs, name, after=None):
    n_t = len(arrs)
    ms = [a.shape[0] for a in arrs]

    def gather(*refs):
        ins, outs = refs[:n_t], refs[n_t:2 * n_t]
        send_sems, recv_sems, local_sems = refs[2 * n_t:]
        x, y, c = _pos()
        me, sibling = (x, y, c), (x, y, 1 - c)
        chips = _other_chips(x, y)

        def rows(t, px, py, pc):
            return outs[t].at[pl.ds((4 * px + 2 * py + pc) * ms[t], ms[t])]

        def copy(t, k, block, to, src=None):
            return pltpu.make_async_remote_copy(
                src_ref=rows(t, *block) if src is None else src, dst_ref=rows(t, *block),
                send_sem=send_sems.at[7 * t + k], recv_sem=recv_sems.at[7 * t + k],
                device_id=to, device_id_type=MESH)

        mine, first, passed = [], [], []
        for t in range(n_t):
            src = ins[t]
            cp = pltpu.make_async_copy(src, rows(t, *me), local_sems.at[t])
            cp.start()
            mine.append(cp)
            sends = [copy(t, 0, me, sibling, src=src)]
            sends += [copy(t, 1 + j, me, (*chip, c), src=src) for j, chip in enumerate(chips)]
            for cp in sends:
                cp.start()
            first += sends
        for t in range(n_t):
            for j, chip in enumerate(chips):
                copy(t, 1 + j, (*chip, c), me).wait_recv()
                cp = copy(t, 4 + j, (*chip, c), sibling)
                cp.start()
                passed.append(cp)
        for t in range(n_t):
            copy(t, 0, sibling, me).wait_recv()
            for j, chip in enumerate(chips):
                copy(t, 4 + j, (*chip, 1 - c), me).wait_recv()
        for cp in first + passed:
            cp.wait_send()
        for cp in mine:
            cp.wait()

    body, extra_specs, extra = _ordered(gather, n_t, after)
    return pl.pallas_call(
        body, name=name,
        out_shape=[jax.ShapeDtypeStruct((8 * m, a.shape[1]), a.dtype) for m, a in zip(ms, arrs)],
        in_specs=[ANY] * n_t + extra_specs, out_specs=[ANY] * n_t,
        scratch_shapes=[pltpu.SemaphoreType.DMA((7 * n_t,)), pltpu.SemaphoreType.DMA((7 * n_t,)),
                        pltpu.SemaphoreType.DMA((n_t,))],
    )(*arrs, *extra)


HBM = pl.BlockSpec(memory_space=pltpu.HBM)
SEM = pl.BlockSpec(memory_space=pltpu.SEMAPHORE)
TOKEN = pl.BlockSpec(memory_space=pltpu.VMEM)
IN_FLIGHT = pltpu.CompilerParams(has_side_effects=pltpu.SideEffectType.DATAFLOW_SIDE_EFFECTING)


def _in_hbm(arrs):
    return [pltpu.with_memory_space_constraint(a, pltpu.HBM) for a in arrs]


def _shard_rows(ref, h, px, py, pc):
    return ref.at[pl.ds((4 * px + 2 * py + pc) * h, h)]


def _gather_start(groups, after, name):
    bufs = [b for grp in groups for b in grp]
    n_t, n_g = len(bufs), len(groups)

    def body(*refs):
        sems, thru = refs[n_t + 1:n_t + 1 + 2 * n_g], refs[n_t + 1 + 2 * n_g:2 * n_t + 1 + 2 * n_g]
        token = refs[-1]
        x, y, c = _pos()
        t = 0
        for g, grp in enumerate(groups):
            for i in range(len(grp)):
                h = bufs[t].shape[0] // 8
                rows = _shard_rows(thru[t], h, x, y, c)
                for j, chip in enumerate(_other_chips(x, y)):
                    pltpu.make_async_remote_copy(
                        src_ref=rows, dst_ref=rows, send_sem=sems[2 * g].at[3 * i + j],
                        recv_sem=sems[2 * g + 1].at[3 * i + j], device_id=(*chip, c),
                        device_id_type=MESH).start()
                t += 1
        token[...] = jnp.zeros_like(token)

    sem_shapes = []
    for grp in groups:
        sem_shapes += [pltpu.SemaphoreType.DMA((3 * len(grp),))] * 2
    out = pl.pallas_call(
        body, name=name,
        out_shape=sem_shapes + [pltpu.HBM(b.shape, b.dtype) for b in bufs] + [jax.ShapeDtypeStruct((8, LANES), F32)],
        in_specs=[HBM] * n_t + [ANY], out_specs=[SEM] * (2 * n_g) + [HBM] * n_t + [TOKEN],
        input_output_aliases={t: 2 * n_g + t for t in range(n_t)},
        compiler_params=IN_FLIGHT,
    )(*_in_hbm(bufs), after)
    sems, thru, token = out[:2 * n_g], out[2 * n_g:2 * n_g + n_t], out[-1]
    per_group, t = [], 0
    for g, grp in enumerate(groups):
        per_group.append((sems[2 * g], sems[2 * g + 1], thru[t:t + len(grp)]))
        t += len(grp)
    return per_group, token


def _gather_forward(send_sems, recv_sems, bufs, after, name, sources=(0, 1, 2)):
    n_t = len(bufs)

    def body(*refs):
        ins = refs[:n_t]
        send1, recv1 = refs[n_t], refs[n_t + 1]
        send2, recv2 = refs[n_t + 3], refs[n_t + 4]
        token = refs[-1]
        x, y, c = _pos()
        chips = _other_chips(x, y)
        for t in range(n_t):
            h = bufs[t].shape[0] // 8
            mine = _shard_rows(ins[t], h, x, y, c)
            for j in sources:
                chip = chips[j]
                landed = _shard_rows(ins[t], h, *chip, c)
                pltpu.make_async_remote_copy(
                    src_ref=mine, dst_ref=landed, send_sem=send1.at[3 * t + j], recv_sem=recv1.at[3 * t + j],
                    device_id=(*chip, c), device_id_type=MESH).wait_recv()
                pltpu.make_async_remote_copy(
                    src_ref=landed, dst_ref=landed, send_sem=send2.at[3 * t + j], recv_sem=recv2.at[3 * t + j],
                    device_id=(x, y, 1 - c), device_id_type=MESH).start()
        for t in range(n_t):
            h = bufs[t].shape[0] // 8
            mine = _shard_rows(ins[t], h, x, y, c)
            for j in sources:
                pltpu.make_async_remote_copy(
                    src_ref=mine, dst_ref=mine, send_sem=send1.at[3 * t + j], recv_sem=recv1.at[3 * t + j],
                    device_id=(*chips[j], c), device_id_type=MESH).wait_send()
        token[...] = jnp.zeros_like(token)

    out = pl.pallas_call(
        body, name=name,
        out_shape=[pltpu.SemaphoreType.DMA((3 * n_t,))] * 2 + [pltpu.HBM(b.shape, b.dtype) for b in bufs]
        + [jax.ShapeDtypeStruct((8, LANES), F32)],
        in_specs=[HBM] * n_t + [SEM, SEM, ANY], out_specs=[SEM, SEM] + [HBM] * n_t + [TOKEN],
        input_output_aliases={t: 2 + t for t in range(n_t)},
        compiler_params=IN_FLIGHT,
    )(*bufs, send_sems, recv_sems, after)
    return out[0], out[1], out[2:2 + n_t], out[-1]


def _gather_finish(send_sems, recv_sems, bufs, after, name, sources=(0, 1, 2)):
    n_t = len(bufs)

    def body(*refs):
        ins = refs[:n_t]
        send2, recv2 = refs[n_t], refs[n_t + 1]
        x, y, c = _pos()
        chips = _other_chips(x, y)
        for t in range(n_t):
            h = bufs[t].shape[0] // 8
            for j in sources:
                chip = chips[j]
                sent = _shard_rows(ins[t], h, *chip, c)
                got = _shard_rows(ins[t], h, *chip, 1 - c)
                cp = pltpu.make_async_remote_copy(
                    src_ref=sent, dst_ref=got, send_sem=send2.at[3 * t + j], recv_sem=recv2.at[3 * t + j],
                    device_id=(x, y, 1 - c), device_id_type=MESH)
                cp.wait_send()
                cp.wait_recv()

    return pl.pallas_call(
        body, name=name, out_shape=[pltpu.HBM(b.shape, b.dtype) for b in bufs],
        in_specs=[HBM] * n_t + [SEM, SEM, ANY], out_specs=[HBM] * n_t,
        input_output_aliases={t: t for t in range(n_t)},
        compiler_params=IN_FLIGHT,
    )(*bufs, send_sems, recv_sems, after)


def _part_rows(ref, h, px, py, pc, which):
    return ref.at[pl.ds((4 * px + 2 * py + pc) * h + which * (h // 2), h // 2)]


def _remote(src, dst, send_sem, recv_sem, device):
    return pltpu.make_async_remote_copy(src_ref=src, dst_ref=dst, send_sem=send_sem, recv_sem=recv_sem,
                                        device_id=device, device_id_type=MESH)


def _first_start(buf, after, name):
    h = buf.shape[0] // 8

    def body(buf_in, after_ref, send, recv, thru, token):
        x, y, c = _pos()
        rows = _shard_rows(thru, h, x, y, c)
        for j, chip in enumerate(_other_chips(x, y)[:2]):
            _remote(rows, rows, send.at[j], recv.at[j], (*chip, c)).start()
        token[...] = jnp.zeros_like(token)

    return pl.pallas_call(
        body, name=name,
        out_shape=[pltpu.SemaphoreType.DMA((2,))] * 2 + [pltpu.HBM(buf.shape, buf.dtype),
                                                         jax.ShapeDtypeStruct((8, LANES), F32)],
        in_specs=[HBM, ANY], out_specs=[SEM, SEM, HBM, TOKEN], input_output_aliases={0: 2},
        compiler_params=IN_FLIGHT,
    )(*_in_hbm([buf]), after)


def _first_relay(send_a, recv_a, buf, j, after, name):
    h = buf.shape[0] // 8

    def body(buf_in, send_a, recv_a, *rest):
        send_b, recv_b, send_c, recv_c, _, token = rest[len(after):]
        x, y, c = _pos()
        chips = _other_chips(x, y)
        nbr, other = chips[j], chips[1 - j]
        mine, landed = _shard_rows(buf_in, h, x, y, c), _shard_rows(buf_in, h, *nbr, c)
        _remote(mine, landed, send_a.at[j], recv_a.at[j], (*nbr, c)).wait_recv()
        _remote(landed, landed, send_b.at[0], recv_b.at[0], (x, y, 1 - c)).start()
        part = _part_rows(buf_in, h, *nbr, c, j)
        _remote(part, part, send_c.at[0], recv_c.at[0], (*other, c)).start()
        _remote(mine, mine, send_a.at[j], recv_a.at[j], (*nbr, c)).wait_send()
        token[...] = jnp.zeros_like(token)

    out = pl.pallas_call(
        body, name=name,
        out_shape=[pltpu.SemaphoreType.DMA((1,))] * 4 + [pltpu.HBM(buf.shape, buf.dtype),
                                                         jax.ShapeDtypeStruct((8, LANES), F32)],
        in_specs=[HBM, SEM, SEM] + [ANY] * len(after), out_specs=[SEM] * 4 + [HBM, TOKEN],
        input_output_aliases={0: 4}, compiler_params=IN_FLIGHT,
    )(buf, send_a, recv_a, *after)
    return tuple(out)


def _first_relay_done(send_b, recv_b, buf, j, after, name):
    h = buf.shape[0] // 8

    def body(buf_in, send_b, recv_b, after_ref, thru):
        x, y, c = _pos()
        nbr = _other_chips(x, y)[j]
        cp = _remote(_shard_rows(buf_in, h, *nbr, c), _shard_rows(buf_in, h, *nbr, 1 - c),
                     send_b.at[0], recv_b.at[0], (x, y, 1 - c))
        cp.wait_send()
        cp.wait_recv()

    return pl.pallas_call(
        body, name=name, out_shape=pltpu.HBM(buf.shape, buf.dtype),
        in_specs=[HBM, SEM, SEM, ANY], out_specs=HBM, input_output_aliases={0: 0},
        compiler_params=IN_FLIGHT,
    )(buf, send_b, recv_b, after)


def _first_diagonal(relays, buf, after, name):
    h = buf.shape[0] // 8

    def body(buf_in, send_c0, recv_c0, send_c1, recv_c1, after_ref, send_b, recv_b, thru, token):
        x, y, c = _pos()
        chips = _other_chips(x, y)
        diag = chips[2]
        for j, (send_c, recv_c) in enumerate(((send_c0, recv_c0), (send_c1, recv_c1))):
            part = _part_rows(buf_in, h, *diag, c, j)
            _remote(part, part, send_c.at[0], recv_c.at[0], (*chips[1 - j], c)).wait_recv()
        whole = _shard_rows(buf_in, h, *diag, c)
        _remote(whole, whole, send_b.at[0], recv_b.at[0], (x, y, 1 - c)).start()
        token[...] = jnp.zeros_like(token)

    out = pl.pallas_call(
        body, name=name,
        out_shape=[pltpu.SemaphoreType.DMA((1,))] * 2 + [pltpu.HBM(buf.shape, buf.dtype),
                                                         jax.ShapeDtypeStruct((8, LANES), F32)],
        in_specs=[HBM] + [SEM] * 4 + [ANY], out_specs=[SEM, SEM, HBM, TOKEN], input_output_aliases={0: 2},
        compiler_params=IN_FLIGHT,
    )(buf, relays[0][0], relays[0][1], relays[1][0], relays[1][1], after)
    return tuple(out)


def _first_diagonal_done(send_b, recv_b, relays, buf, after, name):
    h = buf.shape[0] // 8

    def body(buf_in, send_b, recv_b, send_c0, recv_c0, send_c1, recv_c1, after_ref, thru):
        x, y, c = _pos()
        chips = _other_chips(x, y)
        diag = chips[2]
        cp = _remote(_shard_rows(buf_in, h, *diag, c), _shard_rows(buf_in, h, *diag, 1 - c),
                     send_b.at[0], recv_b.at[0], (x, y, 1 - c))
        cp.wait_send()
        cp.wait_recv()
        for j, (send_c, recv_c) in enumerate(((send_c0, recv_c0), (send_c1, recv_c1))):
            part = _part_rows(buf_in, h, *chips[j], c, j)
            _remote(part, part, send_c.at[0], recv_c.at[0], (*chips[1 - j], c)).wait_send()

    return pl.pallas_call(
        body, name=name, out_shape=pltpu.HBM(buf.shape, buf.dtype),
        in_specs=[HBM] + [SEM] * 6 + [ANY], out_specs=HBM, input_output_aliases={0: 0},
        compiler_params=IN_FLIGHT,
    )(buf, send_b, recv_b, relays[0][0], relays[0][1], relays[1][0], relays[1][1], after)


def _exchange_start(name, srcs, lands, n_copies, plan, after):
    ns, nl = len(srcs), len(lands)
    extra = [] if after is None else [after]

    def body(*refs):
        base = ns + nl + len(extra)
        send_sems, recv_sems = refs[base], refs[base + 1]
        src_refs, land_refs = refs[base + 2:base + 2 + ns], refs[base + 2 + ns:base + 2 + ns + nl]
        token = refs[-1]
        x, y, c = _pos()
        copies = plan(src_refs, land_refs, x, y, c)
        assert len(copies) == n_copies
        for k, (src, dst, dev) in enumerate(copies):
            pltpu.make_async_remote_copy(
                src_ref=src, dst_ref=dst, send_sem=send_sems.at[k], recv_sem=recv_sems.at[k],
                device_id=dev, device_id_type=MESH).start()
        token[...] = jnp.zeros_like(token)

    out = pl.pallas_call(
        body, name=name,
        out_shape=[pltpu.SemaphoreType.DMA((n_copies,))] * 2
        + [pltpu.HBM(a.shape, a.dtype) for a in list(srcs) + list(lands)] + [jax.ShapeDtypeStruct((8, LANES), F32)],
        in_specs=[HBM] * (ns + nl) + [ANY] * len(extra), out_specs=[SEM, SEM] + [HBM] * (ns + nl) + [TOKEN],
        input_output_aliases={i: 2 + i for i in range(ns + nl)},
        compiler_params=IN_FLIGHT,
    )(*_in_hbm(list(srcs) + list(lands)), *extra)
    return out[0], out[1], out[2:2 + ns], out[2 + ns:2 + ns + nl], out[-1]


def _exchange_wait(name, send_sems, recv_sems, srcs, lands, plan, after):
    ns, nl = len(srcs), len(lands)

    def body(*refs):
        src_refs, land_refs = refs[:ns], refs[ns:ns + nl]
        send, recv = refs[ns + nl], refs[ns + nl + 1]
        x, y, c = _pos()
        for k, (src, dst, dev) in enumerate(plan(src_refs, land_refs, x, y, c)):
            cp = pltpu.make_async_remote_copy(
                src_ref=src, dst_ref=dst, send_sem=send.at[k], recv_sem=recv.at[k],
                device_id=dev, device_id_type=MESH)
            cp.wait_send()
            cp.wait_recv()

    out = pl.pallas_call(
        body, name=name, out_shape=[pltpu.HBM(a.shape, a.dtype) for a in list(srcs) + list(lands)],
        in_specs=[HBM] * (ns + nl) + [SEM, SEM, ANY], out_specs=[HBM] * (ns + nl),
        input_output_aliases={i: i for i in range(ns + nl)},
        compiler_params=IN_FLIGHT,
    )(*srcs, *lands, send_sems, recv_sems, after)
    return out[:ns], out[ns:]


def _plan_other_half_to_sibling(src_refs, land_refs, x, y, c):
    out = []
    for g_ref, r_ref in zip(src_refs, land_refs):
        h = g_ref.shape[1] // 2
        out.append((g_ref.at[:, pl.ds((1 - c) * h, h), :], r_ref, (x, y, 1 - c)))
    return out


def _plan_partials_to_chips(src_refs, land_refs, x, y, c):
    out = []
    for p_ref, r_ref in zip(src_refs, land_refs):
        for j, (px, py) in enumerate(_other_chips(x, y)):
            out.append((p_ref.at[2 * px + py], r_ref.at[j], (px, py, c)))
    return out


def _plan_share_half(src_refs, land_refs, x, y, c):
    out = []
    for f_ref in land_refs:
        h = f_ref.shape[0] // 2
        rows = f_ref.at[pl.ds(c * h, h)]
        out.append((rows, rows, (x, y, 1 - c)))
    return out


def _cast_into_slot(chip, w, name):
    r, c = w.shape
    br, bc = _blk(r, 512, 8), _blk(c, 2048)
    nb = r // br

    def body(chip_ref, w_ref, o_ref):
        o_ref[...] = w_ref[...].astype(BF16)

    grid_spec = pltpu.PrefetchScalarGridSpec(
        num_scalar_prefetch=1, grid=(nb, c // bc),
        in_specs=[pl.BlockSpec((br, bc), lambda i, j, chip_ref: (i, j))],
        out_specs=pl.BlockSpec((br, bc), lambda i, j, chip_ref: (chip_ref[0] * nb + i, j)))
    return pl.pallas_call(
        body, name=name, grid_spec=grid_spec,
        out_shape=jax.ShapeDtypeStruct((4 * r, c), BF16),
        compiler_params=_params(("parallel", "parallel")),
    )(chip, w)


def _sum_own_and_sibling(core, g, r1, name):
    _, r, c = g.shape
    h = r // 2
    br, bc = _blk(h, 256, 8), _blk(c, 2048)
    nb = h // br

    def body(core_ref, g_ref, r_ref, o_ref):
        o_ref[...] = (g_ref[...] + r_ref[...]).astype(BF16)

    grid_spec = pltpu.PrefetchScalarGridSpec(
        num_scalar_prefetch=1, grid=(4, nb, c // bc),
        in_specs=[pl.BlockSpec((None, br, bc), lambda k, i, j, core_ref: (k, core_ref[0] * nb + i, j)),
                  pl.BlockSpec((None, br, bc), lambda k, i, j, core_ref: (k, i, j))],
        out_specs=pl.BlockSpec((None, br, bc), lambda k, i, j, core_ref: (k, i, j)))
    return pl.pallas_call(
        body, name=name, grid_spec=grid_spec,
        out_shape=jax.ShapeDtypeStruct((4, h, c), BF16),
        compiler_params=_params(("parallel", "parallel", "parallel")),
    )(core, g, r1)


def _sum_chips(where, p, r2, name, piece=(0, 1), so_far=None):
    _, h, c = p.shape
    k, n = piece
    br, bc = _blk(h, 256, 8), _blk(c, 2048)
    nb, ncb = h // br, c // bc

    def body(where_ref, p_ref, r_ref, *rest):
        acc = p_ref[...].astype(F32)
        for j in range(3):
            acc = acc + r_ref[j].astype(F32)
        rest[-1][...] = acc

    extra_specs, extra, aliases = ([], [], {}) if so_far is None else ([ANY], [so_far], {3: 0})
    grid_spec = pltpu.PrefetchScalarGridSpec(
        num_scalar_prefetch=1, grid=(nb, ncb),
        in_specs=[pl.BlockSpec((None, br, bc), lambda i, j, where_ref: (where_ref[0], i, j)),
                  pl.BlockSpec((3, br, bc), lambda i, j, where_ref: (0, i, j))] + extra_specs,
        out_specs=pl.BlockSpec((br, bc), lambda i, j, where_ref: (where_ref[1] * nb + i, k * ncb + j)))
    return pl.pallas_call(
        body, name=name, grid_spec=grid_spec,
        out_shape=jax.ShapeDtypeStruct((2 * h, n * c), F32), input_output_aliases=aliases,
        compiler_params=_params(("parallel", "parallel")),
    )(where, p, r2, *extra)


def _adamw_math(w, g, m, v):
    m2 = ADAM_B1 * m + (1.0 - ADAM_B1) * g
    v2 = ADAM_B2 * v + (1.0 - ADAM_B2) * (g * g)
    m_hat = m2 / (1.0 - ADAM_B1 ** ADAM_STEP)
    v_hat = v2 / (1.0 - ADAM_B2 ** ADAM_STEP)
    delta = -ADAM_LR * (m_hat / (jnp.sqrt(v_hat) + ADAM_EPS) + ADAM_WD * w)
    return delta, m2, v2


def _adamw(w, g, m, v, name, after=None):
    r, c = w.shape
    br, bc = _blk(r, 128, 8), _blk(c, 2048)

    def body(w_ref, g_ref, m_ref, v_ref, go_ref, d_ref, m2_ref, v2_ref):
        gv = g_ref[...]
        d, m2, v2 = _adamw_math(w_ref[...], gv, m_ref[...], v_ref[...])
        go_ref[...] = gv
        d_ref[...] = d
        m2_ref[...] = m2
        v2_ref[...] = v2

    spec = pl.BlockSpec((br, bc), lambda i, j: (i, j))
    body, extra_specs, extra = _ordered(body, 4, after)
    return pl.pallas_call(
        body, name=name, grid=(r // br, c // bc),
        in_specs=[spec] * 4 + extra_specs, out_specs=[spec] * 4,
        out_shape=[jax.ShapeDtypeStruct((r, c), F32)] * 4,
        compiler_params=_params(("parallel", "parallel")),
    )(w, g, m, v, *extra)


def _sum_rows8(g, name, after=None):
    n = g.shape[1]

    def body(g_ref, o_ref):
        acc = g_ref[0:1, :]
        for k in range(1, 8):
            acc = acc + g_ref[k:k + 1, :]
        o_ref[...] = acc

    body, extra_specs, extra = _ordered(body, 1, after)
    return pl.pallas_call(
        body, name=name, out_shape=jax.ShapeDtypeStruct((1, n), F32),
        in_specs=[pl.BlockSpec(memory_space=pltpu.VMEM)] + extra_specs,
        out_specs=pl.BlockSpec(memory_space=pltpu.VMEM),
        compiler_params=_params(),
    )(g, *extra)


def _mm(a, b, mode, out_dtype, name, bm=1024, bn=None, after=None, col_blocks=None):
    if mode == "nn":
        (_, m, k), (g, _, n) = a.shape, b.shape
    elif mode == "tn":
        (_, k, m), (g, _, n) = a.shape, b.shape
    else:
        (g, m, k), (_, n, _) = a.shape, b.shape
    if bn is None:
        bn = 1024 if k <= 2048 else 512
    bm, bn = _blk(m, bm), _blk(n, bn)

    if mode == "nt":
        def body(a_ref, b_ref, o_ref, acc_ref):
            part = lax.dot_general(a_ref[...], b_ref[...], (((1,), (1,)), ((), ())),
                                   preferred_element_type=F32)
            if g == 1:
                o_ref[...] = part.astype(out_dtype)
            else:
                gi = pl.program_id(2)

                @pl.when(gi == 0)
                def _():
                    acc_ref[...] = part

                @pl.when(gi > 0)
                def _():
                    acc_ref[...] += part

                @pl.when(gi == g - 1)
                def _():
                    o_ref[...] = acc_ref[...].astype(out_dtype)

        body, extra_specs, extra = _ordered(body, 2, after)
        return pl.pallas_call(
            body, name=name, grid=(m // bm, n // bn, g),
            in_specs=[pl.BlockSpec((None, bm, k), lambda i, j, gi: (gi, i, 0)),
                      pl.BlockSpec((None, bn, k), lambda i, j, gi: (gi, j, 0))] + extra_specs,
            out_specs=pl.BlockSpec((None, bm, bn), lambda i, j, gi: (0, i, j)),
            out_shape=jax.ShapeDtypeStruct((1, m, n), out_dtype),
            scratch_shapes=[pltpu.VMEM((bm, bn), F32)],
            compiler_params=_params(("parallel", "parallel", "arbitrary")),
        )(a, b, *extra)

    contract = (((1,), (0,)), ((), ())) if mode == "nn" else (((0,), (0,)), ((), ()))

    def body(a_ref, b_ref, o_ref):
        o_ref[...] = lax.dot_general(a_ref[...], b_ref[...], contract,
                                     preferred_element_type=F32).astype(out_dtype)

    a_spec = (pl.BlockSpec((None, bm, k), lambda i, gi, j: (0, i, 0)) if mode == "nn"
              else pl.BlockSpec((None, k, bm), lambda i, gi, j: (0, 0, i)))
    first, count = (0, n // bn) if col_blocks is None else col_blocks
    body, extra_specs, extra = _ordered(body, 2, after)
    return pl.pallas_call(
        body, name=name, grid=(m // bm, g, count),
        in_specs=[a_spec, pl.BlockSpec((None, k, bn), lambda i, gi, j: (gi, 0, first + j))] + extra_specs,
        out_specs=pl.BlockSpec((None, bm, bn), lambda i, gi, j: (gi, i, j)),
        out_shape=jax.ShapeDtypeStruct((g, m, count * bn), out_dtype),
        compiler_params=_params(("parallel", "parallel", "parallel")),
    )(a, b, *extra)


def _mm_plane(a, b, plane, name, planes_so_far=None, after=None, bm=1024, bn=1024):
    (m, k), (g, _, n) = a.shape, b.shape
    bm, bn = _blk(m, bm), _blk(n, bn)

    def body(plane_ref, a_ref, b_ref, *rest):
        rest[-1][...] = jnp.dot(a_ref[...], b_ref[...], preferred_element_type=F32).astype(BF16)

    extra_specs, extra, aliases = [], [], {}
    if planes_so_far is not None:
        extra_specs.append(ANY)
        extra.append(planes_so_far)
        aliases = {3: 0}
    if after is not None:
        extra_specs.append(ANY)
        extra.append(after)
    grid_spec = pltpu.PrefetchScalarGridSpec(
        num_scalar_prefetch=1, grid=(m // bm, n // bn),
        in_specs=[pl.BlockSpec((bm, k), lambda i, j, p: (i, 0)),
                  pl.BlockSpec((None, k, bn), lambda i, j, p: (p[0], 0, j))] + extra_specs,
        out_specs=pl.BlockSpec((None, bm, bn), lambda i, j, p: (p[0], i, j)))
    return pl.pallas_call(
        body, name=name, grid_spec=grid_spec,
        out_shape=jax.ShapeDtypeStruct((g, m, n), BF16), input_output_aliases=aliases,
        compiler_params=_params(("parallel", "parallel")),
    )(plane, a, b, *extra)


def _row_specs(br, d):
    return (pl.BlockSpec((br, d), lambda i: (i, 0)), pl.BlockSpec((1, d), lambda i: (0, 0)),
            pl.BlockSpec((8, d), lambda i: (0, 0)))


def _rstd(xv):
    return lax.rsqrt(jnp.mean(xv * xv, axis=-1, keepdims=True) + RMS_EPS)


def _colsum(v):
    return jnp.sum(v, axis=0, keepdims=True)


def _norm_mod_fwd(x, g, scale, shift, name, o=None, gate=None, after=None):
    s, d = x.shape
    br = _blk(s, 256, 8)
    has_res = o is not None
    row, vec, _ = _row_specs(br, d)

    def body(*refs):
        if has_res:
            x_ref, o_ref, gate_ref, g_ref, sc_ref, sh_ref, x1_ref, h_ref = refs
            xv = x_ref[...] + gate_ref[...] * o_ref[...]
            x1_ref[...] = xv
        else:
            x_ref, g_ref, sc_ref, sh_ref, h_ref = refs
            xv = x_ref[...]
        n = xv * _rstd(xv) * g_ref[...]
        h_ref[...] = (n * (1.0 + sc_ref[...]) + sh_ref[...]).astype(BF16)

    ins = [x] + ([o, gate] if has_res else []) + [g, scale, shift]
    in_specs = [row] + ([row, vec] if has_res else []) + [vec] * 3
    out_shape = ([jax.ShapeDtypeStruct((s, d), F32)] if has_res else []) + [jax.ShapeDtypeStruct((s, d), BF16)]
    body, extra_specs, extra = _ordered(body, len(ins), after)
    out = pl.pallas_call(
        body, name=name, grid=(s // br,), in_specs=in_specs + extra_specs, out_specs=[row] * len(out_shape),
        out_shape=out_shape, compiler_params=_params(("parallel",)),
    )(*ins, *extra)
    return out if has_res else out[0]


def _final_loss(x1, o1, gate1, final_g, tgt, name):
    s, d = x1.shape
    br = _blk(s, 256, 8)
    row, vec, acc = _row_specs(br, d)

    def body(x1_ref, o_ref, gate_ref, g_ref, t_ref, dx_ref, do_ref, acc_ref):
        @pl.when(pl.program_id(0) == 0)
        def _():
            acc_ref[...] = jnp.zeros_like(acc_ref)

        gate, o, g = gate_ref[...], o_ref[...], g_ref[...]
        x2 = x1_ref[...] + gate * o
        r = _rstd(x2)
        xh = x2 * r
        err = xh * g - t_ref[...]
        loss = 0.5 * _colsum(jnp.mean(err * err, axis=-1, keepdims=True))
        dout = err * (1.0 / d)
        dxh = dout * g
        dx2 = r * (dxh - xh * jnp.mean(dxh * xh, axis=-1, keepdims=True))
        dx_ref[...] = dx2
        do_ref[...] = (dx2 * gate).astype(BF16)
        acc_ref[0:1, :] += _colsum(dout * xh)
        acc_ref[1:2, :] += _colsum(dx2 * o)
        acc_ref[2:3, :] += jnp.broadcast_to(loss, (1, d))

    return pl.pallas_call(
        body, name=name, grid=(s // br,),
        in_specs=[row, row, vec, vec, row], out_specs=[row, row, acc],
        out_shape=[jax.ShapeDtypeStruct((s, d), F32), jax.ShapeDtypeStruct((s, d), BF16),
                   jax.ShapeDtypeStruct((8, d), F32)],
        compiler_params=_params(("arbitrary",)),
    )(x1, o1, gate1, final_g, tgt)


def _norm_mod_bwd(dh, x, g, scale, dx_next, name, o_prev=None, gate_prev=None, after=None):
    s, d = x.shape
    br = _blk(s, 256, 8)
    has_prev = o_prev is not None
    row, vec, acc = _row_specs(br, d)

    def body(*refs):
        if has_prev:
            dh_ref, x_ref, g_ref, sc_ref, dxn_ref, o_ref, gate_ref, dx_ref, do_ref, acc_ref = refs
        else:
            dh_ref, x_ref, g_ref, sc_ref, dxn_ref, dx_ref, acc_ref = refs

        @pl.when(pl.program_id(0) == 0)
        def _():
            acc_ref[...] = jnp.zeros_like(acc_ref)

        xv, gv, dhv = x_ref[...], g_ref[...], dh_ref[...]
        r = _rstd(xv)
        xh = xv * r
        acc_ref[0:1, :] += _colsum(dhv * (xh * gv))
        acc_ref[1:2, :] += _colsum(dhv)
        dn = dhv * (1.0 + sc_ref[...])
        acc_ref[2:3, :] += _colsum(dn * xh)
        dxh = dn * gv
        dx = dxn_ref[...] + r * (dxh - xh * jnp.mean(dxh * xh, axis=-1, keepdims=True))
        dx_ref[...] = dx
        if has_prev:
            acc_ref[3:4, :] += _colsum(dx * o_ref[...])
            do_ref[...] = (dx * gate_ref[...]).astype(BF16)

    ins = [dh, x, g, scale, dx_next] + ([o_prev, gate_prev] if has_prev else [])
    in_specs = [row, row, vec, vec, row] + ([row, vec] if has_prev else [])
    out_shape = [jax.ShapeDtypeStruct((s, d), F32)]
    out_specs = [row]
    if has_prev:
        out_shape.append(jax.ShapeDtypeStruct((s, d), BF16))
        out_specs.append(row)
    out_shape.append(jax.ShapeDtypeStruct((8, d), F32))
    out_specs.append(acc)
    body, extra_specs, extra = _ordered(body, len(ins), after)
    return pl.pallas_call(
        body, name=name, grid=(s // br,), in_specs=in_specs + extra_specs, out_specs=out_specs,
        out_shape=out_shape, compiler_params=_params(("arbitrary",)),
    )(*ins, *extra)


def _tiles(p):
    s, c = p.shape
    return p.reshape(s // SUBLANES, SUBLANES, c)


def _shift_down(p, k):
    if k == 0:
        return p
    r = pltpu.roll(_tiles(p), k, 1)
    before = jnp.concatenate([jnp.zeros_like(r[:1]), r[:-1]], axis=0)
    rows = lax.broadcasted_iota(jnp.int32, r.shape, 1)
    return jnp.where(rows >= k, r, before).reshape(p.shape)


def _shift_up(p, k):
    if k == 0:
        return p
    r = pltpu.roll(_tiles(p), SUBLANES - k, 1)
    after = jnp.concatenate([r[1:], jnp.zeros_like(r[:1])], axis=0)
    rows = lax.broadcasted_iota(jnp.int32, r.shape, 1)
    return jnp.where(rows < SUBLANES - k, r, after).reshape(p.shape)


def _sigmoid(z):
    return 0.5 * (jnp.tanh(0.5 * z) + 1.0)


def _sc_parts(proj_ref, w_ref):
    b, cg, v, g = (proj_ref[i].astype(F32) for i in range(4))
    p = cg * v
    u = w_ref[2:3, :] * p + w_ref[1:2, :] * _shift_down(p, 1) + w_ref[0:1, :] * _shift_down(p, 2)
    return b, cg, v, g, p, u


def _sc_fwd(proj, conv_w, name):
    _, s, e = proj.shape
    bc = _blk(e, 256)

    def body(proj_ref, w_ref, y_ref):
        b, _, _, g, _, u = _sc_parts(proj_ref, w_ref)
        y_ref[...] = (b * u * (g * _sigmoid(g))).astype(BF16)

    return pl.pallas_call(
        body, name=name, grid=(e // bc,),
        in_specs=[pl.BlockSpec((4, s, bc), lambda j: (0, 0, j)), pl.BlockSpec((3, bc), lambda j: (0, j))],
        out_specs=pl.BlockSpec((s, bc), lambda j: (0, j)),
        out_shape=jax.ShapeDtypeStruct((s, e), BF16),
        compiler_params=_params(("parallel",)),
    )(proj, conv_w)


def _sc_bwd(proj, dy, conv_w, name, after=None):
    _, s, e = proj.shape
    bc = _blk(e, 256)

    def body(proj_ref, dy_ref, w_ref, dp_ref, dw_ref):
        b, cg, v, g, p, u = _sc_parts(proj_ref, w_ref)
        dyv = dy_ref[...].astype(F32)
        sig = _sigmoid(g)
        t = dyv * (g * sig)
        du = t * b
        dp_ref[0] = (t * u).astype(BF16)
        dp_ref[3] = (dyv * b * u * (sig * (1.0 + g * (1.0 - sig)))).astype(BF16)
        dpp = w_ref[2:3, :] * du + w_ref[1:2, :] * _shift_up(du, 1) + w_ref[0:1, :] * _shift_up(du, 2)
        dp_ref[1] = (dpp * v).astype(BF16)
        dp_ref[2] = (dpp * cg).astype(BF16)
        dw_ref[2:3, :] = _colsum(du * p)
        dw_ref[1:2, :] = _colsum(du * _shift_down(p, 1))
        dw_ref[0:1, :] = _colsum(du * _shift_down(p, 2))

    body, extra_specs, extra = _ordered(body, 3, after)
    return pl.pallas_call(
        body, name=name, grid=(e // bc,),
        in_specs=[pl.BlockSpec((4, s, bc), lambda j: (0, 0, j)), pl.BlockSpec((s, bc), lambda j: (0, j)),
                  pl.BlockSpec((3, bc), lambda j: (0, j))] + extra_specs,
        out_specs=[pl.BlockSpec((4, s, bc), lambda j: (0, 0, j)), pl.BlockSpec((3, bc), lambda j: (0, j))],
        out_shape=[jax.ShapeDtypeStruct((4, s, e), BF16), jax.ShapeDtypeStruct((3, e), F32)],
        compiler_params=_params(("parallel",)),
    )(proj, dy, conv_w, *extra)


def _softplus_neg(lam):
    u = jnp.exp(-jnp.abs(lam))
    w = 1.0 + u
    log1p = jnp.where(w == 1.0, u, jnp.log(w) * (u / jnp.where(w == 1.0, 1.0, w - 1.0)))
    return jnp.maximum(-lam, 0.0) + log1p


def _one_minus_exp(z):
    series = -z * (1.0 + z * (0.5 + z * (1.0 / 6.0 + z * (1.0 / 24.0))))
    return jnp.where(z > -0.02, series, 1.0 - jnp.exp(z))


def _scan_in_tiles(a, b, reverse):
    shape = a.shape
    a, b = _tiles(a), _tiles(b)
    rows = lax.broadcasted_iota(jnp.int32, a.shape, 1)
    for step in (1, 2, 4):
        shift = SUBLANES - step if reverse else step
        ok = rows < SUBLANES - step if reverse else rows >= step
        a_s, b_s = pltpu.roll(a, shift, 1), pltpu.roll(b, shift, 1)
        b = jnp.where(ok, a * b_s + b, b)
        a = jnp.where(ok, a * a_s, a)
    return a.reshape(shape), b.reshape(shape)


def _by_rows(fn, arrays, rows=32):
    s = arrays[0].shape[0]
    rows = min(rows, s)
    for t in range(0, s, rows):
        fn(t, *(a[t:t + rows] for a in arrays))


def _scan_carry(a_ref, b_ref, h_ref, reverse):
    s, c = a_ref.shape
    n = s // 8

    def step(i, carry):
        gi = n - 1 - i if reverse else i
        sl = pl.ds(pl.multiple_of(gi * 8, 8), 8)
        h = b_ref[sl, :] + a_ref[sl, :] * carry
        h_ref[sl, :] = h
        return h[0:1, :] if reverse else h[7:8, :]

    lax.fori_loop(0, n, step, jnp.zeros((1, c), F32), unroll=8 if n % 8 == 0 else 1)


def _lru_specs(s, e_half, n_heads):
    hp = e_half // HEAD_DIM
    c = HEAD_DIM
    return dict(
        pair=pl.BlockSpec((2, None, s, c), lambda h: (0, h // hp, 0, h % hp)),
        conv_w=pl.BlockSpec((4, c), lambda h: (0, h)),
        chan=pl.BlockSpec((1, c), lambda h: (0, h)),
        w=pl.BlockSpec((4, None, c // 4, c), lambda h: (0, h, 0, 0)),
        bias=pl.BlockSpec((None, 1, c), lambda h: (h, 0, 0)),
        plane=pl.BlockSpec((s, c), lambda h: (0, h)),
    )


def _lru_gate_inputs(vp, cw_ref, cb_ref, wa_ref, ba_ref, wx_ref, bx_ref):
    c = HEAD_DIM
    taps = [_shift_down(vp, 3 - k) for k in range(4)]
    v = cb_ref[...] + sum(cw_ref[k:k + 1, :] * taps[k] for k in range(4))
    vb = v.astype(BF16)
    wa = wa_ref[...].reshape(c, c)
    wx = wx_ref[...].reshape(c, c)
    zr = jnp.dot(vb, wa, preferred_element_type=F32) + ba_ref[...]
    zi = jnp.dot(vb, wx, preferred_element_type=F32) + bx_ref[...]
    return taps, v, vb, wa, wx, zr, zi


def _lru_fwd(proj, conv_w, conv_b, w_a, b_a, w_x, b_x, lam, name):
    _, _, s, e_half = proj.shape
    n_heads = 2 * e_half // HEAD_DIM
    sp_ = _lru_specs(s, e_half, n_heads)

    def body(pg_ref, cw_ref, cb_ref, wa_ref, ba_ref, wx_ref, bx_ref, lam_ref,
             y_ref, a_ref, hs_ref, sa_ref, sb_ref, sh_ref):
        _, v, _, _, _, zr, zi = _lru_gate_inputs(
            pg_ref[0].astype(F32), cw_ref, cb_ref, wa_ref, ba_ref, wx_ref, bx_ref)
        rate = (-RGLRU_C) * _softplus_neg(lam_ref[...])

        def decay_and_input(t, v_c, zr_c, zi_c):
            la = rate * _sigmoid(zr_c)
            a = jnp.exp(la)
            b = jnp.sqrt(_one_minus_exp(2.0 * la)) * (_sigmoid(zi_c) * v_c)
            a_ref[t:t + a.shape[0]] = a
            sa_ref[t:t + a.shape[0]], sb_ref[t:t + a.shape[0]] = _scan_in_tiles(a, b, reverse=False)

        _by_rows(decay_and_input, [v, zr, zi])
        _scan_carry(sa_ref, sb_ref, sh_ref, reverse=False)

        def gated_output(t, hs_c, g_c):
            g = g_c.astype(F32)
            y_ref[t:t + g.shape[0]] = (hs_c * (g * _sigmoid(g))).astype(BF16)
            hs_ref[t:t + g.shape[0]] = hs_c.astype(BF16)

        _by_rows(gated_output, [sh_ref, pg_ref.at[1]])

    e = 2 * e_half
    return pl.pallas_call(
        body, name=name, grid=(n_heads,),
        in_specs=[sp_["pair"], sp_["conv_w"], sp_["chan"], sp_["w"], sp_["bias"], sp_["w"],
                  sp_["bias"], sp_["chan"]],
        out_specs=[sp_["plane"]] * 3,
        out_shape=[jax.ShapeDtypeStruct((s, e), BF16), jax.ShapeDtypeStruct((s, e), F32),
                   jax.ShapeDtypeStruct((s, e), BF16)],
        scratch_shapes=[pltpu.VMEM((s, HEAD_DIM), F32)] * 3,
        compiler_params=_params(("parallel",)),
    )(proj, conv_w, conv_b, w_a, b_a, w_x, b_x, lam)


def _lru_bwd(proj, dy, saved, conv_w, conv_b, w_a, b_a, w_x, b_x, lam, name):
    _, _, s, e_half = proj.shape
    e = 2 * e_half
    c = HEAD_DIM
    n_heads = e // c
    sp_ = _lru_specs(s, e_half, n_heads)

    def body(pg_ref, dy_ref, a_ref, hs_ref, cw_ref, cb_ref, wa_ref, ba_ref, wx_ref, bx_ref, lam_ref,
             dpg_ref, dwa_ref, dwx_ref, dba_ref, dbx_ref, dlam_ref, dcw_ref, dcb_ref,
             sa_ref, sb_ref, sd_ref, dzr_ref, dzi_ref):
        taps, v, vb, wa, wx, zr, zi = _lru_gate_inputs(
            pg_ref[0].astype(F32), cw_ref, cb_ref, wa_ref, ba_ref, wx_ref, bx_ref)
        lam = lam_ref[...]
        rate = (-RGLRU_C) * _softplus_neg(lam)
        a = a_ref[...]

        def state_gradient_in_tiles(t, a_next, dy_c, g_c, hs_c):
            g, dyv = g_c.astype(F32), dy_c.astype(F32)
            sig = _sigmoid(g)
            rows = slice(t, t + g.shape[0])
            dpg_ref[1, rows] = (dyv * hs_c.astype(F32) * (sig * (1.0 + g * (1.0 - sig)))).astype(BF16)
            sa_ref[rows], sb_ref[rows] = _scan_in_tiles(a_next, dyv * (g * sig), reverse=True)

        _by_rows(state_gradient_in_tiles, [_shift_up(a, 1), dy_ref, pg_ref.at[1], hs_ref])
        _scan_carry(sa_ref, sb_ref, sd_ref, reverse=True)

        sums = []

        def gate_gradients(t, dh, hs_before, a_c, zr_c, zi_c, v_c):
            r, i = _sigmoid(zr_c), _sigmoid(zi_c)
            q = (1.0 - a_c) * (1.0 + a_c)
            inv_nm = lax.rsqrt(q)
            div = dh * (q * inv_nm)
            dla = (dh * hs_before) * a_c - (dh * (i * v_c)) * (a_c * a_c * inv_nm)
            dzr = (dla * rate) * (r * (1.0 - r))
            dzi = (div * v_c) * (i * (1.0 - i))
            rows = slice(t, t + dh.shape[0])
            dzr_ref[rows], dzi_ref[rows] = dzr.astype(BF16), dzi.astype(BF16)
            sa_ref[rows] = div * i
            sums.append((_colsum(dla * r), _colsum(dzr), _colsum(dzi)))

        _by_rows(gate_gradients, [sd_ref, _shift_down(hs_ref[...].astype(F32), 1), a_ref, zr, zi, v])
        dlam_ref[...] = sum(p[0] for p in sums) * ((-RGLRU_C) * (-_sigmoid(-lam)))
        dba_ref[...] = sum(p[1] for p in sums)
        dbx_ref[...] = sum(p[2] for p in sums)
        dzr_b, dzi_b = dzr_ref[...], dzi_ref[...]
        tn = (((0,), (0,)), ((), ()))
        nt = (((1,), (1,)), ((), ()))
        dwa_ref[...] = lax.dot_general(vb, dzr_b, tn, preferred_element_type=F32)
        dwx_ref[...] = lax.dot_general(vb, dzi_b, tn, preferred_element_type=F32)
        dv = (sa_ref[...] + lax.dot_general(dzr_b, wa, nt, preferred_element_type=F32)
              + lax.dot_general(dzi_b, wx, nt, preferred_element_type=F32))
        dcb_ref[...] = _colsum(dv)
        dvp = jnp.zeros_like(dv)
        for k in range(4):
            dvp = dvp + cw_ref[k:k + 1, :] * _shift_up(dv, 3 - k)
            dcw_ref[k:k + 1, :] = _colsum(dv * taps[k])
        dpg_ref[0] = dvp.astype(BF16)

    head_mat = pl.BlockSpec((None, c, c), lambda h: (h, 0, 0))
    outs = pl.pallas_call(
        body, name=name, grid=(n_heads,),
        in_specs=[sp_["pair"]] + [sp_["plane"]] * 3 + [sp_["conv_w"], sp_["chan"], sp_["w"], sp_["bias"],
                                                        sp_["w"], sp_["bias"], sp_["chan"]],
        out_specs=[sp_["pair"], head_mat, head_mat, sp_["bias"], sp_["bias"],
                   sp_["chan"], sp_["conv_w"], sp_["chan"]],
        out_shape=[jax.ShapeDtypeStruct((2, 2, s, e_half), BF16),
                   jax.ShapeDtypeStruct((n_heads, c, c), F32), jax.ShapeDtypeStruct((n_heads, c, c), F32),
                   jax.ShapeDtypeStruct((n_heads, 1, c), F32), jax.ShapeDtypeStruct((n_heads, 1, c), F32),
                   jax.ShapeDtypeStruct((1, e), F32), jax.ShapeDtypeStruct((4, e), F32),
                   jax.ShapeDtypeStruct((1, e), F32)],
        scratch_shapes=[pltpu.VMEM((s, c), F32)] * 3 + [pltpu.VMEM((s, c), BF16)] * 2,
        compiler_params=_params(("parallel",)),
    )(proj, dy, *saved, conv_w, conv_b, w_a, b_a, w_x, b_x, lam)
    return tuple(outs)


def _ada_fwd(c_all, ada_w, name):
    n_l, d, f = ada_w.shape
    bf = _blk(f, 512)

    def body(c_ref, w_ref, o_ref):
        cv = c_ref[...]
        sc = (cv * _sigmoid(cv)).astype(BF16)
        o_ref[...] = jnp.dot(sc, w_ref[...].astype(BF16), preferred_element_type=F32)

    return pl.pallas_call(
        body, name=name, grid=(n_l, f // bf),
        in_specs=[pl.BlockSpec((8, d), lambda l, j: (0, 0)), pl.BlockSpec((None, d, bf), lambda l, j: (l, 0, j))],
        out_specs=pl.BlockSpec((None, 8, bf), lambda l, j: (l, 0, j)),
        out_shape=jax.ShapeDtypeStruct((n_l, 8, f), F32),
        compiler_params=_params(("parallel", "parallel")),
    )(c_all, ada_w)


def _ada_bwd_adamw(c_t, dmod, w, m, v, name, after=None):
    n_l, d, f = w.shape
    bf = _blk(f, 256)

    def body(c_ref, dm_ref, w_ref, m_ref, v_ref, g_ref, d_ref, m2_ref, v2_ref):
        cv = c_ref[...]
        sc = cv * _sigmoid(cv)
        dm = dm_ref[...]
        g = sc[:, 0:1] * dm[0:1, :]
        for b in range(1, 8):
            g = g + sc[:, b:b + 1] * dm[b:b + 1, :]
        g_ref[...] = g
        dl, m2, v2 = _adamw_math(w_ref[...], g, m_ref[...], v_ref[...])
        d_ref[...] = dl
        m2_ref[...] = m2
        v2_ref[...] = v2

    big = pl.BlockSpec((None, d, bf), lambda l, j: (l, 0, j))
    body, extra_specs, extra = _ordered(body, 5, after)
    return pl.pallas_call(
        body, name=name, grid=(n_l, f // bf),
        in_specs=[pl.BlockSpec((d, 8), lambda l, j: (0, 0)), pl.BlockSpec((None, 8, bf), lambda l, j: (l, 0, j)),
                  big, big, big] + extra_specs,
        out_specs=[big] * 4, out_shape=[jax.ShapeDtypeStruct((n_l, d, f), F32)] * 4,
        compiler_params=_params(("parallel", "parallel")),
    )(c_t, dmod, w, m, v, *extra)


def _pack(parts):
    padded, offs, n = [], [], 0
    for p in parts:
        p = p.reshape(-1)
        size = -(-p.shape[0] // PACK) * PACK
        offs.append(n)
        n += size
        padded.append(jnp.pad(p, (0, size - p.shape[0])) if size != p.shape[0] else p)
    return jnp.concatenate(padded), offs, n


def kernel(x, c, norm_g, ada_w, ada_b, sc_w_in, sc_conv_w, sc_w_out, lru_w_in, lru_conv_w, lru_conv_b, lru_w_a, lru_b_a, lru_w_x, lru_b_x, lru_lambda, lru_w_out, final_g, loss_target, m_norm_g, m_ada_w, m_ada_b, m_sc_w_in, m_sc_conv_w, m_sc_w_out, m_lru_w_in, m_lru_conv_w, m_lru_conv_b, m_lru_w_a, m_lru_b_a, m_lru_w_x, m_lru_b_x, m_lru_lambda, m_lru_w_out, m_final_g, v_norm_g, v_ada_w, v_ada_b, v_sc_w_in, v_sc_conv_w, v_sc_w_out, v_lru_w_in, v_lru_conv_w, v_lru_conv_b, v_lru_w_a, v_lru_b_a, v_lru_w_x, v_lru_b_x, v_lru_lambda, v_lru_w_out, v_final_g):
    xi, yi, ci = _pos()
    chip = 2 * xi + yi
    batch = 4 * xi + 2 * yi + ci
    core_op = jnp.reshape(ci, (1,)).astype(jnp.int32)
    chip_op = jnp.reshape(chip, (1,)).astype(jnp.int32)
    where_op = jnp.stack([chip, ci]).astype(jnp.int32)

    x2d, tgt = x[0], loss_target[0]
    s, d = x2d.shape
    es = sc_conv_w.shape[2]
    e = 4 * es
    n_heads = lru_w_a.shape[1]
    hj = lru_b_a.shape[2]
    f = ada_w.shape[2]
    row = lambda t: t.reshape(1, -1)

    small_parts = [c, sc_conv_w, lru_conv_w, lru_conv_b, lru_b_a, lru_b_x, lru_lambda]
    small, offs, n_small = _pack(small_parts)
    got = _allgather8([small.reshape(8, n_small // 8)], "ag_small")[0].reshape(8, n_small)
    c_all = got[:, :d]
    per_chip = got[0::2]

    def chip_part(k, shape):
        size = 1
        for dim in shape:
            size *= dim
        return per_chip[:, offs[k]:offs[k] + size].reshape((4,) + shape)

    conv_w0 = jnp.transpose(chip_part(1, (3, es)), (1, 0, 2)).reshape(3, e)
    conv_w1 = jnp.transpose(chip_part(2, (4, es)), (1, 0, 2)).reshape(4, e)
    conv_b1 = chip_part(3, (es,)).reshape(1, e)
    b_a = jnp.transpose(chip_part(4, (n_heads, hj)), (1, 0, 2)).reshape(n_heads, 1, 4 * hj)
    b_x = jnp.transpose(chip_part(5, (n_heads, hj)), (1, 0, 2)).reshape(n_heads, 1, 4 * hj)
    lam = chip_part(6, (es,)).reshape(1, e)

    mod_nb = _ada_fwd(c_all, ada_w, "ada_fwd")
    mods = _allgather8([mod_nb.reshape(16, f)], "ag_mod")[0].reshape(8, 2, 8, f)[0::2]
    mine = lax.dynamic_index_in_dim(mods, batch, axis=2, keepdims=False)
    mod = jnp.transpose(mine, (1, 0, 2)).reshape(2, 4 * f) + ada_b
    shift = [row(mod[l, :d]) for l in range(2)]
    scale = [row(mod[l, d:2 * d]) for l in range(2)]
    gate = [row(mod[l, 2 * d:]) for l in range(2)]
    ng = [row(norm_g[l]) for l in range(2)]

    shards = [sc_w_in[0], sc_w_out[0], lru_w_in[0], lru_w_a[0].reshape(n_heads * hj, HEAD_DIM),
              lru_w_x[0].reshape(n_heads * hj, HEAD_DIM), lru_w_out[0]]
    names = ["sc_w_in", "sc_w_out", "lru_w_in", "lru_w_a", "lru_w_x", "lru_w_out"]
    slots = [_cast_into_slot(chip_op, w, "cast_" + nm) for w, nm in zip(shards, names)]
    send_a, recv_a, buf, started = _first_start(slots[0], mod, "ag_first_start")
    h0 = _norm_mod_fwd(x2d, ng[0], scale[0], shift[0], "norm0", after=started)
    planes = [jnp.reshape(2 * px + py, (1,)).astype(jnp.int32) for px, py in _other_chips(xi, yi)]
    proj0 = _mm_plane(h0, buf.reshape(4, d, e), chip_op, "sc_in_own")
    relays, passed, relayed = [], [], [proj0] + slots[1:]
    for j in range(2):
        send_b, recv_b, send_c, recv_c, buf, token = _first_relay(send_a, recv_a, buf, j, relayed,
                                                                  "ag_first_relay_%d" % j)
        relays.append((send_c, recv_c))
        passed.append((send_b, recv_b))
        relayed = [token]
    rest_flight, relayed = _gather_start([[slots[1]], [slots[2]], slots[3:5], [slots[5]]], token, "ag_start")
    for j in range(2):
        buf = _first_relay_done(*passed[j], buf, j, relayed, "ag_first_relay_done_%d" % j)
        proj0 = _mm_plane(h0, buf.reshape(4, d, e), planes[j], "sc_in_%d" % j, planes_so_far=proj0)
        relayed = proj0
    send_b, recv_b, buf, relayed = _first_diagonal(relays, buf, proj0, "ag_first_diagonal")
    buf = _first_diagonal_done(send_b, recv_b, relays, buf, relayed, "ag_first_diagonal_done")
    proj0 = _mm_plane(h0, buf.reshape(4, d, e), planes[2], "sc_in_2", planes_so_far=proj0)
    w_in0 = buf.reshape(4, d, e)
    in_flight = [None] + rest_flight

    def arrived(g, after, tag):
        send1, recv1, bufs = in_flight[g]
        send2, recv2, bufs, passed = _gather_forward(send1, recv1, bufs, after, "ag_forward_" + tag)
        return _gather_finish(send2, recv2, bufs, passed, "ag_finish_" + tag)

    w_out0 = arrived(1, proj0, "sc_w_out")[0].reshape(1, e, d)
    y0 = _sc_fwd(proj0, conv_w0, "sc_mix")
    o0 = _mm(y0[None], w_out0, "nn", F32, "sc_out")[0]
    x1, h1 = _norm_mod_fwd(x2d, ng[1], scale[1], shift[1], "norm1", o=o0, gate=gate[0])
    send1, recv1, bufs = in_flight[2]
    proj1 = _mm_plane(h1, bufs[0].reshape(4, d, e // 2), chip_op, "lru_in_own")
    for j in range(3):
        send2, recv2, bufs, passed_on = _gather_forward(send1, recv1, bufs, proj1, "ag_forward_lru_w_in_%d" % j,
                                                        sources=(j,))
        bufs = _gather_finish(send2, recv2, bufs, passed_on, "ag_finish_lru_w_in_%d" % j, sources=(j,))
        proj1 = _mm_plane(h1, bufs[0].reshape(4, d, e // 2), planes[j], "lru_in_%d" % j, planes_so_far=proj1)
    w_in1 = bufs[0].reshape(4, d, e // 2)
    gate_ws = arrived(3, proj1, "lru_gates")
    w_a = gate_ws[0].reshape(4, n_heads, hj, HEAD_DIM)
    w_x = gate_ws[1].reshape(4, n_heads, hj, HEAD_DIM)
    pairs1 = proj1.reshape(2, 2, s, e // 2)
    y1, *lru_saved = _lru_fwd(pairs1, conv_w1, conv_b1, w_a, b_a, w_x, b_x, lam, "lru_mix")
    w_out1 = arrived(4, y1, "lru_w_out")[0].reshape(1, e, d)
    o1 = _mm(y1[None], w_out1, "nn", F32, "lru_out")[0]
    dx2, do1, acc_f = _final_loss(x1, o1, gate[1], row(final_g), tgt, "final_loss")

    def reduce_stage1(tag, grads):
        lands = [lax.empty((4, g.shape[1] // 2, g.shape[2]), F32) for g in grads]
        return _exchange_start("rs_sibling_start_" + tag, grads, lands, len(grads),
                               _plan_other_half_to_sibling, None)

    def reduce_stage2(tag, stage1, nms, after):
        send, recv, grads, lands, _ = stage1
        grads, lands = _exchange_wait("rs_sibling_wait_" + tag, send, recv, grads, lands,
                                      _plan_other_half_to_sibling, after)
        parts = [_sum_own_and_sibling(core_op, g, r1, "rs_sum1_" + nm) for g, r1, nm in zip(grads, lands, nms)]
        lands = [lax.empty((3,) + p.shape[1:], BF16) for p in parts]
        return _exchange_start("rs_chips_start_" + tag, parts, lands, 3 * len(parts),
                               _plan_partials_to_chips, None)

    def reduce_stage3(tag, stage2, nms, after):
        send, recv, parts, lands, _ = stage2
        parts, lands = _exchange_wait("rs_chips_wait_" + tag, send, recv, parts, lands,
                                      _plan_partials_to_chips, after)
        halves = [_sum_chips(where_op, p, r2, "rs_sum2_" + nm) for p, r2, nm in zip(parts, lands, nms)]
        return _exchange_start("rs_share_start_" + tag, [], halves, len(halves), _plan_share_half, None)

    def reduce_done(tag, stage3, after):
        send, recv, _, fulls, _ = stage3
        return _exchange_wait("rs_share_wait_" + tag, send, recv, [], fulls, _plan_share_half, after)[1]

    def head_major_to_chip_major(t):
        return jnp.transpose(t.reshape(n_heads, 4, hj, HEAD_DIM), (1, 0, 2, 3)).reshape(4, n_heads * hj, HEAD_DIM)

    big_state = dict(zip(names, zip(shards, [m_sc_w_in, m_sc_w_out, m_lru_w_in, m_lru_w_a, m_lru_w_x, m_lru_w_out],
                                    [v_sc_w_in, v_sc_w_out, v_lru_w_in, v_lru_w_a, v_lru_w_x, v_lru_w_out],
                                    [sc_w_in, sc_w_out, lru_w_in, lru_w_a, lru_w_x, lru_w_out])))
    big = {}

    def update(nms, fulls, after):
        for nm, g2 in zip(nms, fulls):
            w2, m4, v4, w4 = big_state[nm]
            outs = _adamw(w2, g2, m4.reshape(w2.shape), v4.reshape(w2.shape), "adamw_" + nm, after=after)
            big[nm] = tuple(t.reshape(w4.shape) for t in outs)
            after = outs[1]
        return after

    g_w_out1 = _mm(y1[None], do1[None], "tn", F32, "lru_out_dw", bm=512, bn=2048)
    dy1 = _mm(do1[None], w_out1, "nt", BF16, "lru_out_dx")[0]
    dpairs1, g_wa, g_wx, g_ba, g_bx, g_lam, g_cw1, g_cb1 = _lru_bwd(
        pairs1, dy1, lru_saved, conv_w1, conv_b1, w_a, b_a, w_x, b_x, lam, "lru_mix_bwd")
    dproj1 = dpairs1.reshape(4, s, e // 2)
    g_w_in1 = _mm(h1[None], dproj1, "tn", F32, "lru_in_dw", bm=512, bn=2048)
    lru_names = ["lru_w_out", "lru_w_a", "lru_w_x", "lru_w_in"]
    lru_rs = reduce_stage1("lru", [g_w_out1.reshape(4, es, d), head_major_to_chip_major(g_wa),
                                   head_major_to_chip_major(g_wx), g_w_in1])
    dh1 = _mm(dproj1, w_in1, "nt", F32, "lru_in_dx", after=lru_rs[4])[0]
    lru_rs = reduce_stage2("lru", lru_rs, lru_names, dh1)
    dx1, do0, acc1 = _norm_mod_bwd(dh1, x1, ng[1], scale[1], dx2, "norm1_bwd", o_prev=o0, gate_prev=gate[0],
                                   after=lru_rs[4])

    g_w_out0 = _mm(y0[None], do0[None], "tn", F32, "sc_out_dw", bm=512, bn=2048)
    out_rs = reduce_stage1("sc_out", [g_w_out0.reshape(4, es, d)])
    dy0 = _mm(do0[None], w_out0, "nt", BF16, "sc_out_dx", after=out_rs[4])[0]
    out_rs = reduce_stage2("sc_out", out_rs, ["sc_w_out"], dy0)
    dproj0, g_cw0 = _sc_bwd(proj0, dy0, conv_w0, "sc_mix_bwd", after=out_rs[4])
    stage1, stage2, after = [], [], None
    for k in range(IN_PIECES):
        g_piece = _mm(h0[None], dproj0, "tn", F32, "sc_in_dw_%d" % k, bm=512, bn=e // IN_PIECES,
                      col_blocks=(k, 1), after=after)
        stage1.append(reduce_stage1("sc_in_%d" % k, [g_piece]))
        after = stage1[k][4]
        if k:
            stage2.append(reduce_stage2("sc_in_%d" % (k - 1), stage1[k - 1], ["sc_w_in_%d" % (k - 1)], after))
            after = stage2[k - 1][4]
    lru_rs = reduce_stage3("lru", lru_rs, lru_names, after)
    k = IN_PIECES - 1
    stage2.append(reduce_stage2("sc_in_%d" % k, stage1[k], ["sc_w_in_%d" % k], lru_rs[4]))
    dh0 = _mm(dproj0, w_in0, "nt", F32, "sc_in_dx", after=stage2[k][4])[0]
    grad_x, acc0 = _norm_mod_bwd(dh0, x2d, ng[0], scale[0], dx1, "norm0_bwd")
    out_rs = reduce_stage3("sc_out", out_rs, ["sc_w_out"], acc0)
    last = update(lru_names, reduce_done("lru", lru_rs, out_rs[4]), None)
    last = update(["sc_w_out"], reduce_done("sc_out", out_rs, last), None)
    g_half = None
    for k, (send, recv, parts, lands, _) in enumerate(stage2):
        parts, lands = _exchange_wait("rs_chips_wait_sc_in_%d" % k, send, recv, parts, lands,
                                      _plan_partials_to_chips, last)
        g_half = _sum_chips(where_op, parts[0], lands[0], "rs_sum2_sc_w_in_%d" % k, piece=(k, IN_PIECES),
                            so_far=g_half)
    in_rs = _exchange_start("rs_share_start_sc_in", [], [g_half], 1, _plan_share_half, None)

    dmod = jnp.stack([jnp.concatenate([acc0[1], acc0[0], acc1[3]]),
                      jnp.concatenate([acc1[1], acc1[0], acc_f[1]])])
    part_list = [jnp.stack([acc0[2], acc1[2]]), acc_f[0], acc_f[2, :1], g_cw0, g_cw1, g_cb1, g_ba, g_bx,
                 g_lam, dmod]
    partials, poffs, n_part = _pack(part_list)
    every = _allgather8([partials.reshape(8, n_part // 8)], "ag_partials", after=in_rs[4])[0].reshape(8, n_part)
    total = _sum_rows8(every, "sum_partials")[0]

    def tot(k, shape):
        size = 1
        for dim in shape:
            size *= dim
        return total[poffs[k]:poffs[k] + size].reshape(shape)

    def my_cols(t, width):
        return lax.dynamic_slice_in_dim(t, chip * width, width, axis=t.ndim - 1)

    loss = tot(2, (1,))[0]
    g_norm_g, g_final_g, g_ada_b = tot(0, (2, d)), tot(1, (d,)), tot(9, (2, 3 * d))
    g_sc_conv_w = my_cols(tot(3, (3, e)), es)[None]
    g_lru_conv_w = my_cols(tot(4, (4, e)), es)[None]
    g_lru_conv_b = my_cols(tot(5, (1, e)), es)
    g_lru_b_a = my_cols(tot(6, (n_heads, 4 * hj)), hj)[None]
    g_lru_b_x = my_cols(tot(7, (n_heads, 4 * hj)), hj)[None]
    g_lru_lambda = my_cols(tot(8, (1, e)), es)

    dmod_all = every[:, poffs[9]:poffs[9] + 6 * d].reshape(8, 2, 3 * d)
    dmod_mine = jnp.transpose(my_cols(dmod_all, f), (1, 0, 2))
    ada = _ada_bwd_adamw(jnp.transpose(c_all), dmod_mine, ada_w, m_ada_w, v_ada_w, "ada_bwd_adamw", after=total)

    small_names = ["norm_g", "ada_b", "final_g", "sc_conv_w", "lru_conv_w", "lru_conv_b", "lru_b_a",
                   "lru_b_x", "lru_lambda"]
    small_w = [norm_g, ada_b, final_g, sc_conv_w, lru_conv_w, lru_conv_b, lru_b_a, lru_b_x, lru_lambda]
    small_g = [g_norm_g, g_ada_b, g_final_g, g_sc_conv_w, g_lru_conv_w, g_lru_conv_b, g_lru_b_a,
               g_lru_b_x, g_lru_lambda]
    small_m = [m_norm_g, m_ada_b, m_final_g, m_sc_conv_w, m_lru_conv_w, m_lru_conv_b, m_lru_b_a,
               m_lru_b_x, m_lru_lambda]
    small_v = [v_norm_g, v_ada_b, v_final_g, v_sc_conv_w, v_lru_conv_w, v_lru_conv_b, v_lru_b_a,
               v_lru_b_x, v_lru_lambda]
    pw, soffs, n_s = _pack(small_w)
    pg, pm, pv = _pack(small_g)[0], _pack(small_m)[0], _pack(small_v)[0]
    shape2 = (n_s // PACK, PACK)
    _, pd, pm2, pv2 = _adamw(pw.reshape(shape2), pg.reshape(shape2), pm.reshape(shape2), pv.reshape(shape2),
                             "adamw_small", after=ada[0])
    update(["sc_w_in"], reduce_done("sc_in", in_rs, pd), None)
    small = {}
    for k, (nm, w_) in enumerate(zip(small_names, small_w)):
        take = lambda t: t.reshape(-1)[soffs[k]:soffs[k] + w_.size].reshape(w_.shape)
        small[nm] = (small_g[k].reshape(w_.shape), take(pd), take(pm2), take(pv2))

    results = dict(small)
    results.update(big)
    results["ada_w"] = tuple(ada)
    order = ["norm_g", "ada_w", "ada_b", "sc_w_in", "sc_conv_w", "sc_w_out", "lru_w_in", "lru_conv_w",
             "lru_conv_b", "lru_w_a", "lru_b_a", "lru_w_x", "lru_b_x", "lru_lambda", "lru_w_out", "final_g"]
    out = [loss, grad_x[None]]
    for kind in range(4):
        out += [results[nm][kind] for nm in order]
    return tuple(out)
```

```python
import jax
import jax.numpy as jnp
from jax import lax
from jax.experimental import pallas as pl
from jax.experimental.pallas import tpu as pltpu

F32 = jnp.float32
BF16 = jnp.bfloat16
MESH = pl.DeviceIdType.MESH
ANY = pl.BlockSpec(memory_space=pl.ANY)

RMS_EPS = 1e-6
RGLRU_C = 8.0
HEAD_DIM = 256
ADAM_LR = 0.001
ADAM_B1 = 0.9
ADAM_B2 = 0.999
ADAM_EPS = 1e-08
ADAM_WD = 0.01
ADAM_STEP = 10
V7X_VMEM_LIMIT = 56 * 1024 * 1024
IN_PIECES = 2
LANES = 128
SUBLANES = 8
PACK = SUBLANES * LANES


def _blk(dim, pref, unit=LANES):
    if dim <= pref:
        return dim
    b = (pref // unit) * unit
    while b > unit and dim % b:
        b -= unit
    assert dim % b == 0, (dim, pref, unit)
    return b


def _params(sem=None):
    return pltpu.CompilerParams(dimension_semantics=sem, vmem_limit_bytes=V7X_VMEM_LIMIT)


def _ordered(body, n_in, after):
    if after is None:
        return body, [], []

    def ordered_body(*refs):
        return body(*refs[:n_in], *refs[n_in + 1:])

    return ordered_body, [ANY], [after]


def _pos():
    return lax.axis_index("x"), lax.axis_index("y"), lax.axis_index("c")


def _other_chips(x, y):
    return [(1 - x, y), (x, 1 - y), (1 - x, 1 - y)]


def _allgather8(arrs, name, after=None):
    n_t = len(arrs)
    ms = [a.shape[0] for a in arrs]

    def gather(*refs):
        ins, outs = refs[:n_t], refs[n_t:2 * n_t]
        send_sems, recv_sems, local_sems = refs[2 * n_t:]
        x, y, c = _pos()
        me, sibling = (x, y, c), (x, y, 1 - c)
        chips = _other_chips(x, y)

        def rows(t, px, py, pc):
            return outs[t].at[pl.ds((4 * px + 2 * py + pc) * ms[t], ms[t])]

        def copy(t, k, block, to, src=None):
            return pltpu.make_async_remote_copy(
                src_ref=rows(t, *block) if src is None else src, dst_ref=rows(t, *block),
                send_sem=send_sems.at[7 * t + k], recv_sem=recv_sems.at[7 * t + k],
                device_id=to, device_id_type=MESH)

        mine, first, passed = [], [], []
        for t in range(n_t):
            src = ins[t]
            cp = pltpu.make_async_copy(src, rows(t, *me), local_sems.at[t])
            cp.start()
            mine.append(cp)
            sends = [copy(t, 0, me, sibling, src=src)]
            sends += [copy(t, 1 + j, me, (*chip, c), src=src) for j, chip in enumerate(chips)]
            for cp in sends:
                cp.start()
            first += sends
        for t in range(n_t):
            for j, chip in enumerate(chips):
                copy(t, 1 + j, (*chip, c), me).wait_recv()
                cp = copy(t, 4 + j, (*chip, c), sibling)
                cp.start()
                passed.append(cp)
        for t in range(n_t):
            copy(t, 0, sibling, me).wait_recv()
            for j, chip in enumerate(chips):
                copy(t, 4 + j, (*chip, 1 - c), me).wait_recv()
        for cp in first + passed:
            cp.wait_send()
        for cp in mine:
            cp.wait()

    body, extra_specs, extra = _ordered(gather, n_t, after)
    return pl.pallas_call(
        body, name=name,
        out_shape=[jax.ShapeDtypeStruct((8 * m, a.shape[1]), a.dtype) for m, a in zip(ms, arrs)],
        in_specs=[ANY] * n_t + extra_specs, out_specs=[ANY] * n_t,
        scratch_shapes=[pltpu.SemaphoreType.DMA((7 * n_t,)), pltpu.SemaphoreType.DMA((7 * n_t,)),
                        pltpu.SemaphoreType.DMA((n_t,))],
    )(*arrs, *extra)


HBM = pl.BlockSpec(memory_space=pltpu.HBM)
SEM = pl.BlockSpec(memory_space=pltpu.SEMAPHORE)
TOKEN = pl.BlockSpec(memory_space=pltpu.VMEM)
IN_FLIGHT = pltpu.CompilerParams(has_side_effects=pltpu.SideEffectType.DATAFLOW_SIDE_EFFECTING)


def _in_hbm(arrs):
    return [pltpu.with_memory_space_constraint(a, pltpu.HBM) for a in arrs]


def _shard_rows(ref, h, px, py, pc):
    return ref.at[pl.ds((4 * px + 2 * py + pc) * h, h)]


def _gather_start(groups, after, name):
    bufs = [b for grp in groups for b in grp]
    n_t, n_g = len(bufs), len(groups)

    def body(*refs):
        sems, thru = refs[n_t + 1:n_t + 1 + 2 * n_g], refs[n_t + 1 + 2 * n_g:2 * n_t + 1 + 2 * n_g]
        token = refs[-1]
        x, y, c = _pos()
        t = 0
        for g, grp in enumerate(groups):
            for i in range(len(grp)):
                h = bufs[t].shape[0] // 8
                rows = _shard_rows(thru[t], h, x, y, c)
                for j, chip in enumerate(_other_chips(x, y)):
                    pltpu.make_async_remote_copy(
                        src_ref=rows, dst_ref=rows, send_sem=sems[2 * g].at[3 * i + j],
                        recv_sem=sems[2 * g + 1].at[3 * i + j], device_id=(*chip, c),
                        device_id_type=MESH).start()
                t += 1
        token[...] = jnp.zeros_like(token)

    sem_shapes = []
    for grp in groups:
        sem_shapes += [pltpu.SemaphoreType.DMA((3 * len(grp),))] * 2
    out = pl.pallas_call(
        body, name=name,
        out_shape=sem_shapes + [pltpu.HBM(b.shape, b.dtype) for b in bufs] + [jax.ShapeDtypeStruct((8, LANES), F32)],
        in_specs=[HBM] * n_t + [ANY], out_specs=[SEM] * (2 * n_g) + [HBM] * n_t + [TOKEN],
        input_output_aliases={t: 2 * n_g + t for t in range(n_t)},
        compiler_params=IN_FLIGHT,
    )(*_in_hbm(bufs), after)
    sems, thru, token = out[:2 * n_g], out[2 * n_g:2 * n_g + n_t], out[-1]
    per_group, t = [], 0
    for g, grp in enumerate(groups):
        per_group.append((sems[2 * g], sems[2 * g + 1], thru[t:t + len(grp)]))
        t += len(grp)
    return per_group, token


def _gather_forward(send_sems, recv_sems, bufs, after, name, sources=(0, 1, 2)):
    n_t = len(bufs)

    def body(*refs):
        ins = refs[:n_t]
        send1, recv1 = refs[n_t], refs[n_t + 1]
        send2, recv2 = refs[n_t + 3], refs[n_t + 4]
        token = refs[-1]
        x, y, c = _pos()
        chips = _other_chips(x, y)
        for t in range(n_t):
            h = bufs[t].shape[0] // 8
            mine = _shard_rows(ins[t], h, x, y, c)
            for j in sources:
                chip = chips[j]
                landed = _shard_rows(ins[t], h, *chip, c)
                pltpu.make_async_remote_copy(
                    src_ref=mine, dst_ref=landed, send_sem=send1.at[3 * t + j], recv_sem=recv1.at[3 * t + j],
                    device_id=(*chip, c), device_id_type=MESH).wait_recv()
                pltpu.make_async_remote_copy(
                    src_ref=landed, dst_ref=landed, send_sem=send2.at[3 * t + j], recv_sem=recv2.at[3 * t + j],
                    device_id=(x, y, 1 - c), device_id_type=MESH).start()
        for t in range(n_t):
            h = bufs[t].shape[0] // 8
            mine = _shard_rows(ins[t], h, x, y, c)
            for j in sources:
                pltpu.make_async_remote_copy(
                    src_ref=mine, dst_ref=mine, send_sem=send1.at[3 * t + j], recv_sem=recv1.at[3 * t + j],
                    device_id=(*chips[j], c), device_id_type=MESH).wait_send()
        token[...] = jnp.zeros_like(token)

    out = pl.pallas_call(
        body, name=name,
        out_shape=[pltpu.SemaphoreType.DMA((3 * n_t,))] * 2 + [pltpu.HBM(b.shape, b.dtype) for b in bufs]
        + [jax.ShapeDtypeStruct((8, LANES), F32)],
        in_specs=[HBM] * n_t + [SEM, SEM, ANY], out_specs=[SEM, SEM] + [HBM] * n_t + [TOKEN],
        input_output_aliases={t: 2 + t for t in range(n_t)},
        compiler_params=IN_FLIGHT,
    )(*bufs, send_sems, recv_sems, after)
    return out[0], out[1], out[2:2 + n_t], out[-1]


def _gather_finish(send_sems, recv_sems, bufs, after, name, sources=(0, 1, 2)):
    n_t = len(bufs)

    def body(*refs):
        ins = refs[:n_t]
        send2, recv2 = refs[n_t], refs[n_t + 1]
        x, y, c = _pos()
        chips = _other_chips(x, y)
        for t in range(n_t):
            h = bufs[t].shape[0] // 8
            for j in sources:
                chip = chips[j]
                sent = _shard_rows(ins[t], h, *chip, c)
                got = _shard_rows(ins[t], h, *chip, 1 - c)
                cp = pltpu.make_async_remote_copy(
                    src_ref=sent, dst_ref=got, send_sem=send2.at[3 * t + j], recv_sem=recv2.at[3 * t + j],
                    device_id=(x, y, 1 - c), device_id_type=MESH)
                cp.wait_send()
                cp.wait_recv()

    return pl.pallas_call(
        body, name=name, out_shape=[pltpu.HBM(b.shape, b.dtype) for b in bufs],
        in_specs=[HBM] * n_t + [SEM, SEM, ANY], out_specs=[HBM] * n_t,
        input_output_aliases={t: t for t in range(n_t)},
        compiler_params=IN_FLIGHT,
    )(*bufs, send_sems, recv_sems, after)


def _part_rows(ref, h, px, py, pc, which):
    return ref.at[pl.ds((4 * px + 2 * py + pc) * h + which * (h // 2), h // 2)]


def _remote(src, dst, send_sem, recv_sem, device):
    return pltpu.make_async_remote_copy(src_ref=src, dst_ref=dst, send_sem=send_sem, recv_sem=recv_sem,
                                        device_id=device, device_id_type=MESH)


def _first_start(buf, after, name):
    h = buf.shape[0] // 8

    def body(buf_in, after_ref, send, recv, thru, token):
        x, y, c = _pos()
        rows = _shard_rows(thru, h, x, y, c)
        for j, chip in enumerate(_other_chips(x, y)[:2]):
            _remote(rows, rows, send.at[j], recv.at[j], (*chip, c)).start()
        token[...] = jnp.zeros_like(token)

    return pl.pallas_call(
        body, name=name,
        out_shape=[pltpu.SemaphoreType.DMA((2,))] * 2 + [pltpu.HBM(buf.shape, buf.dtype),
                                                         jax.ShapeDtypeStruct((8, LANES), F32)],
        in_specs=[HBM, ANY], out_specs=[SEM, SEM, HBM, TOKEN], input_output_aliases={0: 2},
        compiler_params=IN_FLIGHT,
    )(*_in_hbm([buf]), after)


def _first_relay(send_a, recv_a, buf, j, after, name):
    h = buf.shape[0] // 8

    def body(buf_in, send_a, recv_a, *rest):
        send_b, recv_b, send_c, recv_c, _, token = rest[len(after):]
        x, y, c = _pos()
        chips = _other_chips(x, y)
        nbr, other = chips[j], chips[1 - j]
        mine, landed = _shard_rows(buf_in, h, x, y, c), _shard_rows(buf_in, h, *nbr, c)
        _remote(mine, landed, send_a.at[j], recv_a.at[j], (*nbr, c)).wait_recv()
        _remote(landed, landed, send_b.at[0], recv_b.at[0], (x, y, 1 - c)).start()
        part = _part_rows(buf_in, h, *nbr, c, j)
        _remote(part, part, send_c.at[0], recv_c.at[0], (*other, c)).start()
        _remote(mine, mine, send_a.at[j], recv_a.at[j], (*nbr, c)).wait_send()
        token[...] = jnp.zeros_like(token)

    out = pl.pallas_call(
        body, name=name,
        out_shape=[pltpu.SemaphoreType.DMA((1,))] * 4 + [pltpu.HBM(buf.shape, buf.dtype),
                                                         jax.ShapeDtypeStruct((8, LANES), F32)],
        in_specs=[HBM, SEM, SEM] + [ANY] * len(after), out_specs=[SEM] * 4 + [HBM, TOKEN],
        input_output_aliases={0: 4}, compiler_params=IN_FLIGHT,
    )(buf, send_a, recv_a, *after)
    return tuple(out)


def _first_relay_done(send_b, recv_b, buf, j, after, name):
    h = buf.shape[0] // 8

    def body(buf_in, send_b, recv_b, after_ref, thru):
        x, y, c = _pos()
        nbr = _other_chips(x, y)[j]
        cp = _remote(_shard_rows(buf_in, h, *nbr, c), _shard_rows(buf_in, h, *nbr, 1 - c),
                     send_b.at[0], recv_b.at[0], (x, y, 1 - c))
        cp.wait_send()
        cp.wait_recv()

    return pl.pallas_call(
        body, name=name, out_shape=pltpu.HBM(buf.shape, buf.dtype),
        in_specs=[HBM, SEM, SEM, ANY], out_specs=HBM, input_output_aliases={0: 0},
        compiler_params=IN_FLIGHT,
    )(buf, send_b, recv_b, after)


def _first_diagonal(relays, buf, after, name):
    h = buf.shape[0] // 8

    def body(buf_in, send_c0, recv_c0, send_c1, recv_c1, after_ref, send_b, recv_b, thru, token):
        x, y, c = _pos()
        chips = _other_chips(x, y)
        diag = chips[2]
        for j, (send_c, recv_c) in enumerate(((send_c0, recv_c0), (send_c1, recv_c1))):
            part = _part_rows(buf_in, h, *diag, c, j)
            _remote(part, part, send_c.at[0], recv_c.at[0], (*chips[1 - j], c)).wait_recv()
        whole = _shard_rows(buf_in, h, *diag, c)
        _remote(whole, whole, send_b.at[0], recv_b.at[0], (x, y, 1 - c)).start()
        token[...] = jnp.zeros_like(token)

    out = pl.pallas_call(
        body, name=name,
        out_shape=[pltpu.SemaphoreType.DMA((1,))] * 2 + [pltpu.HBM(buf.shape, buf.dtype),
                                                         jax.ShapeDtypeStruct((8, LANES), F32)],
        in_specs=[HBM] + [SEM] * 4 + [ANY], out_specs=[SEM, SEM, HBM, TOKEN], input_output_aliases={0: 2},
        compiler_params=IN_FLIGHT,
    )(buf, relays[0][0], relays[0][1], relays[1][0], relays[1][1], after)
    return tuple(out)


def _first_diagonal_done(send_b, recv_b, relays, buf, after, name):
    h = buf.shape[0] // 8

    def body(buf_in, send_b, recv_b, send_c0, recv_c0, send_c1, recv_c1, after_ref, thru):
        x, y, c = _pos()
        chips = _other_chips(x, y)
        diag = chips[2]
        cp = _remote(_shard_rows(buf_in, h, *diag, c), _shard_rows(buf_in, h, *diag, 1 - c),
                     send_b.at[0], recv_b.at[0], (x, y, 1 - c))
        cp.wait_send()
        cp.wait_recv()
        for j, (send_c, recv_c) in enumerate(((send_c0, recv_c0), (send_c1, recv_c1))):
            part = _part_rows(buf_in, h, *chips[j], c, j)
            _remote(part, part, send_c.at[0], recv_c.at[0], (*chips[1 - j], c)).wait_send()

    return pl.pallas_call(
        body, name=name, out_shape=pltpu.HBM(buf.shape, buf.dtype),
        in_specs=[HBM] + [SEM] * 6 + [ANY], out_specs=HBM, input_output_aliases={0: 0},
        compiler_params=IN_FLIGHT,
    )(buf, send_b, recv_b, relays[0][0], relays[0][1], relays[1][0], relays[1][1], after)


def _exchange_start(name, srcs, lands, n_copies, plan, after):
    ns, nl = len(srcs), len(lands)
    extra = [] if after is None else [after]

    def body(*refs):
        base = ns + nl + len(extra)
        send_sems, recv_sems = refs[base], refs[base + 1]
        src_refs, land_refs = refs[base + 2:base + 2 + ns], refs[base + 2 + ns:base + 2 + ns + nl]
        token = refs[-1]
        x, y, c = _pos()
        copies = plan(src_refs, land_refs, x, y, c)
        assert len(copies) == n_copies
        for k, (src, dst, dev) in enumerate(copies):
            pltpu.make_async_remote_copy(
                src_ref=src, dst_ref=dst, send_sem=send_sems.at[k], recv_sem=recv_sems.at[k],
                device_id=dev, device_id_type=MESH).start()
        token[...] = jnp.zeros_like(token)

    out = pl.pallas_call(
        body, name=name,
        out_shape=[pltpu.SemaphoreType.DMA((n_copies,))] * 2
        + [pltpu.HBM(a.shape, a.dtype) for a in list(srcs) + list(lands)] + [jax.ShapeDtypeStruct((8, LANES), F32)],
        in_specs=[HBM] * (ns + nl) + [ANY] * len(extra), out_specs=[SEM, SEM] + [HBM] * (ns + nl) + [TOKEN],
        input_output_aliases={i: 2 + i for i in range(ns + nl)},
        compiler_params=IN_FLIGHT,
    )(*_in_hbm(list(srcs) + list(lands)), *extra)
    return out[0], out[1], out[2:2 + ns], out[2 + ns:2 + ns + nl], out[-1]


def _exchange_wait(name, send_sems, recv_sems, srcs, lands, plan, after):
    ns, nl = len(srcs), len(lands)

    def body(*refs):
        src_refs, land_refs = refs[:ns], refs[ns:ns + nl]
        send, recv = refs[ns + nl], refs[ns + nl + 1]
        x, y, c = _pos()
        for k, (src, dst, dev) in enumerate(plan(src_refs, land_refs, x, y, c)):
            cp = pltpu.make_async_remote_copy(
                src_ref=src, dst_ref=dst, send_sem=send.at[k], recv_sem=recv.at[k],
                device_id=dev, device_id_type=MESH)
            cp.wait_send()
            cp.wait_recv()

    out = pl.pallas_call(
        body, name=name, out_shape=[pltpu.HBM(a.shape, a.dtype) for a in list(srcs) + list(lands)],
        in_specs=[HBM] * (ns + nl) + [SEM, SEM, ANY], out_specs=[HBM] * (ns + nl),
        input_output_aliases={i: i for i in range(ns + nl)},
        compiler_params=IN_FLIGHT,
    )(*srcs, *lands, send_sems, recv_sems, after)
    return out[:ns], out[ns:]


def _plan_other_half_to_sibling(src_refs, land_refs, x, y, c):
    out = []
    for g_ref, r_ref in zip(src_refs, land_refs):
        h = g_ref.shape[1] // 2
        out.append((g_ref.at[:, pl.ds((1 - c) * h, h), :], r_ref, (x, y, 1 - c)))
    return out


def _plan_partials_to_chips(src_refs, land_refs, x, y, c):
    out = []
    for p_ref, r_ref in zip(src_refs, land_refs):
        for j, (px, py) in enumerate(_other_chips(x, y)):
            out.append((p_ref.at[2 * px + py], r_ref.at[j], (px, py, c)))
    return out


def _plan_share_half(src_refs, land_refs, x, y, c):
    out = []
    for f_ref in land_refs:
        h = f_ref.shape[0] // 2
        rows = f_ref.at[pl.ds(c * h, h)]
        out.append((rows, rows, (x, y, 1 - c)))
    return out


def _cast_into_slot(chip, w, name):
    r, c = w.shape
    br, bc = _blk(r, 512, 8), _blk(c, 2048)
    nb = r // br

    def body(chip_ref, w_ref, o_ref):
        o_ref[...] = w_ref[...].astype(BF16)

    grid_spec = pltpu.PrefetchScalarGridSpec(
        num_scalar_prefetch=1, grid=(nb, c // bc),
        in_specs=[pl.BlockSpec((br, bc), lambda i, j, chip_ref: (i, j))],
        out_specs=pl.BlockSpec((br, bc), lambda i, j, chip_ref: (chip_ref[0] * nb + i, j)))
    return pl.pallas_call(
        body, name=name, grid_spec=grid_spec,
        out_shape=jax.ShapeDtypeStruct((4 * r, c), BF16),
        compiler_params=_params(("parallel", "parallel")),
    )(chip, w)


def _sum_own_and_sibling(core, g, r1, name):
    _, r, c = g.shape
    h = r // 2
    br, bc = _blk(h, 256, 8), _blk(c, 2048)
    nb = h // br

    def body(core_ref, g_ref, r_ref, o_ref):
        o_ref[...] = (g_ref[...] + r_ref[...]).astype(BF16)

    grid_spec = pltpu.PrefetchScalarGridSpec(
        num_scalar_prefetch=1, grid=(4, nb, c // bc),
        in_specs=[pl.BlockSpec((None, br, bc), lambda k, i, j, core_ref: (k, core_ref[0] * nb + i, j)),
                  pl.BlockSpec((None, br, bc), lambda k, i, j, core_ref: (k, i, j))],
        out_specs=pl.BlockSpec((None, br, bc), lambda k, i, j, core_ref: (k, i, j)))
    return pl.pallas_call(
        body, name=name, grid_spec=grid_spec,
        out_shape=jax.ShapeDtypeStruct((4, h, c), BF16),
        compiler_params=_params(("parallel", "parallel", "parallel")),
    )(core, g, r1)


def _sum_chips(where, p, r2, name, piece=(0, 1), so_far=None):
    _, h, c = p.shape
    k, n = piece
    br, bc = _blk(h, 256, 8), _blk(c, 2048)
    nb, ncb = h // br, c // bc

    def body(where_ref, p_ref, r_ref, *rest):
        acc = p_ref[...].astype(F32)
        for j in range(3):
            acc = acc + r_ref[j].astype(F32)
        rest[-1][...] = acc

    extra_specs, extra, aliases = ([], [], {}) if so_far is None else ([ANY], [so_far], {3: 0})
    grid_spec = pltpu.PrefetchScalarGridSpec(
        num_scalar_prefetch=1, grid=(nb, ncb),
        in_specs=[pl.BlockSpec((None, br, bc), lambda i, j, where_ref: (where_ref[0], i, j)),
                  pl.BlockSpec((3, br, bc), lambda i, j, where_ref: (0, i, j))] + extra_specs,
        out_specs=pl.BlockSpec((br, bc), lambda i, j, where_ref: (where_ref[1] * nb + i, k * ncb + j)))
    return pl.pallas_call(
        body, name=name, grid_spec=grid_spec,
        out_shape=jax.ShapeDtypeStruct((2 * h, n * c), F32), input_output_aliases=aliases,
        compiler_params=_params(("parallel", "parallel")),
    )(where, p, r2, *extra)


def _adamw_math(w, g, m, v):
    m2 = ADAM_B1 * m + (1.0 - ADAM_B1) * g
    v2 = ADAM_B2 * v + (1.0 - ADAM_B2) * (g * g)
    m_hat = m2 / (1.0 - ADAM_B1 ** ADAM_STEP)
    v_hat = v2 / (1.0 - ADAM_B2 ** ADAM_STEP)
    delta = -ADAM_LR * (m_hat / (jnp.sqrt(v_hat) + ADAM_EPS) + ADAM_WD * w)
    return delta, m2, v2


def _adamw(w, g, m, v, name, after=None):
    r, c = w.shape
    br, bc = _blk(r, 128, 8), _blk(c, 2048)

    def body(w_ref, g_ref, m_ref, v_ref, go_ref, d_ref, m2_ref, v2_ref):
        gv = g_ref[...]
        d, m2, v2 = _adamw_math(w_ref[...], gv, m_ref[...], v_ref[...])
        go_ref[...] = gv
        d_ref[...] = d
        m2_ref[...] = m2
        v2_ref[...] = v2

    spec = pl.BlockSpec((br, bc), lambda i, j: (i, j))
    body, extra_specs, extra = _ordered(body, 4, after)
    return pl.pallas_call(
        body, name=name, grid=(r // br, c // bc),
        in_specs=[spec] * 4 + extra_specs, out_specs=[spec] * 4,
        out_shape=[jax.ShapeDtypeStruct((r, c), F32)] * 4,
        compiler_params=_params(("parallel", "parallel")),
    )(w, g, m, v, *extra)


def _sum_rows8(g, name, after=None):
    n = g.shape[1]

    def body(g_ref, o_ref):
        acc = g_ref[0:1, :]
        for k in range(1, 8):
            acc = acc + g_ref[k:k + 1, :]
        o_ref[...] = acc

    body, extra_specs, extra = _ordered(body, 1, after)
    return pl.pallas_call(
        body, name=name, out_shape=jax.ShapeDtypeStruct((1, n), F32),
        in_specs=[pl.BlockSpec(memory_space=pltpu.VMEM)] + extra_specs,
        out_specs=pl.BlockSpec(memory_space=pltpu.VMEM),
        compiler_params=_params(),
    )(g, *extra)


def _mm(a, b, mode, out_dtype, name, bm=1024, bn=None, after=None, col_blocks=None):
    if mode == "nn":
        (_, m, k), (g, _, n) = a.shape, b.shape
    elif mode == "tn":
        (_, k, m), (g, _, n) = a.shape, b.shape
    else:
        (g, m, k), (_, n, _) = a.shape, b.shape
    if bn is None:
        bn = 1024 if k <= 2048 else 512
    bm, bn = _blk(m, bm), _blk(n, bn)

    if mode == "nt":
        def body(a_ref, b_ref, o_ref, acc_ref):
            part = lax.dot_general(a_ref[...], b_ref[...], (((1,), (1,)), ((), ())),
                                   preferred_element_type=F32)
            if g == 1:
                o_ref[...] = part.astype(out_dtype)
            else:
                gi = pl.program_id(2)

                @pl.when(gi == 0)
                def _():
                    acc_ref[...] = part

                @pl.when(gi > 0)
                def _():
                    acc_ref[...] += part

                @pl.when(gi == g - 1)
                def _():
                    o_ref[...] = acc_ref[...].astype(out_dtype)

        body, extra_specs, extra = _ordered(body, 2, after)
        return pl.pallas_call(
            body, name=name, grid=(m // bm, n // bn, g),
            in_specs=[pl.BlockSpec((None, bm, k), lambda i, j, gi: (gi, i, 0)),
                      pl.BlockSpec((None, bn, k), lambda i, j, gi: (gi, j, 0))] + extra_specs,
            out_specs=pl.BlockSpec((None, bm, bn), lambda i, j, gi: (0, i, j)),
            out_shape=jax.ShapeDtypeStruct((1, m, n), out_dtype),
            scratch_shapes=[pltpu.VMEM((bm, bn), F32)],
            compiler_params=_params(("parallel", "parallel", "arbitrary")),
        )(a, b, *extra)

    contract = (((1,), (0,)), ((), ())) if mode == "nn" else (((0,), (0,)), ((), ()))

    def body(a_ref, b_ref, o_ref):
        o_ref[...] = lax.dot_general(a_ref[...], b_ref[...], contract,
                                     preferred_element_type=F32).astype(out_dtype)

    a_spec = (pl.BlockSpec((None, bm, k), lambda i, gi, j: (0, i, 0)) if mode == "nn"
              else pl.BlockSpec((None, k, bm), lambda i, gi, j: (0, 0, i)))
    first, count = (0, n // bn) if col_blocks is None else col_blocks
    body, extra_specs, extra = _ordered(body, 2, after)
    return pl.pallas_call(
        body, name=name, grid=(m // bm, g, count),
        in_specs=[a_spec, pl.BlockSpec((None, k, bn), lambda i, gi, j: (gi, 0, first + j))] + extra_specs,
        out_specs=pl.BlockSpec((None, bm, bn), lambda i, gi, j: (gi, i, j)),
        out_shape=jax.ShapeDtypeStruct((g, m, count * bn), out_dtype),
        compiler_params=_params(("parallel", "parallel", "parallel")),
    )(a, b, *extra)


def _mm_plane(a, b, plane, name, planes_so_far=None, after=None, bm=1024, bn=1024):
    (m, k), (g, _, n) = a.shape, b.shape
    bm, bn = _blk(m, bm), _blk(n, bn)

    def body(plane_ref, a_ref, b_ref, *rest):
        rest[-1][...] = jnp.dot(a_ref[...], b_ref[...], preferred_element_type=F32).astype(BF16)

    extra_specs, extra, aliases = [], [], {}
    if planes_so_far is not None:
        extra_specs.append(ANY)
        extra.append(planes_so_far)
        aliases = {3: 0}
    if after is not None:
        extra_specs.append(ANY)
        extra.append(after)
    grid_spec = pltpu.PrefetchScalarGridSpec(
        num_scalar_prefetch=1, grid=(m // bm, n // bn),
        in_specs=[pl.BlockSpec((bm, k), lambda i, j, p: (i, 0)),
                  pl.BlockSpec((None, k, bn), lambda i, j, p: (p[0], 0, j))] + extra_specs,
        out_specs=pl.BlockSpec((None, bm, bn), lambda i, j, p: (p[0], i, j)))
    return pl.pallas_call(
        body, name=name, grid_spec=grid_spec,
        out_shape=jax.ShapeDtypeStruct((g, m, n), BF16), input_output_aliases=aliases,
        compiler_params=_params(("parallel", "parallel")),
    )(plane, a, b, *extra)


def _row_specs(br, d):
    return (pl.BlockSpec((br, d), lambda i: (i, 0)), pl.BlockSpec((1, d), lambda i: (0, 0)),
            pl.BlockSpec((8, d), lambda i: (0, 0)))


def _rstd(xv):
    return lax.rsqrt(jnp.mean(xv * xv, axis=-1, keepdims=True) + RMS_EPS)


def _colsum(v):
    return jnp.sum(v, axis=0, keepdims=True)


def _norm_mod_fwd(x, g, scale, shift, name, o=None, gate=None, after=None):
    s, d = x.shape
    br = _blk(s, 256, 8)
    has_res = o is not None
    row, vec, _ = _row_specs(br, d)

    def body(*refs):
        if has_res:
            x_ref, o_ref, gate_ref, g_ref, sc_ref, sh_ref, x1_ref, h_ref = refs
            xv = x_ref[...] + gate_ref[...] * o_ref[...]
            x1_ref[...] = xv
        else:
            x_ref, g_ref, sc_ref, sh_ref, h_ref = refs
            xv = x_ref[...]
        n = xv * _rstd(xv) * g_ref[...]
        h_ref[...] = (n * (1.0 + sc_ref[...]) + sh_ref[...]).astype(BF16)

    ins = [x] + ([o, gate] if has_res else []) + [g, scale, shift]
    in_specs = [row] + ([row, vec] if has_res else []) + [vec] * 3
    out_shape = ([jax.ShapeDtypeStruct((s, d), F32)] if has_res else []) + [jax.ShapeDtypeStruct((s, d), BF16)]
    body, extra_specs, extra = _ordered(body, len(ins), after)
    out = pl.pallas_call(
        body, name=name, grid=(s // br,), in_specs=in_specs + extra_specs, out_specs=[row] * len(out_shape),
        out_shape=out_shape, compiler_params=_params(("parallel",)),
    )(*ins, *extra)
    return out if has_res else out[0]


def _final_loss(x1, o1, gate1, final_g, tgt, name):
    s, d = x1.shape
    br = _blk(s, 256, 8)
    row, vec, acc = _row_specs(br, d)

    def body(x1_ref, o_ref, gate_ref, g_ref, t_ref, dx_ref, do_ref, acc_ref):
        @pl.when(pl.program_id(0) == 0)
        def _():
            acc_ref[...] = jnp.zeros_like(acc_ref)

        gate, o, g = gate_ref[...], o_ref[...], g_ref[...]
        x2 = x1_ref[...] + gate * o
        r = _rstd(x2)
        xh = x2 * r
        err = xh * g - t_ref[...]
        loss = 0.5 * _colsum(jnp.mean(err * err, axis=-1, keepdims=True))
        dout = err * (1.0 / d)
        dxh = dout * g
        dx2 = r * (dxh - xh * jnp.mean(dxh * xh, axis=-1, keepdims=True))
        dx_ref[...] = dx2
        do_ref[...] = (dx2 * gate).astype(BF16)
        acc_ref[0:1, :] += _colsum(dout * xh)
        acc_ref[1:2, :] += _colsum(dx2 * o)
        acc_ref[2:3, :] += jnp.broadcast_to(loss, (1, d))

    return pl.pallas_call(
        body, name=name, grid=(s // br,),
        in_specs=[row, row, vec, vec, row], out_specs=[row, row, acc],
        out_shape=[jax.ShapeDtypeStruct((s, d), F32), jax.ShapeDtypeStruct((s, d), BF16),
                   jax.ShapeDtypeStruct((8, d), F32)],
        compiler_params=_params(("arbitrary",)),
    )(x1, o1, gate1, final_g, tgt)


def _norm_mod_bwd(dh, x, g, scale, dx_next, name, o_prev=None, gate_prev=None, after=None):
    s, d = x.shape
    br = _blk(s, 256, 8)
    has_prev = o_prev is not None
    row, vec, acc = _row_specs(br, d)

    def body(*refs):
        if has_prev:
            dh_ref, x_ref, g_ref, sc_ref, dxn_ref, o_ref, gate_ref, dx_ref, do_ref, acc_ref = refs
        else:
            dh_ref, x_ref, g_ref, sc_ref, dxn_ref, dx_ref, acc_ref = refs

        @pl.when(pl.program_id(0) == 0)
        def _():
            acc_ref[...] = jnp.zeros_like(acc_ref)

        xv, gv, dhv = x_ref[...], g_ref[...], dh_ref[...]
        r = _rstd(xv)
        xh = xv * r
        acc_ref[0:1, :] += _colsum(dhv * (xh * gv))
        acc_ref[1:2, :] += _colsum(dhv)
        dn = dhv * (1.0 + sc_ref[...])
        acc_ref[2:3, :] += _colsum(dn * xh)
        dxh = dn * gv
        dx = dxn_ref[...] + r * (dxh - xh * jnp.mean(dxh * xh, axis=-1, keepdims=True))
        dx_ref[...] = dx
        if has_prev:
            acc_ref[3:4, :] += _colsum(dx * o_ref[...])
            do_ref[...] = (dx * gate_ref[...]).astype(BF16)

    ins = [dh, x, g, scale, dx_next] + ([o_prev, gate_prev] if has_prev else [])
    in_specs = [row, row, vec, vec, row] + ([row, vec] if has_prev else [])
    out_shape = [jax.ShapeDtypeStruct((s, d), F32)]
    out_specs = [row]
    if has_prev:
        out_shape.append(jax.ShapeDtypeStruct((s, d), BF16))
        out_specs.append(row)
    out_shape.append(jax.ShapeDtypeStruct((8, d), F32))
    out_specs.append(acc)
    body, extra_specs, extra = _ordered(body, len(ins), after)
    return pl.pallas_call(
        body, name=name, grid=(s // br,), in_specs=in_specs + extra_specs, out_specs=out_specs,
        out_shape=out_shape, compiler_params=_params(("arbitrary",)),
    )(*ins, *extra)


def _tiles(p):
    s, c = p.shape
    return p.reshape(s // SUBLANES, SUBLANES, c)


def _shift_down(p, k):
    if k == 0:
        return p
    r = pltpu.roll(_tiles(p), k, 1)
    before = jnp.concatenate([jnp.zeros_like(r[:1]), r[:-1]], axis=0)
    rows = lax.broadcasted_iota(jnp.int32, r.shape, 1)
    return jnp.where(rows >= k, r, before).reshape(p.shape)


def _shift_up(p, k):
    if k == 0:
        return p
    r = pltpu.roll(_tiles(p), SUBLANES - k, 1)
    after = jnp.concatenate([r[1:], jnp.zeros_like(r[:1])], axis=0)
    rows = lax.broadcasted_iota(jnp.int32, r.shape, 1)
    return jnp.where(rows < SUBLANES - k, r, after).reshape(p.shape)


def _sigmoid(z):
    return 0.5 * (jnp.tanh(0.5 * z) + 1.0)


def _sc_parts(proj_ref, w_ref):
    b, cg, v, g = (proj_ref[i].astype(F32) for i in range(4))
    p = cg * v
    u = w_ref[2:3, :] * p + w_ref[1:2, :] * _shift_down(p, 1) + w_ref[0:1, :] * _shift_down(p, 2)
    return b, cg, v, g, p, u


def _sc_fwd(proj, conv_w, name):
    _, s, e = proj.shape
    bc = _blk(e, 256)

    def body(proj_ref, w_ref, y_ref):
        b, _, _, g, _, u = _sc_parts(proj_ref, w_ref)
        y_ref[...] = (b * u * (g * _sigmoid(g))).astype(BF16)

    return pl.pallas_call(
        body, name=name, grid=(e // bc,),
        in_specs=[pl.BlockSpec((4, s, bc), lambda j: (0, 0, j)), pl.BlockSpec((3, bc), lambda j: (0, j))],
        out_specs=pl.BlockSpec((s, bc), lambda j: (0, j)),
        out_shape=jax.ShapeDtypeStruct((s, e), BF16),
        compiler_params=_params(("parallel",)),
    )(proj, conv_w)


def _sc_bwd(proj, dy, conv_w, name, after=None):
    _, s, e = proj.shape
    bc = _blk(e, 256)

    def body(proj_ref, dy_ref, w_ref, dp_ref, dw_ref):
        b, cg, v, g, p, u = _sc_parts(proj_ref, w_ref)
        dyv = dy_ref[...].astype(F32)
        sig = _sigmoid(g)
        t = dyv * (g * sig)
        du = t * b
        dp_ref[0] = (t * u).astype(BF16)
        dp_ref[3] = (dyv * b * u * (sig * (1.0 + g * (1.0 - sig)))).astype(BF16)
        dpp = w_ref[2:3, :] * du + w_ref[1:2, :] * _shift_up(du, 1) + w_ref[0:1, :] * _shift_up(du, 2)
        dp_ref[1] = (dpp * v).astype(BF16)
        dp_ref[2] = (dpp * cg).astype(BF16)
        dw_ref[2:3, :] = _colsum(du * p)
        dw_ref[1:2, :] = _colsum(du * _shift_down(p, 1))
        dw_ref[0:1, :] = _colsum(du * _shift_down(p, 2))

    body, extra_specs, extra = _ordered(body, 3, after)
    return pl.pallas_call(
        body, name=name, grid=(e // bc,),
        in_specs=[pl.BlockSpec((4, s, bc), lambda j: (0, 0, j)), pl.BlockSpec((s, bc), lambda j: (0, j)),
                  pl.BlockSpec((3, bc), lambda j: (0, j))] + extra_specs,
        out_specs=[pl.BlockSpec((4, s, bc), lambda j: (0, 0, j)), pl.BlockSpec((3, bc), lambda j: (0, j))],
        out_shape=[jax.ShapeDtypeStruct((4, s, e), BF16), jax.ShapeDtypeStruct((3, e), F32)],
        compiler_params=_params(("parallel",)),
    )(proj, dy, conv_w, *extra)


def _softplus_neg(lam):
    u = jnp.exp(-jnp.abs(lam))
    w = 1.0 + u
    log1p = jnp.where(w == 1.0, u, jnp.log(w) * (u / jnp.where(w == 1.0, 1.0, w - 1.0)))
    return jnp.maximum(-lam, 0.0) + log1p


def _one_minus_exp(z):
    series = -z * (1.0 + z * (0.5 + z * (1.0 / 6.0 + z * (1.0 / 24.0))))
    return jnp.where(z > -0.02, series, 1.0 - jnp.exp(z))


def _scan_in_tiles(a, b, reverse):
    shape = a.shape
    a, b = _tiles(a), _tiles(b)
    rows = lax.broadcasted_iota(jnp.int32, a.shape, 1)
    for step in (1, 2, 4):
        shift = SUBLANES - step if reverse else step
        ok = rows < SUBLANES - step if reverse else rows >= step
        a_s, b_s = pltpu.roll(a, shift, 1), pltpu.roll(b, shift, 1)
        b = jnp.where(ok, a * b_s + b, b)
        a = jnp.where(ok, a * a_s, a)
    return a.reshape(shape), b.reshape(shape)


def _by_rows(fn, arrays, rows=32):
    s = arrays[0].shape[0]
    rows = min(rows, s)
    for t in range(0, s, rows):
        fn(t, *(a[t:t + rows] for a in arrays))


def _scan_carry(a_ref, b_ref, h_ref, reverse):
    s, c = a_ref.shape
    n = s // 8

    def step(i, carry):
        gi = n - 1 - i if reverse else i
        sl = pl.ds(pl.multiple_of(gi * 8, 8), 8)
        h = b_ref[sl, :] + a_ref[sl, :] * carry
        h_ref[sl, :] = h
        return h[0:1, :] if reverse else h[7:8, :]

    lax.fori_loop(0, n, step, jnp.zeros((1, c), F32), unroll=8 if n % 8 == 0 else 1)


def _lru_specs(s, e_half, n_heads):
    hp = e_half // HEAD_DIM
    c = HEAD_DIM
    return dict(
        pair=pl.BlockSpec((2, None, s, c), lambda h: (0, h // hp, 0, h % hp)),
        conv_w=pl.BlockSpec((4, c), lambda h: (0, h)),
        chan=pl.BlockSpec((1, c), lambda h: (0, h)),
        w=pl.BlockSpec((4, None, c // 4, c), lambda h: (0, h, 0, 0)),
        bias=pl.BlockSpec((None, 1, c), lambda h: (h, 0, 0)),
        plane=pl.BlockSpec((s, c), lambda h: (0, h)),
    )


def _lru_gate_inputs(vp, cw_ref, cb_ref, wa_ref, ba_ref, wx_ref, bx_ref):
    c = HEAD_DIM
    taps = [_shift_down(vp, 3 - k) for k in range(4)]
    v = cb_ref[...] + sum(cw_ref[k:k + 1, :] * taps[k] for k in range(4))
    vb = v.astype(BF16)
    wa = wa_ref[...].reshape(c, c)
    wx = wx_ref[...].reshape(c, c)
    zr = jnp.dot(vb, wa, preferred_element_type=F32) + ba_ref[...]
    zi = jnp.dot(vb, wx, preferred_element_type=F32) + bx_ref[...]
    return taps, v, vb, wa, wx, zr, zi


def _lru_fwd(proj, conv_w, conv_b, w_a, b_a, w_x, b_x, lam, name):
    _, _, s, e_half = proj.shape
    n_heads = 2 * e_half // HEAD_DIM
    sp_ = _lru_specs(s, e_half, n_heads)

    def body(pg_ref, cw_ref, cb_ref, wa_ref, ba_ref, wx_ref, bx_ref, lam_ref,
             y_ref, a_ref, hs_ref, sa_ref, sb_ref, sh_ref):
        _, v, _, _, _, zr, zi = _lru_gate_inputs(
            pg_ref[0].astype(F32), cw_ref, cb_ref, wa_ref, ba_ref, wx_ref, bx_ref)
        rate = (-RGLRU_C) * _softplus_neg(lam_ref[...])

        def decay_and_input(t, v_c, zr_c, zi_c):
            la = rate * _sigmoid(zr_c)
            a = jnp.exp(la)
            b = jnp.sqrt(_one_minus_exp(2.0 * la)) * (_sigmoid(zi_c) * v_c)
            a_ref[t:t + a.shape[0]] = a
            sa_ref[t:t + a.shape[0]], sb_ref[t:t + a.shape[0]] = _scan_in_tiles(a, b, reverse=False)

        _by_rows(decay_and_input, [v, zr, zi])
        _scan_carry(sa_ref, sb_ref, sh_ref, reverse=False)

        def gated_output(t, hs_c, g_c):
            g = g_c.astype(F32)
            y_ref[t:t + g.shape[0]] = (hs_c * (g * _sigmoid(g))).astype(BF16)
            hs_ref[t:t + g.shape[0]] = hs_c.astype(BF16)

        _by_rows(gated_output, [sh_ref, pg_ref.at[1]])

    e = 2 * e_half
    return pl.pallas_call(
        body, name=name, grid=(n_heads,),
        in_specs=[sp_["pair"], sp_["conv_w"], sp_["chan"], sp_["w"], sp_["bias"], sp_["w"],
                  sp_["bias"], sp_["chan"]],
        out_specs=[sp_["plane"]] * 3,
        out_shape=[jax.ShapeDtypeStruct((s, e), BF16), jax.ShapeDtypeStruct((s, e), F32),
                   jax.ShapeDtypeStruct((s, e), BF16)],
        scratch_shapes=[pltpu.VMEM((s, HEAD_DIM), F32)] * 3,
        compiler_params=_params(("parallel",)),
    )(proj, conv_w, conv_b, w_a, b_a, w_x, b_x, lam)


def _lru_bwd(proj, dy, saved, conv_w, conv_b, w_a, b_a, w_x, b_x, lam, name):
    _, _, s, e_half = proj.shape
    e = 2 * e_half
    c = HEAD_DIM
    n_heads = e // c
    sp_ = _lru_specs(s, e_half, n_heads)

    def body(pg_ref, dy_ref, a_ref, hs_ref, cw_ref, cb_ref, wa_ref, ba_ref, wx_ref, bx_ref, lam_ref,
             dpg_ref, dwa_ref, dwx_ref, dba_ref, dbx_ref, dlam_ref, dcw_ref, dcb_ref,
             sa_ref, sb_ref, sd_ref, dzr_ref, dzi_ref):
        taps, v, vb, wa, wx, zr, zi = _lru_gate_inputs(
            pg_ref[0].astype(F32), cw_ref, cb_ref, wa_ref, ba_ref, wx_ref, bx_ref)
        lam = lam_ref[...]
        rate = (-RGLRU_C) * _softplus_neg(lam)
        a = a_ref[...]

        def state_gradient_in_tiles(t, a_next, dy_c, g_c, hs_c):
            g, dyv = g_c.astype(F32), dy_c.astype(F32)
            sig = _sigmoid(g)
            rows = slice(t, t + g.shape[0])
            dpg_ref[1, rows] = (dyv * hs_c.astype(F32) * (sig * (1.0 + g * (1.0 - sig)))).astype(BF16)
            sa_ref[rows], sb_ref[rows] = _scan_in_tiles(a_next, dyv * (g * sig), reverse=True)

        _by_rows(state_gradient_in_tiles, [_shift_up(a, 1), dy_ref, pg_ref.at[1], hs_ref])
        _scan_carry(sa_ref, sb_ref, sd_ref, reverse=True)

        sums = []

        def gate_gradients(t, dh, hs_before, a_c, zr_c, zi_c, v_c):
            r, i = _sigmoid(zr_c), _sigmoid(zi_c)
            q = (1.0 - a_c) * (1.0 + a_c)
            inv_nm = lax.rsqrt(q)
            div = dh * (q * inv_nm)
            dla = (dh * hs_before) * a_c - (dh * (i * v_c)) * (a_c * a_c * inv_nm)
            dzr = (dla * rate) * (r * (1.0 - r))
            dzi = (div * v_c) * (i * (1.0 - i))
            rows = slice(t, t + dh.shape[0])
            dzr_ref[rows], dzi_ref[rows] = dzr.astype(BF16), dzi.astype(BF16)
            sa_ref[rows] = div * i
            sums.append((_colsum(dla * r), _colsum(dzr), _colsum(dzi)))

        _by_rows(gate_gradients, [sd_ref, _shift_down(hs_ref[...].astype(F32), 1), a_ref, zr, zi, v])
        dlam_ref[...] = sum(p[0] for p in sums) * ((-RGLRU_C) * (-_sigmoid(-lam)))
        dba_ref[...] = sum(p[1] for p in sums)
        dbx_ref[...] = sum(p[2] for p in sums)
        dzr_b, dzi_b = dzr_ref[...], dzi_ref[...]
        tn = (((0,), (0,)), ((), ()))
        nt = (((1,), (1,)), ((), ()))
        dwa_ref[...] = lax.dot_general(vb, dzr_b, tn, preferred_element_type=F32)
        dwx_ref[...] = lax.dot_general(vb, dzi_b, tn, preferred_element_type=F32)
        dv = (sa_ref[...] + lax.dot_general(dzr_b, wa, nt, preferred_element_type=F32)
              + lax.dot_general(dzi_b, wx, nt, preferred_element_type=F32))
        dcb_ref[...] = _colsum(dv)
        dvp = jnp.zeros_like(dv)
        for k in range(4):
            dvp = dvp + cw_ref[k:k + 1, :] * _shift_up(dv, 3 - k)
            dcw_ref[k:k + 1, :] = _colsum(dv * taps[k])
        dpg_ref[0] = dvp.astype(BF16)

    head_mat = pl.BlockSpec((None, c, c), lambda h: (h, 0, 0))
    outs = pl.pallas_call(
        body, name=name, grid=(n_heads,),
        in_specs=[sp_["pair"]] + [sp_["plane"]] * 3 + [sp_["conv_w"], sp_["chan"], sp_["w"], sp_["bias"],
                                                        sp_["w"], sp_["bias"], sp_["chan"]],
        out_specs=[sp_["pair"], head_mat, head_mat, sp_["bias"], sp_["bias"],
                   sp_["chan"], sp_["conv_w"], sp_["chan"]],
        out_shape=[jax.ShapeDtypeStruct((2, 2, s, e_half), BF16),
                   jax.ShapeDtypeStruct((n_heads, c, c), F32), jax.ShapeDtypeStruct((n_heads, c, c), F32),
                   jax.ShapeDtypeStruct((n_heads, 1, c), F32), jax.ShapeDtypeStruct((n_heads, 1, c), F32),
                   jax.ShapeDtypeStruct((1, e), F32), jax.ShapeDtypeStruct((4, e), F32),
                   jax.ShapeDtypeStruct((1, e), F32)],
        scratch_shapes=[pltpu.VMEM((s, c), F32)] * 3 + [pltpu.VMEM((s, c), BF16)] * 2,
        compiler_params=_params(("parallel",)),
    )(proj, dy, *saved, conv_w, conv_b, w_a, b_a, w_x, b_x, lam)
    return tuple(outs)


def _ada_fwd(c_all, ada_w, name):
    n_l, d, f = ada_w.shape
    bf = _blk(f, 512)

    def body(c_ref, w_ref, o_ref):
        cv = c_ref[...]
        sc = (cv * _sigmoid(cv)).astype(BF16)
        o_ref[...] = jnp.dot(sc, w_ref[...].astype(BF16), preferred_element_type=F32)

    return pl.pallas_call(
        body, name=name, grid=(n_l, f // bf),
        in_specs=[pl.BlockSpec((8, d), lambda l, j: (0, 0)), pl.BlockSpec((None, d, bf), lambda l, j: (l, 0, j))],
        out_specs=pl.BlockSpec((None, 8, bf), lambda l, j: (l, 0, j)),
        out_shape=jax.ShapeDtypeStruct((n_l, 8, f), F32),
        compiler_params=_params(("parallel", "parallel")),
    )(c_all, ada_w)


def _ada_bwd_adamw(c_t, dmod, w, m, v, name, after=None):
    n_l, d, f = w.shape
    bf = _blk(f, 256)

    def body(c_ref, dm_ref, w_ref, m_ref, v_ref, g_ref, d_ref, m2_ref, v2_ref):
        cv = c_ref[...]
        sc = cv * _sigmoid(cv)
        dm = dm_ref[...]
        g = sc[:, 0:1] * dm[0:1, :]
        for b in range(1, 8):
            g = g + sc[:, b:b + 1] * dm[b:b + 1, :]
        g_ref[...] = g
        dl, m2, v2 = _adamw_math(w_ref[...], g, m_ref[...], v_ref[...])
        d_ref[...] = dl
        m2_ref[...] = m2
        v2_ref[...] = v2

    big = pl.BlockSpec((None, d, bf), lambda l, j: (l, 0, j))
    body, extra_specs, extra = _ordered(body, 5, after)
    return pl.pallas_call(
        body, name=name, grid=(n_l, f // bf),
        in_specs=[pl.BlockSpec((d, 8), lambda l, j: (0, 0)), pl.BlockSpec((None, 8, bf), lambda l, j: (l, 0, j)),
                  big, big, big] + extra_specs,
        out_specs=[big] * 4, out_shape=[jax.ShapeDtypeStruct((n_l, d, f), F32)] * 4,
        compiler_params=_params(("parallel", "parallel")),
    )(c_t, dmod, w, m, v, *extra)


def _pack(parts):
    padded, offs, n = [], [], 0
    for p in parts:
        p = p.reshape(-1)
        size = -(-p.shape[0] // PACK) * PACK
        offs.append(n)
        n += size
        padded.append(jnp.pad(p, (0, size - p.shape[0])) if size != p.shape[0] else p)
    return jnp.concatenate(padded), offs, n


def kernel(x, c, norm_g, ada_w, ada_b, sc_w_in, sc_conv_w, sc_w_out, lru_w_in, lru_conv_w, lru_conv_b, lru_w_a, lru_b_a, lru_w_x, lru_b_x, lru_lambda, lru_w_out, final_g, loss_target, m_norm_g, m_ada_w, m_ada_b, m_sc_w_in, m_sc_conv_w, m_sc_w_out, m_lru_w_in, m_lru_conv_w, m_lru_conv_b, m_lru_w_a, m_lru_b_a, m_lru_w_x, m_lru_b_x, m_lru_lambda, m_lru_w_out, m_final_g, v_norm_g, v_ada_w, v_ada_b, v_sc_w_in, v_sc_conv_w, v_sc_w_out, v_lru_w_in, v_lru_conv_w, v_lru_conv_b, v_lru_w_a, v_lru_b_a, v_lru_w_x, v_lru_b_x, v_lru_lambda, v_lru_w_out, v_final_g):
    xi, yi, ci = _pos()
    chip = 2 * xi + yi
    batch = 4 * xi + 2 * yi + ci
    core_op = jnp.reshape(ci, (1,)).astype(jnp.int32)
    chip_op = jnp.reshape(chip, (1,)).astype(jnp.int32)
    where_op = jnp.stack([chip, ci]).astype(jnp.int32)

    x2d, tgt = x[0], loss_target[0]
    s, d = x2d.shape
    es = sc_conv_w.shape[2]
    e = 4 * es
    n_heads = lru_w_a.shape[1]
    hj = lru_b_a.shape[2]
    f = ada_w.shape[2]
    row = lambda t: t.reshape(1, -1)

    small_parts = [c, sc_conv_w, lru_conv_w, lru_conv_b, lru_b_a, lru_b_x, lru_lambda]
    small, offs, n_small = _pack(small_parts)
    got = _allgather8([small.reshape(8, n_small // 8)], "ag_small")[0].reshape(8, n_small)
    c_all = got[:, :d]
    per_chip = got[0::2]

    def chip_part(k, shape):
        size = 1
        for dim in shape:
            size *= dim
        return per_chip[:, offs[k]:offs[k] + size].reshape((4,) + shape)

    conv_w0 = jnp.transpose(chip_part(1, (3, es)), (1, 0, 2)).reshape(3, e)
    conv_w1 = jnp.transpose(chip_part(2, (4, es)), (1, 0, 2)).reshape(4, e)
    conv_b1 = chip_part(3, (es,)).reshape(1, e)
    b_a = jnp.transpose(chip_part(4, (n_heads, hj)), (1, 0, 2)).reshape(n_heads, 1, 4 * hj)
    b_x = jnp.transpose(chip_part(5, (n_heads, hj)), (1, 0, 2)).reshape(n_heads, 1, 4 * hj)
    lam = chip_part(6, (es,)).reshape(1, e)

    mod_nb = _ada_fwd(c_all, ada_w, "ada_fwd")
    mods = _allgather8([mod_nb.reshape(16, f)], "ag_mod")[0].reshape(8, 2, 8, f)[0::2]
    mine = lax.dynamic_index_in_dim(mods, batch, axis=2, keepdims=False)
    mod = jnp.transpose(mine, (1, 0, 2)).reshape(2, 4 * f) + ada_b
    shift = [row(mod[l, :d]) for l in range(2)]
    scale = [row(mod[l, d:2 * d]) for l in range(2)]
    gate = [row(mod[l, 2 * d:]) for l in range(2)]
    ng = [row(norm_g[l]) for l in range(2)]

    shards = [sc_w_in[0], sc_w_out[0], lru_w_in[0], lru_w_a[0].reshape(n_heads * hj, HEAD_DIM),
              lru_w_x[0].reshape(n_heads * hj, HEAD_DIM), lru_w_out[0]]
    names = ["sc_w_in", "sc_w_out", "lru_w_in", "lru_w_a", "lru_w_x", "lru_w_out"]
    slots = [_cast_into_slot(chip_op, w, "cast_" + nm) for w, nm in zip(shards, names)]
    send_a, recv_a, buf, started = _first_start(slots[0], mod, "ag_first_start")
    h0 = _norm_mod_fwd(x2d, ng[0], scale[0], shift[0], "norm0", after=started)
    planes = [jnp.reshape(2 * px + py, (1,)).astype(jnp.int32) for px, py in _other_chips(xi, yi)]
    proj0 = _mm_plane(h0, buf.reshape(4, d, e), chip_op, "sc_in_own")
    relays, passed, relayed = [], [], [proj0] + slots[1:]
    for j in range(2):
        send_b, recv_b, send_c, recv_c, buf, token = _first_relay(send_a, recv_a, buf, j, relayed,
                                                                  "ag_first_relay_%d" % j)
        relays.append((send_c, recv_c))
        passed.append((send_b, recv_b))
        relayed = [token]
    rest_flight, relayed = _gather_start([[slots[1]], [slots[2]], slots[3:5], [slots[5]]], token, "ag_start")
    for j in range(2):
        buf = _first_relay_done(*passed[j], buf, j, relayed, "ag_first_relay_done_%d" % j)
        proj0 = _mm_plane(h0, buf.reshape(4, d, e), planes[j], "sc_in_%d" % j, planes_so_far=proj0)
        relayed = proj0
    send_b, recv_b, buf, relayed = _first_diagonal(relays, buf, proj0, "ag_first_diagonal")
    buf = _first_diagonal_done(send_b, recv_b, relays, buf, relayed, "ag_first_diagonal_done")
    proj0 = _mm_plane(h0, buf.reshape(4, d, e), planes[2], "sc_in_2", planes_so_far=proj0)
    w_in0 = buf.reshape(4, d, e)
    in_flight = [None] + rest_flight

    def arrived(g, after, tag):
        send1, recv1, bufs = in_flight[g]
        send2, recv2, bufs, passed = _gather_forward(send1, recv1, bufs, after, "ag_forward_" + tag)
        return _gather_finish(send2, recv2, bufs, passed, "ag_finish_" + tag)

    y0 = _sc_fwd(proj0, conv_w0, "sc_mix")
    w_out0 = arrived(1, y0, "sc_w_out")[0].reshape(1, e, d)
    o0 = _mm(y0[None], w_out0, "nn", F32, "sc_out")[0]
    x1, h1 = _norm_mod_fwd(x2d, ng[1], scale[1], shift[1], "norm1", o=o0, gate=gate[0])
    send1, recv1, bufs = in_flight[2]
    proj1 = _mm_plane(h1, bufs[0].reshape(4, d, e // 2), chip_op, "lru_in_own")
    for j in range(3):
        send2, recv2, bufs, passed_on = _gather_forward(send1, recv1, bufs, proj1, "ag_forward_lru_w_in_%d" % j,
                                                        sources=(j,))
        bufs = _gather_finish(send2, recv2, bufs, passed_on, "ag_finish_lru_w_in_%d" % j, sources=(j,))
        proj1 = _mm_plane(h1, bufs[0].reshape(4, d, e // 2), planes[j], "lru_in_%d" % j, planes_so_far=proj1)
    w_in1 = bufs[0].reshape(4, d, e // 2)
    gate_ws = arrived(3, proj1, "lru_gates")
    w_a = gate_ws[0].reshape(4, n_heads, hj, HEAD_DIM)
    w_x = gate_ws[1].reshape(4, n_heads, hj, HEAD_DIM)
    pairs1 = proj1.reshape(2, 2, s, e // 2)
    y1, *lru_saved = _lru_fwd(pairs1, conv_w1, conv_b1, w_a, b_a, w_x, b_x, lam, "lru_mix")
    w_out1 = arrived(4, y1, "lru_w_out")[0].reshape(1, e, d)
    o1 = _mm(y1[None], w_out1, "nn", F32, "lru_out")[0]
    dx2, do1, acc_f = _final_loss(x1, o1, gate[1], row(final_g), tgt, "final_loss")

    def reduce_stage1(tag, grads):
        lands = [lax.empty((4, g.shape[1] // 2, g.shape[2]), F32) for g in grads]
        return _exchange_start("rs_sibling_start_" + tag, grads, lands, len(grads),
                               _plan_other_half_to_sibling, None)

    def reduce_stage2(tag, stage1, nms, after):
        send, recv, grads, lands, _ = stage1
        grads, lands = _exchange_wait("rs_sibling_wait_" + tag, send, recv, grads, lands,
                                      _plan_other_half_to_sibling, after)
        parts = [_sum_own_and_sibling(core_op, g, r1, "rs_sum1_" + nm) for g, r1, nm in zip(grads, lands, nms)]
        lands = [lax.empty((3,) + p.shape[1:], BF16) for p in parts]
        return _exchange_start("rs_chips_start_" + tag, parts, lands, 3 * len(parts),
                               _plan_partials_to_chips, None)

    def reduce_stage3(tag, stage2, nms, after):
        send, recv, parts, lands, _ = stage2
        parts, lands = _exchange_wait("rs_chips_wait_" + tag, send, recv, parts, lands,
                                      _plan_partials_to_chips, after)
        halves = [_sum_chips(where_op, p, r2, "rs_sum2_" + nm) for p, r2, nm in zip(parts, lands, nms)]
        return _exchange_start("rs_share_start_" + tag, [], halves, len(halves), _plan_share_half, None)

    def reduce_done(tag, stage3, after):
        send, recv, _, fulls, _ = stage3
        return _exchange_wait("rs_share_wait_" + tag, send, recv, [], fulls, _plan_share_half, after)[1]

    def head_major_to_chip_major(t):
        return jnp.transpose(t.reshape(n_heads, 4, hj, HEAD_DIM), (1, 0, 2, 3)).reshape(4, n_heads * hj, HEAD_DIM)

    big_state = dict(zip(names, zip(shards, [m_sc_w_in, m_sc_w_out, m_lru_w_in, m_lru_w_a, m_lru_w_x, m_lru_w_out],
                                    [v_sc_w_in, v_sc_w_out, v_lru_w_in, v_lru_w_a, v_lru_w_x, v_lru_w_out],
                                    [sc_w_in, sc_w_out, lru_w_in, lru_w_a, lru_w_x, lru_w_out])))
    big = {}

    def update(nms, fulls, after):
        for nm, g2 in zip(nms, fulls):
            w2, m4, v4, w4 = big_state[nm]
            outs = _adamw(w2, g2, m4.reshape(w2.shape), v4.reshape(w2.shape), "adamw_" + nm, after=after)
            big[nm] = tuple(t.reshape(w4.shape) for t in outs)
            after = outs[1]
        return after

    g_w_out1 = _mm(y1[None], do1[None], "tn", F32, "lru_out_dw", bm=512, bn=2048)
    dy1 = _mm(do1[None], w_out1, "nt", BF16, "lru_out_dx")[0]
    dpairs1, g_wa, g_wx, g_ba, g_bx, g_lam, g_cw1, g_cb1 = _lru_bwd(
        pairs1, dy1, lru_saved, conv_w1, conv_b1, w_a, b_a, w_x, b_x, lam, "lru_mix_bwd")
    dproj1 = dpairs1.reshape(4, s, e // 2)
    g_w_in1 = _mm(h1[None], dproj1, "tn", F32, "lru_in_dw", bm=512, bn=2048)
    lru_names = ["lru_w_out", "lru_w_a", "lru_w_x", "lru_w_in"]
    lru_rs = reduce_stage1("lru", [g_w_out1.reshape(4, es, d), head_major_to_chip_major(g_wa),
                                   head_major_to_chip_major(g_wx), g_w_in1])
    dh1 = _mm(dproj1, w_in1, "nt", F32, "lru_in_dx", after=lru_rs[4])[0]
    lru_rs = reduce_stage2("lru", lru_rs, lru_names, dh1)
    dx1, do0, acc1 = _norm_mod_bwd(dh1, x1, ng[1], scale[1], dx2, "norm1_bwd", o_prev=o0, gate_prev=gate[0],
                                   after=lru_rs[4])

    g_w_out0 = _mm(y0[None], do0[None], "tn", F32, "sc_out_dw", bm=512, bn=2048)
    out_rs = reduce_stage1("sc_out", [g_w_out0.reshape(4, es, d)])
    dy0 = _mm(do0[None], w_out0, "nt", BF16, "sc_out_dx", after=out_rs[4])[0]
    out_rs = reduce_stage2("sc_out", out_rs, ["sc_w_out"], dy0)
    dproj0, g_cw0 = _sc_bwd(proj0, dy0, conv_w0, "sc_mix_bwd", after=out_rs[4])
    stage1, stage2, after = [], [], None
    for k in range(IN_PIECES):
        g_piece = _mm(h0[None], dproj0, "tn", F32, "sc_in_dw_%d" % k, bm=512, bn=e // IN_PIECES,
                      col_blocks=(k, 1), after=after)
        stage1.append(reduce_stage1("sc_in_%d" % k, [g_piece]))
        after = stage1[k][4]
        if k:
            stage2.append(reduce_stage2("sc_in_%d" % (k - 1), stage1[k - 1], ["sc_w_in_%d" % (k - 1)], after))
            after = stage2[k - 1][4]
    lru_rs = reduce_stage3("lru", lru_rs, lru_names, after)
    k = IN_PIECES - 1
    stage2.append(reduce_stage2("sc_in_%d" % k, stage1[k], ["sc_w_in_%d" % k], lru_rs[4]))
    dh0 = _mm(dproj0, w_in0, "nt", F32, "sc_in_dx", after=stage2[k][4])[0]
    grad_x, acc0 = _norm_mod_bwd(dh0, x2d, ng[0], scale[0], dx1, "norm0_bwd")
    out_rs = reduce_stage3("sc_out", out_rs, ["sc_w_out"], acc0)
    last = update(lru_names, reduce_done("lru", lru_rs, out_rs[4]), None)
    last = update(["sc_w_out"], reduce_done("sc_out", out_rs, last), None)
    g_half = None
    for k, (send, recv, parts, lands, _) in enumerate(stage2):
        parts, lands = _exchange_wait("rs_chips_wait_sc_in_%d" % k, send, recv, parts, lands,
                                      _plan_partials_to_chips, last)
        g_half = _sum_chips(where_op, parts[0], lands[0], "rs_sum2_sc_w_in_%d" % k, piece=(k, IN_PIECES),
                            so_far=g_half)
    in_rs = _exchange_start("rs_share_start_sc_in", [], [g_half], 1, _plan_share_half, None)

    dmod = jnp.stack([jnp.concatenate([acc0[1], acc0[0], acc1[3]]),
                      jnp.concatenate([acc1[1], acc1[0], acc_f[1]])])
    part_list = [jnp.stack([acc0[2], acc1[2]]), acc_f[0], acc_f[2, :1], g_cw0, g_cw1, g_cb1, g_ba, g_bx,
                 g_lam, dmod]
    partials, poffs, n_part = _pack(part_list)
    every = _allgather8([partials.reshape(8, n_part // 8)], "ag_partials", after=in_rs[4])[0].reshape(8, n_part)
    total = _sum_rows8(every, "sum_partials")[0]

    def tot(k, shape):
        size = 1
        for dim in shape:
            size *= dim
        return total[poffs[k]:poffs[k] + size].reshape(shape)

    def my_cols(t, width):
        return lax.dynamic_slice_in_dim(t, chip * width, width, axis=t.ndim - 1)

    loss = tot(2, (1,))[0]
    g_norm_g, g_final_g, g_ada_b = tot(0, (2, d)), tot(1, (d,)), tot(9, (2, 3 * d))
    g_sc_conv_w = my_cols(tot(3, (3, e)), es)[None]
    g_lru_conv_w = my_cols(tot(4, (4, e)), es)[None]
    g_lru_conv_b = my_cols(tot(5, (1, e)), es)
    g_lru_b_a = my_cols(tot(6, (n_heads, 4 * hj)), hj)[None]
    g_lru_b_x = my_cols(tot(7, (n_heads, 4 * hj)), hj)[None]
    g_lru_lambda = my_cols(tot(8, (1, e)), es)

    dmod_all = every[:, poffs[9]:poffs[9] + 6 * d].reshape(8, 2, 3 * d)
    dmod_mine = jnp.transpose(my_cols(dmod_all, f), (1, 0, 2))
    ada = _ada_bwd_adamw(jnp.transpose(c_all), dmod_mine, ada_w, m_ada_w, v_ada_w, "ada_bwd_adamw", after=total)

    small_names = ["norm_g", "ada_b", "final_g", "sc_conv_w", "lru_conv_w", "lru_conv_b", "lru_b_a",
                   "lru_b_x", "lru_lambda"]
    small_w = [norm_g, ada_b, final_g, sc_conv_w, lru_conv_w, lru_conv_b, lru_b_a, lru_b_x, lru_lambda]
    small_g = [g_norm_g, g_ada_b, g_final_g, g_sc_conv_w, g_lru_conv_w, g_lru_conv_b, g_lru_b_a,
               g_lru_b_x, g_lru_lambda]
    small_m = [m_norm_g, m_ada_b, m_final_g, m_sc_conv_w, m_lru_conv_w, m_lru_conv_b, m_lru_b_a,
               m_lru_b_x, m_lru_lambda]
    small_v = [v_norm_g, v_ada_b, v_final_g, v_sc_conv_w, v_lru_conv_w, v_lru_conv_b, v_lru_b_a,
               v_lru_b_x, v_lru_lambda]
    pw, soffs, n_s = _pack(small_w)
    pg, pm, pv = _pack(small_g)[0], _pack(small_m)[0], _pack(small_v)[0]
    shape2 = (n_s // PACK, PACK)
    _, pd, pm2, pv2 = _adamw(pw.reshape(shape2), pg.reshape(shape2), pm.reshape(shape2), pv.reshape(shape2),
                             "adamw_small", after=ada[0])
    update(["sc_w_in"], reduce_done("sc_in", in_rs, pd), None)
    small = {}
    for k, (nm, w_) in enumerate(zip(small_names, small_w)):
        take = lambda t: t.reshape(-1)[soffs[k]:soffs[k] + w_.size].reshape(w_.shape)
        small[nm] = (small_g[k].reshape(w_.shape), take(pd), take(pm2), take(pv2))

    results = dict(small)
    results.update(big)
    results["ada_w"] = tuple(ada)
    order = ["norm_g", "ada_w", "ada_b", "sc_w_in", "sc_conv_w", "sc_w_out", "lru_w_in", "lru_conv_w",
             "lru_conv_b", "lru_w_a", "lru_b_a", "lru_w_x", "lru_b_x", "lru_lambda", "lru_w_out", "final_g"]
    out = [loss, grad_x[None]]
    for kind in range(4):
        out += [results[nm][kind] for nm in order]
    return tuple(out)
```

```python
import jax
import jax.numpy as jnp
from jax import lax
from jax.experimental import pallas as pl
from jax.experimental.pallas import tpu as pltpu

F32 = jnp.float32
BF16 = jnp.bfloat16
MESH = pl.DeviceIdType.MESH
ANY = pl.BlockSpec(memory_space=pl.ANY)

RMS_EPS = 1e-6
RGLRU_C = 8.0
HEAD_DIM = 256
ADAM_LR = 0.001
ADAM_B1 = 0.9
ADAM_B2 = 0.999
ADAM_EPS = 1e-08
ADAM_WD = 0.01
ADAM_STEP = 10
V7X_VMEM_LIMIT = 56 * 1024 * 1024
IN_PIECES = 2
LANES = 128
SUBLANES = 8
PACK = SUBLANES * LANES


def _blk(dim, pref, unit=LANES):
    if dim <= pref:
        return dim
    b = (pref // unit) * unit
    while b > unit and dim % b:
        b -= unit
    assert dim % b == 0, (dim, pref, unit)
    return b


def _params(sem=None):
    return pltpu.CompilerParams(dimension_semantics=sem, vmem_limit_bytes=V7X_VMEM_LIMIT)


def _ordered(body, n_in, after):
    if after is None:
        return body, [], []

    def ordered_body(*refs):
        return body(*refs[:n_in], *refs[n_in + 1:])

    return ordered_body, [ANY], [after]


def _pos():
    return lax.axis_index("x"), lax.axis_index("y"), lax.axis_index("c")


def _other_chips(x, y):
    return [(1 - x, y), (x, 1 - y), (1 - x, 1 - y)]


def _allgather8(arrs, name, after=None):
    n_t = len(arrs)
    ms = [a.shape[0] for a in arrs]

    def gather(*refs):
        ins, outs = refs[:n_t], refs[n_t:2 * n_t]
        send_sems, recv_sems, local_sems = refs[2 * n_t:]
        x, y, c = _pos()
        me, sibling = (x, y, c), (x, y, 1 - c)
        chips = _other_chips(x, y)

        def rows(t, px, py, pc):
            return outs[t].at[pl.ds((4 * px + 2 * py + pc) * ms[t], ms[t])]

        def copy(t, k, block, to, src=None):
            return pltpu.make_async_remote_copy(
                src_ref=rows(t, *block) if src is None else src, dst_ref=rows(t, *block),
                send_sem=send_sems.at[7 * t + k], recv_sem=recv_sems.at[7 * t + k],
                device_id=to, device_id_type=MESH)

        mine, first, passed = [], [], []
        for t in range(n_t):
            src = ins[t]
            cp = pltpu.make_async_copy(src, rows(t, *me), local_sems.at[t])
            cp.start()
            mine.append(cp)
            sends = [copy(t, 0, me, sibling, src=src)]
            sends += [copy(t, 1 + j, me, (*chip, c), src=src) for j, chip in enumerate(chips)]
            for cp in sends:
                cp.start()
            first += sends
        for t in range(n_t):
            for j, chip in enumerate(chips):
                copy(t, 1 + j, (*chip, c), me).wait_recv()
                cp = copy(t, 4 + j, (*chip, c), sibling)
                cp.start()
                passed.append(cp)
        for t in range(n_t):
            copy(t, 0, sibling, me).wait_recv()
            for j, chip in enumerate(chips):
                copy(t, 4 + j, (*chip, 1 - c), me).wait_recv()
        for cp in first + passed:
            cp.wait_send()
        for cp in mine:
            cp.wait()

    body, extra_specs, extra = _ordered(gather, n_t, after)
    return pl.pallas_call(
        body, name=name,
        out_shape=[jax.ShapeDtypeStruct((8 * m, a.shape[1]), a.dtype) for m, a in zip(ms, arrs)],
        in_specs=[ANY] * n_t + extra_specs, out_specs=[ANY] * n_t,
        scratch_shapes=[pltpu.SemaphoreType.DMA((7 * n_t,)), pltpu.SemaphoreType.DMA((7 * n_t,)),
                        pltpu.SemaphoreType.DMA((n_t,))],
    )(*arrs, *extra)


HBM = pl.BlockSpec(memory_space=pltpu.HBM)
SEM = pl.BlockSpec(memory_space=pltpu.SEMAPHORE)
TOKEN = pl.BlockSpec(memory_space=pltpu.VMEM)
IN_FLIGHT = pltpu.CompilerParams(has_side_effects=pltpu.SideEffectType.DATAFLOW_SIDE_EFFECTING)


def _in_hbm(arrs):
    return [pltpu.with_memory_space_constraint(a, pltpu.HBM) for a in arrs]


def _shard_rows(ref, h, px, py, pc):
    return ref.at[pl.ds((4 * px + 2 * py + pc) * h, h)]


def _gather_start(groups, after, name):
    bufs = [b for grp in groups for b in grp]
    n_t, n_g = len(bufs), len(groups)

    def body(*refs):
        sems, thru = refs[n_t + 1:n_t + 1 + 2 * n_g], refs[n_t + 1 + 2 * n_g:2 * n_t + 1 + 2 * n_g]
        token = refs[-1]
        x, y, c = _pos()
        t = 0
        for g, grp in enumerate(groups):
            for i in range(len(grp)):
                h = bufs[t].shape[0] // 8
                rows = _shard_rows(thru[t], h, x, y, c)
                for j, chip in enumerate(_other_chips(x, y)):
                    pltpu.make_async_remote_copy(
                        src_ref=rows, dst_ref=rows, send_sem=sems[2 * g].at[3 * i + j],
                        recv_sem=sems[2 * g + 1].at[3 * i + j], device_id=(*chip, c),
                        device_id_type=MESH).start()
                t += 1
        token[...] = jnp.zeros_like(token)

    sem_shapes = []
    for grp in groups:
        sem_shapes += [pltpu.SemaphoreType.DMA((3 * len(grp),))] * 2
    out = pl.pallas_call(
        body, name=name,
        out_shape=sem_shapes + [pltpu.HBM(b.shape, b.dtype) for b in bufs] + [jax.ShapeDtypeStruct((8, LANES), F32)],
        in_specs=[HBM] * n_t + [ANY], out_specs=[SEM] * (2 * n_g) + [HBM] * n_t + [TOKEN],
        input_output_aliases={t: 2 * n_g + t for t in range(n_t)},
        compiler_params=IN_FLIGHT,
    )(*_in_hbm(bufs), after)
    sems, thru, token = out[:2 * n_g], out[2 * n_g:2 * n_g + n_t], out[-1]
    per_group, t = [], 0
    for g, grp in enumerate(groups):
        per_group.append((sems[2 * g], sems[2 * g + 1], thru[t:t + len(grp)]))
        t += len(grp)
    return per_group, token


def _gather_forward(send_sems, recv_sems, bufs, after, name, sources=(0, 1, 2)):
    n_t = len(bufs)

    def body(*refs):
        ins = refs[:n_t]
        send1, recv1 = refs[n_t], refs[n_t + 1]
        send2, recv2 = refs[n_t + 3], refs[n_t + 4]
        token = refs[-1]
        x, y, c = _pos()
        chips = _other_chips(x, y)
        for t in range(n_t):
            h = bufs[t].shape[0] // 8
            mine = _shard_rows(ins[t], h, x, y, c)
            for j in sources:
                chip = chips[j]
                landed = _shard_rows(ins[t], h, *chip, c)
                pltpu.make_async_remote_copy(
                    src_ref=mine, dst_ref=landed, send_sem=send1.at[3 * t + j], recv_sem=recv1.at[3 * t + j],
                    device_id=(*chip, c), device_id_type=MESH).wait_recv()
                pltpu.make_async_remote_copy(
                    src_ref=landed, dst_ref=landed, send_sem=send2.at[3 * t + j], recv_sem=recv2.at[3 * t + j],
                    device_id=(x, y, 1 - c), device_id_type=MESH).start()
        for t in range(n_t):
            h = bufs[t].shape[0] // 8
            mine = _shard_rows(ins[t], h, x, y, c)
            for j in sources:
                pltpu.make_async_remote_copy(
                    src_ref=mine, dst_ref=mine, send_sem=send1.at[3 * t + j], recv_sem=recv1.at[3 * t + j],
                    device_id=(*chips[j], c), device_id_type=MESH).wait_send()
        token[...] = jnp.zeros_like(token)

    out = pl.pallas_call(
        body, name=name,
        out_shape=[pltpu.SemaphoreType.DMA((3 * n_t,))] * 2 + [pltpu.HBM(b.shape, b.dtype) for b in bufs]
        + [jax.ShapeDtypeStruct((8, LANES), F32)],
        in_specs=[HBM] * n_t + [SEM, SEM, ANY], out_specs=[SEM, SEM] + [HBM] * n_t + [TOKEN],
        input_output_aliases={t: 2 + t for t in range(n_t)},
        compiler_params=IN_FLIGHT,
    )(*bufs, send_sems, recv_sems, after)
    return out[0], out[1], out[2:2 + n_t], out[-1]


def _gather_finish(send_sems, recv_sems, bufs, after, name, sources=(0, 1, 2)):
    n_t = len(bufs)

    def body(*refs):
        ins = refs[:n_t]
        send2, recv2 = refs[n_t], refs[n_t + 1]
        x, y, c = _pos()
        chips = _other_chips(x, y)
        for t in range(n_t):
            h = bufs[t].shape[0] // 8
            for j in sources:
                chip = chips[j]
                sent = _shard_rows(ins[t], h, *chip, c)
                got = _shard_rows(ins[t], h, *chip, 1 - c)
                cp = pltpu.make_async_remote_copy(
                    src_ref=sent, dst_ref=got, send_sem=send2.at[3 * t + j], recv_sem=recv2.at[3 * t + j],
                    device_id=(x, y, 1 - c), device_id_type=MESH)
                cp.wait_send()
                cp.wait_recv()

    return pl.pallas_call(
        body, name=name, out_shape=[pltpu.HBM(b.shape, b.dtype) for b in bufs],
        in_specs=[HBM] * n_t + [SEM, SEM, ANY], out_specs=[HBM] * n_t,
        input_output_aliases={t: t for t in range(n_t)},
        compiler_params=IN_FLIGHT,
    )(*bufs, send_sems, recv_sems, after)


def _part_rows(ref, h, px, py, pc, which):
    return ref.at[pl.ds((4 * px + 2 * py + pc) * h + which * (h // 2), h // 2)]


def _remote(src, dst, send_sem, recv_sem, device):
    return pltpu.make_async_remote_copy(src_ref=src, dst_ref=dst, send_sem=send_sem, recv_sem=recv_sem,
                                        device_id=device, device_id_type=MESH)


def _first_start(buf, after, name):
    h = buf.shape[0] // 8

    def body(buf_in, after_ref, send, recv, thru, token):
        x, y, c = _pos()
        rows = _shard_rows(thru, h, x, y, c)
        for j, chip in enumerate(_other_chips(x, y)[:2]):
            _remote(rows, rows, send.at[j], recv.at[j], (*chip, c)).start()
        token[...] = jnp.zeros_like(token)

    return pl.pallas_call(
        body, name=name,
        out_shape=[pltpu.SemaphoreType.DMA((2,))] * 2 + [pltpu.HBM(buf.shape, buf.dtype),
                                                         jax.ShapeDtypeStruct((8, LANES), F32)],
        in_specs=[HBM, ANY], out_specs=[SEM, SEM, HBM, TOKEN], input_output_aliases={0: 2},
        compiler_params=IN_FLIGHT,
    )(*_in_hbm([buf]), after)


def _first_relay(send_a, recv_a, buf, j, after, name):
    h = buf.shape[0] // 8

    def body(buf_in, send_a, recv_a, *rest):
        send_b, recv_b, send_c, recv_c, _, token = rest[len(after):]
        x, y, c = _pos()
        chips = _other_chips(x, y)
        nbr, other = chips[j], chips[1 - j]
        mine, landed = _shard_rows(buf_in, h, x, y, c), _shard_rows(buf_in, h, *nbr, c)
        _remote(mine, landed, send_a.at[j], recv_a.at[j], (*nbr, c)).wait_recv()
        _remote(landed, landed, send_b.at[0], recv_b.at[0], (x, y, 1 - c)).start()
        part = _part_rows(buf_in, h, *nbr, c, j)
        _remote(part, part, send_c.at[0], recv_c.at[0], (*other, c)).start()
        _remote(mine, mine, send_a.at[j], recv_a.at[j], (*nbr, c)).wait_send()
        token[...] = jnp.zeros_like(token)

    out = pl.pallas_call(
        body, name=name,
        out_shape=[pltpu.SemaphoreType.DMA((1,))] * 4 + [pltpu.HBM(buf.shape, buf.dtype),
                                                         jax.ShapeDtypeStruct((8, LANES), F32)],
        in_specs=[HBM, SEM, SEM] + [ANY] * len(after), out_specs=[SEM] * 4 + [HBM, TOKEN],
        input_output_aliases={0: 4}, compiler_params=IN_FLIGHT,
    )(buf, send_a, recv_a, *after)
    return tuple(out)


def _first_relay_done(send_b, recv_b, buf, j, after, name):
    h = buf.shape[0] // 8

    def body(buf_in, send_b, recv_b, after_ref, thru):
        x, y, c = _pos()
        nbr = _other_chips(x, y)[j]
        cp = _remote(_shard_rows(buf_in, h, *nbr, c), _shard_rows(buf_in, h, *nbr, 1 - c),
                     send_b.at[0], recv_b.at[0], (x, y, 1 - c))
        cp.wait_send()
        cp.wait_recv()

    return pl.pallas_call(
        body, name=name, out_shape=pltpu.HBM(buf.shape, buf.dtype),
        in_specs=[HBM, SEM, SEM, ANY], out_specs=HBM, input_output_aliases={0: 0},
        compiler_params=IN_FLIGHT,
    )(buf, send_b, recv_b, after)


def _first_diagonal(relays, buf, after, name):
    h = buf.shape[0] // 8

    def body(buf_in, send_c0, recv_c0, send_c1, recv_c1, after_ref, send_b, recv_b, thru, token):
        x, y, c = _pos()
        chips = _other_chips(x, y)
        diag = chips[2]
        for j, (send_c, recv_c) in enumerate(((send_c0, recv_c0), (send_c1, recv_c1))):
            part = _part_rows(buf_in, h, *diag, c, j)
            _remote(part, part, send_c.at[0], recv_c.at[0], (*chips[1 - j], c)).wait_recv()
        whole = _shard_rows(buf_in, h, *diag, c)
        _remote(whole, whole, send_b.at[0], recv_b.at[0], (x, y, 1 - c)).start()
        token[...] = jnp.zeros_like(token)

    out = pl.pallas_call(
        body, name=name,
        out_shape=[pltpu.SemaphoreType.DMA((1,))] * 2 + [pltpu.HBM(buf.shape, buf.dtype),
                                                         jax.ShapeDtypeStruct((8, LANES), F32)],
        in_specs=[HBM] + [SEM] * 4 + [ANY], out_specs=[SEM, SEM, HBM, TOKEN], input_output_aliases={0: 2},
        compiler_params=IN_FLIGHT,
    )(buf, relays[0][0], relays[0][1], relays[1][0], relays[1][1], after)
    return tuple(out)


def _first_diagonal_done(send_b, recv_b, relays, buf, after, name):
    h = buf.shape[0] // 8

    def body(buf_in, send_b, recv_b, send_c0, recv_c0, send_c1, recv_c1, after_ref, thru):
        x, y, c = _pos()
        chips = _other_chips(x, y)
        diag = chips[2]
        cp = _remote(_shard_rows(buf_in, h, *diag, c), _shard_rows(buf_in, h, *diag, 1 - c),
                     send_b.at[0], recv_b.at[0], (x, y, 1 - c))
        cp.wait_send()
        cp.wait_recv()
        for j, (send_c, recv_c) in enumerate(((send_c0, recv_c0), (send_c1, recv_c1))):
            part = _part_rows(buf_in, h, *chips[j], c, j)
            _remote(part, part, send_c.at[0], recv_c.at[0], (*chips[1 - j], c)).wait_send()

    return pl.pallas_call(
        body, name=name, out_shape=pltpu.HBM(buf.shape, buf.dtype),
        in_specs=[HBM] + [SEM] * 6 + [ANY], out_specs=HBM, input_output_aliases={0: 0},
        compiler_params=IN_FLIGHT,
    )(buf, send_b, recv_b, relays[0][0], relays[0][1], relays[1][0], relays[1][1], after)


def _exchange_start(name, srcs, lands, n_copies, plan, after):
    ns, nl = len(srcs), len(lands)
    extra = [] if after is None else [after]

    def body(*refs):
        base = ns + nl + len(extra)
        send_sems, recv_sems = refs[base], refs[base + 1]
        src_refs, land_refs = refs[base + 2:base + 2 + ns], refs[base + 2 + ns:base + 2 + ns + nl]
        token = refs[-1]
        x, y, c = _pos()
        copies = plan(src_refs, land_refs, x, y, c)
        assert len(copies) == n_copies
        for k, (src, dst, dev) in enumerate(copies):
            pltpu.make_async_remote_copy(
                src_ref=src, dst_ref=dst, send_sem=send_sems.at[k], recv_sem=recv_sems.at[k],
                device_id=dev, device_id_type=MESH).start()
        token[...] = jnp.zeros_like(token)

    out = pl.pallas_call(
        body, name=name,
        out_shape=[pltpu.SemaphoreType.DMA((n_copies,))] * 2
        + [pltpu.HBM(a.shape, a.dtype) for a in list(srcs) + list(lands)] + [jax.ShapeDtypeStruct((8, LANES), F32)],
        in_specs=[HBM] * (ns + nl) + [ANY] * len(extra), out_specs=[SEM, SEM] + [HBM] * (ns + nl) + [TOKEN],
        input_output_aliases={i: 2 + i for i in range(ns + nl)},
        compiler_params=IN_FLIGHT,
    )(*_in_hbm(list(srcs) + list(lands)), *extra)
    return out[0], out[1], out[2:2 + ns], out[2 + ns:2 + ns + nl], out[-1]


def _exchange_wait(name, send_sems, recv_sems, srcs, lands, plan, after):
    ns, nl = len(srcs), len(lands)

    def body(*refs):
        src_refs, land_refs = refs[:ns], refs[ns:ns + nl]
        send, recv = refs[ns + nl], refs[ns + nl + 1]
        x, y, c = _pos()
        for k, (src, dst, dev) in enumerate(plan(src_refs, land_refs, x, y, c)):
            cp = pltpu.make_async_remote_copy(
                src_ref=src, dst_ref=dst, send_sem=send.at[k], recv_sem=recv.at[k],
                device_id=dev, device_id_type=MESH)
            cp.wait_send()
            cp.wait_recv()

    out = pl.pallas_call(
        body, name=name, out_shape=[pltpu.HBM(a.shape, a.dtype) for a in list(srcs) + list(lands)],
        in_specs=[HBM] * (ns + nl) + [SEM, SEM, ANY], out_specs=[HBM] * (ns + nl),
        input_output_aliases={i: i for i in range(ns + nl)},
        compiler_params=IN_FLIGHT,
    )(*srcs, *lands, send_sems, recv_sems, after)
    return out[:ns], out[ns:]


def _plan_other_half_to_sibling(src_refs, land_refs, x, y, c):
    out = []
    for g_ref, r_ref in zip(src_refs, land_refs):
        h = g_ref.shape[1] // 2
        out.append((g_ref.at[:, pl.ds((1 - c) * h, h), :], r_ref, (x, y, 1 - c)))
    return out


def _plan_partials_to_chips(src_refs, land_refs, x, y, c):
    out = []
    for p_ref, r_ref in zip(src_refs, land_refs):
        for j, (px, py) in enumerate(_other_chips(x, y)):
            out.append((p_ref.at[2 * px + py], r_ref.at[j], (px, py, c)))
    return out


def _plan_share_half(src_refs, land_refs, x, y, c):
    out = []
    for f_ref in land_refs:
        h = f_ref.shape[0] // 2
        rows = f_ref.at[pl.ds(c * h, h)]
        out.append((rows, rows, (x, y, 1 - c)))
    return out


def _cast_into_slot(chip, w, name):
    r, c = w.shape
    br, bc = _blk(r, 512, 8), _blk(c, 2048)
    nb = r // br

    def body(chip_ref, w_ref, o_ref):
        o_ref[...] = w_ref[...].astype(BF16)

    grid_spec = pltpu.PrefetchScalarGridSpec(
        num_scalar_prefetch=1, grid=(nb, c // bc),
        in_specs=[pl.BlockSpec((br, bc), lambda i, j, chip_ref: (i, j))],
        out_specs=pl.BlockSpec((br, bc), lambda i, j, chip_ref: (chip_ref[0] * nb + i, j)))
    return pl.pallas_call(
        body, name=name, grid_spec=grid_spec,
        out_shape=jax.ShapeDtypeStruct((4 * r, c), BF16),
        compiler_params=_params(("parallel", "parallel")),
    )(chip, w)


def _sum_own_and_sibling(core, g, r1, name):
    _, r, c = g.shape
    h = r // 2
    br, bc = _blk(h, 256, 8), _blk(c, 2048)
    nb = h // br

    def body(core_ref, g_ref, r_ref, o_ref):
        o_ref[...] = (g_ref[...] + r_ref[...]).astype(BF16)

    grid_spec = pltpu.PrefetchScalarGridSpec(
        num_scalar_prefetch=1, grid=(4, nb, c // bc),
        in_specs=[pl.BlockSpec((None, br, bc), lambda k, i, j, core_ref: (k, core_ref[0] * nb + i, j)),
                  pl.BlockSpec((None, br, bc), lambda k, i, j, core_ref: (k, i, j))],
        out_specs=pl.BlockSpec((None, br, bc), lambda k, i, j, core_ref: (k, i, j)))
    return pl.pallas_call(
        body, name=name, grid_spec=grid_spec,
        out_shape=jax.ShapeDtypeStruct((4, h, c), BF16),
        compiler_params=_params(("parallel", "parallel", "parallel")),
    )(core, g, r1)


def _sum_chips(where, p, r2, name, piece=(0, 1), so_far=None):
    _, h, c = p.shape
    k, n = piece
    br, bc = _blk(h, 256, 8), _blk(c, 2048)
    nb, ncb = h // br, c // bc

    def body(where_ref, p_ref, r_ref, *rest):
        acc = p_ref[...].astype(F32)
        for j in range(3):
            acc = acc + r_ref[j].astype(F32)
        rest[-1][...] = acc

    extra_specs, extra, aliases = ([], [], {}) if so_far is None else ([ANY], [so_far], {3: 0})
    grid_spec = pltpu.PrefetchScalarGridSpec(
        num_scalar_prefetch=1, grid=(nb, ncb),
        in_specs=[pl.BlockSpec((None, br, bc), lambda i, j, where_ref: (where_ref[0], i, j)),
                  pl.BlockSpec((3, br, bc), lambda i, j, where_ref: (0, i, j))] + extra_specs,
        out_specs=pl.BlockSpec((br, bc), lambda i, j, where_ref: (where_ref[1] * nb + i, k * ncb + j)))
    return pl.pallas_call(
        body, name=name, grid_spec=grid_spec,
        out_shape=jax.ShapeDtypeStruct((2 * h, n * c), F32), input_output_aliases=aliases,
        compiler_params=_params(("parallel", "parallel")),
    )(where, p, r2, *extra)


def _adamw_math(w, g, m, v):
    m2 = ADAM_B1 * m + (1.0 - ADAM_B1) * g
    v2 = ADAM_B2 * v + (1.0 - ADAM_B2) * (g * g)
    m_hat = m2 / (1.0 - ADAM_B1 ** ADAM_STEP)
    v_hat = v2 / (1.0 - ADAM_B2 ** ADAM_STEP)
    delta = -ADAM_LR * (m_hat / (jnp.sqrt(v_hat) + ADAM_EPS) + ADAM_WD * w)
    return delta, m2, v2


def _adamw(w, g, m, v, name, after=None):
    r, c = w.shape
    br, bc = _blk(r, 128, 8), _blk(c, 2048)

    def body(w_ref, g_ref, m_ref, v_ref, go_ref, d_ref, m2_ref, v2_ref):
        gv = g_ref[...]
        d, m2, v2 = _adamw_math(w_ref[...], gv, m_ref[...], v_ref[...])
        go_ref[...] = gv
        d_ref[...] = d
        m2_ref[...] = m2
        v2_ref[...] = v2

    spec = pl.BlockSpec((br, bc), lambda i, j: (i, j))
    body, extra_specs, extra = _ordered(body, 4, after)
    return pl.pallas_call(
        body, name=name, grid=(r // br, c // bc),
        in_specs=[spec] * 4 + extra_specs, out_specs=[spec] * 4,
        out_shape=[jax.ShapeDtypeStruct((r, c), F32)] * 4,
        compiler_params=_params(("parallel", "parallel")),
    )(w, g, m, v, *extra)


def _sum_rows8(g, name, after=None):
    n = g.shape[1]

    def body(g_ref, o_ref):
        acc = g_ref[0:1, :]
        for k in range(1, 8):
            acc = acc + g_ref[k:k + 1, :]
        o_ref[...] = acc

    body, extra_specs, extra = _ordered(body, 1, after)
    return pl.pallas_call(
        body, name=name, out_shape=jax.ShapeDtypeStruct((1, n), F32),
        in_specs=[pl.BlockSpec(memory_space=pltpu.VMEM)] + extra_specs,
        out_specs=pl.BlockSpec(memory_space=pltpu.VMEM),
        compiler_params=_params(),
    )(g, *extra)


def _mm(a, b, mode, out_dtype, name, bm=1024, bn=None, after=None, col_blocks=None):
    if mode == "nn":
        (_, m, k), (g, _, n) = a.shape, b.shape
    elif mode == "tn":
        (_, k, m), (g, _, n) = a.shape, b.shape
    else:
        (g, m, k), (_, n, _) = a.shape, b.shape
    if bn is None:
        bn = 1024 if k <= 2048 else 512
    bm, bn = _blk(m, bm), _blk(n, bn)

    if mode == "nt":
        def body(a_ref, b_ref, o_ref, acc_ref):
            part = lax.dot_general(a_ref[...], b_ref[...], (((1,), (1,)), ((), ())),
                                   preferred_element_type=F32)
            if g == 1:
                o_ref[...] = part.astype(out_dtype)
            else:
                gi = pl.program_id(2)

                @pl.when(gi == 0)
                def _():
                    acc_ref[...] = part

                @pl.when(gi > 0)
                def _():
                    acc_ref[...] += part

                @pl.when(gi == g - 1)
                def _():
                    o_ref[...] = acc_ref[...].astype(out_dtype)

        body, extra_specs, extra = _ordered(body, 2, after)
        return pl.pallas_call(
            body, name=name, grid=(m // bm, n // bn, g),
            in_specs=[pl.BlockSpec((None, bm, k), lambda i, j, gi: (gi, i, 0)),
                      pl.BlockSpec((None, bn, k), lambda i, j, gi: (gi, j, 0))] + extra_specs,
            out_specs=pl.BlockSpec((None, bm, bn), lambda i, j, gi: (0, i, j)),
            out_shape=jax.ShapeDtypeStruct((1, m, n), out_dtype),
            scratch_shapes=[pltpu.VMEM((bm, bn), F32)],
            compiler_params=_params(("parallel", "parallel", "arbitrary")),
        )(a, b, *extra)

    contract = (((1,), (0,)), ((), ())) if mode == "nn" else (((0,), (0,)), ((), ()))

    def body(a_ref, b_ref, o_ref):
        o_ref[...] = lax.dot_general(a_ref[...], b_ref[...], contract,
                                     preferred_element_type=F32).astype(out_dtype)

    a_spec = (pl.BlockSpec((None, bm, k), lambda i, gi, j: (0, i, 0)) if mode == "nn"
              else pl.BlockSpec((None, k, bm), lambda i, gi, j: (0, 0, i)))
    first, count = (0, n // bn) if col_blocks is None else col_blocks
    body, extra_specs, extra = _ordered(body, 2, after)
    return pl.pallas_call(
        body, name=name, grid=(m // bm, g, count),
        in_specs=[a_spec, pl.BlockSpec((None, k, bn), lambda i, gi, j: (gi, 0, first + j))] + extra_specs,
        out_specs=pl.BlockSpec((None, bm, bn), lambda i, gi, j: (gi, i, j)),
        out_shape=jax.ShapeDtypeStruct((g, m, count * bn), out_dtype),
        compiler_params=_params(("parallel", "parallel", "parallel")),
    )(a, b, *extra)


def _mm_plane(a, b, plane, name, planes_so_far=None, after=None, bm=1024, bn=1024):
    (m, k), (g, _, n) = a.shape, b.shape
    bm, bn = _blk(m, bm), _blk(n, bn)

    def body(plane_ref, a_ref, b_ref, *rest):
        rest[-1][...] = jnp.dot(a_ref[...], b_ref[...], preferred_element_type=F32).astype(BF16)

    extra_specs, extra, aliases = [], [], {}
    if planes_so_far is not None:
        extra_specs.append(ANY)
        extra.append(planes_so_far)
        aliases = {3: 0}
    if after is not None:
        extra_specs.append(ANY)
        extra.append(after)
    grid_spec = pltpu.PrefetchScalarGridSpec(
        num_scalar_prefetch=1, grid=(m // bm, n // bn),
        in_specs=[pl.BlockSpec((bm, k), lambda i, j, p: (i, 0)),
                  pl.BlockSpec((None, k, bn), lambda i, j, p: (p[0], 0, j))] + extra_specs,
        out_specs=pl.BlockSpec((None, bm, bn), lambda i, j, p: (p[0], i, j)))
    return pl.pallas_call(
        body, name=name, grid_spec=grid_spec,
        out_shape=jax.ShapeDtypeStruct((g, m, n), BF16), input_output_aliases=aliases,
        compiler_params=_params(("parallel", "parallel")),
    )(plane, a, b, *extra)


def _row_specs(br, d):
    return (pl.BlockSpec((br, d), lambda i: (i, 0)), pl.BlockSpec((1, d), lambda i: (0, 0)),
            pl.BlockSpec((8, d), lambda i: (0, 0)))


def _rstd(xv):
    return lax.rsqrt(jnp.mean(xv * xv, axis=-1, keepdims=True) + RMS_EPS)


def _colsum(v):
    return jnp.sum(v, axis=0, keepdims=True)


def _norm_mod_fwd(x, g, scale, shift, name, o=None, gate=None, after=None):
    s, d = x.shape
    br = _blk(s, 256, 8)
    has_res = o is not None
    row, vec, _ = _row_specs(br, d)

    def body(*refs):
        if has_res:
            x_ref, o_ref, gate_ref, g_ref, sc_ref, sh_ref, x1_ref, h_ref = refs
            xv = x_ref[...] + gate_ref[...] * o_ref[...]
            x1_ref[...] = xv
        else:
            x_ref, g_ref, sc_ref, sh_ref, h_ref = refs
            xv = x_ref[...]
        n = xv * _rstd(xv) * g_ref[...]
        h_ref[...] = (n * (1.0 + sc_ref[...]) + sh_ref[...]).astype(BF16)

    ins = [x] + ([o, gate] if has_res else []) + [g, scale, shift]
    in_specs = [row] + ([row, vec] if has_res else []) + [vec] * 3
    out_shape = ([jax.ShapeDtypeStruct((s, d), F32)] if has_res else []) + [jax.ShapeDtypeStruct((s, d), BF16)]
    body, extra_specs, extra = _ordered(body, len(ins), after)
    out = pl.pallas_call(
        body, name=name, grid=(s // br,), in_specs=in_specs + extra_specs, out_specs=[row] * len(out_shape),
        out_shape=out_shape, compiler_params=_params(("parallel",)),
    )(*ins, *extra)
    return out if has_res else out[0]


def _final_loss(x1, o1, gate1, final_g, tgt, name):
    s, d = x1.shape
    br = _blk(s, 256, 8)
    row, vec, acc = _row_specs(br, d)

    def body(x1_ref, o_ref, gate_ref, g_ref, t_ref, dx_ref, do_ref, acc_ref):
        @pl.when(pl.program_id(0) == 0)
        def _():
            acc_ref[...] = jnp.zeros_like(acc_ref)

        gate, o, g = gate_ref[...], o_ref[...], g_ref[...]
        x2 = x1_ref[...] + gate * o
        r = _rstd(x2)
        xh = x2 * r
        err = xh * g - t_ref[...]
        loss = 0.5 * _colsum(jnp.mean(err * err, axis=-1, keepdims=True))
        dout = err * (1.0 / d)
        dxh = dout * g
        dx2 = r * (dxh - xh * jnp.mean(dxh * xh, axis=-1, keepdims=True))
        dx_ref[...] = dx2
        do_ref[...] = (dx2 * gate).astype(BF16)
        acc_ref[0:1, :] += _colsum(dout * xh)
        acc_ref[1:2, :] += _colsum(dx2 * o)
        acc_ref[2:3, :] += jnp.broadcast_to(loss, (1, d))

    return pl.pallas_call(
        body, name=name, grid=(s // br,),
        in_specs=[row, row, vec, vec, row], out_specs=[row, row, acc],
        out_shape=[jax.ShapeDtypeStruct((s, d), F32), jax.ShapeDtypeStruct((s, d), BF16),
                   jax.ShapeDtypeStruct((8, d), F32)],
        compiler_params=_params(("arbitrary",)),
    )(x1, o1, gate1, final_g, tgt)


def _norm_mod_bwd(dh, x, g, scale, dx_next, name, o_prev=None, gate_prev=None, after=None):
    s, d = x.shape
    br = _blk(s, 256, 8)
    has_prev = o_prev is not None
    row, vec, acc = _row_specs(br, d)

    def body(*refs):
        if has_prev:
            dh_ref, x_ref, g_ref, sc_ref, dxn_ref, o_ref, gate_ref, dx_ref, do_ref, acc_ref = refs
        else:
            dh_ref, x_ref, g_ref, sc_ref, dxn_ref, dx_ref, acc_ref = refs

        @pl.when(pl.program_id(0) == 0)
        def _():
            acc_ref[...] = jnp.zeros_like(acc_ref)

        xv, gv, dhv = x_ref[...], g_ref[...], dh_ref[...]
        r = _rstd(xv)
        xh = xv * r
        acc_ref[0:1, :] += _colsum(dhv * (xh * gv))
        acc_ref[1:2, :] += _colsum(dhv)
        dn = dhv * (1.0 + sc_ref[...])
        acc_ref[2:3, :] += _colsum(dn * xh)
        dxh = dn * gv
        dx = dxn_ref[...] + r * (dxh - xh * jnp.mean(dxh * xh, axis=-1, keepdims=True))
        dx_ref[...] = dx
        if has_prev:
            acc_ref[3:4, :] += _colsum(dx * o_ref[...])
            do_ref[...] = (dx * gate_ref[...]).astype(BF16)

    ins = [dh, x, g, scale, dx_next] + ([o_prev, gate_prev] if has_prev else [])
    in_specs = [row, row, vec, vec, row] + ([row, vec] if has_prev else [])
    out_shape = [jax.ShapeDtypeStruct((s, d), F32)]
    out_specs = [row]
    if has_prev:
        out_shape.append(jax.ShapeDtypeStruct((s, d), BF16))
        out_specs.append(row)
    out_shape.append(jax.ShapeDtypeStruct((8, d), F32))
    out_specs.append(acc)
    body, extra_specs, extra = _ordered(body, len(ins), after)
    return pl.pallas_call(
        body, name=name, grid=(s // br,), in_specs=in_specs + extra_specs, out_specs=out_specs,
        out_shape=out_shape, compiler_params=_params(("arbitrary",)),
    )(*ins, *extra)


def _tiles(p):
    s, c = p.shape
    return p.reshape(s // SUBLANES, SUBLANES, c)


def _shift_down(p, k):
    if k == 0:
        return p
    r = pltpu.roll(_tiles(p), k, 1)
    before = jnp.concatenate([jnp.zeros_like(r[:1]), r[:-1]], axis=0)
    rows = lax.broadcasted_iota(jnp.int32, r.shape, 1)
    return jnp.where(rows >= k, r, before).reshape(p.shape)


def _shift_up(p, k):
    if k == 0:
        return p
    r = pltpu.roll(_tiles(p), SUBLANES - k, 1)
    after = jnp.concatenate([r[1:], jnp.zeros_like(r[:1])], axis=0)
    rows = lax.broadcasted_iota(jnp.int32, r.shape, 1)
    return jnp.where(rows < SUBLANES - k, r, after).reshape(p.shape)


def _sigmoid(z):
    return 0.5 * (jnp.tanh(0.5 * z) + 1.0)


def _sc_parts(proj_ref, w_ref):
    b, cg, v, g = (proj_ref[i].astype(F32) for i in range(4))
    p = cg * v
    u = w_ref[2:3, :] * p + w_ref[1:2, :] * _shift_down(p, 1) + w_ref[0:1, :] * _shift_down(p, 2)
    return b, cg, v, g, p, u


def _sc_fwd(proj, conv_w, name):
    _, s, e = proj.shape
    bc = _blk(e, 256)

    def body(proj_ref, w_ref, y_ref):
        b, _, _, g, _, u = _sc_parts(proj_ref, w_ref)
        y_ref[...] = (b * u * (g * _sigmoid(g))).astype(BF16)

    return pl.pallas_call(
        body, name=name, grid=(e // bc,),
        in_specs=[pl.BlockSpec((4, s, bc), lambda j: (0, 0, j)), pl.BlockSpec((3, bc), lambda j: (0, j))],
        out_specs=pl.BlockSpec((s, bc), lambda j: (0, j)),
        out_shape=jax.ShapeDtypeStruct((s, e), BF16),
        compiler_params=_params(("parallel",)),
    )(proj, conv_w)


def _sc_bwd(proj, dy, conv_w, name, after=None):
    _, s, e = proj.shape
    bc = _blk(e, 256)

    def body(proj_ref, dy_ref, w_ref, dp_ref, dw_ref):
        b, cg, v, g, p, u = _sc_parts(proj_ref, w_ref)
        dyv = dy_ref[...].astype(F32)
        sig = _sigmoid(g)
        t = dyv * (g * sig)
        du = t * b
        dp_ref[0] = (t * u).astype(BF16)
        dp_ref[3] = (dyv * b * u * (sig * (1.0 + g * (1.0 - sig)))).astype(BF16)
        dpp = w_ref[2:3, :] * du + w_ref[1:2, :] * _shift_up(du, 1) + w_ref[0:1, :] * _shift_up(du, 2)
        dp_ref[1] = (dpp * v).astype(BF16)
        dp_ref[2] = (dpp * cg).astype(BF16)
        dw_ref[2:3, :] = _colsum(du * p)
        dw_ref[1:2, :] = _colsum(du * _shift_down(p, 1))
        dw_ref[0:1, :] = _colsum(du * _shift_down(p, 2))

    body, extra_specs, extra = _ordered(body, 3, after)
    return pl.pallas_call(
        body, name=name, grid=(e // bc,),
        in_specs=[pl.BlockSpec((4, s, bc), lambda j: (0, 0, j)), pl.BlockSpec((s, bc), lambda j: (0, j)),
                  pl.BlockSpec((3, bc), lambda j: (0, j))] + extra_specs,
        out_specs=[pl.BlockSpec((4, s, bc), lambda j: (0, 0, j)), pl.BlockSpec((3, bc), lambda j: (0, j))],
        out_shape=[jax.ShapeDtypeStruct((4, s, e), BF16), jax.ShapeDtypeStruct((3, e), F32)],
        compiler_params=_params(("parallel",)),
    )(proj, dy, conv_w, *extra)


def _softplus_neg(lam):
    u = jnp.exp(-jnp.abs(lam))
    w = 1.0 + u
    log1p = jnp.where(w == 1.0, u, jnp.log(w) * (u / jnp.where(w == 1.0, 1.0, w - 1.0)))
    return jnp.maximum(-lam, 0.0) + log1p


def _one_minus_exp(z):
    series = -z * (1.0 + z * (0.5 + z * (1.0 / 6.0 + z * (1.0 / 24.0))))
    return jnp.where(z > -0.02, series, 1.0 - jnp.exp(z))


def _scan_in_tiles(a, b, reverse):
    shape = a.shape
    a, b = _tiles(a), _tiles(b)
    rows = lax.broadcasted_iota(jnp.int32, a.shape, 1)
    for step in (1, 2, 4):
        shift = SUBLANES - step if reverse else step
        ok = rows < SUBLANES - step if reverse else rows >= step
        a_s, b_s = pltpu.roll(a, shift, 1), pltpu.roll(b, shift, 1)
        b = jnp.where(ok, a * b_s + b, b)
        a = jnp.where(ok, a * a_s, a)
    return a.reshape(shape), b.reshape(shape)


def _by_rows(fn, arrays, rows=32):
    s = arrays[0].shape[0]
    rows = min(rows, s)
    for t in range(0, s, rows):
        fn(t, *(a[t:t + rows] for a in arrays))


def _scan_carry(a_ref, b_ref, h_ref, reverse):
    s, c = a_ref.shape
    n = s // 8

    def step(i, carry):
        gi = n - 1 - i if reverse else i
        sl = pl.ds(pl.multiple_of(gi * 8, 8), 8)
        h = b_ref[sl, :] + a_ref[sl, :] * carry
        h_ref[sl, :] = h
        return h[0:1, :] if reverse else h[7:8, :]

    lax.fori_loop(0, n, step, jnp.zeros((1, c), F32), unroll=8 if n % 8 == 0 else 1)


def _lru_specs(s, e_half, n_heads):
    hp = e_half // HEAD_DIM
    c = HEAD_DIM
    return dict(
        pair=pl.BlockSpec((2, None, s, c), lambda h: (0, h // hp, 0, h % hp)),
        conv_w=pl.BlockSpec((4, c), lambda h: (0, h)),
        chan=pl.BlockSpec((1, c), lambda h: (0, h)),
        w=pl.BlockSpec((4, None, c // 4, c), lambda h: (0, h, 0, 0)),
        bias=pl.BlockSpec((None, 1, c), lambda h: (h, 0, 0)),
        plane=pl.BlockSpec((s, c), lambda h: (0, h)),
    )


def _lru_gate_inputs(vp, cw_ref, cb_ref, wa_ref, ba_ref, wx_ref, bx_ref):
    c = HEAD_DIM
    taps = [_shift_down(vp, 3 - k) for k in range(4)]
    v = cb_ref[...] + sum(cw_ref[k:k + 1, :] * taps[k] for k in range(4))
    vb = v.astype(BF16)
    wa = wa_ref[...].reshape(c, c)
    wx = wx_ref[...].reshape(c, c)
    zr = jnp.dot(vb, wa, preferred_element_type=F32) + ba_ref[...]
    zi = jnp.dot(vb, wx, preferred_element_type=F32) + bx_ref[...]
    return taps, v, vb, wa, wx, zr, zi


def _lru_fwd(proj, conv_w, conv_b, w_a, b_a, w_x, b_x, lam, name):
    _, _, s, e_half = proj.shape
    n_heads = 2 * e_half // HEAD_DIM
    sp_ = _lru_specs(s, e_half, n_heads)

    def body(pg_ref, cw_ref, cb_ref, wa_ref, ba_ref, wx_ref, bx_ref, lam_ref,
             y_ref, a_ref, hs_ref, sa_ref, sb_ref, sh_ref):
        _, v, _, _, _, zr, zi = _lru_gate_inputs(
            pg_ref[0].astype(F32), cw_ref, cb_ref, wa_ref, ba_ref, wx_ref, bx_ref)
        rate = (-RGLRU_C) * _softplus_neg(lam_ref[...])

        def decay_and_input(t, v_c, zr_c, zi_c):
            la = rate * _sigmoid(zr_c)
            a = jnp.exp(la)
            b = jnp.sqrt(_one_minus_exp(2.0 * la)) * (_sigmoid(zi_c) * v_c)
            a_ref[t:t + a.shape[0]] = a
            sa_ref[t:t + a.shape[0]], sb_ref[t:t + a.shape[0]] = _scan_in_tiles(a, b, reverse=False)

        _by_rows(decay_and_input, [v, zr, zi])
        _scan_carry(sa_ref, sb_ref, sh_ref, reverse=False)

        def gated_output(t, hs_c, g_c):
            g = g_c.astype(F32)
            y_ref[t:t + g.shape[0]] = (hs_c * (g * _sigmoid(g))).astype(BF16)
            hs_ref[t:t + g.shape[0]] = hs_c.astype(BF16)

        _by_rows(gated_output, [sh_ref, pg_ref.at[1]])

    e = 2 * e_half
    return pl.pallas_call(
        body, name=name, grid=(n_heads,),
        in_specs=[sp_["pair"], sp_["conv_w"], sp_["chan"], sp_["w"], sp_["bias"], sp_["w"],
                  sp_["bias"], sp_["chan"]],
        out_specs=[sp_["plane"]] * 3,
        out_shape=[jax.ShapeDtypeStruct((s, e), BF16), jax.ShapeDtypeStruct((s, e), F32),
                   jax.ShapeDtypeStruct((s, e), BF16)],
        scratch_shapes=[pltpu.VMEM((s, HEAD_DIM), F32)] * 3,
        compiler_params=_params(("parallel",)),
    )(proj, conv_w, conv_b, w_a, b_a, w_x, b_x, lam)


def _lru_bwd(proj, dy, saved, conv_w, conv_b, w_a, b_a, w_x, b_x, lam, name):
    _, _, s, e_half = proj.shape
    e = 2 * e_half
    c = HEAD_DIM
    n_heads = e // c
    sp_ = _lru_specs(s, e_half, n_heads)

    def body(pg_ref, dy_ref, a_ref, hs_ref, cw_ref, cb_ref, wa_ref, ba_ref, wx_ref, bx_ref, lam_ref,
             dpg_ref, dwa_ref, dwx_ref, dba_ref, dbx_ref, dlam_ref, dcw_ref, dcb_ref,
             sa_ref, sb_ref, sd_ref, dzr_ref, dzi_ref):
        taps, v, vb, wa, wx, zr, zi = _lru_gate_inputs(
            pg_ref[0].astype(F32), cw_ref, cb_ref, wa_ref, ba_ref, wx_ref, bx_ref)
        lam = lam_ref[...]
        rate = (-RGLRU_C) * _softplus_neg(lam)
        a = a_ref[...]

        def state_gradient_in_tiles(t, a_next, dy_c, g_c, hs_c):
            g, dyv = g_c.astype(F32), dy_c.astype(F32)
            sig = _sigmoid(g)
            rows = slice(t, t + g.shape[0])
            dpg_ref[1, rows] = (dyv * hs_c.astype(F32) * (sig * (1.0 + g * (1.0 - sig)))).astype(BF16)
            sa_ref[rows], sb_ref[rows] = _scan_in_tiles(a_next, dyv * (g * sig), reverse=True)

        _by_rows(state_gradient_in_tiles, [_shift_up(a, 1), dy_ref, pg_ref.at[1], hs_ref])
        _scan_carry(sa_ref, sb_ref, sd_ref, reverse=True)

        sums = []

        def gate_gradients(t, dh, hs_before, a_c, zr_c, zi_c, v_c):
            r, i = _sigmoid(zr_c), _sigmoid(zi_c)
            q = (1.0 - a_c) * (1.0 + a_c)
            inv_nm = lax.rsqrt(q)
            div = dh * (q * inv_nm)
            dla = (dh * hs_before) * a_c - (dh * (i * v_c)) * (a_c * a_c * inv_nm)
            dzr = (dla * rate) * (r * (1.0 - r))
            dzi = (div * v_c) * (i * (1.0 - i))
            rows = slice(t, t + dh.shape[0])
            dzr_ref[rows], dzi_ref[rows] = dzr.astype(BF16), dzi.astype(BF16)
            sa_ref[rows] = div * i
            sums.append((_colsum(dla * r), _colsum(dzr), _colsum(dzi)))

        _by_rows(gate_gradients, [sd_ref, _shift_down(hs_ref[...].astype(F32), 1), a_ref, zr, zi, v])
        dlam_ref[...] = sum(p[0] for p in sums) * ((-RGLRU_C) * (-_sigmoid(-lam)))
        dba_ref[...] = sum(p[1] for p in sums)
        dbx_ref[...] = sum(p[2] for p in sums)
        dzr_b, dzi_b = dzr_ref[...], dzi_ref[...]
        tn = (((0,), (0,)), ((), ()))
        nt = (((1,), (1,)), ((), ()))
        dwa_ref[...] = lax.dot_general(vb, dzr_b, tn, preferred_element_type=F32)
        dwx_ref[...] = lax.dot_general(vb, dzi_b, tn, preferred_element_type=F32)
        dv = (sa_ref[...] + lax.dot_general(dzr_b, wa, nt, preferred_element_type=F32)
              + lax.dot_general(dzi_b, wx, nt, preferred_element_type=F32))
        dcb_ref[...] = _colsum(dv)
        dvp = jnp.zeros_like(dv)
        for k in range(4):
            dvp = dvp + cw_ref[k:k + 1, :] * _shift_up(dv, 3 - k)
            dcw_ref[k:k + 1, :] = _colsum(dv * taps[k])
        dpg_ref[0] = dvp.astype(BF16)

    head_mat = pl.BlockSpec((None, c, c), lambda h: (h, 0, 0))
    outs = pl.pallas_call(
        body, name=name, grid=(n_heads,),
        in_specs=[sp_["pair"]] + [sp_["plane"]] * 3 + [sp_["conv_w"], sp_["chan"], sp_["w"], sp_["bias"],
                                                        sp_["w"], sp_["bias"], sp_["chan"]],
        out_specs=[sp_["pair"], head_mat, head_mat, sp_["bias"], sp_["bias"],
                   sp_["chan"], sp_["conv_w"], sp_["chan"]],
        out_shape=[jax.ShapeDtypeStruct((2, 2, s, e_half), BF16),
                   jax.ShapeDtypeStruct((n_heads, c, c), F32), jax.ShapeDtypeStruct((n_heads, c, c), F32),
                   jax.ShapeDtypeStruct((n_heads, 1, c), F32), jax.ShapeDtypeStruct((n_heads, 1, c), F32),
                   jax.ShapeDtypeStruct((1, e), F32), jax.ShapeDtypeStruct((4, e), F32),
                   jax.ShapeDtypeStruct((1, e), F32)],
        scratch_shapes=[pltpu.VMEM((s, c), F32)] * 3 + [pltpu.VMEM((s, c), BF16)] * 2,
        compiler_params=_params(("parallel",)),
    )(proj, dy, *saved, conv_w, conv_b, w_a, b_a, w_x, b_x, lam)
    return tuple(outs)


def _ada_fwd(c_all, ada_w, name):
    n_l, d, f = ada_w.shape
    bf = _blk(f, 512)

    def body(c_ref, w_ref, o_ref):
        cv = c_ref[...]
        sc = (cv * _sigmoid(cv)).astype(BF16)
        o_ref[...] = jnp.dot(sc, w_ref[...].astype(BF16), preferred_element_type=F32)

    return pl.pallas_call(
        body, name=name, grid=(n_l, f // bf),
        in_specs=[pl.BlockSpec((8, d), lambda l, j: (0, 0)), pl.BlockSpec((None, d, bf), lambda l, j: (l, 0, j))],
        out_specs=pl.BlockSpec((None, 8, bf), lambda l, j: (l, 0, j)),
        out_shape=jax.ShapeDtypeStruct((n_l, 8, f), F32),
        compiler_params=_params(("parallel", "parallel")),
    )(c_all, ada_w)


def _ada_bwd_adamw(c_t, dmod, w, m, v, name, after=None):
    n_l, d, f = w.shape
    bf = _blk(f, 256)

    def body(c_ref, dm_ref, w_ref, m_ref, v_ref, g_ref, d_ref, m2_ref, v2_ref):
        cv = c_ref[...]
        sc = cv * _sigmoid(cv)
        dm = dm_ref[...]
        g = sc[:, 0:1] * dm[0:1, :]
        for b in range(1, 8):
            g = g + sc[:, b:b + 1] * dm[b:b + 1, :]
        g_ref[...] = g
        dl, m2, v2 = _adamw_math(w_ref[...], g, m_ref[...], v_ref[...])
        d_ref[...] = dl
        m2_ref[...] = m2
        v2_ref[...] = v2

    big = pl.BlockSpec((None, d, bf), lambda l, j: (l, 0, j))
    body, extra_specs, extra = _ordered(body, 5, after)
    return pl.pallas_call(
        body, name=name, grid=(n_l, f // bf),
        in_specs=[pl.BlockSpec((d, 8), lambda l, j: (0, 0)), pl.BlockSpec((None, 8, bf), lambda l, j: (l, 0, j)),
                  big, big, big] + extra_specs,
        out_specs=[big] * 4, out_shape=[jax.ShapeDtypeStruct((n_l, d, f), F32)] * 4,
        compiler_params=_params(("parallel", "parallel")),
    )(c_t, dmod, w, m, v, *extra)


def _pack(parts):
    padded, offs, n = [], [], 0
    for p in parts:
        p = p.reshape(-1)
        size = -(-p.shape[0] // PACK) * PACK
        offs.append(n)
        n += size
        padded.append(jnp.pad(p, (0, size - p.shape[0])) if size != p.shape[0] else p)
    return jnp.concatenate(padded), offs, n


def kernel(x, c, norm_g, ada_w, ada_b, sc_w_in, sc_conv_w, sc_w_out, lru_w_in, lru_conv_w, lru_conv_b, lru_w_a, lru_b_a, lru_w_x, lru_b_x, lru_lambda, lru_w_out, final_g, loss_target, m_norm_g, m_ada_w, m_ada_b, m_sc_w_in, m_sc_conv_w, m_sc_w_out, m_lru_w_in, m_lru_conv_w, m_lru_conv_b, m_lru_w_a, m_lru_b_a, m_lru_w_x, m_lru_b_x, m_lru_lambda, m_lru_w_out, m_final_g, v_norm_g, v_ada_w, v_ada_b, v_sc_w_in, v_sc_conv_w, v_sc_w_out, v_lru_w_in, v_lru_conv_w, v_lru_conv_b, v_lru_w_a, v_lru_b_a, v_lru_w_x, v_lru_b_x, v_lru_lambda, v_lru_w_out, v_final_g):
    xi, yi, ci = _pos()
    chip = 2 * xi + yi
    batch = 4 * xi + 2 * yi + ci
    core_op = jnp.reshape(ci, (1,)).astype(jnp.int32)
    chip_op = jnp.reshape(chip, (1,)).astype(jnp.int32)
    where_op = jnp.stack([chip, ci]).astype(jnp.int32)

    x2d, tgt = x[0], loss_target[0]
    s, d = x2d.shape
    es = sc_conv_w.shape[2]
    e = 4 * es
    n_heads = lru_w_a.shape[1]
    hj = lru_b_a.shape[2]
    f = ada_w.shape[2]
    row = lambda t: t.reshape(1, -1)

    small_parts = [c, sc_conv_w, lru_conv_w, lru_conv_b, lru_b_a, lru_b_x, lru_lambda]
    small, offs, n_small = _pack(small_parts)
    got = _allgather8([small.reshape(8, n_small // 8)], "ag_small")[0].reshape(8, n_small)
    c_all = got[:, :d]
    per_chip = got[0::2]

    def chip_part(k, shape):
        size = 1
        for dim in shape:
            size *= dim
        return per_chip[:, offs[k]:offs[k] + size].reshape((4,) + shape)

    conv_w0 = jnp.transpose(chip_part(1, (3, es)), (1, 0, 2)).reshape(3, e)
    conv_w1 = jnp.transpose(chip_part(2, (4, es)), (1, 0, 2)).reshape(4, e)
    conv_b1 = chip_part(3, (es,)).reshape(1, e)
    b_a = jnp.transpose(chip_part(4, (n_heads, hj)), (1, 0, 2)).reshape(n_heads, 1, 4 * hj)
    b_x = jnp.transpose(chip_part(5, (n_heads, hj)), (1, 0, 2)).reshape(n_heads, 1, 4 * hj)
    lam = chip_part(6, (es,)).reshape(1, e)

    mod_nb = _ada_fwd(c_all, ada_w, "ada_fwd")
    mods = _allgather8([mod_nb.reshape(16, f)], "ag_mod")[0].reshape(8, 2, 8, f)[0::2]
    mine = lax.dynamic_index_in_dim(mods, batch, axis=2, keepdims=False)
    mod = jnp.transpose(mine, (1, 0, 2)).reshape(2, 4 * f) + ada_b
    shift = [row(mod[l, :d]) for l in range(2)]
    scale = [row(mod[l, d:2 * d]) for l in range(2)]
    gate = [row(mod[l, 2 * d:]) for l in range(2)]
    ng = [row(norm_g[l]) for l in range(2)]

    shards = [sc_w_in[0], sc_w_out[0], lru_w_in[0], lru_w_a[0].reshape(n_heads * hj, HEAD_DIM),
              lru_w_x[0].reshape(n_heads * hj, HEAD_DIM), lru_w_out[0]]
    names = ["sc_w_in", "sc_w_out", "lru_w_in", "lru_w_a", "lru_w_x", "lru_w_out"]
    slots = [_cast_into_slot(chip_op, w, "cast_" + nm) for w, nm in zip(shards, names)]
    send_a, recv_a, buf, started = _first_start(slots[0], mod, "ag_first_start")
    h0 = _norm_mod_fwd(x2d, ng[0], scale[0], shift[0], "norm0", after=started)
    planes = [jnp.reshape(2 * px + py, (1,)).astype(jnp.int32) for px, py in _other_chips(xi, yi)]
    proj0 = _mm_plane(h0, buf.reshape(4, d, e), chip_op, "sc_in_own")
    relays, passed, relayed = [], [], [proj0] + slots[1:]
    for j in range(2):
        send_b, recv_b, send_c, recv_c, buf, token = _first_relay(send_a, recv_a, buf, j, relayed,
                                                                  "ag_first_relay_%d" % j)
        relays.append((send_c, recv_c))
        passed.append((send_b, recv_b))
        relayed = [token]
    rest_flight, relayed = _gather_start([[slots[1]], [slots[2]], slots[3:5], [slots[5]]], token, "ag_start")
    for j in range(2):
        buf = _first_relay_done(*passed[j], buf, j, relayed, "ag_first_relay_done_%d" % j)
        proj0 = _mm_plane(h0, buf.reshape(4, d, e), planes[j], "sc_in_%d" % j, planes_so_far=proj0)
        relayed = proj0
    send_b, recv_b, buf, relayed = _first_diagonal(relays, buf, proj0, "ag_first_diagonal")
    buf = _first_diagonal_done(send_b, recv_b, relays, buf, relayed, "ag_first_diagonal_done")
    proj0 = _mm_plane(h0, buf.reshape(4, d, e), planes[2], "sc_in_2", planes_so_far=proj0)
    w_in0 = buf.reshape(4, d, e)
    in_flight = [None] + rest_flight

    def arrived(g, after, tag):
        send1, recv1, bufs = in_flight[g]
        send2, recv2, bufs, passed = _gather_forward(send1, recv1, bufs, after, "ag_forward_" + tag)
        return _gather_finish(send2, recv2, bufs, passed, "ag_finish_" + tag)

    y0 = _sc_fwd(proj0, conv_w0, "sc_mix")
    w_out0 = arrived(1, y0, "sc_w_out")[0].reshape(1, e, d)
    o0 = _mm(y0[None], w_out0, "nn", F32, "sc_out")[0]
    x1, h1 = _norm_mod_fwd(x2d, ng[1], scale[1], shift[1], "norm1", o=o0, gate=gate[0])
    send1, recv1, bufs = in_flight[2]
    proj1 = _mm_plane(h1, bufs[0].reshape(4, d, e // 2), chip_op, "lru_in_own")
    for j in range(3):
        send2, recv2, bufs, passed_on = _gather_forward(send1, recv1, bufs, proj1, "ag_forward_lru_w_in_%d" % j,
                                                        sources=(j,))
        bufs = _gather_finish(send2, recv2, bufs, passed_on, "ag_finish_lru_w_in_%d" % j, sources=(j,))
        proj1 = _mm_plane(h1, bufs[0].reshape(4, d, e // 2), planes[j], "lru_in_%d" % j, planes_so_far=proj1)
    w_in1 = bufs[0].reshape(4, d, e // 2)
    gate_ws = arrived(3, proj1, "lru_gates")
    w_a = gate_ws[0].reshape(4, n_heads, hj, HEAD_DIM)
    w_x = gate_ws[1].reshape(4, n_heads, hj, HEAD_DIM)
    pairs1 = proj1.reshape(2, 2, s, e // 2)
    y1, *lru_saved = _lru_fwd(pairs1, conv_w1, conv_b1, w_a, b_a, w_x, b_x, lam, "lru_mix")
    w_out1 = arrived(4, y1, "lru_w_out")[0].reshape(1, e, d)
    o1 = _mm(y1[None], w_out1, "nn", F32, "lru_out")[0]
    dx2, do1, acc_f = _final_loss(x1, o1, gate[1], row(final_g), tgt, "final_loss")

    def reduce_stage1(tag, grads):
        lands = [lax.empty((4, g.shape[1] // 2, g.shape[2]), F32) for g in grads]
        return _exchange_start("rs_sibling_start_" + tag, grads, lands, len(grads),
                               _plan_other_half_to_sibling, None)

    def reduce_stage2(tag, stage1, nms, after):
        send, recv, grads, lands, _ = stage1
        grads, lands = _exchange_wait("rs_sibling_wait_" + tag, send, recv, grads, lands,
                                      _plan_other_half_to_sibling, after)
        parts = [_sum_own_and_sibling(core_op, g, r1, "rs_sum1_" + nm) for g, r1, nm in zip(grads, lands, nms)]
        lands = [lax.empty((3,) + p.shape[1:], BF16) for p in parts]
        return _exchange_start("rs_chips_start_" + tag, parts, lands, 3 * len(parts),
                               _plan_partials_to_chips, None)

    def reduce_stage3(tag, stage2, nms, after):
        send, recv, parts, lands, _ = stage2
        parts, lands = _exchange_wait("rs_chips_wait_" + tag, send, recv, parts, lands,
                                      _plan_partials_to_chips, after)
        halves = [_sum_chips(where_op, p, r2, "rs_sum2_" + nm) for p, r2, nm in zip(parts, lands, nms)]
        return _exchange_start("rs_share_start_" + tag, [], halves, len(halves), _plan_share_half, None)

    def reduce_done(tag, stage3, after):
        send, recv, _, fulls, _ = stage3
        return _exchange_wait("rs_share_wait_" + tag, send, recv, [], fulls, _plan_share_half, after)[1]

    def head_major_to_chip_major(t):
        return jnp.transpose(t.reshape(n_heads, 4, hj, HEAD_DIM), (1, 0, 2, 3)).reshape(4, n_heads * hj, HEAD_DIM)

    big_state = dict(zip(names, zip(shards, [m_sc_w_in, m_sc_w_out, m_lru_w_in, m_lru_w_a, m_lru_w_x, m_lru_w_out],
                                    [v_sc_w_in, v_sc_w_out, v_lru_w_in, v_lru_w_a, v_lru_w_x, v_lru_w_out],
                                    [sc_w_in, sc_w_out, lru_w_in, lru_w_a, lru_w_x, lru_w_out])))
    big = {}

    def update(nms, fulls, after):
        for nm, g2 in zip(nms, fulls):
            w2, m4, v4, w4 = big_state[nm]
            outs = _adamw(w2, g2, m4.reshape(w2.shape), v4.reshape(w2.shape), "adamw_" + nm, after=after)
            big[nm] = tuple(t.reshape(w4.shape) for t in outs)
            after = outs[1]
        return after

    g_w_out1 = _mm(y1[None], do1[None], "tn", F32, "lru_out_dw", bm=512, bn=2048)
    dy1 = _mm(do1[None], w_out1, "nt", BF16, "lru_out_dx")[0]
    dpairs1, g_wa, g_wx, g_ba, g_bx, g_lam, g_cw1, g_cb1 = _lru_bwd(
        pairs1, dy1, lru_saved, conv_w1, conv_b1, w_a, b_a, w_x, b_x, lam, "lru_mix_bwd")
    dproj1 = dpairs1.reshape(4, s, e // 2)
    g_w_in1 = _mm(h1[None], dproj1, "tn", F32, "lru_in_dw", bm=1024, bn=2048)
    lru_names = ["lru_w_out", "lru_w_a", "lru_w_x", "lru_w_in"]
    lru_rs = reduce_stage1("lru", [g_w_out1.reshape(4, es, d), head_major_to_chip_major(g_wa),
                                   head_major_to_chip_major(g_wx), g_w_in1])
    dh1 = _mm(dproj1, w_in1, "nt", F32, "lru_in_dx", after=lru_rs[4])[0]
    lru_rs = reduce_stage2("lru", lru_rs, lru_names, dh1)
    dx1, do0, acc1 = _norm_mod_bwd(dh1, x1, ng[1], scale[1], dx2, "norm1_bwd", o_prev=o0, gate_prev=gate[0],
                                   after=lru_rs[4])

    g_w_out0 = _mm(y0[None], do0[None], "tn", F32, "sc_out_dw", bm=512, bn=2048)
    out_rs = reduce_stage1("sc_out", [g_w_out0.reshape(4, es, d)])
    dy0 = _mm(do0[None], w_out0, "nt", BF16, "sc_out_dx", after=out_rs[4])[0]
    out_rs = reduce_stage2("sc_out", out_rs, ["sc_w_out"], dy0)
    dproj0, g_cw0 = _sc_bwd(proj0, dy0, conv_w0, "sc_mix_bwd", after=out_rs[4])
    stage1, stage2, after = [], [], None
    for k in range(IN_PIECES):
        g_piece = _mm(h0[None], dproj0, "tn", F32, "sc_in_dw_%d" % k, bm=1024, bn=e // IN_PIECES,
                      col_blocks=(k, 1), after=after)
        stage1.append(reduce_stage1("sc_in_%d" % k, [g_piece]))
        after = stage1[k][4]
        if k:
            stage2.append(reduce_stage2("sc_in_%d" % (k - 1), stage1[k - 1], ["sc_w_in_%d" % (k - 1)], after))
            after = stage2[k - 1][4]
    lru_rs = reduce_stage3("lru", lru_rs, lru_names, after)
    k = IN_PIECES - 1
    stage2.append(reduce_stage2("sc_in_%d" % k, stage1[k], ["sc_w_in_%d" % k], lru_rs[4]))
    dh0 = _mm(dproj0, w_in0, "nt", F32, "sc_in_dx", after=stage2[k][4])[0]
    grad_x, acc0 = _norm_mod_bwd(dh0, x2d, ng[0], scale[0], dx1, "norm0_bwd")
    out_rs = reduce_stage3("sc_out", out_rs, ["sc_w_out"], acc0)
    last = update(lru_names, reduce_done("lru", lru_rs, out_rs[4]), None)
    last = update(["sc_w_out"], reduce_done("sc_out", out_rs, last), None)
    g_half = None
    for k, (send, recv, parts, lands, _) in enumerate(stage2):
        parts, lands = _exchange_wait("rs_chips_wait_sc_in_%d" % k, send, recv, parts, lands,
                                      _plan_partials_to_chips, last)
        g_half = _sum_chips(where_op, parts[0], lands[0], "rs_sum2_sc_w_in_%d" % k, piece=(k, IN_PIECES),
                            so_far=g_half)
    in_rs = _exchange_start("rs_share_start_sc_in", [], [g_half], 1, _plan_share_half, None)

    dmod = jnp.stack([jnp.concatenate([acc0[1], acc0[0], acc1[3]]),
                      jnp.concatenate([acc1[1], acc1[0], acc_f[1]])])
    part_list = [jnp.stack([acc0[2], acc1[2]]), acc_f[0], acc_f[2, :1], g_cw0, g_cw1, g_cb1, g_ba, g_bx,
                 g_lam, dmod]
    partials, poffs, n_part = _pack(part_list)
    every = _allgather8([partials.reshape(8, n_part // 8)], "ag_partials", after=in_rs[4])[0].reshape(8, n_part)
    total = _sum_rows8(every, "sum_partials")[0]

    def tot(k, shape):
        size = 1
        for dim in shape:
            size *= dim
        return total[poffs[k]:poffs[k] + size].reshape(shape)

    def my_cols(t, width):
        return lax.dynamic_slice_in_dim(t, chip * width, width, axis=t.ndim - 1)

    loss = tot(2, (1,))[0]
    g_norm_g, g_final_g, g_ada_b = tot(0, (2, d)), tot(1, (d,)), tot(9, (2, 3 * d))
    g_sc_conv_w = my_cols(tot(3, (3, e)), es)[None]
    g_lru_conv_w = my_cols(tot(4, (4, e)), es)[None]
    g_lru_conv_b = my_cols(tot(5, (1, e)), es)
    g_lru_b_a = my_cols(tot(6, (n_heads, 4 * hj)), hj)[None]
    g_lru_b_x = my_cols(tot(7, (n_heads, 4 * hj)), hj)[None]
    g_lru_lambda = my_cols(tot(8, (1, e)), es)

    dmod_all = every[:, poffs[9]:poffs[9] + 6 * d].reshape(8, 2, 3 * d)
    dmod_mine = jnp.transpose(my_cols(dmod_all, f), (1, 0, 2))
    ada = _ada_bwd_adamw(jnp.transpose(c_all), dmod_mine, ada_w, m_ada_w, v_ada_w, "ada_bwd_adamw", after=total)

    small_names = ["norm_g", "ada_b", "final_g", "sc_conv_w", "lru_conv_w", "lru_conv_b", "lru_b_a",
                   "lru_b_x", "lru_lambda"]
    small_w = [norm_g, ada_b, final_g, sc_conv_w, lru_conv_w, lru_conv_b, lru_b_a, lru_b_x, lru_lambda]
    small_g = [g_norm_g, g_ada_b, g_final_g, g_sc_conv_w, g_lru_conv_w, g_lru_conv_b, g_lru_b_a,
               g_lru_b_x, g_lru_lambda]
    small_m = [m_norm_g, m_ada_b, m_final_g, m_sc_conv_w, m_lru_conv_w, m_lru_conv_b, m_lru_b_a,
               m_lru_b_x, m_lru_lambda]
    small_v = [v_norm_g, v_ada_b, v_final_g, v_sc_conv_w, v_lru_conv_w, v_lru_conv_b, v_lru_b_a,
               v_lru_b_x, v_lru_lambda]
    pw, soffs, n_s = _pack(small_w)
    pg, pm, pv = _pack(small_g)[0], _pack(small_m)[0], _pack(small_v)[0]
    shape2 = (n_s // PACK, PACK)
    _, pd, pm2, pv2 = _adamw(pw.reshape(shape2), pg.reshape(shape2), pm.reshape(shape2), pv.reshape(shape2),
                             "adamw_small", after=ada[0])
    update(["sc_w_in"], reduce_done("sc_in", in_rs, pd), None)
    small = {}
    for k, (nm, w_) in enumerate(zip(small_names, small_w)):
        take = lambda t: t.reshape(-1)[soffs[k]:soffs[k] + w_.size].reshape(w_.shape)
        small[nm] = (small_g[k].reshape(w_.shape), take(pd), take(pm2), take(pv2))

    results = dict(small)
    results.update(big)
    results["ada_w"] = tuple(ada)
    order = ["norm_g", "ada_w", "ada_b", "sc_w_in", "sc_conv_w", "sc_w_out", "lru_w_in", "lru_conv_w",
             "lru_conv_b", "lru_w_a", "lru_b_a", "lru_w_x", "lru_b_x", "lru_lambda", "lru_w_out", "final_g"]
    out = [loss, grad_x[None]]
    for kind in range(4):
        out += [results[nm][kind] for nm in order]
    return tuple(out)
```

```python
import jax
import jax.numpy as jnp
from jax import lax
from jax.experimental import pallas as pl
from jax.experimental.pallas import tpu as pltpu

F32 = jnp.float32
BF16 = jnp.bfloat16
MESH = pl.DeviceIdType.MESH
ANY = pl.BlockSpec(memory_space=pl.ANY)

RMS_EPS = 1e-6
RGLRU_C = 8.0
HEAD_DIM = 256
ADAM_LR = 0.001
ADAM_B1 = 0.9
ADAM_B2 = 0.999
ADAM_EPS = 1e-08
ADAM_WD = 0.01
ADAM_STEP = 10
V7X_VMEM_LIMIT = 56 * 1024 * 1024
IN_PIECES = 2
LANES = 128
SUBLANES = 8
PACK = SUBLANES * LANES


def _blk(dim, pref, unit=LANES):
    if dim <= pref:
        return dim
    b = (pref // unit) * unit
    while b > unit and dim % b:
        b -= unit
    assert dim % b == 0, (dim, pref, unit)
    return b


def _params(sem=None):
    return pltpu.CompilerParams(dimension_semantics=sem, vmem_limit_bytes=V7X_VMEM_LIMIT)


def _ordered(body, n_in, after):
    if after is None:
        return body, [], []

    def ordered_body(*refs):
        return body(*refs[:n_in], *refs[n_in + 1:])

    return ordered_body, [ANY], [after]


def _pos():
    return lax.axis_index("x"), lax.axis_index("y"), lax.axis_index("c")


def _other_chips(x, y):
    return [(1 - x, y), (x, 1 - y), (1 - x, 1 - y)]


def _allgather8(arrs, name, after=None):
    n_t = len(arrs)
    ms = [a.shape[0] for a in arrs]

    def gather(*refs):
        ins, outs = refs[:n_t], refs[n_t:2 * n_t]
        send_sems, recv_sems, local_sems = refs[2 * n_t:]
        x, y, c = _pos()
        me, sibling = (x, y, c), (x, y, 1 - c)
        chips = _other_chips(x, y)

        def rows(t, px, py, pc):
            return outs[t].at[pl.ds((4 * px + 2 * py + pc) * ms[t], ms[t])]

        def copy(t, k, block, to, src=None):
            return pltpu.make_async_remote_copy(
                src_ref=rows(t, *block) if src is None else src, dst_ref=rows(t, *block),
                send_sem=send_sems.at[7 * t + k], recv_sem=recv_sems.at[7 * t + k],
                device_id=to, device_id_type=MESH)

        mine, first, passed = [], [], []
        for t in range(n_t):
            src = ins[t]
            cp = pltpu.make_async_copy(src, rows(t, *me), local_sems.at[t])
            cp.start()
            mine.append(cp)
            sends = [copy(t, 0, me, sibling, src=src)]
            sends += [copy(t, 1 + j, me, (*chip, c), src=src) for j, chip in enumerate(chips)]
            for cp in sends:
                cp.start()
            first += sends
        for t in range(n_t):
            for j, chip in enumerate(chips):
                copy(t, 1 + j, (*chip, c), me).wait_recv()
                cp = copy(t, 4 + j, (*chip, c), sibling)
                cp.start()
                passed.append(cp)
        for t in range(n_t):
            copy(t, 0, sibling, me).wait_recv()
            for j, chip in enumerate(chips):
                copy(t, 4 + j, (*chip, 1 - c), me).wait_recv()
        for cp in first + passed:
            cp.wait_send()
        for cp in mine:
            cp.wait()

    body, extra_specs, extra = _ordered(gather, n_t, after)
    return pl.pallas_call(
        body, name=name,
        out_shape=[jax.ShapeDtypeStruct((8 * m, a.shape[1]), a.dtype) for m, a in zip(ms, arrs)],
        in_specs=[ANY] * n_t + extra_specs, out_specs=[ANY] * n_t,
        scratch_shapes=[pltpu.SemaphoreType.DMA((7 * n_t,)), pltpu.SemaphoreType.DMA((7 * n_t,)),
                        pltpu.SemaphoreType.DMA((n_t,))],
    )(*arrs, *extra)


HBM = pl.BlockSpec(memory_space=pltpu.HBM)
SEM = pl.BlockSpec(memory_space=pltpu.SEMAPHORE)
TOKEN = pl.BlockSpec(memory_space=pltpu.VMEM)
IN_FLIGHT = pltpu.CompilerParams(has_side_effects=pltpu.SideEffectType.DATAFLOW_SIDE_EFFECTING)


def _in_hbm(arrs):
    return [pltpu.with_memory_space_constraint(a, pltpu.HBM) for a in arrs]


def _shard_rows(ref, h, px, py, pc):
    return ref.at[pl.ds((4 * px + 2 * py + pc) * h, h)]


def _gather_start(groups, after, name):
    bufs = [b for grp in groups for b in grp]
    n_t, n_g = len(bufs), len(groups)

    def body(*refs):
        sems, thru = refs[n_t + 1:n_t + 1 + 2 * n_g], refs[n_t + 1 + 2 * n_g:2 * n_t + 1 + 2 * n_g]
        token = refs[-1]
        x, y, c = _pos()
        t = 0
        for g, grp in enumerate(groups):
            for i in range(len(grp)):
                h = bufs[t].shape[0] // 8
                rows = _shard_rows(thru[t], h, x, y, c)
                for j, chip in enumerate(_other_chips(x, y)):
                    pltpu.make_async_remote_copy(
                        src_ref=rows, dst_ref=rows, send_sem=sems[2 * g].at[3 * i + j],
                        recv_sem=sems[2 * g + 1].at[3 * i + j], device_id=(*chip, c),
                        device_id_type=MESH).start()
                t += 1
        token[...] = jnp.zeros_like(token)

    sem_shapes = []
    for grp in groups:
        sem_shapes += [pltpu.SemaphoreType.DMA((3 * len(grp),))] * 2
    out = pl.pallas_call(
        body, name=name,
        out_shape=sem_shapes + [pltpu.HBM(b.shape, b.dtype) for b in bufs] + [jax.ShapeDtypeStruct((8, LANES), F32)],
        in_specs=[HBM] * n_t + [ANY], out_specs=[SEM] * (2 * n_g) + [HBM] * n_t + [TOKEN],
        input_output_aliases={t: 2 * n_g + t for t in range(n_t)},
        compiler_params=IN_FLIGHT,
    )(*_in_hbm(bufs), after)
    sems, thru, token = out[:2 * n_g], out[2 * n_g:2 * n_g + n_t], out[-1]
    per_group, t = [], 0
    for g, grp in enumerate(groups):
        per_group.append((sems[2 * g], sems[2 * g + 1], thru[t:t + len(grp)]))
        t += len(grp)
    return per_group, token


def _gather_forward(send_sems, recv_sems, bufs, after, name, sources=(0, 1, 2)):
    n_t = len(bufs)

    def body(*refs):
        ins = refs[:n_t]
        send1, recv1 = refs[n_t], refs[n_t + 1]
        send2, recv2 = refs[n_t + 3], refs[n_t + 4]
        token = refs[-1]
        x, y, c = _pos()
        chips = _other_chips(x, y)
        for t in range(n_t):
            h = bufs[t].shape[0] // 8
            mine = _shard_rows(ins[t], h, x, y, c)
            for j in sources:
                chip = chips[j]
                landed = _shard_rows(ins[t], h, *chip, c)
                pltpu.make_async_remote_copy(
                    src_ref=mine, dst_ref=landed, send_sem=send1.at[3 * t + j], recv_sem=recv1.at[3 * t + j],
                    device_id=(*chip, c), device_id_type=MESH).wait_recv()
                pltpu.make_async_remote_copy(
                    src_ref=landed, dst_ref=landed, send_sem=send2.at[3 * t + j], recv_sem=recv2.at[3 * t + j],
                    device_id=(x, y, 1 - c), device_id_type=MESH).start()
        for t in range(n_t):
            h = bufs[t].shape[0] // 8
            mine = _shard_rows(ins[t], h, x, y, c)
            for j in sources:
                pltpu.make_async_remote_copy(
                    src_ref=mine, dst_ref=mine, send_sem=send1.at[3 * t + j], recv_sem=recv1.at[3 * t + j],
                    device_id=(*chips[j], c), device_id_type=MESH).wait_send()
        token[...] = jnp.zeros_like(token)

    out = pl.pallas_call(
        body, name=name,
        out_shape=[pltpu.SemaphoreType.DMA((3 * n_t,))] * 2 + [pltpu.HBM(b.shape, b.dtype) for b in bufs]
        + [jax.ShapeDtypeStruct((8, LANES), F32)],
        in_specs=[HBM] * n_t + [SEM, SEM, ANY], out_specs=[SEM, SEM] + [HBM] * n_t + [TOKEN],
        input_output_aliases={t: 2 + t for t in range(n_t)},
        compiler_params=IN_FLIGHT,
    )(*bufs, send_sems, recv_sems, after)
    return out[0], out[1], out[2:2 + n_t], out[-1]


def _gather_finish(send_sems, recv_sems, bufs, after, name, sources=(0, 1, 2)):
    n_t = len(bufs)

    def body(*refs):
        ins = refs[:n_t]
        send2, recv2 = refs[n_t], refs[n_t + 1]
        x, y, c = _pos()
        chips = _other_chips(x, y)
        for t in range(n_t):
            h = bufs[t].shape[0] // 8
            for j in sources:
                chip = chips[j]
                sent = _shard_rows(ins[t], h, *chip, c)
                got = _shard_rows(ins[t], h, *chip, 1 - c)
                cp = pltpu.make_async_remote_copy(
                    src_ref=sent, dst_ref=got, send_sem=send2.at[3 * t + j], recv_sem=recv2.at[3 * t + j],
                    device_id=(x, y, 1 - c), device_id_type=MESH)
                cp.wait_send()
                cp.wait_recv()

    return pl.pallas_call(
        body, name=name, out_shape=[pltpu.HBM(b.shape, b.dtype) for b in bufs],
        in_specs=[HBM] * n_t + [SEM, SEM, ANY], out_specs=[HBM] * n_t,
        input_output_aliases={t: t for t in range(n_t)},
        compiler_params=IN_FLIGHT,
    )(*bufs, send_sems, recv_sems, after)


def _part_rows(ref, h, px, py, pc, which):
    return ref.at[pl.ds((4 * px + 2 * py + pc) * h + which * (h // 2), h // 2)]


def _remote(src, dst, send_sem, recv_sem, device):
    return pltpu.make_async_remote_copy(src_ref=src, dst_ref=dst, send_sem=send_sem, recv_sem=recv_sem,
                                        device_id=device, device_id_type=MESH)


def _first_start(buf, after, name):
    h = buf.shape[0] // 8

    def body(buf_in, after_ref, send, recv, thru, token):
        x, y, c = _pos()
        rows = _shard_rows(thru, h, x, y, c)
        for j, chip in enumerate(_other_chips(x, y)[:2]):
            _remote(rows, rows, send.at[j], recv.at[j], (*chip, c)).start()
        token[...] = jnp.zeros_like(token)

    return pl.pallas_call(
        body, name=name,
        out_shape=[pltpu.SemaphoreType.DMA((2,))] * 2 + [pltpu.HBM(buf.shape, buf.dtype),
                                                         jax.ShapeDtypeStruct((8, LANES), F32)],
        in_specs=[HBM, ANY], out_specs=[SEM, SEM, HBM, TOKEN], input_output_aliases={0: 2},
        compiler_params=IN_FLIGHT,
    )(*_in_hbm([buf]), after)


def _first_relay(send_a, recv_a, buf, j, after, name):
    h = buf.shape[0] // 8

    def body(buf_in, send_a, recv_a, *rest):
        send_b, recv_b, send_c, recv_c, _, token = rest[len(after):]
        x, y, c = _pos()
        chips = _other_chips(x, y)
        nbr, other = chips[j], chips[1 - j]
        mine, landed = _shard_rows(buf_in, h, x, y, c), _shard_rows(buf_in, h, *nbr, c)
        _remote(mine, landed, send_a.at[j], recv_a.at[j], (*nbr, c)).wait_recv()
        _remote(landed, landed, send_b.at[0], recv_b.at[0], (x, y, 1 - c)).start()
        part = _part_rows(buf_in, h, *nbr, c, j)
        _remote(part, part, send_c.at[0], recv_c.at[0], (*other, c)).start()
        _remote(mine, mine, send_a.at[j], recv_a.at[j], (*nbr, c)).wait_send()
        token[...] = jnp.zeros_like(token)

    out = pl.pallas_call(
        body, name=name,
        out_shape=[pltpu.SemaphoreType.DMA((1,))] * 4 + [pltpu.HBM(buf.shape, buf.dtype),
                                                         jax.ShapeDtypeStruct((8, LANES), F32)],
        in_specs=[HBM, SEM, SEM] + [ANY] * len(after), out_specs=[SEM] * 4 + [HBM, TOKEN],
        input_output_aliases={0: 4}, compiler_params=IN_FLIGHT,
    )(buf, send_a, recv_a, *after)
    return tuple(out)


def _first_relay_done(send_b, recv_b, buf, j, after, name):
    h = buf.shape[0] // 8

    def body(buf_in, send_b, recv_b, after_ref, thru):
        x, y, c = _pos()
        nbr = _other_chips(x, y)[j]
        cp = _remote(_shard_rows(buf_in, h, *nbr, c), _shard_rows(buf_in, h, *nbr, 1 - c),
                     send_b.at[0], recv_b.at[0], (x, y, 1 - c))
        cp.wait_send()
        cp.wait_recv()

    return pl.pallas_call(
        body, name=name, out_shape=pltpu.HBM(buf.shape, buf.dtype),
        in_specs=[HBM, SEM, SEM, ANY], out_specs=HBM, input_output_aliases={0: 0},
        compiler_params=IN_FLIGHT,
    )(buf, send_b, recv_b, after)


def _first_diagonal(relays, buf, after, name):
    h = buf.shape[0] // 8

    def body(buf_in, send_c0, recv_c0, send_c1, recv_c1, after_ref, send_b, recv_b, thru, token):
        x, y, c = _pos()
        chips = _other_chips(x, y)
        diag = chips[2]
        for j, (send_c, recv_c) in enumerate(((send_c0, recv_c0), (send_c1, recv_c1))):
            part = _part_rows(buf_in, h, *diag, c, j)
            _remote(part, part, send_c.at[0], recv_c.at[0], (*chips[1 - j], c)).wait_recv()
        whole = _shard_rows(buf_in, h, *diag, c)
        _remote(whole, whole, send_b.at[0], recv_b.at[0], (x, y, 1 - c)).start()
        token[...] = jnp.zeros_like(token)

    out = pl.pallas_call(
        body, name=name,
        out_shape=[pltpu.SemaphoreType.DMA((1,))] * 2 + [pltpu.HBM(buf.shape, buf.dtype),
                                                         jax.ShapeDtypeStruct((8, LANES), F32)],
        in_specs=[HBM] + [SEM] * 4 + [ANY], out_specs=[SEM, SEM, HBM, TOKEN], input_output_aliases={0: 2},
        compiler_params=IN_FLIGHT,
    )(buf, relays[0][0], relays[0][1], relays[1][0], relays[1][1], after)
    return tuple(out)


def _first_diagonal_done(send_b, recv_b, relays, buf, after, name):
    h = buf.shape[0] // 8

    def body(buf_in, send_b, recv_b, send_c0, recv_c0, send_c1, recv_c1, after_ref, thru):
        x, y, c = _pos()
        chips = _other_chips(x, y)
        diag = chips[2]
        cp = _remote(_shard_rows(buf_in, h, *diag, c), _shard_rows(buf_in, h, *diag, 1 - c),
                     send_b.at[0], recv_b.at[0], (x, y, 1 - c))
        cp.wait_send()
        cp.wait_recv()
        for j, (send_c, recv_c) in enumerate(((send_c0, recv_c0), (send_c1, recv_c1))):
            part = _part_rows(buf_in, h, *chips[j], c, j)
            _remote(part, part, send_c.at[0], recv_c.at[0], (*chips[1 - j], c)).wait_send()

    return pl.pallas_call(
        body, name=name, out_shape=pltpu.HBM(buf.shape, buf.dtype),
        in_specs=[HBM] + [SEM] * 6 + [ANY], out_specs=HBM, input_output_aliases={0: 0},
        compiler_params=IN_FLIGHT,
    )(buf, send_b, recv_b, relays[0][0], relays[0][1], relays[1][0], relays[1][1], after)


def _exchange_start(name, srcs, lands, n_copies, plan, after):
    ns, nl = len(srcs), len(lands)
    extra = [] if after is None else [after]

    def body(*refs):
        base = ns + nl + len(extra)
        send_sems, recv_sems = refs[base], refs[base + 1]
        src_refs, land_refs = refs[base + 2:base + 2 + ns], refs[base + 2 + ns:base + 2 + ns + nl]
        token = refs[-1]
        x, y, c = _pos()
        copies = plan(src_refs, land_refs, x, y, c)
        assert len(copies) == n_copies
        for k, (src, dst, dev) in enumerate(copies):
            pltpu.make_async_remote_copy(
                src_ref=src, dst_ref=dst, send_sem=send_sems.at[k], recv_sem=recv_sems.at[k],
                device_id=dev, device_id_type=MESH).start()
        token[...] = jnp.zeros_like(token)

    out = pl.pallas_call(
        body, name=name,
        out_shape=[pltpu.SemaphoreType.DMA((n_copies,))] * 2
        + [pltpu.HBM(a.shape, a.dtype) for a in list(srcs) + list(lands)] + [jax.ShapeDtypeStruct((8, LANES), F32)],
        in_specs=[HBM] * (ns + nl) + [ANY] * len(extra), out_specs=[SEM, SEM] + [HBM] * (ns + nl) + [TOKEN],
        input_output_aliases={i: 2 + i for i in range(ns + nl)},
        compiler_params=IN_FLIGHT,
    )(*_in_hbm(list(srcs) + list(lands)), *extra)
    return out[0], out[1], out[2:2 + ns], out[2 + ns:2 + ns + nl], out[-1]


def _exchange_wait(name, send_sems, recv_sems, srcs, lands, plan, after):
    ns, nl = len(srcs), len(lands)

    def body(*refs):
        src_refs, land_refs = refs[:ns], refs[ns:ns + nl]
        send, recv = refs[ns + nl], refs[ns + nl + 1]
        x, y, c = _pos()
        for k, (src, dst, dev) in enumerate(plan(src_refs, land_refs, x, y, c)):
            cp = pltpu.make_async_remote_copy(
                src_ref=src, dst_ref=dst, send_sem=send.at[k], recv_sem=recv.at[k],
                device_id=dev, device_id_type=MESH)
            cp.wait_send()
            cp.wait_recv()

    out = pl.pallas_call(
        body, name=name, out_shape=[pltpu.HBM(a.shape, a.dtype) for a in list(srcs) + list(lands)],
        in_specs=[HBM] * (ns + nl) + [SEM, SEM, ANY], out_specs=[HBM] * (ns + nl),
        input_output_aliases={i: i for i in range(ns + nl)},
        compiler_params=IN_FLIGHT,
    )(*srcs, *lands, send_sems, recv_sems, after)
    return out[:ns], out[ns:]


def _plan_other_half_to_sibling(src_refs, land_refs, x, y, c):
    out = []
    for g_ref, r_ref in zip(src_refs, land_refs):
        h = g_ref.shape[1] // 2
        out.append((g_ref.at[:, pl.ds((1 - c) * h, h), :], r_ref, (x, y, 1 - c)))
    return out


def _plan_partials_to_chips(src_refs, land_refs, x, y, c):
    out = []
    for p_ref, r_ref in zip(src_refs, land_refs):
        for j, (px, py) in enumerate(_other_chips(x, y)):
            out.append((p_ref.at[2 * px + py], r_ref.at[j], (px, py, c)))
    return out


def _plan_share_half(src_refs, land_refs, x, y, c):
    out = []
    for f_ref in land_refs:
        h = f_ref.shape[0] // 2
        rows = f_ref.at[pl.ds(c * h, h)]
        out.append((rows, rows, (x, y, 1 - c)))
    return out


def _cast_into_slot(chip, w, name):
    r, c = w.shape
    br, bc = _blk(r, 512, 8), _blk(c, 2048)
    nb = r // br

    def body(chip_ref, w_ref, o_ref):
        o_ref[...] = w_ref[...].astype(BF16)

    grid_spec = pltpu.PrefetchScalarGridSpec(
        num_scalar_prefetch=1, grid=(nb, c // bc),
        in_specs=[pl.BlockSpec((br, bc), lambda i, j, chip_ref: (i, j))],
        out_specs=pl.BlockSpec((br, bc), lambda i, j, chip_ref: (chip_ref[0] * nb + i, j)))
    return pl.pallas_call(
        body, name=name, grid_spec=grid_spec,
        out_shape=jax.ShapeDtypeStruct((4 * r, c), BF16),
        compiler_params=_params(("parallel", "parallel")),
    )(chip, w)


def _sum_own_and_sibling(core, g, r1, name):
    _, r, c = g.shape
    h = r // 2
    br, bc = _blk(h, 256, 8), _blk(c, 2048)
    nb = h // br

    def body(core_ref, g_ref, r_ref, o_ref):
        o_ref[...] = (g_ref[...] + r_ref[...]).astype(BF16)

    grid_spec = pltpu.PrefetchScalarGridSpec(
        num_scalar_prefetch=1, grid=(4, nb, c // bc),
        in_specs=[pl.BlockSpec((None, br, bc), lambda k, i, j, core_ref: (k, core_ref[0] * nb + i, j)),
                  pl.BlockSpec((None, br, bc), lambda k, i, j, core_ref: (k, i, j))],
        out_specs=pl.BlockSpec((None, br, bc), lambda k, i, j, core_ref: (k, i, j)))
    return pl.pallas_call(
        body, name=name, grid_spec=grid_spec,
        out_shape=jax.ShapeDtypeStruct((4, h, c), BF16),
        compiler_params=_params(("parallel", "parallel", "parallel")),
    )(core, g, r1)


def _sum_chips(where, p, r2, name, piece=(0, 1), so_far=None):
    _, h, c = p.shape
    k, n = piece
    br, bc = _blk(h, 256, 8), _blk(c, 2048)
    nb, ncb = h // br, c // bc

    def body(where_ref, p_ref, r_ref, *rest):
        acc = p_ref[...].astype(F32)
        for j in range(3):
            acc = acc + r_ref[j].astype(F32)
        rest[-1][...] = acc

    extra_specs, extra, aliases = ([], [], {}) if so_far is None else ([ANY], [so_far], {3: 0})
    grid_spec = pltpu.PrefetchScalarGridSpec(
        num_scalar_prefetch=1, grid=(nb, ncb),
        in_specs=[pl.BlockSpec((None, br, bc), lambda i, j, where_ref: (where_ref[0], i, j)),
                  pl.BlockSpec((3, br, bc), lambda i, j, where_ref: (0, i, j))] + extra_specs,
        out_specs=pl.BlockSpec((br, bc), lambda i, j, where_ref: (where_ref[1] * nb + i, k * ncb + j)))
    return pl.pallas_call(
        body, name=name, grid_spec=grid_spec,
        out_shape=jax.ShapeDtypeStruct((2 * h, n * c), F32), input_output_aliases=aliases,
        compiler_params=_params(("parallel", "parallel")),
    )(where, p, r2, *extra)


def _adamw_math(w, g, m, v):
    m2 = ADAM_B1 * m + (1.0 - ADAM_B1) * g
    v2 = ADAM_B2 * v + (1.0 - ADAM_B2) * (g * g)
    m_hat = m2 / (1.0 - ADAM_B1 ** ADAM_STEP)
    v_hat = v2 / (1.0 - ADAM_B2 ** ADAM_STEP)
    delta = -ADAM_LR * (m_hat / (jnp.sqrt(v_hat) + ADAM_EPS) + ADAM_WD * w)
    return delta, m2, v2


def _adamw(w, g, m, v, name, after=None):
    r, c = w.shape
    br, bc = _blk(r, 128, 8), _blk(c, 2048)

    def body(w_ref, g_ref, m_ref, v_ref, go_ref, d_ref, m2_ref, v2_ref):
        gv = g_ref[...]
        d, m2, v2 = _adamw_math(w_ref[...], gv, m_ref[...], v_ref[...])
        go_ref[...] = gv
        d_ref[...] = d
        m2_ref[...] = m2
        v2_ref[...] = v2

    spec = pl.BlockSpec((br, bc), lambda i, j: (i, j))
    body, extra_specs, extra = _ordered(body, 4, after)
    return pl.pallas_call(
        body, name=name, grid=(r // br, c // bc),
        in_specs=[spec] * 4 + extra_specs, out_specs=[spec] * 4,
        out_shape=[jax.ShapeDtypeStruct((r, c), F32)] * 4,
        compiler_params=_params(("parallel", "parallel")),
    )(w, g, m, v, *extra)


def _sum_rows8(g, name, after=None):
    n = g.shape[1]

    def body(g_ref, o_ref):
        acc = g_ref[0:1, :]
        for k in range(1, 8):
            acc = acc + g_ref[k:k + 1, :]
        o_ref[...] = acc

    body, extra_specs, extra = _ordered(body, 1, after)
    return pl.pallas_call(
        body, name=name, out_shape=jax.ShapeDtypeStruct((1, n), F32),
        in_specs=[pl.BlockSpec(memory_space=pltpu.VMEM)] + extra_specs,
        out_specs=pl.BlockSpec(memory_space=pltpu.VMEM),
        compiler_params=_params(),
    )(g, *extra)


def _mm(a, b, mode, out_dtype, name, bm=1024, bn=None, after=None, col_blocks=None):
    if mode == "nn":
        (_, m, k), (g, _, n) = a.shape, b.shape
    elif mode == "tn":
        (_, k, m), (g, _, n) = a.shape, b.shape
    else:
        (g, m, k), (_, n, _) = a.shape, b.shape
    if bn is None:
        bn = 1024 if k <= 2048 else 512
    bm, bn = _blk(m, bm), _blk(n, bn)

    if mode == "nt":
        def body(a_ref, b_ref, o_ref, acc_ref):
            part = lax.dot_general(a_ref[...], b_ref[...], (((1,), (1,)), ((), ())),
                                   preferred_element_type=F32)
            if g == 1:
                o_ref[...] = part.astype(out_dtype)
            else:
                gi = pl.program_id(2)

                @pl.when(gi == 0)
                def _():
                    acc_ref[...] = part

                @pl.when(gi > 0)
                def _():
                    acc_ref[...] += part

                @pl.when(gi == g - 1)
                def _():
                    o_ref[...] = acc_ref[...].astype(out_dtype)

        body, extra_specs, extra = _ordered(body, 2, after)
        return pl.pallas_call(
            body, name=name, grid=(m // bm, n // bn, g),
            in_specs=[pl.BlockSpec((None, bm, k), lambda i, j, gi: (gi, i, 0)),
                      pl.BlockSpec((None, bn, k), lambda i, j, gi: (gi, j, 0))] + extra_specs,
            out_specs=pl.BlockSpec((None, bm, bn), lambda i, j, gi: (0, i, j)),
            out_shape=jax.ShapeDtypeStruct((1, m, n), out_dtype),
            scratch_shapes=[pltpu.VMEM((bm, bn), F32)],
            compiler_params=_params(("parallel", "parallel", "arbitrary")),
        )(a, b, *extra)

    contract = (((1,), (0,)), ((), ())) if mode == "nn" else (((0,), (0,)), ((), ()))

    def body(a_ref, b_ref, o_ref):
        o_ref[...] = lax.dot_general(a_ref[...], b_ref[...], contract,
                                     preferred_element_type=F32).astype(out_dtype)

    a_spec = (pl.BlockSpec((None, bm, k), lambda i, gi, j: (0, i, 0)) if mode == "nn"
              else pl.BlockSpec((None, k, bm), lambda i, gi, j: (0, 0, i)))
    first, count = (0, n // bn) if col_blocks is None else col_blocks
    body, extra_specs, extra = _ordered(body, 2, after)
    return pl.pallas_call(
        body, name=name, grid=(m // bm, g, count),
        in_specs=[a_spec, pl.BlockSpec((None, k, bn), lambda i, gi, j: (gi, 0, first + j))] + extra_specs,
        out_specs=pl.BlockSpec((None, bm, bn), lambda i, gi, j: (gi, i, j)),
        out_shape=jax.ShapeDtypeStruct((g, m, count * bn), out_dtype),
        compiler_params=_params(("parallel", "parallel", "parallel")),
    )(a, b, *extra)


def _mm_plane(a, b, plane, name, planes_so_far=None, after=None, bm=1024, bn=1024):
    (m, k), (g, _, n) = a.shape, b.shape
    bm, bn = _blk(m, bm), _blk(n, bn)

    def body(plane_ref, a_ref, b_ref, *rest):
        rest[-1][...] = jnp.dot(a_ref[...], b_ref[...], preferred_element_type=F32).astype(BF16)

    extra_specs, extra, aliases = [], [], {}
    if planes_so_far is not None:
        extra_specs.append(ANY)
        extra.append(planes_so_far)
        aliases = {3: 0}
    if after is not None:
        extra_specs.append(ANY)
        extra.append(after)
    grid_spec = pltpu.PrefetchScalarGridSpec(
        num_scalar_prefetch=1, grid=(m // bm, n // bn),
        in_specs=[pl.BlockSpec((bm, k), lambda i, j, p: (i, 0)),
                  pl.BlockSpec((None, k, bn), lambda i, j, p: (p[0], 0, j))] + extra_specs,
        out_specs=pl.BlockSpec((None, bm, bn), lambda i, j, p: (p[0], i, j)))
    return pl.pallas_call(
        body, name=name, grid_spec=grid_spec,
        out_shape=jax.ShapeDtypeStruct((g, m, n), BF16), input_output_aliases=aliases,
        compiler_params=_params(("parallel", "parallel")),
    )(plane, a, b, *extra)


def _row_specs(br, d):
    return (pl.BlockSpec((br, d), lambda i: (i, 0)), pl.BlockSpec((1, d), lambda i: (0, 0)),
            pl.BlockSpec((8, d), lambda i: (0, 0)))


def _rstd(xv):
    return lax.rsqrt(jnp.mean(xv * xv, axis=-1, keepdims=True) + RMS_EPS)


def _colsum(v):
    return jnp.sum(v, axis=0, keepdims=True)


def _norm_mod_fwd(x, g, scale, shift, name, o=None, gate=None, after=None):
    s, d = x.shape
    br = _blk(s, 256, 8)
    has_res = o is not None
    row, vec, _ = _row_specs(br, d)

    def body(*refs):
        if has_res:
            x_ref, o_ref, gate_ref, g_ref, sc_ref, sh_ref, x1_ref, h_ref = refs
            xv = x_ref[...] + gate_ref[...] * o_ref[...]
            x1_ref[...] = xv
        else:
            x_ref, g_ref, sc_ref, sh_ref, h_ref = refs
            xv = x_ref[...]
        n = xv * _rstd(xv) * g_ref[...]
        h_ref[...] = (n * (1.0 + sc_ref[...]) + sh_ref[...]).astype(BF16)

    ins = [x] + ([o, gate] if has_res else []) + [g, scale, shift]
    in_specs = [row] + ([row, vec] if has_res else []) + [vec] * 3
    out_shape = ([jax.ShapeDtypeStruct((s, d), F32)] if has_res else []) + [jax.ShapeDtypeStruct((s, d), BF16)]
    body, extra_specs, extra = _ordered(body, len(ins), after)
    out = pl.pallas_call(
        body, name=name, grid=(s // br,), in_specs=in_specs + extra_specs, out_specs=[row] * len(out_shape),
        out_shape=out_shape, compiler_params=_params(("parallel",)),
    )(*ins, *extra)
    return out if has_res else out[0]


def _final_loss(x1, o1, gate1, final_g, tgt, name):
    s, d = x1.shape
    br = _blk(s, 256, 8)
    row, vec, acc = _row_specs(br, d)

    def body(x1_ref, o_ref, gate_ref, g_ref, t_ref, dx_ref, do_ref, acc_ref):
        @pl.when(pl.program_id(0) == 0)
        def _():
            acc_ref[...] = jnp.zeros_like(acc_ref)

        gate, o, g = gate_ref[...], o_ref[...], g_ref[...]
        x2 = x1_ref[...] + gate * o
        r = _rstd(x2)
        xh = x2 * r
        err = xh * g - t_ref[...]
        loss = 0.5 * _colsum(jnp.mean(err * err, axis=-1, keepdims=True))
        dout = err * (1.0 / d)
        dxh = dout * g
        dx2 = r * (dxh - xh * jnp.mean(dxh * xh, axis=-1, keepdims=True))
        dx_ref[...] = dx2
        do_ref[...] = (dx2 * gate).astype(BF16)
        acc_ref[0:1, :] += _colsum(dout * xh)
        acc_ref[1:2, :] += _colsum(dx2 * o)
        acc_ref[2:3, :] += jnp.broadcast_to(loss, (1, d))

    return pl.pallas_call(
        body, name=name, grid=(s // br,),
        in_specs=[row, row, vec, vec, row], out_specs=[row, row, acc],
        out_shape=[jax.ShapeDtypeStruct((s, d), F32), jax.ShapeDtypeStruct((s, d), BF16),
                   jax.ShapeDtypeStruct((8, d), F32)],
        compiler_params=_params(("arbitrary",)),
    )(x1, o1, gate1, final_g, tgt)


def _norm_mod_bwd(dh, x, g, scale, dx_next, name, o_prev=None, gate_prev=None, after=None):
    s, d = x.shape
    br = _blk(s, 256, 8)
    has_prev = o_prev is not None
    row, vec, acc = _row_specs(br, d)

    def body(*refs):
        if has_prev:
            dh_ref, x_ref, g_ref, sc_ref, dxn_ref, o_ref, gate_ref, dx_ref, do_ref, acc_ref = refs
        else:
            dh_ref, x_ref, g_ref, sc_ref, dxn_ref, dx_ref, acc_ref = refs

        @pl.when(pl.program_id(0) == 0)
        def _():
            acc_ref[...] = jnp.zeros_like(acc_ref)

        xv, gv, dhv = x_ref[...], g_ref[...], dh_ref[...]
        r = _rstd(xv)
        xh = xv * r
        acc_ref[0:1, :] += _colsum(dhv * (xh * gv))
        acc_ref[1:2, :] += _colsum(dhv)
        dn = dhv * (1.0 + sc_ref[...])
        acc_ref[2:3, :] += _colsum(dn * xh)
        dxh = dn * gv
        dx = dxn_ref[...] + r * (dxh - xh * jnp.mean(dxh * xh, axis=-1, keepdims=True))
        dx_ref[...] = dx
        if has_prev:
            acc_ref[3:4, :] += _colsum(dx * o_ref[...])
            do_ref[...] = (dx * gate_ref[...]).astype(BF16)

    ins = [dh, x, g, scale, dx_next] + ([o_prev, gate_prev] if has_prev else [])
    in_specs = [row, row, vec, vec, row] + ([row, vec] if has_prev else [])
    out_shape = [jax.ShapeDtypeStruct((s, d), F32)]
    out_specs = [row]
    if has_prev:
        out_shape.append(jax.ShapeDtypeStruct((s, d), BF16))
        out_specs.append(row)
    out_shape.append(jax.ShapeDtypeStruct((8, d), F32))
    out_specs.append(acc)
    body, extra_specs, extra = _ordered(body, len(ins), after)
    return pl.pallas_call(
        body, name=name, grid=(s // br,), in_specs=in_specs + extra_specs, out_specs=out_specs,
        out_shape=out_shape, compiler_params=_params(("arbitrary",)),
    )(*ins, *extra)


def _tiles(p):
    s, c = p.shape
    return p.reshape(s // SUBLANES, SUBLANES, c)


def _shift_down(p, k):
    if k == 0:
        return p
    r = pltpu.roll(_tiles(p), k, 1)
    before = jnp.concatenate([jnp.zeros_like(r[:1]), r[:-1]], axis=0)
    rows = lax.broadcasted_iota(jnp.int32, r.shape, 1)
    return jnp.where(rows >= k, r, before).reshape(p.shape)


def _shift_up(p, k):
    if k == 0:
        return p
    r = pltpu.roll(_tiles(p), SUBLANES - k, 1)
    after = jnp.concatenate([r[1:], jnp.zeros_like(r[:1])], axis=0)
    rows = lax.broadcasted_iota(jnp.int32, r.shape, 1)
    return jnp.where(rows < SUBLANES - k, r, after).reshape(p.shape)


def _sigmoid(z):
    return 0.5 * (jnp.tanh(0.5 * z) + 1.0)


def _sc_parts(proj_ref, w_ref):
    b, cg, v, g = (proj_ref[i].astype(F32) for i in range(4))
    p = cg * v
    u = w_ref[2:3, :] * p + w_ref[1:2, :] * _shift_down(p, 1) + w_ref[0:1, :] * _shift_down(p, 2)
    return b, cg, v, g, p, u


def _sc_fwd(proj, conv_w, name):
    _, s, e = proj.shape
    bc = _blk(e, 256)

    def body(proj_ref, w_ref, y_ref):
        b, _, _, g, _, u = _sc_parts(proj_ref, w_ref)
        y_ref[...] = (b * u * (g * _sigmoid(g))).astype(BF16)

    return pl.pallas_call(
        body, name=name, grid=(e // bc,),
        in_specs=[pl.BlockSpec((4, s, bc), lambda j: (0, 0, j)), pl.BlockSpec((3, bc), lambda j: (0, j))],
        out_specs=pl.BlockSpec((s, bc), lambda j: (0, j)),
        out_shape=jax.ShapeDtypeStruct((s, e), BF16),
        compiler_params=_params(("parallel",)),
    )(proj, conv_w)


def _sc_bwd(proj, dy, conv_w, name, after=None):
    _, s, e = proj.shape
    bc = _blk(e, 256)

    def body(proj_ref, dy_ref, w_ref, dp_ref, dw_ref):
        b, cg, v, g, p, u = _sc_parts(proj_ref, w_ref)
        dyv = dy_ref[...].astype(F32)
        sig = _sigmoid(g)
        t = dyv * (g * sig)
        du = t * b
        dp_ref[0] = (t * u).astype(BF16)
        dp_ref[3] = (dyv * b * u * (sig * (1.0 + g * (1.0 - sig)))).astype(BF16)
        dpp = w_ref[2:3, :] * du + w_ref[1:2, :] * _shift_up(du, 1) + w_ref[0:1, :] * _shift_up(du, 2)
        dp_ref[1] = (dpp * v).astype(BF16)
        dp_ref[2] = (dpp * cg).astype(BF16)
        dw_ref[2:3, :] = _colsum(du * p)
        dw_ref[1:2, :] = _colsum(du * _shift_down(p, 1))
        dw_ref[0:1, :] = _colsum(du * _shift_down(p, 2))

    body, extra_specs, extra = _ordered(body, 3, after)
    return pl.pallas_call(
        body, name=name, grid=(e // bc,),
        in_specs=[pl.BlockSpec((4, s, bc), lambda j: (0, 0, j)), pl.BlockSpec((s, bc), lambda j: (0, j)),
                  pl.BlockSpec((3, bc), lambda j: (0, j))] + extra_specs,
        out_specs=[pl.BlockSpec((4, s, bc), lambda j: (0, 0, j)), pl.BlockSpec((3, bc), lambda j: (0, j))],
        out_shape=[jax.ShapeDtypeStruct((4, s, e), BF16), jax.ShapeDtypeStruct((3, e), F32)],
        compiler_params=_params(("parallel",)),
    )(proj, dy, conv_w, *extra)


def _softplus_neg(lam):
    u = jnp.exp(-jnp.abs(lam))
    w = 1.0 + u
    log1p = jnp.where(w == 1.0, u, jnp.log(w) * (u / jnp.where(w == 1.0, 1.0, w - 1.0)))
    return jnp.maximum(-lam, 0.0) + log1p


def _one_minus_exp(z):
    series = -z * (1.0 + z * (0.5 + z * (1.0 / 6.0 + z * (1.0 / 24.0))))
    return jnp.where(z > -0.02, series, 1.0 - jnp.exp(z))


def _scan_in_tiles(a, b, reverse):
    shape = a.shape
    a, b = _tiles(a), _tiles(b)
    rows = lax.broadcasted_iota(jnp.int32, a.shape, 1)
    for step in (1, 2, 4):
        shift = SUBLANES - step if reverse else step
        ok = rows < SUBLANES - step if reverse else rows >= step
        a_s, b_s = pltpu.roll(a, shift, 1), pltpu.roll(b, shift, 1)
        b = jnp.where(ok, a * b_s + b, b)
        a = jnp.where(ok, a * a_s, a)
    return a.reshape(shape), b.reshape(shape)


def _by_rows(fn, arrays, rows=32):
    s = arrays[0].shape[0]
    rows = min(rows, s)
    for t in range(0, s, rows):
        fn(t, *(a[t:t + rows] for a in arrays))


def _scan_carry(a_ref, b_ref, h_ref, reverse):
    s, c = a_ref.shape
    n = s // 8

    def step(i, carry):
        gi = n - 1 - i if reverse else i
        sl = pl.ds(pl.multiple_of(gi * 8, 8), 8)
        h = b_ref[sl, :] + a_ref[sl, :] * carry
        h_ref[sl, :] = h
        return h[0:1, :] if reverse else h[7:8, :]

    lax.fori_loop(0, n, step, jnp.zeros((1, c), F32), unroll=8 if n % 8 == 0 else 1)


def _lru_specs(s, e_half, n_heads):
    hp = e_half // HEAD_DIM
    c = HEAD_DIM
    return dict(
        pair=pl.BlockSpec((2, None, s, c), lambda h: (0, h // hp, 0, h % hp)),
        conv_w=pl.BlockSpec((4, c), lambda h: (0, h)),
        chan=pl.BlockSpec((1, c), lambda h: (0, h)),
        w=pl.BlockSpec((4, None, c // 4, c), lambda h: (0, h, 0, 0)),
        bias=pl.BlockSpec((None, 1, c), lambda h: (h, 0, 0)),
        plane=pl.BlockSpec((s, c), lambda h: (0, h)),
    )


def _lru_gate_inputs(vp, cw_ref, cb_ref, wa_ref, ba_ref, wx_ref, bx_ref):
    c = HEAD_DIM
    taps = [_shift_down(vp, 3 - k) for k in range(4)]
    v = cb_ref[...] + sum(cw_ref[k:k + 1, :] * taps[k] for k in range(4))
    vb = v.astype(BF16)
    wa = wa_ref[...].reshape(c, c)
    wx = wx_ref[...].reshape(c, c)
    zr = jnp.dot(vb, wa, preferred_element_type=F32) + ba_ref[...]
    zi = jnp.dot(vb, wx, preferred_element_type=F32) + bx_ref[...]
    return taps, v, vb, wa, wx, zr, zi


def _lru_fwd(proj, conv_w, conv_b, w_a, b_a, w_x, b_x, lam, name):
    _, _, s, e_half = proj.shape
    n_heads = 2 * e_half // HEAD_DIM
    sp_ = _lru_specs(s, e_half, n_heads)

    def body(pg_ref, cw_ref, cb_ref, wa_ref, ba_ref, wx_ref, bx_ref, lam_ref,
             y_ref, a_ref, hs_ref, sa_ref, sb_ref, sh_ref):
        _, v, _, _, _, zr, zi = _lru_gate_inputs(
            pg_ref[0].astype(F32), cw_ref, cb_ref, wa_ref, ba_ref, wx_ref, bx_ref)
        rate = (-RGLRU_C) * _softplus_neg(lam_ref[...])

        def decay_and_input(t, v_c, zr_c, zi_c):
            la = rate * _sigmoid(zr_c)
            a = jnp.exp(la)
            b = jnp.sqrt(_one_minus_exp(2.0 * la)) * (_sigmoid(zi_c) * v_c)
            a_ref[t:t + a.shape[0]] = a
            sa_ref[t:t + a.shape[0]], sb_ref[t:t + a.shape[0]] = _scan_in_tiles(a, b, reverse=False)

        _by_rows(decay_and_input, [v, zr, zi])
        _scan_carry(sa_ref, sb_ref, sh_ref, reverse=False)

        def gated_output(t, hs_c, g_c):
            g = g_c.astype(F32)
            y_ref[t:t + g.shape[0]] = (hs_c * (g * _sigmoid(g))).astype(BF16)
            hs_ref[t:t + g.shape[0]] = hs_c.astype(BF16)

        _by_rows(gated_output, [sh_ref, pg_ref.at[1]])

    e = 2 * e_half
    return pl.pallas_call(
        body, name=name, grid=(n_heads,),
        in_specs=[sp_["pair"], sp_["conv_w"], sp_["chan"], sp_["w"], sp_["bias"], sp_["w"],
                  sp_["bias"], sp_["chan"]],
        out_specs=[sp_["plane"]] * 3,
        out_shape=[jax.ShapeDtypeStruct((s, e), BF16), jax.ShapeDtypeStruct((s, e), F32),
                   jax.ShapeDtypeStruct((s, e), BF16)],
        scratch_shapes=[pltpu.VMEM((s, HEAD_DIM), F32)] * 3,
        compiler_params=_params(("parallel",)),
    )(proj, conv_w, conv_b, w_a, b_a, w_x, b_x, lam)


def _lru_bwd(proj, dy, saved, conv_w, conv_b, w_a, b_a, w_x, b_x, lam, name):
    _, _, s, e_half = proj.shape
    e = 2 * e_half
    c = HEAD_DIM
    n_heads = e // c
    sp_ = _lru_specs(s, e_half, n_heads)

    def body(pg_ref, dy_ref, a_ref, hs_ref, cw_ref, cb_ref, wa_ref, ba_ref, wx_ref, bx_ref, lam_ref,
             dpg_ref, dwa_ref, dwx_ref, dba_ref, dbx_ref, dlam_ref, dcw_ref, dcb_ref,
             sa_ref, sb_ref, sd_ref, dzr_ref, dzi_ref):
        taps, v, vb, wa, wx, zr, zi = _lru_gate_inputs(
            pg_ref[0].astype(F32), cw_ref, cb_ref, wa_ref, ba_ref, wx_ref, bx_ref)
        lam = lam_ref[...]
        rate = (-RGLRU_C) * _softplus_neg(lam)
        a = a_ref[...]

        def state_gradient_in_tiles(t, a_next, dy_c, g_c, hs_c):
            g, dyv = g_c.astype(F32), dy_c.astype(F32)
            sig = _sigmoid(g)
            rows = slice(t, t + g.shape[0])
            dpg_ref[1, rows] = (dyv * hs_c.astype(F32) * (sig * (1.0 + g * (1.0 - sig)))).astype(BF16)
            sa_ref[rows], sb_ref[rows] = _scan_in_tiles(a_next, dyv * (g * sig), reverse=True)

        _by_rows(state_gradient_in_tiles, [_shift_up(a, 1), dy_ref, pg_ref.at[1], hs_ref])
        _scan_carry(sa_ref, sb_ref, sd_ref, reverse=True)

        sums = []

        def gate_gradients(t, dh, hs_before, a_c, zr_c, zi_c, v_c):
            r, i = _sigmoid(zr_c), _sigmoid(zi_c)
            q = (1.0 - a_c) * (1.0 + a_c)
            inv_nm = lax.rsqrt(q)
            div = dh * (q * inv_nm)
            dla = (dh * hs_before) * a_c - (dh * (i * v_c)) * (a_c * a_c * inv_nm)
            dzr = (dla * rate) * (r * (1.0 - r))
            dzi = (div * v_c) * (i * (1.0 - i))
            rows = slice(t, t + dh.shape[0])
            dzr_ref[rows], dzi_ref[rows] = dzr.astype(BF16), dzi.astype(BF16)
            sa_ref[rows] = div * i
            sums.append((_colsum(dla * r), _colsum(dzr), _colsum(dzi)))

        _by_rows(gate_gradients, [sd_ref, _shift_down(hs_ref[...].astype(F32), 1), a_ref, zr, zi, v])
        dlam_ref[...] = sum(p[0] for p in sums) * ((-RGLRU_C) * (-_sigmoid(-lam)))
        dba_ref[...] = sum(p[1] for p in sums)
        dbx_ref[...] = sum(p[2] for p in sums)
        dzr_b, dzi_b = dzr_ref[...], dzi_ref[...]
        tn = (((0,), (0,)), ((), ()))
        nt = (((1,), (1,)), ((), ()))
        dwa_ref[...] = lax.dot_general(vb, dzr_b, tn, preferred_element_type=F32)
        dwx_ref[...] = lax.dot_general(vb, dzi_b, tn, preferred_element_type=F32)
        dv = (sa_ref[...] + lax.dot_general(dzr_b, wa, nt, preferred_element_type=F32)
              + lax.dot_general(dzi_b, wx, nt, preferred_element_type=F32))
        dcb_ref[...] = _colsum(dv)
        dvp = jnp.zeros_like(dv)
        for k in range(4):
            dvp = dvp + cw_ref[k:k + 1, :] * _shift_up(dv, 3 - k)
            dcw_ref[k:k + 1, :] = _colsum(dv * taps[k])
        dpg_ref[0] = dvp.astype(BF16)

    head_mat = pl.BlockSpec((None, c, c), lambda h: (h, 0, 0))
    outs = pl.pallas_call(
        body, name=name, grid=(n_heads,),
        in_specs=[sp_["pair"]] + [sp_["plane"]] * 3 + [sp_["conv_w"], sp_["chan"], sp_["w"], sp_["bias"],
                                                        sp_["w"], sp_["bias"], sp_["chan"]],
        out_specs=[sp_["pair"], head_mat, head_mat, sp_["bias"], sp_["bias"],
                   sp_["chan"], sp_["conv_w"], sp_["chan"]],
        out_shape=[jax.ShapeDtypeStruct((2, 2, s, e_half), BF16),
                   jax.ShapeDtypeStruct((n_heads, c, c), F32), jax.ShapeDtypeStruct((n_heads, c, c), F32),
                   jax.ShapeDtypeStruct((n_heads, 1, c), F32), jax.ShapeDtypeStruct((n_heads, 1, c), F32),
                   jax.ShapeDtypeStruct((1, e), F32), jax.ShapeDtypeStruct((4, e), F32),
                   jax.ShapeDtypeStruct((1, e), F32)],
        scratch_shapes=[pltpu.VMEM((s, c), F32)] * 3 + [pltpu.VMEM((s, c), BF16)] * 2,
        compiler_params=_params(("parallel",)),
    )(proj, dy, *saved, conv_w, conv_b, w_a, b_a, w_x, b_x, lam)
    return tuple(outs)


def _ada_fwd(c_all, ada_w, name):
    n_l, d, f = ada_w.shape
    bf = _blk(f, 512)

    def body(c_ref, w_ref, o_ref):
        cv = c_ref[...]
        sc = (cv * _sigmoid(cv)).astype(BF16)
        o_ref[...] = jnp.dot(sc, w_ref[...].astype(BF16), preferred_element_type=F32)

    return pl.pallas_call(
        body, name=name, grid=(n_l, f // bf),
        in_specs=[pl.BlockSpec((8, d), lambda l, j: (0, 0)), pl.BlockSpec((None, d, bf), lambda l, j: (l, 0, j))],
        out_specs=pl.BlockSpec((None, 8, bf), lambda l, j: (l, 0, j)),
        out_shape=jax.ShapeDtypeStruct((n_l, 8, f), F32),
        compiler_params=_params(("parallel", "parallel")),
    )(c_all, ada_w)


def _ada_bwd_adamw(c_t, dmod, w, m, v, name, after=None):
    n_l, d, f = w.shape
    bf = _blk(f, 256)

    def body(c_ref, dm_ref, w_ref, m_ref, v_ref, g_ref, d_ref, m2_ref, v2_ref):
        cv = c_ref[...]
        sc = cv * _sigmoid(cv)
        dm = dm_ref[...]
        g = sc[:, 0:1] * dm[0:1, :]
        for b in range(1, 8):
            g = g + sc[:, b:b + 1] * dm[b:b + 1, :]
        g_ref[...] = g
        dl, m2, v2 = _adamw_math(w_ref[...], g, m_ref[...], v_ref[...])
        d_ref[...] = dl
        m2_ref[...] = m2
        v2_ref[...] = v2

    big = pl.BlockSpec((None, d, bf), lambda l, j: (l, 0, j))
    body, extra_specs, extra = _ordered(body, 5, after)
    return pl.pallas_call(
        body, name=name, grid=(n_l, f // bf),
        in_specs=[pl.BlockSpec((d, 8), lambda l, j: (0, 0)), pl.BlockSpec((None, 8, bf), lambda l, j: (l, 0, j)),
                  big, big, big] + extra_specs,
        out_specs=[big] * 4, out_shape=[jax.ShapeDtypeStruct((n_l, d, f), F32)] * 4,
        compiler_params=_params(("parallel", "parallel")),
    )(c_t, dmod, w, m, v, *extra)


def _pack(parts):
    padded, offs, n = [], [], 0
    for p in parts:
        p = p.reshape(-1)
        size = -(-p.shape[0] // PACK) * PACK
        offs.append(n)
        n += size
        padded.append(jnp.pad(p, (0, size - p.shape[0])) if size != p.shape[0] else p)
    return jnp.concatenate(padded), offs, n


def kernel(x, c, norm_g, ada_w, ada_b, sc_w_in, sc_conv_w, sc_w_out, lru_w_in, lru_conv_w, lru_conv_b, lru_w_a, lru_b_a, lru_w_x, lru_b_x, lru_lambda, lru_w_out, final_g, loss_target, m_norm_g, m_ada_w, m_ada_b, m_sc_w_in, m_sc_conv_w, m_sc_w_out, m_lru_w_in, m_lru_conv_w, m_lru_conv_b, m_lru_w_a, m_lru_b_a, m_lru_w_x, m_lru_b_x, m_lru_lambda, m_lru_w_out, m_final_g, v_norm_g, v_ada_w, v_ada_b, v_sc_w_in, v_sc_conv_w, v_sc_w_out, v_lru_w_in, v_lru_conv_w, v_lru_conv_b, v_lru_w_a, v_lru_b_a, v_lru_w_x, v_lru_b_x, v_lru_lambda, v_lru_w_out, v_final_g):
    xi, yi, ci = _pos()
    chip = 2 * xi + yi
    batch = 4 * xi + 2 * yi + ci
    core_op = jnp.reshape(ci, (1,)).astype(jnp.int32)
    chip_op = jnp.reshape(chip, (1,)).astype(jnp.int32)
    where_op = jnp.stack([chip, ci]).astype(jnp.int32)

    x2d, tgt = x[0], loss_target[0]
    s, d = x2d.shape
    es = sc_conv_w.shape[2]
    e = 4 * es
    n_heads = lru_w_a.shape[1]
    hj = lru_b_a.shape[2]
    f = ada_w.shape[2]
    row = lambda t: t.reshape(1, -1)

    small_parts = [c, sc_conv_w, lru_conv_w, lru_conv_b, lru_b_a, lru_b_x, lru_lambda]
    small, offs, n_small = _pack(small_parts)
    got = _allgather8([small.reshape(8, n_small // 8)], "ag_small")[0].reshape(8, n_small)
    c_all = got[:, :d]
    per_chip = got[0::2]

    def chip_part(k, shape):
        size = 1
        for dim in shape:
            size *= dim
        return per_chip[:, offs[k]:offs[k] + size].reshape((4,) + shape)

    conv_w0 = jnp.transpose(chip_part(1, (3, es)), (1, 0, 2)).reshape(3, e)
    conv_w1 = jnp.transpose(chip_part(2, (4, es)), (1, 0, 2)).reshape(4, e)
    conv_b1 = chip_part(3, (es,)).reshape(1, e)
    b_a = jnp.transpose(chip_part(4, (n_heads, hj)), (1, 0, 2)).reshape(n_heads, 1, 4 * hj)
    b_x = jnp.transpose(chip_part(5, (n_heads, hj)), (1, 0, 2)).reshape(n_heads, 1, 4 * hj)
    lam = chip_part(6, (es,)).reshape(1, e)

    mod_nb = _ada_fwd(c_all, ada_w, "ada_fwd")
    mods = _allgather8([mod_nb.reshape(16, f)], "ag_mod")[0].reshape(8, 2, 8, f)[0::2]
    mine = lax.dynamic_index_in_dim(mods, batch, axis=2, keepdims=False)
    mod = jnp.transpose(mine, (1, 0, 2)).reshape(2, 4 * f) + ada_b
    shift = [row(mod[l, :d]) for l in range(2)]
    scale = [row(mod[l, d:2 * d]) for l in range(2)]
    gate = [row(mod[l, 2 * d:]) for l in range(2)]
    ng = [row(norm_g[l]) for l in range(2)]

    shards = [sc_w_in[0], sc_w_out[0], lru_w_in[0], lru_w_a[0].reshape(n_heads * hj, HEAD_DIM),
              lru_w_x[0].reshape(n_heads * hj, HEAD_DIM), lru_w_out[0]]
    names = ["sc_w_in", "sc_w_out", "lru_w_in", "lru_w_a", "lru_w_x", "lru_w_out"]
    slots = [_cast_into_slot(chip_op, w, "cast_" + nm) for w, nm in zip(shards, names)]
    send_a, recv_a, buf, started = _first_start(slots[0], mod, "ag_first_start")
    h0 = _norm_mod_fwd(x2d, ng[0], scale[0], shift[0], "norm0", after=started)
    planes = [jnp.reshape(2 * px + py, (1,)).astype(jnp.int32) for px, py in _other_chips(xi, yi)]
    proj0 = _mm_plane(h0, buf.reshape(4, d, e), chip_op, "sc_in_own")
    relays, passed, relayed = [], [], [proj0] + slots[1:]
    for j in range(2):
        send_b, recv_b, send_c, recv_c, buf, token = _first_relay(send_a, recv_a, buf, j, relayed,
                                                                  "ag_first_relay_%d" % j)
        relays.append((send_c, recv_c))
        passed.append((send_b, recv_b))
        relayed = [token]
    rest_flight, relayed = _gather_start([[slots[1]], [slots[2]], slots[3:5], [slots[5]]], token, "ag_start")
    for j in range(2):
        buf = _first_relay_done(*passed[j], buf, j, relayed, "ag_first_relay_done_%d" % j)
        proj0 = _mm_plane(h0, buf.reshape(4, d, e), planes[j], "sc_in_%d" % j, planes_so_far=proj0)
        relayed = proj0
    send_b, recv_b, buf, relayed = _first_diagonal(relays, buf, proj0, "ag_first_diagonal")
    buf = _first_diagonal_done(send_b, recv_b, relays, buf, relayed, "ag_first_diagonal_done")
    proj0 = _mm_plane(h0, buf.reshape(4, d, e), planes[2], "sc_in_2", planes_so_far=proj0)
    w_in0 = buf.reshape(4, d, e)
    in_flight = [None] + rest_flight

    def arrived(g, after, tag):
        send1, recv1, bufs = in_flight[g]
        send2, recv2, bufs, passed = _gather_forward(send1, recv1, bufs, after, "ag_forward_" + tag)
        return _gather_finish(send2, recv2, bufs, passed, "ag_finish_" + tag)

    y0 = _sc_fwd(proj0, conv_w0, "sc_mix")
    w_out0 = arrived(1, y0, "sc_w_out")[0].reshape(1, e, d)
    o0 = _mm(y0[None], w_out0, "nn", F32, "sc_out")[0]
    x1, h1 = _norm_mod_fwd(x2d, ng[1], scale[1], shift[1], "norm1", o=o0, gate=gate[0])
    send1, recv1, bufs = in_flight[2]
    proj1 = _mm_plane(h1, bufs[0].reshape(4, d, e // 2), chip_op, "lru_in_own")
    for j in range(3):
        send2, recv2, bufs, passed_on = _gather_forward(send1, recv1, bufs, proj1, "ag_forward_lru_w_in_%d" % j,
                                                        sources=(j,))
        bufs = _gather_finish(send2, recv2, bufs, passed_on, "ag_finish_lru_w_in_%d" % j, sources=(j,))
        proj1 = _mm_plane(h1, bufs[0].reshape(4, d, e // 2), planes[j], "lru_in_%d" % j, planes_so_far=proj1)
    w_in1 = bufs[0].reshape(4, d, e // 2)
    gate_ws = arrived(3, proj1, "lru_gates")
    w_a = gate_ws[0].reshape(4, n_heads, hj, HEAD_DIM)
    w_x = gate_ws[1].reshape(4, n_heads, hj, HEAD_DIM)
    pairs1 = proj1.reshape(2, 2, s, e // 2)
    y1, *lru_saved = _lru_fwd(pairs1, conv_w1, conv_b1, w_a, b_a, w_x, b_x, lam, "lru_mix")
    w_out1 = arrived(4, y1, "lru_w_out")[0].reshape(1, e, d)
    o1 = _mm(y1[None], w_out1, "nn", F32, "lru_out")[0]
    dx2, do1, acc_f = _final_loss(x1, o1, gate[1], row(final_g), tgt, "final_loss")

    def reduce_stage1(tag, grads):
        lands = [lax.empty((4, g.shape[1] // 2, g.shape[2]), F32) for g in grads]
        return _exchange_start("rs_sibling_start_" + tag, grads, lands, len(grads),
                               _plan_other_half_to_sibling, None)

    def reduce_stage2(tag, stage1, nms, after):
        send, recv, grads, lands, _ = stage1
        grads, lands = _exchange_wait("rs_sibling_wait_" + tag, send, recv, grads, lands,
                                      _plan_other_half_to_sibling, after)
        parts = [_sum_own_and_sibling(core_op, g, r1, "rs_sum1_" + nm) for g, r1, nm in zip(grads, lands, nms)]
        lands = [lax.empty((3,) + p.shape[1:], BF16) for p in parts]
        return _exchange_start("rs_chips_start_" + tag, parts, lands, 3 * len(parts),
                               _plan_partials_to_chips, None)

    def reduce_stage3(tag, stage2, nms, after):
        send, recv, parts, lands, _ = stage2
        parts, lands = _exchange_wait("rs_chips_wait_" + tag, send, recv, parts, lands,
                                      _plan_partials_to_chips, after)
        halves = [_sum_chips(where_op, p, r2, "rs_sum2_" + nm) for p, r2, nm in zip(parts, lands, nms)]
        return _exchange_start("rs_share_start_" + tag, [], halves, len(halves), _plan_share_half, None)

    def reduce_done(tag, stage3, after):
        send, recv, _, fulls, _ = stage3
        return _exchange_wait("rs_share_wait_" + tag, send, recv, [], fulls, _plan_share_half, after)[1]

    def head_major_to_chip_major(t):
        return jnp.transpose(t.reshape(n_heads, 4, hj, HEAD_DIM), (1, 0, 2, 3)).reshape(4, n_heads * hj, HEAD_DIM)

    big_state = dict(zip(names, zip(shards, [m_sc_w_in, m_sc_w_out, m_lru_w_in, m_lru_w_a, m_lru_w_x, m_lru_w_out],
                                    [v_sc_w_in, v_sc_w_out, v_lru_w_in, v_lru_w_a, v_lru_w_x, v_lru_w_out],
                                    [sc_w_in, sc_w_out, lru_w_in, lru_w_a, lru_w_x, lru_w_out])))
    big = {}

    def update(nms, fulls, after):
        for nm, g2 in zip(nms, fulls):
            w2, m4, v4, w4 = big_state[nm]
            outs = _adamw(w2, g2, m4.reshape(w2.shape), v4.reshape(w2.shape), "adamw_" + nm, after=after)
            big[nm] = tuple(t.reshape(w4.shape) for t in outs)
            after = outs[1]
        return after

    g_w_out1 = _mm(y1[None], do1[None], "tn", F32, "lru_out_dw", bm=512, bn=2048)
    dy1 = _mm(do1[None], w_out1, "nt", BF16, "lru_out_dx")[0]
    dpairs1, g_wa, g_wx, g_ba, g_bx, g_lam, g_cw1, g_cb1 = _lru_bwd(
        pairs1, dy1, lru_saved, conv_w1, conv_b1, w_a, b_a, w_x, b_x, lam, "lru_mix_bwd")
    dproj1 = dpairs1.reshape(4, s, e // 2)
    g_w_in1 = _mm(h1[None], dproj1, "tn", F32, "lru_in_dw", bm=1024, bn=2048)
    lru_names = ["lru_w_out", "lru_w_a", "lru_w_x", "lru_w_in"]
    lru_rs = reduce_stage1("lru", [g_w_out1.reshape(4, es, d), head_major_to_chip_major(g_wa),
                                   head_major_to_chip_major(g_wx), g_w_in1])
    dh1 = _mm(dproj1, w_in1, "nt", F32, "lru_in_dx", after=lru_rs[4])[0]
    lru_rs = reduce_stage2("lru", lru_rs, lru_names, dh1)
    dx1, do0, acc1 = _norm_mod_bwd(dh1, x1, ng[1], scale[1], dx2, "norm1_bwd", o_prev=o0, gate_prev=gate[0],
                                   after=lru_rs[4])

    g_w_out0 = _mm(y0[None], do0[None], "tn", F32, "sc_out_dw", bm=512, bn=2048)
    out_rs = reduce_stage1("sc_out", [g_w_out0.reshape(4, es, d)])
    dy0 = _mm(do0[None], w_out0, "nt", BF16, "sc_out_dx", after=out_rs[4])[0]
    out_rs = reduce_stage2("sc_out", out_rs, ["sc_w_out"], dy0)
    dproj0, g_cw0 = _sc_bwd(proj0, dy0, conv_w0, "sc_mix_bwd", after=out_rs[4])
    stage1, stage2, after = [], [], None
    for k in range(IN_PIECES):
        g_piece = _mm(h0[None], dproj0, "tn", F32, "sc_in_dw_%d" % k, bm=1024, bn=e // IN_PIECES,
                      col_blocks=(k, 1), after=after)
        stage1.append(reduce_stage1("sc_in_%d" % k, [g_piece]))
        after = stage1[k][4]
        if k:
            stage2.append(reduce_stage2("sc_in_%d" % (k - 1), stage1[k - 1], ["sc_w_in_%d" % (k - 1)], after))
            after = stage2[k - 1][4]
    lru_rs = reduce_stage3("lru", lru_rs, lru_names, after)
    k = IN_PIECES - 1
    stage2.append(reduce_stage2("sc_in_%d" % k, stage1[k], ["sc_w_in_%d" % k], lru_rs[4]))
    dh0 = _mm(dproj0, w_in0, "nt", F32, "sc_in_dx", bn=1024, after=stage2[k][4])[0]
    grad_x, acc0 = _norm_mod_bwd(dh0, x2d, ng[0], scale[0], dx1, "norm0_bwd")
    out_rs = reduce_stage3("sc_out", out_rs, ["sc_w_out"], acc0)
    last = update(lru_names, reduce_done("lru", lru_rs, out_rs[4]), None)
    last = update(["sc_w_out"], reduce_done("sc_out", out_rs, last), None)
    g_half = None
    for k, (send, recv, parts, lands, _) in enumerate(stage2):
        parts, lands = _exchange_wait("rs_chips_wait_sc_in_%d" % k, send, recv, parts, lands,
                                      _plan_partials_to_chips, last)
        g_half = _sum_chips(where_op, parts[0], lands[0], "rs_sum2_sc_w_in_%d" % k, piece=(k, IN_PIECES),
                            so_far=g_half)
    in_rs = _exchange_start("rs_share_start_sc_in", [], [g_half], 1, _plan_share_half, None)

    dmod = jnp.stack([jnp.concatenate([acc0[1], acc0[0], acc1[3]]),
                      jnp.concatenate([acc1[1], acc1[0], acc_f[1]])])
    part_list = [jnp.stack([acc0[2], acc1[2]]), acc_f[0], acc_f[2, :1], g_cw0, g_cw1, g_cb1, g_ba, g_bx,
                 g_lam, dmod]
    partials, poffs, n_part = _pack(part_list)
    every = _allgather8([partials.reshape(8, n_part // 8)], "ag_partials", after=in_rs[4])[0].reshape(8, n_part)
    total = _sum_rows8(every, "sum_partials")[0]

    def tot(k, shape):
        size = 1
        for dim in shape:
            size *= dim
        return total[poffs[k]:poffs[k] + size].reshape(shape)

    def my_cols(t, width):
        return lax.dynamic_slice_in_dim(t, chip * width, width, axis=t.ndim - 1)

    loss = tot(2, (1,))[0]
    g_norm_g, g_final_g, g_ada_b = tot(0, (2, d)), tot(1, (d,)), tot(9, (2, 3 * d))
    g_sc_conv_w = my_cols(tot(3, (3, e)), es)[None]
    g_lru_conv_w = my_cols(tot(4, (4, e)), es)[None]
    g_lru_conv_b = my_cols(tot(5, (1, e)), es)
    g_lru_b_a = my_cols(tot(6, (n_heads, 4 * hj)), hj)[None]
    g_lru_b_x = my_cols(tot(7, (n_heads, 4 * hj)), hj)[None]
    g_lru_lambda = my_cols(tot(8, (1, e)), es)

    dmod_all = every[:, poffs[9]:poffs[9] + 6 * d].reshape(8, 2, 3 * d)
    dmod_mine = jnp.transpose(my_cols(dmod_all, f), (1, 0, 2))
    ada = _ada_bwd_adamw(jnp.transpose(c_all), dmod_mine, ada_w, m_ada_w, v_ada_w, "ada_bwd_adamw", after=total)

    small_names = ["norm_g", "ada_b", "final_g", "sc_conv_w", "lru_conv_w", "lru_conv_b", "lru_b_a",
                   "lru_b_x", "lru_lambda"]
    small_w = [norm_g, ada_b, final_g, sc_conv_w, lru_conv_w, lru_conv_b, lru_b_a, lru_b_x, lru_lambda]
    small_g = [g_norm_g, g_ada_b, g_final_g, g_sc_conv_w, g_lru_conv_w, g_lru_conv_b, g_lru_b_a,
               g_lru_b_x, g_lru_lambda]
    small_m = [m_norm_g, m_ada_b, m_final_g, m_sc_conv_w, m_lru_conv_w, m_lru_conv_b, m_lru_b_a,
               m_lru_b_x, m_lru_lambda]
    small_v = [v_norm_g, v_ada_b, v_final_g, v_sc_conv_w, v_lru_conv_w, v_lru_conv_b, v_lru_b_a,
               v_lru_b_x, v_lru_lambda]
    pw, soffs, n_s = _pack(small_w)
    pg, pm, pv = _pack(small_g)[0], _pack(small_m)[0], _pack(small_v)[0]
    shape2 = (n_s // PACK, PACK)
    _, pd, pm2, pv2 = _adamw(pw.reshape(shape2), pg.reshape(shape2), pm.reshape(shape2), pv.reshape(shape2),
                             "adamw_small", after=ada[0])
    update(["sc_w_in"], reduce_done("sc_in", in_rs, pd), None)
    small = {}
    for k, (nm, w_) in enumerate(zip(small_names, small_w)):
        take = lambda t: t.reshape(-1)[soffs[k]:soffs[k] + w_.size].reshape(w_.shape)
        small[nm] = (small_g[k].reshape(w_.shape), take(pd), take(pm2), take(pv2))

    results = dict(small)
    results.update(big)
    results["ada_w"] = tuple(ada)
    order = ["norm_g", "ada_w", "ada_b", "sc_w_in", "sc_conv_w", "sc_w_out", "lru_w_in", "lru_conv_w",
             "lru_conv_b", "lru_w_a", "lru_b_a", "lru_w_x", "lru_b_x", "lru_lambda", "lru_w_out", "final_g"]
    out = [loss, grad_x[None]]
    for kind in range(4):
        out += [results[nm][kind] for nm in order]
    return tuple(out)
```

```python
import jax
import jax.numpy as jnp
from jax import lax
from jax.experimental import pallas as pl
from jax.experimental.pallas import tpu as pltpu

F32 = jnp.float32
BF16 = jnp.bfloat16
MESH = pl.DeviceIdType.MESH
ANY = pl.BlockSpec(memory_space=pl.ANY)

RMS_EPS = 1e-6
RGLRU_C = 8.0
HEAD_DIM = 256
ADAM_LR = 0.001
ADAM_B1 = 0.9
ADAM_B2 = 0.999
ADAM_EPS = 1e-08
ADAM_WD = 0.01
ADAM_STEP = 10
V7X_VMEM_LIMIT = 56 * 1024 * 1024
IN_PIECES = 2
LANES = 128
SUBLANES = 8
PACK = SUBLANES * LANES


def _blk(dim, pref, unit=LANES):
    if dim <= pref:
        return dim
    b = (pref // unit) * unit
    while b > unit and dim % b:
        b -= unit
    assert dim % b == 0, (dim, pref, unit)
    return b


def _params(sem=None):
    return pltpu.CompilerParams(dimension_semantics=sem, vmem_limit_bytes=V7X_VMEM_LIMIT)


def _ordered(body, n_in, after):
    if after is None:
        return body, [], []

    def ordered_body(*refs):
        return body(*refs[:n_in], *refs[n_in + 1:])

    return ordered_body, [ANY], [after]


def _pos():
    return lax.axis_index("x"), lax.axis_index("y"), lax.axis_index("c")


def _other_chips(x, y):
    return [(1 - x, y), (x, 1 - y), (1 - x, 1 - y)]


def _allgather8(arrs, name, after=None):
    n_t = len(arrs)
    ms = [a.shape[0] for a in arrs]

    def gather(*refs):
        ins, outs = refs[:n_t], refs[n_t:2 * n_t]
        send_sems, recv_sems, local_sems = refs[2 * n_t:]
        x, y, c = _pos()
        me, sibling = (x, y, c), (x, y, 1 - c)
        chips = _other_chips(x, y)

        def rows(t, px, py, pc):
            return outs[t].at[pl.ds((4 * px + 2 * py + pc) * ms[t], ms[t])]

        def copy(t, k, block, to, src=None):
            return pltpu.make_async_remote_copy(
                src_ref=rows(t, *block) if src is None else src, dst_ref=rows(t, *block),
                send_sem=send_sems.at[7 * t + k], recv_sem=recv_sems.at[7 * t + k],
                device_id=to, device_id_type=MESH)

        mine, first, passed = [], [], []
        for t in range(n_t):
            src = ins[t]
            cp = pltpu.make_async_copy(src, rows(t, *me), local_sems.at[t])
            cp.start()
            mine.append(cp)
            sends = [copy(t, 0, me, sibling, src=src)]
            sends += [copy(t, 1 + j, me, (*chip, c), src=src) for j, chip in enumerate(chips)]
            for cp in sends:
                cp.start()
            first += sends
        for t in range(n_t):
            for j, chip in enumerate(chips):
                copy(t, 1 + j, (*chip, c), me).wait_recv()
                cp = copy(t, 4 + j, (*chip, c), sibling)
                cp.start()
                passed.append(cp)
        for t in range(n_t):
            copy(t, 0, sibling, me).wait_recv()
            for j, chip in enumerate(chips):
                copy(t, 4 + j, (*chip, 1 - c), me).wait_recv()
        for cp in first + passed:
            cp.wait_send()
        for cp in mine:
            cp.wait()

    body, extra_specs, extra = _ordered(gather, n_t, after)
    return pl.pallas_call(
        body, name=name,
        out_shape=[jax.ShapeDtypeStruct((8 * m, a.shape[1]), a.dtype) for m, a in zip(ms, arrs)],
        in_specs=[ANY] * n_t + extra_specs, out_specs=[ANY] * n_t,
        scratch_shapes=[pltpu.SemaphoreType.DMA((7 * n_t,)), pltpu.SemaphoreType.DMA((7 * n_t,)),
                        pltpu.SemaphoreType.DMA((n_t,))],
    )(*arrs, *extra)


HBM = pl.BlockSpec(memory_space=pltpu.HBM)
SEM = pl.BlockSpec(memory_space=pltpu.SEMAPHORE)
TOKEN = pl.BlockSpec(memory_space=pltpu.VMEM)
IN_FLIGHT = pltpu.CompilerParams(has_side_effects=pltpu.SideEffectType.DATAFLOW_SIDE_EFFECTING)


def _in_hbm(arrs):
    return [pltpu.with_memory_space_constraint(a, pltpu.HBM) for a in arrs]


def _shard_rows(ref, h, px, py, pc):
    return ref.at[pl.ds((4 * px + 2 * py + pc) * h, h)]


def _gather_start(groups, after, name):
    bufs = [b for grp in groups for b in grp]
    n_t, n_g = len(bufs), len(groups)

    def body(*refs):
        sems, thru = refs[n_t + 1:n_t + 1 + 2 * n_g], refs[n_t + 1 + 2 * n_g:2 * n_t + 1 + 2 * n_g]
        token = refs[-1]
        x, y, c = _pos()
        t = 0
        for g, grp in enumerate(groups):
            for i in range(len(grp)):
                h = bufs[t].shape[0] // 8
                rows = _shard_rows(thru[t], h, x, y, c)
                for j, chip in enumerate(_other_chips(x, y)):
                    pltpu.make_async_remote_copy(
                        src_ref=rows, dst_ref=rows, send_sem=sems[2 * g].at[3 * i + j],
                        recv_sem=sems[2 * g + 1].at[3 * i + j], device_id=(*chip, c),
                        device_id_type=MESH).start()
                t += 1
        token[...] = jnp.zeros_like(token)

    sem_shapes = []
    for grp in groups:
        sem_shapes += [pltpu.SemaphoreType.DMA((3 * len(grp),))] * 2
    out = pl.pallas_call(
        body, name=name,
        out_shape=sem_shapes + [pltpu.HBM(b.shape, b.dtype) for b in bufs] + [jax.ShapeDtypeStruct((8, LANES), F32)],
        in_specs=[HBM] * n_t + [ANY], out_specs=[SEM] * (2 * n_g) + [HBM] * n_t + [TOKEN],
        input_output_aliases={t: 2 * n_g + t for t in range(n_t)},
        compiler_params=IN_FLIGHT,
    )(*_in_hbm(bufs), after)
    sems, thru, token = out[:2 * n_g], out[2 * n_g:2 * n_g + n_t], out[-1]
    per_group, t = [], 0
    for g, grp in enumerate(groups):
        per_group.append((sems[2 * g], sems[2 * g + 1], thru[t:t + len(grp)]))
        t += len(grp)
    return per_group, token


def _gather_forward(send_sems, recv_sems, bufs, after, name, sources=(0, 1, 2)):
    n_t = len(bufs)

    def body(*refs):
        ins = refs[:n_t]
        send1, recv1 = refs[n_t], refs[n_t + 1]
        send2, recv2 = refs[n_t + 3], refs[n_t + 4]
        token = refs[-1]
        x, y, c = _pos()
        chips = _other_chips(x, y)
        for t in range(n_t):
            h = bufs[t].shape[0] // 8
            mine = _shard_rows(ins[t], h, x, y, c)
            for j in sources:
                chip = chips[j]
                landed = _shard_rows(ins[t], h, *chip, c)
                pltpu.make_async_remote_copy(
                    src_ref=mine, dst_ref=landed, send_sem=send1.at[3 * t + j], recv_sem=recv1.at[3 * t + j],
                    device_id=(*chip, c), device_id_type=MESH).wait_recv()
                pltpu.make_async_remote_copy(
                    src_ref=landed, dst_ref=landed, send_sem=send2.at[3 * t + j], recv_sem=recv2.at[3 * t + j],
                    device_id=(x, y, 1 - c), device_id_type=MESH).start()
        for t in range(n_t):
            h = bufs[t].shape[0] // 8
            mine = _shard_rows(ins[t], h, x, y, c)
            for j in sources:
                pltpu.make_async_remote_copy(
                    src_ref=mine, dst_ref=mine, send_sem=send1.at[3 * t + j], recv_sem=recv1.at[3 * t + j],
                    device_id=(*chips[j], c), device_id_type=MESH).wait_send()
        token[...] = jnp.zeros_like(token)

    out = pl.pallas_call(
        body, name=name,
        out_shape=[pltpu.SemaphoreType.DMA((3 * n_t,))] * 2 + [pltpu.HBM(b.shape, b.dtype) for b in bufs]
        + [jax.ShapeDtypeStruct((8, LANES), F32)],
        in_specs=[HBM] * n_t + [SEM, SEM, ANY], out_specs=[SEM, SEM] + [HBM] * n_t + [TOKEN],
        input_output_aliases={t: 2 + t for t in range(n_t)},
        compiler_params=IN_FLIGHT,
    )(*bufs, send_sems, recv_sems, after)
    return out[0], out[1], out[2:2 + n_t], out[-1]


def _gather_finish(send_sems, recv_sems, bufs, after, name, sources=(0, 1, 2)):
    n_t = len(bufs)

    def body(*refs):
        ins = refs[:n_t]
        send2, recv2 = refs[n_t], refs[n_t + 1]
        x, y, c = _pos()
        chips = _other_chips(x, y)
        for t in range(n_t):
            h = bufs[t].shape[0] // 8
            for j in sources:
                chip = chips[j]
                sent = _shard_rows(ins[t], h, *chip, c)
                got = _shard_rows(ins[t], h, *chip, 1 - c)
                cp = pltpu.make_async_remote_copy(
                    src_ref=sent, dst_ref=got, send_sem=send2.at[3 * t + j], recv_sem=recv2.at[3 * t + j],
                    device_id=(x, y, 1 - c), device_id_type=MESH)
                cp.wait_send()
                cp.wait_recv()

    return pl.pallas_call(
        body, name=name, out_shape=[pltpu.HBM(b.shape, b.dtype) for b in bufs],
        in_specs=[HBM] * n_t + [SEM, SEM, ANY], out_specs=[HBM] * n_t,
        input_output_aliases={t: t for t in range(n_t)},
        compiler_params=IN_FLIGHT,
    )(*bufs, send_sems, recv_sems, after)


def _part_rows(ref, h, px, py, pc, which):
    return ref.at[pl.ds((4 * px + 2 * py + pc) * h + which * (h // 2), h // 2)]


def _remote(src, dst, send_sem, recv_sem, device):
    return pltpu.make_async_remote_copy(src_ref=src, dst_ref=dst, send_sem=send_sem, recv_sem=recv_sem,
                                        device_id=device, device_id_type=MESH)


def _first_start(buf, after, name):
    h = buf.shape[0] // 8

    def body(buf_in, after_ref, send, recv, thru, token):
        x, y, c = _pos()
        rows = _shard_rows(thru, h, x, y, c)
        for j, chip in enumerate(_other_chips(x, y)[:2]):
            _remote(rows, rows, send.at[j], recv.at[j], (*chip, c)).start()
        token[...] = jnp.zeros_like(token)

    return pl.pallas_call(
        body, name=name,
        out_shape=[pltpu.SemaphoreType.DMA((2,))] * 2 + [pltpu.HBM(buf.shape, buf.dtype),
                                                         jax.ShapeDtypeStruct((8, LANES), F32)],
        in_specs=[HBM, ANY], out_specs=[SEM, SEM, HBM, TOKEN], input_output_aliases={0: 2},
        compiler_params=IN_FLIGHT,
    )(*_in_hbm([buf]), after)


def _first_relay(send_a, recv_a, buf, j, after, name):
    h = buf.shape[0] // 8

    def body(buf_in, send_a, recv_a, *rest):
        send_b, recv_b, send_c, recv_c, _, token = rest[len(after):]
        x, y, c = _pos()
        chips = _other_chips(x, y)
        nbr, other = chips[j], chips[1 - j]
        mine, landed = _shard_rows(buf_in, h, x, y, c), _shard_rows(buf_in, h, *nbr, c)
        _remote(mine, landed, send_a.at[j], recv_a.at[j], (*nbr, c)).wait_recv()
        _remote(landed, landed, send_b.at[0], recv_b.at[0], (x, y, 1 - c)).start()
        part = _part_rows(buf_in, h, *nbr, c, j)
        _remote(part, part, send_c.at[0], recv_c.at[0], (*other, c)).start()
        _remote(mine, mine, send_a.at[j], recv_a.at[j], (*nbr, c)).wait_send()
        token[...] = jnp.zeros_like(token)

    out = pl.pallas_call(
        body, name=name,
        out_shape=[pltpu.SemaphoreType.DMA((1,))] * 4 + [pltpu.HBM(buf.shape, buf.dtype),
                                                         jax.ShapeDtypeStruct((8, LANES), F32)],
        in_specs=[HBM, SEM, SEM] + [ANY] * len(after), out_specs=[SEM] * 4 + [HBM, TOKEN],
        input_output_aliases={0: 4}, compiler_params=IN_FLIGHT,
    )(buf, send_a, recv_a, *after)
    return tuple(out)


def _first_relay_done(send_b, recv_b, buf, j, after, name):
    h = buf.shape[0] // 8

    def body(buf_in, send_b, recv_b, after_ref, thru):
        x, y, c = _pos()
        nbr = _other_chips(x, y)[j]
        cp = _remote(_shard_rows(buf_in, h, *nbr, c), _shard_rows(buf_in, h, *nbr, 1 - c),
                     send_b.at[0], recv_b.at[0], (x, y, 1 - c))
        cp.wait_send()
        cp.wait_recv()

    return pl.pallas_call(
        body, name=name, out_shape=pltpu.HBM(buf.shape, buf.dtype),
        in_specs=[HBM, SEM, SEM, ANY], out_specs=HBM, input_output_aliases={0: 0},
        compiler_params=IN_FLIGHT,
    )(buf, send_b, recv_b, after)


def _first_diagonal(relays, buf, after, name):
    h = buf.shape[0] // 8

    def body(buf_in, send_c0, recv_c0, send_c1, recv_c1, after_ref, send_b, recv_b, thru, token):
        x, y, c = _pos()
        chips = _other_chips(x, y)
        diag = chips[2]
        for j, (send_c, recv_c) in enumerate(((send_c0, recv_c0), (send_c1, recv_c1))):
            part = _part_rows(buf_in, h, *diag, c, j)
            _remote(part, part, send_c.at[0], recv_c.at[0], (*chips[1 - j], c)).wait_recv()
        whole = _shard_rows(buf_in, h, *diag, c)
        _remote(whole, whole, send_b.at[0], recv_b.at[0], (x, y, 1 - c)).start()
        token[...] = jnp.zeros_like(token)

    out = pl.pallas_call(
        body, name=name,
        out_shape=[pltpu.SemaphoreType.DMA((1,))] * 2 + [pltpu.HBM(buf.shape, buf.dtype),
                                                         jax.ShapeDtypeStruct((8, LANES), F32)],
        in_specs=[HBM] + [SEM] * 4 + [ANY], out_specs=[SEM, SEM, HBM, TOKEN], input_output_aliases={0: 2},
        compiler_params=IN_FLIGHT,
    )(buf, relays[0][0], relays[0][1], relays[1][0], relays[1][1], after)
    return tuple(out)


def _first_diagonal_done(send_b, recv_b, relays, buf, after, name):
    h = buf.shape[0] // 8

    def body(buf_in, send_b, recv_b, send_c0, recv_c0, send_c1, recv_c1, after_ref, thru):
        x, y, c = _pos()
        chips = _other_chips(x, y)
        diag = chips[2]
        cp = _remote(_shard_rows(buf_in, h, *diag, c), _shard_rows(buf_in, h, *diag, 1 - c),
                     send_b.at[0], recv_b.at[0], (x, y, 1 - c))
        cp.wait_send()
        cp.wait_recv()
        for j, (send_c, recv_c) in enumerate(((send_c0, recv_c0), (send_c1, recv_c1))):
            part = _part_rows(buf_in, h, *chips[j], c, j)
            _remote(part, part, send_c.at[0], recv_c.at[0], (*chips[1 - j], c)).wait_send()

    return pl.pallas_call(
        body, name=name, out_shape=pltpu.HBM(buf.shape, buf.dtype),
        in_specs=[HBM] + [SEM] * 6 + [ANY], out_specs=HBM, input_output_aliases={0: 0},
        compiler_params=IN_FLIGHT,
    )(buf, send_b, recv_b, relays[0][0], relays[0][1], relays[1][0], relays[1][1], after)


def _exchange_start(name, srcs, lands, n_copies, plan, after):
    ns, nl = len(srcs), len(lands)
    extra = [] if after is None else [after]

    def body(*refs):
        base = ns + nl + len(extra)
        send_sems, recv_sems = refs[base], refs[base + 1]
        src_refs, land_refs = refs[base + 2:base + 2 + ns], refs[base + 2 + ns:base + 2 + ns + nl]
        token = refs[-1]
        x, y, c = _pos()
        copies = plan(src_refs, land_refs, x, y, c)
        assert len(copies) == n_copies
        for k, (src, dst, dev) in enumerate(copies):
            pltpu.make_async_remote_copy(
                src_ref=src, dst_ref=dst, send_sem=send_sems.at[k], recv_sem=recv_sems.at[k],
                device_id=dev, device_id_type=MESH).start()
        token[...] = jnp.zeros_like(token)

    out = pl.pallas_call(
        body, name=name,
        out_shape=[pltpu.SemaphoreType.DMA((n_copies,))] * 2
        + [pltpu.HBM(a.shape, a.dtype) for a in list(srcs) + list(lands)] + [jax.ShapeDtypeStruct((8, LANES), F32)],
        in_specs=[HBM] * (ns + nl) + [ANY] * len(extra), out_specs=[SEM, SEM] + [HBM] * (ns + nl) + [TOKEN],
        input_output_aliases={i: 2 + i for i in range(ns + nl)},
        compiler_params=IN_FLIGHT,
    )(*_in_hbm(list(srcs) + list(lands)), *extra)
    return out[0], out[1], out[2:2 + ns], out[2 + ns:2 + ns + nl], out[-1]


def _exchange_wait(name, send_sems, recv_sems, srcs, lands, plan, after):
    ns, nl = len(srcs), len(lands)

    def body(*refs):
        src_refs, land_refs = refs[:ns], refs[ns:ns + nl]
        send, recv = refs[ns + nl], refs[ns + nl + 1]
        x, y, c = _pos()
        for k, (src, dst, dev) in enumerate(plan(src_refs, land_refs, x, y, c)):
            cp = pltpu.make_async_remote_copy(
                src_ref=src, dst_ref=dst, send_sem=send.at[k], recv_sem=recv.at[k],
                device_id=dev, device_id_type=MESH)
            cp.wait_send()
            cp.wait_recv()

    out = pl.pallas_call(
        body, name=name, out_shape=[pltpu.HBM(a.shape, a.dtype) for a in list(srcs) + list(lands)],
        in_specs=[HBM] * (ns + nl) + [SEM, SEM, ANY], out_specs=[HBM] * (ns + nl),
        input_output_aliases={i: i for i in range(ns + nl)},
        compiler_params=IN_FLIGHT,
    )(*srcs, *lands, send_sems, recv_sems, after)
    return out[:ns], out[ns:]


def _plan_other_half_to_sibling(src_refs, land_refs, x, y, c):
    out = []
    for g_ref, r_ref in zip(src_refs, land_refs):
        h = g_ref.shape[1] // 2
        out.append((g_ref.at[:, pl.ds((1 - c) * h, h), :], r_ref, (x, y, 1 - c)))
    return out


def _plan_partials_to_chips(src_refs, land_refs, x, y, c):
    out = []
    for p_ref, r_ref in zip(src_refs, land_refs):
        for j, (px, py) in enumerate(_other_chips(x, y)):
            out.append((p_ref.at[2 * px + py], r_ref.at[j], (px, py, c)))
    return out


def _plan_share_half(src_refs, land_refs, x, y, c):
    out = []
    for f_ref in land_refs:
        h = f_ref.shape[0] // 2
        rows = f_ref.at[pl.ds(c * h, h)]
        out.append((rows, rows, (x, y, 1 - c)))
    return out


def _cast_into_slot(chip, w, name):
    r, c = w.shape
    br, bc = _blk(r, 512, 8), _blk(c, 2048)
    nb = r // br

    def body(chip_ref, w_ref, o_ref):
        o_ref[...] = w_ref[...].astype(BF16)

    grid_spec = pltpu.PrefetchScalarGridSpec(
        num_scalar_prefetch=1, grid=(nb, c // bc),
        in_specs=[pl.BlockSpec((br, bc), lambda i, j, chip_ref: (i, j))],
        out_specs=pl.BlockSpec((br, bc), lambda i, j, chip_ref: (chip_ref[0] * nb + i, j)))
    return pl.pallas_call(
        body, name=name, grid_spec=grid_spec,
        out_shape=jax.ShapeDtypeStruct((4 * r, c), BF16),
        compiler_params=_params(("parallel", "parallel")),
    )(chip, w)


def _sum_own_and_sibling(core, g, r1, name):
    _, r, c = g.shape
    h = r // 2
    br, bc = _blk(h, 512, 8), _blk(c, 2048)
    nb = h // br

    def body(core_ref, g_ref, r_ref, o_ref):
        o_ref[...] = (g_ref[...] + r_ref[...]).astype(BF16)

    grid_spec = pltpu.PrefetchScalarGridSpec(
        num_scalar_prefetch=1, grid=(4, nb, c // bc),
        in_specs=[pl.BlockSpec((None, br, bc), lambda k, i, j, core_ref: (k, core_ref[0] * nb + i, j)),
                  pl.BlockSpec((None, br, bc), lambda k, i, j, core_ref: (k, i, j))],
        out_specs=pl.BlockSpec((None, br, bc), lambda k, i, j, core_ref: (k, i, j)))
    return pl.pallas_call(
        body, name=name, grid_spec=grid_spec,
        out_shape=jax.ShapeDtypeStruct((4, h, c), BF16),
        compiler_params=_params(("parallel", "parallel", "parallel")),
    )(core, g, r1)


def _sum_chips(where, p, r2, name, piece=(0, 1), so_far=None):
    _, h, c = p.shape
    k, n = piece
    br, bc = _blk(h, 512, 8), _blk(c, 2048)
    nb, ncb = h // br, c // bc

    def body(where_ref, p_ref, r_ref, *rest):
        acc = p_ref[...].astype(F32)
        for j in range(3):
            acc = acc + r_ref[j].astype(F32)
        rest[-1][...] = acc

    extra_specs, extra, aliases = ([], [], {}) if so_far is None else ([ANY], [so_far], {3: 0})
    grid_spec = pltpu.PrefetchScalarGridSpec(
        num_scalar_prefetch=1, grid=(nb, ncb),
        in_specs=[pl.BlockSpec((None, br, bc), lambda i, j, where_ref: (where_ref[0], i, j)),
                  pl.BlockSpec((3, br, bc), lambda i, j, where_ref: (0, i, j))] + extra_specs,
        out_specs=pl.BlockSpec((br, bc), lambda i, j, where_ref: (where_ref[1] * nb + i, k * ncb + j)))
    return pl.pallas_call(
        body, name=name, grid_spec=grid_spec,
        out_shape=jax.ShapeDtypeStruct((2 * h, n * c), F32), input_output_aliases=aliases,
        compiler_params=_params(("parallel", "parallel")),
    )(where, p, r2, *extra)


def _adamw_math(w, g, m, v):
    m2 = ADAM_B1 * m + (1.0 - ADAM_B1) * g
    v2 = ADAM_B2 * v + (1.0 - ADAM_B2) * (g * g)
    m_hat = m2 / (1.0 - ADAM_B1 ** ADAM_STEP)
    v_hat = v2 / (1.0 - ADAM_B2 ** ADAM_STEP)
    delta = -ADAM_LR * (m_hat / (jnp.sqrt(v_hat) + ADAM_EPS) + ADAM_WD * w)
    return delta, m2, v2


def _adamw(w, g, m, v, name, after=None):
    r, c = w.shape
    br, bc = _blk(r, 256, 8), _blk(c, 2048)

    def body(w_ref, g_ref, m_ref, v_ref, go_ref, d_ref, m2_ref, v2_ref):
        gv = g_ref[...]
        d, m2, v2 = _adamw_math(w_ref[...], gv, m_ref[...], v_ref[...])
        go_ref[...] = gv
        d_ref[...] = d
        m2_ref[...] = m2
        v2_ref[...] = v2

    spec = pl.BlockSpec((br, bc), lambda i, j: (i, j))
    body, extra_specs, extra = _ordered(body, 4, after)
    return pl.pallas_call(
        body, name=name, grid=(r // br, c // bc),
        in_specs=[spec] * 4 + extra_specs, out_specs=[spec] * 4,
        out_shape=[jax.ShapeDtypeStruct((r, c), F32)] * 4,
        compiler_params=_params(("parallel", "parallel")),
    )(w, g, m, v, *extra)


def _sum_rows8(g, name, after=None):
    n = g.shape[1]

    def body(g_ref, o_ref):
        acc = g_ref[0:1, :]
        for k in range(1, 8):
            acc = acc + g_ref[k:k + 1, :]
        o_ref[...] = acc

    body, extra_specs, extra = _ordered(body, 1, after)
    return pl.pallas_call(
        body, name=name, out_shape=jax.ShapeDtypeStruct((1, n), F32),
        in_specs=[pl.BlockSpec(memory_space=pltpu.VMEM)] + extra_specs,
        out_specs=pl.BlockSpec(memory_space=pltpu.VMEM),
        compiler_params=_params(),
    )(g, *extra)


def _mm(a, b, mode, out_dtype, name, bm=1024, bn=None, after=None, col_blocks=None):
    if mode == "nn":
        (_, m, k), (g, _, n) = a.shape, b.shape
    elif mode == "tn":
        (_, k, m), (g, _, n) = a.shape, b.shape
    else:
        (g, m, k), (_, n, _) = a.shape, b.shape
    if bn is None:
        bn = 1024 if k <= 2048 else 512
    bm, bn = _blk(m, bm), _blk(n, bn)

    if mode == "nt":
        def body(a_ref, b_ref, o_ref, acc_ref):
            part = lax.dot_general(a_ref[...], b_ref[...], (((1,), (1,)), ((), ())),
                                   preferred_element_type=F32)
            if g == 1:
                o_ref[...] = part.astype(out_dtype)
            else:
                gi = pl.program_id(2)

                @pl.when(gi == 0)
                def _():
                    acc_ref[...] = part

                @pl.when(gi > 0)
                def _():
                    acc_ref[...] += part

                @pl.when(gi == g - 1)
                def _():
                    o_ref[...] = acc_ref[...].astype(out_dtype)

        body, extra_specs, extra = _ordered(body, 2, after)
        return pl.pallas_call(
            body, name=name, grid=(m // bm, n // bn, g),
            in_specs=[pl.BlockSpec((None, bm, k), lambda i, j, gi: (gi, i, 0)),
                      pl.BlockSpec((None, bn, k), lambda i, j, gi: (gi, j, 0))] + extra_specs,
            out_specs=pl.BlockSpec((None, bm, bn), lambda i, j, gi: (0, i, j)),
            out_shape=jax.ShapeDtypeStruct((1, m, n), out_dtype),
            scratch_shapes=[pltpu.VMEM((bm, bn), F32)],
            compiler_params=_params(("parallel", "parallel", "arbitrary")),
        )(a, b, *extra)

    contract = (((1,), (0,)), ((), ())) if mode == "nn" else (((0,), (0,)), ((), ()))

    def body(a_ref, b_ref, o_ref):
        o_ref[...] = lax.dot_general(a_ref[...], b_ref[...], contract,
                                     preferred_element_type=F32).astype(out_dtype)

    a_spec = (pl.BlockSpec((None, bm, k), lambda i, gi, j: (0, i, 0)) if mode == "nn"
              else pl.BlockSpec((None, k, bm), lambda i, gi, j: (0, 0, i)))
    first, count = (0, n // bn) if col_blocks is None else col_blocks
    body, extra_specs, extra = _ordered(body, 2, after)
    return pl.pallas_call(
        body, name=name, grid=(m // bm, g, count),
        in_specs=[a_spec, pl.BlockSpec((None, k, bn), lambda i, gi, j: (gi, 0, first + j))] + extra_specs,
        out_specs=pl.BlockSpec((None, bm, bn), lambda i, gi, j: (gi, i, j)),
        out_shape=jax.ShapeDtypeStruct((g, m, count * bn), out_dtype),
        compiler_params=_params(("parallel", "parallel", "parallel")),
    )(a, b, *extra)


def _mm_plane(a, b, plane, name, planes_so_far=None, after=None, bm=1024, bn=1024):
    (m, k), (g, _, n) = a.shape, b.shape
    bm, bn = _blk(m, bm), _blk(n, bn)

    def body(plane_ref, a_ref, b_ref, *rest):
        rest[-1][...] = jnp.dot(a_ref[...], b_ref[...], preferred_element_type=F32).astype(BF16)

    extra_specs, extra, aliases = [], [], {}
    if planes_so_far is not None:
        extra_specs.append(ANY)
        extra.append(planes_so_far)
        aliases = {3: 0}
    if after is not None:
        extra_specs.append(ANY)
        extra.append(after)
    grid_spec = pltpu.PrefetchScalarGridSpec(
        num_scalar_prefetch=1, grid=(m // bm, n // bn),
        in_specs=[pl.BlockSpec((bm, k), lambda i, j, p: (i, 0)),
                  pl.BlockSpec((None, k, bn), lambda i, j, p: (p[0], 0, j))] + extra_specs,
        out_specs=pl.BlockSpec((None, bm, bn), lambda i, j, p: (p[0], i, j)))
    return pl.pallas_call(
        body, name=name, grid_spec=grid_spec,
        out_shape=jax.ShapeDtypeStruct((g, m, n), BF16), input_output_aliases=aliases,
        compiler_params=_params(("parallel", "parallel")),
    )(plane, a, b, *extra)


def _row_specs(br, d):
    return (pl.BlockSpec((br, d), lambda i: (i, 0)), pl.BlockSpec((1, d), lambda i: (0, 0)),
            pl.BlockSpec((8, d), lambda i: (0, 0)))


def _rstd(xv):
    return lax.rsqrt(jnp.mean(xv * xv, axis=-1, keepdims=True) + RMS_EPS)


def _colsum(v):
    return jnp.sum(v, axis=0, keepdims=True)


def _norm_mod_fwd(x, g, scale, shift, name, o=None, gate=None, after=None):
    s, d = x.shape
    br = _blk(s, 256, 8)
    has_res = o is not None
    row, vec, _ = _row_specs(br, d)

    def body(*refs):
        if has_res:
            x_ref, o_ref, gate_ref, g_ref, sc_ref, sh_ref, x1_ref, h_ref = refs
            xv = x_ref[...] + gate_ref[...] * o_ref[...]
            x1_ref[...] = xv
        else:
            x_ref, g_ref, sc_ref, sh_ref, h_ref = refs
            xv = x_ref[...]
        n = xv * _rstd(xv) * g_ref[...]
        h_ref[...] = (n * (1.0 + sc_ref[...]) + sh_ref[...]).astype(BF16)

    ins = [x] + ([o, gate] if has_res else []) + [g, scale, shift]
    in_specs = [row] + ([row, vec] if has_res else []) + [vec] * 3
    out_shape = ([jax.ShapeDtypeStruct((s, d), F32)] if has_res else []) + [jax.ShapeDtypeStruct((s, d), BF16)]
    body, extra_specs, extra = _ordered(body, len(ins), after)
    out = pl.pallas_call(
        body, name=name, grid=(s // br,), in_specs=in_specs + extra_specs, out_specs=[row] * len(out_shape),
        out_shape=out_shape, compiler_params=_params(("parallel",)),
    )(*ins, *extra)
    return out if has_res else out[0]


def _final_loss(x1, o1, gate1, final_g, tgt, name):
    s, d = x1.shape
    br = _blk(s, 256, 8)
    row, vec, acc = _row_specs(br, d)

    def body(x1_ref, o_ref, gate_ref, g_ref, t_ref, dx_ref, do_ref, acc_ref):
        @pl.when(pl.program_id(0) == 0)
        def _():
            acc_ref[...] = jnp.zeros_like(acc_ref)

        gate, o, g = gate_ref[...], o_ref[...], g_ref[...]
        x2 = x1_ref[...] + gate * o
        r = _rstd(x2)
        xh = x2 * r
        err = xh * g - t_ref[...]
        loss = 0.5 * _colsum(jnp.mean(err * err, axis=-1, keepdims=True))
        dout = err * (1.0 / d)
        dxh = dout * g
        dx2 = r * (dxh - xh * jnp.mean(dxh * xh, axis=-1, keepdims=True))
        dx_ref[...] = dx2
        do_ref[...] = (dx2 * gate).astype(BF16)
        acc_ref[0:1, :] += _colsum(dout * xh)
        acc_ref[1:2, :] += _colsum(dx2 * o)
        acc_ref[2:3, :] += jnp.broadcast_to(loss, (1, d))

    return pl.pallas_call(
        body, name=name, grid=(s // br,),
        in_specs=[row, row, vec, vec, row], out_specs=[row, row, acc],
        out_shape=[jax.ShapeDtypeStruct((s, d), F32), jax.ShapeDtypeStruct((s, d), BF16),
                   jax.ShapeDtypeStruct((8, d), F32)],
        compiler_params=_params(("arbitrary",)),
    )(x1, o1, gate1, final_g, tgt)


def _norm_mod_bwd(dh, x, g, scale, dx_next, name, o_prev=None, gate_prev=None, after=None):
    s, d = x.shape
    br = _blk(s, 256, 8)
    has_prev = o_prev is not None
    row, vec, acc = _row_specs(br, d)

    def body(*refs):
        if has_prev:
            dh_ref, x_ref, g_ref, sc_ref, dxn_ref, o_ref, gate_ref, dx_ref, do_ref, acc_ref = refs
        else:
            dh_ref, x_ref, g_ref, sc_ref, dxn_ref, dx_ref, acc_ref = refs

        @pl.when(pl.program_id(0) == 0)
        def _():
            acc_ref[...] = jnp.zeros_like(acc_ref)

        xv, gv, dhv = x_ref[...], g_ref[...], dh_ref[...]
        r = _rstd(xv)
        xh = xv * r
        acc_ref[0:1, :] += _colsum(dhv * (xh * gv))
        acc_ref[1:2, :] += _colsum(dhv)
        dn = dhv * (1.0 + sc_ref[...])
        acc_ref[2:3, :] += _colsum(dn * xh)
        dxh = dn * gv
        dx = dxn_ref[...] + r * (dxh - xh * jnp.mean(dxh * xh, axis=-1, keepdims=True))
        dx_ref[...] = dx
        if has_prev:
            acc_ref[3:4, :] += _colsum(dx * o_ref[...])
            do_ref[...] = (dx * gate_ref[...]).astype(BF16)

    ins = [dh, x, g, scale, dx_next] + ([o_prev, gate_prev] if has_prev else [])
    in_specs = [row, row, vec, vec, row] + ([row, vec] if has_prev else [])
    out_shape = [jax.ShapeDtypeStruct((s, d), F32)]
    out_specs = [row]
    if has_prev:
        out_shape.append(jax.ShapeDtypeStruct((s, d), BF16))
        out_specs.append(row)
    out_shape.append(jax.ShapeDtypeStruct((8, d), F32))
    out_specs.append(acc)
    body, extra_specs, extra = _ordered(body, len(ins), after)
    return pl.pallas_call(
        body, name=name, grid=(s // br,), in_specs=in_specs + extra_specs, out_specs=out_specs,
        out_shape=out_shape, compiler_params=_params(("arbitrary",)),
    )(*ins, *extra)


def _tiles(p):
    s, c = p.shape
    return p.reshape(s // SUBLANES, SUBLANES, c)


def _shift_down(p, k):
    if k == 0:
        return p
    r = pltpu.roll(_tiles(p), k, 1)
    before = jnp.concatenate([jnp.zeros_like(r[:1]), r[:-1]], axis=0)
    rows = lax.broadcasted_iota(jnp.int32, r.shape, 1)
    return jnp.where(rows >= k, r, before).reshape(p.shape)


def _shift_up(p, k):
    if k == 0:
        return p
    r = pltpu.roll(_tiles(p), SUBLANES - k, 1)
    after = jnp.concatenate([r[1:], jnp.zeros_like(r[:1])], axis=0)
    rows = lax.broadcasted_iota(jnp.int32, r.shape, 1)
    return jnp.where(rows < SUBLANES - k, r, after).reshape(p.shape)


def _sigmoid(z):
    return 0.5 * (jnp.tanh(0.5 * z) + 1.0)


def _sc_parts(proj_ref, w_ref):
    b, cg, v, g = (proj_ref[i].astype(F32) for i in range(4))
    p = cg * v
    u = w_ref[2:3, :] * p + w_ref[1:2, :] * _shift_down(p, 1) + w_ref[0:1, :] * _shift_down(p, 2)
    return b, cg, v, g, p, u


def _sc_fwd(proj, conv_w, name):
    _, s, e = proj.shape
    bc = _blk(e, 256)

    def body(proj_ref, w_ref, y_ref):
        b, _, _, g, _, u = _sc_parts(proj_ref, w_ref)
        y_ref[...] = (b * u * (g * _sigmoid(g))).astype(BF16)

    return pl.pallas_call(
        body, name=name, grid=(e // bc,),
        in_specs=[pl.BlockSpec((4, s, bc), lambda j: (0, 0, j)), pl.BlockSpec((3, bc), lambda j: (0, j))],
        out_specs=pl.BlockSpec((s, bc), lambda j: (0, j)),
        out_shape=jax.ShapeDtypeStruct((s, e), BF16),
        compiler_params=_params(("parallel",)),
    )(proj, conv_w)


def _sc_bwd(proj, dy, conv_w, name, after=None):
    _, s, e = proj.shape
    bc = _blk(e, 256)

    def body(proj_ref, dy_ref, w_ref, dp_ref, dw_ref):
        b, cg, v, g, p, u = _sc_parts(proj_ref, w_ref)
        dyv = dy_ref[...].astype(F32)
        sig = _sigmoid(g)
        t = dyv * (g * sig)
        du = t * b
        dp_ref[0] = (t * u).astype(BF16)
        dp_ref[3] = (dyv * b * u * (sig * (1.0 + g * (1.0 - sig)))).astype(BF16)
        dpp = w_ref[2:3, :] * du + w_ref[1:2, :] * _shift_up(du, 1) + w_ref[0:1, :] * _shift_up(du, 2)
        dp_ref[1] = (dpp * v).astype(BF16)
        dp_ref[2] = (dpp * cg).astype(BF16)
        dw_ref[2:3, :] = _colsum(du * p)
        dw_ref[1:2, :] = _colsum(du * _shift_down(p, 1))
        dw_ref[0:1, :] = _colsum(du * _shift_down(p, 2))

    body, extra_specs, extra = _ordered(body, 3, after)
    return pl.pallas_call(
        body, name=name, grid=(e // bc,),
        in_specs=[pl.BlockSpec((4, s, bc), lambda j: (0, 0, j)), pl.BlockSpec((s, bc), lambda j: (0, j)),
                  pl.BlockSpec((3, bc), lambda j: (0, j))] + extra_specs,
        out_specs=[pl.BlockSpec((4, s, bc), lambda j: (0, 0, j)), pl.BlockSpec((3, bc), lambda j: (0, j))],
        out_shape=[jax.ShapeDtypeStruct((4, s, e), BF16), jax.ShapeDtypeStruct((3, e), F32)],
        compiler_params=_params(("parallel",)),
    )(proj, dy, conv_w, *extra)


def _softplus_neg(lam):
    u = jnp.exp(-jnp.abs(lam))
    w = 1.0 + u
    log1p = jnp.where(w == 1.0, u, jnp.log(w) * (u / jnp.where(w == 1.0, 1.0, w - 1.0)))
    return jnp.maximum(-lam, 0.0) + log1p


def _one_minus_exp(z):
    series = -z * (1.0 + z * (0.5 + z * (1.0 / 6.0 + z * (1.0 / 24.0))))
    return jnp.where(z > -0.02, series, 1.0 - jnp.exp(z))


def _scan_in_tiles(a, b, reverse):
    shape = a.shape
    a, b = _tiles(a), _tiles(b)
    rows = lax.broadcasted_iota(jnp.int32, a.shape, 1)
    for step in (1, 2, 4):
        shift = SUBLANES - step if reverse else step
        ok = rows < SUBLANES - step if reverse else rows >= step
        a_s, b_s = pltpu.roll(a, shift, 1), pltpu.roll(b, shift, 1)
        b = jnp.where(ok, a * b_s + b, b)
        a = jnp.where(ok, a * a_s, a)
    return a.reshape(shape), b.reshape(shape)


def _by_rows(fn, arrays, rows=32):
    s = arrays[0].shape[0]
    rows = min(rows, s)
    for t in range(0, s, rows):
        fn(t, *(a[t:t + rows] for a in arrays))


def _scan_carry(a_ref, b_ref, h_ref, reverse):
    s, c = a_ref.shape
    n = s // 8

    def step(i, carry):
        gi = n - 1 - i if reverse else i
        sl = pl.ds(pl.multiple_of(gi * 8, 8), 8)
        h = b_ref[sl, :] + a_ref[sl, :] * carry
        h_ref[sl, :] = h
        return h[0:1, :] if reverse else h[7:8, :]

    lax.fori_loop(0, n, step, jnp.zeros((1, c), F32), unroll=8 if n % 8 == 0 else 1)


def _lru_specs(s, e_half, n_heads):
    hp = e_half // HEAD_DIM
    c = HEAD_DIM
    return dict(
        pair=pl.BlockSpec((2, None, s, c), lambda h: (0, h // hp, 0, h % hp)),
        conv_w=pl.BlockSpec((4, c), lambda h: (0, h)),
        chan=pl.BlockSpec((1, c), lambda h: (0, h)),
        w=pl.BlockSpec((4, None, c // 4, c), lambda h: (0, h, 0, 0)),
        bias=pl.BlockSpec((None, 1, c), lambda h: (h, 0, 0)),
        plane=pl.BlockSpec((s, c), lambda h: (0, h)),
    )


def _lru_gate_inputs(vp, cw_ref, cb_ref, wa_ref, ba_ref, wx_ref, bx_ref):
    c = HEAD_DIM
    taps = [_shift_down(vp, 3 - k) for k in range(4)]
    v = cb_ref[...] + sum(cw_ref[k:k + 1, :] * taps[k] for k in range(4))
    vb = v.astype(BF16)
    wa = wa_ref[...].reshape(c, c)
    wx = wx_ref[...].reshape(c, c)
    zr = jnp.dot(vb, wa, preferred_element_type=F32) + ba_ref[...]
    zi = jnp.dot(vb, wx, preferred_element_type=F32) + bx_ref[...]
    return taps, v, vb, wa, wx, zr, zi


def _lru_fwd(proj, conv_w, conv_b, w_a, b_a, w_x, b_x, lam, name):
    _, _, s, e_half = proj.shape
    n_heads = 2 * e_half // HEAD_DIM
    sp_ = _lru_specs(s, e_half, n_heads)

    def body(pg_ref, cw_ref, cb_ref, wa_ref, ba_ref, wx_ref, bx_ref, lam_ref,
             y_ref, a_ref, hs_ref, sa_ref, sb_ref, sh_ref):
        _, v, _, _, _, zr, zi = _lru_gate_inputs(
            pg_ref[0].astype(F32), cw_ref, cb_ref, wa_ref, ba_ref, wx_ref, bx_ref)
        rate = (-RGLRU_C) * _softplus_neg(lam_ref[...])

        def decay_and_input(t, v_c, zr_c, zi_c):
            la = rate * _sigmoid(zr_c)
            a = jnp.exp(la)
            b = jnp.sqrt(_one_minus_exp(2.0 * la)) * (_sigmoid(zi_c) * v_c)
            a_ref[t:t + a.shape[0]] = a
            sa_ref[t:t + a.shape[0]], sb_ref[t:t + a.shape[0]] = _scan_in_tiles(a, b, reverse=False)

        _by_rows(decay_and_input, [v, zr, zi])
        _scan_carry(sa_ref, sb_ref, sh_ref, reverse=False)

        def gated_output(t, hs_c, g_c):
            g = g_c.astype(F32)
            y_ref[t:t + g.shape[0]] = (hs_c * (g * _sigmoid(g))).astype(BF16)
            hs_ref[t:t + g.shape[0]] = hs_c.astype(BF16)

        _by_rows(gated_output, [sh_ref, pg_ref.at[1]])

    e = 2 * e_half
    return pl.pallas_call(
        body, name=name, grid=(n_heads,),
        in_specs=[sp_["pair"], sp_["conv_w"], sp_["chan"], sp_["w"], sp_["bias"], sp_["w"],
                  sp_["bias"], sp_["chan"]],
        out_specs=[sp_["plane"]] * 3,
        out_shape=[jax.ShapeDtypeStruct((s, e), BF16), jax.ShapeDtypeStruct((s, e), F32),
                   jax.ShapeDtypeStruct((s, e), BF16)],
        scratch_shapes=[pltpu.VMEM((s, HEAD_DIM), F32)] * 3,
        compiler_params=_params(("parallel",)),
    )(proj, conv_w, conv_b, w_a, b_a, w_x, b_x, lam)


def _lru_bwd(proj, dy, saved, conv_w, conv_b, w_a, b_a, w_x, b_x, lam, name):
    _, _, s, e_half = proj.shape
    e = 2 * e_half
    c = HEAD_DIM
    n_heads = e // c
    sp_ = _lru_specs(s, e_half, n_heads)

    def body(pg_ref, dy_ref, a_ref, hs_ref, cw_ref, cb_ref, wa_ref, ba_ref, wx_ref, bx_ref, lam_ref,
             dpg_ref, dwa_ref, dwx_ref, dba_ref, dbx_ref, dlam_ref, dcw_ref, dcb_ref,
             sa_ref, sb_ref, sd_ref, dzr_ref, dzi_ref):
        taps, v, vb, wa, wx, zr, zi = _lru_gate_inputs(
            pg_ref[0].astype(F32), cw_ref, cb_ref, wa_ref, ba_ref, wx_ref, bx_ref)
        lam = lam_ref[...]
        rate = (-RGLRU_C) * _softplus_neg(lam)
        a = a_ref[...]

        def state_gradient_in_tiles(t, a_next, dy_c, g_c, hs_c):
            g, dyv = g_c.astype(F32), dy_c.astype(F32)
            sig = _sigmoid(g)
            rows = slice(t, t + g.shape[0])
            dpg_ref[1, rows] = (dyv * hs_c.astype(F32) * (sig * (1.0 + g * (1.0 - sig)))).astype(BF16)
            sa_ref[rows], sb_ref[rows] = _scan_in_tiles(a_next, dyv * (g * sig), reverse=True)

        _by_rows(state_gradient_in_tiles, [_shift_up(a, 1), dy_ref, pg_ref.at[1], hs_ref])
        _scan_carry(sa_ref, sb_ref, sd_ref, reverse=True)

        sums = []

        def gate_gradients(t, dh, hs_before, a_c, zr_c, zi_c, v_c):
            r, i = _sigmoid(zr_c), _sigmoid(zi_c)
            q = (1.0 - a_c) * (1.0 + a_c)
            inv_nm = lax.rsqrt(q)
            div = dh * (q * inv_nm)
            dla = (dh * hs_before) * a_c - (dh * (i * v_c)) * (a_c * a_c * inv_nm)
            dzr = (dla * rate) * (r * (1.0 - r))
            dzi = (div * v_c) * (i * (1.0 - i))
            rows = slice(t, t + dh.shape[0])
            dzr_ref[rows], dzi_ref[rows] = dzr.astype(BF16), dzi.astype(BF16)
            sa_ref[rows] = div * i
            sums.append((_colsum(dla * r), _colsum(dzr), _colsum(dzi)))

        _by_rows(gate_gradients, [sd_ref, _shift_down(hs_ref[...].astype(F32), 1), a_ref, zr, zi, v])
        dlam_ref[...] = sum(p[0] for p in sums) * ((-RGLRU_C) * (-_sigmoid(-lam)))
        dba_ref[...] = sum(p[1] for p in sums)
        dbx_ref[...] = sum(p[2] for p in sums)
        dzr_b, dzi_b = dzr_ref[...], dzi_ref[...]
        tn = (((0,), (0,)), ((), ()))
        nt = (((1,), (1,)), ((), ()))
        dwa_ref[...] = lax.dot_general(vb, dzr_b, tn, preferred_element_type=F32)
        dwx_ref[...] = lax.dot_general(vb, dzi_b, tn, preferred_element_type=F32)
        dv = (sa_ref[...] + lax.dot_general(dzr_b, wa, nt, preferred_element_type=F32)
              + lax.dot_general(dzi_b, wx, nt, preferred_element_type=F32))
        dcb_ref[...] = _colsum(dv)
        dvp = jnp.zeros_like(dv)
        for k in range(4):
            dvp = dvp + cw_ref[k:k + 1, :] * _shift_up(dv, 3 - k)
            dcw_ref[k:k + 1, :] = _colsum(dv * taps[k])
        dpg_ref[0] = dvp.astype(BF16)

    head_mat = pl.BlockSpec((None, c, c), lambda h: (h, 0, 0))
    outs = pl.pallas_call(
        body, name=name, grid=(n_heads,),
        in_specs=[sp_["pair"]] + [sp_["plane"]] * 3 + [sp_["conv_w"], sp_["chan"], sp_["w"], sp_["bias"],
                                                        sp_["w"], sp_["bias"], sp_["chan"]],
        out_specs=[sp_["pair"], head_mat, head_mat, sp_["bias"], sp_["bias"],
                   sp_["chan"], sp_["conv_w"], sp_["chan"]],
        out_shape=[jax.ShapeDtypeStruct((2, 2, s, e_half), BF16),
                   jax.ShapeDtypeStruct((n_heads, c, c), F32), jax.ShapeDtypeStruct((n_heads, c, c), F32),
                   jax.ShapeDtypeStruct((n_heads, 1, c), F32), jax.ShapeDtypeStruct((n_heads, 1, c), F32),
                   jax.ShapeDtypeStruct((1, e), F32), jax.ShapeDtypeStruct((4, e), F32),
                   jax.ShapeDtypeStruct((1, e), F32)],
        scratch_shapes=[pltpu.VMEM((s, c), F32)] * 3 + [pltpu.VMEM((s, c), BF16)] * 2,
        compiler_params=_params(("parallel",)),
    )(proj, dy, *saved, conv_w, conv_b, w_a, b_a, w_x, b_x, lam)
    return tuple(outs)


def _ada_fwd(c_all, ada_w, name):
    n_l, d, f = ada_w.shape
    bf = _blk(f, 512)

    def body(c_ref, w_ref, o_ref):
        cv = c_ref[...]
        sc = (cv * _sigmoid(cv)).astype(BF16)
        o_ref[...] = jnp.dot(sc, w_ref[...].astype(BF16), preferred_element_type=F32)

    return pl.pallas_call(
        body, name=name, grid=(n_l, f // bf),
        in_specs=[pl.BlockSpec((8, d), lambda l, j: (0, 0)), pl.BlockSpec((None, d, bf), lambda l, j: (l, 0, j))],
        out_specs=pl.BlockSpec((None, 8, bf), lambda l, j: (l, 0, j)),
        out_shape=jax.ShapeDtypeStruct((n_l, 8, f), F32),
        compiler_params=_params(("parallel", "parallel")),
    )(c_all, ada_w)


def _ada_bwd_adamw(c_t, dmod, w, m, v, name, after=None):
    n_l, d, f = w.shape
    bf = _blk(f, 256)

    def body(c_ref, dm_ref, w_ref, m_ref, v_ref, g_ref, d_ref, m2_ref, v2_ref):
        cv = c_ref[...]
        sc = cv * _sigmoid(cv)
        dm = dm_ref[...]
        g = sc[:, 0:1] * dm[0:1, :]
        for b in range(1, 8):
            g = g + sc[:, b:b + 1] * dm[b:b + 1, :]
        g_ref[...] = g
        dl, m2, v2 = _adamw_math(w_ref[...], g, m_ref[...], v_ref[...])
        d_ref[...] = dl
        m2_ref[...] = m2
        v2_ref[...] = v2

    big = pl.BlockSpec((None, d, bf), lambda l, j: (l, 0, j))
    body, extra_specs, extra = _ordered(body, 5, after)
    return pl.pallas_call(
        body, name=name, grid=(n_l, f // bf),
        in_specs=[pl.BlockSpec((d, 8), lambda l, j: (0, 0)), pl.BlockSpec((None, 8, bf), lambda l, j: (l, 0, j)),
                  big, big, big] + extra_specs,
        out_specs=[big] * 4, out_shape=[jax.ShapeDtypeStruct((n_l, d, f), F32)] * 4,
        compiler_params=_params(("parallel", "parallel")),
    )(c_t, dmod, w, m, v, *extra)


def _pack(parts):
    padded, offs, n = [], [], 0
    for p in parts:
        p = p.reshape(-1)
        size = -(-p.shape[0] // PACK) * PACK
        offs.append(n)
        n += size
        padded.append(jnp.pad(p, (0, size - p.shape[0])) if size != p.shape[0] else p)
    return jnp.concatenate(padded), offs, n


def kernel(x, c, norm_g, ada_w, ada_b, sc_w_in, sc_conv_w, sc_w_out, lru_w_in, lru_conv_w, lru_conv_b, lru_w_a, lru_b_a, lru_w_x, lru_b_x, lru_lambda, lru_w_out, final_g, loss_target, m_norm_g, m_ada_w, m_ada_b, m_sc_w_in, m_sc_conv_w, m_sc_w_out, m_lru_w_in, m_lru_conv_w, m_lru_conv_b, m_lru_w_a, m_lru_b_a, m_lru_w_x, m_lru_b_x, m_lru_lambda, m_lru_w_out, m_final_g, v_norm_g, v_ada_w, v_ada_b, v_sc_w_in, v_sc_conv_w, v_sc_w_out, v_lru_w_in, v_lru_conv_w, v_lru_conv_b, v_lru_w_a, v_lru_b_a, v_lru_w_x, v_lru_b_x, v_lru_lambda, v_lru_w_out, v_final_g):
    xi, yi, ci = _pos()
    chip = 2 * xi + yi
    batch = 4 * xi + 2 * yi + ci
    core_op = jnp.reshape(ci, (1,)).astype(jnp.int32)
    chip_op = jnp.reshape(chip, (1,)).astype(jnp.int32)
    where_op = jnp.stack([chip, ci]).astype(jnp.int32)

    x2d, tgt = x[0], loss_target[0]
    s, d = x2d.shape
    es = sc_conv_w.shape[2]
    e = 4 * es
    n_heads = lru_w_a.shape[1]
    hj = lru_b_a.shape[2]
    f = ada_w.shape[2]
    row = lambda t: t.reshape(1, -1)

    small_parts = [c, sc_conv_w, lru_conv_w, lru_conv_b, lru_b_a, lru_b_x, lru_lambda]
    small, offs, n_small = _pack(small_parts)
    got = _allgather8([small.reshape(8, n_small // 8)], "ag_small")[0].reshape(8, n_small)
    c_all = got[:, :d]
    per_chip = got[0::2]

    def chip_part(k, shape):
        size = 1
        for dim in shape:
            size *= dim
        return per_chip[:, offs[k]:offs[k] + size].reshape((4,) + shape)

    conv_w0 = jnp.transpose(chip_part(1, (3, es)), (1, 0, 2)).reshape(3, e)
    conv_w1 = jnp.transpose(chip_part(2, (4, es)), (1, 0, 2)).reshape(4, e)
    conv_b1 = chip_part(3, (es,)).reshape(1, e)
    b_a = jnp.transpose(chip_part(4, (n_heads, hj)), (1, 0, 2)).reshape(n_heads, 1, 4 * hj)
    b_x = jnp.transpose(chip_part(5, (n_heads, hj)), (1, 0, 2)).reshape(n_heads, 1, 4 * hj)
    lam = chip_part(6, (es,)).reshape(1, e)

    mod_nb = _ada_fwd(c_all, ada_w, "ada_fwd")
    mods = _allgather8([mod_nb.reshape(16, f)], "ag_mod")[0].reshape(8, 2, 8, f)[0::2]
    mine = lax.dynamic_index_in_dim(mods, batch, axis=2, keepdims=False)
    mod = jnp.transpose(mine, (1, 0, 2)).reshape(2, 4 * f) + ada_b
    shift = [row(mod[l, :d]) for l in range(2)]
    scale = [row(mod[l, d:2 * d]) for l in range(2)]
    gate = [row(mod[l, 2 * d:]) for l in range(2)]
    ng = [row(norm_g[l]) for l in range(2)]

    shards = [sc_w_in[0], sc_w_out[0], lru_w_in[0], lru_w_a[0].reshape(n_heads * hj, HEAD_DIM),
              lru_w_x[0].reshape(n_heads * hj, HEAD_DIM), lru_w_out[0]]
    names = ["sc_w_in", "sc_w_out", "lru_w_in", "lru_w_a", "lru_w_x", "lru_w_out"]
    slots = [_cast_into_slot(chip_op, w, "cast_" + nm) for w, nm in zip(shards, names)]
    send_a, recv_a, buf, started = _first_start(slots[0], mod, "ag_first_start")
    h0 = _norm_mod_fwd(x2d, ng[0], scale[0], shift[0], "norm0", after=started)
    planes = [jnp.reshape(2 * px + py, (1,)).astype(jnp.int32) for px, py in _other_chips(xi, yi)]
    proj0 = _mm_plane(h0, buf.reshape(4, d, e), chip_op, "sc_in_own")
    relays, passed, relayed = [], [], [proj0] + slots[1:]
    for j in range(2):
        send_b, recv_b, send_c, recv_c, buf, token = _first_relay(send_a, recv_a, buf, j, relayed,
                                                                  "ag_first_relay_%d" % j)
        relays.append((send_c, recv_c))
        passed.append((send_b, recv_b))
        relayed = [token]
    rest_flight, relayed = _gather_start([[slots[1]], [slots[2]], slots[3:5], [slots[5]]], token, "ag_start")
    for j in range(2):
        buf = _first_relay_done(*passed[j], buf, j, relayed, "ag_first_relay_done_%d" % j)
        proj0 = _mm_plane(h0, buf.reshape(4, d, e), planes[j], "sc_in_%d" % j, planes_so_far=proj0)
        relayed = proj0
    send_b, recv_b, buf, relayed = _first_diagonal(relays, buf, proj0, "ag_first_diagonal")
    buf = _first_diagonal_done(send_b, recv_b, relays, buf, relayed, "ag_first_diagonal_done")
    proj0 = _mm_plane(h0, buf.reshape(4, d, e), planes[2], "sc_in_2", planes_so_far=proj0)
    w_in0 = buf.reshape(4, d, e)
    in_flight = [None] + rest_flight

    def arrived(g, after, tag):
        send1, recv1, bufs = in_flight[g]
        send2, recv2, bufs, passed = _gather_forward(send1, recv1, bufs, after, "ag_forward_" + tag)
        return _gather_finish(send2, recv2, bufs, passed, "ag_finish_" + tag)

    y0 = _sc_fwd(proj0, conv_w0, "sc_mix")
    w_out0 = arrived(1, y0, "sc_w_out")[0].reshape(1, e, d)
    o0 = _mm(y0[None], w_out0, "nn", F32, "sc_out")[0]
    x1, h1 = _norm_mod_fwd(x2d, ng[1], scale[1], shift[1], "norm1", o=o0, gate=gate[0])
    send1, recv1, bufs = in_flight[2]
    proj1 = _mm_plane(h1, bufs[0].reshape(4, d, e // 2), chip_op, "lru_in_own")
    for j in range(3):
        send2, recv2, bufs, passed_on = _gather_forward(send1, recv1, bufs, proj1, "ag_forward_lru_w_in_%d" % j,
                                                        sources=(j,))
        bufs = _gather_finish(send2, recv2, bufs, passed_on, "ag_finish_lru_w_in_%d" % j, sources=(j,))
        proj1 = _mm_plane(h1, bufs[0].reshape(4, d, e // 2), planes[j], "lru_in_%d" % j, planes_so_far=proj1)
    w_in1 = bufs[0].reshape(4, d, e // 2)
    gate_ws = arrived(3, proj1, "lru_gates")
    w_a = gate_ws[0].reshape(4, n_heads, hj, HEAD_DIM)
    w_x = gate_ws[1].reshape(4, n_heads, hj, HEAD_DIM)
    pairs1 = proj1.reshape(2, 2, s, e // 2)
    y1, *lru_saved = _lru_fwd(pairs1, conv_w1, conv_b1, w_a, b_a, w_x, b_x, lam, "lru_mix")
    w_out1 = arrived(4, y1, "lru_w_out")[0].reshape(1, e, d)
    o1 = _mm(y1[None], w_out1, "nn", F32, "lru_out")[0]
    dx2, do1, acc_f = _final_loss(x1, o1, gate[1], row(final_g), tgt, "final_loss")

    def reduce_stage1(tag, grads):
        lands = [lax.empty((4, g.shape[1] // 2, g.shape[2]), F32) for g in grads]
        return _exchange_start("rs_sibling_start_" + tag, grads, lands, len(grads),
                               _plan_other_half_to_sibling, None)

    def reduce_stage2(tag, stage1, nms, after):
        send, recv, grads, lands, _ = stage1
        grads, lands = _exchange_wait("rs_sibling_wait_" + tag, send, recv, grads, lands,
                                      _plan_other_half_to_sibling, after)
        parts = [_sum_own_and_sibling(core_op, g, r1, "rs_sum1_" + nm) for g, r1, nm in zip(grads, lands, nms)]
        lands = [lax.empty((3,) + p.shape[1:], BF16) for p in parts]
        return _exchange_start("rs_chips_start_" + tag, parts, lands, 3 * len(parts),
                               _plan_partials_to_chips, None)

    def reduce_stage3(tag, stage2, nms, after):
        send, recv, parts, lands, _ = stage2
        parts, lands = _exchange_wait("rs_chips_wait_" + tag, send, recv, parts, lands,
                                      _plan_partials_to_chips, after)
        halves = [_sum_chips(where_op, p, r2, "rs_sum2_" + nm) for p, r2, nm in zip(parts, lands, nms)]
        return _exchange_start("rs_share_start_" + tag, [], halves, len(halves), _plan_share_half, None)

    def reduce_done(tag, stage3, after):
        send, recv, _, fulls, _ = stage3
        return _exchange_wait("rs_share_wait_" + tag, send, recv, [], fulls, _plan_share_half, after)[1]

    def head_major_to_chip_major(t):
        return jnp.transpose(t.reshape(n_heads, 4, hj, HEAD_DIM), (1, 0, 2, 3)).reshape(4, n_heads * hj, HEAD_DIM)

    big_state = dict(zip(names, zip(shards, [m_sc_w_in, m_sc_w_out, m_lru_w_in, m_lru_w_a, m_lru_w_x, m_lru_w_out],
                                    [v_sc_w_in, v_sc_w_out, v_lru_w_in, v_lru_w_a, v_lru_w_x, v_lru_w_out],
                                    [sc_w_in, sc_w_out, lru_w_in, lru_w_a, lru_w_x, lru_w_out])))
    big = {}

    def update(nms, fulls, after):
        for nm, g2 in zip(nms, fulls):
            w2, m4, v4, w4 = big_state[nm]
            outs = _adamw(w2, g2, m4.reshape(w2.shape), v4.reshape(w2.shape), "adamw_" + nm, after=after)
            big[nm] = tuple(t.reshape(w4.shape) for t in outs)
            after = outs[1]
        return after

    g_w_out1 = _mm(y1[None], do1[None], "tn", F32, "lru_out_dw", bm=512, bn=2048)
    dy1 = _mm(do1[None], w_out1, "nt", BF16, "lru_out_dx")[0]
    dpairs1, g_wa, g_wx, g_ba, g_bx, g_lam, g_cw1, g_cb1 = _lru_bwd(
        pairs1, dy1, lru_saved, conv_w1, conv_b1, w_a, b_a, w_x, b_x, lam, "lru_mix_bwd")
    dproj1 = dpairs1.reshape(4, s, e // 2)
    g_w_in1 = _mm(h1[None], dproj1, "tn", F32, "lru_in_dw", bm=1024, bn=2048)
    lru_names = ["lru_w_out", "lru_w_a", "lru_w_x", "lru_w_in"]
    lru_rs = reduce_stage1("lru", [g_w_out1.reshape(4, es, d), head_major_to_chip_major(g_wa),
                                   head_major_to_chip_major(g_wx), g_w_in1])
    dh1 = _mm(dproj1, w_in1, "nt", F32, "lru_in_dx", after=lru_rs[4])[0]
    lru_rs = reduce_stage2("lru", lru_rs, lru_names, dh1)
    dx1, do0, acc1 = _norm_mod_bwd(dh1, x1, ng[1], scale[1], dx2, "norm1_bwd", o_prev=o0, gate_prev=gate[0],
                                   after=lru_rs[4])

    g_w_out0 = _mm(y0[None], do0[None], "tn", F32, "sc_out_dw", bm=512, bn=2048)
    out_rs = reduce_stage1("sc_out", [g_w_out0.reshape(4, es, d)])
    dy0 = _mm(do0[None], w_out0, "nt", BF16, "sc_out_dx", after=out_rs[4])[0]
    out_rs = reduce_stage2("sc_out", out_rs, ["sc_w_out"], dy0)
    dproj0, g_cw0 = _sc_bwd(proj0, dy0, conv_w0, "sc_mix_bwd", after=out_rs[4])
    stage1, stage2, after = [], [], None
    for k in range(IN_PIECES):
        g_piece = _mm(h0[None], dproj0, "tn", F32, "sc_in_dw_%d" % k, bm=1024, bn=e // IN_PIECES,
                      col_blocks=(k, 1), after=after)
        stage1.append(reduce_stage1("sc_in_%d" % k, [g_piece]))
        after = stage1[k][4]
        if k:
            stage2.append(reduce_stage2("sc_in_%d" % (k - 1), stage1[k - 1], ["sc_w_in_%d" % (k - 1)], after))
            after = stage2[k - 1][4]
    lru_rs = reduce_stage3("lru", lru_rs, lru_names, after)
    k = IN_PIECES - 1
    stage2.append(reduce_stage2("sc_in_%d" % k, stage1[k], ["sc_w_in_%d" % k], lru_rs[4]))
    dh0 = _mm(dproj0, w_in0, "nt", F32, "sc_in_dx", bn=1024, after=stage2[k][4])[0]
    grad_x, acc0 = _norm_mod_bwd(dh0, x2d, ng[0], scale[0], dx1, "norm0_bwd")
    out_rs = reduce_stage3("sc_out", out_rs, ["sc_w_out"], acc0)
    last = update(lru_names, reduce_done("lru", lru_rs, out_rs[4]), None)
    last = update(["sc_w_out"], reduce_done("sc_out", out_rs, last), None)
    g_half = None
    for k, (send, recv, parts, lands, _) in enumerate(stage2):
        parts, lands = _exchange_wait("rs_chips_wait_sc_in_%d" % k, send, recv, parts, lands,
                                      _plan_partials_to_chips, last)
        g_half = _sum_chips(where_op, parts[0], lands[0], "rs_sum2_sc_w_in_%d" % k, piece=(k, IN_PIECES),
                            so_far=g_half)
    in_rs = _exchange_start("rs_share_start_sc_in", [], [g_half], 1, _plan_share_half, None)

    dmod = jnp.stack([jnp.concatenate([acc0[1], acc0[0], acc1[3]]),
                      jnp.concatenate([acc1[1], acc1[0], acc_f[1]])])
    part_list = [jnp.stack([acc0[2], acc1[2]]), acc_f[0], acc_f[2, :1], g_cw0, g_cw1, g_cb1, g_ba, g_bx,
                 g_lam, dmod]
    partials, poffs, n_part = _pack(part_list)
    every = _allgather8([partials.reshape(8, n_part // 8)], "ag_partials", after=in_rs[4])[0].reshape(8, n_part)
    total = _sum_rows8(every, "sum_partials")[0]

    def tot(k, shape):
        size = 1
        for dim in shape:
            size *= dim
        return total[poffs[k]:poffs[k] + size].reshape(shape)

    def my_cols(t, width):
        return lax.dynamic_slice_in_dim(t, chip * width, width, axis=t.ndim - 1)

    loss = tot(2, (1,))[0]
    g_norm_g, g_final_g, g_ada_b = tot(0, (2, d)), tot(1, (d,)), tot(9, (2, 3 * d))
    g_sc_conv_w = my_cols(tot(3, (3, e)), es)[None]
    g_lru_conv_w = my_cols(tot(4, (4, e)), es)[None]
    g_lru_conv_b = my_cols(tot(5, (1, e)), es)
    g_lru_b_a = my_cols(tot(6, (n_heads, 4 * hj)), hj)[None]
    g_lru_b_x = my_cols(tot(7, (n_heads, 4 * hj)), hj)[None]
    g_lru_lambda = my_cols(tot(8, (1, e)), es)

    dmod_all = every[:, poffs[9]:poffs[9] + 6 * d].reshape(8, 2, 3 * d)
    dmod_mine = jnp.transpose(my_cols(dmod_all, f), (1, 0, 2))
    ada = _ada_bwd_adamw(jnp.transpose(c_all), dmod_mine, ada_w, m_ada_w, v_ada_w, "ada_bwd_adamw", after=total)

    small_names = ["norm_g", "ada_b", "final_g", "sc_conv_w", "lru_conv_w", "lru_conv_b", "lru_b_a",
                   "lru_b_x", "lru_lambda"]
    small_w = [norm_g, ada_b, final_g, sc_conv_w, lru_conv_w, lru_conv_b, lru_b_a, lru_b_x, lru_lambda]
    small_g = [g_norm_g, g_ada_b, g_final_g, g_sc_conv_w, g_lru_conv_w, g_lru_conv_b, g_lru_b_a,
               g_lru_b_x, g_lru_lambda]
    small_m = [m_norm_g, m_ada_b, m_final_g, m_sc_conv_w, m_lru_conv_w, m_lru_conv_b, m_lru_b_a,
               m_lru_b_x, m_lru_lambda]
    small_v = [v_norm_g, v_ada_b, v_final_g, v_sc_conv_w, v_lru_conv_w, v_lru_conv_b, v_lru_b_a,
               v_lru_b_x, v_lru_lambda]
    pw, soffs, n_s = _pack(small_w)
    pg, pm, pv = _pack(small_g)[0], _pack(small_m)[0], _pack(small_v)[0]
    shape2 = (n_s // PACK, PACK)
    _, pd, pm2, pv2 = _adamw(pw.reshape(shape2), pg.reshape(shape2), pm.reshape(shape2), pv.reshape(shape2),
                             "adamw_small", after=ada[0])
    update(["sc_w_in"], reduce_done("sc_in", in_rs, pd), None)
    small = {}
    for k, (nm, w_) in enumerate(zip(small_names, small_w)):
        take = lambda t: t.reshape(-1)[soffs[k]:soffs[k] + w_.size].reshape(w_.shape)
        small[nm] = (small_g[k].reshape(w_.shape), take(pd), take(pm2), take(pv2))

    results = dict(small)
    results.update(big)
    results["ada_w"] = tuple(ada)
    order = ["norm_g", "ada_w", "ada_b", "sc_w_in", "sc_conv_w", "sc_w_out", "lru_w_in", "lru_conv_w",
             "lru_conv_b", "lru_w_a", "lru_b_a", "lru_w_x", "lru_b_x", "lru_lambda", "lru_w_out", "final_g"]
    out = [loss, grad_x[None]]
    for kind in range(4):
        out += [results[nm][kind] for nm in order]
    return tuple(out)
```

```python
import jax
import jax.numpy as jnp
from jax import lax
from jax.experimental import pallas as pl
from jax.experimental.pallas import tpu as pltpu

F32 = jnp.float32
BF16 = jnp.bfloat16
MESH = pl.DeviceIdType.MESH
ANY = pl.BlockSpec(memory_space=pl.ANY)

RMS_EPS = 1e-6
RGLRU_C = 8.0
HEAD_DIM = 256
ADAM_LR = 0.001
ADAM_B1 = 0.9
ADAM_B2 = 0.999
ADAM_EPS = 1e-08
ADAM_WD = 0.01
ADAM_STEP = 10
V7X_VMEM_LIMIT = 56 * 1024 * 1024
IN_PIECES = 2
LANES = 128
SUBLANES = 8
PACK = SUBLANES * LANES


def _blk(dim, pref, unit=LANES):
    if dim <= pref:
        return dim
    b = (pref // unit) * unit
    while b > unit and dim % b:
        b -= unit
    assert dim % b == 0, (dim, pref, unit)
    return b


def _params(sem=None):
    return pltpu.CompilerParams(dimension_semantics=sem, vmem_limit_bytes=V7X_VMEM_LIMIT)


def _ordered(body, n_in, after):
    if after is None:
        return body, [], []

    def ordered_body(*refs):
        return body(*refs[:n_in], *refs[n_in + 1:])

    return ordered_body, [ANY], [after]


def _pos():
    return lax.axis_index("x"), lax.axis_index("y"), lax.axis_index("c")


def _other_chips(x, y):
    return [(1 - x, y), (x, 1 - y), (1 - x, 1 - y)]


def _allgather8(arrs, name, after=None):
    n_t = len(arrs)
    ms = [a.shape[0] for a in arrs]

    def gather(*refs):
        ins, outs = refs[:n_t], refs[n_t:2 * n_t]
        send_sems, recv_sems, local_sems = refs[2 * n_t:]
        x, y, c = _pos()
        me, sibling = (x, y, c), (x, y, 1 - c)
        chips = _other_chips(x, y)

        def rows(t, px, py, pc):
            return outs[t].at[pl.ds((4 * px + 2 * py + pc) * ms[t], ms[t])]

        def copy(t, k, block, to, src=None):
            return pltpu.make_async_remote_copy(
                src_ref=rows(t, *block) if src is None else src, dst_ref=rows(t, *block),
                send_sem=send_sems.at[7 * t + k], recv_sem=recv_sems.at[7 * t + k],
                device_id=to, device_id_type=MESH)

        mine, first, passed = [], [], []
        for t in range(n_t):
            src = ins[t]
            cp = pltpu.make_async_copy(src, rows(t, *me), local_sems.at[t])
            cp.start()
            mine.append(cp)
            sends = [copy(t, 0, me, sibling, src=src)]
            sends += [copy(t, 1 + j, me, (*chip, c), src=src) for j, chip in enumerate(chips)]
            for cp in sends:
                cp.start()
            first += sends
        for t in range(n_t):
            for j, chip in enumerate(chips):
                copy(t, 1 + j, (*chip, c), me).wait_recv()
                cp = copy(t, 4 + j, (*chip, c), sibling)
                cp.start()
                passed.append(cp)
        for t in range(n_t):
            copy(t, 0, sibling, me).wait_recv()
            for j, chip in enumerate(chips):
                copy(t, 4 + j, (*chip, 1 - c), me).wait_recv()
        for cp in first + passed:
            cp.wait_send()
        for cp in mine:
            cp.wait()

    body, extra_specs, extra = _ordered(gather, n_t, after)
    return pl.pallas_call(
        body, name=name,
        out_shape=[jax.ShapeDtypeStruct((8 * m, a.shape[1]), a.dtype) for m, a in zip(ms, arrs)],
        in_specs=[ANY] * n_t + extra_specs, out_specs=[ANY] * n_t,
        scratch_shapes=[pltpu.SemaphoreType.DMA((7 * n_t,)), pltpu.SemaphoreType.DMA((7 * n_t,)),
                        pltpu.SemaphoreType.DMA((n_t,))],
    )(*arrs, *extra)


HBM = pl.BlockSpec(memory_space=pltpu.HBM)
SEM = pl.BlockSpec(memory_space=pltpu.SEMAPHORE)
TOKEN = pl.BlockSpec(memory_space=pltpu.VMEM)
IN_FLIGHT = pltpu.CompilerParams(has_side_effects=pltpu.SideEffectType.DATAFLOW_SIDE_EFFECTING)


def _in_hbm(arrs):
    return [pltpu.with_memory_space_constraint(a, pltpu.HBM) for a in arrs]


def _shard_rows(ref, h, px, py, pc):
    return ref.at[pl.ds((4 * px + 2 * py + pc) * h, h)]


def _gather_start(groups, after, name):
    bufs = [b for grp in groups for b in grp]
    n_t, n_g = len(bufs), len(groups)

    def body(*refs):
        sems, thru = refs[n_t + 1:n_t + 1 + 2 * n_g], refs[n_t + 1 + 2 * n_g:2 * n_t + 1 + 2 * n_g]
        token = refs[-1]
        x, y, c = _pos()
        t = 0
        for g, grp in enumerate(groups):
            for i in range(len(grp)):
                h = bufs[t].shape[0] // 8
                rows = _shard_rows(thru[t], h, x, y, c)
                for j, chip in enumerate(_other_chips(x, y)):
                    pltpu.make_async_remote_copy(
                        src_ref=rows, dst_ref=rows, send_sem=sems[2 * g].at[3 * i + j],
                        recv_sem=sems[2 * g + 1].at[3 * i + j], device_id=(*chip, c),
                        device_id_type=MESH).start()
                t += 1
        token[...] = jnp.zeros_like(token)

    sem_shapes = []
    for grp in groups:
        sem_shapes += [pltpu.SemaphoreType.DMA((3 * len(grp),))] * 2
    out = pl.pallas_call(
        body, name=name,
        out_shape=sem_shapes + [pltpu.HBM(b.shape, b.dtype) for b in bufs] + [jax.ShapeDtypeStruct((8, LANES), F32)],
        in_specs=[HBM] * n_t + [ANY], out_specs=[SEM] * (2 * n_g) + [HBM] * n_t + [TOKEN],
        input_output_aliases={t: 2 * n_g + t for t in range(n_t)},
        compiler_params=IN_FLIGHT,
    )(*_in_hbm(bufs), after)
    sems, thru, token = out[:2 * n_g], out[2 * n_g:2 * n_g + n_t], out[-1]
    per_group, t = [], 0
    for g, grp in enumerate(groups):
        per_group.append((sems[2 * g], sems[2 * g + 1], thru[t:t + len(grp)]))
        t += len(grp)
    return per_group, token


def _gather_forward(send_sems, recv_sems, bufs, after, name, sources=(0, 1, 2)):
    n_t = len(bufs)

    def body(*refs):
        ins = refs[:n_t]
        send1, recv1 = refs[n_t], refs[n_t + 1]
        send2, recv2 = refs[n_t + 3], refs[n_t + 4]
        token = refs[-1]
        x, y, c = _pos()
        chips = _other_chips(x, y)
        for t in range(n_t):
            h = bufs[t].shape[0] // 8
            mine = _shard_rows(ins[t], h, x, y, c)
            for j in sources:
                chip = chips[j]
                landed = _shard_rows(ins[t], h, *chip, c)
                pltpu.make_async_remote_copy(
                    src_ref=mine, dst_ref=landed, send_sem=send1.at[3 * t + j], recv_sem=recv1.at[3 * t + j],
                    device_id=(*chip, c), device_id_type=MESH).wait_recv()
                pltpu.make_async_remote_copy(
                    src_ref=landed, dst_ref=landed, send_sem=send2.at[3 * t + j], recv_sem=recv2.at[3 * t + j],
                    device_id=(x, y, 1 - c), device_id_type=MESH).start()
        for t in range(n_t):
            h = bufs[t].shape[0] // 8
            mine = _shard_rows(ins[t], h, x, y, c)
            for j in sources:
                pltpu.make_async_remote_copy(
                    src_ref=mine, dst_ref=mine, send_sem=send1.at[3 * t + j], recv_sem=recv1.at[3 * t + j],
                    device_id=(*chips[j], c), device_id_type=MESH).wait_send()
        token[...] = jnp.zeros_like(token)

    out = pl.pallas_call(
        body, name=name,
        out_shape=[pltpu.SemaphoreType.DMA((3 * n_t,))] * 2 + [pltpu.HBM(b.shape, b.dtype) for b in bufs]
        + [jax.ShapeDtypeStruct((8, LANES), F32)],
        in_specs=[HBM] * n_t + [SEM, SEM, ANY], out_specs=[SEM, SEM] + [HBM] * n_t + [TOKEN],
        input_output_aliases={t: 2 + t for t in range(n_t)},
        compiler_params=IN_FLIGHT,
    )(*bufs, send_sems, recv_sems, after)
    return out[0], out[1], out[2:2 + n_t], out[-1]


def _gather_finish(send_sems, recv_sems, bufs, after, name, sources=(0, 1, 2)):
    n_t = len(bufs)

    def body(*refs):
        ins = refs[:n_t]
        send2, recv2 = refs[n_t], refs[n_t + 1]
        x, y, c = _pos()
        chips = _other_chips(x, y)
        for t in range(n_t):
            h = bufs[t].shape[0] // 8
            for j in sources:
                chip = chips[j]
                sent = _shard_rows(ins[t], h, *chip, c)
                got = _shard_rows(ins[t], h, *chip, 1 - c)
                cp = pltpu.make_async_remote_copy(
                    src_ref=sent, dst_ref=got, send_sem=send2.at[3 * t + j], recv_sem=recv2.at[3 * t + j],
                    device_id=(x, y, 1 - c), device_id_type=MESH)
                cp.wait_send()
                cp.wait_recv()

    return pl.pallas_call(
        body, name=name, out_shape=[pltpu.HBM(b.shape, b.dtype) for b in bufs],
        in_specs=[HBM] * n_t + [SEM, SEM, ANY], out_specs=[HBM] * n_t,
        input_output_aliases={t: t for t in range(n_t)},
        compiler_params=IN_FLIGHT,
    )(*bufs, send_sems, recv_sems, after)


def _part_rows(ref, h, px, py, pc, which):
    return ref.at[pl.ds((4 * px + 2 * py + pc) * h + which * (h // 2), h // 2)]


def _remote(src, dst, send_sem, recv_sem, device):
    return pltpu.make_async_remote_copy(src_ref=src, dst_ref=dst, send_sem=send_sem, recv_sem=recv_sem,
                                        device_id=device, device_id_type=MESH)


def _first_start(buf, after, name):
    h = buf.shape[0] // 8

    def body(buf_in, after_ref, send, recv, thru, token):
        x, y, c = _pos()
        rows = _shard_rows(thru, h, x, y, c)
        for j, chip in enumerate(_other_chips(x, y)[:2]):
            _remote(rows, rows, send.at[j], recv.at[j], (*chip, c)).start()
        token[...] = jnp.zeros_like(token)

    return pl.pallas_call(
        body, name=name,
        out_shape=[pltpu.SemaphoreType.DMA((2,))] * 2 + [pltpu.HBM(buf.shape, buf.dtype),
                                                         jax.ShapeDtypeStruct((8, LANES), F32)],
        in_specs=[HBM, ANY], out_specs=[SEM, SEM, HBM, TOKEN], input_output_aliases={0: 2},
        compiler_params=IN_FLIGHT,
    )(*_in_hbm([buf]), after)


def _first_relay(send_a, recv_a, buf, j, after, name):
    h = buf.shape[0] // 8

    def body(buf_in, send_a, recv_a, *rest):
        send_b, recv_b, send_c, recv_c, _, token = rest[len(after):]
        x, y, c = _pos()
        chips = _other_chips(x, y)
        nbr, other = chips[j], chips[1 - j]
        mine, landed = _shard_rows(buf_in, h, x, y, c), _shard_rows(buf_in, h, *nbr, c)
        _remote(mine, landed, send_a.at[j], recv_a.at[j], (*nbr, c)).wait_recv()
        _remote(landed, landed, send_b.at[0], recv_b.at[0], (x, y, 1 - c)).start()
        part = _part_rows(buf_in, h, *nbr, c, j)
        _remote(part, part, send_c.at[0], recv_c.at[0], (*other, c)).start()
        _remote(mine, mine, send_a.at[j], recv_a.at[j], (*nbr, c)).wait_send()
        token[...] = jnp.zeros_like(token)

    out = pl.pallas_call(
        body, name=name,
        out_shape=[pltpu.SemaphoreType.DMA((1,))] * 4 + [pltpu.HBM(buf.shape, buf.dtype),
                                                         jax.ShapeDtypeStruct((8, LANES), F32)],
        in_specs=[HBM, SEM, SEM] + [ANY] * len(after), out_specs=[SEM] * 4 + [HBM, TOKEN],
        input_output_aliases={0: 4}, compiler_params=IN_FLIGHT,
    )(buf, send_a, recv_a, *after)
    return tuple(out)


def _first_relay_done(send_b, recv_b, buf, j, after, name):
    h = buf.shape[0] // 8

    def body(buf_in, send_b, recv_b, after_ref, thru):
        x, y, c = _pos()
        nbr = _other_chips(x, y)[j]
        cp = _remote(_shard_rows(buf_in, h, *nbr, c), _shard_rows(buf_in, h, *nbr, 1 - c),
                     send_b.at[0], recv_b.at[0], (x, y, 1 - c))
        cp.wait_send()
        cp.wait_recv()

    return pl.pallas_call(
        body, name=name, out_shape=pltpu.HBM(buf.shape, buf.dtype),
        in_specs=[HBM, SEM, SEM, ANY], out_specs=HBM, input_output_aliases={0: 0},
        compiler_params=IN_FLIGHT,
    )(buf, send_b, recv_b, after)


def _first_diagonal(relays, buf, after, name):
    h = buf.shape[0] // 8

    def body(buf_in, send_c0, recv_c0, send_c1, recv_c1, after_ref, send_b, recv_b, thru, token):
        x, y, c = _pos()
        chips = _other_chips(x, y)
        diag = chips[2]
        for j, (send_c, recv_c) in enumerate(((send_c0, recv_c0), (send_c1, recv_c1))):
            part = _part_rows(buf_in, h, *diag, c, j)
            _remote(part, part, send_c.at[0], recv_c.at[0], (*chips[1 - j], c)).wait_recv()
        whole = _shard_rows(buf_in, h, *diag, c)
        _remote(whole, whole, send_b.at[0], recv_b.at[0], (x, y, 1 - c)).start()
        token[...] = jnp.zeros_like(token)

    out = pl.pallas_call(
        body, name=name,
        out_shape=[pltpu.SemaphoreType.DMA((1,))] * 2 + [pltpu.HBM(buf.shape, buf.dtype),
                                                         jax.ShapeDtypeStruct((8, LANES), F32)],
        in_specs=[HBM] + [SEM] * 4 + [ANY], out_specs=[SEM, SEM, HBM, TOKEN], input_output_aliases={0: 2},
        compiler_params=IN_FLIGHT,
    )(buf, relays[0][0], relays[0][1], relays[1][0], relays[1][1], after)
    return tuple(out)


def _first_diagonal_done(send_b, recv_b, relays, buf, after, name):
    h = buf.shape[0] // 8

    def body(buf_in, send_b, recv_b, send_c0, recv_c0, send_c1, recv_c1, after_ref, thru):
        x, y, c = _pos()
        chips = _other_chips(x, y)
        diag = chips[2]
        cp = _remote(_shard_rows(buf_in, h, *diag, c), _shard_rows(buf_in, h, *diag, 1 - c),
                     send_b.at[0], recv_b.at[0], (x, y, 1 - c))
        cp.wait_send()
        cp.wait_recv()
        for j, (send_c, recv_c) in enumerate(((send_c0, recv_c0), (send_c1, recv_c1))):
            part = _part_rows(buf_in, h, *chips[j], c, j)
            _remote(part, part, send_c.at[0], recv_c.at[0], (*chips[1 - j], c)).wait_send()

    return pl.pallas_call(
        body, name=name, out_shape=pltpu.HBM(buf.shape, buf.dtype),
        in_specs=[HBM] + [SEM] * 6 + [ANY], out_specs=HBM, input_output_aliases={0: 0},
        compiler_params=IN_FLIGHT,
    )(buf, send_b, recv_b, relays[0][0], relays[0][1], relays[1][0], relays[1][1], after)


def _exchange_start(name, srcs, lands, n_copies, plan, after):
    ns, nl = len(srcs), len(lands)
    extra = [] if after is None else [after]

    def body(*refs):
        base = ns + nl + len(extra)
        send_sems, recv_sems = refs[base], refs[base + 1]
        src_refs, land_refs = refs[base + 2:base + 2 + ns], refs[base + 2 + ns:base + 2 + ns + nl]
        token = refs[-1]
        x, y, c = _pos()
        copies = plan(src_refs, land_refs, x, y, c)
        assert len(copies) == n_copies
        for k, (src, dst, dev) in enumerate(copies):
            pltpu.make_async_remote_copy(
                src_ref=src, dst_ref=dst, send_sem=send_sems.at[k], recv_sem=recv_sems.at[k],
                device_id=dev, device_id_type=MESH).start()
        token[...] = jnp.zeros_like(token)

    out = pl.pallas_call(
        body, name=name,
        out_shape=[pltpu.SemaphoreType.DMA((n_copies,))] * 2
        + [pltpu.HBM(a.shape, a.dtype) for a in list(srcs) + list(lands)] + [jax.ShapeDtypeStruct((8, LANES), F32)],
        in_specs=[HBM] * (ns + nl) + [ANY] * len(extra), out_specs=[SEM, SEM] + [HBM] * (ns + nl) + [TOKEN],
        input_output_aliases={i: 2 + i for i in range(ns + nl)},
        compiler_params=IN_FLIGHT,
    )(*_in_hbm(list(srcs) + list(lands)), *extra)
    return out[0], out[1], out[2:2 + ns], out[2 + ns:2 + ns + nl], out[-1]


def _exchange_wait(name, send_sems, recv_sems, srcs, lands, plan, after):
    ns, nl = len(srcs), len(lands)

    def body(*refs):
        src_refs, land_refs = refs[:ns], refs[ns:ns + nl]
        send, recv = refs[ns + nl], refs[ns + nl + 1]
        x, y, c = _pos()
        for k, (src, dst, dev) in enumerate(plan(src_refs, land_refs, x, y, c)):
            cp = pltpu.make_async_remote_copy(
                src_ref=src, dst_ref=dst, send_sem=send.at[k], recv_sem=recv.at[k],
                device_id=dev, device_id_type=MESH)
            cp.wait_send()
            cp.wait_recv()

    out = pl.pallas_call(
        body, name=name, out_shape=[pltpu.HBM(a.shape, a.dtype) for a in list(srcs) + list(lands)],
        in_specs=[HBM] * (ns + nl) + [SEM, SEM, ANY], out_specs=[HBM] * (ns + nl),
        input_output_aliases={i: i for i in range(ns + nl)},
        compiler_params=IN_FLIGHT,
    )(*srcs, *lands, send_sems, recv_sems, after)
    return out[:ns], out[ns:]


def _plan_other_half_to_sibling(src_refs, land_refs, x, y, c):
    out = []
    for g_ref, r_ref in zip(src_refs, land_refs):
        h = g_ref.shape[1] // 2
        out.append((g_ref.at[:, pl.ds((1 - c) * h, h), :], r_ref, (x, y, 1 - c)))
    return out


def _plan_partials_to_chips(src_refs, land_refs, x, y, c):
    out = []
    for p_ref, r_ref in zip(src_refs, land_refs):
        for j, (px, py) in enumerate(_other_chips(x, y)):
            out.append((p_ref.at[2 * px + py], r_ref.at[j], (px, py, c)))
    return out


def _plan_share_half(src_refs, land_refs, x, y, c):
    out = []
    for f_ref in land_refs:
        h = f_ref.shape[0] // 2
        rows = f_ref.at[pl.ds(c * h, h)]
        out.append((rows, rows, (x, y, 1 - c)))
    return out


def _cast_into_slot(chip, w, name):
    r, c = w.shape
    br, bc = _blk(r, 512, 8), _blk(c, 2048)
    nb = r // br

    def body(chip_ref, w_ref, o_ref):
        o_ref[...] = w_ref[...].astype(BF16)

    grid_spec = pltpu.PrefetchScalarGridSpec(
        num_scalar_prefetch=1, grid=(nb, c // bc),
        in_specs=[pl.BlockSpec((br, bc), lambda i, j, chip_ref: (i, j))],
        out_specs=pl.BlockSpec((br, bc), lambda i, j, chip_ref: (chip_ref[0] * nb + i, j)))
    return pl.pallas_call(
        body, name=name, grid_spec=grid_spec,
        out_shape=jax.ShapeDtypeStruct((4 * r, c), BF16),
        compiler_params=_params(("parallel", "parallel")),
    )(chip, w)


def _sum_own_and_sibling(core, g, r1, name):
    _, r, c = g.shape
    h = r // 2
    br, bc = _blk(h, 512, 8), _blk(c, 2048)
    nb = h // br

    def body(core_ref, g_ref, r_ref, o_ref):
        o_ref[...] = (g_ref[...] + r_ref[...]).astype(BF16)

    grid_spec = pltpu.PrefetchScalarGridSpec(
        num_scalar_prefetch=1, grid=(4, nb, c // bc),
        in_specs=[pl.BlockSpec((None, br, bc), lambda k, i, j, core_ref: (k, core_ref[0] * nb + i, j)),
                  pl.BlockSpec((None, br, bc), lambda k, i, j, core_ref: (k, i, j))],
        out_specs=pl.BlockSpec((None, br, bc), lambda k, i, j, core_ref: (k, i, j)))
    return pl.pallas_call(
        body, name=name, grid_spec=grid_spec,
        out_shape=jax.ShapeDtypeStruct((4, h, c), BF16),
        compiler_params=_params(("parallel", "parallel", "parallel")),
    )(core, g, r1)


def _sum_chips(where, p, r2, name, piece=(0, 1), so_far=None):
    _, h, c = p.shape
    k, n = piece
    br, bc = _blk(h, 512, 8), _blk(c, 2048)
    nb, ncb = h // br, c // bc

    def body(where_ref, p_ref, r_ref, *rest):
        acc = p_ref[...].astype(F32)
        for j in range(3):
            acc = acc + r_ref[j].astype(F32)
        rest[-1][...] = acc

    extra_specs, extra, aliases = ([], [], {}) if so_far is None else ([ANY], [so_far], {3: 0})
    grid_spec = pltpu.PrefetchScalarGridSpec(
        num_scalar_prefetch=1, grid=(nb, ncb),
        in_specs=[pl.BlockSpec((None, br, bc), lambda i, j, where_ref: (where_ref[0], i, j)),
                  pl.BlockSpec((3, br, bc), lambda i, j, where_ref: (0, i, j))] + extra_specs,
        out_specs=pl.BlockSpec((br, bc), lambda i, j, where_ref: (where_ref[1] * nb + i, k * ncb + j)))
    return pl.pallas_call(
        body, name=name, grid_spec=grid_spec,
        out_shape=jax.ShapeDtypeStruct((2 * h, n * c), F32), input_output_aliases=aliases,
        compiler_params=_params(("parallel", "parallel")),
    )(where, p, r2, *extra)


def _adamw_math(w, g, m, v):
    m2 = ADAM_B1 * m + (1.0 - ADAM_B1) * g
    v2 = ADAM_B2 * v + (1.0 - ADAM_B2) * (g * g)
    m_hat = m2 / (1.0 - ADAM_B1 ** ADAM_STEP)
    v_hat = v2 / (1.0 - ADAM_B2 ** ADAM_STEP)
    delta = -ADAM_LR * (m_hat / (jnp.sqrt(v_hat) + ADAM_EPS) + ADAM_WD * w)
    return delta, m2, v2


def _adamw(w, g, m, v, name, after=None):
    r, c = w.shape
    br, bc = _blk(r, 256, 8), _blk(c, 2048)

    def body(w_ref, g_ref, m_ref, v_ref, go_ref, d_ref, m2_ref, v2_ref):
        gv = g_ref[...]
        d, m2, v2 = _adamw_math(w_ref[...], gv, m_ref[...], v_ref[...])
        go_ref[...] = gv
        d_ref[...] = d
        m2_ref[...] = m2
        v2_ref[...] = v2

    spec = pl.BlockSpec((br, bc), lambda i, j: (i, j))
    body, extra_specs, extra = _ordered(body, 4, after)
    return pl.pallas_call(
        body, name=name, grid=(r // br, c // bc),
        in_specs=[spec] * 4 + extra_specs, out_specs=[spec] * 4,
        out_shape=[jax.ShapeDtypeStruct((r, c), F32)] * 4,
        compiler_params=_params(("parallel", "parallel")),
    )(w, g, m, v, *extra)


def _sum_rows8(g, name, after=None):
    n = g.shape[1]

    def body(g_ref, o_ref):
        acc = g_ref[0:1, :]
        for k in range(1, 8):
            acc = acc + g_ref[k:k + 1, :]
        o_ref[...] = acc

    body, extra_specs, extra = _ordered(body, 1, after)
    return pl.pallas_call(
        body, name=name, out_shape=jax.ShapeDtypeStruct((1, n), F32),
        in_specs=[pl.BlockSpec(memory_space=pltpu.VMEM)] + extra_specs,
        out_specs=pl.BlockSpec(memory_space=pltpu.VMEM),
        compiler_params=_params(),
    )(g, *extra)


def _mm(a, b, mode, out_dtype, name, bm=1024, bn=None, after=None, col_blocks=None):
    if mode == "nn":
        (_, m, k), (g, _, n) = a.shape, b.shape
    elif mode == "tn":
        (_, k, m), (g, _, n) = a.shape, b.shape
    else:
        (g, m, k), (_, n, _) = a.shape, b.shape
    if bn is None:
        bn = 1024 if k <= 2048 else 512
    bm, bn = _blk(m, bm), _blk(n, bn)

    if mode == "nt":
        def body(a_ref, b_ref, o_ref, acc_ref):
            part = lax.dot_general(a_ref[...], b_ref[...], (((1,), (1,)), ((), ())),
                                   preferred_element_type=F32)
            if g == 1:
                o_ref[...] = part.astype(out_dtype)
            else:
                gi = pl.program_id(2)

                @pl.when(gi == 0)
                def _():
                    acc_ref[...] = part

                @pl.when(gi > 0)
                def _():
                    acc_ref[...] += part

                @pl.when(gi == g - 1)
                def _():
                    o_ref[...] = acc_ref[...].astype(out_dtype)

        body, extra_specs, extra = _ordered(body, 2, after)
        return pl.pallas_call(
            body, name=name, grid=(m // bm, n // bn, g),
            in_specs=[pl.BlockSpec((None, bm, k), lambda i, j, gi: (gi, i, 0)),
                      pl.BlockSpec((None, bn, k), lambda i, j, gi: (gi, j, 0))] + extra_specs,
            out_specs=pl.BlockSpec((None, bm, bn), lambda i, j, gi: (0, i, j)),
            out_shape=jax.ShapeDtypeStruct((1, m, n), out_dtype),
            scratch_shapes=[pltpu.VMEM((bm, bn), F32)],
            compiler_params=_params(("parallel", "parallel", "arbitrary")),
        )(a, b, *extra)

    contract = (((1,), (0,)), ((), ())) if mode == "nn" else (((0,), (0,)), ((), ()))

    def body(a_ref, b_ref, o_ref):
        o_ref[...] = lax.dot_general(a_ref[...], b_ref[...], contract,
                                     preferred_element_type=F32).astype(out_dtype)

    a_spec = (pl.BlockSpec((None, bm, k), lambda i, gi, j: (0, i, 0)) if mode == "nn"
              else pl.BlockSpec((None, k, bm), lambda i, gi, j: (0, 0, i)))
    first, count = (0, n // bn) if col_blocks is None else col_blocks
    body, extra_specs, extra = _ordered(body, 2, after)
    return pl.pallas_call(
        body, name=name, grid=(m // bm, g, count),
        in_specs=[a_spec, pl.BlockSpec((None, k, bn), lambda i, gi, j: (gi, 0, first + j))] + extra_specs,
        out_specs=pl.BlockSpec((None, bm, bn), lambda i, gi, j: (gi, i, j)),
        out_shape=jax.ShapeDtypeStruct((g, m, count * bn), out_dtype),
        compiler_params=_params(("parallel", "parallel", "parallel")),
    )(a, b, *extra)


def _mm_plane(a, b, plane, name, planes_so_far=None, after=None, bm=1024, bn=1024):
    (m, k), (g, _, n) = a.shape, b.shape
    bm, bn = _blk(m, bm), _blk(n, bn)

    def body(plane_ref, a_ref, b_ref, *rest):
        rest[-1][...] = jnp.dot(a_ref[...], b_ref[...], preferred_element_type=F32).astype(BF16)

    extra_specs, extra, aliases = [], [], {}
    if planes_so_far is not None:
        extra_specs.append(ANY)
        extra.append(planes_so_far)
        aliases = {3: 0}
    if after is not None:
        extra_specs.append(ANY)
        extra.append(after)
    grid_spec = pltpu.PrefetchScalarGridSpec(
        num_scalar_prefetch=1, grid=(m // bm, n // bn),
        in_specs=[pl.BlockSpec((bm, k), lambda i, j, p: (i, 0)),
                  pl.BlockSpec((None, k, bn), lambda i, j, p: (p[0], 0, j))] + extra_specs,
        out_specs=pl.BlockSpec((None, bm, bn), lambda i, j, p: (p[0], i, j)))
    return pl.pallas_call(
        body, name=name, grid_spec=grid_spec,
        out_shape=jax.ShapeDtypeStruct((g, m, n), BF16), input_output_aliases=aliases,
        compiler_params=_params(("parallel", "parallel")),
    )(plane, a, b, *extra)


def _row_specs(br, d):
    return (pl.BlockSpec((br, d), lambda i: (i, 0)), pl.BlockSpec((1, d), lambda i: (0, 0)),
            pl.BlockSpec((8, d), lambda i: (0, 0)))


def _rstd(xv):
    return lax.rsqrt(jnp.mean(xv * xv, axis=-1, keepdims=True) + RMS_EPS)


def _colsum(v):
    return jnp.sum(v, axis=0, keepdims=True)


def _norm_mod_fwd(x, g, scale, shift, name, o=None, gate=None, after=None):
    s, d = x.shape
    br = _blk(s, 256, 8)
    has_res = o is not None
    row, vec, _ = _row_specs(br, d)

    def body(*refs):
        if has_res:
            x_ref, o_ref, gate_ref, g_ref, sc_ref, sh_ref, x1_ref, h_ref = refs
            xv = x_ref[...] + gate_ref[...] * o_ref[...]
            x1_ref[...] = xv
        else:
            x_ref, g_ref, sc_ref, sh_ref, h_ref = refs
            xv = x_ref[...]
        n = xv * _rstd(xv) * g_ref[...]
        h_ref[...] = (n * (1.0 + sc_ref[...]) + sh_ref[...]).astype(BF16)

    ins = [x] + ([o, gate] if has_res else []) + [g, scale, shift]
    in_specs = [row] + ([row, vec] if has_res else []) + [vec] * 3
    out_shape = ([jax.ShapeDtypeStruct((s, d), F32)] if has_res else []) + [jax.ShapeDtypeStruct((s, d), BF16)]
    body, extra_specs, extra = _ordered(body, len(ins), after)
    out = pl.pallas_call(
        body, name=name, grid=(s // br,), in_specs=in_specs + extra_specs, out_specs=[row] * len(out_shape),
        out_shape=out_shape, compiler_params=_params(("parallel",)),
    )(*ins, *extra)
    return out if has_res else out[0]


def _final_loss(x1, o1, gate1, final_g, tgt, name):
    s, d = x1.shape
    br = _blk(s, 256, 8)
    row, vec, acc = _row_specs(br, d)

    def body(x1_ref, o_ref, gate_ref, g_ref, t_ref, dx_ref, do_ref, acc_ref):
        @pl.when(pl.program_id(0) == 0)
        def _():
            acc_ref[...] = jnp.zeros_like(acc_ref)

        gate, o, g = gate_ref[...], o_ref[...], g_ref[...]
        x2 = x1_ref[...] + gate * o
        r = _rstd(x2)
        xh = x2 * r
        err = xh * g - t_ref[...]
        loss = 0.5 * _colsum(jnp.mean(err * err, axis=-1, keepdims=True))
        dout = err * (1.0 / d)
        dxh = dout * g
        dx2 = r * (dxh - xh * jnp.mean(dxh * xh, axis=-1, keepdims=True))
        dx_ref[...] = dx2
        do_ref[...] = (dx2 * gate).astype(BF16)
        acc_ref[0:1, :] += _colsum(dout * xh)
        acc_ref[1:2, :] += _colsum(dx2 * o)
        acc_ref[2:3, :] += jnp.broadcast_to(loss, (1, d))

    return pl.pallas_call(
        body, name=name, grid=(s // br,),
        in_specs=[row, row, vec, vec, row], out_specs=[row, row, acc],
        out_shape=[jax.ShapeDtypeStruct((s, d), F32), jax.ShapeDtypeStruct((s, d), BF16),
                   jax.ShapeDtypeStruct((8, d), F32)],
        compiler_params=_params(("arbitrary",)),
    )(x1, o1, gate1, final_g, tgt)


def _norm_mod_bwd(dh, x, g, scale, dx_next, name, o_prev=None, gate_prev=None, after=None):
    s, d = x.shape
    br = _blk(s, 256, 8)
    has_prev = o_prev is not None
    row, vec, acc = _row_specs(br, d)

    def body(*refs):
        if has_prev:
            dh_ref, x_ref, g_ref, sc_ref, dxn_ref, o_ref, gate_ref, dx_ref, do_ref, acc_ref = refs
        else:
            dh_ref, x_ref, g_ref, sc_ref, dxn_ref, dx_ref, acc_ref = refs

        @pl.when(pl.program_id(0) == 0)
        def _():
            acc_ref[...] = jnp.zeros_like(acc_ref)

        xv, gv, dhv = x_ref[...], g_ref[...], dh_ref[...]
        r = _rstd(xv)
        xh = xv * r
        acc_ref[0:1, :] += _colsum(dhv * (xh * gv))
        acc_ref[1:2, :] += _colsum(dhv)
        dn = dhv * (1.0 + sc_ref[...])
        acc_ref[2:3, :] += _colsum(dn * xh)
        dxh = dn * gv
        dx = dxn_ref[...] + r * (dxh - xh * jnp.mean(dxh * xh, axis=-1, keepdims=True))
        dx_ref[...] = dx
        if has_prev:
            acc_ref[3:4, :] += _colsum(dx * o_ref[...])
            do_ref[...] = (dx * gate_ref[...]).astype(BF16)

    ins = [dh, x, g, scale, dx_next] + ([o_prev, gate_prev] if has_prev else [])
    in_specs = [row, row, vec, vec, row] + ([row, vec] if has_prev else [])
    out_shape = [jax.ShapeDtypeStruct((s, d), F32)]
    out_specs = [row]
    if has_prev:
        out_shape.append(jax.ShapeDtypeStruct((s, d), BF16))
        out_specs.append(row)
    out_shape.append(jax.ShapeDtypeStruct((8, d), F32))
    out_specs.append(acc)
    body, extra_specs, extra = _ordered(body, len(ins), after)
    return pl.pallas_call(
        body, name=name, grid=(s // br,), in_specs=in_specs + extra_specs, out_specs=out_specs,
        out_shape=out_shape, compiler_params=_params(("arbitrary",)),
    )(*ins, *extra)


def _tiles(p):
    s, c = p.shape
    return p.reshape(s // SUBLANES, SUBLANES, c)


def _shift_down(p, k):
    if k == 0:
        return p
    r = pltpu.roll(_tiles(p), k, 1)
    before = jnp.concatenate([jnp.zeros_like(r[:1]), r[:-1]], axis=0)
    rows = lax.broadcasted_iota(jnp.int32, r.shape, 1)
    return jnp.where(rows >= k, r, before).reshape(p.shape)


def _shift_up(p, k):
    if k == 0:
        return p
    r = pltpu.roll(_tiles(p), SUBLANES - k, 1)
    after = jnp.concatenate([r[1:], jnp.zeros_like(r[:1])], axis=0)
    rows = lax.broadcasted_iota(jnp.int32, r.shape, 1)
    return jnp.where(rows < SUBLANES - k, r, after).reshape(p.shape)


def _sigmoid(z):
    return 0.5 * (jnp.tanh(0.5 * z) + 1.0)


def _sc_parts(proj_ref, w_ref):
    b, cg, v, g = (proj_ref[i].astype(F32) for i in range(4))
    p = cg * v
    u = w_ref[2:3, :] * p + w_ref[1:2, :] * _shift_down(p, 1) + w_ref[0:1, :] * _shift_down(p, 2)
    return b, cg, v, g, p, u


def _sc_fwd(proj, conv_w, name):
    _, s, e = proj.shape
    bc = _blk(e, 256)

    def body(proj_ref, w_ref, y_ref):
        b, _, _, g, _, u = _sc_parts(proj_ref, w_ref)
        y_ref[...] = (b * u * (g * _sigmoid(g))).astype(BF16)

    return pl.pallas_call(
        body, name=name, grid=(e // bc,),
        in_specs=[pl.BlockSpec((4, s, bc), lambda j: (0, 0, j)), pl.BlockSpec((3, bc), lambda j: (0, j))],
        out_specs=pl.BlockSpec((s, bc), lambda j: (0, j)),
        out_shape=jax.ShapeDtypeStruct((s, e), BF16),
        compiler_params=_params(("parallel",)),
    )(proj, conv_w)


def _sc_bwd(proj, dy, conv_w, name, after=None):
    _, s, e = proj.shape
    bc = _blk(e, 256)

    def body(proj_ref, dy_ref, w_ref, dp_ref, dw_ref):
        b, cg, v, g, p, u = _sc_parts(proj_ref, w_ref)
        dyv = dy_ref[...].astype(F32)
        sig = _sigmoid(g)
        t = dyv * (g * sig)
        du = t * b
        dp_ref[0] = (t * u).astype(BF16)
        dp_ref[3] = (dyv * b * u * (sig * (1.0 + g * (1.0 - sig)))).astype(BF16)
        dpp = w_ref[2:3, :] * du + w_ref[1:2, :] * _shift_up(du, 1) + w_ref[0:1, :] * _shift_up(du, 2)
        dp_ref[1] = (dpp * v).astype(BF16)
        dp_ref[2] = (dpp * cg).astype(BF16)
        dw_ref[2:3, :] = _colsum(du * p)
        dw_ref[1:2, :] = _colsum(du * _shift_down(p, 1))
        dw_ref[0:1, :] = _colsum(du * _shift_down(p, 2))

    body, extra_specs, extra = _ordered(body, 3, after)
    return pl.pallas_call(
        body, name=name, grid=(e // bc,),
        in_specs=[pl.BlockSpec((4, s, bc), lambda j: (0, 0, j)), pl.BlockSpec((s, bc), lambda j: (0, j)),
                  pl.BlockSpec((3, bc), lambda j: (0, j))] + extra_specs,
        out_specs=[pl.BlockSpec((4, s, bc), lambda j: (0, 0, j)), pl.BlockSpec((3, bc), lambda j: (0, j))],
        out_shape=[jax.ShapeDtypeStruct((4, s, e), BF16), jax.ShapeDtypeStruct((3, e), F32)],
        compiler_params=_params(("parallel",)),
    )(proj, dy, conv_w, *extra)


def _softplus_neg(lam):
    u = jnp.exp(-jnp.abs(lam))
    w = 1.0 + u
    log1p = jnp.where(w == 1.0, u, jnp.log(w) * (u / jnp.where(w == 1.0, 1.0, w - 1.0)))
    return jnp.maximum(-lam, 0.0) + log1p


def _one_minus_exp(z):
    series = -z * (1.0 + z * (0.5 + z * (1.0 / 6.0 + z * (1.0 / 24.0))))
    return jnp.where(z > -0.02, series, 1.0 - jnp.exp(z))


def _scan_in_tiles(a, b, reverse):
    shape = a.shape
    a, b = _tiles(a), _tiles(b)
    rows = lax.broadcasted_iota(jnp.int32, a.shape, 1)
    for step in (1, 2, 4):
        shift = SUBLANES - step if reverse else step
        ok = rows < SUBLANES - step if reverse else rows >= step
        a_s, b_s = pltpu.roll(a, shift, 1), pltpu.roll(b, shift, 1)
        b = jnp.where(ok, a * b_s + b, b)
        a = jnp.where(ok, a * a_s, a)
    return a.reshape(shape), b.reshape(shape)


def _by_rows(fn, arrays, rows=32):
    s = arrays[0].shape[0]
    rows = min(rows, s)
    for t in range(0, s, rows):
        fn(t, *(a[t:t + rows] for a in arrays))


def _scan_carry(a_ref, b_ref, h_ref, reverse):
    s, c = a_ref.shape
    n = s // 8

    def step(i, carry):
        gi = n - 1 - i if reverse else i
        sl = pl.ds(pl.multiple_of(gi * 8, 8), 8)
        h = b_ref[sl, :] + a_ref[sl, :] * carry
        h_ref[sl, :] = h
        return h[0:1, :] if reverse else h[7:8, :]

    lax.fori_loop(0, n, step, jnp.zeros((1, c), F32), unroll=8 if n % 8 == 0 else 1)


def _lru_specs(s, e_half, n_heads):
    hp = e_half // HEAD_DIM
    c = HEAD_DIM
    return dict(
        pair=pl.BlockSpec((2, None, s, c), lambda h: (0, h // hp, 0, h % hp)),
        conv_w=pl.BlockSpec((4, c), lambda h: (0, h)),
        chan=pl.BlockSpec((1, c), lambda h: (0, h)),
        w=pl.BlockSpec((4, None, c // 4, c), lambda h: (0, h, 0, 0)),
        bias=pl.BlockSpec((None, 1, c), lambda h: (h, 0, 0)),
        plane=pl.BlockSpec((s, c), lambda h: (0, h)),
    )


def _lru_gate_inputs(vp, cw_ref, cb_ref, wa_ref, ba_ref, wx_ref, bx_ref):
    c = HEAD_DIM
    taps = [_shift_down(vp, 3 - k) for k in range(4)]
    v = cb_ref[...] + sum(cw_ref[k:k + 1, :] * taps[k] for k in range(4))
    vb = v.astype(BF16)
    wa = wa_ref[...].reshape(c, c)
    wx = wx_ref[...].reshape(c, c)
    zr = jnp.dot(vb, wa, preferred_element_type=F32) + ba_ref[...]
    zi = jnp.dot(vb, wx, preferred_element_type=F32) + bx_ref[...]
    return taps, v, vb, wa, wx, zr, zi


def _lru_fwd(proj, conv_w, conv_b, w_a, b_a, w_x, b_x, lam, name):
    _, _, s, e_half = proj.shape
    n_heads = 2 * e_half // HEAD_DIM
    sp_ = _lru_specs(s, e_half, n_heads)

    def body(pg_ref, cw_ref, cb_ref, wa_ref, ba_ref, wx_ref, bx_ref, lam_ref,
             y_ref, a_ref, hs_ref, sa_ref, sb_ref, sh_ref):
        _, v, _, _, _, zr, zi = _lru_gate_inputs(
            pg_ref[0].astype(F32), cw_ref, cb_ref, wa_ref, ba_ref, wx_ref, bx_ref)
        rate = (-RGLRU_C) * _softplus_neg(lam_ref[...])

        def decay_and_input(t, v_c, zr_c, zi_c):
            la = rate * _sigmoid(zr_c)
            a = jnp.exp(la)
            b = jnp.sqrt(_one_minus_exp(2.0 * la)) * (_sigmoid(zi_c) * v_c)
            a_ref[t:t + a.shape[0]] = a
            sa_ref[t:t + a.shape[0]], sb_ref[t:t + a.shape[0]] = _scan_in_tiles(a, b, reverse=False)

        _by_rows(decay_and_input, [v, zr, zi])
        _scan_carry(sa_ref, sb_ref, sh_ref, reverse=False)

        def gated_output(t, hs_c, g_c):
            g = g_c.astype(F32)
            y_ref[t:t + g.shape[0]] = (hs_c * (g * _sigmoid(g))).astype(BF16)
            hs_ref[t:t + g.shape[0]] = hs_c.astype(BF16)

        _by_rows(gated_output, [sh_ref, pg_ref.at[1]])

    e = 2 * e_half
    return pl.pallas_call(
        body, name=name, grid=(n_heads,),
        in_specs=[sp_["pair"], sp_["conv_w"], sp_["chan"], sp_["w"], sp_["bias"], sp_["w"],
                  sp_["bias"], sp_["chan"]],
        out_specs=[sp_["plane"]] * 3,
        out_shape=[jax.ShapeDtypeStruct((s, e), BF16), jax.ShapeDtypeStruct((s, e), F32),
                   jax.ShapeDtypeStruct((s, e), BF16)],
        scratch_shapes=[pltpu.VMEM((s, HEAD_DIM), F32)] * 3,
        compiler_params=_params(("parallel",)),
    )(proj, conv_w, conv_b, w_a, b_a, w_x, b_x, lam)


def _lru_bwd(proj, dy, saved, conv_w, conv_b, w_a, b_a, w_x, b_x, lam, name):
    _, _, s, e_half = proj.shape
    e = 2 * e_half
    c = HEAD_DIM
    n_heads = e // c
    sp_ = _lru_specs(s, e_half, n_heads)

    def body(pg_ref, dy_ref, a_ref, hs_ref, cw_ref, cb_ref, wa_ref, ba_ref, wx_ref, bx_ref, lam_ref,
             dpg_ref, dwa_ref, dwx_ref, dba_ref, dbx_ref, dlam_ref, dcw_ref, dcb_ref,
             sa_ref, sb_ref, sd_ref, dzr_ref, dzi_ref):
        taps, v, vb, wa, wx, zr, zi = _lru_gate_inputs(
            pg_ref[0].astype(F32), cw_ref, cb_ref, wa_ref, ba_ref, wx_ref, bx_ref)
        lam = lam_ref[...]
        rate = (-RGLRU_C) * _softplus_neg(lam)
        a = a_ref[...]

        def state_gradient_in_tiles(t, a_next, dy_c, g_c, hs_c):
            g, dyv = g_c.astype(F32), dy_c.astype(F32)
            sig = _sigmoid(g)
            rows = slice(t, t + g.shape[0])
            dpg_ref[1, rows] = (dyv * hs_c.astype(F32) * (sig * (1.0 + g * (1.0 - sig)))).astype(BF16)
            sa_ref[rows], sb_ref[rows] = _scan_in_tiles(a_next, dyv * (g * sig), reverse=True)

        _by_rows(state_gradient_in_tiles, [_shift_up(a, 1), dy_ref, pg_ref.at[1], hs_ref])
        _scan_carry(sa_ref, sb_ref, sd_ref, reverse=True)

        sums = []

        def gate_gradients(t, dh, hs_before, a_c, zr_c, zi_c, v_c):
            r, i = _sigmoid(zr_c), _sigmoid(zi_c)
            q = (1.0 - a_c) * (1.0 + a_c)
            inv_nm = lax.rsqrt(q)
            div = dh * (q * inv_nm)
            dla = (dh * hs_before) * a_c - (dh * (i * v_c)) * (a_c * a_c * inv_nm)
            dzr = (dla * rate) * (r * (1.0 - r))
            dzi = (div * v_c) * (i * (1.0 - i))
            rows = slice(t, t + dh.shape[0])
            dzr_ref[rows], dzi_ref[rows] = dzr.astype(BF16), dzi.astype(BF16)
            sa_ref[rows] = div * i
            sums.append((_colsum(dla * r), _colsum(dzr), _colsum(dzi)))

        _by_rows(gate_gradients, [sd_ref, _shift_down(hs_ref[...].astype(F32), 1), a_ref, zr, zi, v])
        dlam_ref[...] = sum(p[0] for p in sums) * ((-RGLRU_C) * (-_sigmoid(-lam)))
        dba_ref[...] = sum(p[1] for p in sums)
        dbx_ref[...] = sum(p[2] for p in sums)
        dzr_b, dzi_b = dzr_ref[...], dzi_ref[...]
        tn = (((0,), (0,)), ((), ()))
        nt = (((1,), (1,)), ((), ()))
        dwa_ref[...] = lax.dot_general(vb, dzr_b, tn, preferred_element_type=F32)
        dwx_ref[...] = lax.dot_general(vb, dzi_b, tn, preferred_element_type=F32)
        dv = (sa_ref[...] + lax.dot_general(dzr_b, wa, nt, preferred_element_type=F32)
              + lax.dot_general(dzi_b, wx, nt, preferred_element_type=F32))
        dcb_ref[...] = _colsum(dv)
        dvp = jnp.zeros_like(dv)
        for k in range(4):
            dvp = dvp + cw_ref[k:k + 1, :] * _shift_up(dv, 3 - k)
            dcw_ref[k:k + 1, :] = _colsum(dv * taps[k])
        dpg_ref[0] = dvp.astype(BF16)

    head_mat = pl.BlockSpec((None, c, c), lambda h: (h, 0, 0))
    outs = pl.pallas_call(
        body, name=name, grid=(n_heads,),
        in_specs=[sp_["pair"]] + [sp_["plane"]] * 3 + [sp_["conv_w"], sp_["chan"], sp_["w"], sp_["bias"],
                                                        sp_["w"], sp_["bias"], sp_["chan"]],
        out_specs=[sp_["pair"], head_mat, head_mat, sp_["bias"], sp_["bias"],
                   sp_["chan"], sp_["conv_w"], sp_["chan"]],
        out_shape=[jax.ShapeDtypeStruct((2, 2, s, e_half), BF16),
                   jax.ShapeDtypeStruct((n_heads, c, c), F32), jax.ShapeDtypeStruct((n_heads, c, c), F32),
                   jax.ShapeDtypeStruct((n_heads, 1, c), F32), jax.ShapeDtypeStruct((n_heads, 1, c), F32),
                   jax.ShapeDtypeStruct((1, e), F32), jax.ShapeDtypeStruct((4, e), F32),
                   jax.ShapeDtypeStruct((1, e), F32)],
        scratch_shapes=[pltpu.VMEM((s, c), F32)] * 3 + [pltpu.VMEM((s, c), BF16)] * 2,
        compiler_params=_params(("parallel",)),
    )(proj, dy, *saved, conv_w, conv_b, w_a, b_a, w_x, b_x, lam)
    return tuple(outs)


def _ada_fwd(c_all, ada_w, name):
    n_l, d, f = ada_w.shape
    bf = _blk(f, 512)

    def body(c_ref, w_ref, o_ref):
        cv = c_ref[...]
        sc = (cv * _sigmoid(cv)).astype(BF16)
        o_ref[...] = jnp.dot(sc, w_ref[...].astype(BF16), preferred_element_type=F32)

    return pl.pallas_call(
        body, name=name, grid=(n_l, f // bf),
        in_specs=[pl.BlockSpec((8, d), lambda l, j: (0, 0)), pl.BlockSpec((None, d, bf), lambda l, j: (l, 0, j))],
        out_specs=pl.BlockSpec((None, 8, bf), lambda l, j: (l, 0, j)),
        out_shape=jax.ShapeDtypeStruct((n_l, 8, f), F32),
        compiler_params=_params(("parallel", "parallel")),
    )(c_all, ada_w)


def _ada_bwd_adamw(c_t, dmod, w, m, v, name, after=None):
    n_l, d, f = w.shape
    bf = _blk(f, 256)

    def body(c_ref, dm_ref, w_ref, m_ref, v_ref, g_ref, d_ref, m2_ref, v2_ref):
        cv = c_ref[...]
        sc = cv * _sigmoid(cv)
        dm = dm_ref[...]
        g = sc[:, 0:1] * dm[0:1, :]
        for b in range(1, 8):
            g = g + sc[:, b:b + 1] * dm[b:b + 1, :]
        g_ref[...] = g
        dl, m2, v2 = _adamw_math(w_ref[...], g, m_ref[...], v_ref[...])
        d_ref[...] = dl
        m2_ref[...] = m2
        v2_ref[...] = v2

    big = pl.BlockSpec((None, d, bf), lambda l, j: (l, 0, j))
    body, extra_specs, extra = _ordered(body, 5, after)
    return pl.pallas_call(
        body, name=name, grid=(n_l, f // bf),
        in_specs=[pl.BlockSpec((d, 8), lambda l, j: (0, 0)), pl.BlockSpec((None, 8, bf), lambda l, j: (l, 0, j)),
                  big, big, big] + extra_specs,
        out_specs=[big] * 4, out_shape=[jax.ShapeDtypeStruct((n_l, d, f), F32)] * 4,
        compiler_params=_params(("parallel", "parallel")),
    )(c_t, dmod, w, m, v, *extra)


def _pack(parts):
    padded, offs, n = [], [], 0
    for p in parts:
        p = p.reshape(-1)
        size = -(-p.shape[0] // PACK) * PACK
        offs.append(n)
        n += size
        padded.append(jnp.pad(p, (0, size - p.shape[0])) if size != p.shape[0] else p)
    return jnp.concatenate(padded), offs, n


def kernel(x, c, norm_g, ada_w, ada_b, sc_w_in, sc_conv_w, sc_w_out, lru_w_in, lru_conv_w, lru_conv_b, lru_w_a, lru_b_a, lru_w_x, lru_b_x, lru_lambda, lru_w_out, final_g, loss_target, m_norm_g, m_ada_w, m_ada_b, m_sc_w_in, m_sc_conv_w, m_sc_w_out, m_lru_w_in, m_lru_conv_w, m_lru_conv_b, m_lru_w_a, m_lru_b_a, m_lru_w_x, m_lru_b_x, m_lru_lambda, m_lru_w_out, m_final_g, v_norm_g, v_ada_w, v_ada_b, v_sc_w_in, v_sc_conv_w, v_sc_w_out, v_lru_w_in, v_lru_conv_w, v_lru_conv_b, v_lru_w_a, v_lru_b_a, v_lru_w_x, v_lru_b_x, v_lru_lambda, v_lru_w_out, v_final_g):
    xi, yi, ci = _pos()
    chip = 2 * xi + yi
    batch = 4 * xi + 2 * yi + ci
    core_op = jnp.reshape(ci, (1,)).astype(jnp.int32)
    chip_op = jnp.reshape(chip, (1,)).astype(jnp.int32)
    where_op = jnp.stack([chip, ci]).astype(jnp.int32)

    x2d, tgt = x[0], loss_target[0]
    s, d = x2d.shape
    es = sc_conv_w.shape[2]
    e = 4 * es
    n_heads = lru_w_a.shape[1]
    hj = lru_b_a.shape[2]
    f = ada_w.shape[2]
    row = lambda t: t.reshape(1, -1)

    small_parts = [c, sc_conv_w, lru_conv_w, lru_conv_b, lru_b_a, lru_b_x, lru_lambda]
    small, offs, n_small = _pack(small_parts)
    got = _allgather8([small.reshape(8, n_small // 8)], "ag_small")[0].reshape(8, n_small)
    c_all = got[:, :d]
    per_chip = got[0::2]

    def chip_part(k, shape):
        size = 1
        for dim in shape:
            size *= dim
        return per_chip[:, offs[k]:offs[k] + size].reshape((4,) + shape)

    conv_w0 = jnp.transpose(chip_part(1, (3, es)), (1, 0, 2)).reshape(3, e)
    conv_w1 = jnp.transpose(chip_part(2, (4, es)), (1, 0, 2)).reshape(4, e)
    conv_b1 = chip_part(3, (es,)).reshape(1, e)
    b_a = jnp.transpose(chip_part(4, (n_heads, hj)), (1, 0, 2)).reshape(n_heads, 1, 4 * hj)
    b_x = jnp.transpose(chip_part(5, (n_heads, hj)), (1, 0, 2)).reshape(n_heads, 1, 4 * hj)
    lam = chip_part(6, (es,)).reshape(1, e)

    mod_nb = _ada_fwd(c_all, ada_w, "ada_fwd")
    mods = _allgather8([mod_nb.reshape(16, f)], "ag_mod")[0].reshape(8, 2, 8, f)[0::2]
    mine = lax.dynamic_index_in_dim(mods, batch, axis=2, keepdims=False)
    mod = jnp.transpose(mine, (1, 0, 2)).reshape(2, 4 * f) + ada_b
    shift = [row(mod[l, :d]) for l in range(2)]
    scale = [row(mod[l, d:2 * d]) for l in range(2)]
    gate = [row(mod[l, 2 * d:]) for l in range(2)]
    ng = [row(norm_g[l]) for l in range(2)]

    shards = [sc_w_in[0], sc_w_out[0], lru_w_in[0], lru_w_a[0].reshape(n_heads * hj, HEAD_DIM),
              lru_w_x[0].reshape(n_heads * hj, HEAD_DIM), lru_w_out[0]]
    names = ["sc_w_in", "sc_w_out", "lru_w_in", "lru_w_a", "lru_w_x", "lru_w_out"]
    slots = [_cast_into_slot(chip_op, w, "cast_" + nm) for w, nm in zip(shards, names)]
    send_a, recv_a, buf, started = _first_start(slots[0], mod, "ag_first_start")
    h0 = _norm_mod_fwd(x2d, ng[0], scale[0], shift[0], "norm0", after=started)
    planes = [jnp.reshape(2 * px + py, (1,)).astype(jnp.int32) for px, py in _other_chips(xi, yi)]
    proj0 = _mm_plane(h0, buf.reshape(4, d, e), chip_op, "sc_in_own")
    small_names = ["norm_g", "ada_b", "final_g", "sc_conv_w", "lru_conv_w", "lru_conv_b", "lru_b_a",
                   "lru_b_x", "lru_lambda"]
    small_w = [norm_g, ada_b, final_g, sc_conv_w, lru_conv_w, lru_conv_b, lru_b_a, lru_b_x, lru_lambda]
    small_m = [m_norm_g, m_ada_b, m_final_g, m_sc_conv_w, m_lru_conv_w, m_lru_conv_b, m_lru_b_a,
               m_lru_b_x, m_lru_lambda]
    small_v = [v_norm_g, v_ada_b, v_final_g, v_sc_conv_w, v_lru_conv_w, v_lru_conv_b, v_lru_b_a,
               v_lru_b_x, v_lru_lambda]
    pw, soffs, n_s = _pack(small_w)
    small_state = [t.reshape(n_s // PACK, PACK) for t in (pw, _pack(small_m)[0], _pack(small_v)[0])]
    relays, passed, relayed = [], [], [proj0] + slots[1:] + small_state
    for j in range(2):
        send_b, recv_b, send_c, recv_c, buf, token = _first_relay(send_a, recv_a, buf, j, relayed,
                                                                  "ag_first_relay_%d" % j)
        relays.append((send_c, recv_c))
        passed.append((send_b, recv_b))
        relayed = [token]
    rest_flight, relayed = _gather_start([[slots[1]], [slots[2]], slots[3:5], [slots[5]]], token, "ag_start")
    for j in range(2):
        buf = _first_relay_done(*passed[j], buf, j, relayed, "ag_first_relay_done_%d" % j)
        proj0 = _mm_plane(h0, buf.reshape(4, d, e), planes[j], "sc_in_%d" % j, planes_so_far=proj0)
        relayed = proj0
    send_b, recv_b, buf, relayed = _first_diagonal(relays, buf, proj0, "ag_first_diagonal")
    buf = _first_diagonal_done(send_b, recv_b, relays, buf, relayed, "ag_first_diagonal_done")
    proj0 = _mm_plane(h0, buf.reshape(4, d, e), planes[2], "sc_in_2", planes_so_far=proj0)
    w_in0 = buf.reshape(4, d, e)
    in_flight = [None] + rest_flight

    def arrived(g, after, tag):
        send1, recv1, bufs = in_flight[g]
        send2, recv2, bufs, passed = _gather_forward(send1, recv1, bufs, after, "ag_forward_" + tag)
        return _gather_finish(send2, recv2, bufs, passed, "ag_finish_" + tag)

    y0 = _sc_fwd(proj0, conv_w0, "sc_mix")
    w_out0 = arrived(1, y0, "sc_w_out")[0].reshape(1, e, d)
    o0 = _mm(y0[None], w_out0, "nn", F32, "sc_out")[0]
    x1, h1 = _norm_mod_fwd(x2d, ng[1], scale[1], shift[1], "norm1", o=o0, gate=gate[0])
    send1, recv1, bufs = in_flight[2]
    proj1 = _mm_plane(h1, bufs[0].reshape(4, d, e // 2), chip_op, "lru_in_own")
    for j in range(3):
        send2, recv2, bufs, passed_on = _gather_forward(send1, recv1, bufs, proj1, "ag_forward_lru_w_in_%d" % j,
                                                        sources=(j,))
        bufs = _gather_finish(send2, recv2, bufs, passed_on, "ag_finish_lru_w_in_%d" % j, sources=(j,))
        proj1 = _mm_plane(h1, bufs[0].reshape(4, d, e // 2), planes[j], "lru_in_%d" % j, planes_so_far=proj1)
    w_in1 = bufs[0].reshape(4, d, e // 2)
    gate_ws = arrived(3, proj1, "lru_gates")
    w_a = gate_ws[0].reshape(4, n_heads, hj, HEAD_DIM)
    w_x = gate_ws[1].reshape(4, n_heads, hj, HEAD_DIM)
    pairs1 = proj1.reshape(2, 2, s, e // 2)
    y1, *lru_saved = _lru_fwd(pairs1, conv_w1, conv_b1, w_a, b_a, w_x, b_x, lam, "lru_mix")
    w_out1 = arrived(4, y1, "lru_w_out")[0].reshape(1, e, d)
    o1 = _mm(y1[None], w_out1, "nn", F32, "lru_out")[0]
    dx2, do1, acc_f = _final_loss(x1, o1, gate[1], row(final_g), tgt, "final_loss")

    def reduce_stage1(tag, grads):
        lands = [lax.empty((4, g.shape[1] // 2, g.shape[2]), F32) for g in grads]
        return _exchange_start("rs_sibling_start_" + tag, grads, lands, len(grads),
                               _plan_other_half_to_sibling, None)

    def reduce_stage2(tag, stage1, nms, after):
        send, recv, grads, lands, _ = stage1
        grads, lands = _exchange_wait("rs_sibling_wait_" + tag, send, recv, grads, lands,
                                      _plan_other_half_to_sibling, after)
        parts = [_sum_own_and_sibling(core_op, g, r1, "rs_sum1_" + nm) for g, r1, nm in zip(grads, lands, nms)]
        lands = [lax.empty((3,) + p.shape[1:], BF16) for p in parts]
        return _exchange_start("rs_chips_start_" + tag, parts, lands, 3 * len(parts),
                               _plan_partials_to_chips, None)

    def reduce_stage3(tag, stage2, nms, after):
        send, recv, parts, lands, _ = stage2
        parts, lands = _exchange_wait("rs_chips_wait_" + tag, send, recv, parts, lands,
                                      _plan_partials_to_chips, after)
        halves = [_sum_chips(where_op, p, r2, "rs_sum2_" + nm) for p, r2, nm in zip(parts, lands, nms)]
        return _exchange_start("rs_share_start_" + tag, [], halves, len(halves), _plan_share_half, None)

    def reduce_done(tag, stage3, after):
        send, recv, _, fulls, _ = stage3
        return _exchange_wait("rs_share_wait_" + tag, send, recv, [], fulls, _plan_share_half, after)[1]

    def head_major_to_chip_major(t):
        return jnp.transpose(t.reshape(n_heads, 4, hj, HEAD_DIM), (1, 0, 2, 3)).reshape(4, n_heads * hj, HEAD_DIM)

    big_state = dict(zip(names, zip(shards, [m_sc_w_in, m_sc_w_out, m_lru_w_in, m_lru_w_a, m_lru_w_x, m_lru_w_out],
                                    [v_sc_w_in, v_sc_w_out, v_lru_w_in, v_lru_w_a, v_lru_w_x, v_lru_w_out],
                                    [sc_w_in, sc_w_out, lru_w_in, lru_w_a, lru_w_x, lru_w_out])))
    big = {}

    def update(nms, fulls, after):
        for nm, g2 in zip(nms, fulls):
            w2, m4, v4, w4 = big_state[nm]
            outs = _adamw(w2, g2, m4.reshape(w2.shape), v4.reshape(w2.shape), "adamw_" + nm, after=after)
            big[nm] = tuple(t.reshape(w4.shape) for t in outs)
            after = outs[1]
        return after

    g_w_out1 = _mm(y1[None], do1[None], "tn", F32, "lru_out_dw", bm=512, bn=2048)
    dy1 = _mm(do1[None], w_out1, "nt", BF16, "lru_out_dx")[0]
    dpairs1, g_wa, g_wx, g_ba, g_bx, g_lam, g_cw1, g_cb1 = _lru_bwd(
        pairs1, dy1, lru_saved, conv_w1, conv_b1, w_a, b_a, w_x, b_x, lam, "lru_mix_bwd")
    dproj1 = dpairs1.reshape(4, s, e // 2)
    g_w_in1 = _mm(h1[None], dproj1, "tn", F32, "lru_in_dw", bm=1024, bn=2048)
    lru_names = ["lru_w_out", "lru_w_a", "lru_w_x", "lru_w_in"]
    lru_rs = reduce_stage1("lru", [g_w_out1.reshape(4, es, d), head_major_to_chip_major(g_wa),
                                   head_major_to_chip_major(g_wx), g_w_in1])
    dh1 = _mm(dproj1, w_in1, "nt", F32, "lru_in_dx", after=lru_rs[4])[0]
    lru_rs = reduce_stage2("lru", lru_rs, lru_names, dh1)
    dx1, do0, acc1 = _norm_mod_bwd(dh1, x1, ng[1], scale[1], dx2, "norm1_bwd", o_prev=o0, gate_prev=gate[0],
                                   after=lru_rs[4])

    g_w_out0 = _mm(y0[None], do0[None], "tn", F32, "sc_out_dw", bm=512, bn=2048)
    out_rs = reduce_stage1("sc_out", [g_w_out0.reshape(4, es, d)])
    dy0 = _mm(do0[None], w_out0, "nt", BF16, "sc_out_dx", after=out_rs[4])[0]
    out_rs = reduce_stage2("sc_out", out_rs, ["sc_w_out"], dy0)
    dproj0, g_cw0 = _sc_bwd(proj0, dy0, conv_w0, "sc_mix_bwd", after=out_rs[4])
    stage1, stage2, after = [], [], None
    for k in range(IN_PIECES):
        g_piece = _mm(h0[None], dproj0, "tn", F32, "sc_in_dw_%d" % k, bm=1024, bn=e // IN_PIECES,
                      col_blocks=(k, 1), after=after)
        stage1.append(reduce_stage1("sc_in_%d" % k, [g_piece]))
        after = stage1[k][4]
        if k:
            stage2.append(reduce_stage2("sc_in_%d" % (k - 1), stage1[k - 1], ["sc_w_in_%d" % (k - 1)], after))
            after = stage2[k - 1][4]
    lru_rs = reduce_stage3("lru", lru_rs, lru_names, after)
    k = IN_PIECES - 1
    stage2.append(reduce_stage2("sc_in_%d" % k, stage1[k], ["sc_w_in_%d" % k], lru_rs[4]))
    dh0 = _mm(dproj0, w_in0, "nt", F32, "sc_in_dx", bn=1024, after=stage2[k][4])[0]
    grad_x, acc0 = _norm_mod_bwd(dh0, x2d, ng[0], scale[0], dx1, "norm0_bwd")
    out_rs = reduce_stage3("sc_out", out_rs, ["sc_w_out"], acc0)
    last = update(lru_names, reduce_done("lru", lru_rs, out_rs[4]), None)
    last = update(["sc_w_out"], reduce_done("sc_out", out_rs, last), None)
    g_half = None
    for k, (send, recv, parts, lands, _) in enumerate(stage2):
        parts, lands = _exchange_wait("rs_chips_wait_sc_in_%d" % k, send, recv, parts, lands,
                                      _plan_partials_to_chips, last)
        g_half = _sum_chips(where_op, parts[0], lands[0], "rs_sum2_sc_w_in_%d" % k, piece=(k, IN_PIECES),
                            so_far=g_half)
    in_rs = _exchange_start("rs_share_start_sc_in", [], [g_half], 1, _plan_share_half, None)

    dmod = jnp.stack([jnp.concatenate([acc0[1], acc0[0], acc1[3]]),
                      jnp.concatenate([acc1[1], acc1[0], acc_f[1]])])
    part_list = [jnp.stack([acc0[2], acc1[2]]), acc_f[0], acc_f[2, :1], g_cw0, g_cw1, g_cb1, g_ba, g_bx,
                 g_lam, dmod]
    partials, poffs, n_part = _pack(part_list)
    every = _allgather8([partials.reshape(8, n_part // 8)], "ag_partials", after=in_rs[4])[0].reshape(8, n_part)
    total = _sum_rows8(every, "sum_partials")[0]

    def tot(k, shape):
        size = 1
        for dim in shape:
            size *= dim
        return total[poffs[k]:poffs[k] + size].reshape(shape)

    def my_cols(t, width):
        return lax.dynamic_slice_in_dim(t, chip * width, width, axis=t.ndim - 1)

    loss = tot(2, (1,))[0]
    g_norm_g, g_final_g, g_ada_b = tot(0, (2, d)), tot(1, (d,)), tot(9, (2, 3 * d))
    g_sc_conv_w = my_cols(tot(3, (3, e)), es)[None]
    g_lru_conv_w = my_cols(tot(4, (4, e)), es)[None]
    g_lru_conv_b = my_cols(tot(5, (1, e)), es)
    g_lru_b_a = my_cols(tot(6, (n_heads, 4 * hj)), hj)[None]
    g_lru_b_x = my_cols(tot(7, (n_heads, 4 * hj)), hj)[None]
    g_lru_lambda = my_cols(tot(8, (1, e)), es)

    dmod_all = every[:, poffs[9]:poffs[9] + 6 * d].reshape(8, 2, 3 * d)
    dmod_mine = jnp.transpose(my_cols(dmod_all, f), (1, 0, 2))
    ada = _ada_bwd_adamw(jnp.transpose(c_all), dmod_mine, ada_w, m_ada_w, v_ada_w, "ada_bwd_adamw", after=total)

    small_g = [g_norm_g, g_ada_b, g_final_g, g_sc_conv_w, g_lru_conv_w, g_lru_conv_b, g_lru_b_a,
               g_lru_b_x, g_lru_lambda]
    pg = _pack(small_g)[0].reshape(small_state[0].shape)
    _, pd, pm2, pv2 = _adamw(small_state[0], pg, small_state[1], small_state[2], "adamw_small", after=ada[0])
    update(["sc_w_in"], reduce_done("sc_in", in_rs, pd), None)
    small = {}
    for k, (nm, w_) in enumerate(zip(small_names, small_w)):
        take = lambda t: t.reshape(-1)[soffs[k]:soffs[k] + w_.size].reshape(w_.shape)
        small[nm] = (small_g[k].reshape(w_.shape), take(pd), take(pm2), take(pv2))

    results = dict(small)
    results.update(big)
    results["ada_w"] = tuple(ada)
    order = ["norm_g", "ada_w", "ada_b", "sc_w_in", "sc_conv_w", "sc_w_out", "lru_w_in", "lru_conv_w",
             "lru_conv_b", "lru_w_a", "lru_b_a", "lru_w_x", "lru_b_x", "lru_lambda", "lru_w_out", "final_g"]
    out = [loss, grad_x[None]]
    for kind in range(4):
        out += [results[nm][kind] for nm in order]
    return tuple(out)
```

```python
import jax
import jax.numpy as jnp
from jax import lax
from jax.experimental import pallas as pl
from jax.experimental.pallas import tpu as pltpu

F32 = jnp.float32
BF16 = jnp.bfloat16
MESH = pl.DeviceIdType.MESH
ANY = pl.BlockSpec(memory_space=pl.ANY)

RMS_EPS = 1e-6
RGLRU_C = 8.0
HEAD_DIM = 256
ADAM_LR = 0.001
ADAM_B1 = 0.9
ADAM_B2 = 0.999
ADAM_EPS = 1e-08
ADAM_WD = 0.01
ADAM_STEP = 10
V7X_VMEM_LIMIT = 56 * 1024 * 1024
IN_PIECES = 2
LANES = 128
SUBLANES = 8
PACK = SUBLANES * LANES


def _blk(dim, pref, unit=LANES):
    if dim <= pref:
        return dim
    b = (pref // unit) * unit
    while b > unit and dim % b:
        b -= unit
    assert dim % b == 0, (dim, pref, unit)
    return b


def _params(sem=None):
    return pltpu.CompilerParams(dimension_semantics=sem, vmem_limit_bytes=V7X_VMEM_LIMIT)


def _ordered(body, n_in, after):
    if after is None:
        return body, [], []

    def ordered_body(*refs):
        return body(*refs[:n_in], *refs[n_in + 1:])

    return ordered_body, [ANY], [after]


def _pos():
    return lax.axis_index("x"), lax.axis_index("y"), lax.axis_index("c")


def _other_chips(x, y):
    return [(1 - x, y), (x, 1 - y), (1 - x, 1 - y)]


def _allgather8(arrs, name, after=None):
    n_t = len(arrs)
    ms = [a.shape[0] for a in arrs]

    def gather(*refs):
        ins, outs = refs[:n_t], refs[n_t:2 * n_t]
        send_sems, recv_sems, local_sems = refs[2 * n_t:]
        x, y, c = _pos()
        me, sibling = (x, y, c), (x, y, 1 - c)
        chips = _other_chips(x, y)

        def rows(t, px, py, pc):
            return outs[t].at[pl.ds((4 * px + 2 * py + pc) * ms[t], ms[t])]

        def copy(t, k, block, to, src=None):
            return pltpu.make_async_remote_copy(
                src_ref=rows(t, *block) if src is None else src, dst_ref=rows(t, *block),
                send_sem=send_sems.at[7 * t + k], recv_sem=recv_sems.at[7 * t + k],
                device_id=to, device_id_type=MESH)

        mine, first, passed = [], [], []
        for t in range(n_t):
            src = ins[t]
            cp = pltpu.make_async_copy(src, rows(t, *me), local_sems.at[t])
            cp.start()
            mine.append(cp)
            sends = [copy(t, 0, me, sibling, src=src)]
            sends += [copy(t, 1 + j, me, (*chip, c), src=src) for j, chip in enumerate(chips)]
            for cp in sends:
                cp.start()
            first += sends
        for t in range(n_t):
            for j, chip in enumerate(chips):
                copy(t, 1 + j, (*chip, c), me).wait_recv()
                cp = copy(t, 4 + j, (*chip, c), sibling)
                cp.start()
                passed.append(cp)
        for t in range(n_t):
            copy(t, 0, sibling, me).wait_recv()
            for j, chip in enumerate(chips):
                copy(t, 4 + j, (*chip, 1 - c), me).wait_recv()
        for cp in first + passed:
            cp.wait_send()
        for cp in mine:
            cp.wait()

    body, extra_specs, extra = _ordered(gather, n_t, after)
    return pl.pallas_call(
        body, name=name,
        out_shape=[jax.ShapeDtypeStruct((8 * m, a.shape[1]), a.dtype) for m, a in zip(ms, arrs)],
        in_specs=[ANY] * n_t + extra_specs, out_specs=[ANY] * n_t,
        scratch_shapes=[pltpu.SemaphoreType.DMA((7 * n_t,)), pltpu.SemaphoreType.DMA((7 * n_t,)),
                        pltpu.SemaphoreType.DMA((n_t,))],
    )(*arrs, *extra)


HBM = pl.BlockSpec(memory_space=pltpu.HBM)
SEM = pl.BlockSpec(memory_space=pltpu.SEMAPHORE)
TOKEN = pl.BlockSpec(memory_space=pltpu.VMEM)
IN_FLIGHT = pltpu.CompilerParams(has_side_effects=pltpu.SideEffectType.DATAFLOW_SIDE_EFFECTING)


def _in_hbm(arrs):
    return [pltpu.with_memory_space_constraint(a, pltpu.HBM) for a in arrs]


def _shard_rows(ref, h, px, py, pc):
    return ref.at[pl.ds((4 * px + 2 * py + pc) * h, h)]


def _gather_start(groups, after, name):
    bufs = [b for grp in groups for b in grp]
    n_t, n_g = len(bufs), len(groups)

    def body(*refs):
        sems, thru = refs[n_t + 1:n_t + 1 + 2 * n_g], refs[n_t + 1 + 2 * n_g:2 * n_t + 1 + 2 * n_g]
        token = refs[-1]
        x, y, c = _pos()
        t = 0
        for g, grp in enumerate(groups):
            for i in range(len(grp)):
                h = bufs[t].shape[0] // 8
                rows = _shard_rows(thru[t], h, x, y, c)
                for j, chip in enumerate(_other_chips(x, y)):
                    pltpu.make_async_remote_copy(
                        src_ref=rows, dst_ref=rows, send_sem=sems[2 * g].at[3 * i + j],
                        recv_sem=sems[2 * g + 1].at[3 * i + j], device_id=(*chip, c),
                        device_id_type=MESH).start()
                t += 1
        token[...] = jnp.zeros_like(token)

    sem_shapes = []
    for grp in groups:
        sem_shapes += [pltpu.SemaphoreType.DMA((3 * len(grp),))] * 2
    out = pl.pallas_call(
        body, name=name,
        out_shape=sem_shapes + [pltpu.HBM(b.shape, b.dtype) for b in bufs] + [jax.ShapeDtypeStruct((8, LANES), F32)],
        in_specs=[HBM] * n_t + [ANY], out_specs=[SEM] * (2 * n_g) + [HBM] * n_t + [TOKEN],
        input_output_aliases={t: 2 * n_g + t for t in range(n_t)},
        compiler_params=IN_FLIGHT,
    )(*_in_hbm(bufs), after)
    sems, thru, token = out[:2 * n_g], out[2 * n_g:2 * n_g + n_t], out[-1]
    per_group, t = [], 0
    for g, grp in enumerate(groups):
        per_group.append((sems[2 * g], sems[2 * g + 1], thru[t:t + len(grp)]))
        t += len(grp)
    return per_group, token


def _gather_forward(send_sems, recv_sems, bufs, after, name, sources=(0, 1, 2)):
    n_t = len(bufs)

    def body(*refs):
        ins = refs[:n_t]
        send1, recv1 = refs[n_t], refs[n_t + 1]
        send2, recv2 = refs[n_t + 3], refs[n_t + 4]
        token = refs[-1]
        x, y, c = _pos()
        chips = _other_chips(x, y)
        for t in range(n_t):
            h = bufs[t].shape[0] // 8
            mine = _shard_rows(ins[t], h, x, y, c)
            for j in sources:
                chip = chips[j]
                landed = _shard_rows(ins[t], h, *chip, c)
                pltpu.make_async_remote_copy(
                    src_ref=mine, dst_ref=landed, send_sem=send1.at[3 * t + j], recv_sem=recv1.at[3 * t + j],
                    device_id=(*chip, c), device_id_type=MESH).wait_recv()
                pltpu.make_async_remote_copy(
                    src_ref=landed, dst_ref=landed, send_sem=send2.at[3 * t + j], recv_sem=recv2.at[3 * t + j],
                    device_id=(x, y, 1 - c), device_id_type=MESH).start()
        for t in range(n_t):
            h = bufs[t].shape[0] // 8
            mine = _shard_rows(ins[t], h, x, y, c)
            for j in sources:
                pltpu.make_async_remote_copy(
                    src_ref=mine, dst_ref=mine, send_sem=send1.at[3 * t + j], recv_sem=recv1.at[3 * t + j],
                    device_id=(*chips[j], c), device_id_type=MESH).wait_send()
        token[...] = jnp.zeros_like(token)

    out = pl.pallas_call(
        body, name=name,
        out_shape=[pltpu.SemaphoreType.DMA((3 * n_t,))] * 2 + [pltpu.HBM(b.shape, b.dtype) for b in bufs]
        + [jax.ShapeDtypeStruct((8, LANES), F32)],
        in_specs=[HBM] * n_t + [SEM, SEM, ANY], out_specs=[SEM, SEM] + [HBM] * n_t + [TOKEN],
        input_output_aliases={t: 2 + t for t in range(n_t)},
        compiler_params=IN_FLIGHT,
    )(*bufs, send_sems, recv_sems, after)
    return out[0], out[1], out[2:2 + n_t], out[-1]


def _gather_finish(send_sems, recv_sems, bufs, after, name, sources=(0, 1, 2)):
    n_t = len(bufs)

    def body(*refs):
        ins = refs[:n_t]
        send2, recv2 = refs[n_t], refs[n_t + 1]
        x, y, c = _pos()
        chips = _other_chips(x, y)
        for t in range(n_t):
            h = bufs[t].shape[0] // 8
            for j in sources:
                chip = chips[j]
                sent = _shard_rows(ins[t], h, *chip, c)
                got = _shard_rows(ins[t], h, *chip, 1 - c)
                cp = pltpu.make_async_remote_copy(
                    src_ref=sent, dst_ref=got, send_sem=send2.at[3 * t + j], recv_sem=recv2.at[3 * t + j],
                    device_id=(x, y, 1 - c), device_id_type=MESH)
                cp.wait_send()
                cp.wait_recv()

    return pl.pallas_call(
        body, name=name, out_shape=[pltpu.HBM(b.shape, b.dtype) for b in bufs],
        in_specs=[HBM] * n_t + [SEM, SEM, ANY], out_specs=[HBM] * n_t,
        input_output_aliases={t: t for t in range(n_t)},
        compiler_params=IN_FLIGHT,
    )(*bufs, send_sems, recv_sems, after)


def _part_rows(ref, h, px, py, pc, which):
    return ref.at[pl.ds((4 * px + 2 * py + pc) * h + which * (h // 2), h // 2)]


def _remote(src, dst, send_sem, recv_sem, device):
    return pltpu.make_async_remote_copy(src_ref=src, dst_ref=dst, send_sem=send_sem, recv_sem=recv_sem,
                                        device_id=device, device_id_type=MESH)


def _first_start(buf, after, name):
    h = buf.shape[0] // 8

    def body(buf_in, after_ref, send, recv, thru, token):
        x, y, c = _pos()
        rows = _shard_rows(thru, h, x, y, c)
        for j, chip in enumerate(_other_chips(x, y)[:2]):
            _remote(rows, rows, send.at[j], recv.at[j], (*chip, c)).start()
        token[...] = jnp.zeros_like(token)

    return pl.pallas_call(
        body, name=name,
        out_shape=[pltpu.SemaphoreType.DMA((2,))] * 2 + [pltpu.HBM(buf.shape, buf.dtype),
                                                         jax.ShapeDtypeStruct((8, LANES), F32)],
        in_specs=[HBM, ANY], out_specs=[SEM, SEM, HBM, TOKEN], input_output_aliases={0: 2},
        compiler_params=IN_FLIGHT,
    )(*_in_hbm([buf]), after)


def _first_relay(send_a, recv_a, buf, j, after, name):
    h = buf.shape[0] // 8

    def body(buf_in, send_a, recv_a, *rest):
        send_b, recv_b, send_c, recv_c, _, token = rest[len(after):]
        x, y, c = _pos()
        chips = _other_chips(x, y)
        nbr, other = chips[j], chips[1 - j]
        mine, landed = _shard_rows(buf_in, h, x, y, c), _shard_rows(buf_in, h, *nbr, c)
        _remote(mine, landed, send_a.at[j], recv_a.at[j], (*nbr, c)).wait_recv()
        _remote(landed, landed, send_b.at[0], recv_b.at[0], (x, y, 1 - c)).start()
        part = _part_rows(buf_in, h, *nbr, c, j)
        _remote(part, part, send_c.at[0], recv_c.at[0], (*other, c)).start()
        _remote(mine, mine, send_a.at[j], recv_a.at[j], (*nbr, c)).wait_send()
        token[...] = jnp.zeros_like(token)

    out = pl.pallas_call(
        body, name=name,
        out_shape=[pltpu.SemaphoreType.DMA((1,))] * 4 + [pltpu.HBM(buf.shape, buf.dtype),
                                                         jax.ShapeDtypeStruct((8, LANES), F32)],
        in_specs=[HBM, SEM, SEM] + [ANY] * len(after), out_specs=[SEM] * 4 + [HBM, TOKEN],
        input_output_aliases={0: 4}, compiler_params=IN_FLIGHT,
    )(buf, send_a, recv_a, *after)
    return tuple(out)


def _first_relay_done(send_b, recv_b, buf, j, after, name):
    h = buf.shape[0] // 8

    def body(buf_in, send_b, recv_b, after_ref, thru):
        x, y, c = _pos()
        nbr = _other_chips(x, y)[j]
        cp = _remote(_shard_rows(buf_in, h, *nbr, c), _shard_rows(buf_in, h, *nbr, 1 - c),
                     send_b.at[0], recv_b.at[0], (x, y, 1 - c))
        cp.wait_send()
        cp.wait_recv()

    return pl.pallas_call(
        body, name=name, out_shape=pltpu.HBM(buf.shape, buf.dtype),
        in_specs=[HBM, SEM, SEM, ANY], out_specs=HBM, input_output_aliases={0: 0},
        compiler_params=IN_FLIGHT,
    )(buf, send_b, recv_b, after)


def _first_diagonal(relays, buf, after, name):
    h = buf.shape[0] // 8

    def body(buf_in, send_c0, recv_c0, send_c1, recv_c1, after_ref, send_b, recv_b, thru, token):
        x, y, c = _pos()
        chips = _other_chips(x, y)
        diag = chips[2]
        for j, (send_c, recv_c) in enumerate(((send_c0, recv_c0), (send_c1, recv_c1))):
            part = _part_rows(buf_in, h, *diag, c, j)
            _remote(part, part, send_c.at[0], recv_c.at[0], (*chips[1 - j], c)).wait_recv()
        whole = _shard_rows(buf_in, h, *diag, c)
        _remote(whole, whole, send_b.at[0], recv_b.at[0], (x, y, 1 - c)).start()
        token[...] = jnp.zeros_like(token)

    out = pl.pallas_call(
        body, name=name,
        out_shape=[pltpu.SemaphoreType.DMA((1,))] * 2 + [pltpu.HBM(buf.shape, buf.dtype),
                                                         jax.ShapeDtypeStruct((8, LANES), F32)],
        in_specs=[HBM] + [SEM] * 4 + [ANY], out_specs=[SEM, SEM, HBM, TOKEN], input_output_aliases={0: 2},
        compiler_params=IN_FLIGHT,
    )(buf, relays[0][0], relays[0][1], relays[1][0], relays[1][1], after)
    return tuple(out)


def _first_diagonal_done(send_b, recv_b, relays, buf, after, name):
    h = buf.shape[0] // 8

    def body(buf_in, send_b, recv_b, send_c0, recv_c0, send_c1, recv_c1, after_ref, thru):
        x, y, c = _pos()
        chips = _other_chips(x, y)
        diag = chips[2]
        cp = _remote(_shard_rows(buf_in, h, *diag, c), _shard_rows(buf_in, h, *diag, 1 - c),
                     send_b.at[0], recv_b.at[0], (x, y, 1 - c))
        cp.wait_send()
        cp.wait_recv()
        for j, (send_c, recv_c) in enumerate(((send_c0, recv_c0), (send_c1, recv_c1))):
            part = _part_rows(buf_in, h, *chips[j], c, j)
            _remote(part, part, send_c.at[0], recv_c.at[0], (*chips[1 - j], c)).wait_send()

    return pl.pallas_call(
        body, name=name, out_shape=pltpu.HBM(buf.shape, buf.dtype),
        in_specs=[HBM] + [SEM] * 6 + [ANY], out_specs=HBM, input_output_aliases={0: 0},
        compiler_params=IN_FLIGHT,
    )(buf, send_b, recv_b, relays[0][0], relays[0][1], relays[1][0], relays[1][1], after)


def _exchange_start(name, srcs, lands, n_copies, plan, after):
    ns, nl = len(srcs), len(lands)
    extra = [] if after is None else [after]

    def body(*refs):
        base = ns + nl + len(extra)
        send_sems, recv_sems = refs[base], refs[base + 1]
        src_refs, land_refs = refs[base + 2:base + 2 + ns], refs[base + 2 + ns:base + 2 + ns + nl]
        token = refs[-1]
        x, y, c = _pos()
        copies = plan(src_refs, land_refs, x, y, c)
        assert len(copies) == n_copies
        for k, (src, dst, dev) in enumerate(copies):
            pltpu.make_async_remote_copy(
                src_ref=src, dst_ref=dst, send_sem=send_sems.at[k], recv_sem=recv_sems.at[k],
                device_id=dev, device_id_type=MESH).start()
        token[...] = jnp.zeros_like(token)

    out = pl.pallas_call(
        body, name=name,
        out_shape=[pltpu.SemaphoreType.DMA((n_copies,))] * 2
        + [pltpu.HBM(a.shape, a.dtype) for a in list(srcs) + list(lands)] + [jax.ShapeDtypeStruct((8, LANES), F32)],
        in_specs=[HBM] * (ns + nl) + [ANY] * len(extra), out_specs=[SEM, SEM] + [HBM] * (ns + nl) + [TOKEN],
        input_output_aliases={i: 2 + i for i in range(ns + nl)},
        compiler_params=IN_FLIGHT,
    )(*_in_hbm(list(srcs) + list(lands)), *extra)
    return out[0], out[1], out[2:2 + ns], out[2 + ns:2 + ns + nl], out[-1]


def _exchange_wait(name, send_sems, recv_sems, srcs, lands, plan, after):
    ns, nl = len(srcs), len(lands)

    def body(*refs):
        src_refs, land_refs = refs[:ns], refs[ns:ns + nl]
        send, recv = refs[ns + nl], refs[ns + nl + 1]
        x, y, c = _pos()
        for k, (src, dst, dev) in enumerate(plan(src_refs, land_refs, x, y, c)):
            cp = pltpu.make_async_remote_copy(
                src_ref=src, dst_ref=dst, send_sem=send.at[k], recv_sem=recv.at[k],
                device_id=dev, device_id_type=MESH)
            cp.wait_send()
            cp.wait_recv()

    out = pl.pallas_call(
        body, name=name, out_shape=[pltpu.HBM(a.shape, a.dtype) for a in list(srcs) + list(lands)],
        in_specs=[HBM] * (ns + nl) + [SEM, SEM, ANY], out_specs=[HBM] * (ns + nl),
        input_output_aliases={i: i for i in range(ns + nl)},
        compiler_params=IN_FLIGHT,
    )(*srcs, *lands, send_sems, recv_sems, after)
    return out[:ns], out[ns:]


def _plan_other_half_to_sibling(src_refs, land_refs, x, y, c):
    out = []
    for g_ref, r_ref in zip(src_refs, land_refs):
        h = g_ref.shape[1] // 2
        out.append((g_ref.at[:, pl.ds((1 - c) * h, h), :], r_ref, (x, y, 1 - c)))
    return out


def _plan_partials_to_chips(src_refs, land_refs, x, y, c):
    out = []
    for p_ref, r_ref in zip(src_refs, land_refs):
        for j, (px, py) in enumerate(_other_chips(x, y)):
            out.append((p_ref.at[2 * px + py], r_ref.at[j], (px, py, c)))
    return out


def _plan_share_half(src_refs, land_refs, x, y, c):
    out = []
    for f_ref in land_refs:
        h = f_ref.shape[0] // 2
        rows = f_ref.at[pl.ds(c * h, h)]
        out.append((rows, rows, (x, y, 1 - c)))
    return out


def _cast_into_slot(where, w, name, half=False, so_far=None):
    r, c = w.shape
    br, bc = _blk(r // 2 if half else r, 512, 8), _blk(c, 2048)
    nb = r // br
    n_steps = nb // 2 if half else nb

    def body(where_ref, w_ref, *rest):
        rest[-1][...] = w_ref[...].astype(BF16)

    first = (lambda where_ref: where_ref[1] * n_steps) if half else (lambda where_ref: 0)
    extra_specs, extra, aliases = ([], [], {}) if so_far is None else ([ANY], [so_far], {2: 0})
    grid_spec = pltpu.PrefetchScalarGridSpec(
        num_scalar_prefetch=1, grid=(n_steps, c // bc),
        in_specs=[pl.BlockSpec((br, bc), lambda i, j, where_ref: (first(where_ref) + i, j))] + extra_specs,
        out_specs=pl.BlockSpec((br, bc), lambda i, j, where_ref: (where_ref[0] * nb + first(where_ref) + i, j)))
    return pl.pallas_call(
        body, name=name, grid_spec=grid_spec,
        out_shape=jax.ShapeDtypeStruct((4 * r, c), BF16), input_output_aliases=aliases,
        compiler_params=_params(("parallel", "parallel")),
    )(where, w, *extra)


def _sum_own_and_sibling(core, g, r1, name):
    _, r, c = g.shape
    h = r // 2
    br, bc = _blk(h, 512, 8), _blk(c, 2048)
    nb = h // br

    def body(core_ref, g_ref, r_ref, o_ref):
        o_ref[...] = (g_ref[...] + r_ref[...]).astype(BF16)

    grid_spec = pltpu.PrefetchScalarGridSpec(
        num_scalar_prefetch=1, grid=(4, nb, c // bc),
        in_specs=[pl.BlockSpec((None, br, bc), lambda k, i, j, core_ref: (k, core_ref[0] * nb + i, j)),
                  pl.BlockSpec((None, br, bc), lambda k, i, j, core_ref: (k, i, j))],
        out_specs=pl.BlockSpec((None, br, bc), lambda k, i, j, core_ref: (k, i, j)))
    return pl.pallas_call(
        body, name=name, grid_spec=grid_spec,
        out_shape=jax.ShapeDtypeStruct((4, h, c), BF16),
        compiler_params=_params(("parallel", "parallel", "parallel")),
    )(core, g, r1)


def _sum_chips(where, p, r2, name, piece=(0, 1), so_far=None):
    _, h, c = p.shape
    k, n = piece
    br, bc = _blk(h, 512, 8), _blk(c, 2048)
    nb, ncb = h // br, c // bc

    def body(where_ref, p_ref, r_ref, *rest):
        acc = p_ref[...].astype(F32)
        for j in range(3):
            acc = acc + r_ref[j].astype(F32)
        rest[-1][...] = acc

    extra_specs, extra, aliases = ([], [], {}) if so_far is None else ([ANY], [so_far], {3: 0})
    grid_spec = pltpu.PrefetchScalarGridSpec(
        num_scalar_prefetch=1, grid=(nb, ncb),
        in_specs=[pl.BlockSpec((None, br, bc), lambda i, j, where_ref: (where_ref[0], i, j)),
                  pl.BlockSpec((3, br, bc), lambda i, j, where_ref: (0, i, j))] + extra_specs,
        out_specs=pl.BlockSpec((br, bc), lambda i, j, where_ref: (where_ref[1] * nb + i, k * ncb + j)))
    return pl.pallas_call(
        body, name=name, grid_spec=grid_spec,
        out_shape=jax.ShapeDtypeStruct((2 * h, n * c), F32), input_output_aliases=aliases,
        compiler_params=_params(("parallel", "parallel")),
    )(where, p, r2, *extra)


def _adamw_math(w, g, m, v):
    m2 = ADAM_B1 * m + (1.0 - ADAM_B1) * g
    v2 = ADAM_B2 * v + (1.0 - ADAM_B2) * (g * g)
    m_hat = m2 / (1.0 - ADAM_B1 ** ADAM_STEP)
    v_hat = v2 / (1.0 - ADAM_B2 ** ADAM_STEP)
    delta = -ADAM_LR * (m_hat / (jnp.sqrt(v_hat) + ADAM_EPS) + ADAM_WD * w)
    return delta, m2, v2


def _adamw(w, g, m, v, name, after=None):
    r, c = w.shape
    br, bc = _blk(r, 256, 8), _blk(c, 2048)

    def body(w_ref, g_ref, m_ref, v_ref, go_ref, d_ref, m2_ref, v2_ref):
        gv = g_ref[...]
        d, m2, v2 = _adamw_math(w_ref[...], gv, m_ref[...], v_ref[...])
        go_ref[...] = gv
        d_ref[...] = d
        m2_ref[...] = m2
        v2_ref[...] = v2

    spec = pl.BlockSpec((br, bc), lambda i, j: (i, j))
    body, extra_specs, extra = _ordered(body, 4, after)
    return pl.pallas_call(
        body, name=name, grid=(r // br, c // bc),
        in_specs=[spec] * 4 + extra_specs, out_specs=[spec] * 4,
        out_shape=[jax.ShapeDtypeStruct((r, c), F32)] * 4,
        compiler_params=_params(("parallel", "parallel")),
    )(w, g, m, v, *extra)


def _sum_rows8(g, name, after=None):
    n = g.shape[1]

    def body(g_ref, o_ref):
        acc = g_ref[0:1, :]
        for k in range(1, 8):
            acc = acc + g_ref[k:k + 1, :]
        o_ref[...] = acc

    body, extra_specs, extra = _ordered(body, 1, after)
    return pl.pallas_call(
        body, name=name, out_shape=jax.ShapeDtypeStruct((1, n), F32),
        in_specs=[pl.BlockSpec(memory_space=pltpu.VMEM)] + extra_specs,
        out_specs=pl.BlockSpec(memory_space=pltpu.VMEM),
        compiler_params=_params(),
    )(g, *extra)


def _mm(a, b, mode, out_dtype, name, bm=1024, bn=None, after=None, col_blocks=None):
    if mode == "nn":
        (_, m, k), (g, _, n) = a.shape, b.shape
    elif mode == "tn":
        (_, k, m), (g, _, n) = a.shape, b.shape
    else:
        (g, m, k), (_, n, _) = a.shape, b.shape
    if bn is None:
        bn = 1024 if k <= 2048 else 512
    bm, bn = _blk(m, bm), _blk(n, bn)

    if mode == "nt":
        def body(a_ref, b_ref, o_ref, acc_ref):
            part = lax.dot_general(a_ref[...], b_ref[...], (((1,), (1,)), ((), ())),
                                   preferred_element_type=F32)
            if g == 1:
                o_ref[...] = part.astype(out_dtype)
            else:
                gi = pl.program_id(2)

                @pl.when(gi == 0)
                def _():
                    acc_ref[...] = part

                @pl.when(gi > 0)
                def _():
                    acc_ref[...] += part

                @pl.when(gi == g - 1)
                def _():
                    o_ref[...] = acc_ref[...].astype(out_dtype)

        body, extra_specs, extra = _ordered(body, 2, after)
        return pl.pallas_call(
            body, name=name, grid=(m // bm, n // bn, g),
            in_specs=[pl.BlockSpec((None, bm, k), lambda i, j, gi: (gi, i, 0)),
                      pl.BlockSpec((None, bn, k), lambda i, j, gi: (gi, j, 0))] + extra_specs,
            out_specs=pl.BlockSpec((None, bm, bn), lambda i, j, gi: (0, i, j)),
            out_shape=jax.ShapeDtypeStruct((1, m, n), out_dtype),
            scratch_shapes=[pltpu.VMEM((bm, bn), F32)],
            compiler_params=_params(("parallel", "parallel", "arbitrary")),
        )(a, b, *extra)

    contract = (((1,), (0,)), ((), ())) if mode == "nn" else (((0,), (0,)), ((), ()))

    def body(a_ref, b_ref, o_ref):
        o_ref[...] = lax.dot_general(a_ref[...], b_ref[...], contract,
                                     preferred_element_type=F32).astype(out_dtype)

    a_spec = (pl.BlockSpec((None, bm, k), lambda i, gi, j: (0, i, 0)) if mode == "nn"
              else pl.BlockSpec((None, k, bm), lambda i, gi, j: (0, 0, i)))
    first, count = (0, n // bn) if col_blocks is None else col_blocks
    body, extra_specs, extra = _ordered(body, 2, after)
    return pl.pallas_call(
        body, name=name, grid=(m // bm, g, count),
        in_specs=[a_spec, pl.BlockSpec((None, k, bn), lambda i, gi, j: (gi, 0, first + j))] + extra_specs,
        out_specs=pl.BlockSpec((None, bm, bn), lambda i, gi, j: (gi, i, j)),
        out_shape=jax.ShapeDtypeStruct((g, m, count * bn), out_dtype),
        compiler_params=_params(("parallel", "parallel", "parallel")),
    )(a, b, *extra)


def _mm_plane(a, b, plane, name, planes_so_far=None, after=None, bm=1024, bn=1024):
    (m, k), (g, _, n) = a.shape, b.shape
    bm, bn = _blk(m, bm), _blk(n, bn)

    def body(plane_ref, a_ref, b_ref, *rest):
        rest[-1][...] = jnp.dot(a_ref[...], b_ref[...], preferred_element_type=F32).astype(BF16)

    extra_specs, extra, aliases = [], [], {}
    if planes_so_far is not None:
        extra_specs.append(ANY)
        extra.append(planes_so_far)
        aliases = {3: 0}
    if after is not None:
        extra_specs.append(ANY)
        extra.append(after)
    grid_spec = pltpu.PrefetchScalarGridSpec(
        num_scalar_prefetch=1, grid=(m // bm, n // bn),
        in_specs=[pl.BlockSpec((bm, k), lambda i, j, p: (i, 0)),
                  pl.BlockSpec((None, k, bn), lambda i, j, p: (p[0], 0, j))] + extra_specs,
        out_specs=pl.BlockSpec((None, bm, bn), lambda i, j, p: (p[0], i, j)))
    return pl.pallas_call(
        body, name=name, grid_spec=grid_spec,
        out_shape=jax.ShapeDtypeStruct((g, m, n), BF16), input_output_aliases=aliases,
        compiler_params=_params(("parallel", "parallel")),
    )(plane, a, b, *extra)


def _row_specs(br, d):
    return (pl.BlockSpec((br, d), lambda i: (i, 0)), pl.BlockSpec((1, d), lambda i: (0, 0)),
            pl.BlockSpec((8, d), lambda i: (0, 0)))


def _rstd(xv):
    return lax.rsqrt(jnp.mean(xv * xv, axis=-1, keepdims=True) + RMS_EPS)


def _colsum(v):
    return jnp.sum(v, axis=0, keepdims=True)


def _norm_mod_fwd(x, g, scale, shift, name, o=None, gate=None, after=None):
    s, d = x.shape
    br = _blk(s, 256, 8)
    has_res = o is not None
    row, vec, _ = _row_specs(br, d)

    def body(*refs):
        if has_res:
            x_ref, o_ref, gate_ref, g_ref, sc_ref, sh_ref, x1_ref, h_ref = refs
            xv = x_ref[...] + gate_ref[...] * o_ref[...]
            x1_ref[...] = xv
        else:
            x_ref, g_ref, sc_ref, sh_ref, h_ref = refs
            xv = x_ref[...]
        n = xv * _rstd(xv) * g_ref[...]
        h_ref[...] = (n * (1.0 + sc_ref[...]) + sh_ref[...]).astype(BF16)

    ins = [x] + ([o, gate] if has_res else []) + [g, scale, shift]
    in_specs = [row] + ([row, vec] if has_res else []) + [vec] * 3
    out_shape = ([jax.ShapeDtypeStruct((s, d), F32)] if has_res else []) + [jax.ShapeDtypeStruct((s, d), BF16)]
    body, extra_specs, extra = _ordered(body, len(ins), after)
    out = pl.pallas_call(
        body, name=name, grid=(s // br,), in_specs=in_specs + extra_specs, out_specs=[row] * len(out_shape),
        out_shape=out_shape, compiler_params=_params(("parallel",)),
    )(*ins, *extra)
    return out if has_res else out[0]


def _final_loss(x1, o1, gate1, final_g, tgt, name):
    s, d = x1.shape
    br = _blk(s, 256, 8)
    row, vec, acc = _row_specs(br, d)

    def body(x1_ref, o_ref, gate_ref, g_ref, t_ref, dx_ref, do_ref, acc_ref):
        @pl.when(pl.program_id(0) == 0)
        def _():
            acc_ref[...] = jnp.zeros_like(acc_ref)

        gate, o, g = gate_ref[...], o_ref[...], g_ref[...]
        x2 = x1_ref[...] + gate * o
        r = _rstd(x2)
        xh = x2 * r
        err = xh * g - t_ref[...]
        loss = 0.5 * _colsum(jnp.mean(err * err, axis=-1, keepdims=True))
        dout = err * (1.0 / d)
        dxh = dout * g
        dx2 = r * (dxh - xh * jnp.mean(dxh * xh, axis=-1, keepdims=True))
        dx_ref[...] = dx2
        do_ref[...] = (dx2 * gate).astype(BF16)
        acc_ref[0:1, :] += _colsum(dout * xh)
        acc_ref[1:2, :] += _colsum(dx2 * o)
        acc_ref[2:3, :] += jnp.broadcast_to(loss, (1, d))

    return pl.pallas_call(
        body, name=name, grid=(s // br,),
        in_specs=[row, row, vec, vec, row], out_specs=[row, row, acc],
        out_shape=[jax.ShapeDtypeStruct((s, d), F32), jax.ShapeDtypeStruct((s, d), BF16),
                   jax.ShapeDtypeStruct((8, d), F32)],
        compiler_params=_params(("arbitrary",)),
    )(x1, o1, gate1, final_g, tgt)


def _norm_mod_bwd(dh, x, g, scale, dx_next, name, o_prev=None, gate_prev=None, after=None):
    s, d = x.shape
    br = _blk(s, 256, 8)
    has_prev = o_prev is not None
    row, vec, acc = _row_specs(br, d)

    def body(*refs):
        if has_prev:
            dh_ref, x_ref, g_ref, sc_ref, dxn_ref, o_ref, gate_ref, dx_ref, do_ref, acc_ref = refs
        else:
            dh_ref, x_ref, g_ref, sc_ref, dxn_ref, dx_ref, acc_ref = refs

        @pl.when(pl.program_id(0) == 0)
        def _():
            acc_ref[...] = jnp.zeros_like(acc_ref)

        xv, gv, dhv = x_ref[...], g_ref[...], dh_ref[...]
        r = _rstd(xv)
        xh = xv * r
        acc_ref[0:1, :] += _colsum(dhv * (xh * gv))
        acc_ref[1:2, :] += _colsum(dhv)
        dn = dhv * (1.0 + sc_ref[...])
        acc_ref[2:3, :] += _colsum(dn * xh)
        dxh = dn * gv
        dx = dxn_ref[...] + r * (dxh - xh * jnp.mean(dxh * xh, axis=-1, keepdims=True))
        dx_ref[...] = dx
        if has_prev:
            acc_ref[3:4, :] += _colsum(dx * o_ref[...])
            do_ref[...] = (dx * gate_ref[...]).astype(BF16)

    ins = [dh, x, g, scale, dx_next] + ([o_prev, gate_prev] if has_prev else [])
    in_specs = [row, row, vec, vec, row] + ([row, vec] if has_prev else [])
    out_shape = [jax.ShapeDtypeStruct((s, d), F32)]
    out_specs = [row]
    if has_prev:
        out_shape.append(jax.ShapeDtypeStruct((s, d), BF16))
        out_specs.append(row)
    out_shape.append(jax.ShapeDtypeStruct((8, d), F32))
    out_specs.append(acc)
    body, extra_specs, extra = _ordered(body, len(ins), after)
    return pl.pallas_call(
        body, name=name, grid=(s // br,), in_specs=in_specs + extra_specs, out_specs=out_specs,
        out_shape=out_shape, compiler_params=_params(("arbitrary",)),
    )(*ins, *extra)


def _tiles(p):
    s, c = p.shape
    return p.reshape(s // SUBLANES, SUBLANES, c)


def _shift_down(p, k):
    if k == 0:
        return p
    r = pltpu.roll(_tiles(p), k, 1)
    before = jnp.concatenate([jnp.zeros_like(r[:1]), r[:-1]], axis=0)
    rows = lax.broadcasted_iota(jnp.int32, r.shape, 1)
    return jnp.where(rows >= k, r, before).reshape(p.shape)


def _shift_up(p, k):
    if k == 0:
        return p
    r = pltpu.roll(_tiles(p), SUBLANES - k, 1)
    after = jnp.concatenate([r[1:], jnp.zeros_like(r[:1])], axis=0)
    rows = lax.broadcasted_iota(jnp.int32, r.shape, 1)
    return jnp.where(rows < SUBLANES - k, r, after).reshape(p.shape)


def _sigmoid(z):
    return 0.5 * (jnp.tanh(0.5 * z) + 1.0)


def _sc_parts(proj_ref, w_ref):
    b, cg, v, g = (proj_ref[i].astype(F32) for i in range(4))
    p = cg * v
    u = w_ref[2:3, :] * p + w_ref[1:2, :] * _shift_down(p, 1) + w_ref[0:1, :] * _shift_down(p, 2)
    return b, cg, v, g, p, u


def _sc_fwd(proj, conv_w, name):
    _, s, e = proj.shape
    bc = _blk(e, 256)

    def body(proj_ref, w_ref, y_ref):
        b, _, _, g, _, u = _sc_parts(proj_ref, w_ref)
        y_ref[...] = (b * u * (g * _sigmoid(g))).astype(BF16)

    return pl.pallas_call(
        body, name=name, grid=(e // bc,),
        in_specs=[pl.BlockSpec((4, s, bc), lambda j: (0, 0, j)), pl.BlockSpec((3, bc), lambda j: (0, j))],
        out_specs=pl.BlockSpec((s, bc), lambda j: (0, j)),
        out_shape=jax.ShapeDtypeStruct((s, e), BF16),
        compiler_params=_params(("parallel",)),
    )(proj, conv_w)


def _sc_bwd(proj, dy, conv_w, name, after=None):
    _, s, e = proj.shape
    bc = _blk(e, 256)

    def body(proj_ref, dy_ref, w_ref, dp_ref, dw_ref):
        b, cg, v, g, p, u = _sc_parts(proj_ref, w_ref)
        dyv = dy_ref[...].astype(F32)
        sig = _sigmoid(g)
        t = dyv * (g * sig)
        du = t * b
        dp_ref[0] = (t * u).astype(BF16)
        dp_ref[3] = (dyv * b * u * (sig * (1.0 + g * (1.0 - sig)))).astype(BF16)
        dpp = w_ref[2:3, :] * du + w_ref[1:2, :] * _shift_up(du, 1) + w_ref[0:1, :] * _shift_up(du, 2)
        dp_ref[1] = (dpp * v).astype(BF16)
        dp_ref[2] = (dpp * cg).astype(BF16)
        dw_ref[2:3, :] = _colsum(du * p)
        dw_ref[1:2, :] = _colsum(du * _shift_down(p, 1))
        dw_ref[0:1, :] = _colsum(du * _shift_down(p, 2))

    body, extra_specs, extra = _ordered(body, 3, after)
    return pl.pallas_call(
        body, name=name, grid=(e // bc,),
        in_specs=[pl.BlockSpec((4, s, bc), lambda j: (0, 0, j)), pl.BlockSpec((s, bc), lambda j: (0, j)),
                  pl.BlockSpec((3, bc), lambda j: (0, j))] + extra_specs,
        out_specs=[pl.BlockSpec((4, s, bc), lambda j: (0, 0, j)), pl.BlockSpec((3, bc), lambda j: (0, j))],
        out_shape=[jax.ShapeDtypeStruct((4, s, e), BF16), jax.ShapeDtypeStruct((3, e), F32)],
        compiler_params=_params(("parallel",)),
    )(proj, dy, conv_w, *extra)


def _softplus_neg(lam):
    u = jnp.exp(-jnp.abs(lam))
    w = 1.0 + u
    log1p = jnp.where(w == 1.0, u, jnp.log(w) * (u / jnp.where(w == 1.0, 1.0, w - 1.0)))
    return jnp.maximum(-lam, 0.0) + log1p


def _one_minus_exp(z):
    series = -z * (1.0 + z * (0.5 + z * (1.0 / 6.0 + z * (1.0 / 24.0))))
    return jnp.where(z > -0.02, series, 1.0 - jnp.exp(z))


def _scan_in_tiles(a, b, reverse):
    shape = a.shape
    a, b = _tiles(a), _tiles(b)
    rows = lax.broadcasted_iota(jnp.int32, a.shape, 1)
    for step in (1, 2, 4):
        shift = SUBLANES - step if reverse else step
        ok = rows < SUBLANES - step if reverse else rows >= step
        a_s, b_s = pltpu.roll(a, shift, 1), pltpu.roll(b, shift, 1)
        b = jnp.where(ok, a * b_s + b, b)
        a = jnp.where(ok, a * a_s, a)
    return a.reshape(shape), b.reshape(shape)


def _by_rows(fn, arrays, rows=32):
    s = arrays[0].shape[0]
    rows = min(rows, s)
    for t in range(0, s, rows):
        fn(t, *(a[t:t + rows] for a in arrays))


def _scan_carry(a_ref, b_ref, h_ref, reverse):
    s, c = a_ref.shape
    n = s // 8

    def step(i, carry):
        gi = n - 1 - i if reverse else i
        sl = pl.ds(pl.multiple_of(gi * 8, 8), 8)
        h = b_ref[sl, :] + a_ref[sl, :] * carry
        h_ref[sl, :] = h
        return h[0:1, :] if reverse else h[7:8, :]

    lax.fori_loop(0, n, step, jnp.zeros((1, c), F32), unroll=8 if n % 8 == 0 else 1)


def _lru_specs(s, e_half, n_heads):
    hp = e_half // HEAD_DIM
    c = HEAD_DIM
    return dict(
        pair=pl.BlockSpec((2, None, s, c), lambda h: (0, h // hp, 0, h % hp)),
        conv_w=pl.BlockSpec((4, c), lambda h: (0, h)),
        chan=pl.BlockSpec((1, c), lambda h: (0, h)),
        w=pl.BlockSpec((4, None, c // 4, c), lambda h: (0, h, 0, 0)),
        bias=pl.BlockSpec((None, 1, c), lambda h: (h, 0, 0)),
        plane=pl.BlockSpec((s, c), lambda h: (0, h)),
    )


def _lru_gate_inputs(vp, cw_ref, cb_ref, wa_ref, ba_ref, wx_ref, bx_ref):
    c = HEAD_DIM
    taps = [_shift_down(vp, 3 - k) for k in range(4)]
    v = cb_ref[...] + sum(cw_ref[k:k + 1, :] * taps[k] for k in range(4))
    vb = v.astype(BF16)
    wa = wa_ref[...].reshape(c, c)
    wx = wx_ref[...].reshape(c, c)
    zr = jnp.dot(vb, wa, preferred_element_type=F32) + ba_ref[...]
    zi = jnp.dot(vb, wx, preferred_element_type=F32) + bx_ref[...]
    return taps, v, vb, wa, wx, zr, zi


def _lru_fwd(proj, conv_w, conv_b, w_a, b_a, w_x, b_x, lam, name):
    _, _, s, e_half = proj.shape
    n_heads = 2 * e_half // HEAD_DIM
    sp_ = _lru_specs(s, e_half, n_heads)

    def body(pg_ref, cw_ref, cb_ref, wa_ref, ba_ref, wx_ref, bx_ref, lam_ref,
             y_ref, a_ref, hs_ref, sa_ref, sb_ref, sh_ref):
        _, v, _, _, _, zr, zi = _lru_gate_inputs(
            pg_ref[0].astype(F32), cw_ref, cb_ref, wa_ref, ba_ref, wx_ref, bx_ref)
        rate = (-RGLRU_C) * _softplus_neg(lam_ref[...])

        def decay_and_input(t, v_c, zr_c, zi_c):
            la = rate * _sigmoid(zr_c)
            a = jnp.exp(la)
            b = jnp.sqrt(_one_minus_exp(2.0 * la)) * (_sigmoid(zi_c) * v_c)
            a_ref[t:t + a.shape[0]] = a
            sa_ref[t:t + a.shape[0]], sb_ref[t:t + a.shape[0]] = _scan_in_tiles(a, b, reverse=False)

        _by_rows(decay_and_input, [v, zr, zi])
        _scan_carry(sa_ref, sb_ref, sh_ref, reverse=False)

        def gated_output(t, hs_c, g_c):
            g = g_c.astype(F32)
            y_ref[t:t + g.shape[0]] = (hs_c * (g * _sigmoid(g))).astype(BF16)
            hs_ref[t:t + g.shape[0]] = hs_c.astype(BF16)

        _by_rows(gated_output, [sh_ref, pg_ref.at[1]])

    e = 2 * e_half
    return pl.pallas_call(
        body, name=name, grid=(n_heads,),
        in_specs=[sp_["pair"], sp_["conv_w"], sp_["chan"], sp_["w"], sp_["bias"], sp_["w"],
                  sp_["bias"], sp_["chan"]],
        out_specs=[sp_["plane"]] * 3,
        out_shape=[jax.ShapeDtypeStruct((s, e), BF16), jax.ShapeDtypeStruct((s, e), F32),
                   jax.ShapeDtypeStruct((s, e), BF16)],
        scratch_shapes=[pltpu.VMEM((s, HEAD_DIM), F32)] * 3,
        compiler_params=_params(("parallel",)),
    )(proj, conv_w, conv_b, w_a, b_a, w_x, b_x, lam)


def _lru_bwd(proj, dy, saved, conv_w, conv_b, w_a, b_a, w_x, b_x, lam, name):
    _, _, s, e_half = proj.shape
    e = 2 * e_half
    c = HEAD_DIM
    n_heads = e // c
    sp_ = _lru_specs(s, e_half, n_heads)

    def body(pg_ref, dy_ref, a_ref, hs_ref, cw_ref, cb_ref, wa_ref, ba_ref, wx_ref, bx_ref, lam_ref,
             dpg_ref, dwa_ref, dwx_ref, dba_ref, dbx_ref, dlam_ref, dcw_ref, dcb_ref,
             sa_ref, sb_ref, sd_ref, dzr_ref, dzi_ref):
        taps, v, vb, wa, wx, zr, zi = _lru_gate_inputs(
            pg_ref[0].astype(F32), cw_ref, cb_ref, wa_ref, ba_ref, wx_ref, bx_ref)
        lam = lam_ref[...]
        rate = (-RGLRU_C) * _softplus_neg(lam)
        a = a_ref[...]

        def state_gradient_in_tiles(t, a_next, dy_c, g_c, hs_c):
            g, dyv = g_c.astype(F32), dy_c.astype(F32)
            sig = _sigmoid(g)
            rows = slice(t, t + g.shape[0])
            dpg_ref[1, rows] = (dyv * hs_c.astype(F32) * (sig * (1.0 + g * (1.0 - sig)))).astype(BF16)
            sa_ref[rows], sb_ref[rows] = _scan_in_tiles(a_next, dyv * (g * sig), reverse=True)

        _by_rows(state_gradient_in_tiles, [_shift_up(a, 1), dy_ref, pg_ref.at[1], hs_ref])
        _scan_carry(sa_ref, sb_ref, sd_ref, reverse=True)

        sums = []

        def gate_gradients(t, dh, hs_before, a_c, zr_c, zi_c, v_c):
            r, i = _sigmoid(zr_c), _sigmoid(zi_c)
            q = (1.0 - a_c) * (1.0 + a_c)
            inv_nm = lax.rsqrt(q)
            div = dh * (q * inv_nm)
            dla = (dh * hs_before) * a_c - (dh * (i * v_c)) * (a_c * a_c * inv_nm)
            dzr = (dla * rate) * (r * (1.0 - r))
            dzi = (div * v_c) * (i * (1.0 - i))
            rows = slice(t, t + dh.shape[0])
            dzr_ref[rows], dzi_ref[rows] = dzr.astype(BF16), dzi.astype(BF16)
            sa_ref[rows] = div * i
            sums.append((_colsum(dla * r), _colsum(dzr), _colsum(dzi)))

        _by_rows(gate_gradients, [sd_ref, _shift_down(hs_ref[...].astype(F32), 1), a_ref, zr, zi, v])
        dlam_ref[...] = sum(p[0] for p in sums) * ((-RGLRU_C) * (-_sigmoid(-lam)))
        dba_ref[...] = sum(p[1] for p in sums)
        dbx_ref[...] = sum(p[2] for p in sums)
        dzr_b, dzi_b = dzr_ref[...], dzi_ref[...]
        tn = (((0,), (0,)), ((), ()))
        nt = (((1,), (1,)), ((), ()))
        dwa_ref[...] = lax.dot_general(vb, dzr_b, tn, preferred_element_type=F32)
        dwx_ref[...] = lax.dot_general(vb, dzi_b, tn, preferred_element_type=F32)
        dv = (sa_ref[...] + lax.dot_general(dzr_b, wa, nt, preferred_element_type=F32)
              + lax.dot_general(dzi_b, wx, nt, preferred_element_type=F32))
        dcb_ref[...] = _colsum(dv)
        dvp = jnp.zeros_like(dv)
        for k in range(4):
            dvp = dvp + cw_ref[k:k + 1, :] * _shift_up(dv, 3 - k)
            dcw_ref[k:k + 1, :] = _colsum(dv * taps[k])
        dpg_ref[0] = dvp.astype(BF16)

    head_mat = pl.BlockSpec((None, c, c), lambda h: (h, 0, 0))
    outs = pl.pallas_call(
        body, name=name, grid=(n_heads,),
        in_specs=[sp_["pair"]] + [sp_["plane"]] * 3 + [sp_["conv_w"], sp_["chan"], sp_["w"], sp_["bias"],
                                                        sp_["w"], sp_["bias"], sp_["chan"]],
        out_specs=[sp_["pair"], head_mat, head_mat, sp_["bias"], sp_["bias"],
                   sp_["chan"], sp_["conv_w"], sp_["chan"]],
        out_shape=[jax.ShapeDtypeStruct((2, 2, s, e_half), BF16),
                   jax.ShapeDtypeStruct((n_heads, c, c), F32), jax.ShapeDtypeStruct((n_heads, c, c), F32),
                   jax.ShapeDtypeStruct((n_heads, 1, c), F32), jax.ShapeDtypeStruct((n_heads, 1, c), F32),
                   jax.ShapeDtypeStruct((1, e), F32), jax.ShapeDtypeStruct((4, e), F32),
                   jax.ShapeDtypeStruct((1, e), F32)],
        scratch_shapes=[pltpu.VMEM((s, c), F32)] * 3 + [pltpu.VMEM((s, c), BF16)] * 2,
        compiler_params=_params(("parallel",)),
    )(proj, dy, *saved, conv_w, conv_b, w_a, b_a, w_x, b_x, lam)
    return tuple(outs)


def _ada_fwd(c_all, ada_w, name):
    n_l, d, f = ada_w.shape
    bf = _blk(f, 512)

    def body(c_ref, w_ref, o_ref):
        cv = c_ref[...]
        sc = (cv * _sigmoid(cv)).astype(BF16)
        o_ref[...] = jnp.dot(sc, w_ref[...].astype(BF16), preferred_element_type=F32)

    return pl.pallas_call(
        body, name=name, grid=(n_l, f // bf),
        in_specs=[pl.BlockSpec((8, d), lambda l, j: (0, 0)), pl.BlockSpec((None, d, bf), lambda l, j: (l, 0, j))],
        out_specs=pl.BlockSpec((None, 8, bf), lambda l, j: (l, 0, j)),
        out_shape=jax.ShapeDtypeStruct((n_l, 8, f), F32),
        compiler_params=_params(("parallel", "parallel")),
    )(c_all, ada_w)


def _ada_bwd_adamw(c_t, dmod, w, m, v, name, after=None):
    n_l, d, f = w.shape
    bf = _blk(f, 256)

    def body(c_ref, dm_ref, w_ref, m_ref, v_ref, g_ref, d_ref, m2_ref, v2_ref):
        cv = c_ref[...]
        sc = cv * _sigmoid(cv)
        dm = dm_ref[...]
        g = sc[:, 0:1] * dm[0:1, :]
        for b in range(1, 8):
            g = g + sc[:, b:b + 1] * dm[b:b + 1, :]
        g_ref[...] = g
        dl, m2, v2 = _adamw_math(w_ref[...], g, m_ref[...], v_ref[...])
        d_ref[...] = dl
        m2_ref[...] = m2
        v2_ref[...] = v2

    big = pl.BlockSpec((None, d, bf), lambda l, j: (l, 0, j))
    body, extra_specs, extra = _ordered(body, 5, after)
    return pl.pallas_call(
        body, name=name, grid=(n_l, f // bf),
        in_specs=[pl.BlockSpec((d, 8), lambda l, j: (0, 0)), pl.BlockSpec((None, 8, bf), lambda l, j: (l, 0, j)),
                  big, big, big] + extra_specs,
        out_specs=[big] * 4, out_shape=[jax.ShapeDtypeStruct((n_l, d, f), F32)] * 4,
        compiler_params=_params(("parallel", "parallel")),
    )(c_t, dmod, w, m, v, *extra)


def _pack(parts):
    padded, offs, n = [], [], 0
    for p in parts:
        p = p.reshape(-1)
        size = -(-p.shape[0] // PACK) * PACK
        offs.append(n)
        n += size
        padded.append(jnp.pad(p, (0, size - p.shape[0])) if size != p.shape[0] else p)
    return jnp.concatenate(padded), offs, n


def kernel(x, c, norm_g, ada_w, ada_b, sc_w_in, sc_conv_w, sc_w_out, lru_w_in, lru_conv_w, lru_conv_b, lru_w_a, lru_b_a, lru_w_x, lru_b_x, lru_lambda, lru_w_out, final_g, loss_target, m_norm_g, m_ada_w, m_ada_b, m_sc_w_in, m_sc_conv_w, m_sc_w_out, m_lru_w_in, m_lru_conv_w, m_lru_conv_b, m_lru_w_a, m_lru_b_a, m_lru_w_x, m_lru_b_x, m_lru_lambda, m_lru_w_out, m_final_g, v_norm_g, v_ada_w, v_ada_b, v_sc_w_in, v_sc_conv_w, v_sc_w_out, v_lru_w_in, v_lru_conv_w, v_lru_conv_b, v_lru_w_a, v_lru_b_a, v_lru_w_x, v_lru_b_x, v_lru_lambda, v_lru_w_out, v_final_g):
    xi, yi, ci = _pos()
    chip = 2 * xi + yi
    batch = 4 * xi + 2 * yi + ci
    core_op = jnp.reshape(ci, (1,)).astype(jnp.int32)
    chip_op = jnp.reshape(chip, (1,)).astype(jnp.int32)
    where_op = jnp.stack([chip, ci]).astype(jnp.int32)

    x2d, tgt = x[0], loss_target[0]
    s, d = x2d.shape
    es = sc_conv_w.shape[2]
    e = 4 * es
    n_heads = lru_w_a.shape[1]
    hj = lru_b_a.shape[2]
    f = ada_w.shape[2]
    row = lambda t: t.reshape(1, -1)

    small_parts = [c, sc_conv_w, lru_conv_w, lru_conv_b, lru_b_a, lru_b_x, lru_lambda]
    small, offs, n_small = _pack(small_parts)
    got = _allgather8([small.reshape(8, n_small // 8)], "ag_small")[0].reshape(8, n_small)
    c_all = got[:, :d]
    per_chip = got[0::2]

    def chip_part(k, shape):
        size = 1
        for dim in shape:
            size *= dim
        return per_chip[:, offs[k]:offs[k] + size].reshape((4,) + shape)

    conv_w0 = jnp.transpose(chip_part(1, (3, es)), (1, 0, 2)).reshape(3, e)
    conv_w1 = jnp.transpose(chip_part(2, (4, es)), (1, 0, 2)).reshape(4, e)
    conv_b1 = chip_part(3, (es,)).reshape(1, e)
    b_a = jnp.transpose(chip_part(4, (n_heads, hj)), (1, 0, 2)).reshape(n_heads, 1, 4 * hj)
    b_x = jnp.transpose(chip_part(5, (n_heads, hj)), (1, 0, 2)).reshape(n_heads, 1, 4 * hj)
    lam = chip_part(6, (es,)).reshape(1, e)

    mod_nb = _ada_fwd(c_all, ada_w, "ada_fwd")
    mods = _allgather8([mod_nb.reshape(16, f)], "ag_mod")[0].reshape(8, 2, 8, f)[0::2]
    mine = lax.dynamic_index_in_dim(mods, batch, axis=2, keepdims=False)
    mod = jnp.transpose(mine, (1, 0, 2)).reshape(2, 4 * f) + ada_b
    shift = [row(mod[l, :d]) for l in range(2)]
    scale = [row(mod[l, d:2 * d]) for l in range(2)]
    gate = [row(mod[l, 2 * d:]) for l in range(2)]
    ng = [row(norm_g[l]) for l in range(2)]

    shards = [sc_w_in[0], sc_w_out[0], lru_w_in[0], lru_w_a[0].reshape(n_heads * hj, HEAD_DIM),
              lru_w_x[0].reshape(n_heads * hj, HEAD_DIM), lru_w_out[0]]
    names = ["sc_w_in", "sc_w_out", "lru_w_in", "lru_w_a", "lru_w_x", "lru_w_out"]
    slots = [None] + [_cast_into_slot(where_op, w, "cast_" + nm) for w, nm in zip(shards[1:], names[1:])]
    sent_half = _cast_into_slot(where_op, shards[0], "cast_sc_w_in_sent", half=True)
    send_a, recv_a, buf, started = _first_start(sent_half, mod, "ag_first_start")
    other_half = jnp.stack([chip, 1 - ci]).astype(jnp.int32)
    buf = _cast_into_slot(other_half, shards[0], "cast_sc_w_in_kept", half=True, so_far=buf)
    h0 = _norm_mod_fwd(x2d, ng[0], scale[0], shift[0], "norm0", after=started)
    planes = [jnp.reshape(2 * px + py, (1,)).astype(jnp.int32) for px, py in _other_chips(xi, yi)]
    proj0 = _mm_plane(h0, buf.reshape(4, d, e), chip_op, "sc_in_own")
    small_names = ["norm_g", "ada_b", "final_g", "sc_conv_w", "lru_conv_w", "lru_conv_b", "lru_b_a",
                   "lru_b_x", "lru_lambda"]
    small_w = [norm_g, ada_b, final_g, sc_conv_w, lru_conv_w, lru_conv_b, lru_b_a, lru_b_x, lru_lambda]
    small_m = [m_norm_g, m_ada_b, m_final_g, m_sc_conv_w, m_lru_conv_w, m_lru_conv_b, m_lru_b_a,
               m_lru_b_x, m_lru_lambda]
    small_v = [v_norm_g, v_ada_b, v_final_g, v_sc_conv_w, v_lru_conv_w, v_lru_conv_b, v_lru_b_a,
               v_lru_b_x, v_lru_lambda]
    pw, soffs, n_s = _pack(small_w)
    small_state = [t.reshape(n_s // PACK, PACK) for t in (pw, _pack(small_m)[0], _pack(small_v)[0])]
    relays, passed, relayed = [], [], [proj0] + slots[1:] + small_state
    for j in range(2):
        send_b, recv_b, send_c, recv_c, buf, token = _first_relay(send_a, recv_a, buf, j, relayed,
                                                                  "ag_first_relay_%d" % j)
        relays.append((send_c, recv_c))
        passed.append((send_b, recv_b))
        relayed = [token]
    rest_flight, relayed = _gather_start([[slots[1]], [slots[2]], slots[3:5], [slots[5]]], token, "ag_start")
    for j in range(2):
        buf = _first_relay_done(*passed[j], buf, j, relayed, "ag_first_relay_done_%d" % j)
        proj0 = _mm_plane(h0, buf.reshape(4, d, e), planes[j], "sc_in_%d" % j, planes_so_far=proj0)
        relayed = proj0
    send_b, recv_b, buf, relayed = _first_diagonal(relays, buf, proj0, "ag_first_diagonal")
    buf = _first_diagonal_done(send_b, recv_b, relays, buf, relayed, "ag_first_diagonal_done")
    proj0 = _mm_plane(h0, buf.reshape(4, d, e), planes[2], "sc_in_2", planes_so_far=proj0)
    w_in0 = buf.reshape(4, d, e)
    in_flight = [None] + rest_flight

    def arrived(g, after, tag):
        send1, recv1, bufs = in_flight[g]
        send2, recv2, bufs, passed = _gather_forward(send1, recv1, bufs, after, "ag_forward_" + tag)
        return _gather_finish(send2, recv2, bufs, passed, "ag_finish_" + tag)

    y0 = _sc_fwd(proj0, conv_w0, "sc_mix")
    w_out0 = arrived(1, y0, "sc_w_out")[0].reshape(1, e, d)
    o0 = _mm(y0[None], w_out0, "nn", F32, "sc_out")[0]
    x1, h1 = _norm_mod_fwd(x2d, ng[1], scale[1], shift[1], "norm1", o=o0, gate=gate[0])
    send1, recv1, bufs = in_flight[2]
    proj1 = _mm_plane(h1, bufs[0].reshape(4, d, e // 2), chip_op, "lru_in_own")
    for j in range(3):
        send2, recv2, bufs, passed_on = _gather_forward(send1, recv1, bufs, proj1, "ag_forward_lru_w_in_%d" % j,
                                                        sources=(j,))
        bufs = _gather_finish(send2, recv2, bufs, passed_on, "ag_finish_lru_w_in_%d" % j, sources=(j,))
        proj1 = _mm_plane(h1, bufs[0].reshape(4, d, e // 2), planes[j], "lru_in_%d" % j, planes_so_far=proj1)
    w_in1 = bufs[0].reshape(4, d, e // 2)
    gate_ws = arrived(3, proj1, "lru_gates")
    w_a = gate_ws[0].reshape(4, n_heads, hj, HEAD_DIM)
    w_x = gate_ws[1].reshape(4, n_heads, hj, HEAD_DIM)
    pairs1 = proj1.reshape(2, 2, s, e // 2)
    y1, *lru_saved = _lru_fwd(pairs1, conv_w1, conv_b1, w_a, b_a, w_x, b_x, lam, "lru_mix")
    w_out1 = arrived(4, y1, "lru_w_out")[0].reshape(1, e, d)
    o1 = _mm(y1[None], w_out1, "nn", F32, "lru_out")[0]
    dx2, do1, acc_f = _final_loss(x1, o1, gate[1], row(final_g), tgt, "final_loss")

    def reduce_stage1(tag, grads):
        lands = [lax.empty((4, g.shape[1] // 2, g.shape[2]), F32) for g in grads]
        return _exchange_start("rs_sibling_start_" + tag, grads, lands, len(grads),
                               _plan_other_half_to_sibling, None)

    def reduce_stage2(tag, stage1, nms, after):
        send, recv, grads, lands, _ = stage1
        grads, lands = _exchange_wait("rs_sibling_wait_" + tag, send, recv, grads, lands,
                                      _plan_other_half_to_sibling, after)
        parts = [_sum_own_and_sibling(core_op, g, r1, "rs_sum1_" + nm) for g, r1, nm in zip(grads, lands, nms)]
        lands = [lax.empty((3,) + p.shape[1:], BF16) for p in parts]
        return _exchange_start("rs_chips_start_" + tag, parts, lands, 3 * len(parts),
                               _plan_partials_to_chips, None)

    def reduce_stage3(tag, stage2, nms, after):
        send, recv, parts, lands, _ = stage2
        parts, lands = _exchange_wait("rs_chips_wait_" + tag, send, recv, parts, lands,
                                      _plan_partials_to_chips, after)
        halves = [_sum_chips(where_op, p, r2, "rs_sum2_" + nm) for p, r2, nm in zip(parts, lands, nms)]
        return _exchange_start("rs_share_start_" + tag, [], halves, len(halves), _plan_share_half, None)

    def reduce_done(tag, stage3, after):
        send, recv, _, fulls, _ = stage3
        return _exchange_wait("rs_share_wait_" + tag, send, recv, [], fulls, _plan_share_half, after)[1]

    def head_major_to_chip_major(t):
        return jnp.transpose(t.reshape(n_heads, 4, hj, HEAD_DIM), (1, 0, 2, 3)).reshape(4, n_heads * hj, HEAD_DIM)

    big_state = dict(zip(names, zip(shards, [m_sc_w_in, m_sc_w_out, m_lru_w_in, m_lru_w_a, m_lru_w_x, m_lru_w_out],
                                    [v_sc_w_in, v_sc_w_out, v_lru_w_in, v_lru_w_a, v_lru_w_x, v_lru_w_out],
                                    [sc_w_in, sc_w_out, lru_w_in, lru_w_a, lru_w_x, lru_w_out])))
    big = {}

    def update(nms, fulls, after):
        for nm, g2 in zip(nms, fulls):
            w2, m4, v4, w4 = big_state[nm]
            outs = _adamw(w2, g2, m4.reshape(w2.shape), v4.reshape(w2.shape), "adamw_" + nm, after=after)
            big[nm] = tuple(t.reshape(w4.shape) for t in outs)
            after = outs[1]
        return after

    g_w_out1 = _mm(y1[None], do1[None], "tn", F32, "lru_out_dw", bm=512, bn=2048)
    dy1 = _mm(do1[None], w_out1, "nt", BF16, "lru_out_dx")[0]
    dpairs1, g_wa, g_wx, g_ba, g_bx, g_lam, g_cw1, g_cb1 = _lru_bwd(
        pairs1, dy1, lru_saved, conv_w1, conv_b1, w_a, b_a, w_x, b_x, lam, "lru_mix_bwd")
    dproj1 = dpairs1.reshape(4, s, e // 2)
    g_w_in1 = _mm(h1[None], dproj1, "tn", F32, "lru_in_dw", bm=1024, bn=2048)
    lru_names = ["lru_w_out", "lru_w_a", "lru_w_x", "lru_w_in"]
    lru_rs = reduce_stage1("lru", [g_w_out1.reshape(4, es, d), head_major_to_chip_major(g_wa),
                                   head_major_to_chip_major(g_wx), g_w_in1])
    dh1 = _mm(dproj1, w_in1, "nt", F32, "lru_in_dx", after=lru_rs[4])[0]
    lru_rs = reduce_stage2("lru", lru_rs, lru_names, dh1)
    dx1, do0, acc1 = _norm_mod_bwd(dh1, x1, ng[1], scale[1], dx2, "norm1_bwd", o_prev=o0, gate_prev=gate[0],
                                   after=lru_rs[4])

    g_w_out0 = _mm(y0[None], do0[None], "tn", F32, "sc_out_dw", bm=512, bn=2048)
    out_rs = reduce_stage1("sc_out", [g_w_out0.reshape(4, es, d)])
    dy0 = _mm(do0[None], w_out0, "nt", BF16, "sc_out_dx", after=out_rs[4])[0]
    out_rs = reduce_stage2("sc_out", out_rs, ["sc_w_out"], dy0)
    dproj0, g_cw0 = _sc_bwd(proj0, dy0, conv_w0, "sc_mix_bwd", after=out_rs[4])
    stage1, stage2, after = [], [], None
    for k in range(IN_PIECES):
        g_piece = _mm(h0[None], dproj0, "tn", F32, "sc_in_dw_%d" % k, bm=1024, bn=e // IN_PIECES,
                      col_blocks=(k, 1), after=after)
        stage1.append(reduce_stage1("sc_in_%d" % k, [g_piece]))
        after = stage1[k][4]
        if k:
            stage2.append(reduce_stage2("sc_in_%d" % (k - 1), stage1[k - 1], ["sc_w_in_%d" % (k - 1)], after))
            after = stage2[k - 1][4]
    lru_rs = reduce_stage3("lru", lru_rs, lru_names, after)
    k = IN_PIECES - 1
    stage2.append(reduce_stage2("sc_in_%d" % k, stage1[k], ["sc_w_in_%d" % k], lru_rs[4]))
    dh0 = _mm(dproj0, w_in0, "nt", F32, "sc_in_dx", bn=1024, after=stage2[k][4])[0]
    grad_x, acc0 = _norm_mod_bwd(dh0, x2d, ng[0], scale[0], dx1, "norm0_bwd")
    out_rs = reduce_stage3("sc_out", out_rs, ["sc_w_out"], acc0)
    last = update(lru_names, reduce_done("lru", lru_rs, out_rs[4]), None)
    last = update(["sc_w_out"], reduce_done("sc_out", out_rs, last), None)
    g_half = None
    for k, (send, recv, parts, lands, _) in enumerate(stage2):
        parts, lands = _exchange_wait("rs_chips_wait_sc_in_%d" % k, send, recv, parts, lands,
                                      _plan_partials_to_chips, last)
        g_half = _sum_chips(where_op, parts[0], lands[0], "rs_sum2_sc_w_in_%d" % k, piece=(k, IN_PIECES),
                            so_far=g_half)
    in_rs = _exchange_start("rs_share_start_sc_in", [], [g_half], 1, _plan_share_half, None)

    dmod = jnp.stack([jnp.concatenate([acc0[1], acc0[0], acc1[3]]),
                      jnp.concatenate([acc1[1], acc1[0], acc_f[1]])])
    part_list = [jnp.stack([acc0[2], acc1[2]]), acc_f[0], acc_f[2, :1], g_cw0, g_cw1, g_cb1, g_ba, g_bx,
                 g_lam, dmod]
    partials, poffs, n_part = _pack(part_list)
    every = _allgather8([partials.reshape(8, n_part // 8)], "ag_partials", after=in_rs[4])[0].reshape(8, n_part)
    total = _sum_rows8(every, "sum_partials")[0]

    def tot(k, shape):
        size = 1
        for dim in shape:
            size *= dim
        return total[poffs[k]:poffs[k] + size].reshape(shape)

    def my_cols(t, width):
        return lax.dynamic_slice_in_dim(t, chip * width, width, axis=t.ndim - 1)

    loss = tot(2, (1,))[0]
    g_norm_g, g_final_g, g_ada_b = tot(0, (2, d)), tot(1, (d,)), tot(9, (2, 3 * d))
    g_sc_conv_w = my_cols(tot(3, (3, e)), es)[None]
    g_lru_conv_w = my_cols(tot(4, (4, e)), es)[None]
    g_lru_conv_b = my_cols(tot(5, (1, e)), es)
    g_lru_b_a = my_cols(tot(6, (n_heads, 4 * hj)), hj)[None]
    g_lru_b_x = my_cols(tot(7, (n_heads, 4 * hj)), hj)[None]
    g_lru_lambda = my_cols(tot(8, (1, e)), es)

    dmod_all = every[:, poffs[9]:poffs[9] + 6 * d].reshape(8, 2, 3 * d)
    dmod_mine = jnp.transpose(my_cols(dmod_all, f), (1, 0, 2))
    ada = _ada_bwd_adamw(jnp.transpose(c_all), dmod_mine, ada_w, m_ada_w, v_ada_w, "ada_bwd_adamw", after=total)

    small_g = [g_norm_g, g_ada_b, g_final_g, g_sc_conv_w, g_lru_conv_w, g_lru_conv_b, g_lru_b_a,
               g_lru_b_x, g_lru_lambda]
    pg = _pack(small_g)[0].reshape(small_state[0].shape)
    _, pd, pm2, pv2 = _adamw(small_state[0], pg, small_state[1], small_state[2], "adamw_small", after=ada[0])
    update(["sc_w_in"], reduce_done("sc_in", in_rs, pd), None)
    small = {}
    for k, (nm, w_) in enumerate(zip(small_names, small_w)):
        take = lambda t: t.reshape(-1)[soffs[k]:soffs[k] + w_.size].reshape(w_.shape)
        small[nm] = (small_g[k].reshape(w_.shape), take(pd), take(pm2), take(pv2))

    results = dict(small)
    results.update(big)
    results["ada_w"] = tuple(ada)
    order = ["norm_g", "ada_w", "ada_b", "sc_w_in", "sc_conv_w", "sc_w_out", "lru_w_in", "lru_conv_w",
             "lru_conv_b", "lru_w_a", "lru_b_a", "lru_w_x", "lru_b_x", "lru_lambda", "lru_w_out", "final_g"]
    out = [loss, grad_x[None]]
    for kind in range(4):
        out += [results[nm][kind] for nm in order]
    return tuple(out)
```

```python
import jax
import jax.numpy as jnp
from jax import lax
from jax.experimental import pallas as pl
from jax.experimental.pallas import tpu as pltpu

F32 = jnp.float32
BF16 = jnp.bfloat16
MESH = pl.DeviceIdType.MESH
ANY = pl.BlockSpec(memory_space=pl.ANY)

RMS_EPS = 1e-6
RGLRU_C = 8.0
HEAD_DIM = 256
ADAM_LR = 0.001
ADAM_B1 = 0.9
ADAM_B2 = 0.999
ADAM_EPS = 1e-08
ADAM_WD = 0.01
ADAM_STEP = 10
V7X_VMEM_LIMIT = 56 * 1024 * 1024
IN_PIECES = 2
LANES = 128
SUBLANES = 8
PACK = SUBLANES * LANES


def _blk(dim, pref, unit=LANES):
    if dim <= pref:
        return dim
    b = (pref // unit) * unit
    while b > unit and dim % b:
        b -= unit
    assert dim % b == 0, (dim, pref, unit)
    return b


def _params(sem=None):
    return pltpu.CompilerParams(dimension_semantics=sem, vmem_limit_bytes=V7X_VMEM_LIMIT)


def _ordered(body, n_in, after):
    if after is None:
        return body, [], []

    def ordered_body(*refs):
        return body(*refs[:n_in], *refs[n_in + 1:])

    return ordered_body, [ANY], [after]


def _pos():
    return lax.axis_index("x"), lax.axis_index("y"), lax.axis_index("c")


def _other_chips(x, y):
    return [(1 - x, y), (x, 1 - y), (1 - x, 1 - y)]


def _allgather8(arrs, name, after=None):
    n_t = len(arrs)
    ms = [a.shape[0] for a in arrs]

    def gather(*refs):
        ins, outs = refs[:n_t], refs[n_t:2 * n_t]
        send_sems, recv_sems, local_sems = refs[2 * n_t:]
        x, y, c = _pos()
        me, sibling = (x, y, c), (x, y, 1 - c)
        chips = _other_chips(x, y)

        def rows(t, px, py, pc):
            return outs[t].at[pl.ds((4 * px + 2 * py + pc) * ms[t], ms[t])]

        def copy(t, k, block, to, src=None):
            return pltpu.make_async_remote_copy(
                src_ref=rows(t, *block) if src is None else src, dst_ref=rows(t, *block),
                send_sem=send_sems.at[7 * t + k], recv_sem=recv_sems.at[7 * t + k],
                device_id=to, device_id_type=MESH)

        mine, first, passed = [], [], []
        for t in range(n_t):
            src = ins[t]
            cp = pltpu.make_async_copy(src, rows(t, *me), local_sems.at[t])
            cp.start()
            mine.append(cp)
            sends = [copy(t, 0, me, sibling, src=src)]
            sends += [copy(t, 1 + j, me, (*chip, c), src=src) for j, chip in enumerate(chips)]
            for cp in sends:
                cp.start()
            first += sends
        for t in range(n_t):
            for j, chip in enumerate(chips):
                copy(t, 1 + j, (*chip, c), me).wait_recv()
                cp = copy(t, 4 + j, (*chip, c), sibling)
                cp.start()
                passed.append(cp)
        for t in range(n_t):
            copy(t, 0, sibling, me).wait_recv()
            for j, chip in enumerate(chips):
                copy(t, 4 + j, (*chip, 1 - c), me).wait_recv()
        for cp in first + passed:
            cp.wait_send()
        for cp in mine:
            cp.wait()

    body, extra_specs, extra = _ordered(gather, n_t, after)
    return pl.pallas_call(
        body, name=name,
        out_shape=[jax.ShapeDtypeStruct((8 * m, a.shape[1]), a.dtype) for m, a in zip(ms, arrs)],
        in_specs=[ANY] * n_t + extra_specs, out_specs=[ANY] * n_t,
        scratch_shapes=[pltpu.SemaphoreType.DMA((7 * n_t,)), pltpu.SemaphoreType.DMA((7 * n_t,)),
                        pltpu.SemaphoreType.DMA((n_t,))],
    )(*arrs, *extra)


HBM = pl.BlockSpec(memory_space=pltpu.HBM)
SEM = pl.BlockSpec(memory_space=pltpu.SEMAPHORE)
TOKEN = pl.BlockSpec(memory_space=pltpu.VMEM)
IN_FLIGHT = pltpu.CompilerParams(has_side_effects=pltpu.SideEffectType.DATAFLOW_SIDE_EFFECTING)


def _in_hbm(arrs):
    return [pltpu.with_memory_space_constraint(a, pltpu.HBM) for a in arrs]


def _shard_rows(ref, h, px, py, pc):
    return ref.at[pl.ds((4 * px + 2 * py + pc) * h, h)]


def _gather_start(groups, after, name):
    bufs = [b for grp in groups for b in grp]
    n_t, n_g = len(bufs), len(groups)

    def body(*refs):
        sems, thru = refs[n_t + 1:n_t + 1 + 2 * n_g], refs[n_t + 1 + 2 * n_g:2 * n_t + 1 + 2 * n_g]
        token = refs[-1]
        x, y, c = _pos()
        t = 0
        for g, grp in enumerate(groups):
            for i in range(len(grp)):
                h = bufs[t].shape[0] // 8
                rows = _shard_rows(thru[t], h, x, y, c)
                for j, chip in enumerate(_other_chips(x, y)):
                    pltpu.make_async_remote_copy(
                        src_ref=rows, dst_ref=rows, send_sem=sems[2 * g].at[3 * i + j],
                        recv_sem=sems[2 * g + 1].at[3 * i + j], device_id=(*chip, c),
                        device_id_type=MESH).start()
                t += 1
        token[...] = jnp.zeros_like(token)

    sem_shapes = []
    for grp in groups:
        sem_shapes += [pltpu.SemaphoreType.DMA((3 * len(grp),))] * 2
    out = pl.pallas_call(
        body, name=name,
        out_shape=sem_shapes + [pltpu.HBM(b.shape, b.dtype) for b in bufs] + [jax.ShapeDtypeStruct((8, LANES), F32)],
        in_specs=[HBM] * n_t + [ANY], out_specs=[SEM] * (2 * n_g) + [HBM] * n_t + [TOKEN],
        input_output_aliases={t: 2 * n_g + t for t in range(n_t)},
        compiler_params=IN_FLIGHT,
    )(*_in_hbm(bufs), after)
    sems, thru, token = out[:2 * n_g], out[2 * n_g:2 * n_g + n_t], out[-1]
    per_group, t = [], 0
    for g, grp in enumerate(groups):
        per_group.append((sems[2 * g], sems[2 * g + 1], thru[t:t + len(grp)]))
        t += len(grp)
    return per_group, token


def _gather_forward(send_sems, recv_sems, bufs, after, name, sources=(0, 1, 2)):
    n_t = len(bufs)

    def body(*refs):
        ins = refs[:n_t]
        send1, recv1 = refs[n_t], refs[n_t + 1]
        send2, recv2 = refs[n_t + 3], refs[n_t + 4]
        token = refs[-1]
        x, y, c = _pos()
        chips = _other_chips(x, y)
        for t in range(n_t):
            h = bufs[t].shape[0] // 8
            mine = _shard_rows(ins[t], h, x, y, c)
            for j in sources:
                chip = chips[j]
                landed = _shard_rows(ins[t], h, *chip, c)
                pltpu.make_async_remote_copy(
                    src_ref=mine, dst_ref=landed, send_sem=send1.at[3 * t + j], recv_sem=recv1.at[3 * t + j],
                    device_id=(*chip, c), device_id_type=MESH).wait_recv()
                pltpu.make_async_remote_copy(
                    src_ref=landed, dst_ref=landed, send_sem=send2.at[3 * t + j], recv_sem=recv2.at[3 * t + j],
                    device_id=(x, y, 1 - c), device_id_type=MESH).start()
        for t in range(n_t):
            h = bufs[t].shape[0] // 8
            mine = _shard_rows(ins[t], h, x, y, c)
            for j in sources:
                pltpu.make_async_remote_copy(
                    src_ref=mine, dst_ref=mine, send_sem=send1.at[3 * t + j], recv_sem=recv1.at[3 * t + j],
                    device_id=(*chips[j], c), device_id_type=MESH).wait_send()
        token[...] = jnp.zeros_like(token)

    out = pl.pallas_call(
        body, name=name,
        out_shape=[pltpu.SemaphoreType.DMA((3 * n_t,))] * 2 + [pltpu.HBM(b.shape, b.dtype) for b in bufs]
        + [jax.ShapeDtypeStruct((8, LANES), F32)],
        in_specs=[HBM] * n_t + [SEM, SEM, ANY], out_specs=[SEM, SEM] + [HBM] * n_t + [TOKEN],
        input_output_aliases={t: 2 + t for t in range(n_t)},
        compiler_params=IN_FLIGHT,
    )(*bufs, send_sems, recv_sems, after)
    return out[0], out[1], out[2:2 + n_t], out[-1]


def _gather_finish(send_sems, recv_sems, bufs, after, name, sources=(0, 1, 2)):
    n_t = len(bufs)

    def body(*refs):
        ins = refs[:n_t]
        send2, recv2 = refs[n_t], refs[n_t + 1]
        x, y, c = _pos()
        chips = _other_chips(x, y)
        for t in range(n_t):
            h = bufs[t].shape[0] // 8
            for j in sources:
                chip = chips[j]
                sent = _shard_rows(ins[t], h, *chip, c)
                got = _shard_rows(ins[t], h, *chip, 1 - c)
                cp = pltpu.make_async_remote_copy(
                    src_ref=sent, dst_ref=got, send_sem=send2.at[3 * t + j], recv_sem=recv2.at[3 * t + j],
                    device_id=(x, y, 1 - c), device_id_type=MESH)
                cp.wait_send()
                cp.wait_recv()

    return pl.pallas_call(
        body, name=name, out_shape=[pltpu.HBM(b.shape, b.dtype) for b in bufs],
        in_specs=[HBM] * n_t + [SEM, SEM, ANY], out_specs=[HBM] * n_t,
        input_output_aliases={t: t for t in range(n_t)},
        compiler_params=IN_FLIGHT,
    )(*bufs, send_sems, recv_sems, after)


def _part_rows(ref, h, px, py, pc, which):
    return ref.at[pl.ds((4 * px + 2 * py + pc) * h + which * (h // 2), h // 2)]


def _remote(src, dst, send_sem, recv_sem, device):
    return pltpu.make_async_remote_copy(src_ref=src, dst_ref=dst, send_sem=send_sem, recv_sem=recv_sem,
                                        device_id=device, device_id_type=MESH)


def _first_start(buf, after, name):
    h = buf.shape[0] // 8

    def body(buf_in, after_ref, send, recv, thru, token):
        x, y, c = _pos()
        rows = _shard_rows(thru, h, x, y, c)
        for j, chip in enumerate(_other_chips(x, y)[:2]):
            _remote(rows, rows, send.at[j], recv.at[j], (*chip, c)).start()
        token[...] = jnp.zeros_like(token)

    return pl.pallas_call(
        body, name=name,
        out_shape=[pltpu.SemaphoreType.DMA((2,))] * 2 + [pltpu.HBM(buf.shape, buf.dtype),
                                                         jax.ShapeDtypeStruct((8, LANES), F32)],
        in_specs=[HBM, ANY], out_specs=[SEM, SEM, HBM, TOKEN], input_output_aliases={0: 2},
        compiler_params=IN_FLIGHT,
    )(*_in_hbm([buf]), after)


def _first_relay(send_a, recv_a, buf, j, after, name):
    h = buf.shape[0] // 8

    def body(buf_in, send_a, recv_a, *rest):
        send_b, recv_b, send_c, recv_c, _, token = rest[len(after):]
        x, y, c = _pos()
        chips = _other_chips(x, y)
        nbr, other = chips[j], chips[1 - j]
        mine, landed = _shard_rows(buf_in, h, x, y, c), _shard_rows(buf_in, h, *nbr, c)
        _remote(mine, landed, send_a.at[j], recv_a.at[j], (*nbr, c)).wait_recv()
        _remote(landed, landed, send_b.at[0], recv_b.at[0], (x, y, 1 - c)).start()
        part = _part_rows(buf_in, h, *nbr, c, j)
        _remote(part, part, send_c.at[0], recv_c.at[0], (*other, c)).start()
        _remote(mine, mine, send_a.at[j], recv_a.at[j], (*nbr, c)).wait_send()
        token[...] = jnp.zeros_like(token)

    out = pl.pallas_call(
        body, name=name,
        out_shape=[pltpu.SemaphoreType.DMA((1,))] * 4 + [pltpu.HBM(buf.shape, buf.dtype),
                                                         jax.ShapeDtypeStruct((8, LANES), F32)],
        in_specs=[HBM, SEM, SEM] + [ANY] * len(after), out_specs=[SEM] * 4 + [HBM, TOKEN],
        input_output_aliases={0: 4}, compiler_params=IN_FLIGHT,
    )(buf, send_a, recv_a, *after)
    return tuple(out)


def _first_relay_done(send_b, recv_b, buf, j, after, name):
    h = buf.shape[0] // 8

    def body(buf_in, send_b, recv_b, after_ref, thru):
        x, y, c = _pos()
        nbr = _other_chips(x, y)[j]
        cp = _remote(_shard_rows(buf_in, h, *nbr, c), _shard_rows(buf_in, h, *nbr, 1 - c),
                     send_b.at[0], recv_b.at[0], (x, y, 1 - c))
        cp.wait_send()
        cp.wait_recv()

    return pl.pallas_call(
        body, name=name, out_shape=pltpu.HBM(buf.shape, buf.dtype),
        in_specs=[HBM, SEM, SEM, ANY], out_specs=HBM, input_output_aliases={0: 0},
        compiler_params=IN_FLIGHT,
    )(buf, send_b, recv_b, after)


def _first_diagonal(relays, buf, after, name):
    h = buf.shape[0] // 8

    def body(buf_in, send_c0, recv_c0, send_c1, recv_c1, after_ref, send_b, recv_b, thru, token):
        x, y, c = _pos()
        chips = _other_chips(x, y)
        diag = chips[2]
        for j, (send_c, recv_c) in enumerate(((send_c0, recv_c0), (send_c1, recv_c1))):
            part = _part_rows(buf_in, h, *diag, c, j)
            _remote(part, part, send_c.at[0], recv_c.at[0], (*chips[1 - j], c)).wait_recv()
        whole = _shard_rows(buf_in, h, *diag, c)
        _remote(whole, whole, send_b.at[0], recv_b.at[0], (x, y, 1 - c)).start()
        token[...] = jnp.zeros_like(token)

    out = pl.pallas_call(
        body, name=name,
        out_shape=[pltpu.SemaphoreType.DMA((1,))] * 2 + [pltpu.HBM(buf.shape, buf.dtype),
                                                         jax.ShapeDtypeStruct((8, LANES), F32)],
        in_specs=[HBM] + [SEM] * 4 + [ANY], out_specs=[SEM, SEM, HBM, TOKEN], input_output_aliases={0: 2},
        compiler_params=IN_FLIGHT,
    )(buf, relays[0][0], relays[0][1], relays[1][0], relays[1][1], after)
    return tuple(out)


def _first_diagonal_done(send_b, recv_b, relays, buf, after, name):
    h = buf.shape[0] // 8

    def body(buf_in, send_b, recv_b, send_c0, recv_c0, send_c1, recv_c1, after_ref, thru):
        x, y, c = _pos()
        chips = _other_chips(x, y)
        diag = chips[2]
        cp = _remote(_shard_rows(buf_in, h, *diag, c), _shard_rows(buf_in, h, *diag, 1 - c),
                     send_b.at[0], recv_b.at[0], (x, y, 1 - c))
        cp.wait_send()
        cp.wait_recv()
        for j, (send_c, recv_c) in enumerate(((send_c0, recv_c0), (send_c1, recv_c1))):
            part = _part_rows(buf_in, h, *chips[j], c, j)
            _remote(part, part, send_c.at[0], recv_c.at[0], (*chips[1 - j], c)).wait_send()

    return pl.pallas_call(
        body, name=name, out_shape=pltpu.HBM(buf.shape, buf.dtype),
        in_specs=[HBM] + [SEM] * 6 + [ANY], out_specs=HBM, input_output_aliases={0: 0},
        compiler_params=IN_FLIGHT,
    )(buf, send_b, recv_b, relays[0][0], relays[0][1], relays[1][0], relays[1][1], after)


def _exchange_start(name, srcs, lands, n_copies, plan, after):
    ns, nl = len(srcs), len(lands)
    extra = [] if after is None else [after]

    def body(*refs):
        base = ns + nl + len(extra)
        send_sems, recv_sems = refs[base], refs[base + 1]
        src_refs, land_refs = refs[base + 2:base + 2 + ns], refs[base + 2 + ns:base + 2 + ns + nl]
        token = refs[-1]
        x, y, c = _pos()
        copies = plan(src_refs, land_refs, x, y, c)
        assert len(copies) == n_copies
        for k, (src, dst, dev) in enumerate(copies):
            pltpu.make_async_remote_copy(
                src_ref=src, dst_ref=dst, send_sem=send_sems.at[k], recv_sem=recv_sems.at[k],
                device_id=dev, device_id_type=MESH).start()
        token[...] = jnp.zeros_like(token)

    out = pl.pallas_call(
        body, name=name,
        out_shape=[pltpu.SemaphoreType.DMA((n_copies,))] * 2
        + [pltpu.HBM(a.shape, a.dtype) for a in list(srcs) + list(lands)] + [jax.ShapeDtypeStruct((8, LANES), F32)],
        in_specs=[HBM] * (ns + nl) + [ANY] * len(extra), out_specs=[SEM, SEM] + [HBM] * (ns + nl) + [TOKEN],
        input_output_aliases={i: 2 + i for i in range(ns + nl)},
        compiler_params=IN_FLIGHT,
    )(*_in_hbm(list(srcs) + list(lands)), *extra)
    return out[0], out[1], out[2:2 + ns], out[2 + ns:2 + ns + nl], out[-1]


def _exchange_wait(name, send_sems, recv_sems, srcs, lands, plan, after):
    ns, nl = len(srcs), len(lands)

    def body(*refs):
        src_refs, land_refs = refs[:ns], refs[ns:ns + nl]
        send, recv = refs[ns + nl], refs[ns + nl + 1]
        x, y, c = _pos()
        for k, (src, dst, dev) in enumerate(plan(src_refs, land_refs, x, y, c)):
            cp = pltpu.make_async_remote_copy(
                src_ref=src, dst_ref=dst, send_sem=send.at[k], recv_sem=recv.at[k],
                device_id=dev, device_id_type=MESH)
            cp.wait_send()
            cp.wait_recv()

    out = pl.pallas_call(
        body, name=name, out_shape=[pltpu.HBM(a.shape, a.dtype) for a in list(srcs) + list(lands)],
        in_specs=[HBM] * (ns + nl) + [SEM, SEM, ANY], out_specs=[HBM] * (ns + nl),
        input_output_aliases={i: i for i in range(ns + nl)},
        compiler_params=IN_FLIGHT,
    )(*srcs, *lands, send_sems, recv_sems, after)
    return out[:ns], out[ns:]


def _plan_other_half_to_sibling(src_refs, land_refs, x, y, c):
    out = []
    for g_ref, r_ref in zip(src_refs, land_refs):
        h = g_ref.shape[1] // 2
        out.append((g_ref.at[:, pl.ds((1 - c) * h, h), :], r_ref, (x, y, 1 - c)))
    return out


def _plan_partials_to_chips(src_refs, land_refs, x, y, c):
    out = []
    for p_ref, r_ref in zip(src_refs, land_refs):
        for j, (px, py) in enumerate(_other_chips(x, y)):
            out.append((p_ref.at[2 * px + py], r_ref.at[j], (px, py, c)))
    return out


def _plan_share_half(src_refs, land_refs, x, y, c):
    out = []
    for f_ref in land_refs:
        h = f_ref.shape[0] // 2
        rows = f_ref.at[pl.ds(c * h, h)]
        out.append((rows, rows, (x, y, 1 - c)))
    return out


def _cast_into_slot(where, w, name, half=False, so_far=None):
    r, c = w.shape
    br, bc = _blk(r // 2 if half else r, 512, 8), _blk(c, 2048)
    nb = r // br
    n_steps = nb // 2 if half else nb

    def body(where_ref, w_ref, *rest):
        rest[-1][...] = w_ref[...].astype(BF16)

    first = (lambda where_ref: where_ref[1] * n_steps) if half else (lambda where_ref: 0)
    extra_specs, extra, aliases = ([], [], {}) if so_far is None else ([ANY], [so_far], {2: 0})
    grid_spec = pltpu.PrefetchScalarGridSpec(
        num_scalar_prefetch=1, grid=(n_steps, c // bc),
        in_specs=[pl.BlockSpec((br, bc), lambda i, j, where_ref: (first(where_ref) + i, j))] + extra_specs,
        out_specs=pl.BlockSpec((br, bc), lambda i, j, where_ref: (where_ref[0] * nb + first(where_ref) + i, j)))
    return pl.pallas_call(
        body, name=name, grid_spec=grid_spec,
        out_shape=jax.ShapeDtypeStruct((4 * r, c), BF16), input_output_aliases=aliases,
        compiler_params=_params(("parallel", "parallel")),
    )(where, w, *extra)


def _sum_own_and_sibling(core, g, r1, name):
    _, r, c = g.shape
    h = r // 2
    br, bc = _blk(h, 512, 8), _blk(c, 2048)
    nb = h // br

    def body(core_ref, g_ref, r_ref, o_ref):
        o_ref[...] = (g_ref[...] + r_ref[...]).astype(BF16)

    grid_spec = pltpu.PrefetchScalarGridSpec(
        num_scalar_prefetch=1, grid=(4, nb, c // bc),
        in_specs=[pl.BlockSpec((None, br, bc), lambda k, i, j, core_ref: (k, core_ref[0] * nb + i, j)),
                  pl.BlockSpec((None, br, bc), lambda k, i, j, core_ref: (k, i, j))],
        out_specs=pl.BlockSpec((None, br, bc), lambda k, i, j, core_ref: (k, i, j)))
    return pl.pallas_call(
        body, name=name, grid_spec=grid_spec,
        out_shape=jax.ShapeDtypeStruct((4, h, c), BF16),
        compiler_params=_params(("parallel", "parallel", "parallel")),
    )(core, g, r1)


def _sum_chips(where, p, r2, name, piece=(0, 1), so_far=None):
    _, h, c = p.shape
    k, n = piece
    br, bc = _blk(h, 512, 8), _blk(c, 2048)
    nb, ncb = h // br, c // bc

    def body(where_ref, p_ref, r_ref, *rest):
        acc = p_ref[...].astype(F32)
        for j in range(3):
            acc = acc + r_ref[j].astype(F32)
        rest[-1][...] = acc

    extra_specs, extra, aliases = ([], [], {}) if so_far is None else ([ANY], [so_far], {3: 0})
    grid_spec = pltpu.PrefetchScalarGridSpec(
        num_scalar_prefetch=1, grid=(nb, ncb),
        in_specs=[pl.BlockSpec((None, br, bc), lambda i, j, where_ref: (where_ref[0], i, j)),
                  pl.BlockSpec((3, br, bc), lambda i, j, where_ref: (0, i, j))] + extra_specs,
        out_specs=pl.BlockSpec((br, bc), lambda i, j, where_ref: (where_ref[1] * nb + i, k * ncb + j)))
    return pl.pallas_call(
        body, name=name, grid_spec=grid_spec,
        out_shape=jax.ShapeDtypeStruct((2 * h, n * c), F32), input_output_aliases=aliases,
        compiler_params=_params(("parallel", "parallel")),
    )(where, p, r2, *extra)


def _adamw_math(w, g, m, v):
    m2 = ADAM_B1 * m + (1.0 - ADAM_B1) * g
    v2 = ADAM_B2 * v + (1.0 - ADAM_B2) * (g * g)
    m_hat = m2 / (1.0 - ADAM_B1 ** ADAM_STEP)
    v_hat = v2 / (1.0 - ADAM_B2 ** ADAM_STEP)
    delta = -ADAM_LR * (m_hat / (jnp.sqrt(v_hat) + ADAM_EPS) + ADAM_WD * w)
    return delta, m2, v2


def _adamw(w, g, m, v, name, after=None):
    r, c = w.shape
    br, bc = _blk(r, 256, 8), _blk(c, 2048)

    def body(w_ref, g_ref, m_ref, v_ref, go_ref, d_ref, m2_ref, v2_ref):
        gv = g_ref[...]
        d, m2, v2 = _adamw_math(w_ref[...], gv, m_ref[...], v_ref[...])
        go_ref[...] = gv
        d_ref[...] = d
        m2_ref[...] = m2
        v2_ref[...] = v2

    spec = pl.BlockSpec((br, bc), lambda i, j: (i, j))
    body, extra_specs, extra = _ordered(body, 4, after)
    return pl.pallas_call(
        body, name=name, grid=(r // br, c // bc),
        in_specs=[spec] * 4 + extra_specs, out_specs=[spec] * 4,
        out_shape=[jax.ShapeDtypeStruct((r, c), F32)] * 4,
        compiler_params=_params(("parallel", "parallel")),
    )(w, g, m, v, *extra)


def _sum_rows8(g, name, after=None):
    n = g.shape[1]

    def body(g_ref, o_ref):
        acc = g_ref[0:1, :]
        for k in range(1, 8):
            acc = acc + g_ref[k:k + 1, :]
        o_ref[...] = acc

    body, extra_specs, extra = _ordered(body, 1, after)
    return pl.pallas_call(
        body, name=name, out_shape=jax.ShapeDtypeStruct((1, n), F32),
        in_specs=[pl.BlockSpec(memory_space=pltpu.VMEM)] + extra_specs,
        out_specs=pl.BlockSpec(memory_space=pltpu.VMEM),
        compiler_params=_params(),
    )(g, *extra)


def _mm(a, b, mode, out_dtype, name, bm=1024, bn=None, after=None, col_blocks=None):
    if mode == "nn":
        (_, m, k), (g, _, n) = a.shape, b.shape
    elif mode == "tn":
        (_, k, m), (g, _, n) = a.shape, b.shape
    else:
        (g, m, k), (_, n, _) = a.shape, b.shape
    if bn is None:
        bn = 1024 if k <= 2048 else 512
    bm, bn = _blk(m, bm), _blk(n, bn)

    if mode == "nt":
        def body(a_ref, b_ref, o_ref, acc_ref):
            part = lax.dot_general(a_ref[...], b_ref[...], (((1,), (1,)), ((), ())),
                                   preferred_element_type=F32)
            if g == 1:
                o_ref[...] = part.astype(out_dtype)
            else:
                gi = pl.program_id(2)

                @pl.when(gi == 0)
                def _():
                    acc_ref[...] = part

                @pl.when(gi > 0)
                def _():
                    acc_ref[...] += part

                @pl.when(gi == g - 1)
                def _():
                    o_ref[...] = acc_ref[...].astype(out_dtype)

        body, extra_specs, extra = _ordered(body, 2, after)
        return pl.pallas_call(
            body, name=name, grid=(m // bm, n // bn, g),
            in_specs=[pl.BlockSpec((None, bm, k), lambda i, j, gi: (gi, i, 0)),
                      pl.BlockSpec((None, bn, k), lambda i, j, gi: (gi, j, 0))] + extra_specs,
            out_specs=pl.BlockSpec((None, bm, bn), lambda i, j, gi: (0, i, j)),
            out_shape=jax.ShapeDtypeStruct((1, m, n), out_dtype),
            scratch_shapes=[pltpu.VMEM((bm, bn), F32)],
            compiler_params=_params(("parallel", "parallel", "arbitrary")),
        )(a, b, *extra)

    contract = (((1,), (0,)), ((), ())) if mode == "nn" else (((0,), (0,)), ((), ()))

    def body(a_ref, b_ref, o_ref):
        o_ref[...] = lax.dot_general(a_ref[...], b_ref[...], contract,
                                     preferred_element_type=F32).astype(out_dtype)

    a_spec = (pl.BlockSpec((None, bm, k), lambda i, gi, j: (0, i, 0)) if mode == "nn"
              else pl.BlockSpec((None, k, bm), lambda i, gi, j: (0, 0, i)))
    first, count = (0, n // bn) if col_blocks is None else col_blocks
    body, extra_specs, extra = _ordered(body, 2, after)
    return pl.pallas_call(
        body, name=name, grid=(m // bm, g, count),
        in_specs=[a_spec, pl.BlockSpec((None, k, bn), lambda i, gi, j: (gi, 0, first + j))] + extra_specs,
        out_specs=pl.BlockSpec((None, bm, bn), lambda i, gi, j: (gi, i, j)),
        out_shape=jax.ShapeDtypeStruct((g, m, count * bn), out_dtype),
        compiler_params=_params(("parallel", "parallel", "parallel")),
    )(a, b, *extra)


def _mm_plane(a, b, plane, name, planes_so_far=None, after=None, bm=1024, bn=1024):
    (m, k), (g, _, n) = a.shape, b.shape
    bm, bn = _blk(m, bm), _blk(n, bn)

    def body(plane_ref, a_ref, b_ref, *rest):
        rest[-1][...] = jnp.dot(a_ref[...], b_ref[...], preferred_element_type=F32).astype(BF16)

    extra_specs, extra, aliases = [], [], {}
    if planes_so_far is not None:
        extra_specs.append(ANY)
        extra.append(planes_so_far)
        aliases = {3: 0}
    if after is not None:
        extra_specs.append(ANY)
        extra.append(after)
    grid_spec = pltpu.PrefetchScalarGridSpec(
        num_scalar_prefetch=1, grid=(m // bm, n // bn),
        in_specs=[pl.BlockSpec((bm, k), lambda i, j, p: (i, 0)),
                  pl.BlockSpec((None, k, bn), lambda i, j, p: (p[0], 0, j))] + extra_specs,
        out_specs=pl.BlockSpec((None, bm, bn), lambda i, j, p: (p[0], i, j)))
    return pl.pallas_call(
        body, name=name, grid_spec=grid_spec,
        out_shape=jax.ShapeDtypeStruct((g, m, n), BF16), input_output_aliases=aliases,
        compiler_params=_params(("parallel", "parallel")),
    )(plane, a, b, *extra)


def _row_specs(br, d):
    return (pl.BlockSpec((br, d), lambda i: (i, 0)), pl.BlockSpec((1, d), lambda i: (0, 0)),
            pl.BlockSpec((8, d), lambda i: (0, 0)))


def _rstd(xv):
    return lax.rsqrt(jnp.mean(xv * xv, axis=-1, keepdims=True) + RMS_EPS)


def _colsum(v):
    return jnp.sum(v, axis=0, keepdims=True)


def _norm_mod_fwd(x, g, scale, shift, name, o=None, gate=None, after=None):
    s, d = x.shape
    br = _blk(s, 256, 8)
    has_res = o is not None
    row, vec, _ = _row_specs(br, d)

    def body(*refs):
        if has_res:
            x_ref, o_ref, gate_ref, g_ref, sc_ref, sh_ref, x1_ref, h_ref = refs
            xv = x_ref[...] + gate_ref[...] * o_ref[...]
            x1_ref[...] = xv
        else:
            x_ref, g_ref, sc_ref, sh_ref, h_ref = refs
            xv = x_ref[...]
        n = xv * _rstd(xv) * g_ref[...]
        h_ref[...] = (n * (1.0 + sc_ref[...]) + sh_ref[...]).astype(BF16)

    ins = [x] + ([o, gate] if has_res else []) + [g, scale, shift]
    in_specs = [row] + ([row, vec] if has_res else []) + [vec] * 3
    out_shape = ([jax.ShapeDtypeStruct((s, d), F32)] if has_res else []) + [jax.ShapeDtypeStruct((s, d), BF16)]
    body, extra_specs, extra = _ordered(body, len(ins), after)
    out = pl.pallas_call(
        body, name=name, grid=(s // br,), in_specs=in_specs + extra_specs, out_specs=[row] * len(out_shape),
        out_shape=out_shape, compiler_params=_params(("parallel",)),
    )(*ins, *extra)
    return out if has_res else out[0]


def _final_loss(x1, o1, gate1, final_g, tgt, name):
    s, d = x1.shape
    br = _blk(s, 256, 8)
    row, vec, acc = _row_specs(br, d)

    def body(x1_ref, o_ref, gate_ref, g_ref, t_ref, dx_ref, do_ref, acc_ref):
        @pl.when(pl.program_id(0) == 0)
        def _():
            acc_ref[...] = jnp.zeros_like(acc_ref)

        gate, o, g = gate_ref[...], o_ref[...], g_ref[...]
        x2 = x1_ref[...] + gate * o
        r = _rstd(x2)
        xh = x2 * r
        err = xh * g - t_ref[...]
        loss = 0.5 * _colsum(jnp.mean(err * err, axis=-1, keepdims=True))
        dout = err * (1.0 / d)
        dxh = dout * g
        dx2 = r * (dxh - xh * jnp.mean(dxh * xh, axis=-1, keepdims=True))
        dx_ref[...] = dx2
        do_ref[...] = (dx2 * gate).astype(BF16)
        acc_ref[0:1, :] += _colsum(dout * xh)
        acc_ref[1:2, :] += _colsum(dx2 * o)
        acc_ref[2:3, :] += jnp.broadcast_to(loss, (1, d))

    return pl.pallas_call(
        body, name=name, grid=(s // br,),
        in_specs=[row, row, vec, vec, row], out_specs=[row, row, acc],
        out_shape=[jax.ShapeDtypeStruct((s, d), F32), jax.ShapeDtypeStruct((s, d), BF16),
                   jax.ShapeDtypeStruct((8, d), F32)],
        compiler_params=_params(("arbitrary",)),
    )(x1, o1, gate1, final_g, tgt)


def _norm_mod_bwd(dh, x, g, scale, dx_next, name, o_prev=None, gate_prev=None, after=None):
    s, d = x.shape
    br = _blk(s, 256, 8)
    has_prev = o_prev is not None
    row, vec, acc = _row_specs(br, d)

    def body(*refs):
        if has_prev:
            dh_ref, x_ref, g_ref, sc_ref, dxn_ref, o_ref, gate_ref, dx_ref, do_ref, acc_ref = refs
        else:
            dh_ref, x_ref, g_ref, sc_ref, dxn_ref, dx_ref, acc_ref = refs

        @pl.when(pl.program_id(0) == 0)
        def _():
            acc_ref[...] = jnp.zeros_like(acc_ref)

        xv, gv, dhv = x_ref[...], g_ref[...], dh_ref[...]
        r = _rstd(xv)
        xh = xv * r
        acc_ref[0:1, :] += _colsum(dhv * (xh * gv))
        acc_ref[1:2, :] += _colsum(dhv)
        dn = dhv * (1.0 + sc_ref[...])
        acc_ref[2:3, :] += _colsum(dn * xh)
        dxh = dn * gv
        dx = dxn_ref[...] + r * (dxh - xh * jnp.mean(dxh * xh, axis=-1, keepdims=True))
        dx_ref[...] = dx
        if has_prev:
            acc_ref[3:4, :] += _colsum(dx * o_ref[...])
            do_ref[...] = (dx * gate_ref[...]).astype(BF16)

    ins = [dh, x, g, scale, dx_next] + ([o_prev, gate_prev] if has_prev else [])
    in_specs = [row, row, vec, vec, row] + ([row, vec] if has_prev else [])
    out_shape = [jax.ShapeDtypeStruct((s, d), F32)]
    out_specs = [row]
    if has_prev:
        out_shape.append(jax.ShapeDtypeStruct((s, d), BF16))
        out_specs.append(row)
    out_shape.append(jax.ShapeDtypeStruct((8, d), F32))
    out_specs.append(acc)
    body, extra_specs, extra = _ordered(body, len(ins), after)
    return pl.pallas_call(
        body, name=name, grid=(s // br,), in_specs=in_specs + extra_specs, out_specs=out_specs,
        out_shape=out_shape, compiler_params=_params(("arbitrary",)),
    )(*ins, *extra)


def _tiles(p):
    s, c = p.shape
    return p.reshape(s // SUBLANES, SUBLANES, c)


def _shift_down(p, k):
    if k == 0:
        return p
    r = pltpu.roll(_tiles(p), k, 1)
    before = jnp.concatenate([jnp.zeros_like(r[:1]), r[:-1]], axis=0)
    rows = lax.broadcasted_iota(jnp.int32, r.shape, 1)
    return jnp.where(rows >= k, r, before).reshape(p.shape)


def _shift_up(p, k):
    if k == 0:
        return p
    r = pltpu.roll(_tiles(p), SUBLANES - k, 1)
    after = jnp.concatenate([r[1:], jnp.zeros_like(r[:1])], axis=0)
    rows = lax.broadcasted_iota(jnp.int32, r.shape, 1)
    return jnp.where(rows < SUBLANES - k, r, after).reshape(p.shape)


def _sigmoid(z):
    return 0.5 * (jnp.tanh(0.5 * z) + 1.0)


def _sc_parts(proj_ref, w_ref):
    b, cg, v, g = (proj_ref[i].astype(F32) for i in range(4))
    p = cg * v
    u = w_ref[2:3, :] * p + w_ref[1:2, :] * _shift_down(p, 1) + w_ref[0:1, :] * _shift_down(p, 2)
    return b, cg, v, g, p, u


def _sc_fwd(proj, conv_w, name):
    _, s, e = proj.shape
    bc = _blk(e, 256)

    def body(proj_ref, w_ref, y_ref):
        b, _, _, g, _, u = _sc_parts(proj_ref, w_ref)
        y_ref[...] = (b * u * (g * _sigmoid(g))).astype(BF16)

    return pl.pallas_call(
        body, name=name, grid=(e // bc,),
        in_specs=[pl.BlockSpec((4, s, bc), lambda j: (0, 0, j)), pl.BlockSpec((3, bc), lambda j: (0, j))],
        out_specs=pl.BlockSpec((s, bc), lambda j: (0, j)),
        out_shape=jax.ShapeDtypeStruct((s, e), BF16),
        compiler_params=_params(("parallel",)),
    )(proj, conv_w)


def _sc_bwd(proj, dy, conv_w, name, after=None):
    _, s, e = proj.shape
    bc = _blk(e, 256)

    def body(proj_ref, dy_ref, w_ref, dp_ref, dw_ref):
        b, cg, v, g, p, u = _sc_parts(proj_ref, w_ref)
        dyv = dy_ref[...].astype(F32)
        sig = _sigmoid(g)
        t = dyv * (g * sig)
        du = t * b
        dp_ref[0] = (t * u).astype(BF16)
        dp_ref[3] = (dyv * b * u * (sig * (1.0 + g * (1.0 - sig)))).astype(BF16)
        dpp = w_ref[2:3, :] * du + w_ref[1:2, :] * _shift_up(du, 1) + w_ref[0:1, :] * _shift_up(du, 2)
        dp_ref[1] = (dpp * v).astype(BF16)
        dp_ref[2] = (dpp * cg).astype(BF16)
        dw_ref[2:3, :] = _colsum(du * p)
        dw_ref[1:2, :] = _colsum(du * _shift_down(p, 1))
        dw_ref[0:1, :] = _colsum(du * _shift_down(p, 2))

    body, extra_specs, extra = _ordered(body, 3, after)
    return pl.pallas_call(
        body, name=name, grid=(e // bc,),
        in_specs=[pl.BlockSpec((4, s, bc), lambda j: (0, 0, j)), pl.BlockSpec((s, bc), lambda j: (0, j)),
                  pl.BlockSpec((3, bc), lambda j: (0, j))] + extra_specs,
        out_specs=[pl.BlockSpec((4, s, bc), lambda j: (0, 0, j)), pl.BlockSpec((3, bc), lambda j: (0, j))],
        out_shape=[jax.ShapeDtypeStruct((4, s, e), BF16), jax.ShapeDtypeStruct((3, e), F32)],
        compiler_params=_params(("parallel",)),
    )(proj, dy, conv_w, *extra)


def _softplus_neg(lam):
    u = jnp.exp(-jnp.abs(lam))
    w = 1.0 + u
    log1p = jnp.where(w == 1.0, u, jnp.log(w) * (u / jnp.where(w == 1.0, 1.0, w - 1.0)))
    return jnp.maximum(-lam, 0.0) + log1p


def _one_minus_exp(z):
    series = -z * (1.0 + z * (0.5 + z * (1.0 / 6.0 + z * (1.0 / 24.0))))
    return jnp.where(z > -0.02, series, 1.0 - jnp.exp(z))


def _scan_in_tiles(a, b, reverse):
    shape = a.shape
    a, b = _tiles(a), _tiles(b)
    rows = lax.broadcasted_iota(jnp.int32, a.shape, 1)
    for step in (1, 2, 4):
        shift = SUBLANES - step if reverse else step
        ok = rows < SUBLANES - step if reverse else rows >= step
        a_s, b_s = pltpu.roll(a, shift, 1), pltpu.roll(b, shift, 1)
        b = jnp.where(ok, a * b_s + b, b)
        a = jnp.where(ok, a * a_s, a)
    return a.reshape(shape), b.reshape(shape)


def _by_rows(fn, arrays, rows=32):
    s = arrays[0].shape[0]
    rows = min(rows, s)
    for t in range(0, s, rows):
        fn(t, *(a[t:t + rows] for a in arrays))


def _scan_carry(a_ref, b_ref, h_ref, ta_ref, tb_ref, reverse):
    s, c = a_ref.shape
    n = s // SUBLANES
    assert n % SUBLANES == 0
    last = 0 if reverse else SUBLANES - 1
    for k in range(n):
        ta_ref[k:k + 1, :] = a_ref[k * SUBLANES + last:k * SUBLANES + last + 1, :]
        tb_ref[k:k + 1, :] = b_ref[k * SUBLANES + last:k * SUBLANES + last + 1, :]
    a2, b2 = _scan_in_tiles(ta_ref[...], tb_ref[...], reverse)
    groups = range(n // SUBLANES)
    carry, leaving = jnp.zeros((1, c), F32), {}
    for g in (reversed(groups) if reverse else groups):
        rows = slice(g * SUBLANES, (g + 1) * SUBLANES)
        leaving[g] = b2[rows] + a2[rows] * carry
        carry = leaving[g][last:last + 1]
    leaving = jnp.concatenate([leaving[g] for g in groups], axis=0)
    entering = _shift_up(leaving, 1) if reverse else _shift_down(leaving, 1)
    for k in range(n):
        rows = slice(k * SUBLANES, (k + 1) * SUBLANES)
        h_ref[rows] = b_ref[rows] + a_ref[rows] * entering[k:k + 1]


def _lru_specs(s, e_half, n_heads):
    hp = e_half // HEAD_DIM
    c = HEAD_DIM
    return dict(
        pair=pl.BlockSpec((2, None, s, c), lambda h: (0, h // hp, 0, h % hp)),
        conv_w=pl.BlockSpec((4, c), lambda h: (0, h)),
        chan=pl.BlockSpec((1, c), lambda h: (0, h)),
        w=pl.BlockSpec((4, None, c // 4, c), lambda h: (0, h, 0, 0)),
        bias=pl.BlockSpec((None, 1, c), lambda h: (h, 0, 0)),
        plane=pl.BlockSpec((s, c), lambda h: (0, h)),
    )


def _lru_gate_inputs(vp, cw_ref, cb_ref, wa_ref, ba_ref, wx_ref, bx_ref):
    c = HEAD_DIM
    taps = [_shift_down(vp, 3 - k) for k in range(4)]
    v = cb_ref[...] + sum(cw_ref[k:k + 1, :] * taps[k] for k in range(4))
    vb = v.astype(BF16)
    wa = wa_ref[...].reshape(c, c)
    wx = wx_ref[...].reshape(c, c)
    zr = jnp.dot(vb, wa, preferred_element_type=F32) + ba_ref[...]
    zi = jnp.dot(vb, wx, preferred_element_type=F32) + bx_ref[...]
    return taps, v, vb, wa, wx, zr, zi


def _lru_fwd(proj, conv_w, conv_b, w_a, b_a, w_x, b_x, lam, name):
    _, _, s, e_half = proj.shape
    n_heads = 2 * e_half // HEAD_DIM
    sp_ = _lru_specs(s, e_half, n_heads)

    def body(pg_ref, cw_ref, cb_ref, wa_ref, ba_ref, wx_ref, bx_ref, lam_ref,
             y_ref, a_ref, hs_ref, sa_ref, sb_ref, sh_ref, ta_ref, tb_ref):
        _, v, _, _, _, zr, zi = _lru_gate_inputs(
            pg_ref[0].astype(F32), cw_ref, cb_ref, wa_ref, ba_ref, wx_ref, bx_ref)
        rate = (-RGLRU_C) * _softplus_neg(lam_ref[...])

        def decay_and_input(t, v_c, zr_c, zi_c):
            la = rate * _sigmoid(zr_c)
            a = jnp.exp(la)
            b = jnp.sqrt(_one_minus_exp(2.0 * la)) * (_sigmoid(zi_c) * v_c)
            a_ref[t:t + a.shape[0]] = a
            sa_ref[t:t + a.shape[0]], sb_ref[t:t + a.shape[0]] = _scan_in_tiles(a, b, reverse=False)

        _by_rows(decay_and_input, [v, zr, zi])
        _scan_carry(sa_ref, sb_ref, sh_ref, ta_ref, tb_ref, reverse=False)

        def gated_output(t, hs_c, g_c):
            g = g_c.astype(F32)
            y_ref[t:t + g.shape[0]] = (hs_c * (g * _sigmoid(g))).astype(BF16)
            hs_ref[t:t + g.shape[0]] = hs_c.astype(BF16)

        _by_rows(gated_output, [sh_ref, pg_ref.at[1]])

    e = 2 * e_half
    return pl.pallas_call(
        body, name=name, grid=(n_heads,),
        in_specs=[sp_["pair"], sp_["conv_w"], sp_["chan"], sp_["w"], sp_["bias"], sp_["w"],
                  sp_["bias"], sp_["chan"]],
        out_specs=[sp_["plane"]] * 3,
        out_shape=[jax.ShapeDtypeStruct((s, e), BF16), jax.ShapeDtypeStruct((s, e), F32),
                   jax.ShapeDtypeStruct((s, e), BF16)],
        scratch_shapes=[pltpu.VMEM((s, HEAD_DIM), F32)] * 3 + [pltpu.VMEM((s // SUBLANES, HEAD_DIM), F32)] * 2,
        compiler_params=_params(("parallel",)),
    )(proj, conv_w, conv_b, w_a, b_a, w_x, b_x, lam)


def _lru_bwd(proj, dy, saved, conv_w, conv_b, w_a, b_a, w_x, b_x, lam, name):
    _, _, s, e_half = proj.shape
    e = 2 * e_half
    c = HEAD_DIM
    n_heads = e // c
    sp_ = _lru_specs(s, e_half, n_heads)

    def body(pg_ref, dy_ref, a_ref, hs_ref, cw_ref, cb_ref, wa_ref, ba_ref, wx_ref, bx_ref, lam_ref,
             dpg_ref, dwa_ref, dwx_ref, dba_ref, dbx_ref, dlam_ref, dcw_ref, dcb_ref,
             sa_ref, sb_ref, sd_ref, dzr_ref, dzi_ref, ta_ref, tb_ref):
        taps, v, vb, wa, wx, zr, zi = _lru_gate_inputs(
            pg_ref[0].astype(F32), cw_ref, cb_ref, wa_ref, ba_ref, wx_ref, bx_ref)
        lam = lam_ref[...]
        rate = (-RGLRU_C) * _softplus_neg(lam)
        a = a_ref[...]

        def state_gradient_in_tiles(t, a_next, dy_c, g_c, hs_c):
            g, dyv = g_c.astype(F32), dy_c.astype(F32)
            sig = _sigmoid(g)
            rows = slice(t, t + g.shape[0])
            dpg_ref[1, rows] = (dyv * hs_c.astype(F32) * (sig * (1.0 + g * (1.0 - sig)))).astype(BF16)
            sa_ref[rows], sb_ref[rows] = _scan_in_tiles(a_next, dyv * (g * sig), reverse=True)

        _by_rows(state_gradient_in_tiles, [_shift_up(a, 1), dy_ref, pg_ref.at[1], hs_ref])
        _scan_carry(sa_ref, sb_ref, sd_ref, ta_ref, tb_ref, reverse=True)

        sums = []

        def gate_gradients(t, dh, hs_before, a_c, zr_c, zi_c, v_c):
            r, i = _sigmoid(zr_c), _sigmoid(zi_c)
            q = (1.0 - a_c) * (1.0 + a_c)
            inv_nm = lax.rsqrt(q)
            div = dh * (q * inv_nm)
            dla = (dh * hs_before) * a_c - (dh * (i * v_c)) * (a_c * a_c * inv_nm)
            dzr = (dla * rate) * (r * (1.0 - r))
            dzi = (div * v_c) * (i * (1.0 - i))
            rows = slice(t, t + dh.shape[0])
            dzr_ref[rows], dzi_ref[rows] = dzr.astype(BF16), dzi.astype(BF16)
            sa_ref[rows] = div * i
            sums.append((_colsum(dla * r), _colsum(dzr), _colsum(dzi)))

        _by_rows(gate_gradients, [sd_ref, _shift_down(hs_ref[...].astype(F32), 1), a_ref, zr, zi, v])
        dlam_ref[...] = sum(p[0] for p in sums) * ((-RGLRU_C) * (-_sigmoid(-lam)))
        dba_ref[...] = sum(p[1] for p in sums)
        dbx_ref[...] = sum(p[2] for p in sums)
        dzr_b, dzi_b = dzr_ref[...], dzi_ref[...]
        tn = (((0,), (0,)), ((), ()))
        nt = (((1,), (1,)), ((), ()))
        dwa_ref[...] = lax.dot_general(vb, dzr_b, tn, preferred_element_type=F32)
        dwx_ref[...] = lax.dot_general(vb, dzi_b, tn, preferred_element_type=F32)
        dv = (sa_ref[...] + lax.dot_general(dzr_b, wa, nt, preferred_element_type=F32)
              + lax.dot_general(dzi_b, wx, nt, preferred_element_type=F32))
        dcb_ref[...] = _colsum(dv)
        dvp = jnp.zeros_like(dv)
        for k in range(4):
            dvp = dvp + cw_ref[k:k + 1, :] * _shift_up(dv, 3 - k)
            dcw_ref[k:k + 1, :] = _colsum(dv * taps[k])
        dpg_ref[0] = dvp.astype(BF16)

    head_mat = pl.BlockSpec((None, c, c), lambda h: (h, 0, 0))
    outs = pl.pallas_call(
        body, name=name, grid=(n_heads,),
        in_specs=[sp_["pair"]] + [sp_["plane"]] * 3 + [sp_["conv_w"], sp_["chan"], sp_["w"], sp_["bias"],
                                                        sp_["w"], sp_["bias"], sp_["chan"]],
        out_specs=[sp_["pair"], head_mat, head_mat, sp_["bias"], sp_["bias"],
                   sp_["chan"], sp_["conv_w"], sp_["chan"]],
        out_shape=[jax.ShapeDtypeStruct((2, 2, s, e_half), BF16),
                   jax.ShapeDtypeStruct((n_heads, c, c), F32), jax.ShapeDtypeStruct((n_heads, c, c), F32),
                   jax.ShapeDtypeStruct((n_heads, 1, c), F32), jax.ShapeDtypeStruct((n_heads, 1, c), F32),
                   jax.ShapeDtypeStruct((1, e), F32), jax.ShapeDtypeStruct((4, e), F32),
                   jax.ShapeDtypeStruct((1, e), F32)],
        scratch_shapes=[pltpu.VMEM((s, c), F32)] * 3 + [pltpu.VMEM((s, c), BF16)] * 2
        + [pltpu.VMEM((s // SUBLANES, c), F32)] * 2,
        compiler_params=_params(("parallel",)),
    )(proj, dy, *saved, conv_w, conv_b, w_a, b_a, w_x, b_x, lam)
    return tuple(outs)


def _ada_fwd(c_all, ada_w, name):
    n_l, d, f = ada_w.shape
    bf = _blk(f, 512)

    def body(c_ref, w_ref, o_ref):
        cv = c_ref[...]
        sc = (cv * _sigmoid(cv)).astype(BF16)
        o_ref[...] = jnp.dot(sc, w_ref[...].astype(BF16), preferred_element_type=F32)

    return pl.pallas_call(
        body, name=name, grid=(n_l, f // bf),
        in_specs=[pl.BlockSpec((8, d), lambda l, j: (0, 0)), pl.BlockSpec((None, d, bf), lambda l, j: (l, 0, j))],
        out_specs=pl.BlockSpec((None, 8, bf), lambda l, j: (l, 0, j)),
        out_shape=jax.ShapeDtypeStruct((n_l, 8, f), F32),
        compiler_params=_params(("parallel", "parallel")),
    )(c_all, ada_w)


def _ada_bwd_adamw(c_t, dmod, w, m, v, name, after=None):
    n_l, d, f = w.shape
    bf = _blk(f, 256)

    def body(c_ref, dm_ref, w_ref, m_ref, v_ref, g_ref, d_ref, m2_ref, v2_ref):
        cv = c_ref[...]
        sc = cv * _sigmoid(cv)
        dm = dm_ref[...]
        g = sc[:, 0:1] * dm[0:1, :]
        for b in range(1, 8):
            g = g + sc[:, b:b + 1] * dm[b:b + 1, :]
        g_ref[...] = g
        dl, m2, v2 = _adamw_math(w_ref[...], g, m_ref[...], v_ref[...])
        d_ref[...] = dl
        m2_ref[...] = m2
        v2_ref[...] = v2

    big = pl.BlockSpec((None, d, bf), lambda l, j: (l, 0, j))
    body, extra_specs, extra = _ordered(body, 5, after)
    return pl.pallas_call(
        body, name=name, grid=(n_l, f // bf),
        in_specs=[pl.BlockSpec((d, 8), lambda l, j: (0, 0)), pl.BlockSpec((None, 8, bf), lambda l, j: (l, 0, j)),
                  big, big, big] + extra_specs,
        out_specs=[big] * 4, out_shape=[jax.ShapeDtypeStruct((n_l, d, f), F32)] * 4,
        compiler_params=_params(("parallel", "parallel")),
    )(c_t, dmod, w, m, v, *extra)


def _pack(parts):
    padded, offs, n = [], [], 0
    for p in parts:
        p = p.reshape(-1)
        size = -(-p.shape[0] // PACK) * PACK
        offs.append(n)
        n += size
        padded.append(jnp.pad(p, (0, size - p.shape[0])) if size != p.shape[0] else p)
    return jnp.concatenate(padded), offs, n


def kernel(x, c, norm_g, ada_w, ada_b, sc_w_in, sc_conv_w, sc_w_out, lru_w_in, lru_conv_w, lru_conv_b, lru_w_a, lru_b_a, lru_w_x, lru_b_x, lru_lambda, lru_w_out, final_g, loss_target, m_norm_g, m_ada_w, m_ada_b, m_sc_w_in, m_sc_conv_w, m_sc_w_out, m_lru_w_in, m_lru_conv_w, m_lru_conv_b, m_lru_w_a, m_lru_b_a, m_lru_w_x, m_lru_b_x, m_lru_lambda, m_lru_w_out, m_final_g, v_norm_g, v_ada_w, v_ada_b, v_sc_w_in, v_sc_conv_w, v_sc_w_out, v_lru_w_in, v_lru_conv_w, v_lru_conv_b, v_lru_w_a, v_lru_b_a, v_lru_w_x, v_lru_b_x, v_lru_lambda, v_lru_w_out, v_final_g):
    xi, yi, ci = _pos()
    chip = 2 * xi + yi
    batch = 4 * xi + 2 * yi + ci
    core_op = jnp.reshape(ci, (1,)).astype(jnp.int32)
    chip_op = jnp.reshape(chip, (1,)).astype(jnp.int32)
    where_op = jnp.stack([chip, ci]).astype(jnp.int32)

    x2d, tgt = x[0], loss_target[0]
    s, d = x2d.shape
    es = sc_conv_w.shape[2]
    e = 4 * es
    n_heads = lru_w_a.shape[1]
    hj = lru_b_a.shape[2]
    f = ada_w.shape[2]
    row = lambda t: t.reshape(1, -1)

    small_parts = [c, sc_conv_w, lru_conv_w, lru_conv_b, lru_b_a, lru_b_x, lru_lambda]
    small, offs, n_small = _pack(small_parts)
    got = _allgather8([small.reshape(8, n_small // 8)], "ag_small")[0].reshape(8, n_small)
    c_all = got[:, :d]
    per_chip = got[0::2]

    def chip_part(k, shape):
        size = 1
        for dim in shape:
            size *= dim
        return per_chip[:, offs[k]:offs[k] + size].reshape((4,) + shape)

    conv_w0 = jnp.transpose(chip_part(1, (3, es)), (1, 0, 2)).reshape(3, e)
    conv_w1 = jnp.transpose(chip_part(2, (4, es)), (1, 0, 2)).reshape(4, e)
    conv_b1 = chip_part(3, (es,)).reshape(1, e)
    b_a = jnp.transpose(chip_part(4, (n_heads, hj)), (1, 0, 2)).reshape(n_heads, 1, 4 * hj)
    b_x = jnp.transpose(chip_part(5, (n_heads, hj)), (1, 0, 2)).reshape(n_heads, 1, 4 * hj)
    lam = chip_part(6, (es,)).reshape(1, e)

    mod_nb = _ada_fwd(c_all, ada_w, "ada_fwd")
    mods = _allgather8([mod_nb.reshape(16, f)], "ag_mod")[0].reshape(8, 2, 8, f)[0::2]
    mine = lax.dynamic_index_in_dim(mods, batch, axis=2, keepdims=False)
    mod = jnp.transpose(mine, (1, 0, 2)).reshape(2, 4 * f) + ada_b
    shift = [row(mod[l, :d]) for l in range(2)]
    scale = [row(mod[l, d:2 * d]) for l in range(2)]
    gate = [row(mod[l, 2 * d:]) for l in range(2)]
    ng = [row(norm_g[l]) for l in range(2)]

    shards = [sc_w_in[0], sc_w_out[0], lru_w_in[0], lru_w_a[0].reshape(n_heads * hj, HEAD_DIM),
              lru_w_x[0].reshape(n_heads * hj, HEAD_DIM), lru_w_out[0]]
    names = ["sc_w_in", "sc_w_out", "lru_w_in", "lru_w_a", "lru_w_x", "lru_w_out"]
    slots = [None] + [_cast_into_slot(where_op, w, "cast_" + nm) for w, nm in zip(shards[1:], names[1:])]
    sent_half = _cast_into_slot(where_op, shards[0], "cast_sc_w_in_sent", half=True)
    send_a, recv_a, buf, started = _first_start(sent_half, mod, "ag_first_start")
    other_half = jnp.stack([chip, 1 - ci]).astype(jnp.int32)
    buf = _cast_into_slot(other_half, shards[0], "cast_sc_w_in_kept", half=True, so_far=buf)
    h0 = _norm_mod_fwd(x2d, ng[0], scale[0], shift[0], "norm0", after=started)
    planes = [jnp.reshape(2 * px + py, (1,)).astype(jnp.int32) for px, py in _other_chips(xi, yi)]
    proj0 = _mm_plane(h0, buf.reshape(4, d, e), chip_op, "sc_in_own")
    small_names = ["norm_g", "ada_b", "final_g", "sc_conv_w", "lru_conv_w", "lru_conv_b", "lru_b_a",
                   "lru_b_x", "lru_lambda"]
    small_w = [norm_g, ada_b, final_g, sc_conv_w, lru_conv_w, lru_conv_b, lru_b_a, lru_b_x, lru_lambda]
    small_m = [m_norm_g, m_ada_b, m_final_g, m_sc_conv_w, m_lru_conv_w, m_lru_conv_b, m_lru_b_a,
               m_lru_b_x, m_lru_lambda]
    small_v = [v_norm_g, v_ada_b, v_final_g, v_sc_conv_w, v_lru_conv_w, v_lru_conv_b, v_lru_b_a,
               v_lru_b_x, v_lru_lambda]
    pw, soffs, n_s = _pack(small_w)
    small_state = [t.reshape(n_s // PACK, PACK) for t in (pw, _pack(small_m)[0], _pack(small_v)[0])]
    relays, passed, relayed = [], [], [proj0] + slots[1:] + small_state
    for j in range(2):
        send_b, recv_b, send_c, recv_c, buf, token = _first_relay(send_a, recv_a, buf, j, relayed,
                                                                  "ag_first_relay_%d" % j)
        relays.append((send_c, recv_c))
        passed.append((send_b, recv_b))
        relayed = [token]
    rest_flight, relayed = _gather_start([[slots[1]], [slots[2]], slots[3:5], [slots[5]]], token, "ag_start")
    for j in range(2):
        buf = _first_relay_done(*passed[j], buf, j, relayed, "ag_first_relay_done_%d" % j)
        proj0 = _mm_plane(h0, buf.reshape(4, d, e), planes[j], "sc_in_%d" % j, planes_so_far=proj0)
        relayed = proj0
    send_b, recv_b, buf, relayed = _first_diagonal(relays, buf, proj0, "ag_first_diagonal")
    buf = _first_diagonal_done(send_b, recv_b, relays, buf, relayed, "ag_first_diagonal_done")
    proj0 = _mm_plane(h0, buf.reshape(4, d, e), planes[2], "sc_in_2", planes_so_far=proj0)
    w_in0 = buf.reshape(4, d, e)
    in_flight = [None] + rest_flight

    def arrived(g, after, tag):
        send1, recv1, bufs = in_flight[g]
        send2, recv2, bufs, passed = _gather_forward(send1, recv1, bufs, after, "ag_forward_" + tag)
        return _gather_finish(send2, recv2, bufs, passed, "ag_finish_" + tag)

    y0 = _sc_fwd(proj0, conv_w0, "sc_mix")
    w_out0 = arrived(1, y0, "sc_w_out")[0].reshape(1, e, d)
    o0 = _mm(y0[None], w_out0, "nn", F32, "sc_out")[0]
    x1, h1 = _norm_mod_fwd(x2d, ng[1], scale[1], shift[1], "norm1", o=o0, gate=gate[0])
    send1, recv1, bufs = in_flight[2]
    proj1 = _mm_plane(h1, bufs[0].reshape(4, d, e // 2), chip_op, "lru_in_own")
    for j in range(3):
        send2, recv2, bufs, passed_on = _gather_forward(send1, recv1, bufs, proj1, "ag_forward_lru_w_in_%d" % j,
                                                        sources=(j,))
        bufs = _gather_finish(send2, recv2, bufs, passed_on, "ag_finish_lru_w_in_%d" % j, sources=(j,))
        proj1 = _mm_plane(h1, bufs[0].reshape(4, d, e // 2), planes[j], "lru_in_%d" % j, planes_so_far=proj1)
    w_in1 = bufs[0].reshape(4, d, e // 2)
    gate_ws = arrived(3, proj1, "lru_gates")
    w_a = gate_ws[0].reshape(4, n_heads, hj, HEAD_DIM)
    w_x = gate_ws[1].reshape(4, n_heads, hj, HEAD_DIM)
    pairs1 = proj1.reshape(2, 2, s, e // 2)
    y1, *lru_saved = _lru_fwd(pairs1, conv_w1, conv_b1, w_a, b_a, w_x, b_x, lam, "lru_mix")
    w_out1 = arrived(4, y1, "lru_w_out")[0].reshape(1, e, d)
    o1 = _mm(y1[None], w_out1, "nn", F32, "lru_out")[0]
    dx2, do1, acc_f = _final_loss(x1, o1, gate[1], row(final_g), tgt, "final_loss")

    def reduce_stage1(tag, grads):
        lands = [lax.empty((4, g.shape[1] // 2, g.shape[2]), F32) for g in grads]
        return _exchange_start("rs_sibling_start_" + tag, grads, lands, len(grads),
                               _plan_other_half_to_sibling, None)

    def reduce_stage2(tag, stage1, nms, after):
        send, recv, grads, lands, _ = stage1
        grads, lands = _exchange_wait("rs_sibling_wait_" + tag, send, recv, grads, lands,
                                      _plan_other_half_to_sibling, after)
        parts = [_sum_own_and_sibling(core_op, g, r1, "rs_sum1_" + nm) for g, r1, nm in zip(grads, lands, nms)]
        lands = [lax.empty((3,) + p.shape[1:], BF16) for p in parts]
        return _exchange_start("rs_chips_start_" + tag, parts, lands, 3 * len(parts),
                               _plan_partials_to_chips, None)

    def reduce_stage3(tag, stage2, nms, after):
        send, recv, parts, lands, _ = stage2
        parts, lands = _exchange_wait("rs_chips_wait_" + tag, send, recv, parts, lands,
                                      _plan_partials_to_chips, after)
        halves = [_sum_chips(where_op, p, r2, "rs_sum2_" + nm) for p, r2, nm in zip(parts, lands, nms)]
        return _exchange_start("rs_share_start_" + tag, [], halves, len(halves), _plan_share_half, None)

    def reduce_done(tag, stage3, after):
        send, recv, _, fulls, _ = stage3
        return _exchange_wait("rs_share_wait_" + tag, send, recv, [], fulls, _plan_share_half, after)[1]

    def head_major_to_chip_major(t):
        return jnp.transpose(t.reshape(n_heads, 4, hj, HEAD_DIM), (1, 0, 2, 3)).reshape(4, n_heads * hj, HEAD_DIM)

    big_state = dict(zip(names, zip(shards, [m_sc_w_in, m_sc_w_out, m_lru_w_in, m_lru_w_a, m_lru_w_x, m_lru_w_out],
                                    [v_sc_w_in, v_sc_w_out, v_lru_w_in, v_lru_w_a, v_lru_w_x, v_lru_w_out],
                                    [sc_w_in, sc_w_out, lru_w_in, lru_w_a, lru_w_x, lru_w_out])))
    big = {}

    def update(nms, fulls, after):
        for nm, g2 in zip(nms, fulls):
            w2, m4, v4, w4 = big_state[nm]
            outs = _adamw(w2, g2, m4.reshape(w2.shape), v4.reshape(w2.shape), "adamw_" + nm, after=after)
            big[nm] = tuple(t.reshape(w4.shape) for t in outs)
            after = outs[1]
        return after

    g_w_out1 = _mm(y1[None], do1[None], "tn", F32, "lru_out_dw", bm=512, bn=2048)
    dy1 = _mm(do1[None], w_out1, "nt", BF16, "lru_out_dx")[0]
    dpairs1, g_wa, g_wx, g_ba, g_bx, g_lam, g_cw1, g_cb1 = _lru_bwd(
        pairs1, dy1, lru_saved, conv_w1, conv_b1, w_a, b_a, w_x, b_x, lam, "lru_mix_bwd")
    dproj1 = dpairs1.reshape(4, s, e // 2)
    g_w_in1 = _mm(h1[None], dproj1, "tn", F32, "lru_in_dw", bm=1024, bn=2048)
    lru_names = ["lru_w_out", "lru_w_a", "lru_w_x", "lru_w_in"]
    lru_rs = reduce_stage1("lru", [g_w_out1.reshape(4, es, d), head_major_to_chip_major(g_wa),
                                   head_major_to_chip_major(g_wx), g_w_in1])
    dh1 = _mm(dproj1, w_in1, "nt", F32, "lru_in_dx", after=lru_rs[4])[0]
    lru_rs = reduce_stage2("lru", lru_rs, lru_names, dh1)
    dx1, do0, acc1 = _norm_mod_bwd(dh1, x1, ng[1], scale[1], dx2, "norm1_bwd", o_prev=o0, gate_prev=gate[0],
                                   after=lru_rs[4])

    g_w_out0 = _mm(y0[None], do0[None], "tn", F32, "sc_out_dw", bm=512, bn=2048)
    out_rs = reduce_stage1("sc_out", [g_w_out0.reshape(4, es, d)])
    dy0 = _mm(do0[None], w_out0, "nt", BF16, "sc_out_dx", after=out_rs[4])[0]
    out_rs = reduce_stage2("sc_out", out_rs, ["sc_w_out"], dy0)
    dproj0, g_cw0 = _sc_bwd(proj0, dy0, conv_w0, "sc_mix_bwd", after=out_rs[4])
    stage1, stage2, after = [], [], None
    for k in range(IN_PIECES):
        g_piece = _mm(h0[None], dproj0, "tn", F32, "sc_in_dw_%d" % k, bm=1024, bn=e // IN_PIECES,
                      col_blocks=(k, 1), after=after)
        stage1.append(reduce_stage1("sc_in_%d" % k, [g_piece]))
        after = stage1[k][4]
        if k:
            stage2.append(reduce_stage2("sc_in_%d" % (k - 1), stage1[k - 1], ["sc_w_in_%d" % (k - 1)], after))
            after = stage2[k - 1][4]
    lru_rs = reduce_stage3("lru", lru_rs, lru_names, after)
    k = IN_PIECES - 1
    stage2.append(reduce_stage2("sc_in_%d" % k, stage1[k], ["sc_w_in_%d" % k], lru_rs[4]))
    dh0 = _mm(dproj0, w_in0, "nt", F32, "sc_in_dx", bn=1024, after=stage2[k][4])[0]
    grad_x, acc0 = _norm_mod_bwd(dh0, x2d, ng[0], scale[0], dx1, "norm0_bwd")
    out_rs = reduce_stage3("sc_out", out_rs, ["sc_w_out"], acc0)
    last = update(lru_names, reduce_done("lru", lru_rs, out_rs[4]), None)
    last = update(["sc_w_out"], reduce_done("sc_out", out_rs, last), None)
    g_half = None
    for k, (send, recv, parts, lands, _) in enumerate(stage2):
        parts, lands = _exchange_wait("rs_chips_wait_sc_in_%d" % k, send, recv, parts, lands,
                                      _plan_partials_to_chips, last)
        g_half = _sum_chips(where_op, parts[0], lands[0], "rs_sum2_sc_w_in_%d" % k, piece=(k, IN_PIECES),
                            so_far=g_half)
    in_rs = _exchange_start("rs_share_start_sc_in", [], [g_half], 1, _plan_share_half, None)

    dmod = jnp.stack([jnp.concatenate([acc0[1], acc0[0], acc1[3]]),
                      jnp.concatenate([acc1[1], acc1[0], acc_f[1]])])
    part_list = [jnp.stack([acc0[2], acc1[2]]), acc_f[0], acc_f[2, :1], g_cw0, g_cw1, g_cb1, g_ba, g_bx,
                 g_lam, dmod]
    partials, poffs, n_part = _pack(part_list)
    every = _allgather8([partials.reshape(8, n_part // 8)], "ag_partials", after=in_rs[4])[0].reshape(8, n_part)
    total = _sum_rows8(every, "sum_partials")[0]

    def tot(k, shape):
        size = 1
        for dim in shape:
            size *= dim
        return total[poffs[k]:poffs[k] + size].reshape(shape)

    def my_cols(t, width):
        return lax.dynamic_slice_in_dim(t, chip * width, width, axis=t.ndim - 1)

    loss = tot(2, (1,))[0]
    g_norm_g, g_final_g, g_ada_b = tot(0, (2, d)), tot(1, (d,)), tot(9, (2, 3 * d))
    g_sc_conv_w = my_cols(tot(3, (3, e)), es)[None]
    g_lru_conv_w = my_cols(tot(4, (4, e)), es)[None]
    g_lru_conv_b = my_cols(tot(5, (1, e)), es)
    g_lru_b_a = my_cols(tot(6, (n_heads, 4 * hj)), hj)[None]
    g_lru_b_x = my_cols(tot(7, (n_heads, 4 * hj)), hj)[None]
    g_lru_lambda = my_cols(tot(8, (1, e)), es)

    dmod_all = every[:, poffs[9]:poffs[9] + 6 * d].reshape(8, 2, 3 * d)
    dmod_mine = jnp.transpose(my_cols(dmod_all, f), (1, 0, 2))
    ada = _ada_bwd_adamw(jnp.transpose(c_all), dmod_mine, ada_w, m_ada_w, v_ada_w, "ada_bwd_adamw", after=total)

    small_g = [g_norm_g, g_ada_b, g_final_g, g_sc_conv_w, g_lru_conv_w, g_lru_conv_b, g_lru_b_a,
               g_lru_b_x, g_lru_lambda]
    pg = _pack(small_g)[0].reshape(small_state[0].shape)
    _, pd, pm2, pv2 = _adamw(small_state[0], pg, small_state[1], small_state[2], "adamw_small", after=ada[0])
    update(["sc_w_in"], reduce_done("sc_in", in_rs, pd), None)
    small = {}
    for k, (nm, w_) in enumerate(zip(small_names, small_w)):
        take = lambda t: t.reshape(-1)[soffs[k]:soffs[k] + w_.size].reshape(w_.shape)
        small[nm] = (small_g[k].reshape(w_.shape), take(pd), take(pm2), take(pv2))

    results = dict(small)
    results.update(big)
    results["ada_w"] = tuple(ada)
    order = ["norm_g", "ada_w", "ada_b", "sc_w_in", "sc_conv_w", "sc_w_out", "lru_w_in", "lru_conv_w",
             "lru_conv_b", "lru_w_a", "lru_b_a", "lru_w_x", "lru_b_x", "lru_lambda", "lru_w_out", "final_g"]
    out = [loss, grad_x[None]]
    for kind in range(4):
        out += [results[nm][kind] for nm in order]
    return tuple(out)
```
